```python
import jax, jax.numpy as jnp
from jax import lax

D_MODEL = 1024
BATCH = 8
SEQ = 4096
DEPTH = 2

CHUNK = 64
N_BRANCH = 3
BRANCH_WIDTH = D_MODEL
ATTN_HEADS = 8
ATTN_HEAD_DIM = BRANCH_WIDTH // ATTN_HEADS
ATTN_LEFT_CHUNKS = 8
REL_CLIP = 2 * CHUNK
RET_HEADS = 4
RET_KEY_DIM = BRANCH_WIDTH // (2 * RET_HEADS)
RET_VAL_DIM = BRANCH_WIDTH // RET_HEADS
ROPE_BASE = 10000.0
GLA_HEADS = 4
GLA_KEY_DIM = BRANCH_WIDTH // (2 * GLA_HEADS)
GLA_VAL_DIM = BRANCH_WIDTH // GLA_HEADS
GLA_GATE_RANK = 16
GLA_GATE_NORMALIZER = 16.0
D_FF = 4 * D_MODEL
LN_EPS = 1e-5
NEG_INF = -1e30
DEEPNORM_ALPHA = (2 * DEPTH) ** 0.25
DEEPNORM_BETA = (8 * DEPTH) ** -0.25

SPLIT_SIZES = (
    ATTN_HEADS * ATTN_HEAD_DIM, ATTN_HEADS * ATTN_HEAD_DIM, ATTN_HEADS * ATTN_HEAD_DIM,
    RET_HEADS * RET_KEY_DIM, RET_HEADS * RET_KEY_DIM, RET_HEADS * RET_VAL_DIM, RET_HEADS * RET_VAL_DIM,
    GLA_HEADS * GLA_KEY_DIM, GLA_HEADS * GLA_KEY_DIM, GLA_HEADS * GLA_VAL_DIM, GLA_HEADS * GLA_VAL_DIM,
    GLA_GATE_RANK, N_BRANCH * D_MODEL,
)
PROJ_WIDTH = sum(SPLIT_SIZES)

kernel_name = 'hybrid_chunk_attn_retention_gla_deepnorm'


def split_columns(t, sizes):
    parts, start = [], 0
    for sz in sizes:
        parts.append(t[..., start:start + sz])
        start += sz
    return parts


def layer_norm(x, g, b):
    xf = x.astype(jnp.float32)
    mu = jnp.mean(xf, axis=-1, keepdims=True)
    var = jnp.mean(jnp.square(xf - mu), axis=-1, keepdims=True)
    return ((xf - mu) * lax.rsqrt(var + LN_EPS) * g + b).astype(x.dtype)


def head_layer_norm(t):
    mu = jnp.mean(t, axis=-1, keepdims=True)
    var = jnp.mean(jnp.square(t - mu), axis=-1, keepdims=True)
    return (t - mu) * lax.rsqrt(var + LN_EPS)


def head_rms_norm(t, g):
    return t * lax.rsqrt(jnp.mean(jnp.square(t), axis=-1, keepdims=True) + LN_EPS) * g.astype(jnp.float32)


def rope(t):
    s, d = t.shape[1], t.shape[-1]
    inv = ROPE_BASE ** (-jnp.arange(0, d, 2, dtype=jnp.float32) / d)
    ang = jnp.arange(s, dtype=jnp.float32)[:, None] * inv[None, :]
    cos, sin = jnp.cos(ang)[None, :, None, :], jnp.sin(ang)[None, :, None, :]
    t1, t2 = jnp.split(t.astype(jnp.float32), 2, axis=-1)
    return jnp.concatenate([t1 * cos - t2 * sin, t1 * sin + t2 * cos], axis=-1)


def to_chunks(t):
    b, s, n, d = t.shape
    return t.astype(jnp.float32).reshape(b, s // CHUNK, CHUNK, n, d).transpose(0, 3, 1, 2, 4)


def from_chunks(t):
    b, n, nc, c, d = t.shape
    return t.transpose(0, 2, 3, 1, 4).reshape(b, nc * c, n * d)


def chunk_band_attention(q, k, v, rel_bias):
    b, s, h, d = q.shape
    nc = s // CHUNK
    band = ATTN_LEFT_CHUNKS + 1
    pad = ((0, 0), (ATTN_LEFT_CHUNKS * CHUNK, 0), (0, 0), (0, 0))
    qc = (q * d ** -0.5).reshape(b, nc, CHUNK, h, d)
    kc = jnp.pad(k, pad).reshape(b, nc + ATTN_LEFT_CHUNKS, CHUNK, h, d)
    vc = jnp.pad(v, pad).reshape(b, nc + ATTN_LEFT_CHUNKS, CHUNK, h, d)
    scores = jnp.concatenate(
        [jnp.einsum('bnihd,bnjhd->bhnij', qc, kc[:, j:j + nc]) for j in range(band)], axis=-1
    ).astype(jnp.float32)
    i = jnp.arange(CHUNK)[:, None]
    m = jnp.arange(band * CHUNK)[None, :]
    rel = jnp.clip(i + ATTN_LEFT_CHUNKS * CHUNK - m, -REL_CLIP, REL_CLIP) + REL_CLIP
    bias = rel_bias.astype(jnp.float32)[:, rel]
    valid = (jnp.arange(nc)[:, None] + m // CHUNK - ATTN_LEFT_CHUNKS) >= 0
    scores = jnp.where(valid[None, None, :, None, :], scores + bias[None, :, None, :, :], NEG_INF)
    p = jax.nn.softmax(scores, axis=-1).astype(v.dtype)
    out = jnp.einsum('bhnij,bnjhd->bnihd', p[..., :CHUNK], vc[:, 0:nc])
    for j in range(1, band):
        out = out + jnp.einsum('bhnij,bnjhd->bnihd', p[..., j * CHUNK:(j + 1) * CHUNK], vc[:, j:j + nc])
    return out.reshape(b, s, h * d)


def chunk_decay_linear_attn(q, k, v, log_a):
    c = q.shape[-2]
    cum = jnp.cumsum(log_a, axis=-2)
    last = cum[..., -1:, :]
    q_fwd = q * jnp.exp(cum)
    s_fwd = jnp.einsum('bhnid,bhnjd->bhnij', q_fwd, k * jnp.exp(-cum))
    s_bwd = jnp.einsum('bhnid,bhnjd->bhnij', q * jnp.exp(-cum), k * jnp.exp(cum))
    lower = jnp.tril(jnp.ones((c, c), dtype=bool))
    intra = jnp.einsum('bhnij,bhnjv->bhniv', jnp.where(lower, s_fwd, s_bwd), v)
    ds = jnp.einsum('bhnjd,bhnjv->bhndv', k * jnp.exp(last - cum), v)
    chunk_decay = jnp.broadcast_to(jnp.exp(last[..., 0, :]), ds.shape[:-1])

    def step(state, inp):
        dec, inc = inp
        return dec[..., None] * state + inc, state

    init = jnp.zeros(ds.shape[:2] + ds.shape[3:], ds.dtype)
    _, s_prev = lax.scan(step, init, (jnp.moveaxis(chunk_decay, 2, 0), jnp.moveaxis(ds, 2, 0)))
    inter = jnp.einsum('bhnid,nbhdv->bhniv', q_fwd, s_prev)
    return intra + inter


def hybrid_token_mixer(h, w_in, rel_bias, gla_w_lr, gla_b_lr, gla_norm_g, w_branch, w_out):
    b, s, _ = h.shape
    (aq, ak, av, rq, rk, rv, rg, gq, gk, gv, gg, glr, gate_logits) = split_columns(h @ w_in, SPLIT_SIZES)

    def heads(t, n):
        return t.reshape(b, s, n, -1)

    attn = chunk_band_attention(heads(aq, ATTN_HEADS), heads(ak, ATTN_HEADS), heads(av, ATTN_HEADS), rel_bias)

    log_gamma = jnp.log1p(-jnp.exp2(-5.0 - jnp.arange(RET_HEADS, dtype=jnp.float32)))
    ret_log_a = jnp.broadcast_to(log_gamma[None, :, None, None, None], (1, RET_HEADS, 1, CHUNK, 1))
    ret = chunk_decay_linear_attn(
        to_chunks(rope(heads(rq, RET_HEADS))),
        to_chunks(rope(heads(rk, RET_HEADS)) * RET_KEY_DIM ** -0.5),
        to_chunks(heads(rv, RET_HEADS)),
        ret_log_a,
    )
    ret = (jax.nn.silu(rg.astype(jnp.float32)) * from_chunks(head_layer_norm(ret))).astype(h.dtype)

    gate_pre = (glr @ gla_w_lr + gla_b_lr).astype(jnp.float32)
    gla_log_a = to_chunks(heads(jax.nn.log_sigmoid(gate_pre) / GLA_GATE_NORMALIZER, GLA_HEADS))
    gla = chunk_decay_linear_attn(
        to_chunks(heads(gq, GLA_HEADS) * GLA_KEY_DIM ** -0.5),
        to_chunks(heads(gk, GLA_HEADS)),
        to_chunks(heads(gv, GLA_HEADS)),
        gla_log_a,
    )
    gla = (jax.nn.silu(gg.astype(jnp.float32)) * from_chunks(head_rms_norm(gla, gla_norm_g))).astype(h.dtype)

    branches = jnp.stack([attn, ret, gla], axis=2)
    proj = jnp.einsum('bsnc,ncd->bsnd', branches, w_branch)
    gates = jax.nn.sigmoid(gate_logits.reshape(b, s, N_BRANCH, D_MODEL))
    merged = jnp.sum(gates * proj, axis=2)
    return merged @ w_out


def squared_relu_mlp(h, w_up, w_down):
    return jnp.square(jax.nn.relu(h @ w_up)) @ w_down


def _fwd_setup_inputs(seed: int = 0) -> dict:
    key = jax.random.key(seed)
    ks = jax.random.split(key, 16)

    def nrm(k, shape, scale):
        return jax.random.normal(k, shape, jnp.float32) * scale

    return {
        'x': nrm(ks[0], (BATCH, SEQ, D_MODEL), 1.0),
        'ln_in_g': 1.0 + nrm(ks[1], (D_MODEL,), 0.02),
        'ln_in_b': nrm(ks[2], (D_MODEL,), 0.02),
        'w_in': nrm(ks[3], (DEPTH, D_MODEL, PROJ_WIDTH), D_MODEL ** -0.5),
        'rel_bias': nrm(ks[4], (DEPTH, ATTN_HEADS, 2 * REL_CLIP + 1), 0.5),
        'gla_w_lr': nrm(ks[5], (DEPTH, GLA_GATE_RANK, GLA_HEADS * GLA_KEY_DIM), GLA_GATE_RANK ** -0.5),
        'gla_b_lr': nrm(ks[6], (DEPTH, GLA_HEADS * GLA_KEY_DIM), 0.5),
        'gla_norm_g': 1.0 + nrm(ks[7], (DEPTH, GLA_VAL_DIM), 0.02),
        'w_branch': nrm(ks[8], (DEPTH, N_BRANCH, BRANCH_WIDTH, D_MODEL), BRANCH_WIDTH ** -0.5 * DEEPNORM_BETA),
        'w_out': nrm(ks[9], (DEPTH, D_MODEL, D_MODEL), D_MODEL ** -0.5 * DEEPNORM_BETA),
        'ln1_g': 1.0 + nrm(ks[10], (DEPTH, D_MODEL), 0.02),
        'ln1_b': nrm(ks[11], (DEPTH, D_MODEL), 0.02),
        'w_up': nrm(ks[12], (DEPTH, D_MODEL, D_FF), D_MODEL ** -0.5),
        'w_down': nrm(ks[13], (DEPTH, D_FF, D_MODEL), D_FF ** -0.5 * DEEPNORM_BETA),
        'ln2_g': 1.0 + nrm(ks[14], (DEPTH, D_MODEL), 0.02),
        'ln2_b': nrm(ks[15], (DEPTH, D_MODEL), 0.02),
    }


def _fwd_reference(x, ln_in_g, ln_in_b, w_in, rel_bias, gla_w_lr, gla_b_lr, gla_norm_g, w_branch, w_out,
              ln1_g, ln1_b, w_up, w_down, ln2_g, ln2_b):
    x = layer_norm(x, ln_in_g, ln_in_b)
    for l in range(DEPTH):
        mix = hybrid_token_mixer(x, w_in[l], rel_bias[l], gla_w_lr[l], gla_b_lr[l], gla_norm_g[l],
                                 w_branch[l], w_out[l])
        x = layer_norm(DEEPNORM_ALPHA * x + mix, ln1_g[l], ln1_b[l])
        x = layer_norm(DEEPNORM_ALPHA * x + squared_relu_mlp(x, w_up[l], w_down[l]), ln2_g[l], ln2_b[l])
    return x


import jax as _jax
import jax.numpy as _jnp

TWIN_FORMAT = 'train_step'
FWD_PARAMS = ['x', 'ln_in_g', 'ln_in_b', 'w_in', 'rel_bias', 'gla_w_lr', 'gla_b_lr', 'gla_norm_g', 'w_branch', 'w_out', 'ln1_g', 'ln1_b', 'w_up', 'w_down', 'ln2_g', 'ln2_b']
TWIN_WEIGHTS = ['ln_in_g', 'ln_in_b', 'w_in', 'rel_bias', 'gla_w_lr', 'gla_b_lr', 'gla_norm_g', 'w_branch', 'w_out', 'ln1_g', 'ln1_b', 'w_up', 'w_down', 'ln2_g', 'ln2_b']
TWIN_DIFF_INPUT = 'x'
TWIN_INPUTS = ['x', 'ln_in_g', 'ln_in_b', 'w_in', 'rel_bias', 'gla_w_lr', 'gla_b_lr', 'gla_norm_g', 'w_branch', 'w_out', 'ln1_g', 'ln1_b', 'w_up', 'w_down', 'ln2_g', 'ln2_b', 'loss_target', 'm_ln_in_g', 'm_ln_in_b', 'm_w_in', 'm_rel_bias', 'm_gla_w_lr', 'm_gla_b_lr', 'm_gla_norm_g', 'm_w_branch', 'm_w_out', 'm_ln1_g', 'm_ln1_b', 'm_w_up', 'm_w_down', 'm_ln2_g', 'm_ln2_b', 'v_ln_in_g', 'v_ln_in_b', 'v_w_in', 'v_rel_bias', 'v_gla_w_lr', 'v_gla_b_lr', 'v_gla_norm_g', 'v_w_branch', 'v_w_out', 'v_ln1_g', 'v_ln1_b', 'v_w_up', 'v_w_down', 'v_ln2_g', 'v_ln2_b']
TWIN_OUTPUTS = ['loss', 'grad_x', 'grad_ln_in_g', 'grad_ln_in_b', 'grad_w_in', 'grad_rel_bias', 'grad_gla_w_lr', 'grad_gla_b_lr', 'grad_gla_norm_g', 'grad_w_branch', 'grad_w_out', 'grad_ln1_g', 'grad_ln1_b', 'grad_w_up', 'grad_w_down', 'grad_ln2_g', 'grad_ln2_b', 'delta_ln_in_g', 'delta_ln_in_b', 'delta_w_in', 'delta_rel_bias', 'delta_gla_w_lr', 'delta_gla_b_lr', 'delta_gla_norm_g', 'delta_w_branch', 'delta_w_out', 'delta_ln1_g', 'delta_ln1_b', 'delta_w_up', 'delta_w_down', 'delta_ln2_g', 'delta_ln2_b', 'new_m_ln_in_g', 'new_m_ln_in_b', 'new_m_w_in', 'new_m_rel_bias', 'new_m_gla_w_lr', 'new_m_gla_b_lr', 'new_m_gla_norm_g', 'new_m_w_branch', 'new_m_w_out', 'new_m_ln1_g', 'new_m_ln1_b', 'new_m_w_up', 'new_m_w_down', 'new_m_ln2_g', 'new_m_ln2_b', 'new_v_ln_in_g', 'new_v_ln_in_b', 'new_v_w_in', 'new_v_rel_bias', 'new_v_gla_w_lr', 'new_v_gla_b_lr', 'new_v_gla_norm_g', 'new_v_w_branch', 'new_v_w_out', 'new_v_ln1_g', 'new_v_ln1_b', 'new_v_w_up', 'new_v_w_down', 'new_v_ln2_g', 'new_v_ln2_b']
TWIN_LEAF_KINDS = {'loss': 'loss', 'grad_x': 'grad_x', 'grad_ln_in_g': 'grad_w', 'grad_ln_in_b': 'grad_w', 'grad_w_in': 'grad_w', 'grad_rel_bias': 'grad_w', 'grad_gla_w_lr': 'grad_w', 'grad_gla_b_lr': 'grad_w', 'grad_gla_norm_g': 'grad_w', 'grad_w_branch': 'grad_w', 'grad_w_out': 'grad_w', 'grad_ln1_g': 'grad_w', 'grad_ln1_b': 'grad_w', 'grad_w_up': 'grad_w', 'grad_w_down': 'grad_w', 'grad_ln2_g': 'grad_w', 'grad_ln2_b': 'grad_w', 'delta_ln_in_g': 'delta_w', 'delta_ln_in_b': 'delta_w', 'delta_w_in': 'delta_w', 'delta_rel_bias': 'delta_w', 'delta_gla_w_lr': 'delta_w', 'delta_gla_b_lr': 'delta_w', 'delta_gla_norm_g': 'delta_w', 'delta_w_branch': 'delta_w', 'delta_w_out': 'delta_w', 'delta_ln1_g': 'delta_w', 'delta_ln1_b': 'delta_w', 'delta_w_up': 'delta_w', 'delta_w_down': 'delta_w', 'delta_ln2_g': 'delta_w', 'delta_ln2_b': 'delta_w', 'new_m_ln_in_g': 'new_m', 'new_m_ln_in_b': 'new_m', 'new_m_w_in': 'new_m', 'new_m_rel_bias': 'new_m', 'new_m_gla_w_lr': 'new_m', 'new_m_gla_b_lr': 'new_m', 'new_m_gla_norm_g': 'new_m', 'new_m_w_branch': 'new_m', 'new_m_w_out': 'new_m', 'new_m_ln1_g': 'new_m', 'new_m_ln1_b': 'new_m', 'new_m_w_up': 'new_m', 'new_m_w_down': 'new_m', 'new_m_ln2_g': 'new_m', 'new_m_ln2_b': 'new_m', 'new_v_ln_in_g': 'new_v', 'new_v_ln_in_b': 'new_v', 'new_v_w_in': 'new_v', 'new_v_rel_bias': 'new_v', 'new_v_gla_w_lr': 'new_v', 'new_v_gla_b_lr': 'new_v', 'new_v_gla_norm_g': 'new_v', 'new_v_w_branch': 'new_v', 'new_v_w_out': 'new_v', 'new_v_ln1_g': 'new_v', 'new_v_ln1_b': 'new_v', 'new_v_w_up': 'new_v', 'new_v_w_down': 'new_v', 'new_v_ln2_g': 'new_v', 'new_v_ln2_b': 'new_v'}


def _forward(args):
    return _fwd_reference(*[args[k] for k in FWD_PARAMS])


def _output_shape():
    out = _jax.eval_shape(lambda: _forward(_fwd_setup_inputs(0)))
    return out.shape, out.dtype

N_MICROBATCH = 1
ADAM_LR = 0.001
ADAM_B1 = 0.9
ADAM_B2 = 0.999
ADAM_EPS = 1e-08
ADAM_WD = 0.01
ADAM_STEP = 10
PER_EXAMPLE_BATCH_AXIS = {'x': 0, 'loss_target': 0}
SHARED_INPUTS = []
_WEIGHT_DTYPES = {'ln_in_g': _jnp.float32, 'ln_in_b': _jnp.float32, 'w_in': _jnp.float32, 'rel_bias': _jnp.float32, 'gla_w_lr': _jnp.float32, 'gla_b_lr': _jnp.float32, 'gla_norm_g': _jnp.float32, 'w_branch': _jnp.float32, 'w_out': _jnp.float32, 'ln1_g': _jnp.float32, 'ln1_b': _jnp.float32, 'w_up': _jnp.float32, 'w_down': _jnp.float32, 'ln2_g': _jnp.float32, 'ln2_b': _jnp.float32}
MOMENT_SCALE = {'ln_in_g': 6.716510e-01, 'ln_in_b': 4.426382e-01, 'w_in': 9.027091e-03, 'rel_bias': 1.867605e-03, 'gla_w_lr': 1.683453e-03, 'gla_b_lr': 6.761459e-03, 'gla_norm_g': 2.151738e-02, 'w_branch': 1.776693e-02, 'w_out': 3.072571e-02, 'ln1_g': 7.725343e-01, 'ln1_b': 4.504881e-01, 'w_up': 4.182265e-02, 'w_down': 1.900571e-01, 'ln2_g': 2.270215e+01, 'ln2_b': 4.988439e+00}


def _to_microbatches(a, axis):
    t = _jnp.moveaxis(a, axis, 0)
    t = t.reshape((N_MICROBATCH, t.shape[0] // N_MICROBATCH) + t.shape[1:])
    return _jnp.moveaxis(t, 1, axis + 1)


def setup_inputs(seed: int = 0) -> dict:
    inp = _fwd_setup_inputs(seed)
    key = _jax.random.fold_in(_jax.random.key(seed), 7919)
    shape, _ = _output_shape()
    out = dict(inp)
    out["loss_target"] = _jax.random.normal(_jax.random.fold_in(key, 0), shape, _jnp.float32)
    for i, name in enumerate(TWIN_WEIGHTS):
        w = inp[name].astype(_jnp.float32)
        if MOMENT_SCALE is None:
            s = _jnp.sqrt(_jnp.mean(_jnp.square(w)) + 1e-30)
        else:
            s = MOMENT_SCALE[name]
        km, kv = _jax.random.split(_jax.random.fold_in(key, i + 1))
        out[name] = w
        out["m_" + name] = s * _jax.random.normal(km, w.shape, _jnp.float32)
        out["v_" + name] = (s * s) * _jax.random.uniform(kv, w.shape, _jnp.float32, 0.5, 1.5)
    if N_MICROBATCH > 1:
        for name, axis in PER_EXAMPLE_BATCH_AXIS.items():
            out[name] = _to_microbatches(out[name], axis)
    return {'x': out['x'], 'ln_in_g': out['ln_in_g'], 'ln_in_b': out['ln_in_b'], 'w_in': out['w_in'], 'rel_bias': out['rel_bias'], 'gla_w_lr': out['gla_w_lr'], 'gla_b_lr': out['gla_b_lr'], 'gla_norm_g': out['gla_norm_g'], 'w_branch': out['w_branch'], 'w_out': out['w_out'], 'ln1_g': out['ln1_g'], 'ln1_b': out['ln1_b'], 'w_up': out['w_up'], 'w_down': out['w_down'], 'ln2_g': out['ln2_g'], 'ln2_b': out['ln2_b'], 'loss_target': out['loss_target'], 'm_ln_in_g': out['m_ln_in_g'], 'm_ln_in_b': out['m_ln_in_b'], 'm_w_in': out['m_w_in'], 'm_rel_bias': out['m_rel_bias'], 'm_gla_w_lr': out['m_gla_w_lr'], 'm_gla_b_lr': out['m_gla_b_lr'], 'm_gla_norm_g': out['m_gla_norm_g'], 'm_w_branch': out['m_w_branch'], 'm_w_out': out['m_w_out'], 'm_ln1_g': out['m_ln1_g'], 'm_ln1_b': out['m_ln1_b'], 'm_w_up': out['m_w_up'], 'm_w_down': out['m_w_down'], 'm_ln2_g': out['m_ln2_g'], 'm_ln2_b': out['m_ln2_b'], 'v_ln_in_g': out['v_ln_in_g'], 'v_ln_in_b': out['v_ln_in_b'], 'v_w_in': out['v_w_in'], 'v_rel_bias': out['v_rel_bias'], 'v_gla_w_lr': out['v_gla_w_lr'], 'v_gla_b_lr': out['v_gla_b_lr'], 'v_gla_norm_g': out['v_gla_norm_g'], 'v_w_branch': out['v_w_branch'], 'v_w_out': out['v_w_out'], 'v_ln1_g': out['v_ln1_g'], 'v_ln1_b': out['v_ln1_b'], 'v_w_up': out['v_w_up'], 'v_w_down': out['v_w_down'], 'v_ln2_g': out['v_ln2_g'], 'v_ln2_b': out['v_ln2_b']}


def _loss(weights, diff, rest, loss_target):
    with _jax.named_scope("forward"):
        args = {**rest, TWIN_DIFF_INPUT: diff, **{k: w.astype(_WEIGHT_DTYPES[k]) for k, w in weights.items()}}
        y = _forward(args)
    with _jax.named_scope("loss_head"):
        err = _jnp.square(y.astype(_jnp.float32) - loss_target)
        return 0.5 * _jnp.sum(_jnp.mean(err, axis=-1)) if err.ndim else 0.5 * err


def _adamw(w, g, m, v):
    m = ADAM_B1 * m + (1.0 - ADAM_B1) * g
    v = ADAM_B2 * v + (1.0 - ADAM_B2) * _jnp.square(g)
    m_hat = m / (1.0 - ADAM_B1 ** ADAM_STEP)
    v_hat = v / (1.0 - ADAM_B2 ** ADAM_STEP)
    delta = -ADAM_LR * (m_hat / (_jnp.sqrt(v_hat) + ADAM_EPS) + ADAM_WD * w)
    return delta, m, v


def reference(x, ln_in_g, ln_in_b, w_in, rel_bias, gla_w_lr, gla_b_lr, gla_norm_g, w_branch, w_out, ln1_g, ln1_b, w_up, w_down, ln2_g, ln2_b, loss_target, m_ln_in_g, m_ln_in_b, m_w_in, m_rel_bias, m_gla_w_lr, m_gla_b_lr, m_gla_norm_g, m_w_branch, m_w_out, m_ln1_g, m_ln1_b, m_w_up, m_w_down, m_ln2_g, m_ln2_b, v_ln_in_g, v_ln_in_b, v_w_in, v_rel_bias, v_gla_w_lr, v_gla_b_lr, v_gla_norm_g, v_w_branch, v_w_out, v_ln1_g, v_ln1_b, v_w_up, v_w_down, v_ln2_g, v_ln2_b):
    given = dict(x=x, ln_in_g=ln_in_g, ln_in_b=ln_in_b, w_in=w_in, rel_bias=rel_bias, gla_w_lr=gla_w_lr, gla_b_lr=gla_b_lr, gla_norm_g=gla_norm_g, w_branch=w_branch, w_out=w_out, ln1_g=ln1_g, ln1_b=ln1_b, w_up=w_up, w_down=w_down, ln2_g=ln2_g, ln2_b=ln2_b, loss_target=loss_target, m_ln_in_g=m_ln_in_g, m_ln_in_b=m_ln_in_b, m_w_in=m_w_in, m_rel_bias=m_rel_bias, m_gla_w_lr=m_gla_w_lr, m_gla_b_lr=m_gla_b_lr, m_gla_norm_g=m_gla_norm_g, m_w_branch=m_w_branch, m_w_out=m_w_out, m_ln1_g=m_ln1_g, m_ln1_b=m_ln1_b, m_w_up=m_w_up, m_w_down=m_w_down, m_ln2_g=m_ln2_g, m_ln2_b=m_ln2_b, v_ln_in_g=v_ln_in_g, v_ln_in_b=v_ln_in_b, v_w_in=v_w_in, v_rel_bias=v_rel_bias, v_gla_w_lr=v_gla_w_lr, v_gla_b_lr=v_gla_b_lr, v_gla_norm_g=v_gla_norm_g, v_w_branch=v_w_branch, v_w_out=v_w_out, v_ln1_g=v_ln1_g, v_ln1_b=v_ln1_b, v_w_up=v_w_up, v_w_down=v_w_down, v_ln2_g=v_ln2_g, v_ln2_b=v_ln2_b)
    weights = {n: given[n] for n in TWIN_WEIGHTS}
    shared = {n: given[n] for n in SHARED_INPUTS}
    per_example = {n: given[n] for n in ['x']}
    grad_fn = _jax.value_and_grad(_loss, argnums=(0, 1))

    def one_microbatch(ex, loss_target):
        ex = dict(ex)
        diff = ex.pop(TWIN_DIFF_INPUT)
        return grad_fn(weights, diff, {**shared, **ex}, loss_target)

    if N_MICROBATCH == 1:
        loss, (grad_w, grad_x) = one_microbatch(per_example, given["loss_target"])
    else:
        def body(carry, xs):
            loss_sum, grad_sum = carry
            l_k, (gw_k, gx_k) = one_microbatch(xs[0], xs[1])
            with _jax.named_scope("update"):
                return (loss_sum + l_k, _jax.tree.map(_jnp.add, grad_sum, gw_k)), gx_k

        init = (_jnp.zeros((), _jnp.float32), _jax.tree.map(_jnp.zeros_like, weights))
        (loss, grad_w), grad_x = _jax.lax.scan(body, init, (per_example, given["loss_target"]))
    with _jax.named_scope("update"):
        delta_w, new_m, new_v = {}, {}, {}
        for n in TWIN_WEIGHTS:
            delta_w[n], new_m[n], new_v[n] = _adamw(weights[n], grad_w[n], given["m_" + n], given["v_" + n])
    return (loss, grad_x, *[grad_w[n] for n in TWIN_WEIGHTS], *[delta_w[n] for n in TWIN_WEIGHTS],
            *[new_m[n] for n in TWIN_WEIGHTS], *[new_v[n] for n in TWIN_WEIGHTS])
```

```python
import functools
import math

import jax
import jax.numpy as jnp
from jax import lax
from jax.experimental import pallas as pl
from jax.experimental.pallas import tpu as pltpu

F32 = jnp.float32
MXU_DTYPE = jnp.bfloat16
WIRE_DTYPE = jnp.bfloat16

N_DEV = 8
D_MODEL = 1024
DEPTH = 2
CHUNK = 64
ATTN_HEADS = 8
ATTN_HEAD_DIM = 128
ATTN_LEFT_CHUNKS = 8
BAND = (ATTN_LEFT_CHUNKS + 1) * CHUNK
PAD_ROWS = ATTN_LEFT_CHUNKS * CHUNK
REL_CLIP = 2 * CHUNK
N_REL = 2 * REL_CLIP + 1
REL_MIN = REL_CLIP - (CHUNK - 1)
LIN_HEADS = 4
LIN_KEY_DIM = 128
LIN_VAL_DIM = 256
GLA_RANK = 16
GLA_GATE_NORMALIZER = 16.0
ROPE_BASE = 10000.0
D_FF = 4 * D_MODEL
LN_EPS = 1e-5
NEG_INF = -1e30
ALPHA = (2 * DEPTH) ** 0.25
ADAM_LR, ADAM_B1, ADAM_B2, ADAM_EPS, ADAM_WD, ADAM_STEP = 0.001, 0.9, 0.999, 1e-08, 0.01, 10

PROJ_WIDTH = 12304
GLR_SRC = 9216
GATES_SRC = GLR_SRC + GLA_RANK
MAIN_W = 9216
GATES_W = 3 * D_MODEL
GATES_OFF = MAIN_W
GLR_OFF = MAIN_W + GATES_W
PROJ_PAD = 12800
GLR_BLOCK = 128
COL = dict(aq=0, ak=1024, av=2048, rq=3072, rk=3584, rv=4096, rg=5120, gq=6144, gk=6656, gv=7168, gg=8192)

SEG_CHUNKS = 8
SEG = SEG_CHUNKS * CHUNK
VMEM_BIG = 56 * 1024 * 1024

_NN = (((1,), (0,)), ((), ()))
_NT = (((1,), (1,)), ((), ()))
_TN = (((0,), (0,)), ((), ()))


def _mxu(a, b, dn):
    return lax.dot_general(a.astype(MXU_DTYPE), b.astype(MXU_DTYPE), dn, preferred_element_type=F32)


@jax.custom_vjp
def dot_nn(a, b):
    return _mxu(a, b, _NN)


def _dot_nn_fwd(a, b):
    return _mxu(a, b, _NN), (a, b)


def _dot_nn_bwd(res, g):
    a, b = res
    return _mxu(g, b, _NT).astype(a.dtype), _mxu(a, g, _TN).astype(b.dtype)


dot_nn.defvjp(_dot_nn_fwd, _dot_nn_bwd)


@jax.custom_vjp
def dot_nt(a, b):
    return _mxu(a, b, _NT)


def _dot_nt_fwd(a, b):
    return _mxu(a, b, _NT), (a, b)


def _dot_nt_bwd(res, g):
    a, b = res
    return _mxu(g, b, _NN).astype(a.dtype), _mxu(g, a, _TN).astype(b.dtype)


dot_nt.defvjp(_dot_nt_fwd, _dot_nt_bwd)


@jax.custom_vjp
def dot_tn(a, b):
    return _mxu(a, b, _TN)


def _dot_tn_fwd(a, b):
    return _mxu(a, b, _TN), (a, b)


def _dot_tn_bwd(res, g):
    a, b = res
    return _mxu(b, g, _NT).astype(a.dtype), _mxu(a, g, _NN).astype(b.dtype)


dot_tn.defvjp(_dot_tn_fwd, _dot_tn_bwd)


def _tri(n, lower):
    r = lax.broadcasted_iota(jnp.int32, (n, n), 0)
    c = lax.broadcasted_iota(jnp.int32, (n, n), 1)
    return (r >= c) if lower else (r <= c)


def _f32_dot(a, b):
    return lax.dot_general(a, b, _NN, precision=lax.Precision.HIGHEST, preferred_element_type=F32)


@jax.custom_vjp
def chunk_cumsum(x):
    return _f32_dot(_tri(x.shape[0], True).astype(F32), x)


def _chunk_cumsum_fwd(x):
    return chunk_cumsum(x), None


def _chunk_cumsum_bwd(_, g):
    return (_f32_dot(_tri(g.shape[0], False).astype(F32), g),)


chunk_cumsum.defvjp(_chunk_cumsum_fwd, _chunk_cumsum_bwd)


@jax.custom_vjp
def swap_halves(t):
    return pltpu.roll(t, t.shape[-1] // 2, t.ndim - 1)


def _swap_fwd(t):
    return swap_halves(t), None


def _swap_bwd(_, g):
    return (swap_halves(g),)


swap_halves.defvjp(_swap_fwd, _swap_bwd)


def mm(name, a, b, dims, *, tm, tn, tk, out_dtype=F32, extras=(), epi=None, n_out=1, out_dtypes=None):
    if dims == "nn":
        (M, K), N = a.shape, b.shape[1]
    elif dims == "nt":
        (M, K), N = a.shape, b.shape[0]
    else:
        (K, M), N = a.shape, b.shape[1]
    tm, tn, tk = min(tm, M), min(tn, N), min(tk, K)
    assert M % tm == 0 and N % tn == 0 and K % tk == 0, (name, M, N, K, tm, tn, tk)
    nk = K // tk
    dn = {"nn": _NN, "nt": _NT, "tn": _TN}[dims]
    a_spec = pl.BlockSpec((tk, tm), lambda i, j, k: (k, i)) if dims == "tn" else pl.BlockSpec((tm, tk), lambda i, j, k: (i, k))
    b_spec = pl.BlockSpec((tn, tk), lambda i, j, k: (j, k)) if dims == "nt" else pl.BlockSpec((tk, tn), lambda i, j, k: (k, j))
    o_spec = pl.BlockSpec((tm, tn), lambda i, j, k: (i, j))
    n_ex = len(extras)
    out_dtypes = out_dtypes or (out_dtype,) * n_out

    def finish(r, ex_refs, o_refs):
        if epi is not None:
            r = epi(r, *[e[...] for e in ex_refs])
        r = r if isinstance(r, tuple) else (r,)
        for o, v in zip(o_refs, r):
            o[...] = v.astype(o.dtype)

    def body(*refs):
        a_ref, b_ref = refs[:2]
        ex_refs = refs[2:2 + n_ex]
        o_refs = refs[2 + n_ex:2 + n_ex + n_out]
        part = _mxu(a_ref[...], b_ref[...], dn)
        if nk == 1:
            finish(part, ex_refs, o_refs)
            return
        acc = refs[-1]
        k = pl.program_id(2)

        @pl.when(k == 0)
        def _():
            acc[...] = part

        @pl.when(k > 0)
        def _():
            acc[...] += part

        @pl.when(k == nk - 1)
        def _():
            finish(acc[...], ex_refs, o_refs)

    outs = pl.pallas_call(
        body,
        name=name,
        grid=(M // tm, N // tn, nk),
        in_specs=[a_spec, b_spec] + [o_spec] * n_ex,
        out_specs=[o_spec] * n_out,
        out_shape=[jax.ShapeDtypeStruct((M, N), dt) for dt in out_dtypes],
        scratch_shapes=[pltpu.VMEM((tm, tn), F32)] if nk > 1 else [],
        compiler_params=pltpu.CompilerParams(
            dimension_semantics=("parallel", "parallel", "arbitrary"), vmem_limit_bytes=VMEM_BIG),
    )(a, b, *extras)
    return outs[0] if n_out == 1 else outs


class Row:
    def __init__(self, arr, width=None, col=0, lead=()):
        self.arr, self.lead, self.col = arr, tuple(lead), col
        self.width = arr.shape[-1] if width is None else width

    def spec(self, tm):
        lead, col = self.lead, self.col
        return pl.BlockSpec((None,) * len(lead) + (tm, self.width), lambda i: (*lead, i, col))


def row_call(name, fn, rows, params, new_outs, red_shapes=(), *, tm, into=None, into_cols=()):
    R = rows[0].arr.shape[len(rows[0].lead)]
    tm = min(tm, R)
    assert R % tm == 0, (name, R, tm)
    n_row, n_par, n_new, n_into, n_red = len(rows), len(params), len(new_outs), len(into_cols), len(red_shapes)

    def body(*refs):
        row_refs = refs[:n_row]
        par_refs = refs[n_row:n_row + n_par]
        pos = n_row + n_par + (1 if into is not None else 0)
        new_refs = refs[pos:pos + n_new]
        red_refs = refs[pos + n_new:pos + n_new + n_red]
        pos2 = pos + n_new + n_red
        into_ref = refs[pos2] if into is not None else None
        stage = refs[pos2 + 1:] if into is not None else ()
        i = pl.program_id(0)
        vals = fn(*[r[...] for r in row_refs], *[p[...] for p in par_refs])
        vals = vals if isinstance(vals, (tuple, list)) else (vals,)
        for o, v in zip(new_refs, vals[:n_new]):
            o[...] = v.astype(o.dtype)
        for (width, off), buf, v in zip(into_cols, stage, vals[n_new:n_new + n_into]):
            buf[...] = v.astype(buf.dtype)
            pltpu.sync_copy(buf, into_ref.at[pl.ds(pl.multiple_of(i * tm, tm), tm), pl.ds(off, width)])
        for o, v in zip(red_refs, vals[n_new + n_into:]):
            @pl.when(i == 0)
            def _(o=o, v=v):
                o[...] = v.astype(o.dtype)

            @pl.when(i > 0)
            def _(o=o, v=v):
                o[...] += v.astype(o.dtype)

    in_specs = [r.spec(tm) for r in rows] + [pl.BlockSpec(p.shape, lambda i, nd=p.ndim: (0,) * nd) for p in params]
    args = [r.arr for r in rows] + list(params)
    out_specs = [pl.BlockSpec((tm, w), lambda i: (i, 0)) for w, _ in new_outs]
    out_shape = [jax.ShapeDtypeStruct((R, w), dt) for w, dt in new_outs]
    out_specs += [pl.BlockSpec(s, lambda i, nd=len(s): (0,) * nd) for s in red_shapes]
    out_shape += [jax.ShapeDtypeStruct(s, F32) for s in red_shapes]
    aliases, scratch = {}, []
    if into is not None:
        in_specs.append(pl.BlockSpec(memory_space=pl.ANY))
        args.append(into)
        out_specs.append(pl.BlockSpec(memory_space=pl.ANY))
        out_shape.append(jax.ShapeDtypeStruct(into.shape, into.dtype))
        aliases = {len(args) - 1: len(out_shape) - 1}
        scratch = [pltpu.VMEM((tm, w), into.dtype) for w, _ in into_cols]
    outs = pl.pallas_call(
        body,
        name=name,
        grid=(R // tm,),
        in_specs=in_specs,
        out_specs=out_specs,
        out_shape=out_shape,
        scratch_shapes=scratch,
        input_output_aliases=aliases,
        compiler_params=pltpu.CompilerParams(dimension_semantics=("arbitrary",), vmem_limit_bytes=VMEM_BIG),
    )(*args)
    return list(outs)


def vjp_fn(fn, n_row, n_out, passthrough=0):
    def bwd(*args):
        rows = [a.astype(F32) for a in args[:n_row]]
        cts = args[n_row:n_row + n_out]
        thru = args[n_row + n_out:n_row + n_out + passthrough]
        pars = args[n_row + n_out + passthrough:]
        outs, pull = jax.vjp(lambda *xs: tuple(fn(*xs)), *rows, *pars)
        grads = pull(tuple(c.astype(o.dtype) for c, o in zip(cts, outs)))
        return tuple(grads[:n_row]) + tuple(thru) + tuple(grads[n_row:])

    return bwd


def _ln(x, g, b):
    mu = jnp.mean(x, axis=-1, keepdims=True)
    xc = x - mu
    var = jnp.mean(xc * xc, axis=-1, keepdims=True)
    return xc * lax.rsqrt(var + LN_EPS) * g + b


def f_ln_in(x, g, b):
    return (_ln(x, g, b),)


def f_deepnorm(x, y, g, b):
    return (_ln(ALPHA * x + y, g, b),)


def _rope(t, cos, sin):
    parts = []
    for h in range(LIN_HEADS):
        th = t[:, h * LIN_KEY_DIM:(h + 1) * LIN_KEY_DIM]
        parts.append(th * cos + swap_halves(th) * sin)
    return jnp.concatenate(parts, axis=1)


def f_ret_pre(rq, rk, cos, sin):
    return _rope(rq, cos, sin), _rope(rk, cos, sin) * LIN_KEY_DIM ** -0.5


def _log_sigmoid(x):
    return jnp.minimum(x, 0.0) - jnp.log(1.0 + jnp.exp(-jnp.abs(x)))


def f_gla_pre(gq, glr, wlr, blr):
    gate_pre = dot_nn(glr, wlr) + blr
    return gq * LIN_KEY_DIM ** -0.5, _log_sigmoid(gate_pre) / GLA_GATE_NORMALIZER


def f_normgate(ret_raw, rg, gla_raw, gg, gnorm):
    rets, glas = [], []
    for h in range(LIN_HEADS):
        sl = slice(h * LIN_VAL_DIM, (h + 1) * LIN_VAL_DIM)
        t = ret_raw[:, sl]
        mu = jnp.mean(t, axis=-1, keepdims=True)
        tc = t - mu
        rets.append(tc * lax.rsqrt(jnp.mean(tc * tc, axis=-1, keepdims=True) + LN_EPS))
        u = gla_raw[:, sl]
        glas.append(u * lax.rsqrt(jnp.mean(u * u, axis=-1, keepdims=True) + LN_EPS) * gnorm)
    ret = jax.nn.silu(rg) * jnp.concatenate(rets, axis=1)
    gla = jax.nn.silu(gg) * jnp.concatenate(glas, axis=1)
    return ret, gla


def f_merge(p0, p1, p2, gl):
    out = jax.nn.sigmoid(gl[:, :D_MODEL]) * p0
    out = out + jax.nn.sigmoid(gl[:, D_MODEL:2 * D_MODEL]) * p1
    return (out + jax.nn.sigmoid(gl[:, 2 * D_MODEL:]) * p2,)


def f_loss(y, tgt):
    diff = y - tgt
    sq = jnp.sum(diff * diff, axis=0, keepdims=True)
    lanes = sq[:, :128]
    for j in range(1, D_MODEL // 128):
        lanes = lanes + sq[:, j * 128:(j + 1) * 128]
    return diff * (1.0 / D_MODEL), lanes * (0.5 / D_MODEL)


def _adam(g, w, m, v):
    m = ADAM_B1 * m + (1.0 - ADAM_B1) * g
    v = ADAM_B2 * v + (1.0 - ADAM_B2) * jnp.square(g)
    m_hat = m / (1.0 - ADAM_B1 ** ADAM_STEP)
    v_hat = v / (1.0 - ADAM_B2 ** ADAM_STEP)
    delta = -ADAM_LR * (m_hat / (jnp.sqrt(v_hat) + ADAM_EPS) + ADAM_WD * w)
    return delta, m, v


def f_sum_adam(*args):
    g = args[0].astype(F32)
    for p in args[1:N_DEV]:
        g = g + p.astype(F32)
    return (g,) + _adam(g, *args[N_DEV:])


def f_adam(g, w, m, v):
    return _adam(g, w, m, v)


def _band_index():
    i = lax.broadcasted_iota(jnp.int32, (CHUNK, BAND), 0)
    m = lax.broadcasted_iota(jnp.int32, (CHUNK, BAND), 1)
    return jnp.minimum(i + PAD_ROWS - m, REL_CLIP) + REL_CLIP


def bias_table(rel_bias_l):
    def body(rb_ref, o_ref):
        h = pl.program_id(0)
        idx = _band_index()

        def step(r, acc):
            return jnp.where(idx == r, rb_ref[h * N_REL + r], acc)

        o_ref[...] = lax.fori_loop(REL_MIN, N_REL, step, jnp.zeros((CHUNK, BAND), F32))

    return pl.pallas_call(
        body,
        name="bias_table",
        grid=(ATTN_HEADS,),
        in_specs=[pl.BlockSpec(memory_space=pltpu.SMEM)],
        out_specs=pl.BlockSpec((None, CHUNK, BAND), lambda h: (h, 0, 0)),
        out_shape=jax.ShapeDtypeStruct((ATTN_HEADS, CHUNK, BAND), F32),
        compiler_params=pltpu.CompilerParams(dimension_semantics=("arbitrary",)),
    )(rel_bias_l.reshape(ATTN_HEADS * N_REL))


REL_LANES = 384


def bias_table_grad(dbias):
    def body(d_ref, o_ref):
        idx = _band_index()
        d = d_ref[...]
        lane = lax.broadcasted_iota(jnp.int32, (1, REL_LANES), 1)

        def step(r, acc):
            part = jnp.sum(jnp.where(idx == r, d, 0.0), axis=1, keepdims=True)
            return jnp.where(lane == r, jnp.sum(part, axis=0, keepdims=True), acc)

        o_ref[...] = lax.fori_loop(REL_MIN, N_REL, step, jnp.zeros((1, REL_LANES), F32))

    out = pl.pallas_call(
        body,
        name="bias_table_grad",
        grid=(ATTN_HEADS,),
        in_specs=[pl.BlockSpec((None, CHUNK, BAND), lambda h: (h, 0, 0))],
        out_specs=pl.BlockSpec((None, 1, REL_LANES), lambda h: (h, 0, 0)),
        out_shape=jax.ShapeDtypeStruct((ATTN_HEADS, 1, REL_LANES), F32),
        compiler_params=pltpu.CompilerParams(dimension_semantics=("arbitrary",)),
    )(dbias)
    return out[:, 0, :N_REL]


def _attn_probs(n, q_ref, kp, bias, key_chunk):
    r0 = pl.multiple_of(n * CHUNK, CHUNK)
    qc = q_ref[pl.ds(r0, CHUNK), :] * ATTN_HEAD_DIM ** -0.5
    kb = kp[pl.ds(r0, BAND), :]
    s = _mxu(qc, kb, _NT) + bias
    s = jnp.where(n + key_chunk >= ATTN_LEFT_CHUNKS, s, NEG_INF)
    e = jnp.exp(s - jnp.max(s, axis=-1, keepdims=True))
    return r0, qc, kb, e / jnp.sum(e, axis=-1, keepdims=True)


def _pad_front(dst, src, T):
    dst[pl.ds(0, PAD_ROWS), :] = jnp.zeros((PAD_ROWS, ATTN_HEAD_DIM), F32)
    dst[pl.ds(PAD_ROWS, T), :] = src[...]


def _head_spec(T, base):
    return pl.BlockSpec((T, ATTN_HEAD_DIM), lambda h: (0, base + h))


def attn_fwd(proj, bias_tab):
    T = proj.shape[0]

    def body(q_ref, k_ref, v_ref, b_ref, o_ref, kp, vp):
        _pad_front(kp, k_ref, T)
        _pad_front(vp, v_ref, T)
        bias = b_ref[...]
        key_chunk = lax.shift_right_logical(lax.broadcasted_iota(jnp.int32, (CHUNK, BAND), 1), 6)

        def step(n, carry):
            r0, _, _, p = _attn_probs(n, q_ref, kp, bias, key_chunk)
            o_ref[pl.ds(r0, CHUNK), :] = _mxu(p, vp[pl.ds(r0, BAND), :], _NN).astype(o_ref.dtype)
            return carry

        lax.fori_loop(0, T // CHUNK, step, 0)

    return pl.pallas_call(
        body,
        name="attn_fwd",
        grid=(ATTN_HEADS,),
        in_specs=[_head_spec(T, COL["aq"] // 128), _head_spec(T, COL["ak"] // 128), _head_spec(T, COL["av"] // 128),
                  pl.BlockSpec((None, CHUNK, BAND), lambda h: (h, 0, 0))],
        out_specs=pl.BlockSpec((T, ATTN_HEAD_DIM), lambda h: (0, h)),
        out_shape=jax.ShapeDtypeStruct((T, ATTN_HEADS * ATTN_HEAD_DIM), MXU_DTYPE),
        scratch_shapes=[pltpu.VMEM((T + PAD_ROWS, ATTN_HEAD_DIM), F32)] * 2,
        compiler_params=pltpu.CompilerParams(dimension_semantics=("arbitrary",), vmem_limit_bytes=VMEM_BIG),
    )(proj, proj, proj, bias_tab)


def attn_bwd(proj, bias_tab, dout, dproj):
    T = proj.shape[0]

    def body(q_ref, k_ref, v_ref, b_ref, do_ref, dproj_in, dbias_ref, dproj_ref, kp, vp, dkp, dvp, dq):
        del dproj_in
        h = pl.program_id(0)
        _pad_front(kp, k_ref, T)
        _pad_front(vp, v_ref, T)
        dkp[...] = jnp.zeros_like(dkp)
        dvp[...] = jnp.zeros_like(dvp)
        dbias_ref[...] = jnp.zeros_like(dbias_ref)
        bias = b_ref[...]
        key_chunk = lax.shift_right_logical(lax.broadcasted_iota(jnp.int32, (CHUNK, BAND), 1), 6)

        def step(n, carry):
            r0, qc, kb, p = _attn_probs(n, q_ref, kp, bias, key_chunk)
            do = do_ref[pl.ds(r0, CHUNK), :]
            dvp[pl.ds(r0, BAND), :] += _mxu(p, do, _TN)
            dp = _mxu(do, vp[pl.ds(r0, BAND), :], _NT)
            ds = p * (dp - jnp.sum(dp * p, axis=-1, keepdims=True))
            dq[pl.ds(r0, CHUNK), :] = _mxu(ds, kb, _NN) * ATTN_HEAD_DIM ** -0.5
            dkp[pl.ds(r0, BAND), :] += _mxu(ds, qc, _TN)
            dbias_ref[...] += ds
            return carry

        lax.fori_loop(0, T // CHUNK, step, 0)
        col = pl.multiple_of(h * ATTN_HEAD_DIM, ATTN_HEAD_DIM)
        pltpu.sync_copy(dq, dproj_ref.at[:, pl.ds(COL["aq"] + col, ATTN_HEAD_DIM)])
        pltpu.sync_copy(dkp.at[pl.ds(PAD_ROWS, T), :], dproj_ref.at[:, pl.ds(COL["ak"] + col, ATTN_HEAD_DIM)])
        pltpu.sync_copy(dvp.at[pl.ds(PAD_ROWS, T), :], dproj_ref.at[:, pl.ds(COL["av"] + col, ATTN_HEAD_DIM)])

    dbias, dproj = pl.pallas_call(
        body,
        name="attn_bwd",
        grid=(ATTN_HEADS,),
        in_specs=[_head_spec(T, COL["aq"] // 128), _head_spec(T, COL["ak"] // 128), _head_spec(T, COL["av"] // 128),
                  pl.BlockSpec((None, CHUNK, BAND), lambda h: (h, 0, 0)),
                  pl.BlockSpec((T, ATTN_HEAD_DIM), lambda h: (0, h)),
                  pl.BlockSpec(memory_space=pl.ANY)],
        out_specs=[pl.BlockSpec((None, CHUNK, BAND), lambda h: (h, 0, 0)), pl.BlockSpec(memory_space=pl.ANY)],
        out_shape=[jax.ShapeDtypeStruct((ATTN_HEADS, CHUNK, BAND), F32), jax.ShapeDtypeStruct(dproj.shape, dproj.dtype)],
        scratch_shapes=[pltpu.VMEM((T + PAD_ROWS, ATTN_HEAD_DIM), F32)] * 4 + [pltpu.VMEM((T, ATTN_HEAD_DIM), F32)],
        input_output_aliases={5: 1},
        compiler_params=pltpu.CompilerParams(dimension_semantics=("arbitrary",), vmem_limit_bytes=VMEM_BIG),
    )(proj, proj, proj, bias_tab, dout, dproj)
    return dproj, dbias


def _lin_segment(q, k, v, la, st):
    lower = _tri(CHUNK, True)
    outs = []
    for i in range(SEG_CHUNKS):
        sl = slice(i * CHUNK, (i + 1) * CHUNK)
        qi, ki, vi, lai = q[sl], k[sl], v[sl], la[sl]
        cum = chunk_cumsum(lai)
        last = jnp.sum(lai, axis=0, keepdims=True)
        e_pos, e_neg = jnp.exp(cum), jnp.exp(-cum)
        q_fwd = qi * e_pos
        s_fwd = dot_nt(q_fwd, ki * e_neg)
        s_bwd = dot_nt(qi * e_neg, ki * e_pos)
        o = dot_nn(jnp.where(lower, s_fwd, s_bwd), vi) + dot_nt(q_fwd, st)
        st = jnp.exp(last) * st + dot_tn(vi, ki * jnp.exp(last - cum))
        outs.append(o)
    return jnp.concatenate(outs, axis=0), st


class Win:
    def __init__(self, arr, width, base=0):
        self.arr, self.width, self.base = arr, width, base


def _lin_spec(w, seg_of):
    base = w.base
    if w.arr.shape[0] == 1:
        return pl.BlockSpec((1, w.width), lambda h, s: (0, base + h))
    return pl.BlockSpec((SEG, w.width), lambda h, s: (seg_of(s), base + h))


def _decay_rows(la_ref):
    la = la_ref[...]
    return jnp.broadcast_to(la, (SEG, LIN_KEY_DIM)) if la.shape[0] == 1 else la


def lin_fwd(name, q, k, v, la):
    T = q.arr.shape[0]
    nseg = T // SEG

    def body(q_ref, k_ref, v_ref, la_ref, o_ref, st_ref, st):
        @pl.when(pl.program_id(1) == 0)
        def _():
            st[...] = jnp.zeros_like(st)

        st0 = st[...]
        st_ref[...] = st0
        o, st1 = _lin_segment(q_ref[...], k_ref[...], v_ref[...], _decay_rows(la_ref), st0)
        o_ref[...] = o
        st[...] = st1

    fwd = lambda s: s
    return pl.pallas_call(
        body,
        name=name,
        grid=(LIN_HEADS, nseg),
        in_specs=[_lin_spec(w, fwd) for w in (q, k, v, la)],
        out_specs=[pl.BlockSpec((SEG, LIN_VAL_DIM), lambda h, s: (s, h)),
                   pl.BlockSpec((None, None, LIN_VAL_DIM, LIN_KEY_DIM), lambda h, s: (h, s, 0, 0))],
        out_shape=[jax.ShapeDtypeStruct((T, LIN_HEADS * LIN_VAL_DIM), F32),
                   jax.ShapeDtypeStruct((LIN_HEADS, nseg, LIN_VAL_DIM, LIN_KEY_DIM), F32)],
        scratch_shapes=[pltpu.VMEM((LIN_VAL_DIM, LIN_KEY_DIM), F32)],
        compiler_params=pltpu.CompilerParams(dimension_semantics=("arbitrary", "arbitrary"), vmem_limit_bytes=VMEM_BIG),
    )(q.arr, k.arr, v.arr, la.arr)


def lin_bwd(name, q, k, v, la, states, dout, with_decay_grad):
    T = q.arr.shape[0]
    nseg = T // SEG
    n_out = 4 if with_decay_grad else 3

    def body(q_ref, k_ref, v_ref, la_ref, st_ref, do_ref, *rest):
        outs, dst = rest[:n_out], rest[n_out]

        @pl.when(pl.program_id(1) == 0)
        def _():
            dst[...] = jnp.zeros_like(dst)

        _, pull = jax.vjp(_lin_segment, q_ref[...].astype(F32), k_ref[...].astype(F32), v_ref[...].astype(F32),
                          _decay_rows(la_ref), st_ref[...])
        grads = pull((do_ref[...], dst[...]))
        for o, g in zip(outs, grads[:n_out]):
            o[...] = g
        dst[...] = grads[4]

    rev = lambda s: nseg - 1 - s
    key_spec = pl.BlockSpec((SEG, LIN_KEY_DIM), lambda h, s: (rev(s), h))
    val_spec = pl.BlockSpec((SEG, LIN_VAL_DIM), lambda h, s: (rev(s), h))
    out_specs = [key_spec, key_spec, val_spec] + ([key_spec] if with_decay_grad else [])
    widths = [LIN_KEY_DIM, LIN_KEY_DIM, LIN_VAL_DIM] + ([LIN_KEY_DIM] if with_decay_grad else [])
    return pl.pallas_call(
        body,
        name=name,
        grid=(LIN_HEADS, nseg),
        in_specs=[_lin_spec(w, rev) for w in (q, k, v, la)] + [
            pl.BlockSpec((None, None, LIN_VAL_DIM, LIN_KEY_DIM), lambda h, s: (h, rev(s), 0, 0)), val_spec],
        out_specs=out_specs,
        out_shape=[jax.ShapeDtypeStruct((T, LIN_HEADS * w), F32) for w in widths],
        scratch_shapes=[pltpu.VMEM((LIN_VAL_DIM, LIN_KEY_DIM), F32)],
        compiler_params=pltpu.CompilerParams(dimension_semantics=("arbitrary", "arbitrary"), vmem_limit_bytes=VMEM_BIG),
    )(q.arr, k.arr, v.arr, la.arr, states, dout)


def _coords():
    return lax.axis_index("x"), lax.axis_index("y"), lax.axis_index("c")


def _peer(k):
    x, y, c = _coords()
    px = 1 - x if k & 4 else x
    py = 1 - y if k & 2 else y
    pc = 1 - c if k & 1 else c
    return (px, py, pc), 4 * px + 2 * py + pc


def exchange(name, arrays, gather):
    n_arr = len(arrays)

    def body(*refs):
        ins, outs = refs[:n_arr], refs[n_arr:2 * n_arr]
        send_sems, recv_sems, own_sems = refs[2 * n_arr:]
        x, y, c = _coords()
        me = 4 * x + 2 * y + c
        copies = []
        for a in range(n_arr):
            own = pltpu.make_async_copy(ins[a] if gather else ins[a].at[me], outs[a].at[me], own_sems.at[a])
            own.start()
            copies.append(own)
        for k in range(1, N_DEV):
            peer, pidx = _peer(k)
            for a in range(n_arr):
                cp = pltpu.make_async_remote_copy(
                    src_ref=ins[a] if gather else ins[a].at[pidx], dst_ref=outs[a].at[me],
                    send_sem=send_sems.at[a, k - 1], recv_sem=recv_sems.at[a, k - 1],
                    device_id=peer, device_id_type=pl.DeviceIdType.MESH)
                cp.start()
                copies.append(cp)
        for cp in copies:
            cp.wait()

    out_shape = [jax.ShapeDtypeStruct(((N_DEV,) + a.shape) if gather else a.shape, a.dtype) for a in arrays]
    return pl.pallas_call(
        body,
        name=name,
        in_specs=[pl.BlockSpec(memory_space=pl.ANY)] * n_arr,
        out_specs=[pl.BlockSpec(memory_space=pl.ANY)] * n_arr,
        out_shape=out_shape,
        scratch_shapes=[pltpu.SemaphoreType.DMA((n_arr, N_DEV - 1)), pltpu.SemaphoreType.DMA((n_arr, N_DEV - 1)),
                        pltpu.SemaphoreType.DMA((n_arr,))],
    )(*arrays)


def all_reduce_small(vec):
    def body(v_ref, o_ref, land, send_sems, recv_sems):
        x, y, c = _coords()
        me = 4 * x + 2 * y + c
        land[me] = v_ref[...]
        copies = []
        for k in range(1, N_DEV):
            peer, _ = _peer(k)
            cp = pltpu.make_async_remote_copy(
                src_ref=v_ref, dst_ref=land.at[me], send_sem=send_sems.at[k - 1], recv_sem=recv_sems.at[k - 1],
                device_id=peer, device_id_type=pl.DeviceIdType.MESH)
            cp.start()
            copies.append(cp)
        for cp in copies:
            cp.wait()
        total = land[0]
        for s in range(1, N_DEV):
            total = total + land[s]
        o_ref[...] = total

    return pl.pallas_call(
        body,
        name="all_reduce_small",
        in_specs=[pl.BlockSpec(memory_space=pltpu.VMEM)],
        out_specs=pl.BlockSpec(memory_space=pltpu.VMEM),
        out_shape=jax.ShapeDtypeStruct(vec.shape, F32),
        scratch_shapes=[pltpu.VMEM((N_DEV,) + vec.shape, F32), pltpu.SemaphoreType.DMA((N_DEV - 1,)),
                        pltpu.SemaphoreType.DMA((N_DEV - 1,))],
    )(vec)


def _pack(parts):
    flat = jnp.concatenate([p.reshape(-1).astype(F32) for p in parts])
    n = flat.shape[0]
    rows = -(-n // 1024) * 8
    return jnp.pad(flat, (0, rows * 128 - n)).reshape(rows, 128)


def _unpack(packed, shapes):
    flat, out, pos = packed.reshape(-1), [], 0
    for s in shapes:
        n = math.prod(s)
        out.append(flat[pos:pos + n].reshape(s))
        pos += n
    return out


def _row2(v):
    return v.reshape(1, -1)


def _layer_forward(h, W, consts):
    T = h.shape[0]
    cos, sin, ret_decay = consts
    proj = mm("proj", h, W["in"], "nn", tm=1024, tn=512, tk=1024)
    bias_tab = bias_table(W["rel_bias"])
    attn = attn_fwd(proj, bias_tab)
    rq2, rk2 = row_call("ret_pre", f_ret_pre, [Row(proj, 512, COL["rq"] // 512), Row(proj, 512, COL["rk"] // 512),
                                              Row(cos), Row(sin)], [], [(512, F32), (512, F32)], tm=512)
    ret_raw, ret_st = lin_fwd("ret_fwd", Win(rq2, 128), Win(rk2, 128), Win(proj, 256, COL["rv"] // 256), Win(ret_decay, 128))
    gq2, gla_la = row_call("gla_pre", f_gla_pre, [Row(proj, 512, COL["gq"] // 512), Row(proj, GLR_BLOCK, GLR_OFF // GLR_BLOCK)],
                           [W["wlr"], W["blr"]], [(512, F32), (512, F32)], tm=512)
    gla_raw, gla_st = lin_fwd("gla_fwd", Win(gq2, 128), Win(proj, 128, COL["gk"] // 128), Win(proj, 256, COL["gv"] // 256),
                              Win(gla_la, 128))
    ret_b, gla_b = row_call("normgate", f_normgate, [Row(ret_raw), Row(proj, 1024, COL["rg"] // 1024), Row(gla_raw),
                                                    Row(proj, 1024, COL["gg"] // 1024)], [W["gnorm"]],
                            [(1024, MXU_DTYPE), (1024, MXU_DTYPE)], tm=256)
    branches = (attn, ret_b, gla_b)
    p = [mm("branch_proj", branches[n], W["branch"][n], "nn", tm=1024, tn=1024, tk=1024) for n in range(3)]
    merged, = row_call("merge", f_merge, [Row(p[0]), Row(p[1]), Row(p[2]), Row(proj, GATES_W, GATES_OFF // GATES_W)], [],
                       [(1024, MXU_DTYPE)], tm=256)
    mix = mm("out_proj", merged, W["out"], "nn", tm=1024, tn=1024, tk=1024)
    x1, = row_call("ln1", f_deepnorm, [Row(h), Row(mix)], [W["ln1_g"], W["ln1_b"]], [(1024, F32)], tm=512)
    u, act = mm("mlp_up", x1, W["up"], "nn", tm=1024, tn=512, tk=1024, n_out=2, out_dtypes=(F32, MXU_DTYPE),
                epi=lambda r: (r, jnp.square(jnp.maximum(r, 0.0))))
    y = mm("mlp_down", act, W["down"], "nn", tm=1024, tn=1024, tk=1024)
    x2, = row_call("ln2", f_deepnorm, [Row(x1), Row(y)], [W["ln2_g"], W["ln2_b"]], [(1024, F32)], tm=512)
    saved = dict(h=h, proj=proj, bias_tab=bias_tab, rq2=rq2, rk2=rk2, ret_st=ret_st, gq2=gq2, gla_la=gla_la, gla_st=gla_st,
                 ret_raw=ret_raw, gla_raw=gla_raw, branches=branches, p=p, merged=merged, mix=mix, x1=x1, u=u, act=act, y=y)
    return x2, saved


def _layer_backward(dx2, S, W, consts):
    cos, sin, ret_decay = consts
    h, proj, T = S["h"], S["proj"], S["h"].shape[0]
    add = lambda r, e: r + e
    dx1a, dy, dg2, db2 = row_call("ln2_bwd", vjp_fn(f_deepnorm, 2, 1), [Row(S["x1"]), Row(S["y"]), Row(dx2)],
                                  [W["ln2_g"], W["ln2_b"]], [(1024, F32), (1024, F32)], [(1, 1024), (1, 1024)], tm=256)
    du = mm("mlp_down_bwd", dy, W["down"], "nt", tm=1024, tn=512, tk=1024, extras=(S["u"],),
            epi=lambda r, u: r * (2.0 * jnp.maximum(u, 0.0)))
    g_down = mm("mlp_down_wgrad", S["act"], dy, "tn", tm=1024, tn=1024, tk=512, out_dtype=WIRE_DTYPE)
    dx1 = mm("mlp_up_bwd", du, W["up"], "nt", tm=512, tn=1024, tk=1024, extras=(dx1a,), epi=add)
    g_up = mm("mlp_up_wgrad", S["x1"], du, "tn", tm=1024, tn=1024, tk=512, out_dtype=WIRE_DTYPE)
    dha, dmix, dg1, db1 = row_call("ln1_bwd", vjp_fn(f_deepnorm, 2, 1), [Row(h), Row(S["mix"]), Row(dx1)],
                                   [W["ln1_g"], W["ln1_b"]], [(1024, F32), (1024, F32)], [(1, 1024), (1, 1024)], tm=256)
    dmerged = mm("out_proj_bwd", dmix, W["out"], "nt", tm=1024, tn=1024, tk=1024)
    g_out = mm("out_proj_wgrad", S["merged"], dmix, "tn", tm=1024, tn=1024, tk=512, out_dtype=WIRE_DTYPE)
    dproj = jnp.zeros((T, PROJ_PAD), F32)
    p = S["p"]
    dp0, dp1, dp2, dproj = row_call(
        "merge_bwd", vjp_fn(f_merge, 4, 1), [Row(p[0]), Row(p[1]), Row(p[2]), Row(proj, GATES_W, GATES_OFF // GATES_W), Row(dmerged)],
        [], [(1024, F32)] * 3, tm=256, into=dproj, into_cols=[(GATES_W, GATES_OFF)])
    dps = (dp0, dp1, dp2)
    dbr = [mm("branch_proj_bwd", dps[n], W["branch"][n], "nt", tm=1024, tn=1024, tk=1024) for n in range(3)]
    g_branch = [mm("branch_proj_wgrad", S["branches"][n], dps[n], "tn", tm=1024, tn=1024, tk=512, out_dtype=WIRE_DTYPE)
                for n in range(3)]
    dret_raw, dgla_raw, dgnorm, dproj = row_call(
        "normgate_bwd", vjp_fn(lambda a, b, c, d, g: _reorder_normgate(a, b, c, d, g), 4, 2),
        [Row(S["ret_raw"]), Row(S["gla_raw"]), Row(proj, 1024, COL["rg"] // 1024), Row(proj, 1024, COL["gg"] // 1024),
         Row(dbr[1]), Row(dbr[2])], [W["gnorm"]], [(1024, F32), (1024, F32)], [(1, LIN_VAL_DIM)], tm=256,
        into=dproj, into_cols=[(1024, COL["rg"]), (1024, COL["gg"])])
    dgq2, dgk, dgv, dgla_la = lin_bwd("gla_bwd", Win(S["gq2"], 128), Win(proj, 128, COL["gk"] // 128),
                                      Win(proj, 256, COL["gv"] // 256), Win(S["gla_la"], 128), S["gla_st"], dgla_raw, True)
    dwlr, dblr, dproj = row_call(
        "gla_pre_bwd", _gla_pre_bwd, [Row(proj, 512, COL["gq"] // 512), Row(proj, GLR_BLOCK, GLR_OFF // GLR_BLOCK),
                                      Row(dgq2), Row(dgla_la), Row(dgk), Row(dgv)], [W["wlr"], W["blr"]], [],
        [(GLR_BLOCK, 512), (1, 512)], tm=256, into=dproj,
        into_cols=[(512, COL["gq"]), (PROJ_PAD - GLR_OFF, GLR_OFF), (512, COL["gk"]), (1024, COL["gv"])])
    drq2, drk2, drv = lin_bwd("ret_bwd", Win(S["rq2"], 128), Win(S["rk2"], 128), Win(proj, 256, COL["rv"] // 256),
                              Win(ret_decay, 128), S["ret_st"], dret_raw, False)
    dproj, = row_call(
        "ret_pre_bwd", _ret_pre_bwd, [Row(proj, 512, COL["rq"] // 512), Row(proj, 512, COL["rk"] // 512), Row(cos), Row(sin),
                                      Row(drq2), Row(drk2), Row(drv)], [], [], tm=256, into=dproj,
        into_cols=[(512, COL["rq"]), (512, COL["rk"]), (1024, COL["rv"])])
    dproj, dbias = attn_bwd(proj, S["bias_tab"], dbr[0], dproj)
    d_rel_bias = bias_table_grad(dbias)
    dh = mm("proj_bwd", dproj, W["in"], "nt", tm=512, tn=1024, tk=1280, extras=(dha,), epi=add)
    g_in = mm("proj_wgrad", h, dproj, "tn", tm=1024, tn=1280, tk=512, out_dtype=WIRE_DTYPE)
    big = dict(g_in=g_in, branch=g_branch, out=g_out, up=g_up, down=g_down)
    small = dict(rel_bias=d_rel_bias, wlr=dwlr[:GLA_RANK], blr=dblr, gnorm=dgnorm, ln1_g=dg1, ln1_b=db1, ln2_g=dg2, ln2_b=db2)
    return dh, big, small


def _reorder_normgate(ret_raw, gla_raw, rg, gg, gnorm):
    return f_normgate(ret_raw, rg, gla_raw, gg, gnorm)


def _gla_pre_bwd(gq, glr, dq2, dla, dgk, dgv, wlr, blr):
    _, pull = jax.vjp(f_gla_pre, gq, glr, wlr, blr)
    dgq, dglr, dwlr, dblr = pull((dq2, dla))
    dglr = jnp.concatenate([dglr, jnp.zeros((dglr.shape[0], PROJ_PAD - GLR_OFF - GLR_BLOCK), F32)], axis=1)
    return dgq, dglr, dgk, dgv, dwlr, dblr


def _ret_pre_bwd(rq, rk, cos, sin, drq2, drk2, drv):
    _, pull = jax.vjp(lambda a, b: f_ret_pre(a, b, cos, sin), rq, rk)
    drq, drk = pull((drq2, drk2))
    return drq, drk, drv


def _gathered_weights(w_in, gla_w_lr, w_branch, w_out, w_up, w_down):
    wire = lambda w: w.astype(WIRE_DTYPE)
    g_in, g_lr, g_br, g_out, g_up, g_down = exchange(
        "gather_weights", [wire(w_in), gla_w_lr, wire(w_branch), wire(w_out), wire(w_up), wire(w_down)], gather=True)
    layers = []
    for l in range(DEPTH):
        full_in = jnp.transpose(g_in[:, l], (1, 0, 2)).reshape(D_MODEL, PROJ_WIDTH)
        w_in_pad = jnp.concatenate(
            [full_in[:, :MAIN_W], full_in[:, GATES_SRC:], full_in[:, GLR_SRC:GATES_SRC],
             jnp.zeros((D_MODEL, PROJ_PAD - PROJ_WIDTH), WIRE_DTYPE)], axis=1)
        wlr = jnp.transpose(g_lr[:, l], (1, 0, 2)).reshape(GLA_RANK, LIN_HEADS * LIN_KEY_DIM)
        wlr_pad = jnp.pad(wlr, ((0, GLR_BLOCK - GLA_RANK), (0, 0)))
        layers.append({
            "in": w_in_pad, "wlr": wlr_pad,
            "branch": [g_br[:, l, n].reshape(D_MODEL, D_MODEL) for n in range(3)],
            "out": g_out[:, l].reshape(D_MODEL, D_MODEL),
            "up": jnp.transpose(g_up[:, l], (1, 0, 2)).reshape(D_MODEL, D_FF),
            "down": g_down[:, l].reshape(D_FF, D_MODEL),
        })
    return layers


def _slabs(big):
    g = big["g_in"]
    g = jnp.concatenate([g[:, :MAIN_W], g[:, GLR_OFF:GLR_OFF + GLA_RANK], g[:, GATES_OFF:GATES_OFF + GATES_W]], axis=1)
    return [
        jnp.transpose(g.reshape(D_MODEL, N_DEV, PROJ_WIDTH // N_DEV), (1, 0, 2)),
        jnp.stack([b.reshape(N_DEV, D_MODEL // N_DEV, D_MODEL) for b in big["branch"]], axis=1),
        big["out"].reshape(N_DEV, D_MODEL // N_DEV, D_MODEL),
        jnp.transpose(big["up"].reshape(D_MODEL, N_DEV, D_FF // N_DEV), (1, 0, 2)),
        big["down"].reshape(N_DEV, D_FF // N_DEV, D_MODEL),
    ]


def _sum_adam(name, parts, w, m, v, tm):
    C = w.shape[-1]
    rows = [Row(parts, lead=(s,)) for s in range(N_DEV)] + [Row(w), Row(m), Row(v)]
    return row_call(name, f_sum_adam, rows, [], [(C, F32)] * 4, tm=tm)


def kernel(x, ln_in_g, ln_in_b, w_in, rel_bias, gla_w_lr, gla_b_lr, gla_norm_g, w_branch, w_out, ln1_g, ln1_b, w_up, w_down, ln2_g, ln2_b, loss_target, m_ln_in_g, m_ln_in_b, m_w_in, m_rel_bias, m_gla_w_lr, m_gla_b_lr, m_gla_norm_g, m_w_branch, m_w_out, m_ln1_g, m_ln1_b, m_w_up, m_w_down, m_ln2_g, m_ln2_b, v_ln_in_g, v_ln_in_b, v_w_in, v_rel_bias, v_gla_w_lr, v_gla_b_lr, v_gla_norm_g, v_w_branch, v_w_out, v_ln1_g, v_ln1_b, v_w_up, v_w_down, v_ln2_g, v_ln2_b):
    T = x.shape[1]
    x2d = x.reshape(T, D_MODEL)
    tgt = loss_target.reshape(T, D_MODEL)
    me = 4 * lax.axis_index("x") + 2 * lax.axis_index("y") + lax.axis_index("c")

    inv = ROPE_BASE ** (-jnp.arange(0, LIN_KEY_DIM, 2, dtype=F32) / LIN_KEY_DIM)
    ang = jnp.arange(T, dtype=F32)[:, None] * inv[None, :]
    cos = jnp.concatenate([jnp.cos(ang), jnp.cos(ang)], axis=1)
    sin = jnp.concatenate([-jnp.sin(ang), jnp.sin(ang)], axis=1)
    log_gamma = jnp.log1p(-jnp.exp2(-5.0 - jnp.arange(LIN_HEADS, dtype=F32)))
    ret_decay = jnp.repeat(log_gamma, LIN_KEY_DIM).reshape(1, LIN_HEADS * LIN_KEY_DIM)
    consts = (cos, sin, ret_decay)

    layers = _gathered_weights(w_in, gla_w_lr, w_branch, w_out, w_up, w_down)
    for l in range(DEPTH):
        layers[l].update(rel_bias=rel_bias[l], blr=_row2(gla_b_lr[l]), gnorm=_row2(gla_norm_g[l]), ln1_g=_row2(ln1_g[l]),
                         ln1_b=_row2(ln1_b[l]), ln2_g=_row2(ln2_g[l]), ln2_b=_row2(ln2_b[l]))

    h, = row_call("ln_in", f_ln_in, [Row(x2d)], [_row2(ln_in_g), _row2(ln_in_b)], [(D_MODEL, F32)], tm=512)
    saved = []
    for l in range(DEPTH):
        h, s = _layer_forward(h, layers[l], consts)
        saved.append(s)
    dh, loss_lanes = row_call("loss", f_loss, [Row(h), Row(tgt)], [], [(D_MODEL, F32)], [(1, 128)], tm=512)

    small, received = [None] * DEPTH, [None] * DEPTH
    for l in reversed(range(DEPTH)):
        dh, big, small[l] = _layer_backward(dh, saved[l], layers[l], consts)
        received[l] = exchange("scatter_grads", _slabs(big), gather=False)
    grad_x, d_in_g, d_in_b = row_call("ln_in_bwd", vjp_fn(f_ln_in, 1, 1), [Row(x2d), Row(dh)],
                                      [_row2(ln_in_g), _row2(ln_in_b)], [(D_MODEL, F32)], [(1, D_MODEL), (1, D_MODEL)], tm=256)

    small_names = ("rel_bias", "wlr", "blr", "gnorm", "ln1_g", "ln1_b", "ln2_g", "ln2_b")
    parts = [d_in_g, d_in_b] + [small[l][n] for l in range(DEPTH) for n in small_names] + [loss_lanes]
    shapes = [(D_MODEL,), (D_MODEL,)] + [s for l in range(DEPTH) for s in (
        (ATTN_HEADS, N_REL), (GLA_RANK, 512), (512,), (LIN_VAL_DIM,), (D_MODEL,), (D_MODEL,), (D_MODEL,), (D_MODEL,))] + [(128,)]
    red = _unpack(all_reduce_small(_pack(parts)), shapes)
    loss = jnp.sum(red[-1])
    g_small = {"ln_in_g": red[0], "ln_in_b": red[1]}
    for l in range(DEPTH):
        for j, n in enumerate(small_names):
            g_small[(l, n)] = red[2 + l * len(small_names) + j]
    shard = LIN_HEADS * LIN_KEY_DIM // N_DEV
    g_wlr = jnp.stack([lax.dynamic_slice_in_dim(g_small[(l, "wlr")], me * shard, shard, axis=1) for l in range(DEPTH)])
    stack = lambda n: jnp.stack([g_small[(l, n)] for l in range(DEPTH)])
    g_rep = [g_small["ln_in_g"], g_small["ln_in_b"], stack("rel_bias"), g_wlr, stack("blr"), stack("gnorm"),
             stack("ln1_g"), stack("ln1_b"), stack("ln2_g"), stack("ln2_b")]
    w_rep = [ln_in_g, ln_in_b, rel_bias, gla_w_lr, gla_b_lr, gla_norm_g, ln1_g, ln1_b, ln2_g, ln2_b]
    m_rep = [m_ln_in_g, m_ln_in_b, m_rel_bias, m_gla_w_lr, m_gla_b_lr, m_gla_norm_g, m_ln1_g, m_ln1_b, m_ln2_g, m_ln2_b]
    v_rep = [v_ln_in_g, v_ln_in_b, v_rel_bias, v_gla_w_lr, v_gla_b_lr, v_gla_norm_g, v_ln1_g, v_ln1_b, v_ln2_g, v_ln2_b]
    rep_shapes = [w.shape for w in w_rep]
    d_pack, m_pack, v_pack = row_call("adam_small", f_adam, [Row(_pack(g_rep)), Row(_pack(w_rep)), Row(_pack(m_rep)), Row(_pack(v_rep))],
                                      [], [(128, F32)] * 3, tm=256)
    d_rep, nm_rep, nv_rep = _unpack(d_pack, rep_shapes), _unpack(m_pack, rep_shapes), _unpack(v_pack, rep_shapes)

    def owned(idx, w, m, v, name, tm):
        res = []
        for l in range(DEPTH):
            shp = w[l].shape
            flat = lambda a: a.reshape(-1, shp[-1])
            parts_l = received[l][idx].reshape(N_DEV, -1, shp[-1])
            res.append([o.reshape(shp) for o in _sum_adam(name, parts_l, flat(w[l]), flat(m[l]), flat(v[l]), tm)])
        return [jnp.stack([res[l][j] for l in range(DEPTH)]) for j in range(4)]

    o_in = owned(0, w_in, m_w_in, v_w_in, "adam_w_in", 256)
    o_br = owned(1, w_branch, m_w_branch, v_w_branch, "adam_w_branch", 128)
    o_out = owned(2, w_out, m_w_out, v_w_out, "adam_w_out", 128)
    o_up = owned(3, w_up, m_w_up, v_w_up, "adam_w_up", 256)
    o_down = owned(4, w_down, m_w_down, v_w_down, "adam_w_down", 256)

    def tree(j, rep):
        return [rep[0], rep[1], o_in[j], rep[2], rep[3], rep[4], rep[5], o_br[j], o_out[j], rep[6], rep[7], o_up[j], o_down[j],
                rep[8], rep[9]]

    return (loss, grad_x.reshape(1, T, D_MODEL), *tree(0, g_rep), *tree(1, d_rep), *tree(2, nm_rep), *tree(3, nv_rep))
```

```python
import functools
import math

import jax
import jax.numpy as jnp
from jax import lax
from jax.experimental import pallas as pl
from jax.experimental.pallas import tpu as pltpu

F32 = jnp.float32
MXU_DTYPE = jnp.bfloat16
WIRE_DTYPE = jnp.bfloat16

N_DEV = 8
D_MODEL = 1024
DEPTH = 2
CHUNK = 64
ATTN_HEADS = 8
ATTN_HEAD_DIM = 128
ATTN_LEFT_CHUNKS = 8
BAND = (ATTN_LEFT_CHUNKS + 1) * CHUNK
PAD_ROWS = ATTN_LEFT_CHUNKS * CHUNK
REL_CLIP = 2 * CHUNK
N_REL = 2 * REL_CLIP + 1
REL_MIN = REL_CLIP - (CHUNK - 1)
LIN_HEADS = 4
LIN_KEY_DIM = 128
LIN_VAL_DIM = 256
GLA_RANK = 16
GLA_GATE_NORMALIZER = 16.0
ROPE_BASE = 10000.0
D_FF = 4 * D_MODEL
LN_EPS = 1e-5
NEG_INF = -1e30
ALPHA = (2 * DEPTH) ** 0.25
ADAM_LR, ADAM_B1, ADAM_B2, ADAM_EPS, ADAM_WD, ADAM_STEP = 0.001, 0.9, 0.999, 1e-08, 0.01, 10

PROJ_WIDTH = 12304
GLR_SRC = 9216
GATES_SRC = GLR_SRC + GLA_RANK
MAIN_W = 9216
GATES_W = 3 * D_MODEL
GATES_OFF = MAIN_W
GLR_OFF = MAIN_W + GATES_W
PROJ_PAD = 12800
GLR_BLOCK = 128
COL = dict(aq=0, ak=1024, av=2048, rq=3072, rk=3584, rv=4096, rg=5120, gq=6144, gk=6656, gv=7168, gg=8192)

SEG_CHUNKS = 8
SEG = SEG_CHUNKS * CHUNK
VMEM_BIG = 56 * 1024 * 1024

_NN = (((1,), (0,)), ((), ()))
_NT = (((1,), (1,)), ((), ()))
_TN = (((0,), (0,)), ((), ()))


def _mxu(a, b, dn):
    return lax.dot_general(a.astype(MXU_DTYPE), b.astype(MXU_DTYPE), dn, preferred_element_type=F32)


@jax.custom_vjp
def dot_nn(a, b):
    return _mxu(a, b, _NN)


def _dot_nn_fwd(a, b):
    return _mxu(a, b, _NN), (a, b)


def _dot_nn_bwd(res, g):
    a, b = res
    return _mxu(g, b, _NT).astype(a.dtype), _mxu(a, g, _TN).astype(b.dtype)


dot_nn.defvjp(_dot_nn_fwd, _dot_nn_bwd)


@jax.custom_vjp
def dot_nt(a, b):
    return _mxu(a, b, _NT)


def _dot_nt_fwd(a, b):
    return _mxu(a, b, _NT), (a, b)


def _dot_nt_bwd(res, g):
    a, b = res
    return _mxu(g, b, _NN).astype(a.dtype), _mxu(g, a, _TN).astype(b.dtype)


dot_nt.defvjp(_dot_nt_fwd, _dot_nt_bwd)


@jax.custom_vjp
def dot_tn(a, b):
    return _mxu(a, b, _TN)


def _dot_tn_fwd(a, b):
    return _mxu(a, b, _TN), (a, b)


def _dot_tn_bwd(res, g):
    a, b = res
    return _mxu(b, g, _NT).astype(a.dtype), _mxu(a, g, _NN).astype(b.dtype)


dot_tn.defvjp(_dot_tn_fwd, _dot_tn_bwd)


def _tri(n, lower):
    r = lax.broadcasted_iota(jnp.int32, (n, n), 0)
    c = lax.broadcasted_iota(jnp.int32, (n, n), 1)
    return (r >= c) if lower else (r <= c)


def _f32_dot(a, b):
    return lax.dot_general(a, b, _NN, precision=lax.Precision.HIGHEST, preferred_element_type=F32)


@jax.custom_vjp
def chunk_cumsum(x):
    return _f32_dot(_tri(x.shape[0], True).astype(F32), x)


def _chunk_cumsum_fwd(x):
    return chunk_cumsum(x), None


def _chunk_cumsum_bwd(_, g):
    return (_f32_dot(_tri(g.shape[0], False).astype(F32), g),)


chunk_cumsum.defvjp(_chunk_cumsum_fwd, _chunk_cumsum_bwd)


@jax.custom_vjp
def swap_halves(t):
    return pltpu.roll(t, t.shape[-1] // 2, t.ndim - 1)


def _swap_fwd(t):
    return swap_halves(t), None


def _swap_bwd(_, g):
    return (swap_halves(g),)


swap_halves.defvjp(_swap_fwd, _swap_bwd)


def mm(name, a, b, dims, *, tm, tn, tk, out_dtype=F32, extras=(), epi=None, n_out=1, out_dtypes=None):
    if dims == "nn":
        (M, K), N = a.shape, b.shape[1]
    elif dims == "nt":
        (M, K), N = a.shape, b.shape[0]
    else:
        (K, M), N = a.shape, b.shape[1]
    tm, tn, tk = min(tm, M), min(tn, N), min(tk, K)
    assert M % tm == 0 and N % tn == 0 and K % tk == 0, (name, M, N, K, tm, tn, tk)
    nk = K // tk
    dn = {"nn": _NN, "nt": _NT, "tn": _TN}[dims]
    a_spec = pl.BlockSpec((tk, tm), lambda i, j, k: (k, i)) if dims == "tn" else pl.BlockSpec((tm, tk), lambda i, j, k: (i, k))
    b_spec = pl.BlockSpec((tn, tk), lambda i, j, k: (j, k)) if dims == "nt" else pl.BlockSpec((tk, tn), lambda i, j, k: (k, j))
    o_spec = pl.BlockSpec((tm, tn), lambda i, j, k: (i, j))
    n_ex = len(extras)
    out_dtypes = out_dtypes or (out_dtype,) * n_out

    def finish(r, ex_refs, o_refs):
        if epi is not None:
            r = epi(r, *[e[...] for e in ex_refs])
        r = r if isinstance(r, tuple) else (r,)
        for o, v in zip(o_refs, r):
            o[...] = v.astype(o.dtype)

    def body(*refs):
        a_ref, b_ref = refs[:2]
        ex_refs = refs[2:2 + n_ex]
        o_refs = refs[2 + n_ex:2 + n_ex + n_out]
        part = _mxu(a_ref[...], b_ref[...], dn)
        if nk == 1:
            finish(part, ex_refs, o_refs)
            return
        acc = refs[-1]
        k = pl.program_id(2)

        @pl.when(k == 0)
        def _():
            acc[...] = part

        @pl.when(k > 0)
        def _():
            acc[...] += part

        @pl.when(k == nk - 1)
        def _():
            finish(acc[...], ex_refs, o_refs)

    outs = pl.pallas_call(
        body,
        name=name,
        grid=(M // tm, N // tn, nk),
        in_specs=[a_spec, b_spec] + [o_spec] * n_ex,
        out_specs=[o_spec] * n_out,
        out_shape=[jax.ShapeDtypeStruct((M, N), dt) for dt in out_dtypes],
        scratch_shapes=[pltpu.VMEM((tm, tn), F32)] if nk > 1 else [],
        compiler_params=pltpu.CompilerParams(
            dimension_semantics=("parallel", "parallel", "arbitrary"), vmem_limit_bytes=VMEM_BIG),
    )(a, b, *extras)
    return outs[0] if n_out == 1 else outs


class Row:
    def __init__(self, arr, width=None, col=0, lead=()):
        self.arr, self.lead, self.col = arr, tuple(lead), col
        self.width = arr.shape[-1] if width is None else width

    def spec(self, tm):
        lead, col = self.lead, self.col
        return pl.BlockSpec((None,) * len(lead) + (tm, self.width), lambda i: (*lead, i, col))


def row_call(name, fn, rows, params, new_outs, red_shapes=(), *, tm, into=None, into_cols=()):
    R = rows[0].arr.shape[len(rows[0].lead)]
    tm = min(tm, R)
    assert R % tm == 0, (name, R, tm)
    n_row, n_par, n_new, n_into, n_red = len(rows), len(params), len(new_outs), len(into_cols), len(red_shapes)
    fresh = isinstance(into, jax.ShapeDtypeStruct)

    def body(*refs):
        row_refs = refs[:n_row]
        par_refs = refs[n_row:n_row + n_par]
        pos = n_row + n_par + (1 if (into is not None and not fresh) else 0)
        new_refs = refs[pos:pos + n_new]
        red_refs = refs[pos + n_new:pos + n_new + n_red]
        pos2 = pos + n_new + n_red
        into_ref = refs[pos2] if into is not None else None
        stage = refs[pos2 + 1:] if into is not None else ()
        i = pl.program_id(0)
        vals = fn(*[r[...] for r in row_refs], *[p[...] for p in par_refs])
        vals = vals if isinstance(vals, (tuple, list)) else (vals,)
        for o, v in zip(new_refs, vals[:n_new]):
            o[...] = v.astype(o.dtype)
        for (width, off), buf, v in zip(into_cols, stage, vals[n_new:n_new + n_into]):
            buf[...] = v.astype(buf.dtype)
            pltpu.sync_copy(buf, into_ref.at[pl.ds(pl.multiple_of(i * tm, tm), tm), pl.ds(off, width)])
        for o, v in zip(red_refs, vals[n_new + n_into:]):
            @pl.when(i == 0)
            def _(o=o, v=v):
                o[...] = v.astype(o.dtype)

            @pl.when(i > 0)
            def _(o=o, v=v):
                o[...] += v.astype(o.dtype)

    in_specs = [r.spec(tm) for r in rows] + [pl.BlockSpec(p.shape, lambda i, nd=p.ndim: (0,) * nd) for p in params]
    args = [r.arr for r in rows] + list(params)
    out_specs = [pl.BlockSpec((tm, w), lambda i: (i, 0)) for w, _ in new_outs]
    out_shape = [jax.ShapeDtypeStruct((R, w), dt) for w, dt in new_outs]
    out_specs += [pl.BlockSpec(s, lambda i, nd=len(s): (0,) * nd) for s in red_shapes]
    out_shape += [jax.ShapeDtypeStruct(s, F32) for s in red_shapes]
    aliases, scratch = {}, []
    if into is not None:
        out_specs.append(pl.BlockSpec(memory_space=pl.ANY))
        out_shape.append(jax.ShapeDtypeStruct(into.shape, into.dtype))
        if not fresh:
            in_specs.append(pl.BlockSpec(memory_space=pl.ANY))
            args.append(into)
            aliases = {len(args) - 1: len(out_shape) - 1}
        scratch = [pltpu.VMEM((tm, w), into.dtype) for w, _ in into_cols]
    outs = pl.pallas_call(
        body,
        name=name,
        grid=(R // tm,),
        in_specs=in_specs,
        out_specs=out_specs,
        out_shape=out_shape,
        scratch_shapes=scratch,
        input_output_aliases=aliases,
        compiler_params=pltpu.CompilerParams(dimension_semantics=("arbitrary",), vmem_limit_bytes=VMEM_BIG),
    )(*args)
    return list(outs)


def vjp_fn(fn, n_row, n_out, passthrough=0):
    def bwd(*args):
        rows = [a.astype(F32) for a in args[:n_row]]
        cts = args[n_row:n_row + n_out]
        thru = args[n_row + n_out:n_row + n_out + passthrough]
        pars = args[n_row + n_out + passthrough:]
        outs, pull = jax.vjp(lambda *xs: tuple(fn(*xs)), *rows, *pars)
        grads = pull(tuple(c.astype(o.dtype) for c, o in zip(cts, outs)))
        return tuple(grads[:n_row]) + tuple(thru) + tuple(grads[n_row:])

    return bwd


def _ln(x, g, b):
    mu = jnp.mean(x, axis=-1, keepdims=True)
    xc = x - mu
    var = jnp.mean(xc * xc, axis=-1, keepdims=True)
    return xc * lax.rsqrt(var + LN_EPS) * g + b


def f_ln_in(x, g, b):
    return (_ln(x, g, b),)


def f_deepnorm(x, y, g, b):
    return (_ln(ALPHA * x + y, g, b),)


def _rope(t, cos, sin):
    parts = []
    for h in range(LIN_HEADS):
        th = t[:, h * LIN_KEY_DIM:(h + 1) * LIN_KEY_DIM]
        parts.append(th * cos + swap_halves(th) * sin)
    return jnp.concatenate(parts, axis=1)


def f_ret_pre(rq, rk, cos, sin):
    return _rope(rq, cos, sin), _rope(rk, cos, sin) * LIN_KEY_DIM ** -0.5


def _log_sigmoid(x):
    return jnp.minimum(x, 0.0) - jnp.log(1.0 + jnp.exp(-jnp.abs(x)))


def f_gla_pre(gq, glr, wlr, blr):
    gate_pre = dot_nn(glr, wlr) + blr
    return gq * LIN_KEY_DIM ** -0.5, _log_sigmoid(gate_pre) / GLA_GATE_NORMALIZER


def f_normgate(ret_raw, rg, gla_raw, gg, gnorm):
    rets, glas = [], []
    for h in range(LIN_HEADS):
        sl = slice(h * LIN_VAL_DIM, (h + 1) * LIN_VAL_DIM)
        t = ret_raw[:, sl]
        mu = jnp.mean(t, axis=-1, keepdims=True)
        tc = t - mu
        rets.append(tc * lax.rsqrt(jnp.mean(tc * tc, axis=-1, keepdims=True) + LN_EPS))
        u = gla_raw[:, sl]
        glas.append(u * lax.rsqrt(jnp.mean(u * u, axis=-1, keepdims=True) + LN_EPS) * gnorm)
    ret = jax.nn.silu(rg) * jnp.concatenate(rets, axis=1)
    gla = jax.nn.silu(gg) * jnp.concatenate(glas, axis=1)
    return ret, gla


def f_merge(p0, p1, p2, gl):
    out = jax.nn.sigmoid(gl[:, :D_MODEL]) * p0
    out = out + jax.nn.sigmoid(gl[:, D_MODEL:2 * D_MODEL]) * p1
    return (out + jax.nn.sigmoid(gl[:, 2 * D_MODEL:]) * p2,)


def f_loss(y, tgt):
    diff = y - tgt
    sq = jnp.sum(diff * diff, axis=0, keepdims=True)
    lanes = sq[:, :128]
    for j in range(1, D_MODEL // 128):
        lanes = lanes + sq[:, j * 128:(j + 1) * 128]
    return diff * (1.0 / D_MODEL), lanes * (0.5 / D_MODEL)


def _adam(g, w, m, v):
    m = ADAM_B1 * m + (1.0 - ADAM_B1) * g
    v = ADAM_B2 * v + (1.0 - ADAM_B2) * jnp.square(g)
    m_hat = m / (1.0 - ADAM_B1 ** ADAM_STEP)
    v_hat = v / (1.0 - ADAM_B2 ** ADAM_STEP)
    delta = -ADAM_LR * (m_hat / (jnp.sqrt(v_hat) + ADAM_EPS) + ADAM_WD * w)
    return delta, m, v


def f_sum_adam(*args):
    g = args[0].astype(F32)
    for p in args[1:N_DEV]:
        g = g + p.astype(F32)
    return (g,) + _adam(g, *args[N_DEV:])


def f_adam(g, w, m, v):
    return _adam(g, w, m, v)


GROUP = 4
GROUP_ROWS = GROUP * CHUNK
GROUP_KEYS = (GROUP + ATTN_LEFT_CHUNKS) * CHUNK
DIAG_ROWS = REL_CLIP + CHUNK
GRAD_ROWS = 200


def _group_geometry():
    r = lax.broadcasted_iota(jnp.int32, (GROUP_ROWS, GROUP_KEYS), 0)
    c = lax.broadcasted_iota(jnp.int32, (GROUP_ROWS, GROUP_KEYS), 1)
    q_chunk = lax.shift_right_logical(r, 6)
    k_chunk = lax.shift_right_logical(c, 6)
    in_band = (k_chunk >= q_chunk) & (k_chunk <= q_chunk + ATTN_LEFT_CHUNKS)
    rel = jnp.minimum(r + PAD_ROWS - c, REL_CLIP) + REL_CLIP
    return rel, in_band, k_chunk


def bias_table(rel_bias_l):
    def body(rb_ref, o_ref):
        h = pl.program_id(0)
        rel, in_band, _ = _group_geometry()

        def step(r, acc):
            return jnp.where(rel == r, rb_ref[h * N_REL + r], acc)

        tab = lax.fori_loop(REL_MIN, N_REL, step, jnp.zeros((GROUP_ROWS, GROUP_KEYS), F32))
        o_ref[...] = jnp.where(in_band, tab, NEG_INF)

    return pl.pallas_call(
        body,
        name="bias_table",
        grid=(ATTN_HEADS,),
        in_specs=[pl.BlockSpec(memory_space=pltpu.SMEM)],
        out_specs=pl.BlockSpec((None, GROUP_ROWS, GROUP_KEYS), lambda h: (h, 0, 0)),
        out_shape=jax.ShapeDtypeStruct((ATTN_HEADS, GROUP_ROWS, GROUP_KEYS), F32),
        compiler_params=pltpu.CompilerParams(dimension_semantics=("arbitrary",)),
    )(rel_bias_l.reshape(ATTN_HEADS * N_REL))


def bias_table_grad(dbias):
    blocks = {}
    for qi in range(GROUP):
        for J in range(GROUP_KEYS // 128):
            blocks.setdefault(qi - 2 * J, []).append((qi, J))
    below_clip = range(-(CHUNK - 1), REL_CLIP)
    deltas = [d for d in sorted(blocks) if any(-127 <= pos - 64 * d - PAD_ROWS <= CHUNK - 1 for pos in below_clip)]

    def body(d_ref, o_ref, diag):
        sums = []
        for delta in deltas:
            acc = None
            for qi, J in blocks[delta]:
                blk = d_ref[pl.ds(qi * CHUNK, CHUNK), pl.ds(J * 128, 128)]
                acc = blk if acc is None else acc + blk
            sums.append(acc)
        off = lax.broadcasted_iota(jnp.int32, (CHUNK, 128), 0) - lax.broadcasted_iota(jnp.int32, (CHUNK, 128), 1)
        diag[...] = jnp.zeros_like(diag)

        def step(t, carry):
            pos = t - (CHUNK - 1)
            picked = jnp.zeros((CHUNK, 128), F32)
            for delta, s in zip(deltas, sums):
                picked = picked + jnp.where(off == pos - 64 * delta - PAD_ROWS, s, 0.0)
            diag[pl.ds(t, 1), :] = jnp.sum(picked, axis=0, keepdims=True)
            return carry

        lax.fori_loop(0, DIAG_ROWS - 1, step, 0)
        d = d_ref[...]
        total = jnp.sum(jnp.sum(d, axis=1, keepdims=True), axis=0, keepdims=True)
        o_ref[pl.ds(0, DIAG_ROWS), :] = jnp.broadcast_to(jnp.sum(diag[...], axis=1, keepdims=True), (DIAG_ROWS, 128))
        o_ref[pl.ds(DIAG_ROWS, GRAD_ROWS - DIAG_ROWS), :] = jnp.broadcast_to(total, (GRAD_ROWS - DIAG_ROWS, 128))

    out = pl.pallas_call(
        body,
        name="bias_table_grad",
        grid=(ATTN_HEADS,),
        in_specs=[pl.BlockSpec((None, GROUP_ROWS, GROUP_KEYS), lambda h: (h, 0, 0))],
        out_specs=pl.BlockSpec((None, GRAD_ROWS, 128), lambda h: (h, 0, 0)),
        out_shape=jax.ShapeDtypeStruct((ATTN_HEADS, GRAD_ROWS, 128), F32),
        scratch_shapes=[pltpu.VMEM((DIAG_ROWS, 128), F32)],
        compiler_params=pltpu.CompilerParams(dimension_semantics=("arbitrary",)),
    )(dbias)
    below = out[:, :DIAG_ROWS - 1, 0]
    clipped = out[:, DIAG_ROWS, 0] - jnp.sum(below, axis=1)
    return jnp.concatenate([jnp.zeros((ATTN_HEADS, REL_MIN), F32), below, clipped[:, None]], axis=1)


def _attn_stage(q_ref, k_ref, v_ref, qs, kp, vp, T):
    qs[...] = (q_ref[...] * ATTN_HEAD_DIM ** -0.5).astype(qs.dtype)
    for dst, src in ((kp, k_ref), (vp, v_ref)):
        dst[pl.ds(0, PAD_ROWS), :] = jnp.zeros((PAD_ROWS, ATTN_HEAD_DIM), dst.dtype)
        dst[pl.ds(PAD_ROWS, T), :] = src[...].astype(dst.dtype)


def _attn_probs(g, first_groups, qs, kp, b_ref):
    r0 = g * GROUP_ROWS if first_groups else pl.multiple_of(g * GROUP_ROWS, GROUP_ROWS)
    qc = qs[pl.ds(r0, GROUP_ROWS), :]
    kb = kp[pl.ds(r0, GROUP_KEYS), :]
    s = _mxu(qc, kb, _NT) + b_ref[...]
    if first_groups:
        k_chunk = lax.shift_right_logical(lax.broadcasted_iota(jnp.int32, (GROUP_ROWS, GROUP_KEYS), 1), 6)
        s = jnp.where(g * GROUP + k_chunk >= ATTN_LEFT_CHUNKS, s, NEG_INF)
    e = jnp.exp(s - jnp.max(s, axis=-1, keepdims=True))
    return r0, qc, kb, e / jnp.sum(e, axis=-1, keepdims=True)


def _for_groups(T, step):
    n_first = min(ATTN_LEFT_CHUNKS // GROUP, T // GROUP_ROWS)
    for g in range(n_first):
        step(g, True)
    lax.fori_loop(n_first, T // GROUP_ROWS, lambda g, c: (step(g, False), c)[1], 0)


def _head_spec(T, base):
    return pl.BlockSpec((T, ATTN_HEAD_DIM), lambda h: (0, base + h))


def _attn_scratch(T):
    return [pltpu.VMEM((T, ATTN_HEAD_DIM), MXU_DTYPE)] + [pltpu.VMEM((T + PAD_ROWS, ATTN_HEAD_DIM), MXU_DTYPE)] * 2


def attn_fwd(proj, bias_tab):
    T = proj.shape[0]

    def body(q_ref, k_ref, v_ref, b_ref, o_ref, qs, kp, vp):
        _attn_stage(q_ref, k_ref, v_ref, qs, kp, vp, T)

        def step(g, first):
            r0, _, _, p = _attn_probs(g, first, qs, kp, b_ref)
            o_ref[pl.ds(r0, GROUP_ROWS), :] = _mxu(p, vp[pl.ds(r0, GROUP_KEYS), :], _NN).astype(o_ref.dtype)

        _for_groups(T, step)

    return pl.pallas_call(
        body,
        name="attn_fwd",
        grid=(ATTN_HEADS,),
        in_specs=[_head_spec(T, COL["aq"] // 128), _head_spec(T, COL["ak"] // 128), _head_spec(T, COL["av"] // 128),
                  pl.BlockSpec((None, GROUP_ROWS, GROUP_KEYS), lambda h: (h, 0, 0))],
        out_specs=pl.BlockSpec((T, ATTN_HEAD_DIM), lambda h: (0, h)),
        out_shape=jax.ShapeDtypeStruct((T, ATTN_HEADS * ATTN_HEAD_DIM), MXU_DTYPE),
        scratch_shapes=_attn_scratch(T),
        compiler_params=pltpu.CompilerParams(dimension_semantics=("arbitrary",), vmem_limit_bytes=VMEM_BIG),
    )(proj, proj, proj, bias_tab)


def attn_bwd(proj, bias_tab, dout, dproj):
    T = proj.shape[0]

    def body(q_ref, k_ref, v_ref, b_ref, do_ref, dproj_in, dbias_ref, dproj_ref, qs, kp, vp, dkp, dvp, dq):
        del dproj_in
        h = pl.program_id(0)
        _attn_stage(q_ref, k_ref, v_ref, qs, kp, vp, T)
        dkp[...] = jnp.zeros_like(dkp)
        dvp[...] = jnp.zeros_like(dvp)
        dbias_ref[...] = jnp.zeros_like(dbias_ref)

        def step(g, first):
            r0, qc, kb, p = _attn_probs(g, first, qs, kp, b_ref)
            do = do_ref[pl.ds(r0, GROUP_ROWS), :]
            dvp[pl.ds(r0, GROUP_KEYS), :] += _mxu(p, do, _TN)
            dp = _mxu(do, vp[pl.ds(r0, GROUP_KEYS), :], _NT)
            ds = p * (dp - jnp.sum(dp * p, axis=-1, keepdims=True))
            dq[pl.ds(r0, GROUP_ROWS), :] = _mxu(ds, kb, _NN) * ATTN_HEAD_DIM ** -0.5
            dkp[pl.ds(r0, GROUP_KEYS), :] += _mxu(ds, qc, _TN)
            dbias_ref[...] += ds

        _for_groups(T, step)
        col = pl.multiple_of(h * ATTN_HEAD_DIM, ATTN_HEAD_DIM)
        pltpu.sync_copy(dq, dproj_ref.at[:, pl.ds(COL["aq"] + col, ATTN_HEAD_DIM)])
        pltpu.sync_copy(dkp.at[pl.ds(PAD_ROWS, T), :], dproj_ref.at[:, pl.ds(COL["ak"] + col, ATTN_HEAD_DIM)])
        pltpu.sync_copy(dvp.at[pl.ds(PAD_ROWS, T), :], dproj_ref.at[:, pl.ds(COL["av"] + col, ATTN_HEAD_DIM)])

    dbias, dproj = pl.pallas_call(
        body,
        name="attn_bwd",
        grid=(ATTN_HEADS,),
        in_specs=[_head_spec(T, COL["aq"] // 128), _head_spec(T, COL["ak"] // 128), _head_spec(T, COL["av"] // 128),
                  pl.BlockSpec((None, GROUP_ROWS, GROUP_KEYS), lambda h: (h, 0, 0)),
                  pl.BlockSpec((T, ATTN_HEAD_DIM), lambda h: (0, h)),
                  pl.BlockSpec(memory_space=pl.ANY)],
        out_specs=[pl.BlockSpec((None, GROUP_ROWS, GROUP_KEYS), lambda h: (h, 0, 0)), pl.BlockSpec(memory_space=pl.ANY)],
        out_shape=[jax.ShapeDtypeStruct((ATTN_HEADS, GROUP_ROWS, GROUP_KEYS), F32),
                   jax.ShapeDtypeStruct(dproj.shape, dproj.dtype)],
        scratch_shapes=_attn_scratch(T) + [pltpu.VMEM((T + PAD_ROWS, ATTN_HEAD_DIM), F32)] * 2
        + [pltpu.VMEM((T, ATTN_HEAD_DIM), F32)],
        input_output_aliases={5: 1},
        compiler_params=pltpu.CompilerParams(dimension_semantics=("arbitrary",), vmem_limit_bytes=VMEM_BIG),
    )(proj, proj, proj, bias_tab, dout, dproj)
    return dproj, dbias


def _lin_segment(q, k, v, la, st):
    lower = _tri(CHUNK, True)
    outs = []
    for i in range(SEG_CHUNKS):
        sl = slice(i * CHUNK, (i + 1) * CHUNK)
        qi, ki, vi, lai = q[sl], k[sl], v[sl], la[sl]
        cum = chunk_cumsum(lai)
        last = jnp.sum(lai, axis=0, keepdims=True)
        e_pos, e_neg = jnp.exp(cum), jnp.exp(-cum)
        q_fwd = qi * e_pos
        s_fwd = dot_nt(q_fwd, ki * e_neg)
        s_bwd = dot_nt(qi * e_neg, ki * e_pos)
        o = dot_nn(jnp.where(lower, s_fwd, s_bwd), vi) + dot_nt(q_fwd, st)
        st = jnp.exp(last) * st + dot_tn(vi, ki * jnp.exp(last - cum))
        outs.append(o)
    return jnp.concatenate(outs, axis=0), st


class Win:
    def __init__(self, arr, width, base=0):
        self.arr, self.width, self.base = arr, width, base


def _lin_spec(w, seg_of):
    base = w.base
    if w.arr.shape[0] == 1:
        return pl.BlockSpec((1, w.width), lambda h, s: (0, base + h))
    return pl.BlockSpec((SEG, w.width), lambda h, s: (seg_of(s), base + h))


def _decay_rows(la_ref):
    la = la_ref[...]
    return jnp.broadcast_to(la, (SEG, LIN_KEY_DIM)) if la.shape[0] == 1 else la


def lin_fwd(name, q, k, v, la):
    T = q.arr.shape[0]
    nseg = T // SEG

    def body(q_ref, k_ref, v_ref, la_ref, o_ref, st_ref, st):
        @pl.when(pl.program_id(1) == 0)
        def _():
            st[...] = jnp.zeros_like(st)

        st0 = st[...]
        st_ref[...] = st0
        o, st1 = _lin_segment(q_ref[...], k_ref[...], v_ref[...], _decay_rows(la_ref), st0)
        o_ref[...] = o
        st[...] = st1

    fwd = lambda s: s
    return pl.pallas_call(
        body,
        name=name,
        grid=(LIN_HEADS, nseg),
        in_specs=[_lin_spec(w, fwd) for w in (q, k, v, la)],
        out_specs=[pl.BlockSpec((SEG, LIN_VAL_DIM), lambda h, s: (s, h)),
                   pl.BlockSpec((None, None, LIN_VAL_DIM, LIN_KEY_DIM), lambda h, s: (h, s, 0, 0))],
        out_shape=[jax.ShapeDtypeStruct((T, LIN_HEADS * LIN_VAL_DIM), F32),
                   jax.ShapeDtypeStruct((LIN_HEADS, nseg, LIN_VAL_DIM, LIN_KEY_DIM), F32)],
        scratch_shapes=[pltpu.VMEM((LIN_VAL_DIM, LIN_KEY_DIM), F32)],
        compiler_params=pltpu.CompilerParams(dimension_semantics=("arbitrary", "arbitrary"), vmem_limit_bytes=VMEM_BIG),
    )(q.arr, k.arr, v.arr, la.arr)


def lin_bwd(name, q, k, v, la, states, dout, with_decay_grad):
    T = q.arr.shape[0]
    nseg = T // SEG
    n_out = 4 if with_decay_grad else 3

    def body(q_ref, k_ref, v_ref, la_ref, st_ref, do_ref, *rest):
        outs, dst = rest[:n_out], rest[n_out]

        @pl.when(pl.program_id(1) == 0)
        def _():
            dst[...] = jnp.zeros_like(dst)

        _, pull = jax.vjp(_lin_segment, q_ref[...].astype(F32), k_ref[...].astype(F32), v_ref[...].astype(F32),
                          _decay_rows(la_ref), st_ref[...])
        grads = pull((do_ref[...], dst[...]))
        for o, g in zip(outs, grads[:n_out]):
            o[...] = g
        dst[...] = grads[4]

    rev = lambda s: nseg - 1 - s
    key_spec = pl.BlockSpec((SEG, LIN_KEY_DIM), lambda h, s: (rev(s), h))
    val_spec = pl.BlockSpec((SEG, LIN_VAL_DIM), lambda h, s: (rev(s), h))
    out_specs = [key_spec, key_spec, val_spec] + ([key_spec] if with_decay_grad else [])
    widths = [LIN_KEY_DIM, LIN_KEY_DIM, LIN_VAL_DIM] + ([LIN_KEY_DIM] if with_decay_grad else [])
    return pl.pallas_call(
        body,
        name=name,
        grid=(LIN_HEADS, nseg),
        in_specs=[_lin_spec(w, rev) for w in (q, k, v, la)] + [
            pl.BlockSpec((None, None, LIN_VAL_DIM, LIN_KEY_DIM), lambda h, s: (h, rev(s), 0, 0)), val_spec],
        out_specs=out_specs,
        out_shape=[jax.ShapeDtypeStruct((T, LIN_HEADS * w), F32) for w in widths],
        scratch_shapes=[pltpu.VMEM((LIN_VAL_DIM, LIN_KEY_DIM), F32)],
        compiler_params=pltpu.CompilerParams(dimension_semantics=("arbitrary", "arbitrary"), vmem_limit_bytes=VMEM_BIG),
    )(q.arr, k.arr, v.arr, la.arr, states, dout)


def _coords():
    return lax.axis_index("x"), lax.axis_index("y"), lax.axis_index("c")


def _peer(k):
    x, y, c = _coords()
    px = 1 - x if k & 4 else x
    py = 1 - y if k & 2 else y
    pc = 1 - c if k & 1 else c
    return (px, py, pc), 4 * px + 2 * py + pc


def exchange(name, arrays, gather):
    n_arr = len(arrays)

    def body(*refs):
        ins, outs = refs[:n_arr], refs[n_arr:2 * n_arr]
        send_sems, recv_sems, own_sems = refs[2 * n_arr:]
        x, y, c = _coords()
        me = 4 * x + 2 * y + c
        copies = []
        for a in range(n_arr):
            own = pltpu.make_async_copy(ins[a] if gather else ins[a].at[me], outs[a].at[me], own_sems.at[a])
            own.start()
            copies.append(own)
        for k in range(1, N_DEV):
            peer, pidx = _peer(k)
            for a in range(n_arr):
                cp = pltpu.make_async_remote_copy(
                    src_ref=ins[a] if gather else ins[a].at[pidx], dst_ref=outs[a].at[me],
                    send_sem=send_sems.at[a, k - 1], recv_sem=recv_sems.at[a, k - 1],
                    device_id=peer, device_id_type=pl.DeviceIdType.MESH)
                cp.start()
                copies.append(cp)
        for cp in copies:
            cp.wait()

    out_shape = [jax.ShapeDtypeStruct(((N_DEV,) + a.shape) if gather else a.shape, a.dtype) for a in arrays]
    return pl.pallas_call(
        body,
        name=name,
        in_specs=[pl.BlockSpec(memory_space=pl.ANY)] * n_arr,
        out_specs=[pl.BlockSpec(memory_space=pl.ANY)] * n_arr,
        out_shape=out_shape,
        scratch_shapes=[pltpu.SemaphoreType.DMA((n_arr, N_DEV - 1)), pltpu.SemaphoreType.DMA((n_arr, N_DEV - 1)),
                        pltpu.SemaphoreType.DMA((n_arr,))],
    )(*arrays)


def all_reduce_small(vec):
    def body(v_ref, o_ref, land, send_sems, recv_sems):
        x, y, c = _coords()
        me = 4 * x + 2 * y + c
        land[me] = v_ref[...]
        copies = []
        for k in range(1, N_DEV):
            peer, _ = _peer(k)
            cp = pltpu.make_async_remote_copy(
                src_ref=v_ref, dst_ref=land.at[me], send_sem=send_sems.at[k - 1], recv_sem=recv_sems.at[k - 1],
                device_id=peer, device_id_type=pl.DeviceIdType.MESH)
            cp.start()
            copies.append(cp)
        for cp in copies:
            cp.wait()
        total = land[0]
        for s in range(1, N_DEV):
            total = total + land[s]
        o_ref[...] = total

    return pl.pallas_call(
        body,
        name="all_reduce_small",
        in_specs=[pl.BlockSpec(memory_space=pltpu.VMEM)],
        out_specs=pl.BlockSpec(memory_space=pltpu.VMEM),
        out_shape=jax.ShapeDtypeStruct(vec.shape, F32),
        scratch_shapes=[pltpu.VMEM((N_DEV,) + vec.shape, F32), pltpu.SemaphoreType.DMA((N_DEV - 1,)),
                        pltpu.SemaphoreType.DMA((N_DEV - 1,))],
    )(vec)


def _pack(parts):
    flat = jnp.concatenate([p.reshape(-1).astype(F32) for p in parts])
    n = flat.shape[0]
    rows = -(-n // 1024) * 8
    return jnp.pad(flat, (0, rows * 128 - n)).reshape(rows, 128)


def _unpack(packed, shapes):
    flat, out, pos = packed.reshape(-1), [], 0
    for s in shapes:
        n = math.prod(s)
        out.append(flat[pos:pos + n].reshape(s))
        pos += n
    return out


def _row2(v):
    return v.reshape(1, -1)


def _layer_forward(h, W, consts):
    T = h.shape[0]
    cos, sin, ret_decay = consts
    proj = mm("proj", h, W["in"], "nn", tm=1024, tn=512, tk=1024)
    bias_tab = bias_table(W["rel_bias"])
    attn = attn_fwd(proj, bias_tab)
    rq2, rk2 = row_call("ret_pre", f_ret_pre, [Row(proj, 512, COL["rq"] // 512), Row(proj, 512, COL["rk"] // 512),
                                              Row(cos), Row(sin)], [], [(512, F32), (512, F32)], tm=512)
    ret_raw, ret_st = lin_fwd("ret_fwd", Win(rq2, 128), Win(rk2, 128), Win(proj, 256, COL["rv"] // 256), Win(ret_decay, 128))
    gq2, gla_la = row_call("gla_pre", f_gla_pre, [Row(proj, 512, COL["gq"] // 512), Row(proj, GLR_BLOCK, GLR_OFF // GLR_BLOCK)],
                           [W["wlr"], W["blr"]], [(512, F32), (512, F32)], tm=512)
    gla_raw, gla_st = lin_fwd("gla_fwd", Win(gq2, 128), Win(proj, 128, COL["gk"] // 128), Win(proj, 256, COL["gv"] // 256),
                              Win(gla_la, 128))
    ret_b, gla_b = row_call("normgate", f_normgate, [Row(ret_raw), Row(proj, 1024, COL["rg"] // 1024), Row(gla_raw),
                                                    Row(proj, 1024, COL["gg"] // 1024)], [W["gnorm"]],
                            [(1024, MXU_DTYPE), (1024, MXU_DTYPE)], tm=256)
    branches = (attn, ret_b, gla_b)
    p = [mm("branch_proj", branches[n], W["branch"][n], "nn", tm=1024, tn=1024, tk=1024) for n in range(3)]
    merged, = row_call("merge", f_merge, [Row(p[0]), Row(p[1]), Row(p[2]), Row(proj, GATES_W, GATES_OFF // GATES_W)], [],
                       [(1024, MXU_DTYPE)], tm=256)
    mix = mm("out_proj", merged, W["out"], "nn", tm=1024, tn=1024, tk=1024)
    x1, = row_call("ln1", f_deepnorm, [Row(h), Row(mix)], [W["ln1_g"], W["ln1_b"]], [(1024, F32)], tm=512)
    u, act = mm("mlp_up", x1, W["up"], "nn", tm=1024, tn=512, tk=1024, n_out=2, out_dtypes=(F32, MXU_DTYPE),
                epi=lambda r: (r, jnp.square(jnp.maximum(r, 0.0))))
    y = mm("mlp_down", act, W["down"], "nn", tm=1024, tn=1024, tk=1024)
    x2, = row_call("ln2", f_deepnorm, [Row(x1), Row(y)], [W["ln2_g"], W["ln2_b"]], [(1024, F32)], tm=512)
    saved = dict(h=h, proj=proj, bias_tab=bias_tab, rq2=rq2, rk2=rk2, ret_st=ret_st, gq2=gq2, gla_la=gla_la, gla_st=gla_st,
                 ret_raw=ret_raw, gla_raw=gla_raw, branches=branches, p=p, merged=merged, mix=mix, x1=x1, u=u, act=act, y=y)
    return x2, saved


def _layer_backward(dx2, S, W, consts):
    cos, sin, ret_decay = consts
    h, proj, T = S["h"], S["proj"], S["h"].shape[0]
    add = lambda r, e: r + e
    dx1a, dy, dg2, db2 = row_call("ln2_bwd", vjp_fn(f_deepnorm, 2, 1), [Row(S["x1"]), Row(S["y"]), Row(dx2)],
                                  [W["ln2_g"], W["ln2_b"]], [(1024, F32), (1024, F32)], [(1, 1024), (1, 1024)], tm=256)
    du = mm("mlp_down_bwd", dy, W["down"], "nt", tm=1024, tn=512, tk=1024, extras=(S["u"],),
            epi=lambda r, u: r * (2.0 * jnp.maximum(u, 0.0)))
    g_down = mm("mlp_down_wgrad", S["act"], dy, "tn", tm=1024, tn=1024, tk=512, out_dtype=WIRE_DTYPE)
    dx1 = mm("mlp_up_bwd", du, W["up"], "nt", tm=512, tn=1024, tk=1024, extras=(dx1a,), epi=add)
    g_up = mm("mlp_up_wgrad", S["x1"], du, "tn", tm=1024, tn=1024, tk=512, out_dtype=WIRE_DTYPE)
    dha, dmix, dg1, db1 = row_call("ln1_bwd", vjp_fn(f_deepnorm, 2, 1), [Row(h), Row(S["mix"]), Row(dx1)],
                                   [W["ln1_g"], W["ln1_b"]], [(1024, F32), (1024, F32)], [(1, 1024), (1, 1024)], tm=256)
    dmerged = mm("out_proj_bwd", dmix, W["out"], "nt", tm=1024, tn=1024, tk=1024)
    g_out = mm("out_proj_wgrad", S["merged"], dmix, "tn", tm=1024, tn=1024, tk=512, out_dtype=WIRE_DTYPE)
    dproj = jax.ShapeDtypeStruct((T, PROJ_PAD), F32)
    p = S["p"]
    dp0, dp1, dp2, dproj = row_call(
        "merge_bwd", vjp_fn(f_merge, 4, 1), [Row(p[0]), Row(p[1]), Row(p[2]), Row(proj, GATES_W, GATES_OFF // GATES_W), Row(dmerged)],
        [], [(1024, F32)] * 3, tm=256, into=dproj, into_cols=[(GATES_W, GATES_OFF)])
    dps = (dp0, dp1, dp2)
    dbr = [mm("branch_proj_bwd", dps[n], W["branch"][n], "nt", tm=1024, tn=1024, tk=1024) for n in range(3)]
    g_branch = [mm("branch_proj_wgrad", S["branches"][n], dps[n], "tn", tm=1024, tn=1024, tk=512, out_dtype=WIRE_DTYPE)
                for n in range(3)]
    dret_raw, dgla_raw, dgnorm, dproj = row_call(
        "normgate_bwd", vjp_fn(lambda a, b, c, d, g: _reorder_normgate(a, b, c, d, g), 4, 2),
        [Row(S["ret_raw"]), Row(S["gla_raw"]), Row(proj, 1024, COL["rg"] // 1024), Row(proj, 1024, COL["gg"] // 1024),
         Row(dbr[1]), Row(dbr[2])], [W["gnorm"]], [(1024, F32), (1024, F32)], [(1, LIN_VAL_DIM)], tm=256,
        into=dproj, into_cols=[(1024, COL["rg"]), (1024, COL["gg"])])
    dgq2, dgk, dgv, dgla_la = lin_bwd("gla_bwd", Win(S["gq2"], 128), Win(proj, 128, COL["gk"] // 128),
                                      Win(proj, 256, COL["gv"] // 256), Win(S["gla_la"], 128), S["gla_st"], dgla_raw, True)
    dwlr, dblr, dproj = row_call(
        "gla_pre_bwd", _gla_pre_bwd, [Row(proj, 512, COL["gq"] // 512), Row(proj, GLR_BLOCK, GLR_OFF // GLR_BLOCK),
                                      Row(dgq2), Row(dgla_la), Row(dgk), Row(dgv)], [W["wlr"], W["blr"]], [],
        [(GLR_BLOCK, 512), (1, 512)], tm=256, into=dproj,
        into_cols=[(512, COL["gq"]), (PROJ_PAD - GLR_OFF, GLR_OFF), (512, COL["gk"]), (1024, COL["gv"])])
    drq2, drk2, drv = lin_bwd("ret_bwd", Win(S["rq2"], 128), Win(S["rk2"], 128), Win(proj, 256, COL["rv"] // 256),
                              Win(ret_decay, 128), S["ret_st"], dret_raw, False)
    dproj, = row_call(
        "ret_pre_bwd", _ret_pre_bwd, [Row(proj, 512, COL["rq"] // 512), Row(proj, 512, COL["rk"] // 512), Row(cos), Row(sin),
                                      Row(drq2), Row(drk2), Row(drv)], [], [], tm=256, into=dproj,
        into_cols=[(512, COL["rq"]), (512, COL["rk"]), (1024, COL["rv"])])
    dproj, dbias = attn_bwd(proj, S["bias_tab"], dbr[0], dproj)
    d_rel_bias = bias_table_grad(dbias)
    dh = mm("proj_bwd", dproj, W["in"], "nt", tm=512, tn=1024, tk=1280, extras=(dha,), epi=add)
    g_in = mm("proj_wgrad", h, dproj, "tn", tm=1024, tn=1280, tk=512, out_dtype=WIRE_DTYPE)
    big = dict(g_in=g_in, branch=g_branch, out=g_out, up=g_up, down=g_down)
    small = dict(rel_bias=d_rel_bias, wlr=dwlr[:GLA_RANK], blr=dblr, gnorm=dgnorm, ln1_g=dg1, ln1_b=db1, ln2_g=dg2, ln2_b=db2)
    return dh, big, small


def _reorder_normgate(ret_raw, gla_raw, rg, gg, gnorm):
    return f_normgate(ret_raw, rg, gla_raw, gg, gnorm)


def _gla_pre_bwd(gq, glr, dq2, dla, dgk, dgv, wlr, blr):
    _, pull = jax.vjp(f_gla_pre, gq, glr, wlr, blr)
    dgq, dglr, dwlr, dblr = pull((dq2, dla))
    dglr = jnp.concatenate([dglr, jnp.zeros((dglr.shape[0], PROJ_PAD - GLR_OFF - GLR_BLOCK), F32)], axis=1)
    return dgq, dglr, dgk, dgv, dwlr, dblr


def _ret_pre_bwd(rq, rk, cos, sin, drq2, drk2, drv):
    _, pull = jax.vjp(lambda a, b: f_ret_pre(a, b, cos, sin), rq, rk)
    drq, drk = pull((drq2, drk2))
    return drq, drk, drv


def _gathered_weights(w_in, gla_w_lr, w_branch, w_out, w_up, w_down):
    wire = lambda w: w.astype(WIRE_DTYPE)
    g_in, g_lr, g_br, g_out, g_up, g_down = exchange(
        "gather_weights", [wire(w_in), gla_w_lr, wire(w_branch), wire(w_out), wire(w_up), wire(w_down)], gather=True)
    layers = []
    for l in range(DEPTH):
        full_in = jnp.transpose(g_in[:, l], (1, 0, 2)).reshape(D_MODEL, PROJ_WIDTH)
        w_in_pad = jnp.concatenate(
            [full_in[:, :MAIN_W], full_in[:, GATES_SRC:], full_in[:, GLR_SRC:GATES_SRC],
             jnp.zeros((D_MODEL, PROJ_PAD - PROJ_WIDTH), WIRE_DTYPE)], axis=1)
        wlr = jnp.transpose(g_lr[:, l], (1, 0, 2)).reshape(GLA_RANK, LIN_HEADS * LIN_KEY_DIM)
        wlr_pad = jnp.pad(wlr, ((0, GLR_BLOCK - GLA_RANK), (0, 0)))
        layers.append({
            "in": w_in_pad, "wlr": wlr_pad,
            "branch": [g_br[:, l, n].reshape(D_MODEL, D_MODEL) for n in range(3)],
            "out": g_out[:, l].reshape(D_MODEL, D_MODEL),
            "up": jnp.transpose(g_up[:, l], (1, 0, 2)).reshape(D_MODEL, D_FF),
            "down": g_down[:, l].reshape(D_FF, D_MODEL),
        })
    return layers


def _slabs(big):
    g = big["g_in"]
    g = jnp.concatenate([g[:, :MAIN_W], g[:, GLR_OFF:GLR_OFF + GLA_RANK], g[:, GATES_OFF:GATES_OFF + GATES_W]], axis=1)
    return [
        jnp.transpose(g.reshape(D_MODEL, N_DEV, PROJ_WIDTH // N_DEV), (1, 0, 2)),
        jnp.stack([b.reshape(N_DEV, D_MODEL // N_DEV, D_MODEL) for b in big["branch"]], axis=1),
        big["out"].reshape(N_DEV, D_MODEL // N_DEV, D_MODEL),
        jnp.transpose(big["up"].reshape(D_MODEL, N_DEV, D_FF // N_DEV), (1, 0, 2)),
        big["down"].reshape(N_DEV, D_FF // N_DEV, D_MODEL),
    ]


def _sum_adam(name, parts, w, m, v, tm):
    C = w.shape[-1]
    rows = [Row(parts, lead=(s,)) for s in range(N_DEV)] + [Row(w), Row(m), Row(v)]
    return row_call(name, f_sum_adam, rows, [], [(C, F32)] * 4, tm=tm)


def kernel(x, ln_in_g, ln_in_b, w_in, rel_bias, gla_w_lr, gla_b_lr, gla_norm_g, w_branch, w_out, ln1_g, ln1_b, w_up, w_down, ln2_g, ln2_b, loss_target, m_ln_in_g, m_ln_in_b, m_w_in, m_rel_bias, m_gla_w_lr, m_gla_b_lr, m_gla_norm_g, m_w_branch, m_w_out, m_ln1_g, m_ln1_b, m_w_up, m_w_down, m_ln2_g, m_ln2_b, v_ln_in_g, v_ln_in_b, v_w_in, v_rel_bias, v_gla_w_lr, v_gla_b_lr, v_gla_norm_g, v_w_branch, v_w_out, v_ln1_g, v_ln1_b, v_w_up, v_w_down, v_ln2_g, v_ln2_b):
    T = x.shape[1]
    x2d = x.reshape(T, D_MODEL)
    tgt = loss_target.reshape(T, D_MODEL)
    me = 4 * lax.axis_index("x") + 2 * lax.axis_index("y") + lax.axis_index("c")

    inv = ROPE_BASE ** (-jnp.arange(0, LIN_KEY_DIM, 2, dtype=F32) / LIN_KEY_DIM)
    ang = jnp.arange(T, dtype=F32)[:, None] * inv[None, :]
    cos = jnp.concatenate([jnp.cos(ang), jnp.cos(ang)], axis=1)
    sin = jnp.concatenate([-jnp.sin(ang), jnp.sin(ang)], axis=1)
    log_gamma = jnp.log1p(-jnp.exp2(-5.0 - jnp.arange(LIN_HEADS, dtype=F32)))
    ret_decay = jnp.repeat(log_gamma, LIN_KEY_DIM).reshape(1, LIN_HEADS * LIN_KEY_DIM)
    consts = (cos, sin, ret_decay)

    layers = _gathered_weights(w_in, gla_w_lr, w_branch, w_out, w_up, w_down)
    for l in range(DEPTH):
        layers[l].update(rel_bias=rel_bias[l], blr=_row2(gla_b_lr[l]), gnorm=_row2(gla_norm_g[l]), ln1_g=_row2(ln1_g[l]),
                         ln1_b=_row2(ln1_b[l]), ln2_g=_row2(ln2_g[l]), ln2_b=_row2(ln2_b[l]))

    h, = row_call("ln_in", f_ln_in, [Row(x2d)], [_row2(ln_in_g), _row2(ln_in_b)], [(D_MODEL, F32)], tm=512)
    saved = []
    for l in range(DEPTH):
        h, s = _layer_forward(h, layers[l], consts)
        saved.append(s)
    dh, loss_lanes = row_call("loss", f_loss, [Row(h), Row(tgt)], [], [(D_MODEL, F32)], [(1, 128)], tm=512)

    small, received = [None] * DEPTH, [None] * DEPTH
    for l in reversed(range(DEPTH)):
        dh, big, small[l] = _layer_backward(dh, saved[l], layers[l], consts)
        received[l] = exchange("scatter_grads", _slabs(big), gather=False)
    grad_x, d_in_g, d_in_b = row_call("ln_in_bwd", vjp_fn(f_ln_in, 1, 1), [Row(x2d), Row(dh)],
                                      [_row2(ln_in_g), _row2(ln_in_b)], [(D_MODEL, F32)], [(1, D_MODEL), (1, D_MODEL)], tm=256)

    small_names = ("rel_bias", "wlr", "blr", "gnorm", "ln1_g", "ln1_b", "ln2_g", "ln2_b")
    parts = [d_in_g, d_in_b] + [small[l][n] for l in range(DEPTH) for n in small_names] + [loss_lanes]
    shapes = [(D_MODEL,), (D_MODEL,)] + [s for l in range(DEPTH) for s in (
        (ATTN_HEADS, N_REL), (GLA_RANK, 512), (512,), (LIN_VAL_DIM,), (D_MODEL,), (D_MODEL,), (D_MODEL,), (D_MODEL,))] + [(128,)]
    red = _unpack(all_reduce_small(_pack(parts)), shapes)
    loss = jnp.sum(red[-1])
    g_small = {"ln_in_g": red[0], "ln_in_b": red[1]}
    for l in range(DEPTH):
        for j, n in enumerate(small_names):
            g_small[(l, n)] = red[2 + l * len(small_names) + j]
    shard = LIN_HEADS * LIN_KEY_DIM // N_DEV
    g_wlr = jnp.stack([lax.dynamic_slice_in_dim(g_small[(l, "wlr")], me * shard, shard, axis=1) for l in range(DEPTH)])
    stack = lambda n: jnp.stack([g_small[(l, n)] for l in range(DEPTH)])
    g_rep = [g_small["ln_in_g"], g_small["ln_in_b"], stack("rel_bias"), g_wlr, stack("blr"), stack("gnorm"),
             stack("ln1_g"), stack("ln1_b"), stack("ln2_g"), stack("ln2_b")]
    w_rep = [ln_in_g, ln_in_b, rel_bias, gla_w_lr, gla_b_lr, gla_norm_g, ln1_g, ln1_b, ln2_g, ln2_b]
    m_rep = [m_ln_in_g, m_ln_in_b, m_rel_bias, m_gla_w_lr, m_gla_b_lr, m_gla_norm_g, m_ln1_g, m_ln1_b, m_ln2_g, m_ln2_b]
    v_rep = [v_ln_in_g, v_ln_in_b, v_rel_bias, v_gla_w_lr, v_gla_b_lr, v_gla_norm_g, v_ln1_g, v_ln1_b, v_ln2_g, v_ln2_b]
    rep_shapes = [w.shape for w in w_rep]
    d_pack, m_pack, v_pack = row_call("adam_small", f_adam, [Row(_pack(g_rep)), Row(_pack(w_rep)), Row(_pack(m_rep)), Row(_pack(v_rep))],
                                      [], [(128, F32)] * 3, tm=256)
    d_rep, nm_rep, nv_rep = _unpack(d_pack, rep_shapes), _unpack(m_pack, rep_shapes), _unpack(v_pack, rep_shapes)

    def owned(idx, w, m, v, name, tm):
        res = []
        for l in range(DEPTH):
            shp = w[l].shape
            flat = lambda a: a.reshape(-1, shp[-1])
            parts_l = received[l][idx].reshape(N_DEV, -1, shp[-1])
            res.append([o.reshape(shp) for o in _sum_adam(name, parts_l, flat(w[l]), flat(m[l]), flat(v[l]), tm)])
        return [jnp.stack([res[l][j] for l in range(DEPTH)]) for j in range(4)]

    o_in = owned(0, w_in, m_w_in, v_w_in, "adam_w_in", 256)
    o_br = owned(1, w_branch, m_w_branch, v_w_branch, "adam_w_branch", 128)
    o_out = owned(2, w_out, m_w_out, v_w_out, "adam_w_out", 128)
    o_up = owned(3, w_up, m_w_up, v_w_up, "adam_w_up", 256)
    o_down = owned(4, w_down, m_w_down, v_w_down, "adam_w_down", 256)

    def tree(j, rep):
        return [rep[0], rep[1], o_in[j], rep[2], rep[3], rep[4], rep[5], o_br[j], o_out[j], rep[6], rep[7], o_up[j], o_down[j],
                rep[8], rep[9]]

    return (loss, grad_x.reshape(1, T, D_MODEL), *tree(0, g_rep), *tree(1, d_rep), *tree(2, nm_rep), *tree(3, nv_rep))
```

```python
import functools
import math

import jax
import jax.numpy as jnp
from jax import lax
from jax.experimental import pallas as pl
from jax.experimental.pallas import tpu as pltpu

F32 = jnp.float32
MXU_DTYPE = jnp.bfloat16
WIRE_DTYPE = jnp.bfloat16

N_DEV = 8
D_MODEL = 1024
DEPTH = 2
CHUNK = 64
ATTN_HEADS = 8
ATTN_HEAD_DIM = 128
ATTN_LEFT_CHUNKS = 8
BAND = (ATTN_LEFT_CHUNKS + 1) * CHUNK
PAD_ROWS = ATTN_LEFT_CHUNKS * CHUNK
REL_CLIP = 2 * CHUNK
N_REL = 2 * REL_CLIP + 1
REL_MIN = REL_CLIP - (CHUNK - 1)
LIN_HEADS = 4
LIN_KEY_DIM = 128
LIN_VAL_DIM = 256
GLA_RANK = 16
GLA_GATE_NORMALIZER = 16.0
ROPE_BASE = 10000.0
D_FF = 4 * D_MODEL
LN_EPS = 1e-5
NEG_INF = -1e30
ALPHA = (2 * DEPTH) ** 0.25
ADAM_LR, ADAM_B1, ADAM_B2, ADAM_EPS, ADAM_WD, ADAM_STEP = 0.001, 0.9, 0.999, 1e-08, 0.01, 10

PROJ_WIDTH = 12304
GLR_SRC = 9216
GATES_SRC = GLR_SRC + GLA_RANK
MAIN_W = 9216
GATES_W = 3 * D_MODEL
GATES_OFF = MAIN_W
GLR_OFF = MAIN_W + GATES_W
PROJ_PAD = 12800
GLR_BLOCK = 128
COL = dict(aq=0, ak=1024, av=2048, rq=3072, rk=3584, rv=4096, rg=5120, gq=6144, gk=6656, gv=7168, gg=8192)

SEG_CHUNKS = 8
SEG = SEG_CHUNKS * CHUNK
VMEM_BIG = 56 * 1024 * 1024

_NN = (((1,), (0,)), ((), ()))
_NT = (((1,), (1,)), ((), ()))
_TN = (((0,), (0,)), ((), ()))


def _mxu(a, b, dn):
    return lax.dot_general(a.astype(MXU_DTYPE), b.astype(MXU_DTYPE), dn, preferred_element_type=F32)


@jax.custom_vjp
def dot_nn(a, b):
    return _mxu(a, b, _NN)


def _dot_nn_fwd(a, b):
    return _mxu(a, b, _NN), (a, b)


def _dot_nn_bwd(res, g):
    a, b = res
    return _mxu(g, b, _NT).astype(a.dtype), _mxu(a, g, _TN).astype(b.dtype)


dot_nn.defvjp(_dot_nn_fwd, _dot_nn_bwd)


@jax.custom_vjp
def dot_nt(a, b):
    return _mxu(a, b, _NT)


def _dot_nt_fwd(a, b):
    return _mxu(a, b, _NT), (a, b)


def _dot_nt_bwd(res, g):
    a, b = res
    return _mxu(g, b, _NN).astype(a.dtype), _mxu(g, a, _TN).astype(b.dtype)


dot_nt.defvjp(_dot_nt_fwd, _dot_nt_bwd)


@jax.custom_vjp
def dot_tn(a, b):
    return _mxu(a, b, _TN)


def _dot_tn_fwd(a, b):
    return _mxu(a, b, _TN), (a, b)


def _dot_tn_bwd(res, g):
    a, b = res
    return _mxu(b, g, _NT).astype(a.dtype), _mxu(a, g, _NN).astype(b.dtype)


dot_tn.defvjp(_dot_tn_fwd, _dot_tn_bwd)


def _tri(n, lower):
    r = lax.broadcasted_iota(jnp.int32, (n, n), 0)
    c = lax.broadcasted_iota(jnp.int32, (n, n), 1)
    return (r >= c) if lower else (r <= c)


def _f32_dot(a, b):
    return lax.dot_general(a, b, _NN, precision=lax.Precision.HIGHEST, preferred_element_type=F32)


@jax.custom_vjp
def chunk_cumsum(x):
    return _f32_dot(_tri(x.shape[0], True).astype(F32), x)


def _chunk_cumsum_fwd(x):
    return chunk_cumsum(x), None


def _chunk_cumsum_bwd(_, g):
    return (_f32_dot(_tri(g.shape[0], False).astype(F32), g),)


chunk_cumsum.defvjp(_chunk_cumsum_fwd, _chunk_cumsum_bwd)


@jax.custom_vjp
def swap_halves(t):
    return pltpu.roll(t, t.shape[-1] // 2, t.ndim - 1)


def _swap_fwd(t):
    return swap_halves(t), None


def _swap_bwd(_, g):
    return (swap_halves(g),)


swap_halves.defvjp(_swap_fwd, _swap_bwd)


def mm(name, a, b, dims, *, tm, tn, tk, out_dtype=F32, extras=(), epi=None, n_out=1, out_dtypes=None, side=None):
    if dims == "nn":
        (M, K), N = a.shape, b.shape[1]
    elif dims == "nt":
        (M, K), N = a.shape, b.shape[0]
    else:
        (K, M), N = a.shape, b.shape[1]
    tm, tn, tk = min(tm, M), min(tn, N), min(tk, K)
    assert M % tm == 0 and N % tn == 0 and K % tk == 0, (name, M, N, K, tm, tn, tk)
    nk = K // tk
    dn = {"nn": _NN, "nt": _NT, "tn": _TN}[dims]
    a_spec = pl.BlockSpec((tk, tm), lambda i, j, k: (k, i)) if dims == "tn" else pl.BlockSpec((tm, tk), lambda i, j, k: (i, k))
    b_spec = pl.BlockSpec((tn, tk), lambda i, j, k: (j, k)) if dims == "nt" else pl.BlockSpec((tk, tn), lambda i, j, k: (k, j))
    o_spec = pl.BlockSpec((tm, tn), lambda i, j, k: (i, j))
    n_ex = len(extras)
    out_dtypes = out_dtypes or (out_dtype,) * n_out

    def finish(r, ex_refs, o_refs):
        if epi is not None:
            r = epi(r, *[e[...] for e in ex_refs])
        r = r if isinstance(r, tuple) else (r,)
        for o, v in zip(o_refs, r):
            o[...] = v.astype(o.dtype)

    def body(*refs):
        a_ref, b_ref = refs[:2]
        ex_refs = refs[2:2 + n_ex]
        o_refs = refs[2 + n_ex:2 + n_ex + n_out]
        part = _mxu(a_ref[...], b_ref[...], dn)
        if nk == 1:
            finish(part, ex_refs, o_refs)
            return
        acc = refs[-1]
        k = pl.program_id(2)

        @pl.when(k == 0)
        def _():
            acc[...] = part

        @pl.when(k > 0)
        def _():
            acc[...] += part

        @pl.when(k == nk - 1)
        def _():
            finish(acc[...], ex_refs, o_refs)

    outs = _side_call(
        side, body,
        name=name,
        grid=(M // tm, N // tn, nk),
        in_specs=[a_spec, b_spec] + [o_spec] * n_ex,
        out_specs=[o_spec] * n_out,
        out_shape=[jax.ShapeDtypeStruct((M, N), dt) for dt in out_dtypes],
        scratch_shapes=[pltpu.VMEM((tm, tn), F32)] if nk > 1 else [],
        args=[a, b, *extras],
        semantics=("parallel", "parallel", "arbitrary"),
    )
    return outs[0] if n_out == 1 else outs


class Row:
    def __init__(self, arr, width=None, col=0, lead=()):
        self.arr, self.lead, self.col = arr, tuple(lead), col
        self.width = arr.shape[-1] if width is None else width

    def spec(self, tm):
        lead, col = self.lead, self.col
        return pl.BlockSpec((None,) * len(lead) + (tm, self.width), lambda i: (*lead, i, col))


def row_call(name, fn, rows, params, new_outs, red_shapes=(), *, tm, into=None, into_cols=()):
    R = rows[0].arr.shape[len(rows[0].lead)]
    tm = min(tm, R)
    assert R % tm == 0, (name, R, tm)
    n_row, n_par, n_new, n_into, n_red = len(rows), len(params), len(new_outs), len(into_cols), len(red_shapes)
    fresh = isinstance(into, jax.ShapeDtypeStruct)

    def body(*refs):
        row_refs = refs[:n_row]
        par_refs = refs[n_row:n_row + n_par]
        pos = n_row + n_par + (1 if (into is not None and not fresh) else 0)
        new_refs = refs[pos:pos + n_new]
        red_refs = refs[pos + n_new:pos + n_new + n_red]
        pos2 = pos + n_new + n_red
        into_ref = refs[pos2] if into is not None else None
        stage = refs[pos2 + 1:] if into is not None else ()
        i = pl.program_id(0)
        vals = fn(*[r[...] for r in row_refs], *[p[...] for p in par_refs])
        vals = vals if isinstance(vals, (tuple, list)) else (vals,)
        for o, v in zip(new_refs, vals[:n_new]):
            o[...] = v.astype(o.dtype)
        for (width, off), buf, v in zip(into_cols, stage, vals[n_new:n_new + n_into]):
            buf[...] = v.astype(buf.dtype)
            pltpu.sync_copy(buf, into_ref.at[pl.ds(pl.multiple_of(i * tm, tm), tm), pl.ds(off, width)])
        for o, v in zip(red_refs, vals[n_new + n_into:]):
            @pl.when(i == 0)
            def _(o=o, v=v):
                o[...] = v.astype(o.dtype)

            @pl.when(i > 0)
            def _(o=o, v=v):
                o[...] += v.astype(o.dtype)

    in_specs = [r.spec(tm) for r in rows] + [pl.BlockSpec(p.shape, lambda i, nd=p.ndim: (0,) * nd) for p in params]
    args = [r.arr for r in rows] + list(params)
    out_specs = [pl.BlockSpec((tm, w), lambda i: (i, 0)) for w, _ in new_outs]
    out_shape = [jax.ShapeDtypeStruct((R, w), dt) for w, dt in new_outs]
    out_specs += [pl.BlockSpec(s, lambda i, nd=len(s): (0,) * nd) for s in red_shapes]
    out_shape += [jax.ShapeDtypeStruct(s, F32) for s in red_shapes]
    aliases, scratch = {}, []
    if into is not None:
        out_specs.append(pl.BlockSpec(memory_space=pl.ANY))
        out_shape.append(jax.ShapeDtypeStruct(into.shape, into.dtype))
        if not fresh:
            in_specs.append(pl.BlockSpec(memory_space=pl.ANY))
            args.append(into)
            aliases = {len(args) - 1: len(out_shape) - 1}
        scratch = [pltpu.VMEM((tm, w), into.dtype) for w, _ in into_cols]
    outs = pl.pallas_call(
        body,
        name=name,
        grid=(R // tm,),
        in_specs=in_specs,
        out_specs=out_specs,
        out_shape=out_shape,
        scratch_shapes=scratch,
        input_output_aliases=aliases,
        compiler_params=pltpu.CompilerParams(dimension_semantics=("arbitrary",), vmem_limit_bytes=VMEM_BIG),
    )(*args)
    return list(outs)


def vjp_fn(fn, n_row, n_out, passthrough=0):
    def bwd(*args):
        rows = [a.astype(F32) for a in args[:n_row]]
        cts = args[n_row:n_row + n_out]
        thru = args[n_row + n_out:n_row + n_out + passthrough]
        pars = args[n_row + n_out + passthrough:]
        outs, pull = jax.vjp(lambda *xs: tuple(fn(*xs)), *rows, *pars)
        grads = pull(tuple(c.astype(o.dtype) for c, o in zip(cts, outs)))
        return tuple(grads[:n_row]) + tuple(thru) + tuple(grads[n_row:])

    return bwd


def _ln(x, g, b):
    mu = jnp.mean(x, axis=-1, keepdims=True)
    xc = x - mu
    var = jnp.mean(xc * xc, axis=-1, keepdims=True)
    return xc * lax.rsqrt(var + LN_EPS) * g + b


def f_ln_in(x, g, b):
    return (_ln(x, g, b),)


def f_deepnorm(x, y, g, b):
    return (_ln(ALPHA * x + y, g, b),)


def _rope(t, cos, sin):
    parts = []
    for h in range(LIN_HEADS):
        th = t[:, h * LIN_KEY_DIM:(h + 1) * LIN_KEY_DIM]
        parts.append(th * cos + swap_halves(th) * sin)
    return jnp.concatenate(parts, axis=1)


def f_ret_pre(rq, rk, cos, sin):
    return _rope(rq, cos, sin), _rope(rk, cos, sin) * LIN_KEY_DIM ** -0.5


def _log_sigmoid(x):
    return jnp.minimum(x, 0.0) - jnp.log(1.0 + jnp.exp(-jnp.abs(x)))


def f_gla_pre(gq, glr, wlr, blr):
    gate_pre = dot_nn(glr, wlr) + blr
    return gq * LIN_KEY_DIM ** -0.5, _log_sigmoid(gate_pre) / GLA_GATE_NORMALIZER


def f_normgate(ret_raw, rg, gla_raw, gg, gnorm):
    rets, glas = [], []
    for h in range(LIN_HEADS):
        sl = slice(h * LIN_VAL_DIM, (h + 1) * LIN_VAL_DIM)
        t = ret_raw[:, sl]
        mu = jnp.mean(t, axis=-1, keepdims=True)
        tc = t - mu
        rets.append(tc * lax.rsqrt(jnp.mean(tc * tc, axis=-1, keepdims=True) + LN_EPS))
        u = gla_raw[:, sl]
        glas.append(u * lax.rsqrt(jnp.mean(u * u, axis=-1, keepdims=True) + LN_EPS) * gnorm)
    ret = jax.nn.silu(rg) * jnp.concatenate(rets, axis=1)
    gla = jax.nn.silu(gg) * jnp.concatenate(glas, axis=1)
    return ret, gla


def f_merge(p0, p1, p2, gl):
    out = jax.nn.sigmoid(gl[:, :D_MODEL]) * p0
    out = out + jax.nn.sigmoid(gl[:, D_MODEL:2 * D_MODEL]) * p1
    return (out + jax.nn.sigmoid(gl[:, 2 * D_MODEL:]) * p2,)


def f_loss(y, tgt):
    diff = y - tgt
    sq = jnp.sum(diff * diff, axis=0, keepdims=True)
    lanes = sq[:, :128]
    for j in range(1, D_MODEL // 128):
        lanes = lanes + sq[:, j * 128:(j + 1) * 128]
    return diff * (1.0 / D_MODEL), lanes * (0.5 / D_MODEL)


def _adam(g, w, m, v):
    m = ADAM_B1 * m + (1.0 - ADAM_B1) * g
    v = ADAM_B2 * v + (1.0 - ADAM_B2) * jnp.square(g)
    m_hat = m / (1.0 - ADAM_B1 ** ADAM_STEP)
    v_hat = v / (1.0 - ADAM_B2 ** ADAM_STEP)
    delta = -ADAM_LR * (m_hat / (jnp.sqrt(v_hat) + ADAM_EPS) + ADAM_WD * w)
    return delta, m, v


def f_sum_adam(*args):
    g = args[0].astype(F32)
    for p in args[1:-3]:
        g = g + p.astype(F32)
    return (g,) + _adam(g, *args[-3:])


def f_adam(g, w, m, v):
    return _adam(g, w, m, v)


GROUP = 4
GROUP_ROWS = GROUP * CHUNK
GROUP_KEYS = (GROUP + ATTN_LEFT_CHUNKS) * CHUNK
DIAG_ROWS = REL_CLIP + CHUNK
GRAD_ROWS = 200


def _group_geometry():
    r = lax.broadcasted_iota(jnp.int32, (GROUP_ROWS, GROUP_KEYS), 0)
    c = lax.broadcasted_iota(jnp.int32, (GROUP_ROWS, GROUP_KEYS), 1)
    q_chunk = lax.shift_right_logical(r, 6)
    k_chunk = lax.shift_right_logical(c, 6)
    in_band = (k_chunk >= q_chunk) & (k_chunk <= q_chunk + ATTN_LEFT_CHUNKS)
    rel = jnp.minimum(r + PAD_ROWS - c, REL_CLIP) + REL_CLIP
    return rel, in_band, k_chunk


def bias_table(rel_bias_l):
    def body(rb_ref, o_ref):
        h = pl.program_id(0)
        rel, in_band, _ = _group_geometry()

        def step(r, acc):
            return jnp.where(rel == r, rb_ref[h * N_REL + r], acc)

        tab = lax.fori_loop(REL_MIN, N_REL, step, jnp.zeros((GROUP_ROWS, GROUP_KEYS), F32))
        o_ref[...] = jnp.where(in_band, tab, NEG_INF)

    return pl.pallas_call(
        body,
        name="bias_table",
        grid=(ATTN_HEADS,),
        in_specs=[pl.BlockSpec(memory_space=pltpu.SMEM)],
        out_specs=pl.BlockSpec((None, GROUP_ROWS, GROUP_KEYS), lambda h: (h, 0, 0)),
        out_shape=jax.ShapeDtypeStruct((ATTN_HEADS, GROUP_ROWS, GROUP_KEYS), F32),
        compiler_params=pltpu.CompilerParams(dimension_semantics=("arbitrary",)),
    )(rel_bias_l.reshape(ATTN_HEADS * N_REL))


def bias_table_grad(dbias):
    blocks = {}
    for qi in range(GROUP):
        for J in range(GROUP_KEYS // 128):
            blocks.setdefault(qi - 2 * J, []).append((qi, J))
    below_clip = range(-(CHUNK - 1), REL_CLIP)
    deltas = [d for d in sorted(blocks) if any(-127 <= pos - 64 * d - PAD_ROWS <= CHUNK - 1 for pos in below_clip)]

    def body(d_ref, o_ref, diag):
        sums = []
        for delta in deltas:
            acc = None
            for qi, J in blocks[delta]:
                blk = d_ref[pl.ds(qi * CHUNK, CHUNK), pl.ds(J * 128, 128)]
                acc = blk if acc is None else acc + blk
            sums.append(acc)
        off = lax.broadcasted_iota(jnp.int32, (CHUNK, 128), 0) - lax.broadcasted_iota(jnp.int32, (CHUNK, 128), 1)
        diag[...] = jnp.zeros_like(diag)

        def step(t, carry):
            pos = t - (CHUNK - 1)
            picked = jnp.zeros((CHUNK, 128), F32)
            for delta, s in zip(deltas, sums):
                picked = picked + jnp.where(off == pos - 64 * delta - PAD_ROWS, s, 0.0)
            diag[pl.ds(t, 1), :] = jnp.sum(picked, axis=0, keepdims=True)
            return carry

        lax.fori_loop(0, DIAG_ROWS - 1, step, 0)
        d = d_ref[...]
        total = jnp.sum(jnp.sum(d, axis=1, keepdims=True), axis=0, keepdims=True)
        o_ref[pl.ds(0, DIAG_ROWS), :] = jnp.broadcast_to(jnp.sum(diag[...], axis=1, keepdims=True), (DIAG_ROWS, 128))
        o_ref[pl.ds(DIAG_ROWS, GRAD_ROWS - DIAG_ROWS), :] = jnp.broadcast_to(total, (GRAD_ROWS - DIAG_ROWS, 128))

    out = pl.pallas_call(
        body,
        name="bias_table_grad",
        grid=(ATTN_HEADS,),
        in_specs=[pl.BlockSpec((None, GROUP_ROWS, GROUP_KEYS), lambda h: (h, 0, 0))],
        out_specs=pl.BlockSpec((None, GRAD_ROWS, 128), lambda h: (h, 0, 0)),
        out_shape=jax.ShapeDtypeStruct((ATTN_HEADS, GRAD_ROWS, 128), F32),
        scratch_shapes=[pltpu.VMEM((DIAG_ROWS, 128), F32)],
        compiler_params=pltpu.CompilerParams(dimension_semantics=("arbitrary",)),
    )(dbias)
    below = out[:, :DIAG_ROWS - 1, 0]
    clipped = out[:, DIAG_ROWS, 0] - jnp.sum(below, axis=1)
    return jnp.concatenate([jnp.zeros((ATTN_HEADS, REL_MIN), F32), below, clipped[:, None]], axis=1)


def _attn_stage(q_ref, k_ref, v_ref, qs, kp, vp, T):
    qs[...] = (q_ref[...] * ATTN_HEAD_DIM ** -0.5).astype(qs.dtype)
    for dst, src in ((kp, k_ref), (vp, v_ref)):
        dst[pl.ds(0, PAD_ROWS), :] = jnp.zeros((PAD_ROWS, ATTN_HEAD_DIM), dst.dtype)
        dst[pl.ds(PAD_ROWS, T), :] = src[...].astype(dst.dtype)


def _attn_probs(g, first_groups, qs, kp, b_ref):
    r0 = g * GROUP_ROWS if first_groups else pl.multiple_of(g * GROUP_ROWS, GROUP_ROWS)
    qc = qs[pl.ds(r0, GROUP_ROWS), :]
    kb = kp[pl.ds(r0, GROUP_KEYS), :]
    s = _mxu(qc, kb, _NT) + b_ref[...]
    if first_groups:
        k_chunk = lax.shift_right_logical(lax.broadcasted_iota(jnp.int32, (GROUP_ROWS, GROUP_KEYS), 1), 6)
        s = jnp.where(g * GROUP + k_chunk >= ATTN_LEFT_CHUNKS, s, NEG_INF)
    e = jnp.exp(s - jnp.max(s, axis=-1, keepdims=True))
    return r0, qc, kb, e / jnp.sum(e, axis=-1, keepdims=True)


def _for_groups(T, step):
    n_first = min(ATTN_LEFT_CHUNKS // GROUP, T // GROUP_ROWS)
    for g in range(n_first):
        step(g, True)
    lax.fori_loop(n_first, T // GROUP_ROWS, lambda g, c: (step(g, False), c)[1], 0)


def _head_spec(T, base):
    return pl.BlockSpec((T, ATTN_HEAD_DIM), lambda h: (0, base + h))


def _attn_scratch(T):
    return [pltpu.VMEM((T, ATTN_HEAD_DIM), MXU_DTYPE)] + [pltpu.VMEM((T + PAD_ROWS, ATTN_HEAD_DIM), MXU_DTYPE)] * 2


def attn_fwd(proj, bias_tab, side=None):
    T = proj.shape[0]

    def body(q_ref, k_ref, v_ref, b_ref, o_ref, qs, kp, vp):
        _attn_stage(q_ref, k_ref, v_ref, qs, kp, vp, T)

        def step(g, first):
            r0, _, _, p = _attn_probs(g, first, qs, kp, b_ref)
            o_ref[pl.ds(r0, GROUP_ROWS), :] = _mxu(p, vp[pl.ds(r0, GROUP_KEYS), :], _NN).astype(o_ref.dtype)

        _for_groups(T, step)

    out, = _side_call(
        side, body,
        name="attn_fwd",
        grid=(ATTN_HEADS,),
        in_specs=[_head_spec(T, COL["aq"] // 128), _head_spec(T, COL["ak"] // 128), _head_spec(T, COL["av"] // 128),
                  pl.BlockSpec((None, GROUP_ROWS, GROUP_KEYS), lambda h: (h, 0, 0))],
        out_specs=[pl.BlockSpec((T, ATTN_HEAD_DIM), lambda h: (0, h))],
        out_shape=[jax.ShapeDtypeStruct((T, ATTN_HEADS * ATTN_HEAD_DIM), MXU_DTYPE)],
        scratch_shapes=_attn_scratch(T),
        args=[proj, proj, proj, bias_tab],
    )
    return out


def attn_bwd(proj, bias_tab, dout, dproj, side=None):
    T = proj.shape[0]

    def body(q_ref, k_ref, v_ref, b_ref, do_ref, dproj_in, dbias_ref, dproj_ref, qs, kp, vp, dkp, dvp, dq):
        del dproj_in
        h = pl.program_id(0)
        _attn_stage(q_ref, k_ref, v_ref, qs, kp, vp, T)
        dkp[...] = jnp.zeros_like(dkp)
        dvp[...] = jnp.zeros_like(dvp)
        dbias_ref[...] = jnp.zeros_like(dbias_ref)

        def step(g, first):
            r0, qc, kb, p = _attn_probs(g, first, qs, kp, b_ref)
            do = do_ref[pl.ds(r0, GROUP_ROWS), :]
            dvp[pl.ds(r0, GROUP_KEYS), :] += _mxu(p, do, _TN)
            dp = _mxu(do, vp[pl.ds(r0, GROUP_KEYS), :], _NT)
            ds = p * (dp - jnp.sum(dp * p, axis=-1, keepdims=True))
            dq[pl.ds(r0, GROUP_ROWS), :] = _mxu(ds, kb, _NN) * ATTN_HEAD_DIM ** -0.5
            dkp[pl.ds(r0, GROUP_KEYS), :] += _mxu(ds, qc, _TN)
            dbias_ref[...] += ds

        _for_groups(T, step)
        col = pl.multiple_of(h * ATTN_HEAD_DIM, ATTN_HEAD_DIM)
        pltpu.sync_copy(dq, dproj_ref.at[:, pl.ds(COL["aq"] + col, ATTN_HEAD_DIM)])
        pltpu.sync_copy(dkp.at[pl.ds(PAD_ROWS, T), :], dproj_ref.at[:, pl.ds(COL["ak"] + col, ATTN_HEAD_DIM)])
        pltpu.sync_copy(dvp.at[pl.ds(PAD_ROWS, T), :], dproj_ref.at[:, pl.ds(COL["av"] + col, ATTN_HEAD_DIM)])

    dbias, dproj = _side_call(
        side, body,
        name="attn_bwd",
        grid=(ATTN_HEADS,),
        in_specs=[_head_spec(T, COL["aq"] // 128), _head_spec(T, COL["ak"] // 128), _head_spec(T, COL["av"] // 128),
                  pl.BlockSpec((None, GROUP_ROWS, GROUP_KEYS), lambda h: (h, 0, 0)),
                  pl.BlockSpec((T, ATTN_HEAD_DIM), lambda h: (0, h)),
                  pl.BlockSpec(memory_space=pl.ANY)],
        out_specs=[pl.BlockSpec((None, GROUP_ROWS, GROUP_KEYS), lambda h: (h, 0, 0)), pl.BlockSpec(memory_space=pl.ANY)],
        out_shape=[jax.ShapeDtypeStruct((ATTN_HEADS, GROUP_ROWS, GROUP_KEYS), F32),
                   jax.ShapeDtypeStruct(dproj.shape, dproj.dtype)],
        scratch_shapes=_attn_scratch(T) + [pltpu.VMEM((T + PAD_ROWS, ATTN_HEAD_DIM), F32)] * 2
        + [pltpu.VMEM((T, ATTN_HEAD_DIM), F32)],
        args=[proj, proj, proj, bias_tab, dout, dproj],
        aliases={5: 1},
    )
    return dproj, dbias


def _lin_segment(q, k, v, la, st):
    lower = _tri(CHUNK, True)
    outs = []
    for i in range(SEG_CHUNKS):
        sl = slice(i * CHUNK, (i + 1) * CHUNK)
        qi, ki, vi, lai = q[sl], k[sl], v[sl], la[sl]
        cum = chunk_cumsum(lai)
        last = jnp.sum(lai, axis=0, keepdims=True)
        e_pos, e_neg = jnp.exp(cum), jnp.exp(-cum)
        q_fwd = qi * e_pos
        s_fwd = dot_nt(q_fwd, ki * e_neg)
        s_bwd = dot_nt(qi * e_neg, ki * e_pos)
        o = dot_nn(jnp.where(lower, s_fwd, s_bwd), vi) + dot_nt(q_fwd, st)
        st = jnp.exp(last) * st + dot_tn(vi, ki * jnp.exp(last - cum))
        outs.append(o)
    return jnp.concatenate(outs, axis=0), st


class Win:
    def __init__(self, arr, width, base=0):
        self.arr, self.width, self.base = arr, width, base


def _lin_spec(w, seg_of):
    base = w.base
    if w.arr.shape[0] == 1:
        return pl.BlockSpec((1, w.width), lambda h, s: (0, base + h))
    return pl.BlockSpec((SEG, w.width), lambda h, s: (seg_of(s), base + h))


def _decay_rows(la_ref):
    la = la_ref[...]
    return jnp.broadcast_to(la, (SEG, LIN_KEY_DIM)) if la.shape[0] == 1 else la


def lin_fwd(name, q, k, v, la, side=None):
    T = q.arr.shape[0]
    nseg = T // SEG

    def body(q_ref, k_ref, v_ref, la_ref, o_ref, st_ref, st):
        @pl.when(pl.program_id(1) == 0)
        def _():
            st[...] = jnp.zeros_like(st)

        st0 = st[...]
        st_ref[...] = st0
        o, st1 = _lin_segment(q_ref[...], k_ref[...], v_ref[...], _decay_rows(la_ref), st0)
        o_ref[...] = o
        st[...] = st1

    fwd = lambda s: s
    out, states = _side_call(
        side, body,
        name=name,
        grid=(LIN_HEADS, nseg),
        in_specs=[_lin_spec(w, fwd) for w in (q, k, v, la)],
        out_specs=[pl.BlockSpec((SEG, LIN_VAL_DIM), lambda h, s: (s, h)),
                   pl.BlockSpec((None, None, LIN_VAL_DIM, LIN_KEY_DIM), lambda h, s: (h, s, 0, 0))],
        out_shape=[jax.ShapeDtypeStruct((T, LIN_HEADS * LIN_VAL_DIM), F32),
                   jax.ShapeDtypeStruct((LIN_HEADS, nseg, LIN_VAL_DIM, LIN_KEY_DIM), F32)],
        scratch_shapes=[pltpu.VMEM((LIN_VAL_DIM, LIN_KEY_DIM), F32)],
        args=[q.arr, k.arr, v.arr, la.arr],
    )
    return out, states


def lin_bwd(name, q, k, v, la, states, dout, with_decay_grad, side=None):
    T = q.arr.shape[0]
    nseg = T // SEG
    n_out = 4 if with_decay_grad else 3

    def body(q_ref, k_ref, v_ref, la_ref, st_ref, do_ref, *rest):
        outs, dst = rest[:n_out], rest[n_out]

        @pl.when(pl.program_id(1) == 0)
        def _():
            dst[...] = jnp.zeros_like(dst)

        _, pull = jax.vjp(_lin_segment, q_ref[...].astype(F32), k_ref[...].astype(F32), v_ref[...].astype(F32),
                          _decay_rows(la_ref), st_ref[...])
        grads = pull((do_ref[...], dst[...]))
        for o, g in zip(outs, grads[:n_out]):
            o[...] = g
        dst[...] = grads[4]

    rev = lambda s: nseg - 1 - s
    key_spec = pl.BlockSpec((SEG, LIN_KEY_DIM), lambda h, s: (rev(s), h))
    val_spec = pl.BlockSpec((SEG, LIN_VAL_DIM), lambda h, s: (rev(s), h))
    out_specs = [key_spec, key_spec, val_spec] + ([key_spec] if with_decay_grad else [])
    widths = [LIN_KEY_DIM, LIN_KEY_DIM, LIN_VAL_DIM] + ([LIN_KEY_DIM] if with_decay_grad else [])
    return _side_call(
        side, body,
        name=name,
        grid=(LIN_HEADS, nseg),
        in_specs=[_lin_spec(w, rev) for w in (q, k, v, la)] + [
            pl.BlockSpec((None, None, LIN_VAL_DIM, LIN_KEY_DIM), lambda h, s: (h, rev(s), 0, 0)), val_spec],
        out_specs=out_specs,
        out_shape=[jax.ShapeDtypeStruct((T, LIN_HEADS * w), F32) for w in widths],
        scratch_shapes=[pltpu.VMEM((LIN_VAL_DIM, LIN_KEY_DIM), F32)],
        args=[q.arr, k.arr, v.arr, la.arr, states, dout],
    )


def _coords():
    return lax.axis_index("x"), lax.axis_index("y"), lax.axis_index("c")


def _peer(k):
    x, y, c = _coords()
    px = 1 - x if k & 4 else x
    py = 1 - y if k & 2 else y
    pc = 1 - c if k & 1 else c
    return (px, py, pc), 4 * px + 2 * py + pc


def _remote(src, dst, send_sem, recv_sem, device):
    return pltpu.make_async_remote_copy(src_ref=src, dst_ref=dst, send_sem=send_sem, recv_sem=recv_sem,
                                        device_id=device, device_id_type=pl.DeviceIdType.MESH)


class Side:
    def __init__(self, kind, arrays):
        self.kind, self.arrays, self.n = kind, list(arrays), len(arrays)
        self.outs_per = 2 if kind == "pair" else 1
        self.slots = {"gather": N_DEV - 1, "pair": 4, "chips": 3}[kind]

    def out_shape(self):
        if self.kind == "gather":
            return [jax.ShapeDtypeStruct((N_DEV,) + a.shape, a.dtype) for a in self.arrays]
        four = [jax.ShapeDtypeStruct((4,) + a.shape[1:], a.dtype) for a in self.arrays]
        return four + four if self.kind == "pair" else four

    def sems(self):
        own = (self.n, 4) if self.kind == "pair" else (self.n,)
        return [pltpu.SemaphoreType.DMA((self.n, self.slots)), pltpu.SemaphoreType.DMA((self.n, self.slots)),
                pltpu.SemaphoreType.DMA(own)]

    def start(self, ins, outs, sems):
        send, recv, own = sems
        x, y, c = _coords()
        me = 4 * x + 2 * y + c
        for a in range(self.n):
            if self.kind == "gather":
                pltpu.make_async_copy(ins[a], outs[a].at[me], own.at[a]).start()
                for k in (1, 2, 4, 6):
                    peer, _ = _peer(k)
                    _remote(ins[a], outs[a].at[me], send.at[a, k - 1], recv.at[a, k - 1], peer).start()
            elif self.kind == "pair":
                mine, sibs = outs[a], outs[self.n + a]
                for j in range(4):
                    pltpu.make_async_copy(ins[a].at[2 * j + c], mine.at[j], own.at[a, j]).start()
                    _remote(ins[a].at[2 * j + 1 - c], sibs.at[j], send.at[a, j], recv.at[a, j], (x, y, 1 - c)).start()
            else:
                chip = 2 * x + y
                pltpu.make_async_copy(ins[a].at[chip], outs[a].at[chip], own.at[a]).start()
                for d in (1, 2, 3):
                    tx, ty = (1 - x if d & 2 else x), (1 - y if d & 1 else y)
                    _remote(ins[a].at[2 * tx + ty], outs[a].at[chip], send.at[a, d - 1], recv.at[a, d - 1], (tx, ty, c)).start()

    def finish(self, ins, outs, sems):
        send, recv, own = sems
        x, y, c = _coords()
        me = 4 * x + 2 * y + c
        sibling = (x, y, 1 - c)
        for a in range(self.n):
            if self.kind == "gather":
                blk = outs[a].at[me]
                for k in (2, 4, 6):
                    _, pidx = _peer(k)
                    _remote(blk, blk, send.at[a, k - 1], recv.at[a, k - 1], sibling).wait_recv()
                    _remote(outs[a].at[pidx], outs[a].at[pidx], send.at[a, k], recv.at[a, k], sibling).start()
                pltpu.make_async_copy(ins[a], blk, own.at[a]).wait()
                for k in range(1, N_DEV):
                    _remote(blk, blk, send.at[a, k - 1], recv.at[a, k - 1], sibling).wait_send()
                for k in (1, 3, 5, 7):
                    _remote(blk, blk, send.at[a, k - 1], recv.at[a, k - 1], sibling).wait_recv()
            elif self.kind == "pair":
                mine, sibs = outs[a], outs[self.n + a]
                for j in range(4):
                    pltpu.make_async_copy(ins[a].at[j], mine.at[j], own.at[a, j]).wait()
                    _remote(ins[a].at[j], sibs.at[j], send.at[a, j], recv.at[a, j], sibling).wait()
            else:
                slab = outs[a].at[0]
                pltpu.make_async_copy(slab, slab, own.at[a]).wait()
                for d in (1, 2, 3):
                    _remote(slab, slab, send.at[a, d - 1], recv.at[a, d - 1], sibling).wait()


def _side_call(side, body, *, name, grid, in_specs, out_specs, out_shape, scratch_shapes, args, semantics=None,
               aliases=None):
    in_specs, out_specs, out_shape = list(in_specs), list(out_specs), list(out_shape)
    scratch_shapes, args = list(scratch_shapes), list(args)
    n_in, n_out = len(in_specs), len(out_specs)
    kernel_body = body
    if side is not None:
        n_so = side.n * side.outs_per

        def kernel_body(*refs):
            ins, side_ins = refs[:n_in], refs[n_in:n_in + side.n]
            pos = n_in + side.n
            outs, side_outs = refs[pos:pos + n_out], refs[pos + n_out:pos + n_out + n_so]
            rest = refs[pos + n_out + n_so:]
            scratch, sems = rest[:-3], rest[-3:]
            ids = [pl.program_id(ax) for ax in range(len(grid))]
            first = functools.reduce(jnp.logical_and, [i == 0 for i in ids])
            last = functools.reduce(jnp.logical_and, [i == g - 1 for i, g in zip(ids, grid)])

            @pl.when(first)
            def _():
                side.start(side_ins, side_outs, sems)

            body(*ins, *outs, *scratch)

            @pl.when(last)
            def _():
                side.finish(side_ins, side_outs, sems)

        any_spec = pl.BlockSpec(memory_space=pl.ANY)
        in_specs += [any_spec] * side.n
        args += side.arrays
        out_specs += [any_spec] * n_so
        out_shape += side.out_shape()
        scratch_shapes += side.sems()
        semantics = ("arbitrary",) * len(grid)
    outs = pl.pallas_call(
        kernel_body,
        name=name,
        grid=grid,
        in_specs=in_specs,
        out_specs=out_specs,
        out_shape=out_shape,
        scratch_shapes=scratch_shapes,
        input_output_aliases=aliases or {},
        compiler_params=pltpu.CompilerParams(dimension_semantics=semantics or ("arbitrary",) * len(grid),
                                             vmem_limit_bytes=VMEM_BIG),
    )(*args)
    if side is not None:
        side.results = list(outs[n_out:])
    return list(outs[:n_out])


def carrier(name, side):
    _side_call(side, lambda: None, name=name, grid=(1,), in_specs=[], out_specs=[], out_shape=[], scratch_shapes=[], args=[])
    return side.results


def all_reduce_small(vec):
    def body(v_ref, o_ref, land, send_sems, recv_sems):
        x, y, c = _coords()
        me = 4 * x + 2 * y + c
        land[me] = v_ref[...]
        copies = []
        for k in range(1, N_DEV):
            peer, _ = _peer(k)
            cp = pltpu.make_async_remote_copy(
                src_ref=v_ref, dst_ref=land.at[me], send_sem=send_sems.at[k - 1], recv_sem=recv_sems.at[k - 1],
                device_id=peer, device_id_type=pl.DeviceIdType.MESH)
            cp.start()
            copies.append(cp)
        for cp in copies:
            cp.wait()
        total = land[0]
        for s in range(1, N_DEV):
            total = total + land[s]
        o_ref[...] = total

    return pl.pallas_call(
        body,
        name="all_reduce_small",
        in_specs=[pl.BlockSpec(memory_space=pltpu.VMEM)],
        out_specs=pl.BlockSpec(memory_space=pltpu.VMEM),
        out_shape=jax.ShapeDtypeStruct(vec.shape, F32),
        scratch_shapes=[pltpu.VMEM((N_DEV,) + vec.shape, F32), pltpu.SemaphoreType.DMA((N_DEV - 1,)),
                        pltpu.SemaphoreType.DMA((N_DEV - 1,))],
    )(vec)


def _pack(parts):
    flat = jnp.concatenate([p.reshape(-1).astype(F32) for p in parts])
    n = flat.shape[0]
    rows = -(-n // 1024) * 8
    return jnp.pad(flat, (0, rows * 128 - n)).reshape(rows, 128)


def _unpack(packed, shapes):
    flat, out, pos = packed.reshape(-1), [], 0
    for s in shapes:
        n = math.prod(s)
        out.append(flat[pos:pos + n].reshape(s))
        pos += n
    return out


def _row2(v):
    return v.reshape(1, -1)


def _rider(hooks, name, ctx=None):
    if name not in hooks:
        return None, lambda: None
    side, done = hooks[name](ctx)
    return side, lambda: done(side.results)


def _layer_forward(h, W, consts, hooks):
    T = h.shape[0]
    cos, sin, ret_decay = consts
    side, landed = _rider(hooks, "proj")
    proj = mm("proj", h, W["in"], "nn", tm=1024, tn=512, tk=1024, side=side)
    landed()
    bias_tab = bias_table(W["rel_bias"])
    side, landed = _rider(hooks, "attn_fwd")
    attn = attn_fwd(proj, bias_tab, side=side)
    landed()
    rq2, rk2 = row_call("ret_pre", f_ret_pre, [Row(proj, 512, COL["rq"] // 512), Row(proj, 512, COL["rk"] // 512),
                                              Row(cos), Row(sin)], [], [(512, F32), (512, F32)], tm=512)
    ret_raw, ret_st = lin_fwd("ret_fwd", Win(rq2, 128), Win(rk2, 128), Win(proj, 256, COL["rv"] // 256), Win(ret_decay, 128))
    gq2, gla_la = row_call("gla_pre", f_gla_pre, [Row(proj, 512, COL["gq"] // 512), Row(proj, GLR_BLOCK, GLR_OFF // GLR_BLOCK)],
                           [W["wlr"], W["blr"]], [(512, F32), (512, F32)], tm=512)
    side, landed = _rider(hooks, "gla_fwd")
    gla_raw, gla_st = lin_fwd("gla_fwd", Win(gq2, 128), Win(proj, 128, COL["gk"] // 128), Win(proj, 256, COL["gv"] // 256),
                              Win(gla_la, 128), side=side)
    landed()
    ret_b, gla_b = row_call("normgate", f_normgate, [Row(ret_raw), Row(proj, 1024, COL["rg"] // 1024), Row(gla_raw),
                                                    Row(proj, 1024, COL["gg"] // 1024)], [W["gnorm"]],
                            [(1024, MXU_DTYPE), (1024, MXU_DTYPE)], tm=256)
    branches = (attn, ret_b, gla_b)
    p = [mm("branch_proj", branches[n], W["branch"][n], "nn", tm=1024, tn=1024, tk=1024) for n in range(3)]
    merged, = row_call("merge", f_merge, [Row(p[0]), Row(p[1]), Row(p[2]), Row(proj, GATES_W, GATES_OFF // GATES_W)], [],
                       [(1024, MXU_DTYPE)], tm=256)
    mix = mm("out_proj", merged, W["out"], "nn", tm=1024, tn=1024, tk=1024)
    x1, = row_call("ln1", f_deepnorm, [Row(h), Row(mix)], [W["ln1_g"], W["ln1_b"]], [(1024, F32)], tm=512)
    side, landed = _rider(hooks, "mlp_up")
    u, act = mm("mlp_up", x1, W["up"], "nn", tm=1024, tn=512, tk=1024, n_out=2, out_dtypes=(F32, MXU_DTYPE),
                epi=lambda r: (r, jnp.square(jnp.maximum(r, 0.0))), side=side)
    landed()
    y = mm("mlp_down", act, W["down"], "nn", tm=1024, tn=1024, tk=1024)
    x2, = row_call("ln2", f_deepnorm, [Row(x1), Row(y)], [W["ln2_g"], W["ln2_b"]], [(1024, F32)], tm=512)
    saved = dict(h=h, proj=proj, bias_tab=bias_tab, rq2=rq2, rk2=rk2, ret_st=ret_st, gq2=gq2, gla_la=gla_la, gla_st=gla_st,
                 ret_raw=ret_raw, gla_raw=gla_raw, branches=branches, p=p, merged=merged, mix=mix, x1=x1, u=u, act=act, y=y)
    return x2, saved


def _layer_backward(dx2, S, W, consts, hooks):
    cos, sin, ret_decay = consts
    h, proj, T = S["h"], S["proj"], S["h"].shape[0]
    add = lambda r, e: r + e
    dx1a, dy, dg2, db2 = row_call("ln2_bwd", vjp_fn(f_deepnorm, 2, 1), [Row(S["x1"]), Row(S["y"]), Row(dx2)],
                                  [W["ln2_g"], W["ln2_b"]], [(1024, F32), (1024, F32)], [(1, 1024), (1, 1024)], tm=256)
    du = mm("mlp_down_bwd", dy, W["down"], "nt", tm=1024, tn=512, tk=1024, extras=(S["u"],),
            epi=lambda r, u: r * (2.0 * jnp.maximum(u, 0.0)))
    g_down = mm("mlp_down_wgrad", S["act"], dy, "tn", tm=1024, tn=1024, tk=512, out_dtype=WIRE_DTYPE)
    dx1 = mm("mlp_up_bwd", du, W["up"], "nt", tm=512, tn=1024, tk=1024, extras=(dx1a,), epi=add)
    g_up = mm("mlp_up_wgrad", S["x1"], du, "tn", tm=1024, tn=1024, tk=512, out_dtype=WIRE_DTYPE)
    dha, dmix, dg1, db1 = row_call("ln1_bwd", vjp_fn(f_deepnorm, 2, 1), [Row(h), Row(S["mix"]), Row(dx1)],
                                   [W["ln1_g"], W["ln1_b"]], [(1024, F32), (1024, F32)], [(1, 1024), (1, 1024)], tm=256)
    dmerged = mm("out_proj_bwd", dmix, W["out"], "nt", tm=1024, tn=1024, tk=1024)
    g_out = mm("out_proj_wgrad", S["merged"], dmix, "tn", tm=1024, tn=1024, tk=512, out_dtype=WIRE_DTYPE)
    dproj = jax.ShapeDtypeStruct((T, PROJ_PAD), F32)
    p = S["p"]
    dp0, dp1, dp2, dproj = row_call(
        "merge_bwd", vjp_fn(f_merge, 4, 1), [Row(p[0]), Row(p[1]), Row(p[2]), Row(proj, GATES_W, GATES_OFF // GATES_W), Row(dmerged)],
        [], [(1024, F32)] * 3, tm=256, into=dproj, into_cols=[(GATES_W, GATES_OFF)])
    dps = (dp0, dp1, dp2)
    dbr = [mm("branch_proj_bwd", dps[n], W["branch"][n], "nt", tm=1024, tn=1024, tk=1024) for n in range(3)]
    g_branch = [mm("branch_proj_wgrad", S["branches"][n], dps[n], "tn", tm=1024, tn=1024, tk=512, out_dtype=WIRE_DTYPE)
                for n in range(3)]
    dret_raw, dgla_raw, dgnorm, dproj = row_call(
        "normgate_bwd", vjp_fn(lambda a, b, c, d, g: _reorder_normgate(a, b, c, d, g), 4, 2),
        [Row(S["ret_raw"]), Row(S["gla_raw"]), Row(proj, 1024, COL["rg"] // 1024), Row(proj, 1024, COL["gg"] // 1024),
         Row(dbr[1]), Row(dbr[2])], [W["gnorm"]], [(1024, F32), (1024, F32)], [(1, LIN_VAL_DIM)], tm=256,
        into=dproj, into_cols=[(1024, COL["rg"]), (1024, COL["gg"])])
    side, landed = _rider(hooks, "gla_bwd")
    dgq2, dgk, dgv, dgla_la = lin_bwd("gla_bwd", Win(S["gq2"], 128), Win(proj, 128, COL["gk"] // 128),
                                      Win(proj, 256, COL["gv"] // 256), Win(S["gla_la"], 128), S["gla_st"], dgla_raw, True,
                                      side=side)
    landed()
    dwlr, dblr, dproj = row_call(
        "gla_pre_bwd", _gla_pre_bwd, [Row(proj, 512, COL["gq"] // 512), Row(proj, GLR_BLOCK, GLR_OFF // GLR_BLOCK),
                                      Row(dgq2), Row(dgla_la), Row(dgk), Row(dgv)], [W["wlr"], W["blr"]], [],
        [(GLR_BLOCK, 512), (1, 512)], tm=256, into=dproj,
        into_cols=[(512, COL["gq"]), (PROJ_PAD - GLR_OFF, GLR_OFF), (512, COL["gk"]), (1024, COL["gv"])])
    drq2, drk2, drv = lin_bwd("ret_bwd", Win(S["rq2"], 128), Win(S["rk2"], 128), Win(proj, 256, COL["rv"] // 256),
                              Win(ret_decay, 128), S["ret_st"], dret_raw, False)
    dproj, = row_call(
        "ret_pre_bwd", _ret_pre_bwd, [Row(proj, 512, COL["rq"] // 512), Row(proj, 512, COL["rk"] // 512), Row(cos), Row(sin),
                                      Row(drq2), Row(drk2), Row(drv)], [], [], tm=256, into=dproj,
        into_cols=[(512, COL["rq"]), (512, COL["rk"]), (1024, COL["rv"])])
    side, landed = _rider(hooks, "attn_bwd")
    dproj, dbias = attn_bwd(proj, S["bias_tab"], dbr[0], dproj, side=side)
    landed()
    d_rel_bias = bias_table_grad(dbias)
    side, landed = _rider(hooks, "proj_bwd", dict(branch=g_branch, out=g_out, up=g_up, down=g_down))
    dh = mm("proj_bwd", dproj, W["in"], "nt", tm=512, tn=1024, tk=1280, extras=(dha,), epi=add, side=side)
    landed()
    g_in = mm("proj_wgrad", h, dproj, "tn", tm=1024, tn=1280, tk=512, out_dtype=WIRE_DTYPE)
    big = dict(g_in=g_in, branch=g_branch, out=g_out, up=g_up, down=g_down)
    small = dict(rel_bias=d_rel_bias, wlr=dwlr[:GLA_RANK], blr=dblr, gnorm=dgnorm, ln1_g=dg1, ln1_b=db1, ln2_g=dg2, ln2_b=db2)
    return dh, big, small


def _reorder_normgate(ret_raw, gla_raw, rg, gg, gnorm):
    return f_normgate(ret_raw, rg, gla_raw, gg, gnorm)


def _gla_pre_bwd(gq, glr, dq2, dla, dgk, dgv, wlr, blr):
    _, pull = jax.vjp(f_gla_pre, gq, glr, wlr, blr)
    dgq, dglr, dwlr, dblr = pull((dq2, dla))
    dglr = jnp.concatenate([dglr, jnp.zeros((dglr.shape[0], PROJ_PAD - GLR_OFF - GLR_BLOCK), F32)], axis=1)
    return dgq, dglr, dgk, dgv, dwlr, dblr


def _ret_pre_bwd(rq, rk, cos, sin, drq2, drk2, drv):
    _, pull = jax.vjp(lambda a, b: f_ret_pre(a, b, cos, sin), rq, rk)
    drq, drk = pull((drq2, drk2))
    return drq, drk, drv


SHARDED = ("in", "branch", "out", "up", "down")

_FULL = {
    "branch": lambda g: [g[:, n].reshape(D_MODEL, D_MODEL) for n in range(3)],
    "out": lambda g: g.reshape(D_MODEL, D_MODEL),
    "up": lambda g: jnp.transpose(g, (1, 0, 2)).reshape(D_MODEL, D_FF),
    "down": lambda g: g.reshape(D_FF, D_MODEL),
}


def _full_in(g):
    full = jnp.transpose(g, (1, 0, 2)).reshape(D_MODEL, PROJ_WIDTH)
    return jnp.concatenate([full[:, :MAIN_W], full[:, GATES_SRC:], full[:, GLR_SRC:GATES_SRC],
                            jnp.zeros((D_MODEL, PROJ_PAD - PROJ_WIDTH), full.dtype)], axis=1)


def _full_wlr(g):
    wlr = jnp.transpose(g, (1, 0, 2)).reshape(GLA_RANK, LIN_HEADS * LIN_KEY_DIM)
    return jnp.pad(wlr, ((0, GLR_BLOCK - GLA_RANK), (0, 0)))


def _slab_in(g):
    g = jnp.concatenate([g[:, :MAIN_W], g[:, GLR_OFF:GLR_OFF + GLA_RANK], g[:, GATES_OFF:GATES_OFF + GATES_W]], axis=1)
    return jnp.transpose(g.reshape(D_MODEL, N_DEV, PROJ_WIDTH // N_DEV), (1, 0, 2))


_SLABS = {
    "in": _slab_in,
    "branch": lambda gs: jnp.stack([b.reshape(N_DEV, D_MODEL // N_DEV, D_MODEL) for b in gs], axis=1),
    "out": lambda g: g.reshape(N_DEV, D_MODEL // N_DEV, D_MODEL),
    "up": lambda g: jnp.transpose(g.reshape(D_MODEL, N_DEV, D_FF // N_DEV), (1, 0, 2)),
    "down": lambda g: g.reshape(N_DEV, D_FF // N_DEV, D_MODEL),
}


def _pair_sums(name, slabs):
    n = len(slabs)
    res = carrier(name, Side("pair", slabs))
    sums = []
    for mine, sibs in zip(res[:n], res[n:]):
        width = mine.shape[-1]
        out, = row_call("pair_sum", lambda p, q: (p.astype(F32) + q.astype(F32),),
                        [Row(mine.reshape(-1, width)), Row(sibs.reshape(-1, width))], [], [(width, WIRE_DTYPE)], tm=256)
        sums.append(out.reshape(mine.shape))
    return sums


def _sum_adam(name, parts, w, m, v, tm):
    C = w.shape[-1]
    rows = [Row(parts, lead=(s,)) for s in range(parts.shape[0])] + [Row(w), Row(m), Row(v)]
    return row_call(name, f_sum_adam, rows, [], [(C, F32)] * 4, tm=tm)


def kernel(x, ln_in_g, ln_in_b, w_in, rel_bias, gla_w_lr, gla_b_lr, gla_norm_g, w_branch, w_out, ln1_g, ln1_b, w_up, w_down, ln2_g, ln2_b, loss_target, m_ln_in_g, m_ln_in_b, m_w_in, m_rel_bias, m_gla_w_lr, m_gla_b_lr, m_gla_norm_g, m_w_branch, m_w_out, m_ln1_g, m_ln1_b, m_w_up, m_w_down, m_ln2_g, m_ln2_b, v_ln_in_g, v_ln_in_b, v_w_in, v_rel_bias, v_gla_w_lr, v_gla_b_lr, v_gla_norm_g, v_w_branch, v_w_out, v_ln1_g, v_ln1_b, v_w_up, v_w_down, v_ln2_g, v_ln2_b):
    T = x.shape[1]
    x2d = x.reshape(T, D_MODEL)
    tgt = loss_target.reshape(T, D_MODEL)
    me = 4 * lax.axis_index("x") + 2 * lax.axis_index("y") + lax.axis_index("c")

    inv = ROPE_BASE ** (-jnp.arange(0, LIN_KEY_DIM, 2, dtype=F32) / LIN_KEY_DIM)
    ang = jnp.arange(T, dtype=F32)[:, None] * inv[None, :]
    cos = jnp.concatenate([jnp.cos(ang), jnp.cos(ang)], axis=1)
    sin = jnp.concatenate([-jnp.sin(ang), jnp.sin(ang)], axis=1)
    log_gamma = jnp.log1p(-jnp.exp2(-5.0 - jnp.arange(LIN_HEADS, dtype=F32)))
    ret_decay = jnp.repeat(log_gamma, LIN_KEY_DIM).reshape(1, LIN_HEADS * LIN_KEY_DIM)
    consts = (cos, sin, ret_decay)

    layers = [dict(rel_bias=rel_bias[l], blr=_row2(gla_b_lr[l]), gnorm=_row2(gla_norm_g[l]), ln1_g=_row2(ln1_g[l]),
                   ln1_b=_row2(ln1_b[l]), ln2_g=_row2(ln2_g[l]), ln2_b=_row2(ln2_b[l])) for l in range(DEPTH)]
    shards = dict(branch=w_branch, out=w_out, up=w_up, down=w_down)
    shards["in"] = w_in
    wire = [{n: shards[n][l].astype(WIRE_DTYPE) for n in SHARDED} for l in range(DEPTH)]

    def gather_of(l, names):
        def landed(res):
            for n, g in zip(names, res):
                layers[l][n] = _full_in(g) if n == "in" else _FULL[n](g)
        return lambda ctx: (Side("gather", [wire[l][n] for n in names]), landed)

    first = carrier("gather_first", Side("gather", [wire[0]["in"]] + [gla_w_lr[l] for l in range(DEPTH)]))
    layers[0]["in"] = _full_in(first[0])
    for l in range(DEPTH):
        layers[l]["wlr"] = _full_wlr(first[1 + l])

    fwd_hooks = [{"proj": gather_of(0, ("branch", "out", "up", "down")), "attn_fwd": gather_of(1, ("in",)),
                  "gla_fwd": gather_of(1, ("branch", "out")), "mlp_up": gather_of(1, ("up", "down"))}, {}]
    h, = row_call("ln_in", f_ln_in, [Row(x2d)], [_row2(ln_in_g), _row2(ln_in_b)], [(D_MODEL, F32)], tm=512)
    saved = []
    for l in range(DEPTH):
        h, s = _layer_forward(h, layers[l], consts, fwd_hooks[l])
        saved.append(s)
    dh, loss_lanes = row_call("loss", f_loss, [Row(h), Row(tgt)], [], [(D_MODEL, F32)], [(1, 128)], tm=512)

    received = [dict() for _ in range(DEPTH)]

    def scatter_of(l, names, sums=None):
        def make(ctx):
            parts = sums if sums is not None else _pair_sums("pair_grads", [_SLABS[n](ctx[n]) for n in names])

            def landed(res):
                received[l].update(zip(names, res))
            return Side("chips", parts), landed
        return make

    small = [None] * DEPTH
    dh, big, small[1] = _layer_backward(dh, saved[1], layers[1], consts, {})
    sums1 = _pair_sums("pair_grads", [_SLABS[n](big["g_in"] if n == "in" else big[n]) for n in SHARDED])
    rest = SHARDED[1:]
    dh, big, small[0] = _layer_backward(dh, saved[0], layers[0], consts, {
        "gla_bwd": scatter_of(1, ("in",), sums1[:1]), "attn_bwd": scatter_of(1, rest, sums1[1:]),
        "proj_bwd": scatter_of(0, rest)})
    last = Side("chips", _pair_sums("pair_grads", [_slab_in(big["g_in"])]))
    received[0]["in"] = carrier("scatter_last", last)[0]
    grad_x, d_in_g, d_in_b = row_call("ln_in_bwd", vjp_fn(f_ln_in, 1, 1), [Row(x2d), Row(dh)],
                                      [_row2(ln_in_g), _row2(ln_in_b)], [(D_MODEL, F32)], [(1, D_MODEL), (1, D_MODEL)], tm=256)

    small_names = ("rel_bias", "wlr", "blr", "gnorm", "ln1_g", "ln1_b", "ln2_g", "ln2_b")
    parts = [d_in_g, d_in_b] + [small[l][n] for l in range(DEPTH) for n in small_names] + [loss_lanes]
    shapes = [(D_MODEL,), (D_MODEL,)] + [s for l in range(DEPTH) for s in (
        (ATTN_HEADS, N_REL), (GLA_RANK, 512), (512,), (LIN_VAL_DIM,), (D_MODEL,), (D_MODEL,), (D_MODEL,), (D_MODEL,))] + [(128,)]
    red = _unpack(all_reduce_small(_pack(parts)), shapes)
    loss = jnp.sum(red[-1])
    g_small = {"ln_in_g": red[0], "ln_in_b": red[1]}
    for l in range(DEPTH):
        for j, n in enumerate(small_names):
            g_small[(l, n)] = red[2 + l * len(small_names) + j]
    shard = LIN_HEADS * LIN_KEY_DIM // N_DEV
    g_wlr = jnp.stack([lax.dynamic_slice_in_dim(g_small[(l, "wlr")], me * shard, shard, axis=1) for l in range(DEPTH)])
    stack = lambda n: jnp.stack([g_small[(l, n)] for l in range(DEPTH)])
    g_rep = [g_small["ln_in_g"], g_small["ln_in_b"], stack("rel_bias"), g_wlr, stack("blr"), stack("gnorm"),
             stack("ln1_g"), stack("ln1_b"), stack("ln2_g"), stack("ln2_b")]
    w_rep = [ln_in_g, ln_in_b, rel_bias, gla_w_lr, gla_b_lr, gla_norm_g, ln1_g, ln1_b, ln2_g, ln2_b]
    m_rep = [m_ln_in_g, m_ln_in_b, m_rel_bias, m_gla_w_lr, m_gla_b_lr, m_gla_norm_g, m_ln1_g, m_ln1_b, m_ln2_g, m_ln2_b]
    v_rep = [v_ln_in_g, v_ln_in_b, v_rel_bias, v_gla_w_lr, v_gla_b_lr, v_gla_norm_g, v_ln1_g, v_ln1_b, v_ln2_g, v_ln2_b]
    rep_shapes = [w.shape for w in w_rep]
    d_pack, m_pack, v_pack = row_call("adam_small", f_adam, [Row(_pack(g_rep)), Row(_pack(w_rep)), Row(_pack(m_rep)), Row(_pack(v_rep))],
                                      [], [(128, F32)] * 3, tm=256)
    d_rep, nm_rep, nv_rep = _unpack(d_pack, rep_shapes), _unpack(m_pack, rep_shapes), _unpack(v_pack, rep_shapes)

    def owned(key, w, m, v, name, tm):
        res = []
        for l in range(DEPTH):
            shp = w[l].shape
            flat = lambda a: a.reshape(-1, shp[-1])
            parts_l = received[l][key].reshape(4, -1, shp[-1])
            res.append([o.reshape(shp) for o in _sum_adam(name, parts_l, flat(w[l]), flat(m[l]), flat(v[l]), tm)])
        return [jnp.stack([res[l][j] for l in range(DEPTH)]) for j in range(4)]

    o_in = owned("in", w_in, m_w_in, v_w_in, "adam_w_in", 256)
    o_br = owned("branch", w_branch, m_w_branch, v_w_branch, "adam_w_branch", 128)
    o_out = owned("out", w_out, m_w_out, v_w_out, "adam_w_out", 128)
    o_up = owned("up", w_up, m_w_up, v_w_up, "adam_w_up", 256)
    o_down = owned("down", w_down, m_w_down, v_w_down, "adam_w_down", 256)

    def tree(j, rep):
        return [rep[0], rep[1], o_in[j], rep[2], rep[3], rep[4], rep[5], o_br[j], o_out[j], rep[6], rep[7], o_up[j], o_down[j],
                rep[8], rep[9]]

    return (loss, grad_x.reshape(1, T, D_MODEL), *tree(0, g_rep), *tree(1, d_rep), *tree(2, nm_rep), *tree(3, nv_rep))
```

```python
import functools
import math

import jax
import jax.numpy as jnp
from jax import lax
from jax.experimental import pallas as pl
from jax.experimental.pallas import tpu as pltpu

F32 = jnp.float32
MXU_DTYPE = jnp.bfloat16
WIRE_DTYPE = jnp.bfloat16

N_DEV = 8
D_MODEL = 1024
DEPTH = 2
CHUNK = 64
ATTN_HEADS = 8
ATTN_HEAD_DIM = 128
ATTN_LEFT_CHUNKS = 8
BAND = (ATTN_LEFT_CHUNKS + 1) * CHUNK
PAD_ROWS = ATTN_LEFT_CHUNKS * CHUNK
REL_CLIP = 2 * CHUNK
N_REL = 2 * REL_CLIP + 1
REL_MIN = REL_CLIP - (CHUNK - 1)
LIN_HEADS = 4
LIN_KEY_DIM = 128
LIN_VAL_DIM = 256
GLA_RANK = 16
GLA_GATE_NORMALIZER = 16.0
ROPE_BASE = 10000.0
D_FF = 4 * D_MODEL
LN_EPS = 1e-5
NEG_INF = -1e30
ALPHA = (2 * DEPTH) ** 0.25
ADAM_LR, ADAM_B1, ADAM_B2, ADAM_EPS, ADAM_WD, ADAM_STEP = 0.001, 0.9, 0.999, 1e-08, 0.01, 10

PROJ_WIDTH = 12304
GLR_SRC = 9216
GATES_SRC = GLR_SRC + GLA_RANK
MAIN_W = 9216
GATES_W = 3 * D_MODEL
GATES_OFF = MAIN_W
GLR_OFF = MAIN_W + GATES_W
PROJ_PAD = 12800
GLR_BLOCK = 128
COL = dict(aq=0, ak=1024, av=2048, rq=3072, rk=3584, rv=4096, rg=5120, gq=6144, gk=6656, gv=7168, gg=8192)

SEG_CHUNKS = 8
SEG = SEG_CHUNKS * CHUNK
VMEM_BIG = 56 * 1024 * 1024

_NN = (((1,), (0,)), ((), ()))
_NT = (((1,), (1,)), ((), ()))
_TN = (((0,), (0,)), ((), ()))


def _mxu(a, b, dn):
    return lax.dot_general(a.astype(MXU_DTYPE), b.astype(MXU_DTYPE), dn, preferred_element_type=F32)


@jax.custom_vjp
def dot_nn(a, b):
    return _mxu(a, b, _NN)


def _dot_nn_fwd(a, b):
    return _mxu(a, b, _NN), (a, b)


def _dot_nn_bwd(res, g):
    a, b = res
    return _mxu(g, b, _NT).astype(a.dtype), _mxu(a, g, _TN).astype(b.dtype)


dot_nn.defvjp(_dot_nn_fwd, _dot_nn_bwd)


@jax.custom_vjp
def dot_nt(a, b):
    return _mxu(a, b, _NT)


def _dot_nt_fwd(a, b):
    return _mxu(a, b, _NT), (a, b)


def _dot_nt_bwd(res, g):
    a, b = res
    return _mxu(g, b, _NN).astype(a.dtype), _mxu(g, a, _TN).astype(b.dtype)


dot_nt.defvjp(_dot_nt_fwd, _dot_nt_bwd)


@jax.custom_vjp
def dot_tn(a, b):
    return _mxu(a, b, _TN)


def _dot_tn_fwd(a, b):
    return _mxu(a, b, _TN), (a, b)


def _dot_tn_bwd(res, g):
    a, b = res
    return _mxu(b, g, _NT).astype(a.dtype), _mxu(a, g, _NN).astype(b.dtype)


dot_tn.defvjp(_dot_tn_fwd, _dot_tn_bwd)


def _tri(n, lower):
    r = lax.broadcasted_iota(jnp.int32, (n, n), 0)
    c = lax.broadcasted_iota(jnp.int32, (n, n), 1)
    return (r >= c) if lower else (r <= c)


def _f32_dot(a, b):
    return lax.dot_general(a, b, _NN, precision=lax.Precision.HIGHEST, preferred_element_type=F32)


@jax.custom_vjp
def chunk_cumsum(x):
    return _f32_dot(_tri(x.shape[0], True).astype(F32), x)


def _chunk_cumsum_fwd(x):
    return chunk_cumsum(x), None


def _chunk_cumsum_bwd(_, g):
    return (_f32_dot(_tri(g.shape[0], False).astype(F32), g),)


chunk_cumsum.defvjp(_chunk_cumsum_fwd, _chunk_cumsum_bwd)


@jax.custom_vjp
def swap_halves(t):
    return pltpu.roll(t, t.shape[-1] // 2, t.ndim - 1)


def _swap_fwd(t):
    return swap_halves(t), None


def _swap_bwd(_, g):
    return (swap_halves(g),)


swap_halves.defvjp(_swap_fwd, _swap_bwd)


def mm(name, a, b, dims, *, tm, tn, tk, out_dtype=F32, extras=(), epi=None, n_out=1, out_dtypes=None, side=None):
    if dims == "nn":
        (M, K), N = a.shape, b.shape[1]
    elif dims == "nt":
        (M, K), N = a.shape, b.shape[0]
    else:
        (K, M), N = a.shape, b.shape[1]
    tm, tn, tk = min(tm, M), min(tn, N), min(tk, K)
    assert M % tm == 0 and N % tn == 0 and K % tk == 0, (name, M, N, K, tm, tn, tk)
    nk = K // tk
    dn = {"nn": _NN, "nt": _NT, "tn": _TN}[dims]
    a_spec = pl.BlockSpec((tk, tm), lambda i, j, k: (k, i)) if dims == "tn" else pl.BlockSpec((tm, tk), lambda i, j, k: (i, k))
    b_spec = pl.BlockSpec((tn, tk), lambda i, j, k: (j, k)) if dims == "nt" else pl.BlockSpec((tk, tn), lambda i, j, k: (k, j))
    o_spec = pl.BlockSpec((tm, tn), lambda i, j, k: (i, j))
    n_ex = len(extras)
    out_dtypes = out_dtypes or (out_dtype,) * n_out

    def finish(r, ex_refs, o_refs):
        if epi is not None:
            r = epi(r, *[e[...] for e in ex_refs])
        r = r if isinstance(r, tuple) else (r,)
        for o, v in zip(o_refs, r):
            o[...] = v.astype(o.dtype)

    def body(*refs):
        a_ref, b_ref = refs[:2]
        ex_refs = refs[2:2 + n_ex]
        o_refs = refs[2 + n_ex:2 + n_ex + n_out]
        part = _mxu(a_ref[...], b_ref[...], dn)
        if nk == 1:
            finish(part, ex_refs, o_refs)
            return
        acc = refs[-1]
        k = pl.program_id(2)

        @pl.when(k == 0)
        def _():
            acc[...] = part

        @pl.when(k > 0)
        def _():
            acc[...] += part

        @pl.when(k == nk - 1)
        def _():
            finish(acc[...], ex_refs, o_refs)

    outs = _side_call(
        side, body,
        name=name,
        grid=(M // tm, N // tn, nk),
        in_specs=[a_spec, b_spec] + [o_spec] * n_ex,
        out_specs=[o_spec] * n_out,
        out_shape=[jax.ShapeDtypeStruct((M, N), dt) for dt in out_dtypes],
        scratch_shapes=[pltpu.VMEM((tm, tn), F32)] if nk > 1 else [],
        args=[a, b, *extras],
        semantics=("parallel", "parallel", "arbitrary"),
    )
    return outs[0] if n_out == 1 else outs


class Row:
    def __init__(self, arr, width=None, col=0, lead=(), rowmap=None):
        self.arr, self.lead, self.col = arr, tuple(lead), col
        self.width = arr.shape[-1] if width is None else width
        self.rowmap = rowmap

    def spec(self, tm):
        lead, col, rowmap = self.lead, self.col, self.rowmap or (lambda i: i)
        return pl.BlockSpec((None,) * len(lead) + (tm, self.width), lambda i: (*lead, rowmap(i), col))


def row_call(name, fn, rows, params, new_outs, red_shapes=(), *, tm, into=None, into_cols=()):
    R = rows[0].arr.shape[len(rows[0].lead)]
    tm = min(tm, R)
    assert R % tm == 0, (name, R, tm)
    n_row, n_par, n_new, n_into, n_red = len(rows), len(params), len(new_outs), len(into_cols), len(red_shapes)
    fresh = isinstance(into, jax.ShapeDtypeStruct)

    def body(*refs):
        row_refs = refs[:n_row]
        par_refs = refs[n_row:n_row + n_par]
        pos = n_row + n_par + (1 if (into is not None and not fresh) else 0)
        new_refs = refs[pos:pos + n_new]
        red_refs = refs[pos + n_new:pos + n_new + n_red]
        pos2 = pos + n_new + n_red
        into_ref = refs[pos2] if into is not None else None
        stage = refs[pos2 + 1:] if into is not None else ()
        i = pl.program_id(0)
        vals = fn(*[r[...] for r in row_refs], *[p[...] for p in par_refs])
        vals = vals if isinstance(vals, (tuple, list)) else (vals,)
        for o, v in zip(new_refs, vals[:n_new]):
            o[...] = v.astype(o.dtype)
        for (width, off), buf, v in zip(into_cols, stage, vals[n_new:n_new + n_into]):
            buf[...] = v.astype(buf.dtype)
            pltpu.sync_copy(buf, into_ref.at[pl.ds(pl.multiple_of(i * tm, tm), tm), pl.ds(off, width)])
        for o, v in zip(red_refs, vals[n_new + n_into:]):
            @pl.when(i == 0)
            def _(o=o, v=v):
                o[...] = v.astype(o.dtype)

            @pl.when(i > 0)
            def _(o=o, v=v):
                o[...] += v.astype(o.dtype)

    in_specs = [r.spec(tm) for r in rows] + [pl.BlockSpec(p.shape, lambda i, nd=p.ndim: (0,) * nd) for p in params]
    args = [r.arr for r in rows] + list(params)
    out_specs = [pl.BlockSpec((tm, w), lambda i: (i, 0)) for w, _ in new_outs]
    out_shape = [jax.ShapeDtypeStruct((R, w), dt) for w, dt in new_outs]
    out_specs += [pl.BlockSpec(s, lambda i, nd=len(s): (0,) * nd) for s in red_shapes]
    out_shape += [jax.ShapeDtypeStruct(s, F32) for s in red_shapes]
    aliases, scratch = {}, []
    if into is not None:
        out_specs.append(pl.BlockSpec(memory_space=pl.ANY))
        out_shape.append(jax.ShapeDtypeStruct(into.shape, into.dtype))
        if not fresh:
            in_specs.append(pl.BlockSpec(memory_space=pl.ANY))
            args.append(into)
            aliases = {len(args) - 1: len(out_shape) - 1}
        scratch = [pltpu.VMEM((tm, w), into.dtype) for w, _ in into_cols]
    outs = pl.pallas_call(
        body,
        name=name,
        grid=(R // tm,),
        in_specs=in_specs,
        out_specs=out_specs,
        out_shape=out_shape,
        scratch_shapes=scratch,
        input_output_aliases=aliases,
        compiler_params=pltpu.CompilerParams(dimension_semantics=("arbitrary",), vmem_limit_bytes=VMEM_BIG),
    )(*args)
    return list(outs)


def vjp_fn(fn, n_row, n_out, passthrough=0):
    def bwd(*args):
        rows = [a.astype(F32) for a in args[:n_row]]
        cts = args[n_row:n_row + n_out]
        thru = args[n_row + n_out:n_row + n_out + passthrough]
        pars = args[n_row + n_out + passthrough:]
        outs, pull = jax.vjp(lambda *xs: tuple(fn(*xs)), *rows, *pars)
        grads = pull(tuple(c.astype(o.dtype) for c, o in zip(cts, outs)))
        return tuple(grads[:n_row]) + tuple(thru) + tuple(grads[n_row:])

    return bwd


def _ln(x, g, b):
    mu = jnp.mean(x, axis=-1, keepdims=True)
    xc = x - mu
    var = jnp.mean(xc * xc, axis=-1, keepdims=True)
    return xc * lax.rsqrt(var + LN_EPS) * g + b


def f_ln_in(x, g, b):
    return (_ln(x, g, b),)


def f_deepnorm(x, y, g, b):
    return (_ln(ALPHA * x + y, g, b),)


def _rope(t, cos, sin):
    parts = []
    for h in range(LIN_HEADS):
        th = t[:, h * LIN_KEY_DIM:(h + 1) * LIN_KEY_DIM]
        parts.append(th * cos + swap_halves(th) * sin)
    return jnp.concatenate(parts, axis=1)


def f_ret_pre(rq, rk, cos, sin):
    return _rope(rq, cos, sin), _rope(rk, cos, sin) * LIN_KEY_DIM ** -0.5


def _log_sigmoid(x):
    return jnp.minimum(x, 0.0) - jnp.log(1.0 + jnp.exp(-jnp.abs(x)))


def f_gla_pre(gq, glr, wlr, blr):
    gate_pre = dot_nn(glr, wlr) + blr
    return gq * LIN_KEY_DIM ** -0.5, _log_sigmoid(gate_pre) / GLA_GATE_NORMALIZER


def f_normgate(ret_raw, rg, gla_raw, gg, gnorm):
    rets, glas = [], []
    for h in range(LIN_HEADS):
        sl = slice(h * LIN_VAL_DIM, (h + 1) * LIN_VAL_DIM)
        t = ret_raw[:, sl]
        mu = jnp.mean(t, axis=-1, keepdims=True)
        tc = t - mu
        rets.append(tc * lax.rsqrt(jnp.mean(tc * tc, axis=-1, keepdims=True) + LN_EPS))
        u = gla_raw[:, sl]
        glas.append(u * lax.rsqrt(jnp.mean(u * u, axis=-1, keepdims=True) + LN_EPS) * gnorm)
    ret = jax.nn.silu(rg) * jnp.concatenate(rets, axis=1)
    gla = jax.nn.silu(gg) * jnp.concatenate(glas, axis=1)
    return ret, gla


def f_merge(p0, p1, p2, gl):
    out = jax.nn.sigmoid(gl[:, :D_MODEL]) * p0
    out = out + jax.nn.sigmoid(gl[:, D_MODEL:2 * D_MODEL]) * p1
    return (out + jax.nn.sigmoid(gl[:, 2 * D_MODEL:]) * p2,)


def f_loss(y, tgt):
    diff = y - tgt
    sq = jnp.sum(diff * diff, axis=0, keepdims=True)
    lanes = sq[:, :128]
    for j in range(1, D_MODEL // 128):
        lanes = lanes + sq[:, j * 128:(j + 1) * 128]
    return diff * (1.0 / D_MODEL), lanes * (0.5 / D_MODEL)


def _adam(g, w, m, v):
    m = ADAM_B1 * m + (1.0 - ADAM_B1) * g
    v = ADAM_B2 * v + (1.0 - ADAM_B2) * jnp.square(g)
    m_hat = m / (1.0 - ADAM_B1 ** ADAM_STEP)
    v_hat = v / (1.0 - ADAM_B2 ** ADAM_STEP)
    delta = -ADAM_LR * (m_hat / (jnp.sqrt(v_hat) + ADAM_EPS) + ADAM_WD * w)
    return delta, m, v


def f_sum_adam(*args):
    g = args[0].astype(F32)
    for p in args[1:-3]:
        g = g + p.astype(F32)
    return (g,) + _adam(g, *args[-3:])


def f_adam(g, w, m, v):
    return _adam(g, w, m, v)


GROUP = 4
GROUP_ROWS = GROUP * CHUNK
GROUP_KEYS = (GROUP + ATTN_LEFT_CHUNKS) * CHUNK
DIAG_ROWS = REL_CLIP + CHUNK
GRAD_ROWS = 200


def _blocks_by_delta():
    blocks = {}
    for qi in range(GROUP):
        for J in range(GROUP_KEYS // 128):
            blocks.setdefault(qi - 2 * J, []).append((qi, J))
    return blocks


def bias_table(rel_bias_l):
    def body(rb_ref, o_ref):
        h = pl.program_id(0)
        rm = lax.broadcasted_iota(jnp.int32, (CHUNK, 128), 0)
        cm = lax.broadcasted_iota(jnp.int32, (CHUNK, 128), 1)
        ahead = jnp.where(cm >= CHUNK, 1, 0)
        for delta, where in _blocks_by_delta().items():
            base = 64 * delta + PAD_ROWS
            rel = jnp.minimum(base + rm - cm, REL_CLIP) + REL_CLIP
            in_band = (ahead - delta >= 0) & (ahead - delta <= ATTN_LEFT_CHUNKS)
            lo = max(REL_MIN, min(base - 127, REL_CLIP) + REL_CLIP)
            hi = min(base + CHUNK - 1, REL_CLIP) + REL_CLIP

            def step(r, acc, rel=rel):
                return jnp.where(rel == r, rb_ref[h * N_REL + r], acc)

            blk = jnp.zeros((CHUNK, 128), F32) if hi < lo else lax.fori_loop(lo, hi + 1, step, jnp.zeros((CHUNK, 128), F32))
            blk = jnp.where(in_band, blk, NEG_INF)
            for qi, J in where:
                o_ref[pl.ds(qi * CHUNK, CHUNK), pl.ds(J * 128, 128)] = blk

    return pl.pallas_call(
        body,
        name="bias_table",
        grid=(ATTN_HEADS,),
        in_specs=[pl.BlockSpec(memory_space=pltpu.SMEM)],
        out_specs=pl.BlockSpec((None, GROUP_ROWS, GROUP_KEYS), lambda h: (h, 0, 0)),
        out_shape=jax.ShapeDtypeStruct((ATTN_HEADS, GROUP_ROWS, GROUP_KEYS), F32),
        compiler_params=pltpu.CompilerParams(dimension_semantics=("arbitrary",)),
    )(rel_bias_l.reshape(ATTN_HEADS * N_REL))


def bias_table_grad(dbias):
    blocks = _blocks_by_delta()
    below_clip = range(-(CHUNK - 1), REL_CLIP)
    deltas = [d for d in sorted(blocks) if any(-127 <= pos - 64 * d - PAD_ROWS <= CHUNK - 1 for pos in below_clip)]

    def body(d_ref, o_ref, diag):
        sums = []
        for delta in deltas:
            acc = None
            for qi, J in blocks[delta]:
                blk = d_ref[pl.ds(qi * CHUNK, CHUNK), pl.ds(J * 128, 128)]
                acc = blk if acc is None else acc + blk
            sums.append(acc)
        off = lax.broadcasted_iota(jnp.int32, (CHUNK, 128), 0) - lax.broadcasted_iota(jnp.int32, (CHUNK, 128), 1)
        diag[...] = jnp.zeros_like(diag)

        def step(t, carry):
            pos = t - (CHUNK - 1)
            picked = jnp.zeros((CHUNK, 128), F32)
            for delta, s in zip(deltas, sums):
                picked = picked + jnp.where(off == pos - 64 * delta - PAD_ROWS, s, 0.0)
            diag[pl.ds(t, 1), :] = jnp.sum(picked, axis=0, keepdims=True)
            return carry

        lax.fori_loop(0, DIAG_ROWS - 1, step, 0)
        d = d_ref[...]
        total = jnp.sum(jnp.sum(d, axis=1, keepdims=True), axis=0, keepdims=True)
        o_ref[pl.ds(0, DIAG_ROWS), :] = jnp.broadcast_to(jnp.sum(diag[...], axis=1, keepdims=True), (DIAG_ROWS, 128))
        o_ref[pl.ds(DIAG_ROWS, GRAD_ROWS - DIAG_ROWS), :] = jnp.broadcast_to(total, (GRAD_ROWS - DIAG_ROWS, 128))

    out = pl.pallas_call(
        body,
        name="bias_table_grad",
        grid=(ATTN_HEADS,),
        in_specs=[pl.BlockSpec((None, GROUP_ROWS, GROUP_KEYS), lambda h: (h, 0, 0))],
        out_specs=pl.BlockSpec((None, GRAD_ROWS, 128), lambda h: (h, 0, 0)),
        out_shape=jax.ShapeDtypeStruct((ATTN_HEADS, GRAD_ROWS, 128), F32),
        scratch_shapes=[pltpu.VMEM((DIAG_ROWS, 128), F32)],
        compiler_params=pltpu.CompilerParams(dimension_semantics=("arbitrary",)),
    )(dbias)
    below = out[:, :DIAG_ROWS - 1, 0]
    clipped = out[:, DIAG_ROWS, 0] - jnp.sum(below, axis=1)
    return jnp.concatenate([jnp.zeros((ATTN_HEADS, REL_MIN), F32), below, clipped[:, None]], axis=1)


def _attn_stage(q_ref, k_ref, v_ref, qs, kp, vp, T):
    qs[...] = (q_ref[...] * ATTN_HEAD_DIM ** -0.5).astype(qs.dtype)
    for dst, src in ((kp, k_ref), (vp, v_ref)):
        dst[pl.ds(0, PAD_ROWS), :] = jnp.zeros((PAD_ROWS, ATTN_HEAD_DIM), dst.dtype)
        dst[pl.ds(PAD_ROWS, T), :] = src[...].astype(dst.dtype)


def _attn_probs(g, first_groups, qs, kp, b_ref):
    r0 = g * GROUP_ROWS if first_groups else pl.multiple_of(g * GROUP_ROWS, GROUP_ROWS)
    qc = qs[pl.ds(r0, GROUP_ROWS), :]
    kb = kp[pl.ds(r0, GROUP_KEYS), :]
    s = _mxu(qc, kb, _NT) + b_ref[...]
    if first_groups:
        k_chunk = lax.shift_right_logical(lax.broadcasted_iota(jnp.int32, (GROUP_ROWS, GROUP_KEYS), 1), 6)
        s = jnp.where(g * GROUP + k_chunk >= ATTN_LEFT_CHUNKS, s, NEG_INF)
    e = jnp.exp(s - jnp.max(s, axis=-1, keepdims=True))
    return r0, qc, kb, e / jnp.sum(e, axis=-1, keepdims=True)


def _for_groups(T, step):
    n_first = min(ATTN_LEFT_CHUNKS // GROUP, T // GROUP_ROWS)
    for g in range(n_first):
        step(g, True)
    lax.fori_loop(n_first, T // GROUP_ROWS, lambda g, c: (step(g, False), c)[1], 0)


def _head_spec(T, base):
    return pl.BlockSpec((T, ATTN_HEAD_DIM), lambda h: (0, base + h))


def _attn_scratch(T):
    return [pltpu.VMEM((T, ATTN_HEAD_DIM), MXU_DTYPE)] + [pltpu.VMEM((T + PAD_ROWS, ATTN_HEAD_DIM), MXU_DTYPE)] * 2


def attn_fwd(proj, bias_tab, side=None):
    T = proj.shape[0]

    def body(q_ref, k_ref, v_ref, b_ref, o_ref, qs, kp, vp):
        _attn_stage(q_ref, k_ref, v_ref, qs, kp, vp, T)

        def step(g, first):
            r0, _, _, p = _attn_probs(g, first, qs, kp, b_ref)
            o_ref[pl.ds(r0, GROUP_ROWS), :] = _mxu(p, vp[pl.ds(r0, GROUP_KEYS), :], _NN).astype(o_ref.dtype)

        _for_groups(T, step)

    out, = _side_call(
        side, body,
        name="attn_fwd",
        grid=(ATTN_HEADS,),
        in_specs=[_head_spec(T, COL["aq"] // 128), _head_spec(T, COL["ak"] // 128), _head_spec(T, COL["av"] // 128),
                  pl.BlockSpec((None, GROUP_ROWS, GROUP_KEYS), lambda h: (h, 0, 0))],
        out_specs=[pl.BlockSpec((T, ATTN_HEAD_DIM), lambda h: (0, h))],
        out_shape=[jax.ShapeDtypeStruct((T, ATTN_HEADS * ATTN_HEAD_DIM), MXU_DTYPE)],
        scratch_shapes=_attn_scratch(T),
        args=[proj, proj, proj, bias_tab],
    )
    return out


def attn_bwd(proj, bias_tab, dout, dproj, side=None):
    T = proj.shape[0]

    def body(q_ref, k_ref, v_ref, b_ref, do_ref, dproj_in, dbias_ref, dproj_ref, qs, kp, vp, dkp, dvp, dq):
        del dproj_in
        h = pl.program_id(0)
        _attn_stage(q_ref, k_ref, v_ref, qs, kp, vp, T)
        dkp[...] = jnp.zeros_like(dkp)
        dvp[...] = jnp.zeros_like(dvp)
        dbias_ref[...] = jnp.zeros_like(dbias_ref)

        def step(g, first):
            r0, qc, kb, p = _attn_probs(g, first, qs, kp, b_ref)
            do = do_ref[pl.ds(r0, GROUP_ROWS), :]
            dvp[pl.ds(r0, GROUP_KEYS), :] += _mxu(p, do, _TN)
            dp = _mxu(do, vp[pl.ds(r0, GROUP_KEYS), :], _NT)
            ds = p * (dp - jnp.sum(dp * p, axis=-1, keepdims=True))
            dq[pl.ds(r0, GROUP_ROWS), :] = _mxu(ds, kb, _NN) * ATTN_HEAD_DIM ** -0.5
            dkp[pl.ds(r0, GROUP_KEYS), :] += _mxu(ds, qc, _TN)
            dbias_ref[...] += ds

        _for_groups(T, step)
        col = pl.multiple_of(h * ATTN_HEAD_DIM, ATTN_HEAD_DIM)
        pltpu.sync_copy(dq, dproj_ref.at[:, pl.ds(COL["aq"] + col, ATTN_HEAD_DIM)])
        pltpu.sync_copy(dkp.at[pl.ds(PAD_ROWS, T), :], dproj_ref.at[:, pl.ds(COL["ak"] + col, ATTN_HEAD_DIM)])
        pltpu.sync_copy(dvp.at[pl.ds(PAD_ROWS, T), :], dproj_ref.at[:, pl.ds(COL["av"] + col, ATTN_HEAD_DIM)])

    dbias, dproj = _side_call(
        side, body,
        name="attn_bwd",
        grid=(ATTN_HEADS,),
        in_specs=[_head_spec(T, COL["aq"] // 128), _head_spec(T, COL["ak"] // 128), _head_spec(T, COL["av"] // 128),
                  pl.BlockSpec((None, GROUP_ROWS, GROUP_KEYS), lambda h: (h, 0, 0)),
                  pl.BlockSpec((T, ATTN_HEAD_DIM), lambda h: (0, h)),
                  pl.BlockSpec(memory_space=pl.ANY)],
        out_specs=[pl.BlockSpec((None, GROUP_ROWS, GROUP_KEYS), lambda h: (h, 0, 0)), pl.BlockSpec(memory_space=pl.ANY)],
        out_shape=[jax.ShapeDtypeStruct((ATTN_HEADS, GROUP_ROWS, GROUP_KEYS), F32),
                   jax.ShapeDtypeStruct(dproj.shape, dproj.dtype)],
        scratch_shapes=_attn_scratch(T) + [pltpu.VMEM((T + PAD_ROWS, ATTN_HEAD_DIM), F32)] * 2
        + [pltpu.VMEM((T, ATTN_HEAD_DIM), F32)],
        args=[proj, proj, proj, bias_tab, dout, dproj],
        aliases={5: 1},
    )
    return dproj, dbias


def _lin_segment(q, k, v, la, st):
    lower = _tri(CHUNK, True)
    outs = []
    for i in range(SEG_CHUNKS):
        sl = slice(i * CHUNK, (i + 1) * CHUNK)
        qi, ki, vi, lai = q[sl], k[sl], v[sl], la[sl]
        cum = chunk_cumsum(lai)
        last = jnp.sum(lai, axis=0, keepdims=True)
        e_pos, e_neg = jnp.exp(cum), jnp.exp(-cum)
        q_fwd = qi * e_pos
        s_fwd = dot_nt(q_fwd, ki * e_neg)
        s_bwd = dot_nt(qi * e_neg, ki * e_pos)
        o = dot_nn(jnp.where(lower, s_fwd, s_bwd), vi) + dot_nt(q_fwd, st)
        st = jnp.exp(last) * st + dot_tn(vi, ki * jnp.exp(last - cum))
        outs.append(o)
    return jnp.concatenate(outs, axis=0), st


class Win:
    def __init__(self, arr, width, base=0):
        self.arr, self.width, self.base = arr, width, base


def _lin_spec(w, seg_of):
    base = w.base
    if w.arr.shape[0] == 1:
        return pl.BlockSpec((1, w.width), lambda h, s: (0, base + h))
    return pl.BlockSpec((SEG, w.width), lambda h, s: (seg_of(s), base + h))


def _decay_rows(la_ref):
    la = la_ref[...]
    return jnp.broadcast_to(la, (SEG, LIN_KEY_DIM)) if la.shape[0] == 1 else la


def lin_fwd(name, q, k, v, la, side=None):
    T = q.arr.shape[0]
    nseg = T // SEG

    def body(q_ref, k_ref, v_ref, la_ref, o_ref, st_ref, st):
        @pl.when(pl.program_id(1) == 0)
        def _():
            st[...] = jnp.zeros_like(st)

        st0 = st[...]
        st_ref[...] = st0
        o, st1 = _lin_segment(q_ref[...], k_ref[...], v_ref[...], _decay_rows(la_ref), st0)
        o_ref[...] = o
        st[...] = st1

    fwd = lambda s: s
    out, states = _side_call(
        side, body,
        name=name,
        grid=(LIN_HEADS, nseg),
        in_specs=[_lin_spec(w, fwd) for w in (q, k, v, la)],
        out_specs=[pl.BlockSpec((SEG, LIN_VAL_DIM), lambda h, s: (s, h)),
                   pl.BlockSpec((None, None, LIN_VAL_DIM, LIN_KEY_DIM), lambda h, s: (h, s, 0, 0))],
        out_shape=[jax.ShapeDtypeStruct((T, LIN_HEADS * LIN_VAL_DIM), F32),
                   jax.ShapeDtypeStruct((LIN_HEADS, nseg, LIN_VAL_DIM, LIN_KEY_DIM), F32)],
        scratch_shapes=[pltpu.VMEM((LIN_VAL_DIM, LIN_KEY_DIM), F32)],
        args=[q.arr, k.arr, v.arr, la.arr],
    )
    return out, states


def lin_bwd(name, q, k, v, la, states, dout, with_decay_grad, side=None):
    T = q.arr.shape[0]
    nseg = T // SEG
    n_out = 4 if with_decay_grad else 3

    def body(q_ref, k_ref, v_ref, la_ref, st_ref, do_ref, *rest):
        outs, dst = rest[:n_out], rest[n_out]

        @pl.when(pl.program_id(1) == 0)
        def _():
            dst[...] = jnp.zeros_like(dst)

        _, pull = jax.vjp(_lin_segment, q_ref[...].astype(F32), k_ref[...].astype(F32), v_ref[...].astype(F32),
                          _decay_rows(la_ref), st_ref[...])
        grads = pull((do_ref[...], dst[...]))
        for o, g in zip(outs, grads[:n_out]):
            o[...] = g
        dst[...] = grads[4]

    rev = lambda s: nseg - 1 - s
    key_spec = pl.BlockSpec((SEG, LIN_KEY_DIM), lambda h, s: (rev(s), h))
    val_spec = pl.BlockSpec((SEG, LIN_VAL_DIM), lambda h, s: (rev(s), h))
    out_specs = [key_spec, key_spec, val_spec] + ([key_spec] if with_decay_grad else [])
    widths = [LIN_KEY_DIM, LIN_KEY_DIM, LIN_VAL_DIM] + ([LIN_KEY_DIM] if with_decay_grad else [])
    return _side_call(
        side, body,
        name=name,
        grid=(LIN_HEADS, nseg),
        in_specs=[_lin_spec(w, rev) for w in (q, k, v, la)] + [
            pl.BlockSpec((None, None, LIN_VAL_DIM, LIN_KEY_DIM), lambda h, s: (h, rev(s), 0, 0)), val_spec],
        out_specs=out_specs,
        out_shape=[jax.ShapeDtypeStruct((T, LIN_HEADS * w), F32) for w in widths],
        scratch_shapes=[pltpu.VMEM((LIN_VAL_DIM, LIN_KEY_DIM), F32)],
        args=[q.arr, k.arr, v.arr, la.arr, states, dout],
    )


def _coords():
    return lax.axis_index("x"), lax.axis_index("y"), lax.axis_index("c")


def _peer(k):
    x, y, c = _coords()
    px = 1 - x if k & 4 else x
    py = 1 - y if k & 2 else y
    pc = 1 - c if k & 1 else c
    return (px, py, pc), 4 * px + 2 * py + pc


def _remote(src, dst, send_sem, recv_sem, device):
    return pltpu.make_async_remote_copy(src_ref=src, dst_ref=dst, send_sem=send_sem, recv_sem=recv_sem,
                                        device_id=device, device_id_type=pl.DeviceIdType.MESH)


class Side:
    def __init__(self, kind, arrays):
        self.kind, self.arrays, self.n = kind, list(arrays), len(arrays)
        self.outs_per = 1
        self.slots = {"gather": N_DEV - 1, "pair": 4, "chips": 3}[kind]

    def out_shape(self):
        if self.kind == "gather":
            return [jax.ShapeDtypeStruct((N_DEV,) + a.shape, a.dtype) for a in self.arrays]
        return [jax.ShapeDtypeStruct((4,) + a.shape[1:], a.dtype) for a in self.arrays]

    def sems(self):
        return [pltpu.SemaphoreType.DMA((self.n, self.slots)), pltpu.SemaphoreType.DMA((self.n, self.slots)),
                pltpu.SemaphoreType.DMA((self.n,))]

    def start(self, ins, outs, sems):
        send, recv, own = sems
        x, y, c = _coords()
        me = 4 * x + 2 * y + c
        for a in range(self.n):
            if self.kind == "gather":
                pltpu.make_async_copy(ins[a], outs[a].at[me], own.at[a]).start()
                for k in (1, 2, 4, 6):
                    peer, _ = _peer(k)
                    _remote(ins[a], outs[a].at[me], send.at[a, k - 1], recv.at[a, k - 1], peer).start()
            elif self.kind == "pair":
                for j in range(4):
                    _remote(ins[a].at[2 * j + 1 - c], outs[a].at[j], send.at[a, j], recv.at[a, j], (x, y, 1 - c)).start()
            else:
                chip = 2 * x + y
                pltpu.make_async_copy(ins[a].at[chip], outs[a].at[chip], own.at[a]).start()
                for d in (1, 2, 3):
                    tx, ty = (1 - x if d & 2 else x), (1 - y if d & 1 else y)
                    _remote(ins[a].at[2 * tx + ty], outs[a].at[chip], send.at[a, d - 1], recv.at[a, d - 1], (tx, ty, c)).start()

    def finish(self, ins, outs, sems):
        send, recv, own = sems
        x, y, c = _coords()
        me = 4 * x + 2 * y + c
        sibling = (x, y, 1 - c)
        for a in range(self.n):
            if self.kind == "gather":
                blk = outs[a].at[me]
                for k in (2, 4, 6):
                    _, pidx = _peer(k)
                    _remote(blk, blk, send.at[a, k - 1], recv.at[a, k - 1], sibling).wait_recv()
                    _remote(outs[a].at[pidx], outs[a].at[pidx], send.at[a, k], recv.at[a, k], sibling).start()
                pltpu.make_async_copy(ins[a], blk, own.at[a]).wait()
                for k in range(1, N_DEV):
                    _remote(blk, blk, send.at[a, k - 1], recv.at[a, k - 1], sibling).wait_send()
                for k in (1, 3, 5, 7):
                    _remote(blk, blk, send.at[a, k - 1], recv.at[a, k - 1], sibling).wait_recv()
            elif self.kind == "pair":
                for j in range(4):
                    _remote(ins[a].at[j], outs[a].at[j], send.at[a, j], recv.at[a, j], sibling).wait()
            else:
                slab = outs[a].at[0]
                pltpu.make_async_copy(slab, slab, own.at[a]).wait()
                for d in (1, 2, 3):
                    _remote(slab, slab, send.at[a, d - 1], recv.at[a, d - 1], sibling).wait()


def _side_call(side, body, *, name, grid, in_specs, out_specs, out_shape, scratch_shapes, args, semantics=None,
               aliases=None):
    in_specs, out_specs, out_shape = list(in_specs), list(out_specs), list(out_shape)
    scratch_shapes, args = list(scratch_shapes), list(args)
    n_in, n_out = len(in_specs), len(out_specs)
    kernel_body = body
    if side is not None:
        n_so = side.n * side.outs_per

        def kernel_body(*refs):
            ins, side_ins = refs[:n_in], refs[n_in:n_in + side.n]
            pos = n_in + side.n
            outs, side_outs = refs[pos:pos + n_out], refs[pos + n_out:pos + n_out + n_so]
            rest = refs[pos + n_out + n_so:]
            scratch, sems = rest[:-3], rest[-3:]
            ids = [pl.program_id(ax) for ax in range(len(grid))]
            first = functools.reduce(jnp.logical_and, [i == 0 for i in ids])
            last = functools.reduce(jnp.logical_and, [i == g - 1 for i, g in zip(ids, grid)])

            @pl.when(first)
            def _():
                side.start(side_ins, side_outs, sems)

            body(*ins, *outs, *scratch)

            @pl.when(last)
            def _():
                side.finish(side_ins, side_outs, sems)

        any_spec = pl.BlockSpec(memory_space=pl.ANY)
        in_specs += [any_spec] * side.n
        args += side.arrays
        out_specs += [any_spec] * n_so
        out_shape += side.out_shape()
        scratch_shapes += side.sems()
        semantics = ("arbitrary",) * len(grid)
    outs = pl.pallas_call(
        kernel_body,
        name=name,
        grid=grid,
        in_specs=in_specs,
        out_specs=out_specs,
        out_shape=out_shape,
        scratch_shapes=scratch_shapes,
        input_output_aliases=aliases or {},
        compiler_params=pltpu.CompilerParams(dimension_semantics=semantics or ("arbitrary",) * len(grid),
                                             vmem_limit_bytes=VMEM_BIG),
    )(*args)
    if side is not None:
        side.results = list(outs[n_out:])
    return list(outs[:n_out])


def carrier(name, side):
    _side_call(side, lambda: None, name=name, grid=(1,), in_specs=[], out_specs=[], out_shape=[], scratch_shapes=[], args=[])
    return side.results


def all_reduce_small(vec):
    def body(v_ref, o_ref, land, send_sems, recv_sems):
        x, y, c = _coords()
        me = 4 * x + 2 * y + c
        land[me] = v_ref[...]
        copies = []
        for k in range(1, N_DEV):
            peer, _ = _peer(k)
            cp = pltpu.make_async_remote_copy(
                src_ref=v_ref, dst_ref=land.at[me], send_sem=send_sems.at[k - 1], recv_sem=recv_sems.at[k - 1],
                device_id=peer, device_id_type=pl.DeviceIdType.MESH)
            cp.start()
            copies.append(cp)
        for cp in copies:
            cp.wait()
        total = land[0]
        for s in range(1, N_DEV):
            total = total + land[s]
        o_ref[...] = total

    return pl.pallas_call(
        body,
        name="all_reduce_small",
        in_specs=[pl.BlockSpec(memory_space=pltpu.VMEM)],
        out_specs=pl.BlockSpec(memory_space=pltpu.VMEM),
        out_shape=jax.ShapeDtypeStruct(vec.shape, F32),
        scratch_shapes=[pltpu.VMEM((N_DEV,) + vec.shape, F32), pltpu.SemaphoreType.DMA((N_DEV - 1,)),
                        pltpu.SemaphoreType.DMA((N_DEV - 1,))],
    )(vec)


def _pack(parts):
    flat = jnp.concatenate([p.reshape(-1).astype(F32) for p in parts])
    n = flat.shape[0]
    rows = -(-n // 1024) * 8
    return jnp.pad(flat, (0, rows * 128 - n)).reshape(rows, 128)


def _unpack(packed, shapes):
    flat, out, pos = packed.reshape(-1), [], 0
    for s in shapes:
        n = math.prod(s)
        out.append(flat[pos:pos + n].reshape(s))
        pos += n
    return out


def _row2(v):
    return v.reshape(1, -1)


def _rider(hooks, name, ctx=None):
    if name not in hooks:
        return None, lambda: None
    side, done = hooks[name](ctx)
    return side, lambda: done(side.results)


def _layer_forward(h, W, consts, hooks):
    T = h.shape[0]
    cos, sin, ret_decay = consts
    side, landed = _rider(hooks, "proj")
    proj = mm("proj", h, W["in"], "nn", tm=1024, tn=512, tk=1024, side=side)
    landed()
    bias_tab = bias_table(W["rel_bias"])
    side, landed = _rider(hooks, "attn_fwd")
    attn = attn_fwd(proj, bias_tab, side=side)
    landed()
    rq2, rk2 = row_call("ret_pre", f_ret_pre, [Row(proj, 512, COL["rq"] // 512), Row(proj, 512, COL["rk"] // 512),
                                              Row(cos), Row(sin)], [], [(512, F32), (512, F32)], tm=512)
    ret_raw, ret_st = lin_fwd("ret_fwd", Win(rq2, 128), Win(rk2, 128), Win(proj, 256, COL["rv"] // 256), Win(ret_decay, 128))
    gq2, gla_la = row_call("gla_pre", f_gla_pre, [Row(proj, 512, COL["gq"] // 512), Row(proj, GLR_BLOCK, GLR_OFF // GLR_BLOCK)],
                           [W["wlr"], W["blr"]], [(512, F32), (512, F32)], tm=512)
    side, landed = _rider(hooks, "gla_fwd")
    gla_raw, gla_st = lin_fwd("gla_fwd", Win(gq2, 128), Win(proj, 128, COL["gk"] // 128), Win(proj, 256, COL["gv"] // 256),
                              Win(gla_la, 128), side=side)
    landed()
    ret_b, gla_b = row_call("normgate", f_normgate, [Row(ret_raw), Row(proj, 1024, COL["rg"] // 1024), Row(gla_raw),
                                                    Row(proj, 1024, COL["gg"] // 1024)], [W["gnorm"]],
                            [(1024, MXU_DTYPE), (1024, MXU_DTYPE)], tm=256)
    branches = (attn, ret_b, gla_b)
    p = [mm("branch_proj", branches[n], W["branch"][n], "nn", tm=1024, tn=1024, tk=1024) for n in range(3)]
    merged, = row_call("merge", f_merge, [Row(p[0]), Row(p[1]), Row(p[2]), Row(proj, GATES_W, GATES_OFF // GATES_W)], [],
                       [(1024, MXU_DTYPE)], tm=256)
    mix = mm("out_proj", merged, W["out"], "nn", tm=1024, tn=1024, tk=1024)
    x1, = row_call("ln1", f_deepnorm, [Row(h), Row(mix)], [W["ln1_g"], W["ln1_b"]], [(1024, F32)], tm=512)
    side, landed = _rider(hooks, "mlp_up")
    u, act = mm("mlp_up", x1, W["up"], "nn", tm=1024, tn=512, tk=1024, n_out=2, out_dtypes=(F32, MXU_DTYPE),
                epi=lambda r: (r, jnp.square(jnp.maximum(r, 0.0))), side=side)
    landed()
    y = mm("mlp_down", act, W["down"], "nn", tm=1024, tn=1024, tk=1024)
    x2, = row_call("ln2", f_deepnorm, [Row(x1), Row(y)], [W["ln2_g"], W["ln2_b"]], [(1024, F32)], tm=512)
    saved = dict(h=h, proj=proj, bias_tab=bias_tab, rq2=rq2, rk2=rk2, ret_st=ret_st, gq2=gq2, gla_la=gla_la, gla_st=gla_st,
                 ret_raw=ret_raw, gla_raw=gla_raw, branches=branches, p=p, merged=merged, mix=mix, x1=x1, u=u, act=act, y=y)
    return x2, saved


def _layer_backward(dx2, S, W, consts, hooks):
    cos, sin, ret_decay = consts
    h, proj, T = S["h"], S["proj"], S["h"].shape[0]
    add = lambda r, e: r + e
    dx1a, dy, dg2, db2 = row_call("ln2_bwd", vjp_fn(f_deepnorm, 2, 1), [Row(S["x1"]), Row(S["y"]), Row(dx2)],
                                  [W["ln2_g"], W["ln2_b"]], [(1024, F32), (1024, F32)], [(1, 1024), (1, 1024)], tm=256)
    du = mm("mlp_down_bwd", dy, W["down"], "nt", tm=1024, tn=512, tk=1024, extras=(S["u"],),
            epi=lambda r, u: r * (2.0 * jnp.maximum(u, 0.0)))
    g_down = mm("mlp_down_wgrad", S["act"], dy, "tn", tm=1024, tn=1024, tk=512, out_dtype=WIRE_DTYPE)
    dx1 = mm("mlp_up_bwd", du, W["up"], "nt", tm=512, tn=1024, tk=1024, extras=(dx1a,), epi=add)
    g_up = mm("mlp_up_wgrad", S["x1"], du, "tn", tm=1024, tn=1024, tk=512, out_dtype=WIRE_DTYPE)
    dha, dmix, dg1, db1 = row_call("ln1_bwd", vjp_fn(f_deepnorm, 2, 1), [Row(h), Row(S["mix"]), Row(dx1)],
                                   [W["ln1_g"], W["ln1_b"]], [(1024, F32), (1024, F32)], [(1, 1024), (1, 1024)], tm=256)
    dmerged = mm("out_proj_bwd", dmix, W["out"], "nt", tm=1024, tn=1024, tk=1024)
    g_out = mm("out_proj_wgrad", S["merged"], dmix, "tn", tm=1024, tn=1024, tk=512, out_dtype=WIRE_DTYPE)
    dproj = jax.ShapeDtypeStruct((T, PROJ_PAD), F32)
    p = S["p"]
    dp0, dp1, dp2, dproj = row_call(
        "merge_bwd", vjp_fn(f_merge, 4, 1), [Row(p[0]), Row(p[1]), Row(p[2]), Row(proj, GATES_W, GATES_OFF // GATES_W), Row(dmerged)],
        [], [(1024, F32)] * 3, tm=256, into=dproj, into_cols=[(GATES_W, GATES_OFF)])
    dps = (dp0, dp1, dp2)
    dbr = [mm("branch_proj_bwd", dps[n], W["branch"][n], "nt", tm=1024, tn=1024, tk=1024) for n in range(3)]
    g_branch = [mm("branch_proj_wgrad", S["branches"][n], dps[n], "tn", tm=1024, tn=1024, tk=512, out_dtype=WIRE_DTYPE)
                for n in range(3)]
    dret_raw, dgla_raw, dgnorm, dproj = row_call(
        "normgate_bwd", vjp_fn(lambda a, b, c, d, g: _reorder_normgate(a, b, c, d, g), 4, 2),
        [Row(S["ret_raw"]), Row(S["gla_raw"]), Row(proj, 1024, COL["rg"] // 1024), Row(proj, 1024, COL["gg"] // 1024),
         Row(dbr[1]), Row(dbr[2])], [W["gnorm"]], [(1024, F32), (1024, F32)], [(1, LIN_VAL_DIM)], tm=256,
        into=dproj, into_cols=[(1024, COL["rg"]), (1024, COL["gg"])])
    side, landed = _rider(hooks, "gla_bwd")
    dgq2, dgk, dgv, dgla_la = lin_bwd("gla_bwd", Win(S["gq2"], 128), Win(proj, 128, COL["gk"] // 128),
                                      Win(proj, 256, COL["gv"] // 256), Win(S["gla_la"], 128), S["gla_st"], dgla_raw, True,
                                      side=side)
    landed()
    dwlr, dblr, dproj = row_call(
        "gla_pre_bwd", _gla_pre_bwd, [Row(proj, 512, COL["gq"] // 512), Row(proj, GLR_BLOCK, GLR_OFF // GLR_BLOCK),
                                      Row(dgq2), Row(dgla_la), Row(dgk), Row(dgv)], [W["wlr"], W["blr"]], [],
        [(GLR_BLOCK, 512), (1, 512)], tm=256, into=dproj,
        into_cols=[(512, COL["gq"]), (PROJ_PAD - GLR_OFF, GLR_OFF), (512, COL["gk"]), (1024, COL["gv"])])
    drq2, drk2, drv = lin_bwd("ret_bwd", Win(S["rq2"], 128), Win(S["rk2"], 128), Win(proj, 256, COL["rv"] // 256),
                              Win(ret_decay, 128), S["ret_st"], dret_raw, False)
    dproj, = row_call(
        "ret_pre_bwd", _ret_pre_bwd, [Row(proj, 512, COL["rq"] // 512), Row(proj, 512, COL["rk"] // 512), Row(cos), Row(sin),
                                      Row(drq2), Row(drk2), Row(drv)], [], [], tm=256, into=dproj,
        into_cols=[(512, COL["rq"]), (512, COL["rk"]), (1024, COL["rv"])])
    side, landed = _rider(hooks, "attn_bwd")
    dproj, dbias = attn_bwd(proj, S["bias_tab"], dbr[0], dproj, side=side)
    landed()
    d_rel_bias = bias_table_grad(dbias)
    side, landed = _rider(hooks, "proj_bwd", dict(branch=g_branch, out=g_out, up=g_up, down=g_down))
    dh = mm("proj_bwd", dproj, W["in"], "nt", tm=512, tn=1024, tk=1280, extras=(dha,), epi=add, side=side)
    landed()
    g_in = mm("proj_wgrad", h, dproj, "tn", tm=1024, tn=1280, tk=512, out_dtype=WIRE_DTYPE)
    big = dict(g_in=g_in, branch=g_branch, out=g_out, up=g_up, down=g_down)
    small = dict(rel_bias=d_rel_bias, wlr=dwlr[:GLA_RANK], blr=dblr, gnorm=dgnorm, ln1_g=dg1, ln1_b=db1, ln2_g=dg2, ln2_b=db2)
    return dh, big, small


def _reorder_normgate(ret_raw, gla_raw, rg, gg, gnorm):
    return f_normgate(ret_raw, rg, gla_raw, gg, gnorm)


def _gla_pre_bwd(gq, glr, dq2, dla, dgk, dgv, wlr, blr):
    _, pull = jax.vjp(f_gla_pre, gq, glr, wlr, blr)
    dgq, dglr, dwlr, dblr = pull((dq2, dla))
    dglr = jnp.concatenate([dglr, jnp.zeros((dglr.shape[0], PROJ_PAD - GLR_OFF - GLR_BLOCK), F32)], axis=1)
    return dgq, dglr, dgk, dgv, dwlr, dblr


def _ret_pre_bwd(rq, rk, cos, sin, drq2, drk2, drv):
    _, pull = jax.vjp(lambda a, b: f_ret_pre(a, b, cos, sin), rq, rk)
    drq, drk = pull((drq2, drk2))
    return drq, drk, drv


SHARDED = ("in", "branch", "out", "up", "down")

_FULL = {
    "branch": lambda g: [g[:, n].reshape(D_MODEL, D_MODEL) for n in range(3)],
    "out": lambda g: g.reshape(D_MODEL, D_MODEL),
    "up": lambda g: jnp.transpose(g, (1, 0, 2)).reshape(D_MODEL, D_FF),
    "down": lambda g: g.reshape(D_FF, D_MODEL),
}


def _owner_pieces(s):
    shard = PROJ_WIDTH // N_DEV
    lo, hi = s * shard, (s + 1) * shard
    pieces = []
    for a, b, padded in ((0, GLR_SRC, 0), (GLR_SRC, GATES_SRC, GLR_OFF), (GATES_SRC, PROJ_WIDTH, GATES_OFF)):
        x0, x1 = max(lo, a), min(hi, b)
        if x0 < x1:
            pieces.append((x0 - lo, x1 - x0, padded + x0 - a))
    return pieces


def _full_in(g):
    placed = sorted((p, s, c, w) for s in range(N_DEV) for c, w, p in _owner_pieces(s))
    cols = [g[s][:, c:c + w] for _, s, c, w in placed]
    return jnp.concatenate(cols + [jnp.zeros((D_MODEL, PROJ_PAD - PROJ_WIDTH), g.dtype)], axis=1)


def _full_wlr(g):
    wlr = jnp.transpose(g, (1, 0, 2)).reshape(GLA_RANK, LIN_HEADS * LIN_KEY_DIM)
    return jnp.pad(wlr, ((0, GLR_BLOCK - GLA_RANK), (0, 0)))


def _slab_in(g):
    return jnp.stack([jnp.concatenate([g[:, p:p + w] for _, w, p in _owner_pieces(s)], axis=1) for s in range(N_DEV)])


_SLABS = {
    "in": _slab_in,
    "branch": lambda gs: jnp.stack([b.reshape(N_DEV, D_MODEL // N_DEV, D_MODEL) for b in gs], axis=1),
    "out": lambda g: g.reshape(N_DEV, D_MODEL // N_DEV, D_MODEL),
    "up": lambda g: jnp.transpose(g.reshape(D_MODEL, N_DEV, D_FF // N_DEV), (1, 0, 2)),
    "down": lambda g: g.reshape(N_DEV, D_FF // N_DEV, D_MODEL),
}


def _pair_sums(name, slabs):
    sums = []
    for mine, sibs in zip(slabs, carrier(name, Side("pair", slabs))):
        width = mine.shape[-1]
        rows = math.prod(mine.shape[1:-1])
        tm = 256 if rows % 256 == 0 else 128
        nb = rows // tm
        my_rows = Row(mine.reshape(-1, width), rowmap=lambda i, nb=nb: (2 * (i // nb) + lax.axis_index("c")) * nb + i % nb)
        out, = row_call("pair_sum", lambda q, p: (p.astype(F32) + q.astype(F32),),
                        [Row(sibs.reshape(-1, width)), my_rows], [], [(width, WIRE_DTYPE)], tm=tm)
        sums.append(out.reshape(sibs.shape))
    return sums


def _sum_adam(name, parts, w, m, v, tm):
    C = w.shape[-1]
    rows = [Row(parts, lead=(s,)) for s in range(parts.shape[0])] + [Row(w), Row(m), Row(v)]
    return row_call(name, f_sum_adam, rows, [], [(C, F32)] * 4, tm=tm)


def kernel(x, ln_in_g, ln_in_b, w_in, rel_bias, gla_w_lr, gla_b_lr, gla_norm_g, w_branch, w_out, ln1_g, ln1_b, w_up, w_down, ln2_g, ln2_b, loss_target, m_ln_in_g, m_ln_in_b, m_w_in, m_rel_bias, m_gla_w_lr, m_gla_b_lr, m_gla_norm_g, m_w_branch, m_w_out, m_ln1_g, m_ln1_b, m_w_up, m_w_down, m_ln2_g, m_ln2_b, v_ln_in_g, v_ln_in_b, v_w_in, v_rel_bias, v_gla_w_lr, v_gla_b_lr, v_gla_norm_g, v_w_branch, v_w_out, v_ln1_g, v_ln1_b, v_w_up, v_w_down, v_ln2_g, v_ln2_b):
    T = x.shape[1]
    x2d = x.reshape(T, D_MODEL)
    tgt = loss_target.reshape(T, D_MODEL)
    me = 4 * lax.axis_index("x") + 2 * lax.axis_index("y") + lax.axis_index("c")

    inv = ROPE_BASE ** (-jnp.arange(0, LIN_KEY_DIM, 2, dtype=F32) / LIN_KEY_DIM)
    ang = jnp.arange(T, dtype=F32)[:, None] * inv[None, :]
    cos = jnp.concatenate([jnp.cos(ang), jnp.cos(ang)], axis=1)
    sin = jnp.concatenate([-jnp.sin(ang), jnp.sin(ang)], axis=1)
    log_gamma = jnp.log1p(-jnp.exp2(-5.0 - jnp.arange(LIN_HEADS, dtype=F32)))
    ret_decay = jnp.repeat(log_gamma, LIN_KEY_DIM).reshape(1, LIN_HEADS * LIN_KEY_DIM)
    consts = (cos, sin, ret_decay)

    layers = [dict(rel_bias=rel_bias[l], blr=_row2(gla_b_lr[l]), gnorm=_row2(gla_norm_g[l]), ln1_g=_row2(ln1_g[l]),
                   ln1_b=_row2(ln1_b[l]), ln2_g=_row2(ln2_g[l]), ln2_b=_row2(ln2_b[l])) for l in range(DEPTH)]
    shards = dict(branch=w_branch, out=w_out, up=w_up, down=w_down)
    shards["in"] = w_in
    wire = [{n: shards[n][l].astype(WIRE_DTYPE) for n in SHARDED} for l in range(DEPTH)]

    def gather_of(l, names):
        def landed(res):
            for n, g in zip(names, res):
                layers[l][n] = _full_in(g) if n == "in" else _FULL[n](g)
        return lambda ctx: (Side("gather", [wire[l][n] for n in names]), landed)

    first = carrier("gather_first", Side("gather", [wire[0]["in"]] + [gla_w_lr[l] for l in range(DEPTH)]))
    layers[0]["in"] = _full_in(first[0])
    for l in range(DEPTH):
        layers[l]["wlr"] = _full_wlr(first[1 + l])

    fwd_hooks = [{"proj": gather_of(0, ("branch", "out", "up", "down")), "attn_fwd": gather_of(1, ("in",)),
                  "gla_fwd": gather_of(1, ("branch", "out")), "mlp_up": gather_of(1, ("up", "down"))}, {}]
    h, = row_call("ln_in", f_ln_in, [Row(x2d)], [_row2(ln_in_g), _row2(ln_in_b)], [(D_MODEL, F32)], tm=512)
    saved = []
    for l in range(DEPTH):
        h, s = _layer_forward(h, layers[l], consts, fwd_hooks[l])
        saved.append(s)
    dh, loss_lanes = row_call("loss", f_loss, [Row(h), Row(tgt)], [], [(D_MODEL, F32)], [(1, 128)], tm=512)

    received = [dict() for _ in range(DEPTH)]

    def scatter_of(l, names, sums=None):
        def make(ctx):
            parts = sums if sums is not None else _pair_sums("pair_grads", [_SLABS[n](ctx[n]) for n in names])

            def landed(res):
                received[l].update(zip(names, res))
            return Side("chips", parts), landed
        return make

    small = [None] * DEPTH
    dh, big, small[1] = _layer_backward(dh, saved[1], layers[1], consts, {})
    sums1 = _pair_sums("pair_grads", [_SLABS[n](big["g_in"] if n == "in" else big[n]) for n in SHARDED])
    rest = SHARDED[1:]
    dh, big, small[0] = _layer_backward(dh, saved[0], layers[0], consts, {
        "gla_bwd": scatter_of(1, ("in",), sums1[:1]), "attn_bwd": scatter_of(1, rest, sums1[1:]),
        "proj_bwd": scatter_of(0, rest)})
    last = Side("chips", _pair_sums("pair_grads", [_slab_in(big["g_in"])]))
    received[0]["in"] = carrier("scatter_last", last)[0]
    grad_x, d_in_g, d_in_b = row_call("ln_in_bwd", vjp_fn(f_ln_in, 1, 1), [Row(x2d), Row(dh)],
                                      [_row2(ln_in_g), _row2(ln_in_b)], [(D_MODEL, F32)], [(1, D_MODEL), (1, D_MODEL)], tm=256)

    small_names = ("rel_bias", "wlr", "blr", "gnorm", "ln1_g", "ln1_b", "ln2_g", "ln2_b")
    parts = [d_in_g, d_in_b] + [small[l][n] for l in range(DEPTH) for n in small_names] + [loss_lanes]
    shapes = [(D_MODEL,), (D_MODEL,)] + [s for l in range(DEPTH) for s in (
        (ATTN_HEADS, N_REL), (GLA_RANK, 512), (512,), (LIN_VAL_DIM,), (D_MODEL,), (D_MODEL,), (D_MODEL,), (D_MODEL,))] + [(128,)]
    red = _unpack(all_reduce_small(_pack(parts)), shapes)
    loss = jnp.sum(red[-1])
    g_small = {"ln_in_g": red[0], "ln_in_b": red[1]}
    for l in range(DEPTH):
        for j, n in enumerate(small_names):
            g_small[(l, n)] = red[2 + l * len(small_names) + j]
    shard = LIN_HEADS * LIN_KEY_DIM // N_DEV
    g_wlr = jnp.stack([lax.dynamic_slice_in_dim(g_small[(l, "wlr")], me * shard, shard, axis=1) for l in range(DEPTH)])
    stack = lambda n: jnp.stack([g_small[(l, n)] for l in range(DEPTH)])
    g_rep = [g_small["ln_in_g"], g_small["ln_in_b"], stack("rel_bias"), g_wlr, stack("blr"), stack("gnorm"),
             stack("ln1_g"), stack("ln1_b"), stack("ln2_g"), stack("ln2_b")]
    w_rep = [ln_in_g, ln_in_b, rel_bias, gla_w_lr, gla_b_lr, gla_norm_g, ln1_g, ln1_b, ln2_g, ln2_b]
    m_rep = [m_ln_in_g, m_ln_in_b, m_rel_bias, m_gla_w_lr, m_gla_b_lr, m_gla_norm_g, m_ln1_g, m_ln1_b, m_ln2_g, m_ln2_b]
    v_rep = [v_ln_in_g, v_ln_in_b, v_rel_bias, v_gla_w_lr, v_gla_b_lr, v_gla_norm_g, v_ln1_g, v_ln1_b, v_ln2_g, v_ln2_b]
    rep_shapes = [w.shape for w in w_rep]
    d_pack, m_pack, v_pack = row_call("adam_small", f_adam, [Row(_pack(g_rep)), Row(_pack(w_rep)), Row(_pack(m_rep)), Row(_pack(v_rep))],
                                      [], [(128, F32)] * 3, tm=256)
    d_rep, nm_rep, nv_rep = _unpack(d_pack, rep_shapes), _unpack(m_pack, rep_shapes), _unpack(v_pack, rep_shapes)

    def owned(key, w, m, v, name, tm):
        res = []
        for l in range(DEPTH):
            shp = w[l].shape
            flat = lambda a: a.reshape(-1, shp[-1])
            parts_l = received[l][key].reshape(4, -1, shp[-1])
            res.append([o.reshape(shp) for o in _sum_adam(name, parts_l, flat(w[l]), flat(m[l]), flat(v[l]), tm)])
        return [jnp.stack([res[l][j] for l in range(DEPTH)]) for j in range(4)]

    o_in = owned("in", w_in, m_w_in, v_w_in, "adam_w_in", 256)
    o_br = owned("branch", w_branch, m_w_branch, v_w_branch, "adam_w_branch", 128)
    o_out = owned("out", w_out, m_w_out, v_w_out, "adam_w_out", 128)
    o_up = owned("up", w_up, m_w_up, v_w_up, "adam_w_up", 256)
    o_down = owned("down", w_down, m_w_down, v_w_down, "adam_w_down", 256)

    def tree(j, rep):
        return [rep[0], rep[1], o_in[j], rep[2], rep[3], rep[4], rep[5], o_br[j], o_out[j], rep[6], rep[7], o_up[j], o_down[j],
                rep[8], rep[9]]

    return (loss, grad_x.reshape(1, T, D_MODEL), *tree(0, g_rep), *tree(1, d_rep), *tree(2, nm_rep), *tree(3, nv_rep))
```

```python
import functools
import math

import jax
import jax.numpy as jnp
from jax import lax
from jax.experimental import pallas as pl
from jax.experimental.pallas import tpu as pltpu

F32 = jnp.float32
MXU_DTYPE = jnp.bfloat16
WIRE_DTYPE = jnp.bfloat16

N_DEV = 8
D_MODEL = 1024
DEPTH = 2
CHUNK = 64
ATTN_HEADS = 8
ATTN_HEAD_DIM = 128
ATTN_LEFT_CHUNKS = 8
BAND = (ATTN_LEFT_CHUNKS + 1) * CHUNK
PAD_ROWS = ATTN_LEFT_CHUNKS * CHUNK
REL_CLIP = 2 * CHUNK
N_REL = 2 * REL_CLIP + 1
REL_MIN = REL_CLIP - (CHUNK - 1)
LIN_HEADS = 4
LIN_KEY_DIM = 128
LIN_VAL_DIM = 256
GLA_RANK = 16
GLA_GATE_NORMALIZER = 16.0
ROPE_BASE = 10000.0
D_FF = 4 * D_MODEL
LN_EPS = 1e-5
NEG_INF = -1e30
ALPHA = (2 * DEPTH) ** 0.25
ADAM_LR, ADAM_B1, ADAM_B2, ADAM_EPS, ADAM_WD, ADAM_STEP = 0.001, 0.9, 0.999, 1e-08, 0.01, 10

PROJ_WIDTH = 12304
GLR_SRC = 9216
GATES_SRC = GLR_SRC + GLA_RANK
MAIN_W = 9216
GATES_W = 3 * D_MODEL
GATES_OFF = MAIN_W
GLR_OFF = MAIN_W + GATES_W
PROJ_PAD = 12800
GLR_BLOCK = 128
COL = dict(aq=0, ak=1024, av=2048, rq=3072, rk=3584, rv=4096, rg=5120, gq=6144, gk=6656, gv=7168, gg=8192)

SEG_CHUNKS = 8
SEG = SEG_CHUNKS * CHUNK
VMEM_BIG = 56 * 1024 * 1024

_NN = (((1,), (0,)), ((), ()))
_NT = (((1,), (1,)), ((), ()))
_TN = (((0,), (0,)), ((), ()))


def _mxu(a, b, dn):
    return lax.dot_general(a.astype(MXU_DTYPE), b.astype(MXU_DTYPE), dn, preferred_element_type=F32)


@jax.custom_vjp
def dot_nn(a, b):
    return _mxu(a, b, _NN)


def _dot_nn_fwd(a, b):
    return _mxu(a, b, _NN), (a, b)


def _dot_nn_bwd(res, g):
    a, b = res
    return _mxu(g, b, _NT).astype(a.dtype), _mxu(a, g, _TN).astype(b.dtype)


dot_nn.defvjp(_dot_nn_fwd, _dot_nn_bwd)


@jax.custom_vjp
def dot_nt(a, b):
    return _mxu(a, b, _NT)


def _dot_nt_fwd(a, b):
    return _mxu(a, b, _NT), (a, b)


def _dot_nt_bwd(res, g):
    a, b = res
    return _mxu(g, b, _NN).astype(a.dtype), _mxu(g, a, _TN).astype(b.dtype)


dot_nt.defvjp(_dot_nt_fwd, _dot_nt_bwd)


@jax.custom_vjp
def dot_tn(a, b):
    return _mxu(a, b, _TN)


def _dot_tn_fwd(a, b):
    return _mxu(a, b, _TN), (a, b)


def _dot_tn_bwd(res, g):
    a, b = res
    return _mxu(b, g, _NT).astype(a.dtype), _mxu(a, g, _NN).astype(b.dtype)


dot_tn.defvjp(_dot_tn_fwd, _dot_tn_bwd)


def _tri(n, lower):
    r = lax.broadcasted_iota(jnp.int32, (n, n), 0)
    c = lax.broadcasted_iota(jnp.int32, (n, n), 1)
    return (r >= c) if lower else (r <= c)


def _f32_dot(a, b):
    return lax.dot_general(a, b, _NN, precision=lax.Precision.HIGHEST, preferred_element_type=F32)


@jax.custom_vjp
def chunk_cumsum(x):
    return _f32_dot(_tri(x.shape[0], True).astype(F32), x)


def _chunk_cumsum_fwd(x):
    return chunk_cumsum(x), None


def _chunk_cumsum_bwd(_, g):
    return (_f32_dot(_tri(g.shape[0], False).astype(F32), g),)


chunk_cumsum.defvjp(_chunk_cumsum_fwd, _chunk_cumsum_bwd)


@jax.custom_vjp
def swap_halves(t):
    return pltpu.roll(t, t.shape[-1] // 2, t.ndim - 1)


def _swap_fwd(t):
    return swap_halves(t), None


def _swap_bwd(_, g):
    return (swap_halves(g),)


swap_halves.defvjp(_swap_fwd, _swap_bwd)


def mm(name, a, b, dims, *, tm, tn, tk, out_dtype=F32, extras=(), epi=None, n_out=1, out_dtypes=None, side=None):
    if dims == "nn":
        (M, K), N = a.shape, b.shape[1]
    elif dims == "nt":
        (M, K), N = a.shape, b.shape[0]
    else:
        (K, M), N = a.shape, b.shape[1]
    tm, tn, tk = min(tm, M), min(tn, N), min(tk, K)
    assert M % tm == 0 and N % tn == 0 and K % tk == 0, (name, M, N, K, tm, tn, tk)
    nk = K // tk
    dn = {"nn": _NN, "nt": _NT, "tn": _TN}[dims]
    a_spec = pl.BlockSpec((tk, tm), lambda i, j, k: (k, i)) if dims == "tn" else pl.BlockSpec((tm, tk), lambda i, j, k: (i, k))
    b_spec = pl.BlockSpec((tn, tk), lambda i, j, k: (j, k)) if dims == "nt" else pl.BlockSpec((tk, tn), lambda i, j, k: (k, j))
    o_spec = pl.BlockSpec((tm, tn), lambda i, j, k: (i, j))
    n_ex = len(extras)
    out_dtypes = out_dtypes or (out_dtype,) * n_out

    def finish(r, ex_refs, o_refs):
        if epi is not None:
            r = epi(r, *[e[...] for e in ex_refs])
        r = r if isinstance(r, tuple) else (r,)
        for o, v in zip(o_refs, r):
            o[...] = v.astype(o.dtype)

    def body(*refs):
        a_ref, b_ref = refs[:2]
        ex_refs = refs[2:2 + n_ex]
        o_refs = refs[2 + n_ex:2 + n_ex + n_out]
        part = _mxu(a_ref[...], b_ref[...], dn)
        if nk == 1:
            finish(part, ex_refs, o_refs)
            return
        acc = refs[-1]
        k = pl.program_id(2)

        @pl.when(k == 0)
        def _():
            acc[...] = part

        @pl.when(k > 0)
        def _():
            acc[...] += part

        @pl.when(k == nk - 1)
        def _():
            finish(acc[...], ex_refs, o_refs)

    outs = _side_call(
        side, body,
        name=name,
        grid=(M // tm, N // tn, nk),
        in_specs=[a_spec, b_spec] + [o_spec] * n_ex,
        out_specs=[o_spec] * n_out,
        out_shape=[jax.ShapeDtypeStruct((M, N), dt) for dt in out_dtypes],
        scratch_shapes=[pltpu.VMEM((tm, tn), F32)] if nk > 1 else [],
        args=[a, b, *extras],
        semantics=("parallel", "parallel", "arbitrary"),
    )
    return outs[0] if n_out == 1 else outs


class Row:
    def __init__(self, arr, width=None, col=0, lead=(), rowmap=None):
        self.arr, self.lead, self.col = arr, tuple(lead), col
        self.width = arr.shape[-1] if width is None else width
        self.rowmap = rowmap

    def spec(self, tm):
        lead, col, rowmap = self.lead, self.col, self.rowmap or (lambda i: i)
        return pl.BlockSpec((None,) * len(lead) + (tm, self.width), lambda i: (*lead, rowmap(i), col))


def row_call(name, fn, rows, params, new_outs, red_shapes=(), *, tm, into=None, into_cols=()):
    R = rows[0].arr.shape[len(rows[0].lead)]
    tm = min(tm, R)
    assert R % tm == 0, (name, R, tm)
    n_row, n_par, n_new, n_into, n_red = len(rows), len(params), len(new_outs), len(into_cols), len(red_shapes)
    fresh = isinstance(into, jax.ShapeDtypeStruct)

    def body(*refs):
        row_refs = refs[:n_row]
        par_refs = refs[n_row:n_row + n_par]
        pos = n_row + n_par + (1 if (into is not None and not fresh) else 0)
        new_refs = refs[pos:pos + n_new]
        red_refs = refs[pos + n_new:pos + n_new + n_red]
        pos2 = pos + n_new + n_red
        into_ref = refs[pos2] if into is not None else None
        stage = refs[pos2 + 1:] if into is not None else ()
        i = pl.program_id(0)
        vals = fn(*[r[...] for r in row_refs], *[p[...] for p in par_refs])
        vals = vals if isinstance(vals, (tuple, list)) else (vals,)
        for o, v in zip(new_refs, vals[:n_new]):
            o[...] = v.astype(o.dtype)
        for (width, off), buf, v in zip(into_cols, stage, vals[n_new:n_new + n_into]):
            buf[...] = v.astype(buf.dtype)
            pltpu.sync_copy(buf, into_ref.at[pl.ds(pl.multiple_of(i * tm, tm), tm), pl.ds(off, width)])
        for o, v in zip(red_refs, vals[n_new + n_into:]):
            @pl.when(i == 0)
            def _(o=o, v=v):
                o[...] = v.astype(o.dtype)

            @pl.when(i > 0)
            def _(o=o, v=v):
                o[...] += v.astype(o.dtype)

    in_specs = [r.spec(tm) for r in rows] + [pl.BlockSpec(p.shape, lambda i, nd=p.ndim: (0,) * nd) for p in params]
    args = [r.arr for r in rows] + list(params)
    out_specs = [pl.BlockSpec((tm, w), lambda i: (i, 0)) for w, _ in new_outs]
    out_shape = [jax.ShapeDtypeStruct((R, w), dt) for w, dt in new_outs]
    out_specs += [pl.BlockSpec(s, lambda i, nd=len(s): (0,) * nd) for s in red_shapes]
    out_shape += [jax.ShapeDtypeStruct(s, F32) for s in red_shapes]
    aliases, scratch = {}, []
    if into is not None:
        out_specs.append(pl.BlockSpec(memory_space=pl.ANY))
        out_shape.append(jax.ShapeDtypeStruct(into.shape, into.dtype))
        if not fresh:
            in_specs.append(pl.BlockSpec(memory_space=pl.ANY))
            args.append(into)
            aliases = {len(args) - 1: len(out_shape) - 1}
        scratch = [pltpu.VMEM((tm, w), into.dtype) for w, _ in into_cols]
    outs = pl.pallas_call(
        body,
        name=name,
        grid=(R // tm,),
        in_specs=in_specs,
        out_specs=out_specs,
        out_shape=out_shape,
        scratch_shapes=scratch,
        input_output_aliases=aliases,
        compiler_params=pltpu.CompilerParams(dimension_semantics=("arbitrary",), vmem_limit_bytes=VMEM_BIG),
    )(*args)
    return list(outs)


def vjp_fn(fn, n_row, n_out, passthrough=0):
    def bwd(*args):
        rows = [a.astype(F32) for a in args[:n_row]]
        cts = args[n_row:n_row + n_out]
        thru = args[n_row + n_out:n_row + n_out + passthrough]
        pars = args[n_row + n_out + passthrough:]
        outs, pull = jax.vjp(lambda *xs: tuple(fn(*xs)), *rows, *pars)
        grads = pull(tuple(c.astype(o.dtype) for c, o in zip(cts, outs)))
        return tuple(grads[:n_row]) + tuple(thru) + tuple(grads[n_row:])

    return bwd


def _ln(x, g, b):
    mu = jnp.mean(x, axis=-1, keepdims=True)
    xc = x - mu
    var = jnp.mean(xc * xc, axis=-1, keepdims=True)
    return xc * lax.rsqrt(var + LN_EPS) * g + b


def f_ln_in(x, g, b):
    return (_ln(x, g, b),)


def f_deepnorm(x, y, g, b):
    return (_ln(ALPHA * x + y, g, b),)


def _twice(fn):
    return lambda *args: fn(*args) * 2


STREAM = [(D_MODEL, F32), (D_MODEL, MXU_DTYPE)]


def _rope(t, cos, sin):
    parts = []
    for h in range(LIN_HEADS):
        th = t[:, h * LIN_KEY_DIM:(h + 1) * LIN_KEY_DIM]
        parts.append(th * cos + swap_halves(th) * sin)
    return jnp.concatenate(parts, axis=1)


def f_ret_pre(rq, rk, cos, sin):
    return _rope(rq, cos, sin), _rope(rk, cos, sin) * LIN_KEY_DIM ** -0.5


def _log_sigmoid(x):
    return jnp.minimum(x, 0.0) - jnp.log(1.0 + jnp.exp(-jnp.abs(x)))


def f_gla_pre(gq, glr, wlr, blr):
    gate_pre = dot_nn(glr, wlr) + blr
    return gq * LIN_KEY_DIM ** -0.5, _log_sigmoid(gate_pre) / GLA_GATE_NORMALIZER


def f_normgate(ret_raw, rg, gla_raw, gg, gnorm):
    rets, glas = [], []
    for h in range(LIN_HEADS):
        sl = slice(h * LIN_VAL_DIM, (h + 1) * LIN_VAL_DIM)
        t = ret_raw[:, sl]
        mu = jnp.mean(t, axis=-1, keepdims=True)
        tc = t - mu
        rets.append(tc * lax.rsqrt(jnp.mean(tc * tc, axis=-1, keepdims=True) + LN_EPS))
        u = gla_raw[:, sl]
        glas.append(u * lax.rsqrt(jnp.mean(u * u, axis=-1, keepdims=True) + LN_EPS) * gnorm)
    ret = jax.nn.silu(rg) * jnp.concatenate(rets, axis=1)
    gla = jax.nn.silu(gg) * jnp.concatenate(glas, axis=1)
    return ret, gla


def f_merge(p0, p1, p2, gl):
    out = jax.nn.sigmoid(gl[:, :D_MODEL]) * p0
    out = out + jax.nn.sigmoid(gl[:, D_MODEL:2 * D_MODEL]) * p1
    return (out + jax.nn.sigmoid(gl[:, 2 * D_MODEL:]) * p2,)


def f_loss(y, tgt):
    diff = y - tgt
    sq = jnp.sum(diff * diff, axis=0, keepdims=True)
    lanes = sq[:, :128]
    for j in range(1, D_MODEL // 128):
        lanes = lanes + sq[:, j * 128:(j + 1) * 128]
    return diff * (1.0 / D_MODEL), lanes * (0.5 / D_MODEL)


def _adam(g, w, m, v):
    m = ADAM_B1 * m + (1.0 - ADAM_B1) * g
    v = ADAM_B2 * v + (1.0 - ADAM_B2) * jnp.square(g)
    m_hat = m / (1.0 - ADAM_B1 ** ADAM_STEP)
    v_hat = v / (1.0 - ADAM_B2 ** ADAM_STEP)
    delta = -ADAM_LR * (m_hat / (jnp.sqrt(v_hat) + ADAM_EPS) + ADAM_WD * w)
    return delta, m, v


def f_sum_adam(*args):
    g = args[0].astype(F32)
    for p in args[1:-3]:
        g = g + p.astype(F32)
    return (g,) + _adam(g, *args[-3:])


def f_adam(g, w, m, v):
    return _adam(g, w, m, v)


GROUP = 4
GROUP_ROWS = GROUP * CHUNK
GROUP_KEYS = (GROUP + ATTN_LEFT_CHUNKS) * CHUNK
DIAG_ROWS = REL_CLIP + CHUNK
GRAD_ROWS = 200


def _blocks_by_delta():
    blocks = {}
    for qi in range(GROUP):
        for J in range(GROUP_KEYS // 128):
            blocks.setdefault(qi - 2 * J, []).append((qi, J))
    return blocks


def bias_table(rel_bias_l):
    def body(rb_ref, o_ref):
        h = pl.program_id(0)
        rm = lax.broadcasted_iota(jnp.int32, (CHUNK, 128), 0)
        cm = lax.broadcasted_iota(jnp.int32, (CHUNK, 128), 1)
        ahead = jnp.where(cm >= CHUNK, 1, 0)
        for delta, where in _blocks_by_delta().items():
            base = 64 * delta + PAD_ROWS
            rel = jnp.minimum(base + rm - cm, REL_CLIP) + REL_CLIP
            in_band = (ahead - delta >= 0) & (ahead - delta <= ATTN_LEFT_CHUNKS)
            lo = max(REL_MIN, min(base - 127, REL_CLIP) + REL_CLIP)
            hi = min(base + CHUNK - 1, REL_CLIP) + REL_CLIP

            def step(r, acc, rel=rel):
                return jnp.where(rel == r, rb_ref[h * N_REL + r], acc)

            blk = jnp.zeros((CHUNK, 128), F32) if hi < lo else lax.fori_loop(lo, hi + 1, step, jnp.zeros((CHUNK, 128), F32))
            blk = jnp.where(in_band, blk, NEG_INF)
            for qi, J in where:
                o_ref[pl.ds(qi * CHUNK, CHUNK), pl.ds(J * 128, 128)] = blk

    return pl.pallas_call(
        body,
        name="bias_table",
        grid=(ATTN_HEADS,),
        in_specs=[pl.BlockSpec(memory_space=pltpu.SMEM)],
        out_specs=pl.BlockSpec((None, GROUP_ROWS, GROUP_KEYS), lambda h: (h, 0, 0)),
        out_shape=jax.ShapeDtypeStruct((ATTN_HEADS, GROUP_ROWS, GROUP_KEYS), F32),
        compiler_params=pltpu.CompilerParams(dimension_semantics=("arbitrary",)),
    )(rel_bias_l.reshape(ATTN_HEADS * N_REL))


def bias_table_grad(dbias):
    blocks = _blocks_by_delta()
    below_clip = range(-(CHUNK - 1), REL_CLIP)
    deltas = [d for d in sorted(blocks) if any(-127 <= pos - 64 * d - PAD_ROWS <= CHUNK - 1 for pos in below_clip)]

    def body(d_ref, o_ref, diag):
        sums = []
        for delta in deltas:
            acc = None
            for qi, J in blocks[delta]:
                blk = d_ref[pl.ds(qi * CHUNK, CHUNK), pl.ds(J * 128, 128)]
                acc = blk if acc is None else acc + blk
            sums.append(acc)
        off = lax.broadcasted_iota(jnp.int32, (CHUNK, 128), 0) - lax.broadcasted_iota(jnp.int32, (CHUNK, 128), 1)
        diag[...] = jnp.zeros_like(diag)

        def step(t, carry):
            pos = t - (CHUNK - 1)
            picked = jnp.zeros((CHUNK, 128), F32)
            for delta, s in zip(deltas, sums):
                picked = picked + jnp.where(off == pos - 64 * delta - PAD_ROWS, s, 0.0)
            diag[pl.ds(t, 1), :] = jnp.sum(picked, axis=0, keepdims=True)
            return carry

        lax.fori_loop(0, DIAG_ROWS - 1, step, 0)
        d = d_ref[...]
        total = jnp.sum(jnp.sum(d, axis=1, keepdims=True), axis=0, keepdims=True)
        o_ref[pl.ds(0, DIAG_ROWS), :] = jnp.broadcast_to(jnp.sum(diag[...], axis=1, keepdims=True), (DIAG_ROWS, 128))
        o_ref[pl.ds(DIAG_ROWS, GRAD_ROWS - DIAG_ROWS), :] = jnp.broadcast_to(total, (GRAD_ROWS - DIAG_ROWS, 128))

    out = pl.pallas_call(
        body,
        name="bias_table_grad",
        grid=(ATTN_HEADS,),
        in_specs=[pl.BlockSpec((None, GROUP_ROWS, GROUP_KEYS), lambda h: (h, 0, 0))],
        out_specs=pl.BlockSpec((None, GRAD_ROWS, 128), lambda h: (h, 0, 0)),
        out_shape=jax.ShapeDtypeStruct((ATTN_HEADS, GRAD_ROWS, 128), F32),
        scratch_shapes=[pltpu.VMEM((DIAG_ROWS, 128), F32)],
        compiler_params=pltpu.CompilerParams(dimension_semantics=("arbitrary",)),
    )(dbias)
    below = out[:, :DIAG_ROWS - 1, 0]
    clipped = out[:, DIAG_ROWS, 0] - jnp.sum(below, axis=1)
    return jnp.concatenate([jnp.zeros((ATTN_HEADS, REL_MIN), F32), below, clipped[:, None]], axis=1)


def _attn_stage(q_ref, k_ref, v_ref, qs, kp, vp, T):
    qs[...] = (q_ref[...] * ATTN_HEAD_DIM ** -0.5).astype(qs.dtype)
    for dst, src in ((kp, k_ref), (vp, v_ref)):
        dst[pl.ds(0, PAD_ROWS), :] = jnp.zeros((PAD_ROWS, ATTN_HEAD_DIM), dst.dtype)
        dst[pl.ds(PAD_ROWS, T), :] = src[...].astype(dst.dtype)


def _attn_probs(g, first_groups, qs, kp, b_ref):
    r0 = g * GROUP_ROWS if first_groups else pl.multiple_of(g * GROUP_ROWS, GROUP_ROWS)
    qc = qs[pl.ds(r0, GROUP_ROWS), :]
    kb = kp[pl.ds(r0, GROUP_KEYS), :]
    s = _mxu(qc, kb, _NT) + b_ref[...]
    if first_groups:
        k_chunk = lax.shift_right_logical(lax.broadcasted_iota(jnp.int32, (GROUP_ROWS, GROUP_KEYS), 1), 6)
        s = jnp.where(g * GROUP + k_chunk >= ATTN_LEFT_CHUNKS, s, NEG_INF)
    e = jnp.exp(s - jnp.max(s, axis=-1, keepdims=True))
    return r0, qc, kb, e / jnp.sum(e, axis=-1, keepdims=True)


def _for_groups(T, step):
    n_first = min(ATTN_LEFT_CHUNKS // GROUP, T // GROUP_ROWS)
    for g in range(n_first):
        step(g, True)
    lax.fori_loop(n_first, T // GROUP_ROWS, lambda g, c: (step(g, False), c)[1], 0)


def _head_spec(T, base):
    return pl.BlockSpec((T, ATTN_HEAD_DIM), lambda h: (0, base + h))


def _attn_scratch(T):
    return [pltpu.VMEM((T, ATTN_HEAD_DIM), MXU_DTYPE)] + [pltpu.VMEM((T + PAD_ROWS, ATTN_HEAD_DIM), MXU_DTYPE)] * 2


def attn_fwd(proj, bias_tab, side=None):
    T = proj.shape[0]

    def body(q_ref, k_ref, v_ref, b_ref, o_ref, qs, kp, vp):
        _attn_stage(q_ref, k_ref, v_ref, qs, kp, vp, T)

        def step(g, first):
            r0, _, _, p = _attn_probs(g, first, qs, kp, b_ref)
            o_ref[pl.ds(r0, GROUP_ROWS), :] = _mxu(p, vp[pl.ds(r0, GROUP_KEYS), :], _NN).astype(o_ref.dtype)

        _for_groups(T, step)

    out, = _side_call(
        side, body,
        name="attn_fwd",
        grid=(ATTN_HEADS,),
        in_specs=[_head_spec(T, COL["aq"] // 128), _head_spec(T, COL["ak"] // 128), _head_spec(T, COL["av"] // 128),
                  pl.BlockSpec((None, GROUP_ROWS, GROUP_KEYS), lambda h: (h, 0, 0))],
        out_specs=[pl.BlockSpec((T, ATTN_HEAD_DIM), lambda h: (0, h))],
        out_shape=[jax.ShapeDtypeStruct((T, ATTN_HEADS * ATTN_HEAD_DIM), MXU_DTYPE)],
        scratch_shapes=_attn_scratch(T),
        args=[proj, proj, proj, bias_tab],
    )
    return out


def attn_bwd(proj, bias_tab, dout, dproj, side=None):
    T = proj.shape[0]

    def body(q_ref, k_ref, v_ref, b_ref, do_ref, dproj_in, dbias_ref, dproj_ref, qs, kp, vp, dkp, dvp, dq, stage):
        del dproj_in
        h = pl.program_id(0)
        _attn_stage(q_ref, k_ref, v_ref, qs, kp, vp, T)
        dkp[...] = jnp.zeros_like(dkp)
        dvp[...] = jnp.zeros_like(dvp)
        dbias_ref[...] = jnp.zeros_like(dbias_ref)

        def step(g, first):
            r0, qc, kb, p = _attn_probs(g, first, qs, kp, b_ref)
            do = do_ref[pl.ds(r0, GROUP_ROWS), :]
            dvp[pl.ds(r0, GROUP_KEYS), :] += _mxu(p, do, _TN)
            dp = _mxu(do, vp[pl.ds(r0, GROUP_KEYS), :], _NT)
            ds = p * (dp - jnp.sum(dp * p, axis=-1, keepdims=True))
            dq[pl.ds(r0, GROUP_ROWS), :] = _mxu(ds, kb, _NN) * ATTN_HEAD_DIM ** -0.5
            dkp[pl.ds(r0, GROUP_KEYS), :] += _mxu(ds, qc, _TN)
            dbias_ref[...] += ds

        _for_groups(T, step)
        col = pl.multiple_of(h * ATTN_HEAD_DIM, ATTN_HEAD_DIM)
        for grad, first_row, base in ((dq, 0, "aq"), (dkp, PAD_ROWS, "ak"), (dvp, PAD_ROWS, "av")):
            stage[...] = grad[pl.ds(first_row, T), :].astype(stage.dtype)
            pltpu.sync_copy(stage, dproj_ref.at[:, pl.ds(COL[base] + col, ATTN_HEAD_DIM)])

    dbias, dproj = _side_call(
        side, body,
        name="attn_bwd",
        grid=(ATTN_HEADS,),
        in_specs=[_head_spec(T, COL["aq"] // 128), _head_spec(T, COL["ak"] // 128), _head_spec(T, COL["av"] // 128),
                  pl.BlockSpec((None, GROUP_ROWS, GROUP_KEYS), lambda h: (h, 0, 0)),
                  pl.BlockSpec((T, ATTN_HEAD_DIM), lambda h: (0, h)),
                  pl.BlockSpec(memory_space=pl.ANY)],
        out_specs=[pl.BlockSpec((None, GROUP_ROWS, GROUP_KEYS), lambda h: (h, 0, 0)), pl.BlockSpec(memory_space=pl.ANY)],
        out_shape=[jax.ShapeDtypeStruct((ATTN_HEADS, GROUP_ROWS, GROUP_KEYS), F32),
                   jax.ShapeDtypeStruct(dproj.shape, dproj.dtype)],
        scratch_shapes=_attn_scratch(T) + [pltpu.VMEM((T + PAD_ROWS, ATTN_HEAD_DIM), F32)] * 2
        + [pltpu.VMEM((T, ATTN_HEAD_DIM), F32), pltpu.VMEM((T, ATTN_HEAD_DIM), dproj.dtype)],
        args=[proj, proj, proj, bias_tab, dout, dproj],
        aliases={5: 1},
    )
    return dproj, dbias


def _lin_segment(q, k, v, la, st):
    lower = _tri(CHUNK, True)
    outs = []
    for i in range(SEG_CHUNKS):
        sl = slice(i * CHUNK, (i + 1) * CHUNK)
        qi, ki, vi, lai = q[sl], k[sl], v[sl], la[sl]
        cum = chunk_cumsum(lai)
        last = jnp.sum(lai, axis=0, keepdims=True)
        e_pos, e_neg = jnp.exp(cum), jnp.exp(-cum)
        q_fwd = qi * e_pos
        s_fwd = dot_nt(q_fwd, ki * e_neg)
        s_bwd = dot_nt(qi * e_neg, ki * e_pos)
        o = dot_nn(jnp.where(lower, s_fwd, s_bwd), vi) + dot_nt(q_fwd, st)
        st = jnp.exp(last) * st + dot_tn(vi, ki * jnp.exp(last - cum))
        outs.append(o)
    return jnp.concatenate(outs, axis=0), st


class Win:
    def __init__(self, arr, width, base=0):
        self.arr, self.width, self.base = arr, width, base


def _lin_spec(w, seg_of):
    base = w.base
    if w.arr.shape[0] == 1:
        return pl.BlockSpec((1, w.width), lambda h, s: (0, base + h))
    return pl.BlockSpec((SEG, w.width), lambda h, s: (seg_of(s), base + h))


def _decay_rows(la_ref):
    la = la_ref[...]
    return jnp.broadcast_to(la, (SEG, LIN_KEY_DIM)) if la.shape[0] == 1 else la


def lin_fwd(name, q, k, v, la, side=None):
    T = q.arr.shape[0]
    nseg = T // SEG

    def body(q_ref, k_ref, v_ref, la_ref, o_ref, st_ref, st):
        @pl.when(pl.program_id(1) == 0)
        def _():
            st[...] = jnp.zeros_like(st)

        st0 = st[...]
        st_ref[...] = st0
        o, st1 = _lin_segment(q_ref[...], k_ref[...], v_ref[...], _decay_rows(la_ref), st0)
        o_ref[...] = o
        st[...] = st1

    fwd = lambda s: s
    out, states = _side_call(
        side, body,
        name=name,
        grid=(LIN_HEADS, nseg),
        in_specs=[_lin_spec(w, fwd) for w in (q, k, v, la)],
        out_specs=[pl.BlockSpec((SEG, LIN_VAL_DIM), lambda h, s: (s, h)),
                   pl.BlockSpec((None, None, LIN_VAL_DIM, LIN_KEY_DIM), lambda h, s: (h, s, 0, 0))],
        out_shape=[jax.ShapeDtypeStruct((T, LIN_HEADS * LIN_VAL_DIM), F32),
                   jax.ShapeDtypeStruct((LIN_HEADS, nseg, LIN_VAL_DIM, LIN_KEY_DIM), F32)],
        scratch_shapes=[pltpu.VMEM((LIN_VAL_DIM, LIN_KEY_DIM), F32)],
        args=[q.arr, k.arr, v.arr, la.arr],
    )
    return out, states


def lin_bwd(name, q, k, v, la, states, dout, with_decay_grad, side=None):
    T = q.arr.shape[0]
    nseg = T // SEG
    n_out = 4 if with_decay_grad else 3

    def body(q_ref, k_ref, v_ref, la_ref, st_ref, do_ref, *rest):
        outs, dst = rest[:n_out], rest[n_out]

        @pl.when(pl.program_id(1) == 0)
        def _():
            dst[...] = jnp.zeros_like(dst)

        _, pull = jax.vjp(_lin_segment, q_ref[...].astype(F32), k_ref[...].astype(F32), v_ref[...].astype(F32),
                          _decay_rows(la_ref), st_ref[...])
        grads = pull((do_ref[...], dst[...]))
        for o, g in zip(outs, grads[:n_out]):
            o[...] = g
        dst[...] = grads[4]

    rev = lambda s: nseg - 1 - s
    key_spec = pl.BlockSpec((SEG, LIN_KEY_DIM), lambda h, s: (rev(s), h))
    val_spec = pl.BlockSpec((SEG, LIN_VAL_DIM), lambda h, s: (rev(s), h))
    out_specs = [key_spec, key_spec, val_spec] + ([key_spec] if with_decay_grad else [])
    widths = [LIN_KEY_DIM, LIN_KEY_DIM, LIN_VAL_DIM] + ([LIN_KEY_DIM] if with_decay_grad else [])
    return _side_call(
        side, body,
        name=name,
        grid=(LIN_HEADS, nseg),
        in_specs=[_lin_spec(w, rev) for w in (q, k, v, la)] + [
            pl.BlockSpec((None, None, LIN_VAL_DIM, LIN_KEY_DIM), lambda h, s: (h, rev(s), 0, 0)), val_spec],
        out_specs=out_specs,
        out_shape=[jax.ShapeDtypeStruct((T, LIN_HEADS * w), F32) for w in widths],
        scratch_shapes=[pltpu.VMEM((LIN_VAL_DIM, LIN_KEY_DIM), F32)],
        args=[q.arr, k.arr, v.arr, la.arr, states, dout],
    )


def _coords():
    return lax.axis_index("x"), lax.axis_index("y"), lax.axis_index("c")


def _peer(k):
    x, y, c = _coords()
    px = 1 - x if k & 4 else x
    py = 1 - y if k & 2 else y
    pc = 1 - c if k & 1 else c
    return (px, py, pc), 4 * px + 2 * py + pc


def _remote(src, dst, send_sem, recv_sem, device):
    return pltpu.make_async_remote_copy(src_ref=src, dst_ref=dst, send_sem=send_sem, recv_sem=recv_sem,
                                        device_id=device, device_id_type=pl.DeviceIdType.MESH)


class Side:
    def __init__(self, kind, arrays):
        self.kind, self.arrays, self.n = kind, list(arrays), len(arrays)
        self.outs_per = 1
        self.slots = {"gather": N_DEV - 1, "pair": 4, "chips": 3}[kind]

    def out_shape(self):
        if self.kind == "gather":
            return [jax.ShapeDtypeStruct((N_DEV,) + a.shape, a.dtype) for a in self.arrays]
        return [jax.ShapeDtypeStruct((4,) + a.shape[1:], a.dtype) for a in self.arrays]

    def sems(self):
        return [pltpu.SemaphoreType.DMA((self.n, self.slots)), pltpu.SemaphoreType.DMA((self.n, self.slots)),
                pltpu.SemaphoreType.DMA((self.n,))]

    def start(self, ins, outs, sems):
        send, recv, own = sems
        x, y, c = _coords()
        me = 4 * x + 2 * y + c
        for a in range(self.n):
            if self.kind == "gather":
                pltpu.make_async_copy(ins[a], outs[a].at[me], own.at[a]).start()
                for k in (1, 2, 4, 6):
                    peer, _ = _peer(k)
                    _remote(ins[a], outs[a].at[me], send.at[a, k - 1], recv.at[a, k - 1], peer).start()
            elif self.kind == "pair":
                for j in range(4):
                    _remote(ins[a].at[2 * j + 1 - c], outs[a].at[j], send.at[a, j], recv.at[a, j], (x, y, 1 - c)).start()
            else:
                chip = 2 * x + y
                pltpu.make_async_copy(ins[a].at[chip], outs[a].at[chip], own.at[a]).start()
                for d in (1, 2, 3):
                    tx, ty = (1 - x if d & 2 else x), (1 - y if d & 1 else y)
                    _remote(ins[a].at[2 * tx + ty], outs[a].at[chip], send.at[a, d - 1], recv.at[a, d - 1], (tx, ty, c)).start()

    def finish(self, ins, outs, sems):
        send, recv, own = sems
        x, y, c = _coords()
        me = 4 * x + 2 * y + c
        sibling = (x, y, 1 - c)
        for a in range(self.n):
            if self.kind == "gather":
                blk = outs[a].at[me]
                for k in (2, 4, 6):
                    _, pidx = _peer(k)
                    _remote(blk, blk, send.at[a, k - 1], recv.at[a, k - 1], sibling).wait_recv()
                    _remote(outs[a].at[pidx], outs[a].at[pidx], send.at[a, k], recv.at[a, k], sibling).start()
                pltpu.make_async_copy(ins[a], blk, own.at[a]).wait()
                for k in range(1, N_DEV):
                    _remote(blk, blk, send.at[a, k - 1], recv.at[a, k - 1], sibling).wait_send()
                for k in (1, 3, 5, 7):
                    _remote(blk, blk, send.at[a, k - 1], recv.at[a, k - 1], sibling).wait_recv()
            elif self.kind == "pair":
                for j in range(4):
                    _remote(ins[a].at[j], outs[a].at[j], send.at[a, j], recv.at[a, j], sibling).wait()
            else:
                slab = outs[a].at[0]
                pltpu.make_async_copy(slab, slab, own.at[a]).wait()
                for d in (1, 2, 3):
                    _remote(slab, slab, send.at[a, d - 1], recv.at[a, d - 1], sibling).wait()


def _side_call(side, body, *, name, grid, in_specs, out_specs, out_shape, scratch_shapes, args, semantics=None,
               aliases=None):
    in_specs, out_specs, out_shape = list(in_specs), list(out_specs), list(out_shape)
    scratch_shapes, args = list(scratch_shapes), list(args)
    n_in, n_out = len(in_specs), len(out_specs)
    kernel_body = body
    if side is not None:
        n_so = side.n * side.outs_per

        def kernel_body(*refs):
            ins, side_ins = refs[:n_in], refs[n_in:n_in + side.n]
            pos = n_in + side.n
            outs, side_outs = refs[pos:pos + n_out], refs[pos + n_out:pos + n_out + n_so]
            rest = refs[pos + n_out + n_so:]
            scratch, sems = rest[:-3], rest[-3:]
            ids = [pl.program_id(ax) for ax in range(len(grid))]
            first = functools.reduce(jnp.logical_and, [i == 0 for i in ids])
            last = functools.reduce(jnp.logical_and, [i == g - 1 for i, g in zip(ids, grid)])

            @pl.when(first)
            def _():
                side.start(side_ins, side_outs, sems)

            body(*ins, *outs, *scratch)

            @pl.when(last)
            def _():
                side.finish(side_ins, side_outs, sems)

        any_spec = pl.BlockSpec(memory_space=pl.ANY)
        in_specs += [any_spec] * side.n
        args += side.arrays
        out_specs += [any_spec] * n_so
        out_shape += side.out_shape()
        scratch_shapes += side.sems()
        semantics = ("arbitrary",) * len(grid)
    outs = pl.pallas_call(
        kernel_body,
        name=name,
        grid=grid,
        in_specs=in_specs,
        out_specs=out_specs,
        out_shape=out_shape,
        scratch_shapes=scratch_shapes,
        input_output_aliases=aliases or {},
        compiler_params=pltpu.CompilerParams(dimension_semantics=semantics or ("arbitrary",) * len(grid),
                                             vmem_limit_bytes=VMEM_BIG),
    )(*args)
    if side is not None:
        side.results = list(outs[n_out:])
    return list(outs[:n_out])


def carrier(name, side):
    _side_call(side, lambda: None, name=name, grid=(1,), in_specs=[], out_specs=[], out_shape=[], scratch_shapes=[], args=[])
    return side.results


def all_reduce_small(vec):
    def body(v_ref, o_ref, land, send_sems, recv_sems):
        x, y, c = _coords()
        me = 4 * x + 2 * y + c
        land[me] = v_ref[...]
        copies = []
        for k in range(1, N_DEV):
            peer, _ = _peer(k)
            cp = pltpu.make_async_remote_copy(
                src_ref=v_ref, dst_ref=land.at[me], send_sem=send_sems.at[k - 1], recv_sem=recv_sems.at[k - 1],
                device_id=peer, device_id_type=pl.DeviceIdType.MESH)
            cp.start()
            copies.append(cp)
        for cp in copies:
            cp.wait()
        total = land[0]
        for s in range(1, N_DEV):
            total = total + land[s]
        o_ref[...] = total

    return pl.pallas_call(
        body,
        name="all_reduce_small",
        in_specs=[pl.BlockSpec(memory_space=pltpu.VMEM)],
        out_specs=pl.BlockSpec(memory_space=pltpu.VMEM),
        out_shape=jax.ShapeDtypeStruct(vec.shape, F32),
        scratch_shapes=[pltpu.VMEM((N_DEV,) + vec.shape, F32), pltpu.SemaphoreType.DMA((N_DEV - 1,)),
                        pltpu.SemaphoreType.DMA((N_DEV - 1,))],
    )(vec)


def _pack(parts):
    flat = jnp.concatenate([p.reshape(-1).astype(F32) for p in parts])
    n = flat.shape[0]
    rows = -(-n // 1024) * 8
    return jnp.pad(flat, (0, rows * 128 - n)).reshape(rows, 128)


def _unpack(packed, shapes):
    flat, out, pos = packed.reshape(-1), [], 0
    for s in shapes:
        n = math.prod(s)
        out.append(flat[pos:pos + n].reshape(s))
        pos += n
    return out


def _row2(v):
    return v.reshape(1, -1)


def _rider(hooks, name, ctx=None):
    if name not in hooks:
        return None, lambda: None
    side, done = hooks[name](ctx)
    return side, lambda: done(side.results)


def _layer_forward(h, hb, W, consts, hooks):
    T = h.shape[0]
    cos, sin, ret_decay = consts
    side, landed = _rider(hooks, "proj")
    proj = mm("proj", hb, W["in"], "nn", tm=1024, tn=512, tk=1024, side=side)
    landed()
    bias_tab = bias_table(W["rel_bias"])
    side, landed = _rider(hooks, "attn_fwd")
    attn = attn_fwd(proj, bias_tab, side=side)
    landed()
    rq2, rk2 = row_call("ret_pre", f_ret_pre, [Row(proj, 512, COL["rq"] // 512), Row(proj, 512, COL["rk"] // 512),
                                              Row(cos), Row(sin)], [], [(512, F32), (512, F32)], tm=512)
    side, landed = _rider(hooks, "ret_fwd")
    ret_raw, ret_st = lin_fwd("ret_fwd", Win(rq2, 128), Win(rk2, 128), Win(proj, 256, COL["rv"] // 256), Win(ret_decay, 128),
                              side=side)
    landed()
    gq2, gla_la = row_call("gla_pre", f_gla_pre, [Row(proj, 512, COL["gq"] // 512), Row(proj, GLR_BLOCK, GLR_OFF // GLR_BLOCK)],
                           [W["wlr"], W["blr"]], [(512, F32), (512, F32)], tm=512)
    side, landed = _rider(hooks, "gla_fwd")
    gla_raw, gla_st = lin_fwd("gla_fwd", Win(gq2, 128), Win(proj, 128, COL["gk"] // 128), Win(proj, 256, COL["gv"] // 256),
                              Win(gla_la, 128), side=side)
    landed()
    ret_b, gla_b = row_call("normgate", f_normgate, [Row(ret_raw), Row(proj, 1024, COL["rg"] // 1024), Row(gla_raw),
                                                    Row(proj, 1024, COL["gg"] // 1024)], [W["gnorm"]],
                            [(1024, MXU_DTYPE), (1024, MXU_DTYPE)], tm=256)
    branches = (attn, ret_b, gla_b)
    p = [mm("branch_proj", branches[n], W["branch"][n], "nn", tm=1024, tn=1024, tk=1024) for n in range(3)]
    merged, = row_call("merge", f_merge, [Row(p[0]), Row(p[1]), Row(p[2]), Row(proj, GATES_W, GATES_OFF // GATES_W)], [],
                       [(1024, MXU_DTYPE)], tm=256)
    mix = mm("out_proj", merged, W["out"], "nn", tm=1024, tn=1024, tk=1024)
    x1, x1b = row_call("ln1", _twice(f_deepnorm), [Row(h), Row(mix)], [W["ln1_g"], W["ln1_b"]], STREAM, tm=512)
    side, landed = _rider(hooks, "mlp_up")
    u, act = mm("mlp_up", x1b, W["up"], "nn", tm=1024, tn=512, tk=1024, n_out=2, out_dtypes=(F32, MXU_DTYPE),
                epi=lambda r: (r, jnp.square(jnp.maximum(r, 0.0))), side=side)
    landed()
    y = mm("mlp_down", act, W["down"], "nn", tm=1024, tn=1024, tk=1024)
    x2, x2b = row_call("ln2", _twice(f_deepnorm), [Row(x1), Row(y)], [W["ln2_g"], W["ln2_b"]], STREAM, tm=512)
    saved = dict(h=h, hb=hb, proj=proj, bias_tab=bias_tab, rq2=rq2, rk2=rk2, ret_st=ret_st, gq2=gq2, gla_la=gla_la,
                 gla_st=gla_st, ret_raw=ret_raw, gla_raw=gla_raw, branches=branches, p=p, merged=merged, mix=mix, x1=x1, x1b=x1b,
                 u=u, act=act, y=y)
    return x2, x2b, saved


def _layer_backward(dx2, S, W, consts, hooks):
    cos, sin, ret_decay = consts
    h, proj, T = S["h"], S["proj"], S["h"].shape[0]
    add = lambda r, e: r + e
    dx1a, dy, dg2, db2 = row_call("ln2_bwd", vjp_fn(f_deepnorm, 2, 1), [Row(S["x1"]), Row(S["y"]), Row(dx2)],
                                  [W["ln2_g"], W["ln2_b"]], [(1024, F32), (1024, MXU_DTYPE)], [(1, 1024), (1, 1024)], tm=256)
    du = mm("mlp_down_bwd", dy, W["down"], "nt", tm=1024, tn=512, tk=1024, extras=(S["u"],),
            epi=lambda r, u: r * (2.0 * jnp.maximum(u, 0.0)), out_dtype=MXU_DTYPE)
    g_down = mm("mlp_down_wgrad", S["act"], dy, "tn", tm=1024, tn=1024, tk=512, out_dtype=WIRE_DTYPE)
    dx1 = mm("mlp_up_bwd", du, W["up"], "nt", tm=512, tn=1024, tk=1024, extras=(dx1a,), epi=add)
    g_up = mm("mlp_up_wgrad", S["x1b"], du, "tn", tm=1024, tn=1024, tk=512, out_dtype=WIRE_DTYPE)
    dha, dmix, dg1, db1 = row_call("ln1_bwd", vjp_fn(f_deepnorm, 2, 1), [Row(h), Row(S["mix"]), Row(dx1)],
                                   [W["ln1_g"], W["ln1_b"]], [(1024, F32), (1024, MXU_DTYPE)], [(1, 1024), (1, 1024)], tm=256)
    dmerged = mm("out_proj_bwd", dmix, W["out"], "nt", tm=1024, tn=1024, tk=1024)
    g_out = mm("out_proj_wgrad", S["merged"], dmix, "tn", tm=1024, tn=1024, tk=512, out_dtype=WIRE_DTYPE)
    dproj = jax.ShapeDtypeStruct((T, PROJ_PAD), MXU_DTYPE)
    p = S["p"]
    dp0, dp1, dp2, dproj = row_call(
        "merge_bwd", vjp_fn(f_merge, 4, 1), [Row(p[0]), Row(p[1]), Row(p[2]), Row(proj, GATES_W, GATES_OFF // GATES_W), Row(dmerged)],
        [], [(1024, MXU_DTYPE)] * 3, tm=256, into=dproj, into_cols=[(GATES_W, GATES_OFF)])
    dps = (dp0, dp1, dp2)
    dbr = [mm("branch_proj_bwd", dps[n], W["branch"][n], "nt", tm=1024, tn=1024, tk=1024,
              out_dtype=MXU_DTYPE if n == 0 else F32) for n in range(3)]
    g_branch = [mm("branch_proj_wgrad", S["branches"][n], dps[n], "tn", tm=1024, tn=1024, tk=512, out_dtype=WIRE_DTYPE)
                for n in range(3)]
    dret_raw, dgla_raw, dgnorm, dproj = row_call(
        "normgate_bwd", vjp_fn(lambda a, b, c, d, g: _reorder_normgate(a, b, c, d, g), 4, 2),
        [Row(S["ret_raw"]), Row(S["gla_raw"]), Row(proj, 1024, COL["rg"] // 1024), Row(proj, 1024, COL["gg"] // 1024),
         Row(dbr[1]), Row(dbr[2])], [W["gnorm"]], [(1024, F32), (1024, F32)], [(1, LIN_VAL_DIM)], tm=256,
        into=dproj, into_cols=[(1024, COL["rg"]), (1024, COL["gg"])])
    side, landed = _rider(hooks, "gla_bwd")
    dgq2, dgk, dgv, dgla_la = lin_bwd("gla_bwd", Win(S["gq2"], 128), Win(proj, 128, COL["gk"] // 128),
                                      Win(proj, 256, COL["gv"] // 256), Win(S["gla_la"], 128), S["gla_st"], dgla_raw, True,
                                      side=side)
    landed()
    dwlr, dblr, dproj = row_call(
        "gla_pre_bwd", _gla_pre_bwd, [Row(proj, 512, COL["gq"] // 512), Row(proj, GLR_BLOCK, GLR_OFF // GLR_BLOCK),
                                      Row(dgq2), Row(dgla_la), Row(dgk), Row(dgv)], [W["wlr"], W["blr"]], [],
        [(GLR_BLOCK, 512), (1, 512)], tm=256, into=dproj,
        into_cols=[(512, COL["gq"]), (PROJ_PAD - GLR_OFF, GLR_OFF), (512, COL["gk"]), (1024, COL["gv"])])
    drq2, drk2, drv = lin_bwd("ret_bwd", Win(S["rq2"], 128), Win(S["rk2"], 128), Win(proj, 256, COL["rv"] // 256),
                              Win(ret_decay, 128), S["ret_st"], dret_raw, False)
    dproj, = row_call(
        "ret_pre_bwd", _ret_pre_bwd, [Row(proj, 512, COL["rq"] // 512), Row(proj, 512, COL["rk"] // 512), Row(cos), Row(sin),
                                      Row(drq2), Row(drk2), Row(drv)], [], [], tm=256, into=dproj,
        into_cols=[(512, COL["rq"]), (512, COL["rk"]), (1024, COL["rv"])])
    side, landed = _rider(hooks, "attn_bwd")
    dproj, dbias = attn_bwd(proj, S["bias_tab"], dbr[0], dproj, side=side)
    landed()
    d_rel_bias = bias_table_grad(dbias)
    side, landed = _rider(hooks, "proj_wgrad", dict(branch=g_branch, out=g_out, up=g_up, down=g_down))
    g_in = mm("proj_wgrad", S["hb"], dproj, "tn", tm=1024, tn=1280, tk=512, out_dtype=WIRE_DTYPE, side=side)
    landed()
    side, landed = _rider(hooks, "proj_bwd", {"in": g_in})
    dh = mm("proj_bwd", dproj, W["in"], "nt", tm=512, tn=1024, tk=1280, extras=(dha,), epi=add, side=side)
    landed()
    big = dict(g_in=g_in, branch=g_branch, out=g_out, up=g_up, down=g_down)
    small = dict(rel_bias=d_rel_bias, wlr=dwlr[:GLA_RANK], blr=dblr, gnorm=dgnorm, ln1_g=dg1, ln1_b=db1, ln2_g=dg2, ln2_b=db2)
    return dh, big, small


def _reorder_normgate(ret_raw, gla_raw, rg, gg, gnorm):
    return f_normgate(ret_raw, rg, gla_raw, gg, gnorm)


def _gla_pre_bwd(gq, glr, dq2, dla, dgk, dgv, wlr, blr):
    _, pull = jax.vjp(f_gla_pre, gq, glr, wlr, blr)
    dgq, dglr, dwlr, dblr = pull((dq2, dla))
    dglr = jnp.concatenate([dglr, jnp.zeros((dglr.shape[0], PROJ_PAD - GLR_OFF - GLR_BLOCK), F32)], axis=1)
    return dgq, dglr, dgk, dgv, dwlr, dblr


def _ret_pre_bwd(rq, rk, cos, sin, drq2, drk2, drv):
    _, pull = jax.vjp(lambda a, b: f_ret_pre(a, b, cos, sin), rq, rk)
    drq, drk = pull((drq2, drk2))
    return drq, drk, drv


SHARDED = ("in", "branch", "out", "up", "down")

_FULL = {
    "branch": lambda g: [g[:, n].reshape(D_MODEL, D_MODEL) for n in range(3)],
    "out": lambda g: g.reshape(D_MODEL, D_MODEL),
    "up": lambda g: jnp.transpose(g, (1, 0, 2)).reshape(D_MODEL, D_FF),
    "down": lambda g: g.reshape(D_FF, D_MODEL),
}


def _owner_pieces(s):
    shard = PROJ_WIDTH // N_DEV
    lo, hi = s * shard, (s + 1) * shard
    pieces = []
    for a, b, padded in ((0, GLR_SRC, 0), (GLR_SRC, GATES_SRC, GLR_OFF), (GATES_SRC, PROJ_WIDTH, GATES_OFF)):
        x0, x1 = max(lo, a), min(hi, b)
        if x0 < x1:
            pieces.append((x0 - lo, x1 - x0, padded + x0 - a))
    return pieces


def _full_in(g):
    placed = sorted((p, s, c, w) for s in range(N_DEV) for c, w, p in _owner_pieces(s))
    cols = [g[s][:, c:c + w] for _, s, c, w in placed]
    return jnp.concatenate(cols + [jnp.zeros((D_MODEL, PROJ_PAD - PROJ_WIDTH), g.dtype)], axis=1)


def _full_wlr(g):
    wlr = jnp.transpose(g, (1, 0, 2)).reshape(GLA_RANK, LIN_HEADS * LIN_KEY_DIM)
    return jnp.pad(wlr, ((0, GLR_BLOCK - GLA_RANK), (0, 0)))


def _slab_in(g):
    return jnp.stack([jnp.concatenate([g[:, p:p + w] for _, w, p in _owner_pieces(s)], axis=1) for s in range(N_DEV)])


_SLABS = {
    "in": _slab_in,
    "branch": lambda gs: jnp.stack([b.reshape(N_DEV, D_MODEL // N_DEV, D_MODEL) for b in gs], axis=1),
    "out": lambda g: g.reshape(N_DEV, D_MODEL // N_DEV, D_MODEL),
    "up": lambda g: jnp.transpose(g.reshape(D_MODEL, N_DEV, D_FF // N_DEV), (1, 0, 2)),
    "down": lambda g: g.reshape(N_DEV, D_FF // N_DEV, D_MODEL),
}


def _pair_sums(name, slabs):
    sums = []
    for mine, sibs in zip(slabs, carrier(name, Side("pair", slabs))):
        width = mine.shape[-1]
        rows = math.prod(mine.shape[1:-1])
        tm = 256 if rows % 256 == 0 else 128
        nb = rows // tm
        my_rows = Row(mine.reshape(-1, width), rowmap=lambda i, nb=nb: (2 * (i // nb) + lax.axis_index("c")) * nb + i % nb)
        out, = row_call("pair_sum", lambda q, p: (p.astype(F32) + q.astype(F32),),
                        [Row(sibs.reshape(-1, width)), my_rows], [], [(width, WIRE_DTYPE)], tm=tm)
        sums.append(out.reshape(sibs.shape))
    return sums


def _sum_adam(name, parts, w, m, v, tm):
    C = w.shape[-1]
    rows = [Row(parts, lead=(s,)) for s in range(parts.shape[0])] + [Row(w), Row(m), Row(v)]
    return row_call(name, f_sum_adam, rows, [], [(C, F32)] * 4, tm=tm)


def kernel(x, ln_in_g, ln_in_b, w_in, rel_bias, gla_w_lr, gla_b_lr, gla_norm_g, w_branch, w_out, ln1_g, ln1_b, w_up, w_down, ln2_g, ln2_b, loss_target, m_ln_in_g, m_ln_in_b, m_w_in, m_rel_bias, m_gla_w_lr, m_gla_b_lr, m_gla_norm_g, m_w_branch, m_w_out, m_ln1_g, m_ln1_b, m_w_up, m_w_down, m_ln2_g, m_ln2_b, v_ln_in_g, v_ln_in_b, v_w_in, v_rel_bias, v_gla_w_lr, v_gla_b_lr, v_gla_norm_g, v_w_branch, v_w_out, v_ln1_g, v_ln1_b, v_w_up, v_w_down, v_ln2_g, v_ln2_b):
    T = x.shape[1]
    x2d = x.reshape(T, D_MODEL)
    tgt = loss_target.reshape(T, D_MODEL)
    me = 4 * lax.axis_index("x") + 2 * lax.axis_index("y") + lax.axis_index("c")

    inv = ROPE_BASE ** (-jnp.arange(0, LIN_KEY_DIM, 2, dtype=F32) / LIN_KEY_DIM)
    ang = jnp.arange(T, dtype=F32)[:, None] * inv[None, :]
    cos = jnp.concatenate([jnp.cos(ang), jnp.cos(ang)], axis=1)
    sin = jnp.concatenate([-jnp.sin(ang), jnp.sin(ang)], axis=1)
    log_gamma = jnp.log1p(-jnp.exp2(-5.0 - jnp.arange(LIN_HEADS, dtype=F32)))
    ret_decay = jnp.repeat(log_gamma, LIN_KEY_DIM).reshape(1, LIN_HEADS * LIN_KEY_DIM)
    consts = (cos, sin, ret_decay)

    layers = [dict(rel_bias=rel_bias[l], blr=_row2(gla_b_lr[l]), gnorm=_row2(gla_norm_g[l]), ln1_g=_row2(ln1_g[l]),
                   ln1_b=_row2(ln1_b[l]), ln2_g=_row2(ln2_g[l]), ln2_b=_row2(ln2_b[l])) for l in range(DEPTH)]
    shards = dict(branch=w_branch, out=w_out, up=w_up, down=w_down)
    shards["in"] = w_in
    wire = [{n: shards[n][l].astype(WIRE_DTYPE) for n in SHARDED} for l in range(DEPTH)]

    def gather_of(l, names):
        def landed(res):
            for n, g in zip(names, res):
                layers[l][n] = _full_in(g) if n == "in" else _FULL[n](g)
        return lambda ctx: (Side("gather", [wire[l][n] for n in names]), landed)

    first = carrier("gather_first", Side("gather", [wire[0]["in"]] + [gla_w_lr[l] for l in range(DEPTH)]))
    layers[0]["in"] = _full_in(first[0])
    for l in range(DEPTH):
        layers[l]["wlr"] = _full_wlr(first[1 + l])

    fwd_hooks = [{"proj": gather_of(1, ("in",)), "attn_fwd": gather_of(0, ("branch", "out")),
                  "ret_fwd": gather_of(1, ("branch", "out")), "gla_fwd": gather_of(0, ("up", "down")),
                  "mlp_up": gather_of(1, ("up", "down"))}, {}]
    h, hb = row_call("ln_in", _twice(f_ln_in), [Row(x2d)], [_row2(ln_in_g), _row2(ln_in_b)], STREAM, tm=512)
    saved = []
    for l in range(DEPTH):
        h, hb, s = _layer_forward(h, hb, layers[l], consts, fwd_hooks[l])
        saved.append(s)
    dh, loss_lanes = row_call("loss", f_loss, [Row(h), Row(tgt)], [], [(D_MODEL, F32)], [(1, 128)], tm=512)

    received = [dict() for _ in range(DEPTH)]

    def scatter_of(l, names, sums=None):
        def make(ctx):
            parts = sums if sums is not None else _pair_sums("pair_grads", [_SLABS[n](ctx[n]) for n in names])

            def landed(res):
                received[l].update(zip(names, res))
            return Side("chips", parts), landed
        return make

    small = [None] * DEPTH
    dh, big, small[1] = _layer_backward(dh, saved[1], layers[1], consts, {})
    sums1 = _pair_sums("pair_grads", [_SLABS[n](big["g_in"] if n == "in" else big[n]) for n in SHARDED])
    rest = SHARDED[1:]
    dh, big, small[0] = _layer_backward(dh, saved[0], layers[0], consts, {
        "gla_bwd": scatter_of(1, ("in",), sums1[:1]), "attn_bwd": scatter_of(1, rest, sums1[1:]),
        "proj_wgrad": scatter_of(0, rest), "proj_bwd": scatter_of(0, ("in",))})
    grad_x, d_in_g, d_in_b = row_call("ln_in_bwd", vjp_fn(f_ln_in, 1, 1), [Row(x2d), Row(dh)],
                                      [_row2(ln_in_g), _row2(ln_in_b)], [(D_MODEL, F32)], [(1, D_MODEL), (1, D_MODEL)], tm=256)

    small_names = ("rel_bias", "wlr", "blr", "gnorm", "ln1_g", "ln1_b", "ln2_g", "ln2_b")
    parts = [d_in_g, d_in_b] + [small[l][n] for l in range(DEPTH) for n in small_names] + [loss_lanes]
    shapes = [(D_MODEL,), (D_MODEL,)] + [s for l in range(DEPTH) for s in (
        (ATTN_HEADS, N_REL), (GLA_RANK, 512), (512,), (LIN_VAL_DIM,), (D_MODEL,), (D_MODEL,), (D_MODEL,), (D_MODEL,))] + [(128,)]
    red = _unpack(all_reduce_small(_pack(parts)), shapes)
    loss = jnp.sum(red[-1])
    g_small = {"ln_in_g": red[0], "ln_in_b": red[1]}
    for l in range(DEPTH):
        for j, n in enumerate(small_names):
            g_small[(l, n)] = red[2 + l * len(small_names) + j]
    shard = LIN_HEADS * LIN_KEY_DIM // N_DEV
    g_wlr = jnp.stack([lax.dynamic_slice_in_dim(g_small[(l, "wlr")], me * shard, shard, axis=1) for l in range(DEPTH)])
    stack = lambda n: jnp.stack([g_small[(l, n)] for l in range(DEPTH)])
    g_rep = [g_small["ln_in_g"], g_small["ln_in_b"], stack("rel_bias"), g_wlr, stack("blr"), stack("gnorm"),
             stack("ln1_g"), stack("ln1_b"), stack("ln2_g"), stack("ln2_b")]
    w_rep = [ln_in_g, ln_in_b, rel_bias, gla_w_lr, gla_b_lr, gla_norm_g, ln1_g, ln1_b, ln2_g, ln2_b]
    m_rep = [m_ln_in_g, m_ln_in_b, m_rel_bias, m_gla_w_lr, m_gla_b_lr, m_gla_norm_g, m_ln1_g, m_ln1_b, m_ln2_g, m_ln2_b]
    v_rep = [v_ln_in_g, v_ln_in_b, v_rel_bias, v_gla_w_lr, v_gla_b_lr, v_gla_norm_g, v_ln1_g, v_ln1_b, v_ln2_g, v_ln2_b]
    rep_shapes = [w.shape for w in w_rep]
    d_pack, m_pack, v_pack = row_call("adam_small", f_adam, [Row(_pack(g_rep)), Row(_pack(w_rep)), Row(_pack(m_rep)), Row(_pack(v_rep))],
                                      [], [(128, F32)] * 3, tm=256)
    d_rep, nm_rep, nv_rep = _unpack(d_pack, rep_shapes), _unpack(m_pack, rep_shapes), _unpack(v_pack, rep_shapes)

    def owned(key, w, m, v, name, tm):
        res = []
        for l in range(DEPTH):
            shp = w[l].shape
            flat = lambda a: a.reshape(-1, shp[-1])
            parts_l = received[l][key].reshape(4, -1, shp[-1])
            res.append([o.reshape(shp) for o in _sum_adam(name, parts_l, flat(w[l]), flat(m[l]), flat(v[l]), tm)])
        return [jnp.stack([res[l][j] for l in range(DEPTH)]) for j in range(4)]

    o_in = owned("in", w_in, m_w_in, v_w_in, "adam_w_in", 256)
    o_br = owned("branch", w_branch, m_w_branch, v_w_branch, "adam_w_branch", 128)
    o_out = owned("out", w_out, m_w_out, v_w_out, "adam_w_out", 128)
    o_up = owned("up", w_up, m_w_up, v_w_up, "adam_w_up", 256)
    o_down = owned("down", w_down, m_w_down, v_w_down, "adam_w_down", 256)

    def tree(j, rep):
        return [rep[0], rep[1], o_in[j], rep[2], rep[3], rep[4], rep[5], o_br[j], o_out[j], rep[6], rep[7], o_up[j], o_down[j],
                rep[8], rep[9]]

    return (loss, grad_x.reshape(1, T, D_MODEL), *tree(0, g_rep), *tree(1, d_rep), *tree(2, nm_rep), *tree(3, nv_rep))
```

```python
import functools
import math

import jax
import jax.numpy as jnp
from jax import lax
from jax.experimental import pallas as pl
from jax.experimental.pallas import tpu as pltpu

F32 = jnp.float32
MXU_DTYPE = jnp.bfloat16
WIRE_DTYPE = jnp.bfloat16

N_DEV = 8
D_MODEL = 1024
DEPTH = 2
CHUNK = 64
ATTN_HEADS = 8
ATTN_HEAD_DIM = 128
ATTN_LEFT_CHUNKS = 8
BAND = (ATTN_LEFT_CHUNKS + 1) * CHUNK
PAD_ROWS = ATTN_LEFT_CHUNKS * CHUNK
REL_CLIP = 2 * CHUNK
N_REL = 2 * REL_CLIP + 1
REL_MIN = REL_CLIP - (CHUNK - 1)
LIN_HEADS = 4
LIN_KEY_DIM = 128
LIN_VAL_DIM = 256
GLA_RANK = 16
GLA_GATE_NORMALIZER = 16.0
ROPE_BASE = 10000.0
D_FF = 4 * D_MODEL
LN_EPS = 1e-5
NEG_INF = -1e30
ALPHA = (2 * DEPTH) ** 0.25
ADAM_LR, ADAM_B1, ADAM_B2, ADAM_EPS, ADAM_WD, ADAM_STEP = 0.001, 0.9, 0.999, 1e-08, 0.01, 10

PROJ_WIDTH = 12304
GLR_SRC = 9216
GATES_SRC = GLR_SRC + GLA_RANK
MAIN_W = 9216
GATES_W = 3 * D_MODEL
GATES_OFF = MAIN_W
GLR_OFF = MAIN_W + GATES_W
PROJ_PAD = 12800
GLR_BLOCK = 128
COL = dict(aq=0, ak=1024, av=2048, rq=3072, rk=3584, rv=4096, rg=5120, gq=6144, gk=6656, gv=7168, gg=8192)

SEG_CHUNKS = 8
SEG = SEG_CHUNKS * CHUNK
VMEM_BIG = 56 * 1024 * 1024

_NN = (((1,), (0,)), ((), ()))
_NT = (((1,), (1,)), ((), ()))
_TN = (((0,), (0,)), ((), ()))


def _mxu(a, b, dn):
    return lax.dot_general(a.astype(MXU_DTYPE), b.astype(MXU_DTYPE), dn, preferred_element_type=F32)


@jax.custom_vjp
def dot_nn(a, b):
    return _mxu(a, b, _NN)


def _dot_nn_fwd(a, b):
    return _mxu(a, b, _NN), (a, b)


def _dot_nn_bwd(res, g):
    a, b = res
    return _mxu(g, b, _NT).astype(a.dtype), _mxu(a, g, _TN).astype(b.dtype)


dot_nn.defvjp(_dot_nn_fwd, _dot_nn_bwd)


@jax.custom_vjp
def dot_nt(a, b):
    return _mxu(a, b, _NT)


def _dot_nt_fwd(a, b):
    return _mxu(a, b, _NT), (a, b)


def _dot_nt_bwd(res, g):
    a, b = res
    return _mxu(g, b, _NN).astype(a.dtype), _mxu(g, a, _TN).astype(b.dtype)


dot_nt.defvjp(_dot_nt_fwd, _dot_nt_bwd)


@jax.custom_vjp
def dot_tn(a, b):
    return _mxu(a, b, _TN)


def _dot_tn_fwd(a, b):
    return _mxu(a, b, _TN), (a, b)


def _dot_tn_bwd(res, g):
    a, b = res
    return _mxu(b, g, _NT).astype(a.dtype), _mxu(a, g, _NN).astype(b.dtype)


dot_tn.defvjp(_dot_tn_fwd, _dot_tn_bwd)


def _tri(n, lower):
    r = lax.broadcasted_iota(jnp.int32, (n, n), 0)
    c = lax.broadcasted_iota(jnp.int32, (n, n), 1)
    return (r >= c) if lower else (r <= c)


def _f32_dot(a, b):
    return lax.dot_general(a, b, _NN, precision=lax.Precision.HIGHEST, preferred_element_type=F32)


@jax.custom_vjp
def chunk_cumsum(x):
    return _f32_dot(_tri(x.shape[0], True).astype(F32), x)


def _chunk_cumsum_fwd(x):
    return chunk_cumsum(x), None


def _chunk_cumsum_bwd(_, g):
    return (_f32_dot(_tri(g.shape[0], False).astype(F32), g),)


chunk_cumsum.defvjp(_chunk_cumsum_fwd, _chunk_cumsum_bwd)


@jax.custom_vjp
def swap_halves(t):
    return pltpu.roll(t, t.shape[-1] // 2, t.ndim - 1)


def _swap_fwd(t):
    return swap_halves(t), None


def _swap_bwd(_, g):
    return (swap_halves(g),)


swap_halves.defvjp(_swap_fwd, _swap_bwd)


def mm(name, a, b, dims, *, tm, tn, tk, out_dtype=F32, extras=(), epi=None, n_out=1, out_dtypes=None, side=None,
       column_slabs=False):
    if dims == "nn":
        (M, K), N = a.shape, b.shape[1]
    elif dims == "nt":
        (M, K), N = a.shape, b.shape[0]
    else:
        (K, M), N = a.shape, b.shape[1]
    tm, tn, tk = min(tm, M), min(tn, N), min(tk, K)
    assert M % tm == 0 and N % tn == 0 and K % tk == 0, (name, M, N, K, tm, tn, tk)
    nk = K // tk
    dn = {"nn": _NN, "nt": _NT, "tn": _TN}[dims]
    a_spec = pl.BlockSpec((tk, tm), lambda i, j, k: (k, i)) if dims == "tn" else pl.BlockSpec((tm, tk), lambda i, j, k: (i, k))
    b_spec = pl.BlockSpec((tn, tk), lambda i, j, k: (j, k)) if dims == "nt" else pl.BlockSpec((tk, tn), lambda i, j, k: (k, j))
    o_spec = pl.BlockSpec((tm, tn), lambda i, j, k: (i, j))
    n_ex = len(extras)
    out_dtypes = out_dtypes or (out_dtype,) * n_out

    def finish(r, ex_refs, o_refs):
        if epi is not None:
            r = epi(r, *[e[...] for e in ex_refs])
        r = r if isinstance(r, tuple) else (r,)
        for o, v in zip(o_refs, r):
            o[...] = v.astype(o.dtype)

    def body(*refs):
        a_ref, b_ref = refs[:2]
        ex_refs = refs[2:2 + n_ex]
        o_refs = refs[2 + n_ex:2 + n_ex + n_out]
        part = _mxu(a_ref[...], b_ref[...], dn)
        if nk == 1:
            finish(part, ex_refs, o_refs)
            return
        acc = refs[-1]
        k = pl.program_id(2)

        @pl.when(k == 0)
        def _():
            acc[...] = part

        @pl.when(k > 0)
        def _():
            acc[...] += part

        @pl.when(k == nk - 1)
        def _():
            finish(acc[...], ex_refs, o_refs)

    outs = _side_call(
        side, body,
        name=name,
        grid=(M // tm, N // tn, nk),
        in_specs=[a_spec, b_spec] + [o_spec] * n_ex,
        out_specs=[pl.BlockSpec((None, tm, tn), lambda i, j, k: (j, i, 0))] * n_out if column_slabs else [o_spec] * n_out,
        out_shape=[jax.ShapeDtypeStruct((N // tn, M, tn) if column_slabs else (M, N), dt) for dt in out_dtypes],
        scratch_shapes=[pltpu.VMEM((tm, tn), F32)] if nk > 1 else [],
        args=[a, b, *extras],
        semantics=("parallel", "parallel", "arbitrary"),
    )
    return outs[0] if n_out == 1 else outs


class Row:
    def __init__(self, arr, width=None, col=0, lead=(), rowmap=None):
        self.arr, self.lead, self.col = arr, tuple(lead), col
        self.width = arr.shape[-1] if width is None else width
        self.rowmap = rowmap

    def spec(self, tm):
        lead, col, rowmap = self.lead, self.col, self.rowmap or (lambda i: i)
        return pl.BlockSpec((None,) * len(lead) + (tm, self.width), lambda i: (*lead, rowmap(i), col))


def row_call(name, fn, rows, params, new_outs, red_shapes=(), *, tm, into=None, into_cols=(), stack=None, stacked_onto=None):
    R = rows[0].arr.shape[len(rows[0].lead)]
    tm = min(tm, R)
    assert R % tm == 0, (name, R, tm)
    n_row, n_par, n_new, n_into, n_red = len(rows), len(params), len(new_outs), len(into_cols), len(red_shapes)
    fresh = isinstance(into, jax.ShapeDtypeStruct)

    def body(*refs):
        row_refs = refs[:n_row]
        par_refs = refs[n_row:n_row + n_par]
        pos = n_row + n_par + (1 if (into is not None and not fresh) else 0) + (n_new if stacked_onto is not None else 0)
        new_refs = refs[pos:pos + n_new]
        red_refs = refs[pos + n_new:pos + n_new + n_red]
        pos2 = pos + n_new + n_red
        into_ref = refs[pos2] if into is not None else None
        stage = refs[pos2 + 1:] if into is not None else ()
        i = pl.program_id(0)
        vals = fn(*[r[...] for r in row_refs], *[p[...] for p in par_refs])
        vals = vals if isinstance(vals, (tuple, list)) else (vals,)
        for o, v in zip(new_refs, vals[:n_new]):
            o[...] = v.astype(o.dtype)
        for (width, off), buf, v in zip(into_cols, stage, vals[n_new:n_new + n_into]):
            buf[...] = v.astype(buf.dtype)
            pltpu.sync_copy(buf, into_ref.at[pl.ds(pl.multiple_of(i * tm, tm), tm), pl.ds(off, width)])
        for o, v in zip(red_refs, vals[n_new + n_into:]):
            @pl.when(i == 0)
            def _(o=o, v=v):
                o[...] = v.astype(o.dtype)

            @pl.when(i > 0)
            def _(o=o, v=v):
                o[...] += v.astype(o.dtype)

    in_specs = [r.spec(tm) for r in rows] + [pl.BlockSpec(p.shape, lambda i, nd=p.ndim: (0,) * nd) for p in params]
    args = [r.arr for r in rows] + list(params)
    if stack is None:
        out_specs = [pl.BlockSpec((tm, w), lambda i: (i, 0)) for w, _ in new_outs]
        out_shape = [jax.ShapeDtypeStruct((R, w), dt) for w, dt in new_outs]
    else:
        index, depth = stack
        out_specs = [pl.BlockSpec((None, tm, w), lambda i: (index, i, 0)) for w, _ in new_outs]
        out_shape = [jax.ShapeDtypeStruct((depth, R, w), dt) for w, dt in new_outs]
    out_specs += [pl.BlockSpec(s, lambda i, nd=len(s): (0,) * nd) for s in red_shapes]
    out_shape += [jax.ShapeDtypeStruct(s, F32) for s in red_shapes]
    aliases, scratch = {}, []
    if into is not None:
        out_specs.append(pl.BlockSpec(memory_space=pl.ANY))
        out_shape.append(jax.ShapeDtypeStruct(into.shape, into.dtype))
        if not fresh:
            in_specs.append(pl.BlockSpec(memory_space=pl.ANY))
            args.append(into)
            aliases = {len(args) - 1: len(out_shape) - 1}
        scratch = [pltpu.VMEM((tm, w), into.dtype) for w, _ in into_cols]
    if stacked_onto is not None:
        for j, prev in enumerate(stacked_onto):
            in_specs.append(pl.BlockSpec(memory_space=pl.ANY))
            args.append(prev)
            aliases[len(args) - 1] = j
    outs = pl.pallas_call(
        body,
        name=name,
        grid=(R // tm,),
        in_specs=in_specs,
        out_specs=out_specs,
        out_shape=out_shape,
        scratch_shapes=scratch,
        input_output_aliases=aliases,
        compiler_params=pltpu.CompilerParams(dimension_semantics=("arbitrary",), vmem_limit_bytes=VMEM_BIG),
    )(*args)
    return list(outs)


def vjp_fn(fn, n_row, n_out, passthrough=0):
    def bwd(*args):
        rows = [a.astype(F32) for a in args[:n_row]]
        cts = args[n_row:n_row + n_out]
        thru = args[n_row + n_out:n_row + n_out + passthrough]
        pars = args[n_row + n_out + passthrough:]
        outs, pull = jax.vjp(lambda *xs: tuple(fn(*xs)), *rows, *pars)
        grads = pull(tuple(c.astype(o.dtype) for c, o in zip(cts, outs)))
        return tuple(grads[:n_row]) + tuple(thru) + tuple(grads[n_row:])

    return bwd


def _ln(x, g, b):
    mu = jnp.mean(x, axis=-1, keepdims=True)
    xc = x - mu
    var = jnp.mean(xc * xc, axis=-1, keepdims=True)
    return xc * lax.rsqrt(var + LN_EPS) * g + b


def f_ln_in(x, g, b):
    return (_ln(x, g, b),)


def f_deepnorm(x, y, g, b):
    return (_ln(ALPHA * x + y, g, b),)


def _twice(fn):
    return lambda *args: fn(*args) * 2


STREAM = [(D_MODEL, F32), (D_MODEL, MXU_DTYPE)]


def _rope(t, cos, sin):
    parts = []
    for h in range(LIN_HEADS):
        th = t[:, h * LIN_KEY_DIM:(h + 1) * LIN_KEY_DIM]
        parts.append(th * cos + swap_halves(th) * sin)
    return jnp.concatenate(parts, axis=1)


def f_ret_pre(rq, rk, cos, sin):
    return _rope(rq, cos, sin), _rope(rk, cos, sin) * LIN_KEY_DIM ** -0.5


def _log_sigmoid(x):
    return jnp.minimum(x, 0.0) - jnp.log(1.0 + jnp.exp(-jnp.abs(x)))


def f_gla_pre(gq, glr, wlr, blr):
    gate_pre = dot_nn(glr, wlr) + blr
    return gq * LIN_KEY_DIM ** -0.5, _log_sigmoid(gate_pre) / GLA_GATE_NORMALIZER


def f_normgate(ret_raw, rg, gla_raw, gg, gnorm):
    rets, glas = [], []
    for h in range(LIN_HEADS):
        sl = slice(h * LIN_VAL_DIM, (h + 1) * LIN_VAL_DIM)
        t = ret_raw[:, sl]
        mu = jnp.mean(t, axis=-1, keepdims=True)
        tc = t - mu
        rets.append(tc * lax.rsqrt(jnp.mean(tc * tc, axis=-1, keepdims=True) + LN_EPS))
        u = gla_raw[:, sl]
        glas.append(u * lax.rsqrt(jnp.mean(u * u, axis=-1, keepdims=True) + LN_EPS) * gnorm)
    ret = jax.nn.silu(rg) * jnp.concatenate(rets, axis=1)
    gla = jax.nn.silu(gg) * jnp.concatenate(glas, axis=1)
    return ret, gla


def f_merge(p0, p1, p2, gl):
    out = jax.nn.sigmoid(gl[:, :D_MODEL]) * p0
    out = out + jax.nn.sigmoid(gl[:, D_MODEL:2 * D_MODEL]) * p1
    return (out + jax.nn.sigmoid(gl[:, 2 * D_MODEL:]) * p2,)


def f_loss(y, tgt):
    diff = y - tgt
    sq = jnp.sum(diff * diff, axis=0, keepdims=True)
    lanes = sq[:, :128]
    for j in range(1, D_MODEL // 128):
        lanes = lanes + sq[:, j * 128:(j + 1) * 128]
    return diff * (1.0 / D_MODEL), lanes * (0.5 / D_MODEL)


def _adam(g, w, m, v):
    m = ADAM_B1 * m + (1.0 - ADAM_B1) * g
    v = ADAM_B2 * v + (1.0 - ADAM_B2) * jnp.square(g)
    m_hat = m / (1.0 - ADAM_B1 ** ADAM_STEP)
    v_hat = v / (1.0 - ADAM_B2 ** ADAM_STEP)
    delta = -ADAM_LR * (m_hat / (jnp.sqrt(v_hat) + ADAM_EPS) + ADAM_WD * w)
    return delta, m, v


def f_sum_adam(*args):
    g = args[0].astype(F32)
    for p in args[1:-3]:
        g = g + p.astype(F32)
    return (g,) + _adam(g, *args[-3:])


def f_adam(g, w, m, v):
    return _adam(g, w, m, v)


GROUP = 4
GROUP_ROWS = GROUP * CHUNK
GROUP_KEYS = (GROUP + ATTN_LEFT_CHUNKS) * CHUNK
DIAG_ROWS = REL_CLIP + CHUNK
GRAD_ROWS = 200


def _blocks_by_delta():
    blocks = {}
    for qi in range(GROUP):
        for J in range(GROUP_KEYS // 128):
            blocks.setdefault(qi - 2 * J, []).append((qi, J))
    return blocks


def bias_table(rel_bias_l):
    def body(rb_ref, o_ref):
        h = pl.program_id(0)
        rm = lax.broadcasted_iota(jnp.int32, (CHUNK, 128), 0)
        cm = lax.broadcasted_iota(jnp.int32, (CHUNK, 128), 1)
        ahead = jnp.where(cm >= CHUNK, 1, 0)
        for delta, where in _blocks_by_delta().items():
            base = 64 * delta + PAD_ROWS
            rel = jnp.minimum(base + rm - cm, REL_CLIP) + REL_CLIP
            in_band = (ahead - delta >= 0) & (ahead - delta <= ATTN_LEFT_CHUNKS)
            lo = max(REL_MIN, min(base - 127, REL_CLIP) + REL_CLIP)
            hi = min(base + CHUNK - 1, REL_CLIP) + REL_CLIP

            def step(r, acc, rel=rel):
                return jnp.where(rel == r, rb_ref[h * N_REL + r], acc)

            blk = jnp.zeros((CHUNK, 128), F32) if hi < lo else lax.fori_loop(lo, hi + 1, step, jnp.zeros((CHUNK, 128), F32))
            blk = jnp.where(in_band, blk, NEG_INF)
            for qi, J in where:
                o_ref[pl.ds(qi * CHUNK, CHUNK), pl.ds(J * 128, 128)] = blk

    return pl.pallas_call(
        body,
        name="bias_table",
        grid=(ATTN_HEADS,),
        in_specs=[pl.BlockSpec(memory_space=pltpu.SMEM)],
        out_specs=pl.BlockSpec((None, GROUP_ROWS, GROUP_KEYS), lambda h: (h, 0, 0)),
        out_shape=jax.ShapeDtypeStruct((ATTN_HEADS, GROUP_ROWS, GROUP_KEYS), F32),
        compiler_params=pltpu.CompilerParams(dimension_semantics=("arbitrary",)),
    )(rel_bias_l.reshape(ATTN_HEADS * N_REL))


def bias_table_grad(dbias):
    blocks = _blocks_by_delta()
    below_clip = range(-(CHUNK - 1), REL_CLIP)
    deltas = [d for d in sorted(blocks) if any(-127 <= pos - 64 * d - PAD_ROWS <= CHUNK - 1 for pos in below_clip)]

    def body(d_ref, o_ref, diag):
        sums = []
        for delta in deltas:
            acc = None
            for qi, J in blocks[delta]:
                blk = d_ref[pl.ds(qi * CHUNK, CHUNK), pl.ds(J * 128, 128)]
                acc = blk if acc is None else acc + blk
            sums.append(acc)
        off = lax.broadcasted_iota(jnp.int32, (CHUNK, 128), 0) - lax.broadcasted_iota(jnp.int32, (CHUNK, 128), 1)
        diag[...] = jnp.zeros_like(diag)

        def step(t, carry):
            pos = t - (CHUNK - 1)
            picked = jnp.zeros((CHUNK, 128), F32)
            for delta, s in zip(deltas, sums):
                picked = picked + jnp.where(off == pos - 64 * delta - PAD_ROWS, s, 0.0)
            diag[pl.ds(t, 1), :] = jnp.sum(picked, axis=0, keepdims=True)
            return carry

        lax.fori_loop(0, DIAG_ROWS - 1, step, 0)
        d = d_ref[...]
        total = jnp.sum(jnp.sum(d, axis=1, keepdims=True), axis=0, keepdims=True)
        o_ref[pl.ds(0, DIAG_ROWS), :] = jnp.broadcast_to(jnp.sum(diag[...], axis=1, keepdims=True), (DIAG_ROWS, 128))
        o_ref[pl.ds(DIAG_ROWS, GRAD_ROWS - DIAG_ROWS), :] = jnp.broadcast_to(total, (GRAD_ROWS - DIAG_ROWS, 128))

    out = pl.pallas_call(
        body,
        name="bias_table_grad",
        grid=(ATTN_HEADS,),
        in_specs=[pl.BlockSpec((None, GROUP_ROWS, GROUP_KEYS), lambda h: (h, 0, 0))],
        out_specs=pl.BlockSpec((None, GRAD_ROWS, 128), lambda h: (h, 0, 0)),
        out_shape=jax.ShapeDtypeStruct((ATTN_HEADS, GRAD_ROWS, 128), F32),
        scratch_shapes=[pltpu.VMEM((DIAG_ROWS, 128), F32)],
        compiler_params=pltpu.CompilerParams(dimension_semantics=("arbitrary",)),
    )(dbias)
    below = out[:, :DIAG_ROWS - 1, 0]
    clipped = out[:, DIAG_ROWS, 0] - jnp.sum(below, axis=1)
    return jnp.concatenate([jnp.zeros((ATTN_HEADS, REL_MIN), F32), below, clipped[:, None]], axis=1)


def _attn_stage(q_ref, k_ref, v_ref, qs, kp, vp, T):
    qs[...] = (q_ref[...] * ATTN_HEAD_DIM ** -0.5).astype(qs.dtype)
    for dst, src in ((kp, k_ref), (vp, v_ref)):
        dst[pl.ds(0, PAD_ROWS), :] = jnp.zeros((PAD_ROWS, ATTN_HEAD_DIM), dst.dtype)
        dst[pl.ds(PAD_ROWS, T), :] = src[...].astype(dst.dtype)


def _attn_probs(g, first_groups, qs, kp, b_ref):
    r0 = g * GROUP_ROWS if first_groups else pl.multiple_of(g * GROUP_ROWS, GROUP_ROWS)
    qc = qs[pl.ds(r0, GROUP_ROWS), :]
    kb = kp[pl.ds(r0, GROUP_KEYS), :]
    s = _mxu(qc, kb, _NT) + b_ref[...]
    if first_groups:
        k_chunk = lax.shift_right_logical(lax.broadcasted_iota(jnp.int32, (GROUP_ROWS, GROUP_KEYS), 1), 6)
        s = jnp.where(g * GROUP + k_chunk >= ATTN_LEFT_CHUNKS, s, NEG_INF)
    e = jnp.exp(s - jnp.max(s, axis=-1, keepdims=True))
    return r0, qc, kb, e / jnp.sum(e, axis=-1, keepdims=True)


def _for_groups(T, step):
    n_first = min(ATTN_LEFT_CHUNKS // GROUP, T // GROUP_ROWS)
    for g in range(n_first):
        step(g, True)
    lax.fori_loop(n_first, T // GROUP_ROWS, lambda g, c: (step(g, False), c)[1], 0)


def _head_spec(T, base):
    return pl.BlockSpec((T, ATTN_HEAD_DIM), lambda h: (0, base + h))


def _attn_scratch(T):
    return [pltpu.VMEM((T, ATTN_HEAD_DIM), MXU_DTYPE)] + [pltpu.VMEM((T + PAD_ROWS, ATTN_HEAD_DIM), MXU_DTYPE)] * 2


def attn_fwd(proj, bias_tab, side=None):
    T = proj.shape[0]

    def body(q_ref, k_ref, v_ref, b_ref, o_ref, qs, kp, vp):
        _attn_stage(q_ref, k_ref, v_ref, qs, kp, vp, T)

        def step(g, first):
            r0, _, _, p = _attn_probs(g, first, qs, kp, b_ref)
            o_ref[pl.ds(r0, GROUP_ROWS), :] = _mxu(p, vp[pl.ds(r0, GROUP_KEYS), :], _NN).astype(o_ref.dtype)

        _for_groups(T, step)

    out, = _side_call(
        side, body,
        name="attn_fwd",
        grid=(ATTN_HEADS,),
        in_specs=[_head_spec(T, COL["aq"] // 128), _head_spec(T, COL["ak"] // 128), _head_spec(T, COL["av"] // 128),
                  pl.BlockSpec((None, GROUP_ROWS, GROUP_KEYS), lambda h: (h, 0, 0))],
        out_specs=[pl.BlockSpec((T, ATTN_HEAD_DIM), lambda h: (0, h))],
        out_shape=[jax.ShapeDtypeStruct((T, ATTN_HEADS * ATTN_HEAD_DIM), MXU_DTYPE)],
        scratch_shapes=_attn_scratch(T),
        args=[proj, proj, proj, bias_tab],
    )
    return out


def attn_bwd(proj, bias_tab, dout, dproj, side=None):
    T = proj.shape[0]

    def body(q_ref, k_ref, v_ref, b_ref, do_ref, dproj_in, dbias_ref, dproj_ref, qs, kp, vp, dkp, dvp, dq, stage):
        del dproj_in
        h = pl.program_id(0)
        _attn_stage(q_ref, k_ref, v_ref, qs, kp, vp, T)
        dkp[...] = jnp.zeros_like(dkp)
        dvp[...] = jnp.zeros_like(dvp)
        dbias_ref[...] = jnp.zeros_like(dbias_ref)

        def step(g, first):
            r0, qc, kb, p = _attn_probs(g, first, qs, kp, b_ref)
            do = do_ref[pl.ds(r0, GROUP_ROWS), :]
            dvp[pl.ds(r0, GROUP_KEYS), :] += _mxu(p, do, _TN)
            dp = _mxu(do, vp[pl.ds(r0, GROUP_KEYS), :], _NT)
            ds = p * (dp - jnp.sum(dp * p, axis=-1, keepdims=True))
            dq[pl.ds(r0, GROUP_ROWS), :] = _mxu(ds, kb, _NN) * ATTN_HEAD_DIM ** -0.5
            dkp[pl.ds(r0, GROUP_KEYS), :] += _mxu(ds, qc, _TN)
            dbias_ref[...] += ds

        _for_groups(T, step)
        col = pl.multiple_of(h * ATTN_HEAD_DIM, ATTN_HEAD_DIM)
        for grad, first_row, base in ((dq, 0, "aq"), (dkp, PAD_ROWS, "ak"), (dvp, PAD_ROWS, "av")):
            stage[...] = grad[pl.ds(first_row, T), :].astype(stage.dtype)
            pltpu.sync_copy(stage, dproj_ref.at[:, pl.ds(COL[base] + col, ATTN_HEAD_DIM)])

    dbias, dproj = _side_call(
        side, body,
        name="attn_bwd",
        grid=(ATTN_HEADS,),
        in_specs=[_head_spec(T, COL["aq"] // 128), _head_spec(T, COL["ak"] // 128), _head_spec(T, COL["av"] // 128),
                  pl.BlockSpec((None, GROUP_ROWS, GROUP_KEYS), lambda h: (h, 0, 0)),
                  pl.BlockSpec((T, ATTN_HEAD_DIM), lambda h: (0, h)),
                  pl.BlockSpec(memory_space=pl.ANY)],
        out_specs=[pl.BlockSpec((None, GROUP_ROWS, GROUP_KEYS), lambda h: (h, 0, 0)), pl.BlockSpec(memory_space=pl.ANY)],
        out_shape=[jax.ShapeDtypeStruct((ATTN_HEADS, GROUP_ROWS, GROUP_KEYS), F32),
                   jax.ShapeDtypeStruct(dproj.shape, dproj.dtype)],
        scratch_shapes=_attn_scratch(T) + [pltpu.VMEM((T + PAD_ROWS, ATTN_HEAD_DIM), F32)] * 2
        + [pltpu.VMEM((T, ATTN_HEAD_DIM), F32), pltpu.VMEM((T, ATTN_HEAD_DIM), dproj.dtype)],
        args=[proj, proj, proj, bias_tab, dout, dproj],
        aliases={5: 1},
    )
    return dproj, dbias


def _lin_segment(q, k, v, la, st):
    lower = _tri(CHUNK, True)
    outs = []
    for i in range(SEG_CHUNKS):
        sl = slice(i * CHUNK, (i + 1) * CHUNK)
        qi, ki, vi, lai = q[sl], k[sl], v[sl], la[sl]
        cum = chunk_cumsum(lai)
        last = jnp.sum(lai, axis=0, keepdims=True)
        e_pos, e_neg = jnp.exp(cum), jnp.exp(-cum)
        q_fwd = qi * e_pos
        s_fwd = dot_nt(q_fwd, ki * e_neg)
        s_bwd = dot_nt(qi * e_neg, ki * e_pos)
        o = dot_nn(jnp.where(lower, s_fwd, s_bwd), vi) + dot_nt(q_fwd, st)
        st = jnp.exp(last) * st + dot_tn(vi, ki * jnp.exp(last - cum))
        outs.append(o)
    return jnp.concatenate(outs, axis=0), st


class Win:
    def __init__(self, arr, width, base=0):
        self.arr, self.width, self.base = arr, width, base


def _lin_spec(w, seg_of):
    base = w.base
    if w.arr.shape[0] == 1:
        return pl.BlockSpec((1, w.width), lambda h, s: (0, base + h))
    return pl.BlockSpec((SEG, w.width), lambda h, s: (seg_of(s), base + h))


def _decay_rows(la_ref):
    la = la_ref[...]
    return jnp.broadcast_to(la, (SEG, LIN_KEY_DIM)) if la.shape[0] == 1 else la


def lin_fwd(name, q, k, v, la, side=None):
    T = q.arr.shape[0]
    nseg = T // SEG

    def body(q_ref, k_ref, v_ref, la_ref, o_ref, st_ref, st):
        @pl.when(pl.program_id(1) == 0)
        def _():
            st[...] = jnp.zeros_like(st)

        st0 = st[...]
        st_ref[...] = st0
        o, st1 = _lin_segment(q_ref[...], k_ref[...], v_ref[...], _decay_rows(la_ref), st0)
        o_ref[...] = o
        st[...] = st1

    fwd = lambda s: s
    out, states = _side_call(
        side, body,
        name=name,
        grid=(LIN_HEADS, nseg),
        in_specs=[_lin_spec(w, fwd) for w in (q, k, v, la)],
        out_specs=[pl.BlockSpec((SEG, LIN_VAL_DIM), lambda h, s: (s, h)),
                   pl.BlockSpec((None, None, LIN_VAL_DIM, LIN_KEY_DIM), lambda h, s: (h, s, 0, 0))],
        out_shape=[jax.ShapeDtypeStruct((T, LIN_HEADS * LIN_VAL_DIM), F32),
                   jax.ShapeDtypeStruct((LIN_HEADS, nseg, LIN_VAL_DIM, LIN_KEY_DIM), F32)],
        scratch_shapes=[pltpu.VMEM((LIN_VAL_DIM, LIN_KEY_DIM), F32)],
        args=[q.arr, k.arr, v.arr, la.arr],
    )
    return out, states


def lin_bwd(name, q, k, v, la, states, dout, with_decay_grad, side=None):
    T = q.arr.shape[0]
    nseg = T // SEG
    n_out = 4 if with_decay_grad else 3

    def body(q_ref, k_ref, v_ref, la_ref, st_ref, do_ref, *rest):
        outs, dst = rest[:n_out], rest[n_out]

        @pl.when(pl.program_id(1) == 0)
        def _():
            dst[...] = jnp.zeros_like(dst)

        _, pull = jax.vjp(_lin_segment, q_ref[...].astype(F32), k_ref[...].astype(F32), v_ref[...].astype(F32),
                          _decay_rows(la_ref), st_ref[...])
        grads = pull((do_ref[...], dst[...]))
        for o, g in zip(outs, grads[:n_out]):
            o[...] = g
        dst[...] = grads[4]

    rev = lambda s: nseg - 1 - s
    key_spec = pl.BlockSpec((SEG, LIN_KEY_DIM), lambda h, s: (rev(s), h))
    val_spec = pl.BlockSpec((SEG, LIN_VAL_DIM), lambda h, s: (rev(s), h))
    out_specs = [key_spec, key_spec, val_spec] + ([key_spec] if with_decay_grad else [])
    widths = [LIN_KEY_DIM, LIN_KEY_DIM, LIN_VAL_DIM] + ([LIN_KEY_DIM] if with_decay_grad else [])
    return _side_call(
        side, body,
        name=name,
        grid=(LIN_HEADS, nseg),
        in_specs=[_lin_spec(w, rev) for w in (q, k, v, la)] + [
            pl.BlockSpec((None, None, LIN_VAL_DIM, LIN_KEY_DIM), lambda h, s: (h, rev(s), 0, 0)), val_spec],
        out_specs=out_specs,
        out_shape=[jax.ShapeDtypeStruct((T, LIN_HEADS * w), F32) for w in widths],
        scratch_shapes=[pltpu.VMEM((LIN_VAL_DIM, LIN_KEY_DIM), F32)],
        args=[q.arr, k.arr, v.arr, la.arr, states, dout],
    )


def _coords():
    return lax.axis_index("x"), lax.axis_index("y"), lax.axis_index("c")


def _peer(k):
    x, y, c = _coords()
    px = 1 - x if k & 4 else x
    py = 1 - y if k & 2 else y
    pc = 1 - c if k & 1 else c
    return (px, py, pc), 4 * px + 2 * py + pc


def _remote(src, dst, send_sem, recv_sem, device):
    return pltpu.make_async_remote_copy(src_ref=src, dst_ref=dst, send_sem=send_sem, recv_sem=recv_sem,
                                        device_id=device, device_id_type=pl.DeviceIdType.MESH)


class Side:
    def __init__(self, kind, arrays):
        self.kind, self.arrays, self.n = kind, list(arrays), len(arrays)
        self.outs_per = 1
        self.slots = {"gather": N_DEV - 1, "pair": 4, "chips": 3}[kind]

    def out_shape(self):
        if self.kind == "gather":
            return [jax.ShapeDtypeStruct((N_DEV,) + a.shape, a.dtype) for a in self.arrays]
        return [jax.ShapeDtypeStruct((4,) + a.shape[1:], a.dtype) for a in self.arrays]

    def sems(self):
        return [pltpu.SemaphoreType.DMA((self.n, self.slots)), pltpu.SemaphoreType.DMA((self.n, self.slots)),
                pltpu.SemaphoreType.DMA((self.n,))]

    def start(self, ins, outs, sems):
        send, recv, own = sems
        x, y, c = _coords()
        me = 4 * x + 2 * y + c
        for a in range(self.n):
            if self.kind == "gather":
                pltpu.make_async_copy(ins[a], outs[a].at[me], own.at[a]).start()
                for k in (1, 2, 4, 6):
                    peer, _ = _peer(k)
                    _remote(ins[a], outs[a].at[me], send.at[a, k - 1], recv.at[a, k - 1], peer).start()
            elif self.kind == "pair":
                for j in range(4):
                    _remote(ins[a].at[2 * j + 1 - c], outs[a].at[j], send.at[a, j], recv.at[a, j], (x, y, 1 - c)).start()
            else:
                chip = 2 * x + y
                pltpu.make_async_copy(ins[a].at[chip], outs[a].at[chip], own.at[a]).start()
                for d in (1, 2, 3):
                    tx, ty = (1 - x if d & 2 else x), (1 - y if d & 1 else y)
                    _remote(ins[a].at[2 * tx + ty], outs[a].at[chip], send.at[a, d - 1], recv.at[a, d - 1], (tx, ty, c)).start()

    def finish(self, ins, outs, sems):
        send, recv, own = sems
        x, y, c = _coords()
        me = 4 * x + 2 * y + c
        sibling = (x, y, 1 - c)
        for a in range(self.n):
            if self.kind == "gather":
                blk = outs[a].at[me]
                for k in (2, 4, 6):
                    _, pidx = _peer(k)
                    _remote(blk, blk, send.at[a, k - 1], recv.at[a, k - 1], sibling).wait_recv()
                    _remote(outs[a].at[pidx], outs[a].at[pidx], send.at[a, k], recv.at[a, k], sibling).start()
                pltpu.make_async_copy(ins[a], blk, own.at[a]).wait()
                for k in range(1, N_DEV):
                    _remote(blk, blk, send.at[a, k - 1], recv.at[a, k - 1], sibling).wait_send()
                for k in (1, 3, 5, 7):
                    _remote(blk, blk, send.at[a, k - 1], recv.at[a, k - 1], sibling).wait_recv()
            elif self.kind == "pair":
                for j in range(4):
                    _remote(ins[a].at[j], outs[a].at[j], send.at[a, j], recv.at[a, j], sibling).wait()
            else:
                slab = outs[a].at[0]
                pltpu.make_async_copy(slab, slab, own.at[a]).wait()
                for d in (1, 2, 3):
                    _remote(slab, slab, send.at[a, d - 1], recv.at[a, d - 1], sibling).wait()


def _side_call(side, body, *, name, grid, in_specs, out_specs, out_shape, scratch_shapes, args, semantics=None,
               aliases=None):
    in_specs, out_specs, out_shape = list(in_specs), list(out_specs), list(out_shape)
    scratch_shapes, args = list(scratch_shapes), list(args)
    n_in, n_out = len(in_specs), len(out_specs)
    kernel_body = body
    if side is not None:
        n_so = side.n * side.outs_per

        def kernel_body(*refs):
            ins, side_ins = refs[:n_in], refs[n_in:n_in + side.n]
            pos = n_in + side.n
            outs, side_outs = refs[pos:pos + n_out], refs[pos + n_out:pos + n_out + n_so]
            rest = refs[pos + n_out + n_so:]
            scratch, sems = rest[:-3], rest[-3:]
            ids = [pl.program_id(ax) for ax in range(len(grid))]
            first = functools.reduce(jnp.logical_and, [i == 0 for i in ids])
            last = functools.reduce(jnp.logical_and, [i == g - 1 for i, g in zip(ids, grid)])

            @pl.when(first)
            def _():
                side.start(side_ins, side_outs, sems)

            body(*ins, *outs, *scratch)

            @pl.when(last)
            def _():
                side.finish(side_ins, side_outs, sems)

        any_spec = pl.BlockSpec(memory_space=pl.ANY)
        in_specs += [any_spec] * side.n
        args += side.arrays
        out_specs += [any_spec] * n_so
        out_shape += side.out_shape()
        scratch_shapes += side.sems()
        semantics = ("arbitrary",) * len(grid)
    outs = pl.pallas_call(
        kernel_body,
        name=name,
        grid=grid,
        in_specs=in_specs,
        out_specs=out_specs,
        out_shape=out_shape,
        scratch_shapes=scratch_shapes,
        input_output_aliases=aliases or {},
        compiler_params=pltpu.CompilerParams(dimension_semantics=semantics or ("arbitrary",) * len(grid),
                                             vmem_limit_bytes=VMEM_BIG),
    )(*args)
    if side is not None:
        side.results = list(outs[n_out:])
    return list(outs[:n_out])


def carrier(name, side):
    _side_call(side, lambda: None, name=name, grid=(1,), in_specs=[], out_specs=[], out_shape=[], scratch_shapes=[], args=[])
    return side.results


def all_reduce_small(vec):
    def body(v_ref, o_ref, land, send_sems, recv_sems):
        x, y, c = _coords()
        me = 4 * x + 2 * y + c
        land[me] = v_ref[...]
        copies = []
        for k in range(1, N_DEV):
            peer, _ = _peer(k)
            cp = pltpu.make_async_remote_copy(
                src_ref=v_ref, dst_ref=land.at[me], send_sem=send_sems.at[k - 1], recv_sem=recv_sems.at[k - 1],
                device_id=peer, device_id_type=pl.DeviceIdType.MESH)
            cp.start()
            copies.append(cp)
        for cp in copies:
            cp.wait()
        total = land[0]
        for s in range(1, N_DEV):
            total = total + land[s]
        o_ref[...] = total

    return pl.pallas_call(
        body,
        name="all_reduce_small",
        in_specs=[pl.BlockSpec(memory_space=pltpu.VMEM)],
        out_specs=pl.BlockSpec(memory_space=pltpu.VMEM),
        out_shape=jax.ShapeDtypeStruct(vec.shape, F32),
        scratch_shapes=[pltpu.VMEM((N_DEV,) + vec.shape, F32), pltpu.SemaphoreType.DMA((N_DEV - 1,)),
                        pltpu.SemaphoreType.DMA((N_DEV - 1,))],
    )(vec)


def _pack(parts):
    flat = jnp.concatenate([p.reshape(-1).astype(F32) for p in parts])
    n = flat.shape[0]
    rows = -(-n // 1024) * 8
    return jnp.pad(flat, (0, rows * 128 - n)).reshape(rows, 128)


def _unpack(packed, shapes):
    flat, out, pos = packed.reshape(-1), [], 0
    for s in shapes:
        n = math.prod(s)
        out.append(flat[pos:pos + n].reshape(s))
        pos += n
    return out


def _row2(v):
    return v.reshape(1, -1)


def _rider(hooks, name, ctx=None):
    if name not in hooks:
        return None, lambda: None
    side, done = hooks[name](ctx)
    return side, lambda: done(side.results)


def _layer_forward(h, hb, W, consts, hooks):
    T = h.shape[0]
    cos, sin, ret_decay = consts
    side, landed = _rider(hooks, "proj")
    proj = mm("proj", hb, W["in"], "nn", tm=4096, tn=512, tk=1024, side=side)
    landed()
    bias_tab = bias_table(W["rel_bias"])
    side, landed = _rider(hooks, "attn_fwd")
    attn = attn_fwd(proj, bias_tab, side=side)
    landed()
    rq2, rk2 = row_call("ret_pre", f_ret_pre, [Row(proj, 512, COL["rq"] // 512), Row(proj, 512, COL["rk"] // 512),
                                              Row(cos), Row(sin)], [], [(512, F32), (512, F32)], tm=512)
    side, landed = _rider(hooks, "ret_fwd")
    ret_raw, ret_st = lin_fwd("ret_fwd", Win(rq2, 128), Win(rk2, 128), Win(proj, 256, COL["rv"] // 256), Win(ret_decay, 128),
                              side=side)
    landed()
    gq2, gla_la = row_call("gla_pre", f_gla_pre, [Row(proj, 512, COL["gq"] // 512), Row(proj, GLR_BLOCK, GLR_OFF // GLR_BLOCK)],
                           [W["wlr"], W["blr"]], [(512, F32), (512, F32)], tm=512)
    side, landed = _rider(hooks, "gla_fwd")
    gla_raw, gla_st = lin_fwd("gla_fwd", Win(gq2, 128), Win(proj, 128, COL["gk"] // 128), Win(proj, 256, COL["gv"] // 256),
                              Win(gla_la, 128), side=side)
    landed()
    ret_b, gla_b = row_call("normgate", f_normgate, [Row(ret_raw), Row(proj, 1024, COL["rg"] // 1024), Row(gla_raw),
                                                    Row(proj, 1024, COL["gg"] // 1024)], [W["gnorm"]],
                            [(1024, MXU_DTYPE), (1024, MXU_DTYPE)], tm=256)
    branches = (attn, ret_b, gla_b)
    p = [mm("branch_proj", branches[n], W["branch"][n], "nn", tm=1024, tn=1024, tk=1024) for n in range(3)]
    merged, = row_call("merge", f_merge, [Row(p[0]), Row(p[1]), Row(p[2]), Row(proj, GATES_W, GATES_OFF // GATES_W)], [],
                       [(1024, MXU_DTYPE)], tm=256)
    mix = mm("out_proj", merged, W["out"], "nn", tm=1024, tn=1024, tk=1024)
    x1, x1b = row_call("ln1", _twice(f_deepnorm), [Row(h), Row(mix)], [W["ln1_g"], W["ln1_b"]], STREAM, tm=512)
    side, landed = _rider(hooks, "mlp_up")
    u, act = mm("mlp_up", x1b, W["up"], "nn", tm=2048, tn=512, tk=1024, n_out=2, out_dtypes=(F32, MXU_DTYPE),
                epi=lambda r: (r, jnp.square(jnp.maximum(r, 0.0))), side=side)
    landed()
    y = mm("mlp_down", act, W["down"], "nn", tm=1024, tn=1024, tk=1024)
    x2, x2b = row_call("ln2", _twice(f_deepnorm), [Row(x1), Row(y)], [W["ln2_g"], W["ln2_b"]], STREAM, tm=512)
    saved = dict(h=h, hb=hb, proj=proj, bias_tab=bias_tab, rq2=rq2, rk2=rk2, ret_st=ret_st, gq2=gq2, gla_la=gla_la,
                 gla_st=gla_st, ret_raw=ret_raw, gla_raw=gla_raw, branches=branches, p=p, merged=merged, mix=mix, x1=x1, x1b=x1b,
                 u=u, act=act, y=y)
    return x2, x2b, saved


def _layer_backward(dx2, S, W, consts, hooks):
    cos, sin, ret_decay = consts
    h, proj, T = S["h"], S["proj"], S["h"].shape[0]
    add = lambda r, e: r + e
    dx1a, dy, dg2, db2 = row_call("ln2_bwd", vjp_fn(f_deepnorm, 2, 1), [Row(S["x1"]), Row(S["y"]), Row(dx2)],
                                  [W["ln2_g"], W["ln2_b"]], [(1024, F32), (1024, MXU_DTYPE)], [(1, 1024), (1, 1024)], tm=256)
    du = mm("mlp_down_bwd", dy, W["down"], "nt", tm=1024, tn=512, tk=1024, extras=(S["u"],),
            epi=lambda r, u: r * (2.0 * jnp.maximum(u, 0.0)), out_dtype=MXU_DTYPE)
    g_down = mm("mlp_down_wgrad", S["act"], dy, "tn", tm=1024, tn=1024, tk=512, out_dtype=WIRE_DTYPE)
    dx1 = mm("mlp_up_bwd", du, W["up"], "nt", tm=1024, tn=1024, tk=1024, extras=(dx1a,), epi=add)
    g_up = mm("mlp_up_wgrad", S["x1b"], du, "tn", tm=1024, tn=D_FF // N_DEV, tk=512, out_dtype=WIRE_DTYPE, column_slabs=True)
    dha, dmix, dg1, db1 = row_call("ln1_bwd", vjp_fn(f_deepnorm, 2, 1), [Row(h), Row(S["mix"]), Row(dx1)],
                                   [W["ln1_g"], W["ln1_b"]], [(1024, F32), (1024, MXU_DTYPE)], [(1, 1024), (1, 1024)], tm=256)
    dmerged = mm("out_proj_bwd", dmix, W["out"], "nt", tm=1024, tn=1024, tk=1024)
    g_out = mm("out_proj_wgrad", S["merged"], dmix, "tn", tm=1024, tn=1024, tk=512, out_dtype=WIRE_DTYPE)
    dproj = jax.ShapeDtypeStruct((T, PROJ_PAD), MXU_DTYPE)
    p = S["p"]
    dp0, dp1, dp2, dproj = row_call(
        "merge_bwd", vjp_fn(f_merge, 4, 1), [Row(p[0]), Row(p[1]), Row(p[2]), Row(proj, GATES_W, GATES_OFF // GATES_W), Row(dmerged)],
        [], [(1024, MXU_DTYPE)] * 3, tm=256, into=dproj, into_cols=[(GATES_W, GATES_OFF)])
    dps = (dp0, dp1, dp2)
    dbr = [mm("branch_proj_bwd", dps[n], W["branch"][n], "nt", tm=1024, tn=1024, tk=1024,
              out_dtype=MXU_DTYPE if n == 0 else F32) for n in range(3)]
    g_branch = [mm("branch_proj_wgrad", S["branches"][n], dps[n], "tn", tm=1024, tn=1024, tk=512, out_dtype=WIRE_DTYPE)
                for n in range(3)]
    dret_raw, dgla_raw, dgnorm, dproj = row_call(
        "normgate_bwd", vjp_fn(lambda a, b, c, d, g: _reorder_normgate(a, b, c, d, g), 4, 2),
        [Row(S["ret_raw"]), Row(S["gla_raw"]), Row(proj, 1024, COL["rg"] // 1024), Row(proj, 1024, COL["gg"] // 1024),
         Row(dbr[1]), Row(dbr[2])], [W["gnorm"]], [(1024, F32), (1024, F32)], [(1, LIN_VAL_DIM)], tm=256,
        into=dproj, into_cols=[(1024, COL["rg"]), (1024, COL["gg"])])
    side, landed = _rider(hooks, "gla_bwd")
    dgq2, dgk, dgv, dgla_la = lin_bwd("gla_bwd", Win(S["gq2"], 128), Win(proj, 128, COL["gk"] // 128),
                                      Win(proj, 256, COL["gv"] // 256), Win(S["gla_la"], 128), S["gla_st"], dgla_raw, True,
                                      side=side)
    landed()
    dwlr, dblr, dproj = row_call(
        "gla_pre_bwd", _gla_pre_bwd, [Row(proj, 512, COL["gq"] // 512), Row(proj, GLR_BLOCK, GLR_OFF // GLR_BLOCK),
                                      Row(dgq2), Row(dgla_la), Row(dgk), Row(dgv)], [W["wlr"], W["blr"]], [],
        [(GLR_BLOCK, 512), (1, 512)], tm=256, into=dproj,
        into_cols=[(512, COL["gq"]), (PROJ_PAD - GLR_OFF, GLR_OFF), (512, COL["gk"]), (1024, COL["gv"])])
    drq2, drk2, drv = lin_bwd("ret_bwd", Win(S["rq2"], 128), Win(S["rk2"], 128), Win(proj, 256, COL["rv"] // 256),
                              Win(ret_decay, 128), S["ret_st"], dret_raw, False)
    dproj, = row_call(
        "ret_pre_bwd", _ret_pre_bwd, [Row(proj, 512, COL["rq"] // 512), Row(proj, 512, COL["rk"] // 512), Row(cos), Row(sin),
                                      Row(drq2), Row(drk2), Row(drv)], [], [], tm=256, into=dproj,
        into_cols=[(512, COL["rq"]), (512, COL["rk"]), (1024, COL["rv"])])
    side, landed = _rider(hooks, "attn_bwd")
    dproj, dbias = attn_bwd(proj, S["bias_tab"], dbr[0], dproj, side=side)
    landed()
    d_rel_bias = bias_table_grad(dbias)
    side, landed = _rider(hooks, "proj_wgrad", dict(branch=g_branch, out=g_out, up=g_up, down=g_down))
    g_in = mm("proj_wgrad", S["hb"], dproj, "tn", tm=1024, tn=1280, tk=512, out_dtype=WIRE_DTYPE, side=side)
    landed()
    side, landed = _rider(hooks, "proj_bwd", {"in": g_in})
    dh = mm("proj_bwd", dproj, W["in"], "nt", tm=1024, tn=1024, tk=1280, extras=(dha,), epi=add, side=side)
    landed()
    big = dict(g_in=g_in, branch=g_branch, out=g_out, up=g_up, down=g_down)
    small = dict(rel_bias=d_rel_bias, wlr=dwlr[:GLA_RANK], blr=dblr, gnorm=dgnorm, ln1_g=dg1, ln1_b=db1, ln2_g=dg2, ln2_b=db2)
    return dh, big, small


def _reorder_normgate(ret_raw, gla_raw, rg, gg, gnorm):
    return f_normgate(ret_raw, rg, gla_raw, gg, gnorm)


def _gla_pre_bwd(gq, glr, dq2, dla, dgk, dgv, wlr, blr):
    _, pull = jax.vjp(f_gla_pre, gq, glr, wlr, blr)
    dgq, dglr, dwlr, dblr = pull((dq2, dla))
    dglr = jnp.concatenate([dglr, jnp.zeros((dglr.shape[0], PROJ_PAD - GLR_OFF - GLR_BLOCK), F32)], axis=1)
    return dgq, dglr, dgk, dgv, dwlr, dblr


def _ret_pre_bwd(rq, rk, cos, sin, drq2, drk2, drv):
    _, pull = jax.vjp(lambda a, b: f_ret_pre(a, b, cos, sin), rq, rk)
    drq, drk = pull((drq2, drk2))
    return drq, drk, drv


SHARDED = ("in", "branch", "out", "up", "down")

_FULL = {
    "branch": lambda g: [g[:, n].reshape(D_MODEL, D_MODEL) for n in range(3)],
    "out": lambda g: g.reshape(D_MODEL, D_MODEL),
    "up": lambda g: jnp.transpose(g, (1, 0, 2)).reshape(D_MODEL, D_FF),
    "down": lambda g: g.reshape(D_FF, D_MODEL),
}


def _owner_pieces(s):
    shard = PROJ_WIDTH // N_DEV
    lo, hi = s * shard, (s + 1) * shard
    pieces = []
    for a, b, padded in ((0, GLR_SRC, 0), (GLR_SRC, GATES_SRC, GLR_OFF), (GATES_SRC, PROJ_WIDTH, GATES_OFF)):
        x0, x1 = max(lo, a), min(hi, b)
        if x0 < x1:
            pieces.append((x0 - lo, x1 - x0, padded + x0 - a))
    return pieces


def _full_in(g):
    placed = sorted((p, s, c, w) for s in range(N_DEV) for c, w, p in _owner_pieces(s))
    cols = [g[s][:, c:c + w] for _, s, c, w in placed]
    return jnp.concatenate(cols + [jnp.zeros((D_MODEL, PROJ_PAD - PROJ_WIDTH), g.dtype)], axis=1)


def _full_wlr(g):
    wlr = jnp.transpose(g, (1, 0, 2)).reshape(GLA_RANK, LIN_HEADS * LIN_KEY_DIM)
    return jnp.pad(wlr, ((0, GLR_BLOCK - GLA_RANK), (0, 0)))


def _slab_in(g):
    return jnp.stack([jnp.concatenate([g[:, p:p + w] for _, w, p in _owner_pieces(s)], axis=1) for s in range(N_DEV)])


_SLABS = {
    "in": _slab_in,
    "branch": lambda gs: jnp.stack([b.reshape(N_DEV, D_MODEL // N_DEV, D_MODEL) for b in gs], axis=1),
    "out": lambda g: g.reshape(N_DEV, D_MODEL // N_DEV, D_MODEL),
    "up": lambda g: g,
    "down": lambda g: g.reshape(N_DEV, D_FF // N_DEV, D_MODEL),
}


def _pair_sums(name, slabs):
    sums = []
    for mine, sibs in zip(slabs, carrier(name, Side("pair", slabs))):
        width = mine.shape[-1]
        rows = math.prod(mine.shape[1:-1])
        tm = 256 if rows % 256 == 0 else 128
        nb = rows // tm
        my_rows = Row(mine.reshape(-1, width), rowmap=lambda i, nb=nb: (2 * (i // nb) + lax.axis_index("c")) * nb + i % nb)
        out, = row_call("pair_sum", lambda q, p: (p.astype(F32) + q.astype(F32),),
                        [Row(sibs.reshape(-1, width)), my_rows], [], [(width, WIRE_DTYPE)], tm=tm)
        sums.append(out.reshape(sibs.shape))
    return sums


def _sum_adam(name, parts, w, m, v, layer, tm, earlier):
    C = w.shape[-1]
    rows = [Row(parts, lead=(s,)) for s in range(parts.shape[0])] + [Row(a, lead=(layer,)) for a in (w, m, v)]
    return row_call(name, f_sum_adam, rows, [], [(C, F32)] * 4, tm=tm, stack=(layer, DEPTH), stacked_onto=earlier)


def kernel(x, ln_in_g, ln_in_b, w_in, rel_bias, gla_w_lr, gla_b_lr, gla_norm_g, w_branch, w_out, ln1_g, ln1_b, w_up, w_down, ln2_g, ln2_b, loss_target, m_ln_in_g, m_ln_in_b, m_w_in, m_rel_bias, m_gla_w_lr, m_gla_b_lr, m_gla_norm_g, m_w_branch, m_w_out, m_ln1_g, m_ln1_b, m_w_up, m_w_down, m_ln2_g, m_ln2_b, v_ln_in_g, v_ln_in_b, v_w_in, v_rel_bias, v_gla_w_lr, v_gla_b_lr, v_gla_norm_g, v_w_branch, v_w_out, v_ln1_g, v_ln1_b, v_w_up, v_w_down, v_ln2_g, v_ln2_b):
    T = x.shape[1]
    x2d = x.reshape(T, D_MODEL)
    tgt = loss_target.reshape(T, D_MODEL)
    me = 4 * lax.axis_index("x") + 2 * lax.axis_index("y") + lax.axis_index("c")

    inv = ROPE_BASE ** (-jnp.arange(0, LIN_KEY_DIM, 2, dtype=F32) / LIN_KEY_DIM)
    ang = jnp.arange(T, dtype=F32)[:, None] * inv[None, :]
    cos = jnp.concatenate([jnp.cos(ang), jnp.cos(ang)], axis=1)
    sin = jnp.concatenate([-jnp.sin(ang), jnp.sin(ang)], axis=1)
    log_gamma = jnp.log1p(-jnp.exp2(-5.0 - jnp.arange(LIN_HEADS, dtype=F32)))
    ret_decay = jnp.repeat(log_gamma, LIN_KEY_DIM).reshape(1, LIN_HEADS * LIN_KEY_DIM)
    consts = (cos, sin, ret_decay)

    layers = [dict(rel_bias=rel_bias[l], blr=_row2(gla_b_lr[l]), gnorm=_row2(gla_norm_g[l]), ln1_g=_row2(ln1_g[l]),
                   ln1_b=_row2(ln1_b[l]), ln2_g=_row2(ln2_g[l]), ln2_b=_row2(ln2_b[l])) for l in range(DEPTH)]
    shards = dict(branch=w_branch, out=w_out, up=w_up, down=w_down)
    shards["in"] = w_in
    wire = [{n: shards[n][l].astype(WIRE_DTYPE) for n in SHARDED} for l in range(DEPTH)]

    def gather_of(l, names):
        def landed(res):
            for n, g in zip(names, res):
                layers[l][n] = _full_in(g) if n == "in" else _FULL[n](g)
        return lambda ctx: (Side("gather", [wire[l][n] for n in names]), landed)

    first = carrier("gather_first", Side("gather", [wire[0]["in"]] + [gla_w_lr[l] for l in range(DEPTH)]))
    layers[0]["in"] = _full_in(first[0])
    for l in range(DEPTH):
        layers[l]["wlr"] = _full_wlr(first[1 + l])

    fwd_hooks = [{"proj": gather_of(1, ("in",)), "attn_fwd": gather_of(0, ("up", "down")),
                  "ret_fwd": gather_of(0, ("branch", "out"))},
                 {"proj": gather_of(1, ("up", "down")), "attn_fwd": gather_of(1, ("branch", "out"))}]
    h, hb = row_call("ln_in", _twice(f_ln_in), [Row(x2d)], [_row2(ln_in_g), _row2(ln_in_b)], STREAM, tm=512)
    saved = []
    for l in range(DEPTH):
        h, hb, s = _layer_forward(h, hb, layers[l], consts, fwd_hooks[l])
        saved.append(s)
    dh, loss_lanes = row_call("loss", f_loss, [Row(h), Row(tgt)], [], [(D_MODEL, F32)], [(1, 128)], tm=512)

    received = [dict() for _ in range(DEPTH)]

    def scatter_of(l, names, sums=None):
        def make(ctx):
            parts = sums if sums is not None else _pair_sums("pair_grads", [_SLABS[n](ctx[n]) for n in names])

            def landed(res):
                received[l].update(zip(names, res))
            return Side("chips", parts), landed
        return make

    small = [None] * DEPTH
    dh, big, small[1] = _layer_backward(dh, saved[1], layers[1], consts, {})
    sums1 = _pair_sums("pair_grads", [_SLABS[n](big["g_in"] if n == "in" else big[n]) for n in SHARDED])
    rest = SHARDED[1:]
    dh, big, small[0] = _layer_backward(dh, saved[0], layers[0], consts, {
        "gla_bwd": scatter_of(1, ("in",), sums1[:1]), "attn_bwd": scatter_of(1, rest, sums1[1:]),
        "proj_wgrad": scatter_of(0, rest), "proj_bwd": scatter_of(0, ("in",))})
    grad_x, d_in_g, d_in_b = row_call("ln_in_bwd", vjp_fn(f_ln_in, 1, 1), [Row(x2d), Row(dh)],
                                      [_row2(ln_in_g), _row2(ln_in_b)], [(D_MODEL, F32)], [(1, D_MODEL), (1, D_MODEL)], tm=256)

    small_names = ("rel_bias", "wlr", "blr", "gnorm", "ln1_g", "ln1_b", "ln2_g", "ln2_b")
    parts = [d_in_g, d_in_b] + [small[l][n] for l in range(DEPTH) for n in small_names] + [loss_lanes]
    shapes = [(D_MODEL,), (D_MODEL,)] + [s for l in range(DEPTH) for s in (
        (ATTN_HEADS, N_REL), (GLA_RANK, 512), (512,), (LIN_VAL_DIM,), (D_MODEL,), (D_MODEL,), (D_MODEL,), (D_MODEL,))] + [(128,)]
    red = _unpack(all_reduce_small(_pack(parts)), shapes)
    loss = jnp.sum(red[-1])
    g_small = {"ln_in_g": red[0], "ln_in_b": red[1]}
    for l in range(DEPTH):
        for j, n in enumerate(small_names):
            g_small[(l, n)] = red[2 + l * len(small_names) + j]
    shard = LIN_HEADS * LIN_KEY_DIM // N_DEV
    g_wlr = jnp.stack([lax.dynamic_slice_in_dim(g_small[(l, "wlr")], me * shard, shard, axis=1) for l in range(DEPTH)])
    stack = lambda n: jnp.stack([g_small[(l, n)] for l in range(DEPTH)])
    g_rep = [g_small["ln_in_g"], g_small["ln_in_b"], stack("rel_bias"), g_wlr, stack("blr"), stack("gnorm"),
             stack("ln1_g"), stack("ln1_b"), stack("ln2_g"), stack("ln2_b")]
    w_rep = [ln_in_g, ln_in_b, rel_bias, gla_w_lr, gla_b_lr, gla_norm_g, ln1_g, ln1_b, ln2_g, ln2_b]
    m_rep = [m_ln_in_g, m_ln_in_b, m_rel_bias, m_gla_w_lr, m_gla_b_lr, m_gla_norm_g, m_ln1_g, m_ln1_b, m_ln2_g, m_ln2_b]
    v_rep = [v_ln_in_g, v_ln_in_b, v_rel_bias, v_gla_w_lr, v_gla_b_lr, v_gla_norm_g, v_ln1_g, v_ln1_b, v_ln2_g, v_ln2_b]
    rep_shapes = [w.shape for w in w_rep]
    d_pack, m_pack, v_pack = row_call("adam_small", f_adam, [Row(_pack(g_rep)), Row(_pack(w_rep)), Row(_pack(m_rep)), Row(_pack(v_rep))],
                                      [], [(128, F32)] * 3, tm=256)
    d_rep, nm_rep, nv_rep = _unpack(d_pack, rep_shapes), _unpack(m_pack, rep_shapes), _unpack(v_pack, rep_shapes)

    def owned(key, w, m, v, name, tm):
        C = w.shape[-1]
        flat = lambda a: a.reshape(DEPTH, -1, C)
        outs = None
        for l in range(DEPTH):
            outs = _sum_adam(name, received[l][key].reshape(4, -1, C), flat(w), flat(m), flat(v), l, tm, outs)
        return [o.reshape(w.shape) for o in outs]

    o_in = owned("in", w_in, m_w_in, v_w_in, "adam_w_in", 256)
    o_br = owned("branch", w_branch, m_w_branch, v_w_branch, "adam_w_branch", 128)
    o_out = owned("out", w_out, m_w_out, v_w_out, "adam_w_out", 128)
    o_up = owned("up", w_up, m_w_up, v_w_up, "adam_w_up", 256)
    o_down = owned("down", w_down, m_w_down, v_w_down, "adam_w_down", 256)

    def tree(j, rep):
        return [rep[0], rep[1], o_in[j], rep[2], rep[3], rep[4], rep[5], o_br[j], o_out[j], rep[6], rep[7], o_up[j], o_down[j],
                rep[8], rep[9]]

    return (loss, grad_x.reshape(1, T, D_MODEL), *tree(0, g_rep), *tree(1, d_rep), *tree(2, nm_rep), *tree(3, nv_rep))
```

```python
import functools
import math

import jax
import jax.numpy as jnp
from jax import lax
from jax.experimental import pallas as pl
from jax.experimental.pallas import tpu as pltpu

F32 = jnp.float32
MXU_DTYPE = jnp.bfloat16
WIRE_DTYPE = jnp.bfloat16

N_DEV = 8
D_MODEL = 1024
DEPTH = 2
CHUNK = 64
ATTN_HEADS = 8
ATTN_HEAD_DIM = 128
ATTN_LEFT_CHUNKS = 8
BAND = (ATTN_LEFT_CHUNKS + 1) * CHUNK
PAD_ROWS = ATTN_LEFT_CHUNKS * CHUNK
REL_CLIP = 2 * CHUNK
N_REL = 2 * REL_CLIP + 1
REL_MIN = REL_CLIP - (CHUNK - 1)
LIN_HEADS = 4
LIN_KEY_DIM = 128
LIN_VAL_DIM = 256
GLA_RANK = 16
GLA_GATE_NORMALIZER = 16.0
ROPE_BASE = 10000.0
D_FF = 4 * D_MODEL
LN_EPS = 1e-5
NEG_INF = -1e30
ALPHA = (2 * DEPTH) ** 0.25
ADAM_LR, ADAM_B1, ADAM_B2, ADAM_EPS, ADAM_WD, ADAM_STEP = 0.001, 0.9, 0.999, 1e-08, 0.01, 10

PROJ_WIDTH = 12304
GLR_SRC = 9216
GATES_SRC = GLR_SRC + GLA_RANK
MAIN_W = 9216
GATES_W = 3 * D_MODEL
GATES_OFF = MAIN_W
GLR_OFF = MAIN_W + GATES_W
PROJ_PAD = 12800
GLR_BLOCK = 128
COL = dict(aq=0, ak=1024, av=2048, rq=3072, rk=3584, rv=4096, rg=5120, gq=6144, gk=6656, gv=7168, gg=8192)

SEG_CHUNKS = 8
SEG = SEG_CHUNKS * CHUNK
VMEM_BIG = 56 * 1024 * 1024

_NN = (((1,), (0,)), ((), ()))
_NT = (((1,), (1,)), ((), ()))
_TN = (((0,), (0,)), ((), ()))


def _mxu(a, b, dn):
    return lax.dot_general(a.astype(MXU_DTYPE), b.astype(MXU_DTYPE), dn, preferred_element_type=F32)


@jax.custom_vjp
def dot_nn(a, b):
    return _mxu(a, b, _NN)


def _dot_nn_fwd(a, b):
    return _mxu(a, b, _NN), (a, b)


def _dot_nn_bwd(res, g):
    a, b = res
    return _mxu(g, b, _NT).astype(a.dtype), _mxu(a, g, _TN).astype(b.dtype)


dot_nn.defvjp(_dot_nn_fwd, _dot_nn_bwd)


@jax.custom_vjp
def dot_nt(a, b):
    return _mxu(a, b, _NT)


def _dot_nt_fwd(a, b):
    return _mxu(a, b, _NT), (a, b)


def _dot_nt_bwd(res, g):
    a, b = res
    return _mxu(g, b, _NN).astype(a.dtype), _mxu(g, a, _TN).astype(b.dtype)


dot_nt.defvjp(_dot_nt_fwd, _dot_nt_bwd)


@jax.custom_vjp
def dot_tn(a, b):
    return _mxu(a, b, _TN)


def _dot_tn_fwd(a, b):
    return _mxu(a, b, _TN), (a, b)


def _dot_tn_bwd(res, g):
    a, b = res
    return _mxu(b, g, _NT).astype(a.dtype), _mxu(a, g, _NN).astype(b.dtype)


dot_tn.defvjp(_dot_tn_fwd, _dot_tn_bwd)


def _tri(n, lower):
    r = lax.broadcasted_iota(jnp.int32, (n, n), 0)
    c = lax.broadcasted_iota(jnp.int32, (n, n), 1)
    return (r >= c) if lower else (r <= c)


def _f32_dot(a, b):
    return lax.dot_general(a, b, _NN, precision=lax.Precision.HIGHEST, preferred_element_type=F32)


@jax.custom_vjp
def chunk_cumsum(x):
    return _f32_dot(_tri(x.shape[0], True).astype(F32), x)


def _chunk_cumsum_fwd(x):
    return chunk_cumsum(x), None


def _chunk_cumsum_bwd(_, g):
    return (_f32_dot(_tri(g.shape[0], False).astype(F32), g),)


chunk_cumsum.defvjp(_chunk_cumsum_fwd, _chunk_cumsum_bwd)


@jax.custom_vjp
def swap_halves(t):
    return pltpu.roll(t, t.shape[-1] // 2, t.ndim - 1)


def _swap_fwd(t):
    return swap_halves(t), None


def _swap_bwd(_, g):
    return (swap_halves(g),)


swap_halves.defvjp(_swap_fwd, _swap_bwd)


def mm(name, a, b, dims, *, tm, tn, tk, out_dtype=F32, extras=(), epi=None, n_out=1, out_dtypes=None, side=None,
       column_slabs=False):
    if dims == "nn":
        (M, K), N = a.shape, b.shape[1]
    elif dims == "nt":
        (M, K), N = a.shape, b.shape[0]
    else:
        (K, M), N = a.shape, b.shape[1]
    tm, tn, tk = min(tm, M), min(tn, N), min(tk, K)
    assert M % tm == 0 and N % tn == 0 and K % tk == 0, (name, M, N, K, tm, tn, tk)
    nk = K // tk
    dn = {"nn": _NN, "nt": _NT, "tn": _TN}[dims]
    a_spec = pl.BlockSpec((tk, tm), lambda i, j, k: (k, i)) if dims == "tn" else pl.BlockSpec((tm, tk), lambda i, j, k: (i, k))
    b_spec = pl.BlockSpec((tn, tk), lambda i, j, k: (j, k)) if dims == "nt" else pl.BlockSpec((tk, tn), lambda i, j, k: (k, j))
    o_spec = pl.BlockSpec((tm, tn), lambda i, j, k: (i, j))
    n_ex = len(extras)
    out_dtypes = out_dtypes or (out_dtype,) * n_out

    def finish(r, ex_refs, o_refs):
        if epi is not None:
            r = epi(r, *[e[...] for e in ex_refs])
        r = r if isinstance(r, tuple) else (r,)
        for o, v in zip(o_refs, r):
            o[...] = v.astype(o.dtype)

    def body(*refs):
        a_ref, b_ref = refs[:2]
        ex_refs = refs[2:2 + n_ex]
        o_refs = refs[2 + n_ex:2 + n_ex + n_out]
        part = _mxu(a_ref[...], b_ref[...], dn)
        if nk == 1:
            finish(part, ex_refs, o_refs)
            return
        acc = refs[-1]
        k = pl.program_id(2)

        @pl.when(k == 0)
        def _():
            acc[...] = part

        @pl.when(k > 0)
        def _():
            acc[...] += part

        @pl.when(k == nk - 1)
        def _():
            finish(acc[...], ex_refs, o_refs)

    outs = _side_call(
        side, body,
        name=name,
        grid=(M // tm, N // tn, nk),
        in_specs=[a_spec, b_spec] + [o_spec] * n_ex,
        out_specs=[pl.BlockSpec((None, tm, tn), lambda i, j, k: (j, i, 0))] * n_out if column_slabs else [o_spec] * n_out,
        out_shape=[jax.ShapeDtypeStruct((N // tn, M, tn) if column_slabs else (M, N), dt) for dt in out_dtypes],
        scratch_shapes=[pltpu.VMEM((tm, tn), F32)] if nk > 1 else [],
        args=[a, b, *extras],
        semantics=("parallel", "parallel", "arbitrary"),
    )
    return outs[0] if n_out == 1 else outs


class Row:
    def __init__(self, arr, width=None, col=0, lead=(), rowmap=None):
        self.arr, self.lead, self.col = arr, tuple(lead), col
        self.width = arr.shape[-1] if width is None else width
        self.rowmap = rowmap

    def spec(self, tm):
        lead, col, rowmap = self.lead, self.col, self.rowmap or (lambda i: i)
        return pl.BlockSpec((None,) * len(lead) + (tm, self.width), lambda i: (*lead, rowmap(i), col))


def row_call(name, fn, rows, params, new_outs, red_shapes=(), *, tm, into=None, into_cols=(), stack=None, stacked_onto=None):
    R = rows[0].arr.shape[len(rows[0].lead)]
    tm = min(tm, R)
    assert R % tm == 0, (name, R, tm)
    n_row, n_par, n_new, n_into, n_red = len(rows), len(params), len(new_outs), len(into_cols), len(red_shapes)
    fresh = isinstance(into, jax.ShapeDtypeStruct)

    def body(*refs):
        row_refs = refs[:n_row]
        par_refs = refs[n_row:n_row + n_par]
        pos = n_row + n_par + (1 if (into is not None and not fresh) else 0) + (n_new if stacked_onto is not None else 0)
        new_refs = refs[pos:pos + n_new]
        red_refs = refs[pos + n_new:pos + n_new + n_red]
        pos2 = pos + n_new + n_red
        into_ref = refs[pos2] if into is not None else None
        stage = refs[pos2 + 1:] if into is not None else ()
        i = pl.program_id(0)
        vals = fn(*[r[...] for r in row_refs], *[p[...] for p in par_refs])
        vals = vals if isinstance(vals, (tuple, list)) else (vals,)
        for o, v in zip(new_refs, vals[:n_new]):
            o[...] = v.astype(o.dtype)
        for (width, off), buf, v in zip(into_cols, stage, vals[n_new:n_new + n_into]):
            buf[...] = v.astype(buf.dtype)
            pltpu.sync_copy(buf, into_ref.at[pl.ds(pl.multiple_of(i * tm, tm), tm), pl.ds(off, width)])
        for o, v in zip(red_refs, vals[n_new + n_into:]):
            @pl.when(i == 0)
            def _(o=o, v=v):
                o[...] = v.astype(o.dtype)

            @pl.when(i > 0)
            def _(o=o, v=v):
                o[...] += v.astype(o.dtype)

    in_specs = [r.spec(tm) for r in rows] + [pl.BlockSpec(p.shape, lambda i, nd=p.ndim: (0,) * nd) for p in params]
    args = [r.arr for r in rows] + list(params)
    if stack is None:
        out_specs = [pl.BlockSpec((tm, w), lambda i: (i, 0)) for w, _ in new_outs]
        out_shape = [jax.ShapeDtypeStruct((R, w), dt) for w, dt in new_outs]
    else:
        index, depth = stack
        out_specs = [pl.BlockSpec((None, tm, w), lambda i: (index, i, 0)) for w, _ in new_outs]
        out_shape = [jax.ShapeDtypeStruct((depth, R, w), dt) for w, dt in new_outs]
    out_specs += [pl.BlockSpec(s, lambda i, nd=len(s): (0,) * nd) for s in red_shapes]
    out_shape += [jax.ShapeDtypeStruct(s, F32) for s in red_shapes]
    aliases, scratch = {}, []
    if into is not None:
        out_specs.append(pl.BlockSpec(memory_space=pl.ANY))
        out_shape.append(jax.ShapeDtypeStruct(into.shape, into.dtype))
        if not fresh:
            in_specs.append(pl.BlockSpec(memory_space=pl.ANY))
            args.append(into)
            aliases = {len(args) - 1: len(out_shape) - 1}
        scratch = [pltpu.VMEM((tm, w), into.dtype) for w, _ in into_cols]
    if stacked_onto is not None:
        for j, prev in enumerate(stacked_onto):
            in_specs.append(pl.BlockSpec(memory_space=pl.ANY))
            args.append(prev)
            aliases[len(args) - 1] = j
    outs = pl.pallas_call(
        body,
        name=name,
        grid=(R // tm,),
        in_specs=in_specs,
        out_specs=out_specs,
        out_shape=out_shape,
        scratch_shapes=scratch,
        input_output_aliases=aliases,
        compiler_params=pltpu.CompilerParams(dimension_semantics=("arbitrary",), vmem_limit_bytes=VMEM_BIG),
    )(*args)
    return list(outs)


def vjp_fn(fn, n_row, n_out, passthrough=0):
    def bwd(*args):
        rows = [a.astype(F32) for a in args[:n_row]]
        cts = args[n_row:n_row + n_out]
        thru = args[n_row + n_out:n_row + n_out + passthrough]
        pars = args[n_row + n_out + passthrough:]
        outs, pull = jax.vjp(lambda *xs: tuple(fn(*xs)), *rows, *pars)
        grads = pull(tuple(c.astype(o.dtype) for c, o in zip(cts, outs)))
        return tuple(grads[:n_row]) + tuple(thru) + tuple(grads[n_row:])

    return bwd


def _ln(x, g, b):
    mu = jnp.mean(x, axis=-1, keepdims=True)
    xc = x - mu
    var = jnp.mean(xc * xc, axis=-1, keepdims=True)
    return xc * lax.rsqrt(var + LN_EPS) * g + b


def f_ln_in(x, g, b):
    return (_ln(x, g, b),)


def f_deepnorm(x, y, g, b):
    return (_ln(ALPHA * x + y, g, b),)


def _twice(fn):
    return lambda *args: fn(*args) * 2


STREAM = [(D_MODEL, F32), (D_MODEL, MXU_DTYPE)]


def _rope(t, cos, sin):
    parts = []
    for h in range(LIN_HEADS):
        th = t[:, h * LIN_KEY_DIM:(h + 1) * LIN_KEY_DIM]
        parts.append(th * cos + swap_halves(th) * sin)
    return jnp.concatenate(parts, axis=1)


def f_ret_pre(rq, rk, cos, sin):
    return _rope(rq, cos, sin), _rope(rk, cos, sin) * LIN_KEY_DIM ** -0.5


def _log_sigmoid(x):
    return jnp.minimum(x, 0.0) - jnp.log(1.0 + jnp.exp(-jnp.abs(x)))


def f_gla_pre(gq, glr, wlr, blr):
    gate_pre = dot_nn(glr, wlr) + blr
    return gq * LIN_KEY_DIM ** -0.5, _log_sigmoid(gate_pre) / GLA_GATE_NORMALIZER


def f_normgate(ret_raw, rg, gla_raw, gg, gnorm):
    rets, glas = [], []
    for h in range(LIN_HEADS):
        sl = slice(h * LIN_VAL_DIM, (h + 1) * LIN_VAL_DIM)
        t = ret_raw[:, sl]
        mu = jnp.mean(t, axis=-1, keepdims=True)
        tc = t - mu
        rets.append(tc * lax.rsqrt(jnp.mean(tc * tc, axis=-1, keepdims=True) + LN_EPS))
        u = gla_raw[:, sl]
        glas.append(u * lax.rsqrt(jnp.mean(u * u, axis=-1, keepdims=True) + LN_EPS) * gnorm)
    ret = jax.nn.silu(rg) * jnp.concatenate(rets, axis=1)
    gla = jax.nn.silu(gg) * jnp.concatenate(glas, axis=1)
    return ret, gla


def f_merge(p0, p1, p2, gl):
    out = jax.nn.sigmoid(gl[:, :D_MODEL]) * p0
    out = out + jax.nn.sigmoid(gl[:, D_MODEL:2 * D_MODEL]) * p1
    return (out + jax.nn.sigmoid(gl[:, 2 * D_MODEL:]) * p2,)


def f_loss(y, tgt):
    diff = y - tgt
    sq = jnp.sum(diff * diff, axis=0, keepdims=True)
    lanes = sq[:, :128]
    for j in range(1, D_MODEL // 128):
        lanes = lanes + sq[:, j * 128:(j + 1) * 128]
    return diff * (1.0 / D_MODEL), lanes * (0.5 / D_MODEL)


def _adam(g, w, m, v):
    m = ADAM_B1 * m + (1.0 - ADAM_B1) * g
    v = ADAM_B2 * v + (1.0 - ADAM_B2) * jnp.square(g)
    m_hat = m / (1.0 - ADAM_B1 ** ADAM_STEP)
    v_hat = v / (1.0 - ADAM_B2 ** ADAM_STEP)
    delta = -ADAM_LR * (m_hat / (jnp.sqrt(v_hat) + ADAM_EPS) + ADAM_WD * w)
    return delta, m, v


def f_sum_adam(*args):
    g = args[0].astype(F32)
    for p in args[1:-3]:
        g = g + p.astype(F32)
    return (g,) + _adam(g, *args[-3:])


def f_adam(g, w, m, v):
    return _adam(g, w, m, v)


GROUP = 4
GROUP_ROWS = GROUP * CHUNK
GROUP_KEYS = (GROUP + ATTN_LEFT_CHUNKS) * CHUNK
DIAG_ROWS = REL_CLIP + CHUNK
GRAD_ROWS = 200


def _blocks_by_delta():
    blocks = {}
    for qi in range(GROUP):
        for J in range(GROUP_KEYS // 128):
            blocks.setdefault(qi - 2 * J, []).append((qi, J))
    return blocks


def bias_table(rel_bias, side=None):
    n_tab = rel_bias.shape[0] * ATTN_HEADS

    def body(rb_ref, o_ref):
        h = pl.program_id(0)
        rm = lax.broadcasted_iota(jnp.int32, (CHUNK, 128), 0)
        cm = lax.broadcasted_iota(jnp.int32, (CHUNK, 128), 1)
        ahead = jnp.where(cm >= CHUNK, 1, 0)
        for delta, where in _blocks_by_delta().items():
            base = 64 * delta + PAD_ROWS
            rel = jnp.minimum(base + rm - cm, REL_CLIP) + REL_CLIP
            in_band = (ahead - delta >= 0) & (ahead - delta <= ATTN_LEFT_CHUNKS)
            lo = max(REL_MIN, min(base - 127, REL_CLIP) + REL_CLIP)
            hi = min(base + CHUNK - 1, REL_CLIP) + REL_CLIP

            def step(r, acc, rel=rel):
                return jnp.where(rel == r, rb_ref[h * N_REL + r], acc)

            blk = jnp.zeros((CHUNK, 128), F32) if hi < lo else lax.fori_loop(lo, hi + 1, step, jnp.zeros((CHUNK, 128), F32))
            blk = jnp.where(in_band, blk, NEG_INF)
            for qi, J in where:
                o_ref[pl.ds(qi * CHUNK, CHUNK), pl.ds(J * 128, 128)] = blk

    return _side_call(
        side, body,
        name="bias_table",
        grid=(n_tab,),
        in_specs=[pl.BlockSpec(memory_space=pltpu.SMEM)],
        out_specs=[pl.BlockSpec((None, GROUP_ROWS, GROUP_KEYS), lambda h: (h, 0, 0))],
        out_shape=[jax.ShapeDtypeStruct((n_tab, GROUP_ROWS, GROUP_KEYS), F32)],
        scratch_shapes=[],
        args=[rel_bias.reshape(n_tab * N_REL)],
    )[0]


def bias_table_grad(dbias):
    blocks = _blocks_by_delta()
    below_clip = range(-(CHUNK - 1), REL_CLIP)
    deltas = [d for d in sorted(blocks) if any(-127 <= pos - 64 * d - PAD_ROWS <= CHUNK - 1 for pos in below_clip)]

    def body(d_ref, o_ref, diag):
        sums = []
        for delta in deltas:
            acc = None
            for qi, J in blocks[delta]:
                blk = d_ref[pl.ds(qi * CHUNK, CHUNK), pl.ds(J * 128, 128)]
                acc = blk if acc is None else acc + blk
            sums.append(acc)
        off = lax.broadcasted_iota(jnp.int32, (CHUNK, 128), 0) - lax.broadcasted_iota(jnp.int32, (CHUNK, 128), 1)
        diag[...] = jnp.zeros_like(diag)

        def step(t, carry):
            pos = t - (CHUNK - 1)
            picked = jnp.zeros((CHUNK, 128), F32)
            for delta, s in zip(deltas, sums):
                picked = picked + jnp.where(off == pos - 64 * delta - PAD_ROWS, s, 0.0)
            diag[pl.ds(t, 1), :] = jnp.sum(picked, axis=0, keepdims=True)
            return carry

        lax.fori_loop(0, DIAG_ROWS - 1, step, 0)
        d = d_ref[...]
        total = jnp.sum(jnp.sum(d, axis=1, keepdims=True), axis=0, keepdims=True)
        o_ref[pl.ds(0, DIAG_ROWS), :] = jnp.broadcast_to(jnp.sum(diag[...], axis=1, keepdims=True), (DIAG_ROWS, 128))
        o_ref[pl.ds(DIAG_ROWS, GRAD_ROWS - DIAG_ROWS), :] = jnp.broadcast_to(total, (GRAD_ROWS - DIAG_ROWS, 128))

    out = pl.pallas_call(
        body,
        name="bias_table_grad",
        grid=(ATTN_HEADS,),
        in_specs=[pl.BlockSpec((None, GROUP_ROWS, GROUP_KEYS), lambda h: (h, 0, 0))],
        out_specs=pl.BlockSpec((None, GRAD_ROWS, 128), lambda h: (h, 0, 0)),
        out_shape=jax.ShapeDtypeStruct((ATTN_HEADS, GRAD_ROWS, 128), F32),
        scratch_shapes=[pltpu.VMEM((DIAG_ROWS, 128), F32)],
        compiler_params=pltpu.CompilerParams(dimension_semantics=("arbitrary",)),
    )(dbias)
    below = out[:, :DIAG_ROWS - 1, 0]
    clipped = out[:, DIAG_ROWS, 0] - jnp.sum(below, axis=1)
    return jnp.concatenate([jnp.zeros((ATTN_HEADS, REL_MIN), F32), below, clipped[:, None]], axis=1)


def _attn_stage(q_ref, k_ref, v_ref, qs, kp, vp, T):
    qs[...] = (q_ref[...] * ATTN_HEAD_DIM ** -0.5).astype(qs.dtype)
    for dst, src in ((kp, k_ref), (vp, v_ref)):
        dst[pl.ds(0, PAD_ROWS), :] = jnp.zeros((PAD_ROWS, ATTN_HEAD_DIM), dst.dtype)
        dst[pl.ds(PAD_ROWS, T), :] = src[...].astype(dst.dtype)


def _attn_probs(g, first_groups, qs, kp, b_ref):
    r0 = g * GROUP_ROWS if first_groups else pl.multiple_of(g * GROUP_ROWS, GROUP_ROWS)
    qc = qs[pl.ds(r0, GROUP_ROWS), :]
    kb = kp[pl.ds(r0, GROUP_KEYS), :]
    s = _mxu(qc, kb, _NT) + b_ref[...]
    if first_groups:
        k_chunk = lax.shift_right_logical(lax.broadcasted_iota(jnp.int32, (GROUP_ROWS, GROUP_KEYS), 1), 6)
        s = jnp.where(g * GROUP + k_chunk >= ATTN_LEFT_CHUNKS, s, NEG_INF)
    e = jnp.exp(s - jnp.max(s, axis=-1, keepdims=True))
    return r0, qc, kb, e / jnp.sum(e, axis=-1, keepdims=True)


def _for_groups(T, step):
    n_first = min(ATTN_LEFT_CHUNKS // GROUP, T // GROUP_ROWS)
    for g in range(n_first):
        step(g, True)
    n_loop = T // GROUP_ROWS - n_first
    lax.fori_loop(n_first, T // GROUP_ROWS, lambda g, c: (step(g, False), c)[1], 0, unroll=2 if n_loop % 2 == 0 else 1)


def _head_spec(T, base):
    return pl.BlockSpec((T, ATTN_HEAD_DIM), lambda h: (0, base + h))


def _attn_scratch(T):
    return [pltpu.VMEM((T, ATTN_HEAD_DIM), MXU_DTYPE)] + [pltpu.VMEM((T + PAD_ROWS, ATTN_HEAD_DIM), MXU_DTYPE)] * 2


def attn_fwd(proj, bias_tab, first_tab, side=None):
    T = proj.shape[0]

    def body(q_ref, k_ref, v_ref, b_ref, o_ref, qs, kp, vp):
        _attn_stage(q_ref, k_ref, v_ref, qs, kp, vp, T)

        def step(g, first):
            r0, _, _, p = _attn_probs(g, first, qs, kp, b_ref)
            o_ref[pl.ds(r0, GROUP_ROWS), :] = _mxu(p, vp[pl.ds(r0, GROUP_KEYS), :], _NN).astype(o_ref.dtype)

        _for_groups(T, step)

    out, = _side_call(
        side, body,
        name="attn_fwd",
        grid=(ATTN_HEADS,),
        in_specs=[_head_spec(T, COL["aq"] // 128), _head_spec(T, COL["ak"] // 128), _head_spec(T, COL["av"] // 128),
                  pl.BlockSpec((None, GROUP_ROWS, GROUP_KEYS), lambda h: (first_tab + h, 0, 0))],
        out_specs=[pl.BlockSpec((T, ATTN_HEAD_DIM), lambda h: (0, h))],
        out_shape=[jax.ShapeDtypeStruct((T, ATTN_HEADS * ATTN_HEAD_DIM), MXU_DTYPE)],
        scratch_shapes=_attn_scratch(T),
        args=[proj, proj, proj, bias_tab],
    )
    return out


def attn_bwd(proj, bias_tab, first_tab, dout, dproj, side=None):
    T = proj.shape[0]

    def body(q_ref, k_ref, v_ref, b_ref, do_ref, dproj_in, dbias_ref, dproj_ref, qs, kp, vp, dkp, dvp, dq, stage):
        del dproj_in
        h = pl.program_id(0)
        _attn_stage(q_ref, k_ref, v_ref, qs, kp, vp, T)
        dkp[...] = jnp.zeros_like(dkp)
        dvp[...] = jnp.zeros_like(dvp)
        dbias_ref[...] = jnp.zeros_like(dbias_ref)

        def step(g, first):
            r0, qc, kb, p = _attn_probs(g, first, qs, kp, b_ref)
            do = do_ref[pl.ds(r0, GROUP_ROWS), :]
            dvp[pl.ds(r0, GROUP_KEYS), :] += _mxu(p, do, _TN)
            dp = _mxu(do, vp[pl.ds(r0, GROUP_KEYS), :], _NT)
            ds = p * (dp - jnp.sum(dp * p, axis=-1, keepdims=True))
            dq[pl.ds(r0, GROUP_ROWS), :] = _mxu(ds, kb, _NN) * ATTN_HEAD_DIM ** -0.5
            dkp[pl.ds(r0, GROUP_KEYS), :] += _mxu(ds, qc, _TN)
            dbias_ref[...] += ds

        _for_groups(T, step)
        col = pl.multiple_of(h * ATTN_HEAD_DIM, ATTN_HEAD_DIM)
        for grad, first_row, base in ((dq, 0, "aq"), (dkp, PAD_ROWS, "ak"), (dvp, PAD_ROWS, "av")):
            stage[...] = grad[pl.ds(first_row, T), :].astype(stage.dtype)
            pltpu.sync_copy(stage, dproj_ref.at[:, pl.ds(COL[base] + col, ATTN_HEAD_DIM)])

    dbias, dproj = _side_call(
        side, body,
        name="attn_bwd",
        grid=(ATTN_HEADS,),
        in_specs=[_head_spec(T, COL["aq"] // 128), _head_spec(T, COL["ak"] // 128), _head_spec(T, COL["av"] // 128),
                  pl.BlockSpec((None, GROUP_ROWS, GROUP_KEYS), lambda h: (first_tab + h, 0, 0)),
                  pl.BlockSpec((T, ATTN_HEAD_DIM), lambda h: (0, h)),
                  pl.BlockSpec(memory_space=pl.ANY)],
        out_specs=[pl.BlockSpec((None, GROUP_ROWS, GROUP_KEYS), lambda h: (h, 0, 0)), pl.BlockSpec(memory_space=pl.ANY)],
        out_shape=[jax.ShapeDtypeStruct((ATTN_HEADS, GROUP_ROWS, GROUP_KEYS), F32),
                   jax.ShapeDtypeStruct(dproj.shape, dproj.dtype)],
        scratch_shapes=_attn_scratch(T) + [pltpu.VMEM((T + PAD_ROWS, ATTN_HEAD_DIM), F32)] * 2
        + [pltpu.VMEM((T, ATTN_HEAD_DIM), F32), pltpu.VMEM((T, ATTN_HEAD_DIM), dproj.dtype)],
        args=[proj, proj, proj, bias_tab, dout, dproj],
        aliases={5: 1},
    )
    return dproj, dbias


def _lin_segment(q, k, v, la, st):
    lower = _tri(CHUNK, True)
    outs = []
    for i in range(SEG_CHUNKS):
        sl = slice(i * CHUNK, (i + 1) * CHUNK)
        qi, ki, vi, lai = q[sl], k[sl], v[sl], la[sl]
        cum = chunk_cumsum(lai)
        last = jnp.sum(lai, axis=0, keepdims=True)
        e_pos, e_neg = jnp.exp(cum), jnp.exp(-cum)
        q_fwd = qi * e_pos
        s_fwd = dot_nt(q_fwd, ki * e_neg)
        s_bwd = dot_nt(qi * e_neg, ki * e_pos)
        o = dot_nn(jnp.where(lower, s_fwd, s_bwd), vi) + dot_nt(q_fwd, st)
        st = jnp.exp(last) * st + dot_tn(vi, ki * jnp.exp(last - cum))
        outs.append(o)
    return jnp.concatenate(outs, axis=0), st


class Win:
    def __init__(self, arr, width, base=0):
        self.arr, self.width, self.base = arr, width, base


def _lin_spec(w, seg_of):
    base = w.base
    if w.arr.shape[0] == 1:
        return pl.BlockSpec((1, w.width), lambda h, s: (0, base + h))
    return pl.BlockSpec((SEG, w.width), lambda h, s: (seg_of(s), base + h))


def _decay_rows(la_ref):
    la = la_ref[...]
    return jnp.broadcast_to(la, (SEG, LIN_KEY_DIM)) if la.shape[0] == 1 else la


def lin_fwd(name, q, k, v, la, side=None):
    T = q.arr.shape[0]
    nseg = T // SEG

    def body(q_ref, k_ref, v_ref, la_ref, o_ref, st_ref, st):
        @pl.when(pl.program_id(1) == 0)
        def _():
            st[...] = jnp.zeros_like(st)

        st0 = st[...]
        st_ref[...] = st0
        o, st1 = _lin_segment(q_ref[...], k_ref[...], v_ref[...], _decay_rows(la_ref), st0)
        o_ref[...] = o
        st[...] = st1

    fwd = lambda s: s
    out, states = _side_call(
        side, body,
        name=name,
        grid=(LIN_HEADS, nseg),
        in_specs=[_lin_spec(w, fwd) for w in (q, k, v, la)],
        out_specs=[pl.BlockSpec((SEG, LIN_VAL_DIM), lambda h, s: (s, h)),
                   pl.BlockSpec((None, None, LIN_VAL_DIM, LIN_KEY_DIM), lambda h, s: (h, s, 0, 0))],
        out_shape=[jax.ShapeDtypeStruct((T, LIN_HEADS * LIN_VAL_DIM), F32),
                   jax.ShapeDtypeStruct((LIN_HEADS, nseg, LIN_VAL_DIM, LIN_KEY_DIM), F32)],
        scratch_shapes=[pltpu.VMEM((LIN_VAL_DIM, LIN_KEY_DIM), F32)],
        args=[q.arr, k.arr, v.arr, la.arr],
    )
    return out, states


def lin_bwd(name, q, k, v, la, states, dout, with_decay_grad, side=None):
    T = q.arr.shape[0]
    nseg = T // SEG
    n_out = 4 if with_decay_grad else 3

    def body(q_ref, k_ref, v_ref, la_ref, st_ref, do_ref, *rest):
        outs, dst = rest[:n_out], rest[n_out]

        @pl.when(pl.program_id(1) == 0)
        def _():
            dst[...] = jnp.zeros_like(dst)

        _, pull = jax.vjp(_lin_segment, q_ref[...].astype(F32), k_ref[...].astype(F32), v_ref[...].astype(F32),
                          _decay_rows(la_ref), st_ref[...])
        grads = pull((do_ref[...], dst[...]))
        for o, g in zip(outs, grads[:n_out]):
            o[...] = g
        dst[...] = grads[4]

    rev = lambda s: nseg - 1 - s
    key_spec = pl.BlockSpec((SEG, LIN_KEY_DIM), lambda h, s: (rev(s), h))
    val_spec = pl.BlockSpec((SEG, LIN_VAL_DIM), lambda h, s: (rev(s), h))
    out_specs = [key_spec, key_spec, val_spec] + ([key_spec] if with_decay_grad else [])
    widths = [LIN_KEY_DIM, LIN_KEY_DIM, LIN_VAL_DIM] + ([LIN_KEY_DIM] if with_decay_grad else [])
    return _side_call(
        side, body,
        name=name,
        grid=(LIN_HEADS, nseg),
        in_specs=[_lin_spec(w, rev) for w in (q, k, v, la)] + [
            pl.BlockSpec((None, None, LIN_VAL_DIM, LIN_KEY_DIM), lambda h, s: (h, rev(s), 0, 0)), val_spec],
        out_specs=out_specs,
        out_shape=[jax.ShapeDtypeStruct((T, LIN_HEADS * w), F32) for w in widths],
        scratch_shapes=[pltpu.VMEM((LIN_VAL_DIM, LIN_KEY_DIM), F32)],
        args=[q.arr, k.arr, v.arr, la.arr, states, dout],
    )


def _coords():
    return lax.axis_index("x"), lax.axis_index("y"), lax.axis_index("c")


def _peer(k):
    x, y, c = _coords()
    px = 1 - x if k & 4 else x
    py = 1 - y if k & 2 else y
    pc = 1 - c if k & 1 else c
    return (px, py, pc), 4 * px + 2 * py + pc


def _remote(src, dst, send_sem, recv_sem, device):
    return pltpu.make_async_remote_copy(src_ref=src, dst_ref=dst, send_sem=send_sem, recv_sem=recv_sem,
                                        device_id=device, device_id_type=pl.DeviceIdType.MESH)


class Side:
    def __init__(self, kind, arrays):
        self.kind, self.arrays, self.n = kind, list(arrays), len(arrays)
        self.outs_per = 1
        self.slots = {"gather": N_DEV - 1, "pair": 4, "chips": 3}[kind]

    def out_shape(self):
        if self.kind == "gather":
            return [jax.ShapeDtypeStruct((N_DEV,) + a.shape, a.dtype) for a in self.arrays]
        return [jax.ShapeDtypeStruct((4,) + a.shape[1:], a.dtype) for a in self.arrays]

    def sems(self):
        return [pltpu.SemaphoreType.DMA((self.n, self.slots)), pltpu.SemaphoreType.DMA((self.n, self.slots)),
                pltpu.SemaphoreType.DMA((self.n,))]

    def start(self, ins, outs, sems):
        send, recv, own = sems
        x, y, c = _coords()
        me = 4 * x + 2 * y + c
        for a in range(self.n):
            if self.kind == "gather":
                pltpu.make_async_copy(ins[a], outs[a].at[me], own.at[a]).start()
                for k in (1, 2, 4, 6):
                    peer, _ = _peer(k)
                    _remote(ins[a], outs[a].at[me], send.at[a, k - 1], recv.at[a, k - 1], peer).start()
            elif self.kind == "pair":
                for j in range(4):
                    _remote(ins[a].at[2 * j + 1 - c], outs[a].at[j], send.at[a, j], recv.at[a, j], (x, y, 1 - c)).start()
            else:
                chip = 2 * x + y
                pltpu.make_async_copy(ins[a].at[chip], outs[a].at[chip], own.at[a]).start()
                for d in (1, 2, 3):
                    tx, ty = (1 - x if d & 2 else x), (1 - y if d & 1 else y)
                    _remote(ins[a].at[2 * tx + ty], outs[a].at[chip], send.at[a, d - 1], recv.at[a, d - 1], (tx, ty, c)).start()

    def finish(self, ins, outs, sems):
        send, recv, own = sems
        x, y, c = _coords()
        me = 4 * x + 2 * y + c
        sibling = (x, y, 1 - c)
        for a in range(self.n):
            if self.kind == "gather":
                blk = outs[a].at[me]
                for k in (2, 4, 6):
                    _, pidx = _peer(k)
                    _remote(blk, blk, send.at[a, k - 1], recv.at[a, k - 1], sibling).wait_recv()
                    _remote(outs[a].at[pidx], outs[a].at[pidx], send.at[a, k], recv.at[a, k], sibling).start()
                pltpu.make_async_copy(ins[a], blk, own.at[a]).wait()
                for k in range(1, N_DEV):
                    _remote(blk, blk, send.at[a, k - 1], recv.at[a, k - 1], sibling).wait_send()
                for k in (1, 3, 5, 7):
                    _remote(blk, blk, send.at[a, k - 1], recv.at[a, k - 1], sibling).wait_recv()
            elif self.kind == "pair":
                for j in range(4):
                    _remote(ins[a].at[j], outs[a].at[j], send.at[a, j], recv.at[a, j], sibling).wait()
            else:
                slab = outs[a].at[0]
                pltpu.make_async_copy(slab, slab, own.at[a]).wait()
                for d in (1, 2, 3):
                    _remote(slab, slab, send.at[a, d - 1], recv.at[a, d - 1], sibling).wait()


def _side_call(side, body, *, name, grid, in_specs, out_specs, out_shape, scratch_shapes, args, semantics=None,
               aliases=None):
    in_specs, out_specs, out_shape = list(in_specs), list(out_specs), list(out_shape)
    scratch_shapes, args = list(scratch_shapes), list(args)
    n_in, n_out = len(in_specs), len(out_specs)
    kernel_body = body
    if side is not None:
        n_so = side.n * side.outs_per

        def kernel_body(*refs):
            ins, side_ins = refs[:n_in], refs[n_in:n_in + side.n]
            pos = n_in + side.n
            outs, side_outs = refs[pos:pos + n_out], refs[pos + n_out:pos + n_out + n_so]
            rest = refs[pos + n_out + n_so:]
            scratch, sems = rest[:-3], rest[-3:]
            ids = [pl.program_id(ax) for ax in range(len(grid))]
            first = functools.reduce(jnp.logical_and, [i == 0 for i in ids])
            last = functools.reduce(jnp.logical_and, [i == g - 1 for i, g in zip(ids, grid)])

            @pl.when(first)
            def _():
                side.start(side_ins, side_outs, sems)

            body(*ins, *outs, *scratch)

            @pl.when(last)
            def _():
                side.finish(side_ins, side_outs, sems)

        any_spec = pl.BlockSpec(memory_space=pl.ANY)
        in_specs += [any_spec] * side.n
        args += side.arrays
        out_specs += [any_spec] * n_so
        out_shape += side.out_shape()
        scratch_shapes += side.sems()
        semantics = ("arbitrary",) * len(grid)
    outs = pl.pallas_call(
        kernel_body,
        name=name,
        grid=grid,
        in_specs=in_specs,
        out_specs=out_specs,
        out_shape=out_shape,
        scratch_shapes=scratch_shapes,
        input_output_aliases=aliases or {},
        compiler_params=pltpu.CompilerParams(dimension_semantics=semantics or ("arbitrary",) * len(grid),
                                             vmem_limit_bytes=VMEM_BIG),
    )(*args)
    if side is not None:
        side.results = list(outs[n_out:])
    return list(outs[:n_out])


def carrier(name, side):
    _side_call(side, lambda: None, name=name, grid=(1,), in_specs=[], out_specs=[], out_shape=[], scratch_shapes=[], args=[])
    return side.results


def all_reduce_small(vec):
    def body(v_ref, o_ref, land, send_sems, recv_sems):
        x, y, c = _coords()
        me = 4 * x + 2 * y + c
        land[me] = v_ref[...]
        copies = []
        for k in range(1, N_DEV):
            peer, _ = _peer(k)
            cp = pltpu.make_async_remote_copy(
                src_ref=v_ref, dst_ref=land.at[me], send_sem=send_sems.at[k - 1], recv_sem=recv_sems.at[k - 1],
                device_id=peer, device_id_type=pl.DeviceIdType.MESH)
            cp.start()
            copies.append(cp)
        for cp in copies:
            cp.wait()
        total = land[0]
        for s in range(1, N_DEV):
            total = total + land[s]
        o_ref[...] = total

    return pl.pallas_call(
        body,
        name="all_reduce_small",
        in_specs=[pl.BlockSpec(memory_space=pltpu.VMEM)],
        out_specs=pl.BlockSpec(memory_space=pltpu.VMEM),
        out_shape=jax.ShapeDtypeStruct(vec.shape, F32),
        scratch_shapes=[pltpu.VMEM((N_DEV,) + vec.shape, F32), pltpu.SemaphoreType.DMA((N_DEV - 1,)),
                        pltpu.SemaphoreType.DMA((N_DEV - 1,))],
    )(vec)


def _pack(parts):
    flat = jnp.concatenate([p.reshape(-1).astype(F32) for p in parts])
    n = flat.shape[0]
    rows = -(-n // 1024) * 8
    return jnp.pad(flat, (0, rows * 128 - n)).reshape(rows, 128)


def _unpack(packed, shapes):
    flat, out, pos = packed.reshape(-1), [], 0
    for s in shapes:
        n = math.prod(s)
        out.append(flat[pos:pos + n].reshape(s))
        pos += n
    return out


def _row2(v):
    return v.reshape(1, -1)


def _rider(hooks, name, ctx=None):
    if name not in hooks:
        return None, lambda: None
    side, done = hooks[name](ctx)
    return side, lambda: done(side.results)


def _layer_forward(h, hb, W, consts, hooks):
    T = h.shape[0]
    cos, sin, ret_decay = consts
    side, landed = _rider(hooks, "proj")
    proj = mm("proj", hb, W["in"], "nn", tm=4096, tn=512, tk=1024, side=side)
    landed()
    bias_tab, first_tab = W["bias_tab"]
    side, landed = _rider(hooks, "attn_fwd")
    attn = attn_fwd(proj, bias_tab, first_tab, side=side)
    landed()
    rq2, rk2 = row_call("ret_pre", f_ret_pre, [Row(proj, 512, COL["rq"] // 512), Row(proj, 512, COL["rk"] // 512),
                                              Row(cos), Row(sin)], [], [(512, F32), (512, F32)], tm=512)
    side, landed = _rider(hooks, "ret_fwd")
    ret_raw, ret_st = lin_fwd("ret_fwd", Win(rq2, 128), Win(rk2, 128), Win(proj, 256, COL["rv"] // 256), Win(ret_decay, 128),
                              side=side)
    landed()
    gq2, gla_la = row_call("gla_pre", f_gla_pre, [Row(proj, 512, COL["gq"] // 512), Row(proj, GLR_BLOCK, GLR_OFF // GLR_BLOCK)],
                           [W["wlr"], W["blr"]], [(512, F32), (512, F32)], tm=512)
    side, landed = _rider(hooks, "gla_fwd")
    gla_raw, gla_st = lin_fwd("gla_fwd", Win(gq2, 128), Win(proj, 128, COL["gk"] // 128), Win(proj, 256, COL["gv"] // 256),
                              Win(gla_la, 128), side=side)
    landed()
    ret_b, gla_b = row_call("normgate", f_normgate, [Row(ret_raw), Row(proj, 1024, COL["rg"] // 1024), Row(gla_raw),
                                                    Row(proj, 1024, COL["gg"] // 1024)], [W["gnorm"]],
                            [(1024, MXU_DTYPE), (1024, MXU_DTYPE)], tm=256)
    branches = (attn, ret_b, gla_b)
    p = [mm("branch_proj", branches[n], W["branch"][n], "nn", tm=1024, tn=1024, tk=1024) for n in range(3)]
    merged, = row_call("merge", f_merge, [Row(p[0]), Row(p[1]), Row(p[2]), Row(proj, GATES_W, GATES_OFF // GATES_W)], [],
                       [(1024, MXU_DTYPE)], tm=256)
    mix = mm("out_proj", merged, W["out"], "nn", tm=1024, tn=1024, tk=1024)
    x1, x1b = row_call("ln1", _twice(f_deepnorm), [Row(h), Row(mix)], [W["ln1_g"], W["ln1_b"]], STREAM, tm=512)
    side, landed = _rider(hooks, "mlp_up")
    u, act = mm("mlp_up", x1b, W["up"], "nn", tm=2048, tn=512, tk=1024, n_out=2, out_dtypes=(F32, MXU_DTYPE),
                epi=lambda r: (r, jnp.square(jnp.maximum(r, 0.0))), side=side)
    landed()
    y = mm("mlp_down", act, W["down"], "nn", tm=1024, tn=1024, tk=1024)
    x2, x2b = row_call("ln2", _twice(f_deepnorm), [Row(x1), Row(y)], [W["ln2_g"], W["ln2_b"]], STREAM, tm=512)
    saved = dict(h=h, hb=hb, proj=proj, rq2=rq2, rk2=rk2, ret_st=ret_st, gq2=gq2, gla_la=gla_la,
                 gla_st=gla_st, ret_raw=ret_raw, gla_raw=gla_raw, branches=branches, p=p, merged=merged, mix=mix, x1=x1, x1b=x1b,
                 u=u, act=act, y=y)
    return x2, x2b, saved


def _layer_backward(dx2, S, W, consts, hooks):
    cos, sin, ret_decay = consts
    h, proj, T = S["h"], S["proj"], S["h"].shape[0]
    add = lambda r, e: r + e
    dx1a, dy, dg2, db2 = row_call("ln2_bwd", vjp_fn(f_deepnorm, 2, 1), [Row(S["x1"]), Row(S["y"]), Row(dx2)],
                                  [W["ln2_g"], W["ln2_b"]], [(1024, F32), (1024, MXU_DTYPE)], [(1, 1024), (1, 1024)], tm=256)
    du = mm("mlp_down_bwd", dy, W["down"], "nt", tm=1024, tn=512, tk=1024, extras=(S["u"],),
            epi=lambda r, u: r * (2.0 * jnp.maximum(u, 0.0)), out_dtype=MXU_DTYPE)
    g_down = mm("mlp_down_wgrad", S["act"], dy, "tn", tm=1024, tn=1024, tk=512, out_dtype=WIRE_DTYPE)
    dx1 = mm("mlp_up_bwd", du, W["up"], "nt", tm=1024, tn=1024, tk=1024, extras=(dx1a,), epi=add)
    g_up = mm("mlp_up_wgrad", jnp.transpose(S["x1b"]), du, "nn", tm=1024, tn=D_FF // N_DEV, tk=512, out_dtype=WIRE_DTYPE,
              column_slabs=True)
    dha, dmix, dg1, db1 = row_call("ln1_bwd", vjp_fn(f_deepnorm, 2, 1), [Row(h), Row(S["mix"]), Row(dx1)],
                                   [W["ln1_g"], W["ln1_b"]], [(1024, F32), (1024, MXU_DTYPE)], [(1, 1024), (1, 1024)], tm=256)
    dmerged = mm("out_proj_bwd", dmix, W["out"], "nt", tm=1024, tn=1024, tk=1024)
    g_out = mm("out_proj_wgrad", S["merged"], dmix, "tn", tm=1024, tn=1024, tk=512, out_dtype=WIRE_DTYPE)
    dproj = jax.ShapeDtypeStruct((T, PROJ_PAD), MXU_DTYPE)
    p = S["p"]
    dp0, dp1, dp2, dproj = row_call(
        "merge_bwd", vjp_fn(f_merge, 4, 1), [Row(p[0]), Row(p[1]), Row(p[2]), Row(proj, GATES_W, GATES_OFF // GATES_W), Row(dmerged)],
        [], [(1024, MXU_DTYPE)] * 3, tm=256, into=dproj, into_cols=[(GATES_W, GATES_OFF)])
    dps = (dp0, dp1, dp2)
    dbr = [mm("branch_proj_bwd", dps[n], W["branch"][n], "nt", tm=1024, tn=1024, tk=1024,
              out_dtype=MXU_DTYPE if n == 0 else F32) for n in range(3)]
    g_branch = [mm("branch_proj_wgrad", S["branches"][n], dps[n], "tn", tm=1024, tn=1024, tk=512, out_dtype=WIRE_DTYPE)
                for n in range(3)]
    dret_raw, dgla_raw, dgnorm, dproj = row_call(
        "normgate_bwd", vjp_fn(lambda a, b, c, d, g: _reorder_normgate(a, b, c, d, g), 4, 2),
        [Row(S["ret_raw"]), Row(S["gla_raw"]), Row(proj, 1024, COL["rg"] // 1024), Row(proj, 1024, COL["gg"] // 1024),
         Row(dbr[1]), Row(dbr[2])], [W["gnorm"]], [(1024, F32), (1024, F32)], [(1, LIN_VAL_DIM)], tm=256,
        into=dproj, into_cols=[(1024, COL["rg"]), (1024, COL["gg"])])
    side, landed = _rider(hooks, "gla_bwd")
    dgq2, dgk, dgv, dgla_la = lin_bwd("gla_bwd", Win(S["gq2"], 128), Win(proj, 128, COL["gk"] // 128),
                                      Win(proj, 256, COL["gv"] // 256), Win(S["gla_la"], 128), S["gla_st"], dgla_raw, True,
                                      side=side)
    landed()
    dwlr, dblr, dproj = row_call(
        "gla_pre_bwd", _gla_pre_bwd, [Row(proj, 512, COL["gq"] // 512), Row(proj, GLR_BLOCK, GLR_OFF // GLR_BLOCK),
                                      Row(dgq2), Row(dgla_la), Row(dgk), Row(dgv)], [W["wlr"], W["blr"]], [],
        [(GLR_BLOCK, 512), (1, 512)], tm=256, into=dproj,
        into_cols=[(512, COL["gq"]), (PROJ_PAD - GLR_OFF, GLR_OFF), (512, COL["gk"]), (1024, COL["gv"])])
    drq2, drk2, drv = lin_bwd("ret_bwd", Win(S["rq2"], 128), Win(S["rk2"], 128), Win(proj, 256, COL["rv"] // 256),
                              Win(ret_decay, 128), S["ret_st"], dret_raw, False)
    dproj, = row_call(
        "ret_pre_bwd", _ret_pre_bwd, [Row(proj, 512, COL["rq"] // 512), Row(proj, 512, COL["rk"] // 512), Row(cos), Row(sin),
                                      Row(drq2), Row(drk2), Row(drv)], [], [], tm=256, into=dproj,
        into_cols=[(512, COL["rq"]), (512, COL["rk"]), (1024, COL["rv"])])
    side, landed = _rider(hooks, "attn_bwd")
    dproj, dbias = attn_bwd(proj, *W["bias_tab"], dbr[0], dproj, side=side)
    landed()
    d_rel_bias = bias_table_grad(dbias)
    side, landed = _rider(hooks, "proj_wgrad", dict(branch=g_branch, out=g_out, up=g_up, down=g_down))
    g_in = mm("proj_wgrad", jnp.transpose(S["hb"]), dproj, "nn", tm=1024, tn=1280, tk=512, out_dtype=WIRE_DTYPE, side=side)
    landed()
    side, landed = _rider(hooks, "proj_bwd", {"in": g_in})
    dh = mm("proj_bwd", dproj, W["in"], "nt", tm=1024, tn=1024, tk=1280, extras=(dha,), epi=add, side=side)
    landed()
    big = dict(g_in=g_in, branch=g_branch, out=g_out, up=g_up, down=g_down)
    small = dict(rel_bias=d_rel_bias, wlr=dwlr[:GLA_RANK], blr=dblr, gnorm=dgnorm, ln1_g=dg1, ln1_b=db1, ln2_g=dg2, ln2_b=db2)
    return dh, big, small


def _reorder_normgate(ret_raw, gla_raw, rg, gg, gnorm):
    return f_normgate(ret_raw, rg, gla_raw, gg, gnorm)


def _gla_pre_bwd(gq, glr, dq2, dla, dgk, dgv, wlr, blr):
    _, pull = jax.vjp(f_gla_pre, gq, glr, wlr, blr)
    dgq, dglr, dwlr, dblr = pull((dq2, dla))
    dglr = jnp.concatenate([dglr, jnp.zeros((dglr.shape[0], PROJ_PAD - GLR_OFF - GLR_BLOCK), F32)], axis=1)
    return dgq, dglr, dgk, dgv, dwlr, dblr


def _ret_pre_bwd(rq, rk, cos, sin, drq2, drk2, drv):
    _, pull = jax.vjp(lambda a, b: f_ret_pre(a, b, cos, sin), rq, rk)
    drq, drk = pull((drq2, drk2))
    return drq, drk, drv


SHARDED = ("in", "branch", "out", "up", "down")

_FULL = {
    "branch": lambda g: [g[:, n].reshape(D_MODEL, D_MODEL) for n in range(3)],
    "out": lambda g: g.reshape(D_MODEL, D_MODEL),
    "up": lambda g: jnp.transpose(g, (1, 0, 2)).reshape(D_MODEL, D_FF),
    "down": lambda g: g.reshape(D_FF, D_MODEL),
}


def _owner_pieces(s):
    shard = PROJ_WIDTH // N_DEV
    lo, hi = s * shard, (s + 1) * shard
    pieces = []
    for a, b, padded in ((0, GLR_SRC, 0), (GLR_SRC, GATES_SRC, GLR_OFF), (GATES_SRC, PROJ_WIDTH, GATES_OFF)):
        x0, x1 = max(lo, a), min(hi, b)
        if x0 < x1:
            pieces.append((x0 - lo, x1 - x0, padded + x0 - a))
    return pieces


def _full_in(g):
    placed = sorted((p, s, c, w) for s in range(N_DEV) for c, w, p in _owner_pieces(s))
    cols = [g[s][:, c:c + w] for _, s, c, w in placed]
    return jnp.concatenate(cols + [jnp.zeros((D_MODEL, PROJ_PAD - PROJ_WIDTH), g.dtype)], axis=1)


def _full_wlr(g):
    wlr = jnp.transpose(g, (1, 0, 2)).reshape(GLA_RANK, LIN_HEADS * LIN_KEY_DIM)
    return jnp.pad(wlr, ((0, GLR_BLOCK - GLA_RANK), (0, 0)))


def _slab_in(g):
    return jnp.stack([jnp.concatenate([g[:, p:p + w] for _, w, p in _owner_pieces(s)], axis=1) for s in range(N_DEV)])


_SLABS = {
    "in": _slab_in,
    "branch": lambda gs: jnp.stack([b.reshape(N_DEV, D_MODEL // N_DEV, D_MODEL) for b in gs], axis=1),
    "out": lambda g: g.reshape(N_DEV, D_MODEL // N_DEV, D_MODEL),
    "up": lambda g: g,
    "down": lambda g: g.reshape(N_DEV, D_FF // N_DEV, D_MODEL),
}


def _pair_sums(name, slabs):
    sums = []
    for mine, sibs in zip(slabs, carrier(name, Side("pair", slabs))):
        width = mine.shape[-1]
        rows = math.prod(mine.shape[1:-1])
        tm = 256 if rows % 256 == 0 else 128
        nb = rows // tm
        my_rows = Row(mine.reshape(-1, width), rowmap=lambda i, nb=nb: (2 * (i // nb) + lax.axis_index("c")) * nb + i % nb)
        out, = row_call("pair_sum", lambda q, p: (p.astype(F32) + q.astype(F32),),
                        [Row(sibs.reshape(-1, width)), my_rows], [], [(width, WIRE_DTYPE)], tm=tm)
        sums.append(out.reshape(sibs.shape))
    return sums


def _sum_adam(name, parts, w, m, v, layer, tm, earlier):
    C = w.shape[-1]
    rows = [Row(parts, lead=(s,)) for s in range(parts.shape[0])] + [Row(a, lead=(layer,)) for a in (w, m, v)]
    return row_call(name, f_sum_adam, rows, [], [(C, F32)] * 4, tm=tm, stack=(layer, DEPTH), stacked_onto=earlier)


def kernel(x, ln_in_g, ln_in_b, w_in, rel_bias, gla_w_lr, gla_b_lr, gla_norm_g, w_branch, w_out, ln1_g, ln1_b, w_up, w_down, ln2_g, ln2_b, loss_target, m_ln_in_g, m_ln_in_b, m_w_in, m_rel_bias, m_gla_w_lr, m_gla_b_lr, m_gla_norm_g, m_w_branch, m_w_out, m_ln1_g, m_ln1_b, m_w_up, m_w_down, m_ln2_g, m_ln2_b, v_ln_in_g, v_ln_in_b, v_w_in, v_rel_bias, v_gla_w_lr, v_gla_b_lr, v_gla_norm_g, v_w_branch, v_w_out, v_ln1_g, v_ln1_b, v_w_up, v_w_down, v_ln2_g, v_ln2_b):
    T = x.shape[1]
    x2d = x.reshape(T, D_MODEL)
    tgt = loss_target.reshape(T, D_MODEL)
    me = 4 * lax.axis_index("x") + 2 * lax.axis_index("y") + lax.axis_index("c")

    inv = ROPE_BASE ** (-jnp.arange(0, LIN_KEY_DIM, 2, dtype=F32) / LIN_KEY_DIM)
    ang = jnp.arange(T, dtype=F32)[:, None] * inv[None, :]
    cos = jnp.concatenate([jnp.cos(ang), jnp.cos(ang)], axis=1)
    sin = jnp.concatenate([-jnp.sin(ang), jnp.sin(ang)], axis=1)
    log_gamma = jnp.log1p(-jnp.exp2(-5.0 - jnp.arange(LIN_HEADS, dtype=F32)))
    ret_decay = jnp.repeat(log_gamma, LIN_KEY_DIM).reshape(1, LIN_HEADS * LIN_KEY_DIM)
    consts = (cos, sin, ret_decay)

    layers = [dict(rel_bias=rel_bias[l], blr=_row2(gla_b_lr[l]), gnorm=_row2(gla_norm_g[l]), ln1_g=_row2(ln1_g[l]),
                   ln1_b=_row2(ln1_b[l]), ln2_g=_row2(ln2_g[l]), ln2_b=_row2(ln2_b[l])) for l in range(DEPTH)]
    shards = dict(branch=w_branch, out=w_out, up=w_up, down=w_down)
    shards["in"] = w_in
    wire = [{n: shards[n][l].astype(WIRE_DTYPE) for n in SHARDED} for l in range(DEPTH)]

    def gather_of(l, names):
        def landed(res):
            for n, g in zip(names, res):
                layers[l][n] = _full_in(g) if n == "in" else _FULL[n](g)
        return lambda ctx: (Side("gather", [wire[l][n] for n in names]), landed)

    first = Side("gather", [wire[0]["in"]] + [gla_w_lr[l] for l in range(DEPTH)])
    bias_tabs = bias_table(rel_bias, side=first)
    layers[0]["in"] = _full_in(first.results[0])
    for l in range(DEPTH):
        layers[l]["wlr"] = _full_wlr(first.results[1 + l])
        layers[l]["bias_tab"] = (bias_tabs, l * ATTN_HEADS)

    fwd_hooks = [{"proj": gather_of(1, ("in",)), "attn_fwd": gather_of(0, ("up", "down")),
                  "ret_fwd": gather_of(0, ("branch", "out"))},
                 {"proj": gather_of(1, ("up", "down")), "attn_fwd": gather_of(1, ("branch", "out"))}]
    h, hb = row_call("ln_in", _twice(f_ln_in), [Row(x2d)], [_row2(ln_in_g), _row2(ln_in_b)], STREAM, tm=512)
    saved = []
    for l in range(DEPTH):
        h, hb, s = _layer_forward(h, hb, layers[l], consts, fwd_hooks[l])
        saved.append(s)
    dh, loss_lanes = row_call("loss", f_loss, [Row(h), Row(tgt)], [], [(D_MODEL, F32)], [(1, 128)], tm=512)

    received = [dict() for _ in range(DEPTH)]

    def scatter_of(l, names, sums=None):
        def make(ctx):
            parts = sums if sums is not None else _pair_sums("pair_grads", [_SLABS[n](ctx[n]) for n in names])

            def landed(res):
                received[l].update(zip(names, res))
            return Side("chips", parts), landed
        return make

    small = [None] * DEPTH
    dh, big, small[1] = _layer_backward(dh, saved[1], layers[1], consts, {})
    sums1 = _pair_sums("pair_grads", [_SLABS[n](big["g_in"] if n == "in" else big[n]) for n in SHARDED])
    rest = SHARDED[1:]
    dh, big, small[0] = _layer_backward(dh, saved[0], layers[0], consts, {
        "gla_bwd": scatter_of(1, ("in",), sums1[:1]), "attn_bwd": scatter_of(1, rest, sums1[1:]),
        "proj_wgrad": scatter_of(0, rest), "proj_bwd": scatter_of(0, ("in",))})
    grad_x, d_in_g, d_in_b = row_call("ln_in_bwd", vjp_fn(f_ln_in, 1, 1), [Row(x2d), Row(dh)],
                                      [_row2(ln_in_g), _row2(ln_in_b)], [(D_MODEL, F32)], [(1, D_MODEL), (1, D_MODEL)], tm=256)

    small_names = ("rel_bias", "wlr", "blr", "gnorm", "ln1_g", "ln1_b", "ln2_g", "ln2_b")
    parts = [d_in_g, d_in_b] + [small[l][n] for l in range(DEPTH) for n in small_names] + [loss_lanes]
    shapes = [(D_MODEL,), (D_MODEL,)] + [s for l in range(DEPTH) for s in (
        (ATTN_HEADS, N_REL), (GLA_RANK, 512), (512,), (LIN_VAL_DIM,), (D_MODEL,), (D_MODEL,), (D_MODEL,), (D_MODEL,))] + [(128,)]
    red = _unpack(all_reduce_small(_pack(parts)), shapes)
    loss = jnp.sum(red[-1])
    g_small = {"ln_in_g": red[0], "ln_in_b": red[1]}
    for l in range(DEPTH):
        for j, n in enumerate(small_names):
            g_small[(l, n)] = red[2 + l * len(small_names) + j]
    shard = LIN_HEADS * LIN_KEY_DIM // N_DEV
    g_wlr = jnp.stack([lax.dynamic_slice_in_dim(g_small[(l, "wlr")], me * shard, shard, axis=1) for l in range(DEPTH)])
    stack = lambda n: jnp.stack([g_small[(l, n)] for l in range(DEPTH)])
    g_rep = [g_small["ln_in_g"], g_small["ln_in_b"], stack("rel_bias"), g_wlr, stack("blr"), stack("gnorm"),
             stack("ln1_g"), stack("ln1_b"), stack("ln2_g"), stack("ln2_b")]
    w_rep = [ln_in_g, ln_in_b, rel_bias, gla_w_lr, gla_b_lr, gla_norm_g, ln1_g, ln1_b, ln2_g, ln2_b]
    m_rep = [m_ln_in_g, m_ln_in_b, m_rel_bias, m_gla_w_lr, m_gla_b_lr, m_gla_norm_g, m_ln1_g, m_ln1_b, m_ln2_g, m_ln2_b]
    v_rep = [v_ln_in_g, v_ln_in_b, v_rel_bias, v_gla_w_lr, v_gla_b_lr, v_gla_norm_g, v_ln1_g, v_ln1_b, v_ln2_g, v_ln2_b]
    rep_shapes = [w.shape for w in w_rep]
    d_pack, m_pack, v_pack = row_call("adam_small", f_adam, [Row(_pack(g_rep)), Row(_pack(w_rep)), Row(_pack(m_rep)), Row(_pack(v_rep))],
                                      [], [(128, F32)] * 3, tm=256)
    d_rep, nm_rep, nv_rep = _unpack(d_pack, rep_shapes), _unpack(m_pack, rep_shapes), _unpack(v_pack, rep_shapes)

    def owned(key, w, m, v, name, tm):
        C = w.shape[-1]
        flat = lambda a: a.reshape(DEPTH, -1, C)
        outs = None
        for l in range(DEPTH):
            outs = _sum_adam(name, received[l][key].reshape(4, -1, C), flat(w), flat(m), flat(v), l, tm, outs)
        return [o.reshape(w.shape) for o in outs]

    o_in = owned("in", w_in, m_w_in, v_w_in, "adam_w_in", 256)
    o_br = owned("branch", w_branch, m_w_branch, v_w_branch, "adam_w_branch", 128)
    o_out = owned("out", w_out, m_w_out, v_w_out, "adam_w_out", 128)
    o_up = owned("up", w_up, m_w_up, v_w_up, "adam_w_up", 256)
    o_down = owned("down", w_down, m_w_down, v_w_down, "adam_w_down", 256)

    def tree(j, rep):
        return [rep[0], rep[1], o_in[j], rep[2], rep[3], rep[4], rep[5], o_br[j], o_out[j], rep[6], rep[7], o_up[j], o_down[j],
                rep[8], rep[9]]

    return (loss, grad_x.reshape(1, T, D_MODEL), *tree(0, g_rep), *tree(1, d_rep), *tree(2, nm_rep), *tree(3, nv_rep))
```

```python
import functools
import math

import jax
import jax.numpy as jnp
from jax import lax
from jax.experimental import pallas as pl
from jax.experimental.pallas import tpu as pltpu

F32 = jnp.float32
MXU_DTYPE = jnp.bfloat16
WIRE_DTYPE = jnp.bfloat16

N_DEV = 8
D_MODEL = 1024
DEPTH = 2
CHUNK = 64
ATTN_HEADS = 8
ATTN_HEAD_DIM = 128
ATTN_LEFT_CHUNKS = 8
BAND = (ATTN_LEFT_CHUNKS + 1) * CHUNK
PAD_ROWS = ATTN_LEFT_CHUNKS * CHUNK
REL_CLIP = 2 * CHUNK
N_REL = 2 * REL_CLIP + 1
REL_MIN = REL_CLIP - (CHUNK - 1)
LIN_HEADS = 4
LIN_KEY_DIM = 128
LIN_VAL_DIM = 256
GLA_RANK = 16
GLA_GATE_NORMALIZER = 16.0
ROPE_BASE = 10000.0
D_FF = 4 * D_MODEL
LN_EPS = 1e-5
NEG_INF = -1e30
ALPHA = (2 * DEPTH) ** 0.25
ADAM_LR, ADAM_B1, ADAM_B2, ADAM_EPS, ADAM_WD, ADAM_STEP = 0.001, 0.9, 0.999, 1e-08, 0.01, 10

PROJ_WIDTH = 12304
GLR_SRC = 9216
GATES_SRC = GLR_SRC + GLA_RANK
MAIN_W = 9216
GATES_W = 3 * D_MODEL
GATES_OFF = MAIN_W
GLR_OFF = MAIN_W + GATES_W
PROJ_PAD = 12800
GLR_BLOCK = 128
COL = dict(aq=0, ak=1024, av=2048, rq=3072, rk=3584, rv=4096, rg=5120, gq=6144, gk=6656, gv=7168, gg=8192)

SEG_CHUNKS = 8
SEG = SEG_CHUNKS * CHUNK
VMEM_BIG = 56 * 1024 * 1024

_NN = (((1,), (0,)), ((), ()))
_NT = (((1,), (1,)), ((), ()))
_TN = (((0,), (0,)), ((), ()))


def _mxu(a, b, dn):
    return lax.dot_general(a.astype(MXU_DTYPE), b.astype(MXU_DTYPE), dn, preferred_element_type=F32)


@jax.custom_vjp
def dot_nn(a, b):
    return _mxu(a, b, _NN)


def _dot_nn_fwd(a, b):
    return _mxu(a, b, _NN), (a, b)


def _dot_nn_bwd(res, g):
    a, b = res
    return _mxu(g, b, _NT).astype(a.dtype), _mxu(a, g, _TN).astype(b.dtype)


dot_nn.defvjp(_dot_nn_fwd, _dot_nn_bwd)


@jax.custom_vjp
def dot_nt(a, b):
    return _mxu(a, b, _NT)


def _dot_nt_fwd(a, b):
    return _mxu(a, b, _NT), (a, b)


def _dot_nt_bwd(res, g):
    a, b = res
    return _mxu(g, b, _NN).astype(a.dtype), _mxu(g, a, _TN).astype(b.dtype)


dot_nt.defvjp(_dot_nt_fwd, _dot_nt_bwd)


@jax.custom_vjp
def dot_tn(a, b):
    return _mxu(a, b, _TN)


def _dot_tn_fwd(a, b):
    return _mxu(a, b, _TN), (a, b)


def _dot_tn_bwd(res, g):
    a, b = res
    return _mxu(b, g, _NT).astype(a.dtype), _mxu(a, g, _NN).astype(b.dtype)


dot_tn.defvjp(_dot_tn_fwd, _dot_tn_bwd)


def _tri(n, lower):
    r = lax.broadcasted_iota(jnp.int32, (n, n), 0)
    c = lax.broadcasted_iota(jnp.int32, (n, n), 1)
    return (r >= c) if lower else (r <= c)


def _f32_dot(a, b):
    return lax.dot_general(a, b, _NN, precision=lax.Precision.HIGHEST, preferred_element_type=F32)


@jax.custom_vjp
def chunk_cumsum(x):
    return _f32_dot(_tri(x.shape[0], True).astype(F32), x)


def _chunk_cumsum_fwd(x):
    return chunk_cumsum(x), None


def _chunk_cumsum_bwd(_, g):
    return (_f32_dot(_tri(g.shape[0], False).astype(F32), g),)


chunk_cumsum.defvjp(_chunk_cumsum_fwd, _chunk_cumsum_bwd)


@jax.custom_vjp
def swap_halves(t):
    return pltpu.roll(t, t.shape[-1] // 2, t.ndim - 1)


def _swap_fwd(t):
    return swap_halves(t), None


def _swap_bwd(_, g):
    return (swap_halves(g),)


swap_halves.defvjp(_swap_fwd, _swap_bwd)


def mm(name, a, b, dims, *, tm, tn, tk, out_dtype=F32, extras=(), epi=None, n_out=1, out_dtypes=None, side=None,
       column_slabs=False):
    if dims == "nn":
        (M, K), N = a.shape, b.shape[1]
    elif dims == "nt":
        (M, K), N = a.shape, b.shape[0]
    else:
        (K, M), N = a.shape, b.shape[1]
    tm, tn, tk = min(tm, M), min(tn, N), min(tk, K)
    assert M % tm == 0 and N % tn == 0 and K % tk == 0, (name, M, N, K, tm, tn, tk)
    nk = K // tk
    dn = {"nn": _NN, "nt": _NT, "tn": _TN}[dims]
    a_spec = pl.BlockSpec((tk, tm), lambda i, j, k: (k, i)) if dims == "tn" else pl.BlockSpec((tm, tk), lambda i, j, k: (i, k))
    b_spec = pl.BlockSpec((tn, tk), lambda i, j, k: (j, k)) if dims == "nt" else pl.BlockSpec((tk, tn), lambda i, j, k: (k, j))
    o_spec = pl.BlockSpec((tm, tn), lambda i, j, k: (i, j))
    n_ex = len(extras)
    out_dtypes = out_dtypes or (out_dtype,) * n_out

    def finish(r, ex_refs, o_refs):
        if epi is not None:
            r = epi(r, *[e[...] for e in ex_refs])
        r = r if isinstance(r, tuple) else (r,)
        for o, v in zip(o_refs, r):
            o[...] = v.astype(o.dtype)

    def body(*refs):
        a_ref, b_ref = refs[:2]
        ex_refs = refs[2:2 + n_ex]
        o_refs = refs[2 + n_ex:2 + n_ex + n_out]
        part = _mxu(a_ref[...], b_ref[...], dn)
        if nk == 1:
            finish(part, ex_refs, o_refs)
            return
        acc = refs[-1]
        k = pl.program_id(2)

        @pl.when(k == 0)
        def _():
            acc[...] = part

        @pl.when(k > 0)
        def _():
            acc[...] += part

        @pl.when(k == nk - 1)
        def _():
            finish(acc[...], ex_refs, o_refs)

    outs = _side_call(
        side, body,
        name=name,
        grid=(M // tm, N // tn, nk),
        in_specs=[a_spec, b_spec] + [o_spec] * n_ex,
        out_specs=[pl.BlockSpec((None, tm, tn), lambda i, j, k: (j, i, 0))] * n_out if column_slabs else [o_spec] * n_out,
        out_shape=[jax.ShapeDtypeStruct((N // tn, M, tn) if column_slabs else (M, N), dt) for dt in out_dtypes],
        scratch_shapes=[pltpu.VMEM((tm, tn), F32)] if nk > 1 else [],
        args=[a, b, *extras],
        semantics=("parallel", "parallel", "arbitrary"),
    )
    return outs[0] if n_out == 1 else outs


class Row:
    def __init__(self, arr, width=None, col=0, lead=(), rowmap=None):
        self.arr, self.lead, self.col = arr, tuple(lead), col
        self.width = arr.shape[-1] if width is None else width
        self.rowmap = rowmap

    def spec(self, tm):
        lead, col, rowmap = self.lead, self.col, self.rowmap or (lambda i: i)
        return pl.BlockSpec((None,) * len(lead) + (tm, self.width), lambda i: (*lead, rowmap(i), col))


def row_call(name, fn, rows, params, new_outs, red_shapes=(), *, tm, into=None, into_cols=(), stack=None, stacked_onto=None):
    R = rows[0].arr.shape[len(rows[0].lead)]
    tm = min(tm, R)
    assert R % tm == 0, (name, R, tm)
    n_row, n_par, n_new, n_into, n_red = len(rows), len(params), len(new_outs), len(into_cols), len(red_shapes)
    fresh = isinstance(into, jax.ShapeDtypeStruct)

    def body(*refs):
        row_refs = refs[:n_row]
        par_refs = refs[n_row:n_row + n_par]
        pos = n_row + n_par + (1 if (into is not None and not fresh) else 0) + (n_new if stacked_onto is not None else 0)
        new_refs = refs[pos:pos + n_new]
        red_refs = refs[pos + n_new:pos + n_new + n_red]
        pos2 = pos + n_new + n_red
        into_ref = refs[pos2] if into is not None else None
        stage = refs[pos2 + 1:] if into is not None else ()
        i = pl.program_id(0)
        vals = fn(*[r[...] for r in row_refs], *[p[...] for p in par_refs])
        vals = vals if isinstance(vals, (tuple, list)) else (vals,)
        for o, v in zip(new_refs, vals[:n_new]):
            o[...] = v.astype(o.dtype)
        for (width, off), buf, v in zip(into_cols, stage, vals[n_new:n_new + n_into]):
            buf[...] = v.astype(buf.dtype)
            pltpu.sync_copy(buf, into_ref.at[pl.ds(pl.multiple_of(i * tm, tm), tm), pl.ds(off, width)])
        for o, v in zip(red_refs, vals[n_new + n_into:]):
            @pl.when(i == 0)
            def _(o=o, v=v):
                o[...] = v.astype(o.dtype)

            @pl.when(i > 0)
            def _(o=o, v=v):
                o[...] += v.astype(o.dtype)

    in_specs = [r.spec(tm) for r in rows] + [pl.BlockSpec(p.shape, lambda i, nd=p.ndim: (0,) * nd) for p in params]
    args = [r.arr for r in rows] + list(params)
    if stack is None:
        out_specs = [pl.BlockSpec((tm, w), lambda i: (i, 0)) for w, _ in new_outs]
        out_shape = [jax.ShapeDtypeStruct((R, w), dt) for w, dt in new_outs]
    else:
        index, depth = stack
        out_specs = [pl.BlockSpec((None, tm, w), lambda i: (index, i, 0)) for w, _ in new_outs]
        out_shape = [jax.ShapeDtypeStruct((depth, R, w), dt) for w, dt in new_outs]
    out_specs += [pl.BlockSpec(s, lambda i, nd=len(s): (0,) * nd) for s in red_shapes]
    out_shape += [jax.ShapeDtypeStruct(s, F32) for s in red_shapes]
    aliases, scratch = {}, []
    if into is not None:
        out_specs.append(pl.BlockSpec(memory_space=pl.ANY))
        out_shape.append(jax.ShapeDtypeStruct(into.shape, into.dtype))
        if not fresh:
            in_specs.append(pl.BlockSpec(memory_space=pl.ANY))
            args.append(into)
            aliases = {len(args) - 1: len(out_shape) - 1}
        scratch = [pltpu.VMEM((tm, w), into.dtype) for w, _ in into_cols]
    if stacked_onto is not None:
        for j, prev in enumerate(stacked_onto):
            in_specs.append(pl.BlockSpec(memory_space=pl.ANY))
            args.append(prev)
            aliases[len(args) - 1] = j
    outs = pl.pallas_call(
        body,
        name=name,
        grid=(R // tm,),
        in_specs=in_specs,
        out_specs=out_specs,
        out_shape=out_shape,
        scratch_shapes=scratch,
        input_output_aliases=aliases,
        compiler_params=pltpu.CompilerParams(dimension_semantics=("arbitrary",), vmem_limit_bytes=VMEM_BIG),
    )(*args)
    return list(outs)


def vjp_fn(fn, n_row, n_out, passthrough=0):
    def bwd(*args):
        rows = [a.astype(F32) for a in args[:n_row]]
        cts = args[n_row:n_row + n_out]
        thru = args[n_row + n_out:n_row + n_out + passthrough]
        pars = args[n_row + n_out + passthrough:]
        outs, pull = jax.vjp(lambda *xs: tuple(fn(*xs)), *rows, *pars)
        grads = pull(tuple(c.astype(o.dtype) for c, o in zip(cts, outs)))
        return tuple(grads[:n_row]) + tuple(thru) + tuple(grads[n_row:])

    return bwd


def _ln(x, g, b):
    mu = jnp.mean(x, axis=-1, keepdims=True)
    xc = x - mu
    var = jnp.mean(xc * xc, axis=-1, keepdims=True)
    return xc * lax.rsqrt(var + LN_EPS) * g + b


def f_ln_in(x, g, b):
    return (_ln(x, g, b),)


def f_deepnorm(x, y, g, b):
    return (_ln(ALPHA * x + y, g, b),)


def _twice(fn):
    return lambda *args: fn(*args) * 2


STREAM = [(D_MODEL, F32), (D_MODEL, MXU_DTYPE)]


def _rope(t, cos, sin):
    parts = []
    for h in range(LIN_HEADS):
        th = t[:, h * LIN_KEY_DIM:(h + 1) * LIN_KEY_DIM]
        parts.append(th * cos + swap_halves(th) * sin)
    return jnp.concatenate(parts, axis=1)


def f_ret_pre(rq, rk, cos, sin):
    return _rope(rq, cos, sin), _rope(rk, cos, sin) * LIN_KEY_DIM ** -0.5


def _log_sigmoid(x):
    return jnp.minimum(x, 0.0) - jnp.log(1.0 + jnp.exp(-jnp.abs(x)))


def f_gla_pre(gq, glr, wlr, blr):
    gate_pre = dot_nn(glr, wlr) + blr
    return gq * LIN_KEY_DIM ** -0.5, _log_sigmoid(gate_pre) / GLA_GATE_NORMALIZER


def f_normgate(ret_raw, rg, gla_raw, gg, gnorm):
    rets, glas = [], []
    for h in range(LIN_HEADS):
        sl = slice(h * LIN_VAL_DIM, (h + 1) * LIN_VAL_DIM)
        t = ret_raw[:, sl]
        mu = jnp.mean(t, axis=-1, keepdims=True)
        tc = t - mu
        rets.append(tc * lax.rsqrt(jnp.mean(tc * tc, axis=-1, keepdims=True) + LN_EPS))
        u = gla_raw[:, sl]
        glas.append(u * lax.rsqrt(jnp.mean(u * u, axis=-1, keepdims=True) + LN_EPS) * gnorm)
    ret = jax.nn.silu(rg) * jnp.concatenate(rets, axis=1)
    gla = jax.nn.silu(gg) * jnp.concatenate(glas, axis=1)
    return ret, gla


def f_merge(p0, p1, p2, gl):
    out = jax.nn.sigmoid(gl[:, :D_MODEL]) * p0
    out = out + jax.nn.sigmoid(gl[:, D_MODEL:2 * D_MODEL]) * p1
    return (out + jax.nn.sigmoid(gl[:, 2 * D_MODEL:]) * p2,)


def f_loss(y, tgt):
    diff = y - tgt
    sq = jnp.sum(diff * diff, axis=0, keepdims=True)
    lanes = sq[:, :128]
    for j in range(1, D_MODEL // 128):
        lanes = lanes + sq[:, j * 128:(j + 1) * 128]
    return diff * (1.0 / D_MODEL), lanes * (0.5 / D_MODEL)


def _adam(g, w, m, v):
    m = ADAM_B1 * m + (1.0 - ADAM_B1) * g
    v = ADAM_B2 * v + (1.0 - ADAM_B2) * jnp.square(g)
    m_hat = m / (1.0 - ADAM_B1 ** ADAM_STEP)
    v_hat = v / (1.0 - ADAM_B2 ** ADAM_STEP)
    delta = -ADAM_LR * (m_hat / (jnp.sqrt(v_hat) + ADAM_EPS) + ADAM_WD * w)
    return delta, m, v


def f_sum_adam(*args):
    g = args[0].astype(F32)
    for p in args[1:-3]:
        g = g + p.astype(F32)
    return (g,) + _adam(g, *args[-3:])


def f_adam(g, w, m, v):
    return _adam(g, w, m, v)


GROUP = 4
GROUP_ROWS = GROUP * CHUNK
GROUP_KEYS = (GROUP + ATTN_LEFT_CHUNKS) * CHUNK
DIAG_ROWS = REL_CLIP + CHUNK
GRAD_ROWS = 200


def _blocks_by_delta():
    blocks = {}
    for qi in range(GROUP):
        for J in range(GROUP_KEYS // 128):
            blocks.setdefault(qi - 2 * J, []).append((qi, J))
    return blocks


def bias_table(rel_bias, side=None):
    n_tab = rel_bias.shape[0] * ATTN_HEADS

    def body(rb_ref, o_ref):
        h = pl.program_id(0)
        rm = lax.broadcasted_iota(jnp.int32, (CHUNK, 128), 0)
        cm = lax.broadcasted_iota(jnp.int32, (CHUNK, 128), 1)
        ahead = jnp.where(cm >= CHUNK, 1, 0)
        for delta, where in _blocks_by_delta().items():
            base = 64 * delta + PAD_ROWS
            rel = jnp.minimum(base + rm - cm, REL_CLIP) + REL_CLIP
            in_band = (ahead - delta >= 0) & (ahead - delta <= ATTN_LEFT_CHUNKS)
            lo = max(REL_MIN, min(base - 127, REL_CLIP) + REL_CLIP)
            hi = min(base + CHUNK - 1, REL_CLIP) + REL_CLIP

            def step(r, acc, rel=rel):
                return jnp.where(rel == r, rb_ref[h * N_REL + r], acc)

            blk = jnp.zeros((CHUNK, 128), F32) if hi < lo else lax.fori_loop(lo, hi + 1, step, jnp.zeros((CHUNK, 128), F32))
            blk = jnp.where(in_band, blk, NEG_INF)
            for qi, J in where:
                o_ref[pl.ds(qi * CHUNK, CHUNK), pl.ds(J * 128, 128)] = blk

    return _side_call(
        side, body,
        name="bias_table",
        grid=(n_tab,),
        in_specs=[pl.BlockSpec(memory_space=pltpu.SMEM)],
        out_specs=[pl.BlockSpec((None, GROUP_ROWS, GROUP_KEYS), lambda h: (h, 0, 0))],
        out_shape=[jax.ShapeDtypeStruct((n_tab, GROUP_ROWS, GROUP_KEYS), F32)],
        scratch_shapes=[],
        args=[rel_bias.reshape(n_tab * N_REL)],
    )[0]


def bias_table_grad(dbias):
    blocks = _blocks_by_delta()
    below_clip = range(-(CHUNK - 1), REL_CLIP)
    deltas = [d for d in sorted(blocks) if any(-127 <= pos - 64 * d - PAD_ROWS <= CHUNK - 1 for pos in below_clip)]

    def body(d_ref, o_ref, diag):
        sums = []
        for delta in deltas:
            acc = None
            for qi, J in blocks[delta]:
                blk = d_ref[pl.ds(qi * CHUNK, CHUNK), pl.ds(J * 128, 128)]
                acc = blk if acc is None else acc + blk
            sums.append(acc)
        off = lax.broadcasted_iota(jnp.int32, (CHUNK, 128), 0) - lax.broadcasted_iota(jnp.int32, (CHUNK, 128), 1)
        diag[...] = jnp.zeros_like(diag)

        def step(t, carry):
            pos = t - (CHUNK - 1)
            picked = jnp.zeros((CHUNK, 128), F32)
            for delta, s in zip(deltas, sums):
                picked = picked + jnp.where(off == pos - 64 * delta - PAD_ROWS, s, 0.0)
            diag[pl.ds(t, 1), :] = jnp.sum(picked, axis=0, keepdims=True)
            return carry

        lax.fori_loop(0, DIAG_ROWS - 1, step, 0)
        d = d_ref[...]
        total = jnp.sum(jnp.sum(d, axis=1, keepdims=True), axis=0, keepdims=True)
        o_ref[pl.ds(0, DIAG_ROWS), :] = jnp.broadcast_to(jnp.sum(diag[...], axis=1, keepdims=True), (DIAG_ROWS, 128))
        o_ref[pl.ds(DIAG_ROWS, GRAD_ROWS - DIAG_ROWS), :] = jnp.broadcast_to(total, (GRAD_ROWS - DIAG_ROWS, 128))

    out = pl.pallas_call(
        body,
        name="bias_table_grad",
        grid=(ATTN_HEADS,),
        in_specs=[pl.BlockSpec((None, GROUP_ROWS, GROUP_KEYS), lambda h: (h, 0, 0))],
        out_specs=pl.BlockSpec((None, GRAD_ROWS, 128), lambda h: (h, 0, 0)),
        out_shape=jax.ShapeDtypeStruct((ATTN_HEADS, GRAD_ROWS, 128), F32),
        scratch_shapes=[pltpu.VMEM((DIAG_ROWS, 128), F32)],
        compiler_params=pltpu.CompilerParams(dimension_semantics=("arbitrary",)),
    )(dbias)
    below = out[:, :DIAG_ROWS - 1, 0]
    clipped = out[:, DIAG_ROWS, 0] - jnp.sum(below, axis=1)
    return jnp.concatenate([jnp.zeros((ATTN_HEADS, REL_MIN), F32), below, clipped[:, None]], axis=1)


def _attn_stage(q_ref, k_ref, v_ref, qs, kp, vp, T):
    qs[...] = (q_ref[...] * ATTN_HEAD_DIM ** -0.5).astype(qs.dtype)
    for dst, src in ((kp, k_ref), (vp, v_ref)):
        dst[pl.ds(0, PAD_ROWS), :] = jnp.zeros((PAD_ROWS, ATTN_HEAD_DIM), dst.dtype)
        dst[pl.ds(PAD_ROWS, T), :] = src[...].astype(dst.dtype)


def _attn_probs(g, first_groups, qs, kp, b_ref):
    r0 = g * GROUP_ROWS if first_groups else pl.multiple_of(g * GROUP_ROWS, GROUP_ROWS)
    qc = qs[pl.ds(r0, GROUP_ROWS), :]
    kb = kp[pl.ds(r0, GROUP_KEYS), :]
    s = _mxu(qc, kb, _NT) + b_ref[...]
    if first_groups:
        k_chunk = lax.shift_right_logical(lax.broadcasted_iota(jnp.int32, (GROUP_ROWS, GROUP_KEYS), 1), 6)
        s = jnp.where(g * GROUP + k_chunk >= ATTN_LEFT_CHUNKS, s, NEG_INF)
    e = jnp.exp(s - jnp.max(s, axis=-1, keepdims=True))
    return r0, qc, kb, e / jnp.sum(e, axis=-1, keepdims=True)


def _for_groups(T, step):
    n_first = min(ATTN_LEFT_CHUNKS // GROUP, T // GROUP_ROWS)
    for g in range(n_first):
        step(g, True)
    n_loop = T // GROUP_ROWS - n_first
    lax.fori_loop(n_first, T // GROUP_ROWS, lambda g, c: (step(g, False), c)[1], 0, unroll=2 if n_loop % 2 == 0 else 1)


def _head_spec(T, base):
    return pl.BlockSpec((T, ATTN_HEAD_DIM), lambda h: (0, base + h))


def _attn_scratch(T):
    return [pltpu.VMEM((T, ATTN_HEAD_DIM), MXU_DTYPE)] + [pltpu.VMEM((T + PAD_ROWS, ATTN_HEAD_DIM), MXU_DTYPE)] * 2


def attn_fwd(proj, bias_tab, first_tab, side=None):
    T = proj.shape[0]

    def body(q_ref, k_ref, v_ref, b_ref, o_ref, qs, kp, vp):
        _attn_stage(q_ref, k_ref, v_ref, qs, kp, vp, T)

        def step(g, first):
            r0, _, _, p = _attn_probs(g, first, qs, kp, b_ref)
            o_ref[pl.ds(r0, GROUP_ROWS), :] = _mxu(p, vp[pl.ds(r0, GROUP_KEYS), :], _NN).astype(o_ref.dtype)

        _for_groups(T, step)

    out, = _side_call(
        side, body,
        name="attn_fwd",
        grid=(ATTN_HEADS,),
        in_specs=[_head_spec(T, COL["aq"] // 128), _head_spec(T, COL["ak"] // 128), _head_spec(T, COL["av"] // 128),
                  pl.BlockSpec((None, GROUP_ROWS, GROUP_KEYS), lambda h: (first_tab + h, 0, 0))],
        out_specs=[pl.BlockSpec((T, ATTN_HEAD_DIM), lambda h: (0, h))],
        out_shape=[jax.ShapeDtypeStruct((T, ATTN_HEADS * ATTN_HEAD_DIM), MXU_DTYPE)],
        scratch_shapes=_attn_scratch(T),
        args=[proj, proj, proj, bias_tab],
    )
    return out


def attn_bwd(proj, bias_tab, first_tab, dout, dproj, side=None):
    T = proj.shape[0]

    def body(q_ref, k_ref, v_ref, b_ref, do_ref, dproj_in, dbias_ref, dproj_ref, qs, kp, vp, dkp, dvp, dq, stage):
        del dproj_in
        h = pl.program_id(0)
        _attn_stage(q_ref, k_ref, v_ref, qs, kp, vp, T)
        dkp[...] = jnp.zeros_like(dkp)
        dvp[...] = jnp.zeros_like(dvp)
        dbias_ref[...] = jnp.zeros_like(dbias_ref)

        def step(g, first):
            r0, qc, kb, p = _attn_probs(g, first, qs, kp, b_ref)
            do = do_ref[pl.ds(r0, GROUP_ROWS), :]
            dvp[pl.ds(r0, GROUP_KEYS), :] += _mxu(p, do, _TN)
            dp = _mxu(do, vp[pl.ds(r0, GROUP_KEYS), :], _NT)
            ds = p * (dp - jnp.sum(dp * p, axis=-1, keepdims=True))
            dq[pl.ds(r0, GROUP_ROWS), :] = _mxu(ds, kb, _NN) * ATTN_HEAD_DIM ** -0.5
            dkp[pl.ds(r0, GROUP_KEYS), :] += _mxu(ds, qc, _TN)
            dbias_ref[...] += ds

        _for_groups(T, step)
        col = pl.multiple_of(h * ATTN_HEAD_DIM, ATTN_HEAD_DIM)
        for grad, first_row, base in ((dq, 0, "aq"), (dkp, PAD_ROWS, "ak"), (dvp, PAD_ROWS, "av")):
            stage[...] = grad[pl.ds(first_row, T), :].astype(stage.dtype)
            pltpu.sync_copy(stage, dproj_ref.at[:, pl.ds(COL[base] + col, ATTN_HEAD_DIM)])

    dbias, dproj = _side_call(
        side, body,
        name="attn_bwd",
        grid=(ATTN_HEADS,),
        in_specs=[_head_spec(T, COL["aq"] // 128), _head_spec(T, COL["ak"] // 128), _head_spec(T, COL["av"] // 128),
                  pl.BlockSpec((None, GROUP_ROWS, GROUP_KEYS), lambda h: (first_tab + h, 0, 0)),
                  pl.BlockSpec((T, ATTN_HEAD_DIM), lambda h: (0, h)),
                  pl.BlockSpec(memory_space=pl.ANY)],
        out_specs=[pl.BlockSpec((None, GROUP_ROWS, GROUP_KEYS), lambda h: (h, 0, 0)), pl.BlockSpec(memory_space=pl.ANY)],
        out_shape=[jax.ShapeDtypeStruct((ATTN_HEADS, GROUP_ROWS, GROUP_KEYS), F32),
                   jax.ShapeDtypeStruct(dproj.shape, dproj.dtype)],
        scratch_shapes=_attn_scratch(T) + [pltpu.VMEM((T + PAD_ROWS, ATTN_HEAD_DIM), F32)] * 2
        + [pltpu.VMEM((T, ATTN_HEAD_DIM), F32), pltpu.VMEM((T, ATTN_HEAD_DIM), dproj.dtype)],
        args=[proj, proj, proj, bias_tab, dout, dproj],
        aliases={5: 1},
    )
    return dproj, dbias


def _lin_segment(q, k, v, la, st):
    lower = _tri(CHUNK, True)
    outs = []
    for i in range(SEG_CHUNKS):
        sl = slice(i * CHUNK, (i + 1) * CHUNK)
        qi, ki, vi, lai = q[sl], k[sl], v[sl], la[sl]
        cum = chunk_cumsum(lai)
        last = jnp.sum(lai, axis=0, keepdims=True)
        e_pos, e_neg = jnp.exp(cum), jnp.exp(-cum)
        q_fwd = qi * e_pos
        s_fwd = dot_nt(q_fwd, ki * e_neg)
        s_bwd = dot_nt(qi * e_neg, ki * e_pos)
        o = dot_nn(jnp.where(lower, s_fwd, s_bwd), vi) + dot_nt(q_fwd, st)
        st = jnp.exp(last) * st + dot_tn(vi, ki * jnp.exp(last - cum))
        outs.append(o)
    return jnp.concatenate(outs, axis=0), st


class Win:
    def __init__(self, arr, width, base=0):
        self.arr, self.width, self.base = arr, width, base


HEADS_PER_STEP = 2


def _lin_spec(w, seg_of):
    assert w.base % HEADS_PER_STEP == 0
    first, width = w.base // HEADS_PER_STEP, HEADS_PER_STEP * w.width
    if w.arr.shape[0] == 1:
        return pl.BlockSpec((1, width), lambda p, s: (0, first + p))
    return pl.BlockSpec((SEG, width), lambda p, s: (seg_of(s), first + p))


def _head_cols(ref, j, width):
    return ref[:, j * width:(j + 1) * width]


def _decay_rows(la_ref, j):
    la = _head_cols(la_ref, j, LIN_KEY_DIM)
    return jnp.broadcast_to(la, (SEG, LIN_KEY_DIM)) if la.shape[0] == 1 else la


_STATE_BLOCK = (HEADS_PER_STEP, None, LIN_VAL_DIM, LIN_KEY_DIM)


def lin_fwd(name, q, k, v, la, side=None):
    T = q.arr.shape[0]
    nseg = T // SEG

    def body(q_ref, k_ref, v_ref, la_ref, o_ref, st_ref, st):
        @pl.when(pl.program_id(1) == 0)
        def _():
            st[...] = jnp.zeros_like(st)

        for j in range(HEADS_PER_STEP):
            st0 = st[j]
            st_ref[j] = st0
            o, st1 = _lin_segment(_head_cols(q_ref, j, LIN_KEY_DIM), _head_cols(k_ref, j, LIN_KEY_DIM),
                                  _head_cols(v_ref, j, LIN_VAL_DIM), _decay_rows(la_ref, j), st0)
            o_ref[:, j * LIN_VAL_DIM:(j + 1) * LIN_VAL_DIM] = o
            st[j] = st1

    fwd = lambda s: s
    out, states = _side_call(
        side, body,
        name=name,
        grid=(LIN_HEADS // HEADS_PER_STEP, nseg),
        in_specs=[_lin_spec(w, fwd) for w in (q, k, v, la)],
        out_specs=[pl.BlockSpec((SEG, HEADS_PER_STEP * LIN_VAL_DIM), lambda p, s: (s, p)),
                   pl.BlockSpec(_STATE_BLOCK, lambda p, s: (p, s, 0, 0))],
        out_shape=[jax.ShapeDtypeStruct((T, LIN_HEADS * LIN_VAL_DIM), F32),
                   jax.ShapeDtypeStruct((LIN_HEADS, nseg, LIN_VAL_DIM, LIN_KEY_DIM), F32)],
        scratch_shapes=[pltpu.VMEM((HEADS_PER_STEP, LIN_VAL_DIM, LIN_KEY_DIM), F32)],
        args=[q.arr, k.arr, v.arr, la.arr],
    )
    return out, states


def lin_bwd(name, q, k, v, la, states, dout, with_decay_grad, side=None):
    T = q.arr.shape[0]
    nseg = T // SEG
    n_out = 4 if with_decay_grad else 3

    widths = [LIN_KEY_DIM, LIN_KEY_DIM, LIN_VAL_DIM] + ([LIN_KEY_DIM] if with_decay_grad else [])

    def body(q_ref, k_ref, v_ref, la_ref, st_ref, do_ref, *rest):
        outs, dst = rest[:n_out], rest[n_out]

        @pl.when(pl.program_id(1) == 0)
        def _():
            dst[...] = jnp.zeros_like(dst)

        for j in range(HEADS_PER_STEP):
            _, pull = jax.vjp(_lin_segment, _head_cols(q_ref, j, LIN_KEY_DIM).astype(F32),
                              _head_cols(k_ref, j, LIN_KEY_DIM).astype(F32), _head_cols(v_ref, j, LIN_VAL_DIM).astype(F32),
                              _decay_rows(la_ref, j), st_ref[j])
            grads = pull((_head_cols(do_ref, j, LIN_VAL_DIM), dst[j]))
            for o, g, w in zip(outs, grads[:n_out], widths):
                o[:, j * w:(j + 1) * w] = g
            dst[j] = grads[4]

    rev = lambda s: nseg - 1 - s
    key_spec = pl.BlockSpec((SEG, HEADS_PER_STEP * LIN_KEY_DIM), lambda p, s: (rev(s), p))
    val_spec = pl.BlockSpec((SEG, HEADS_PER_STEP * LIN_VAL_DIM), lambda p, s: (rev(s), p))
    out_specs = [key_spec, key_spec, val_spec] + ([key_spec] if with_decay_grad else [])
    return _side_call(
        side, body,
        name=name,
        grid=(LIN_HEADS // HEADS_PER_STEP, nseg),
        in_specs=[_lin_spec(w, rev) for w in (q, k, v, la)] + [
            pl.BlockSpec(_STATE_BLOCK, lambda p, s: (p, rev(s), 0, 0)), val_spec],
        out_specs=out_specs,
        out_shape=[jax.ShapeDtypeStruct((T, LIN_HEADS * w), F32) for w in widths],
        scratch_shapes=[pltpu.VMEM((HEADS_PER_STEP, LIN_VAL_DIM, LIN_KEY_DIM), F32)],
        args=[q.arr, k.arr, v.arr, la.arr, states, dout],
    )


def _coords():
    return lax.axis_index("x"), lax.axis_index("y"), lax.axis_index("c")


def _peer(k):
    x, y, c = _coords()
    px = 1 - x if k & 4 else x
    py = 1 - y if k & 2 else y
    pc = 1 - c if k & 1 else c
    return (px, py, pc), 4 * px + 2 * py + pc


def _remote(src, dst, send_sem, recv_sem, device):
    return pltpu.make_async_remote_copy(src_ref=src, dst_ref=dst, send_sem=send_sem, recv_sem=recv_sem,
                                        device_id=device, device_id_type=pl.DeviceIdType.MESH)


class Side:
    def __init__(self, kind, arrays):
        self.kind, self.arrays, self.n = kind, list(arrays), len(arrays)
        self.outs_per = 1
        self.slots = {"gather": N_DEV - 1, "pair": 4, "chips": 3}[kind]

    def out_shape(self):
        if self.kind == "gather":
            return [jax.ShapeDtypeStruct((N_DEV,) + a.shape, a.dtype) for a in self.arrays]
        return [jax.ShapeDtypeStruct((4,) + a.shape[1:], a.dtype) for a in self.arrays]

    def sems(self):
        return [pltpu.SemaphoreType.DMA((self.n, self.slots)), pltpu.SemaphoreType.DMA((self.n, self.slots)),
                pltpu.SemaphoreType.DMA((self.n,))]

    def start(self, ins, outs, sems):
        send, recv, own = sems
        x, y, c = _coords()
        me = 4 * x + 2 * y + c
        for a in range(self.n):
            if self.kind == "gather":
                pltpu.make_async_copy(ins[a], outs[a].at[me], own.at[a]).start()
                for k in (1, 2, 4, 6):
                    peer, _ = _peer(k)
                    _remote(ins[a], outs[a].at[me], send.at[a, k - 1], recv.at[a, k - 1], peer).start()
            elif self.kind == "pair":
                for j in range(4):
                    _remote(ins[a].at[2 * j + 1 - c], outs[a].at[j], send.at[a, j], recv.at[a, j], (x, y, 1 - c)).start()
            else:
                chip = 2 * x + y
                pltpu.make_async_copy(ins[a].at[chip], outs[a].at[chip], own.at[a]).start()
                for d in (1, 2, 3):
                    tx, ty = (1 - x if d & 2 else x), (1 - y if d & 1 else y)
                    _remote(ins[a].at[2 * tx + ty], outs[a].at[chip], send.at[a, d - 1], recv.at[a, d - 1], (tx, ty, c)).start()

    def finish(self, ins, outs, sems):
        send, recv, own = sems
        x, y, c = _coords()
        me = 4 * x + 2 * y + c
        sibling = (x, y, 1 - c)
        for a in range(self.n):
            if self.kind == "gather":
                blk = outs[a].at[me]
                for k in (2, 4, 6):
                    _, pidx = _peer(k)
                    _remote(blk, blk, send.at[a, k - 1], recv.at[a, k - 1], sibling).wait_recv()
                    _remote(outs[a].at[pidx], outs[a].at[pidx], send.at[a, k], recv.at[a, k], sibling).start()
                pltpu.make_async_copy(ins[a], blk, own.at[a]).wait()
                for k in range(1, N_DEV):
                    _remote(blk, blk, send.at[a, k - 1], recv.at[a, k - 1], sibling).wait_send()
                for k in (1, 3, 5, 7):
                    _remote(blk, blk, send.at[a, k - 1], recv.at[a, k - 1], sibling).wait_recv()
            elif self.kind == "pair":
                for j in range(4):
                    _remote(ins[a].at[j], outs[a].at[j], send.at[a, j], recv.at[a, j], sibling).wait()
            else:
                slab = outs[a].at[0]
                pltpu.make_async_copy(slab, slab, own.at[a]).wait()
                for d in (1, 2, 3):
                    _remote(slab, slab, send.at[a, d - 1], recv.at[a, d - 1], sibling).wait()


def _side_call(side, body, *, name, grid, in_specs, out_specs, out_shape, scratch_shapes, args, semantics=None,
               aliases=None):
    in_specs, out_specs, out_shape = list(in_specs), list(out_specs), list(out_shape)
    scratch_shapes, args = list(scratch_shapes), list(args)
    n_in, n_out = len(in_specs), len(out_specs)
    kernel_body = body
    if side is not None:
        n_so = side.n * side.outs_per

        def kernel_body(*refs):
            ins, side_ins = refs[:n_in], refs[n_in:n_in + side.n]
            pos = n_in + side.n
            outs, side_outs = refs[pos:pos + n_out], refs[pos + n_out:pos + n_out + n_so]
            rest = refs[pos + n_out + n_so:]
            scratch, sems = rest[:-3], rest[-3:]
            ids = [pl.program_id(ax) for ax in range(len(grid))]
            first = functools.reduce(jnp.logical_and, [i == 0 for i in ids])
            last = functools.reduce(jnp.logical_and, [i == g - 1 for i, g in zip(ids, grid)])

            @pl.when(first)
            def _():
                side.start(side_ins, side_outs, sems)

            body(*ins, *outs, *scratch)

            @pl.when(last)
            def _():
                side.finish(side_ins, side_outs, sems)

        any_spec = pl.BlockSpec(memory_space=pl.ANY)
        in_specs += [any_spec] * side.n
        args += side.arrays
        out_specs += [any_spec] * n_so
        out_shape += side.out_shape()
        scratch_shapes += side.sems()
        semantics = ("arbitrary",) * len(grid)
    outs = pl.pallas_call(
        kernel_body,
        name=name,
        grid=grid,
        in_specs=in_specs,
        out_specs=out_specs,
        out_shape=out_shape,
        scratch_shapes=scratch_shapes,
        input_output_aliases=aliases or {},
        compiler_params=pltpu.CompilerParams(dimension_semantics=semantics or ("arbitrary",) * len(grid),
                                             vmem_limit_bytes=VMEM_BIG),
    )(*args)
    if side is not None:
        side.results = list(outs[n_out:])
    return list(outs[:n_out])


def carrier(name, side):
    _side_call(side, lambda: None, name=name, grid=(1,), in_specs=[], out_specs=[], out_shape=[], scratch_shapes=[], args=[])
    return side.results


def all_reduce_small(vec):
    def body(v_ref, o_ref, land, send_sems, recv_sems):
        x, y, c = _coords()
        me = 4 * x + 2 * y + c
        land[me] = v_ref[...]
        copies = []
        for k in range(1, N_DEV):
            peer, _ = _peer(k)
            cp = pltpu.make_async_remote_copy(
                src_ref=v_ref, dst_ref=land.at[me], send_sem=send_sems.at[k - 1], recv_sem=recv_sems.at[k - 1],
                device_id=peer, device_id_type=pl.DeviceIdType.MESH)
            cp.start()
            copies.append(cp)
        for cp in copies:
            cp.wait()
        total = land[0]
        for s in range(1, N_DEV):
            total = total + land[s]
        o_ref[...] = total

    return pl.pallas_call(
        body,
        name="all_reduce_small",
        in_specs=[pl.BlockSpec(memory_space=pltpu.VMEM)],
        out_specs=pl.BlockSpec(memory_space=pltpu.VMEM),
        out_shape=jax.ShapeDtypeStruct(vec.shape, F32),
        scratch_shapes=[pltpu.VMEM((N_DEV,) + vec.shape, F32), pltpu.SemaphoreType.DMA((N_DEV - 1,)),
                        pltpu.SemaphoreType.DMA((N_DEV - 1,))],
    )(vec)


def _pack(parts):
    flat = jnp.concatenate([p.reshape(-1).astype(F32) for p in parts])
    n = flat.shape[0]
    rows = -(-n // 1024) * 8
    return jnp.pad(flat, (0, rows * 128 - n)).reshape(rows, 128)


def _unpack(packed, shapes):
    flat, out, pos = packed.reshape(-1), [], 0
    for s in shapes:
        n = math.prod(s)
        out.append(flat[pos:pos + n].reshape(s))
        pos += n
    return out


def _row2(v):
    return v.reshape(1, -1)


def _rider(hooks, name, ctx=None):
    if name not in hooks:
        return None, lambda: None
    side, done = hooks[name](ctx)
    return side, lambda: done(side.results)


def _layer_forward(h, hb, W, consts, hooks):
    T = h.shape[0]
    cos, sin, ret_decay = consts
    side, landed = _rider(hooks, "proj")
    proj = mm("proj", hb, W["in"], "nn", tm=4096, tn=512, tk=1024, side=side)
    landed()
    bias_tab, first_tab = W["bias_tab"]
    side, landed = _rider(hooks, "attn_fwd")
    attn = attn_fwd(proj, bias_tab, first_tab, side=side)
    landed()
    rq2, rk2 = row_call("ret_pre", f_ret_pre, [Row(proj, 512, COL["rq"] // 512), Row(proj, 512, COL["rk"] // 512),
                                              Row(cos), Row(sin)], [], [(512, F32), (512, F32)], tm=512)
    side, landed = _rider(hooks, "ret_fwd")
    ret_raw, ret_st = lin_fwd("ret_fwd", Win(rq2, 128), Win(rk2, 128), Win(proj, 256, COL["rv"] // 256), Win(ret_decay, 128),
                              side=side)
    landed()
    gq2, gla_la = row_call("gla_pre", f_gla_pre, [Row(proj, 512, COL["gq"] // 512), Row(proj, GLR_BLOCK, GLR_OFF // GLR_BLOCK)],
                           [W["wlr"], W["blr"]], [(512, F32), (512, F32)], tm=512)
    side, landed = _rider(hooks, "gla_fwd")
    gla_raw, gla_st = lin_fwd("gla_fwd", Win(gq2, 128), Win(proj, 128, COL["gk"] // 128), Win(proj, 256, COL["gv"] // 256),
                              Win(gla_la, 128), side=side)
    landed()
    ret_b, gla_b = row_call("normgate", f_normgate, [Row(ret_raw), Row(proj, 1024, COL["rg"] // 1024), Row(gla_raw),
                                                    Row(proj, 1024, COL["gg"] // 1024)], [W["gnorm"]],
                            [(1024, MXU_DTYPE), (1024, MXU_DTYPE)], tm=256)
    branches = (attn, ret_b, gla_b)
    p = [mm("branch_proj", branches[n], W["branch"][n], "nn", tm=1024, tn=1024, tk=1024) for n in range(3)]
    merged, = row_call("merge", f_merge, [Row(p[0]), Row(p[1]), Row(p[2]), Row(proj, GATES_W, GATES_OFF // GATES_W)], [],
                       [(1024, MXU_DTYPE)], tm=256)
    mix = mm("out_proj", merged, W["out"], "nn", tm=1024, tn=1024, tk=1024)
    x1, x1b = row_call("ln1", _twice(f_deepnorm), [Row(h), Row(mix)], [W["ln1_g"], W["ln1_b"]], STREAM, tm=512)
    side, landed = _rider(hooks, "mlp_up")
    u, act = mm("mlp_up", x1b, W["up"], "nn", tm=2048, tn=512, tk=1024, n_out=2, out_dtypes=(F32, MXU_DTYPE),
                epi=lambda r: (r, jnp.square(jnp.maximum(r, 0.0))), side=side)
    landed()
    y = mm("mlp_down", act, W["down"], "nn", tm=1024, tn=1024, tk=2048)
    x2, x2b = row_call("ln2", _twice(f_deepnorm), [Row(x1), Row(y)], [W["ln2_g"], W["ln2_b"]], STREAM, tm=512)
    saved = dict(h=h, hb=hb, proj=proj, rq2=rq2, rk2=rk2, ret_st=ret_st, gq2=gq2, gla_la=gla_la,
                 gla_st=gla_st, ret_raw=ret_raw, gla_raw=gla_raw, branches=branches, p=p, merged=merged, mix=mix, x1=x1, x1b=x1b,
                 u=u, act=act, y=y)
    return x2, x2b, saved


def _layer_backward(dx2, S, W, consts, hooks):
    cos, sin, ret_decay = consts
    h, proj, T = S["h"], S["proj"], S["h"].shape[0]
    add = lambda r, e: r + e
    dx1a, dy, dg2, db2 = row_call("ln2_bwd", vjp_fn(f_deepnorm, 2, 1), [Row(S["x1"]), Row(S["y"]), Row(dx2)],
                                  [W["ln2_g"], W["ln2_b"]], [(1024, F32), (1024, MXU_DTYPE)], [(1, 1024), (1, 1024)], tm=256)
    du = mm("mlp_down_bwd", dy, W["down"], "nt", tm=1024, tn=512, tk=1024, extras=(S["u"],),
            epi=lambda r, u: r * (2.0 * jnp.maximum(u, 0.0)), out_dtype=MXU_DTYPE)
    g_down = mm("mlp_down_wgrad", S["act"], dy, "tn", tm=1024, tn=1024, tk=2048, out_dtype=WIRE_DTYPE)
    dx1 = mm("mlp_up_bwd", du, W["up"], "nt", tm=1024, tn=1024, tk=2048, extras=(dx1a,), epi=add)
    g_up = mm("mlp_up_wgrad", jnp.transpose(S["x1b"]), du, "nn", tm=1024, tn=D_FF // N_DEV, tk=2048, out_dtype=WIRE_DTYPE,
              column_slabs=True)
    dha, dmix, dg1, db1 = row_call("ln1_bwd", vjp_fn(f_deepnorm, 2, 1), [Row(h), Row(S["mix"]), Row(dx1)],
                                   [W["ln1_g"], W["ln1_b"]], [(1024, F32), (1024, MXU_DTYPE)], [(1, 1024), (1, 1024)], tm=256)
    dmerged = mm("out_proj_bwd", dmix, W["out"], "nt", tm=1024, tn=1024, tk=1024)
    g_out = mm("out_proj_wgrad", S["merged"], dmix, "tn", tm=1024, tn=1024, tk=2048, out_dtype=WIRE_DTYPE)
    dproj = jax.ShapeDtypeStruct((T, PROJ_PAD), MXU_DTYPE)
    p = S["p"]
    dp0, dp1, dp2, dproj = row_call(
        "merge_bwd", vjp_fn(f_merge, 4, 1), [Row(p[0]), Row(p[1]), Row(p[2]), Row(proj, GATES_W, GATES_OFF // GATES_W), Row(dmerged)],
        [], [(1024, MXU_DTYPE)] * 3, tm=256, into=dproj, into_cols=[(GATES_W, GATES_OFF)])
    dps = (dp0, dp1, dp2)
    dbr = [mm("branch_proj_bwd", dps[n], W["branch"][n], "nt", tm=1024, tn=1024, tk=1024,
              out_dtype=MXU_DTYPE if n == 0 else F32) for n in range(3)]
    g_branch = [mm("branch_proj_wgrad", S["branches"][n], dps[n], "tn", tm=1024, tn=1024, tk=2048, out_dtype=WIRE_DTYPE)
                for n in range(3)]
    dret_raw, dgla_raw, dgnorm, dproj = row_call(
        "normgate_bwd", vjp_fn(lambda a, b, c, d, g: _reorder_normgate(a, b, c, d, g), 4, 2),
        [Row(S["ret_raw"]), Row(S["gla_raw"]), Row(proj, 1024, COL["rg"] // 1024), Row(proj, 1024, COL["gg"] // 1024),
         Row(dbr[1]), Row(dbr[2])], [W["gnorm"]], [(1024, F32), (1024, F32)], [(1, LIN_VAL_DIM)], tm=256,
        into=dproj, into_cols=[(1024, COL["rg"]), (1024, COL["gg"])])
    side, landed = _rider(hooks, "gla_bwd")
    dgq2, dgk, dgv, dgla_la = lin_bwd("gla_bwd", Win(S["gq2"], 128), Win(proj, 128, COL["gk"] // 128),
                                      Win(proj, 256, COL["gv"] // 256), Win(S["gla_la"], 128), S["gla_st"], dgla_raw, True,
                                      side=side)
    landed()
    dwlr, dblr, dproj = row_call(
        "gla_pre_bwd", _gla_pre_bwd, [Row(proj, 512, COL["gq"] // 512), Row(proj, GLR_BLOCK, GLR_OFF // GLR_BLOCK),
                                      Row(dgq2), Row(dgla_la), Row(dgk), Row(dgv)], [W["wlr"], W["blr"]], [],
        [(GLR_BLOCK, 512), (1, 512)], tm=256, into=dproj,
        into_cols=[(512, COL["gq"]), (PROJ_PAD - GLR_OFF, GLR_OFF), (512, COL["gk"]), (1024, COL["gv"])])
    drq2, drk2, drv = lin_bwd("ret_bwd", Win(S["rq2"], 128), Win(S["rk2"], 128), Win(proj, 256, COL["rv"] // 256),
                              Win(ret_decay, 128), S["ret_st"], dret_raw, False)
    dproj, = row_call(
        "ret_pre_bwd", _ret_pre_bwd, [Row(proj, 512, COL["rq"] // 512), Row(proj, 512, COL["rk"] // 512), Row(cos), Row(sin),
                                      Row(drq2), Row(drk2), Row(drv)], [], [], tm=256, into=dproj,
        into_cols=[(512, COL["rq"]), (512, COL["rk"]), (1024, COL["rv"])])
    side, landed = _rider(hooks, "attn_bwd")
    dproj, dbias = attn_bwd(proj, *W["bias_tab"], dbr[0], dproj, side=side)
    landed()
    d_rel_bias = bias_table_grad(dbias)
    side, landed = _rider(hooks, "proj_wgrad", dict(branch=g_branch, out=g_out, up=g_up, down=g_down))
    g_in = mm("proj_wgrad", jnp.transpose(S["hb"]), dproj, "nn", tm=1024, tn=1280, tk=2048, out_dtype=WIRE_DTYPE, side=side)
    landed()
    side, landed = _rider(hooks, "proj_bwd", {"in": g_in})
    dh = mm("proj_bwd", dproj, W["in"], "nt", tm=1024, tn=1024, tk=2560, extras=(dha,), epi=add, side=side)
    landed()
    big = dict(g_in=g_in, branch=g_branch, out=g_out, up=g_up, down=g_down)
    small = dict(rel_bias=d_rel_bias, wlr=dwlr[:GLA_RANK], blr=dblr, gnorm=dgnorm, ln1_g=dg1, ln1_b=db1, ln2_g=dg2, ln2_b=db2)
    return dh, big, small


def _reorder_normgate(ret_raw, gla_raw, rg, gg, gnorm):
    return f_normgate(ret_raw, rg, gla_raw, gg, gnorm)


def _gla_pre_bwd(gq, glr, dq2, dla, dgk, dgv, wlr, blr):
    _, pull = jax.vjp(f_gla_pre, gq, glr, wlr, blr)
    dgq, dglr, dwlr, dblr = pull((dq2, dla))
    dglr = jnp.concatenate([dglr, jnp.zeros((dglr.shape[0], PROJ_PAD - GLR_OFF - GLR_BLOCK), F32)], axis=1)
    return dgq, dglr, dgk, dgv, dwlr, dblr


def _ret_pre_bwd(rq, rk, cos, sin, drq2, drk2, drv):
    _, pull = jax.vjp(lambda a, b: f_ret_pre(a, b, cos, sin), rq, rk)
    drq, drk = pull((drq2, drk2))
    return drq, drk, drv


SHARDED = ("in", "branch", "out", "up", "down")

_FULL = {
    "branch": lambda g: [g[:, n].reshape(D_MODEL, D_MODEL) for n in range(3)],
    "out": lambda g: g.reshape(D_MODEL, D_MODEL),
    "up": lambda g: jnp.transpose(g, (1, 0, 2)).reshape(D_MODEL, D_FF),
    "down": lambda g: g.reshape(D_FF, D_MODEL),
}


def _owner_pieces(s):
    shard = PROJ_WIDTH // N_DEV
    lo, hi = s * shard, (s + 1) * shard
    pieces = []
    for a, b, padded in ((0, GLR_SRC, 0), (GLR_SRC, GATES_SRC, GLR_OFF), (GATES_SRC, PROJ_WIDTH, GATES_OFF)):
        x0, x1 = max(lo, a), min(hi, b)
        if x0 < x1:
            pieces.append((x0 - lo, x1 - x0, padded + x0 - a))
    return pieces


def _full_in(g):
    placed = sorted((p, s, c, w) for s in range(N_DEV) for c, w, p in _owner_pieces(s))
    cols = [g[s][:, c:c + w] for _, s, c, w in placed]
    return jnp.concatenate(cols + [jnp.zeros((D_MODEL, PROJ_PAD - PROJ_WIDTH), g.dtype)], axis=1)


def _full_wlr(g):
    wlr = jnp.transpose(g, (1, 0, 2)).reshape(GLA_RANK, LIN_HEADS * LIN_KEY_DIM)
    return jnp.pad(wlr, ((0, GLR_BLOCK - GLA_RANK), (0, 0)))


def _slab_in(g):
    return jnp.stack([jnp.concatenate([g[:, p:p + w] for _, w, p in _owner_pieces(s)], axis=1) for s in range(N_DEV)])


_SLABS = {
    "in": _slab_in,
    "branch": lambda gs: jnp.stack([b.reshape(N_DEV, D_MODEL // N_DEV, D_MODEL) for b in gs], axis=1),
    "out": lambda g: g.reshape(N_DEV, D_MODEL // N_DEV, D_MODEL),
    "up": lambda g: g,
    "down": lambda g: g.reshape(N_DEV, D_FF // N_DEV, D_MODEL),
}


def _pair_sums(name, slabs):
    sums = []
    for mine, sibs in zip(slabs, carrier(name, Side("pair", slabs))):
        width = mine.shape[-1]
        rows = math.prod(mine.shape[1:-1])
        tm = 256 if rows % 256 == 0 else 128
        nb = rows // tm
        my_rows = Row(mine.reshape(-1, width), rowmap=lambda i, nb=nb: (2 * (i // nb) + lax.axis_index("c")) * nb + i % nb)
        out, = row_call("pair_sum", lambda q, p: (p.astype(F32) + q.astype(F32),),
                        [Row(sibs.reshape(-1, width)), my_rows], [], [(width, WIRE_DTYPE)], tm=tm)
        sums.append(out.reshape(sibs.shape))
    return sums


def _sum_adam(name, parts, w, m, v, layer, tm, earlier):
    C = w.shape[-1]
    rows = [Row(parts, lead=(s,)) for s in range(parts.shape[0])] + [Row(a, lead=(layer,)) for a in (w, m, v)]
    return row_call(name, f_sum_adam, rows, [], [(C, F32)] * 4, tm=tm, stack=(layer, DEPTH), stacked_onto=earlier)


def kernel(x, ln_in_g, ln_in_b, w_in, rel_bias, gla_w_lr, gla_b_lr, gla_norm_g, w_branch, w_out, ln1_g, ln1_b, w_up, w_down, ln2_g, ln2_b, loss_target, m_ln_in_g, m_ln_in_b, m_w_in, m_rel_bias, m_gla_w_lr, m_gla_b_lr, m_gla_norm_g, m_w_branch, m_w_out, m_ln1_g, m_ln1_b, m_w_up, m_w_down, m_ln2_g, m_ln2_b, v_ln_in_g, v_ln_in_b, v_w_in, v_rel_bias, v_gla_w_lr, v_gla_b_lr, v_gla_norm_g, v_w_branch, v_w_out, v_ln1_g, v_ln1_b, v_w_up, v_w_down, v_ln2_g, v_ln2_b):
    T = x.shape[1]
    x2d = x.reshape(T, D_MODEL)
    tgt = loss_target.reshape(T, D_MODEL)
    me = 4 * lax.axis_index("x") + 2 * lax.axis_index("y") + lax.axis_index("c")

    inv = ROPE_BASE ** (-jnp.arange(0, LIN_KEY_DIM, 2, dtype=F32) / LIN_KEY_DIM)
    ang = jnp.arange(T, dtype=F32)[:, None] * inv[None, :]
    cos = jnp.concatenate([jnp.cos(ang), jnp.cos(ang)], axis=1)
    sin = jnp.concatenate([-jnp.sin(ang), jnp.sin(ang)], axis=1)
    log_gamma = jnp.log1p(-jnp.exp2(-5.0 - jnp.arange(LIN_HEADS, dtype=F32)))
    ret_decay = jnp.repeat(log_gamma, LIN_KEY_DIM).reshape(1, LIN_HEADS * LIN_KEY_DIM)
    consts = (cos, sin, ret_decay)

    layers = [dict(rel_bias=rel_bias[l], blr=_row2(gla_b_lr[l]), gnorm=_row2(gla_norm_g[l]), ln1_g=_row2(ln1_g[l]),
                   ln1_b=_row2(ln1_b[l]), ln2_g=_row2(ln2_g[l]), ln2_b=_row2(ln2_b[l])) for l in range(DEPTH)]
    shards = dict(branch=w_branch, out=w_out, up=w_up, down=w_down)
    shards["in"] = w_in
    wire = [{n: shards[n][l].astype(WIRE_DTYPE) for n in SHARDED} for l in range(DEPTH)]

    def gather_of(l, names):
        def landed(res):
            for n, g in zip(names, res):
                layers[l][n] = _full_in(g) if n == "in" else _FULL[n](g)
        return lambda ctx: (Side("gather", [wire[l][n] for n in names]), landed)

    first = Side("gather", [wire[0]["in"]] + [gla_w_lr[l] for l in range(DEPTH)])
    bias_tabs = bias_table(rel_bias, side=first)
    layers[0]["in"] = _full_in(first.results[0])
    for l in range(DEPTH):
        layers[l]["wlr"] = _full_wlr(first.results[1 + l])
        layers[l]["bias_tab"] = (bias_tabs, l * ATTN_HEADS)

    fwd_hooks = [{"proj": gather_of(1, ("in",)), "attn_fwd": gather_of(0, ("up", "down")),
                  "ret_fwd": gather_of(0, ("branch", "out"))},
                 {"proj": gather_of(1, ("up", "down")), "attn_fwd": gather_of(1, ("branch", "out"))}]
    h, hb = row_call("ln_in", _twice(f_ln_in), [Row(x2d)], [_row2(ln_in_g), _row2(ln_in_b)], STREAM, tm=512)
    saved = []
    for l in range(DEPTH):
        h, hb, s = _layer_forward(h, hb, layers[l], consts, fwd_hooks[l])
        saved.append(s)
    dh, loss_lanes = row_call("loss", f_loss, [Row(h), Row(tgt)], [], [(D_MODEL, F32)], [(1, 128)], tm=512)

    received = [dict() for _ in range(DEPTH)]

    def scatter_of(l, names, sums=None):
        def make(ctx):
            parts = sums if sums is not None else _pair_sums("pair_grads", [_SLABS[n](ctx[n]) for n in names])

            def landed(res):
                received[l].update(zip(names, res))
            return Side("chips", parts), landed
        return make

    small = [None] * DEPTH
    dh, big, small[1] = _layer_backward(dh, saved[1], layers[1], consts, {})
    sums1 = _pair_sums("pair_grads", [_SLABS[n](big["g_in"] if n == "in" else big[n]) for n in SHARDED])
    rest = SHARDED[1:]
    dh, big, small[0] = _layer_backward(dh, saved[0], layers[0], consts, {
        "gla_bwd": scatter_of(1, ("in",), sums1[:1]), "attn_bwd": scatter_of(1, rest, sums1[1:]),
        "proj_wgrad": scatter_of(0, rest), "proj_bwd": scatter_of(0, ("in",))})
    grad_x, d_in_g, d_in_b = row_call("ln_in_bwd", vjp_fn(f_ln_in, 1, 1), [Row(x2d), Row(dh)],
                                      [_row2(ln_in_g), _row2(ln_in_b)], [(D_MODEL, F32)], [(1, D_MODEL), (1, D_MODEL)], tm=256)

    small_names = ("rel_bias", "wlr", "blr", "gnorm", "ln1_g", "ln1_b", "ln2_g", "ln2_b")
    parts = [d_in_g, d_in_b] + [small[l][n] for l in range(DEPTH) for n in small_names] + [loss_lanes]
    shapes = [(D_MODEL,), (D_MODEL,)] + [s for l in range(DEPTH) for s in (
        (ATTN_HEADS, N_REL), (GLA_RANK, 512), (512,), (LIN_VAL_DIM,), (D_MODEL,), (D_MODEL,), (D_MODEL,), (D_MODEL,))] + [(128,)]
    red = _unpack(all_reduce_small(_pack(parts)), shapes)
    loss = jnp.sum(red[-1])
    g_small = {"ln_in_g": red[0], "ln_in_b": red[1]}
    for l in range(DEPTH):
        for j, n in enumerate(small_names):
            g_small[(l, n)] = red[2 + l * len(small_names) + j]
    shard = LIN_HEADS * LIN_KEY_DIM // N_DEV
    g_wlr = jnp.stack([lax.dynamic_slice_in_dim(g_small[(l, "wlr")], me * shard, shard, axis=1) for l in range(DEPTH)])
    stack = lambda n: jnp.stack([g_small[(l, n)] for l in range(DEPTH)])
    g_rep = [g_small["ln_in_g"], g_small["ln_in_b"], stack("rel_bias"), g_wlr, stack("blr"), stack("gnorm"),
             stack("ln1_g"), stack("ln1_b"), stack("ln2_g"), stack("ln2_b")]
    w_rep = [ln_in_g, ln_in_b, rel_bias, gla_w_lr, gla_b_lr, gla_norm_g, ln1_g, ln1_b, ln2_g, ln2_b]
    m_rep = [m_ln_in_g, m_ln_in_b, m_rel_bias, m_gla_w_lr, m_gla_b_lr, m_gla_norm_g, m_ln1_g, m_ln1_b, m_ln2_g, m_ln2_b]
    v_rep = [v_ln_in_g, v_ln_in_b, v_rel_bias, v_gla_w_lr, v_gla_b_lr, v_gla_norm_g, v_ln1_g, v_ln1_b, v_ln2_g, v_ln2_b]
    rep_shapes = [w.shape for w in w_rep]
    d_pack, m_pack, v_pack = row_call("adam_small", f_adam, [Row(_pack(g_rep)), Row(_pack(w_rep)), Row(_pack(m_rep)), Row(_pack(v_rep))],
                                      [], [(128, F32)] * 3, tm=256)
    d_rep, nm_rep, nv_rep = _unpack(d_pack, rep_shapes), _unpack(m_pack, rep_shapes), _unpack(v_pack, rep_shapes)

    def owned(key, w, m, v, name, tm):
        C = w.shape[-1]
        flat = lambda a: a.reshape(DEPTH, -1, C)
        outs = None
        for l in range(DEPTH):
            outs = _sum_adam(name, received[l][key].reshape(4, -1, C), flat(w), flat(m), flat(v), l, tm, outs)
        return [o.reshape(w.shape) for o in outs]

    o_in = owned("in", w_in, m_w_in, v_w_in, "adam_w_in", 256)
    o_br = owned("branch", w_branch, m_w_branch, v_w_branch, "adam_w_branch", 128)
    o_out = owned("out", w_out, m_w_out, v_w_out, "adam_w_out", 128)
    o_up = owned("up", w_up, m_w_up, v_w_up, "adam_w_up", 256)
    o_down = owned("down", w_down, m_w_down, v_w_down, "adam_w_down", 256)

    def tree(j, rep):
        return [rep[0], rep[1], o_in[j], rep[2], rep[3], rep[4], rep[5], o_br[j], o_out[j], rep[6], rep[7], o_up[j], o_down[j],
                rep[8], rep[9]]

    return (loss, grad_x.reshape(1, T, D_MODEL), *tree(0, g_rep), *tree(1, d_rep), *tree(2, nm_rep), *tree(3, nv_rep))
```

```python
import functools
import math

import jax
import jax.numpy as jnp
from jax import lax
from jax.experimental import pallas as pl
from jax.experimental.pallas import tpu as pltpu

F32 = jnp.float32
MXU_DTYPE = jnp.bfloat16
WIRE_DTYPE = jnp.bfloat16

N_DEV = 8
D_MODEL = 1024
DEPTH = 2
CHUNK = 64
ATTN_HEADS = 8
ATTN_HEAD_DIM = 128
ATTN_LEFT_CHUNKS = 8
BAND = (ATTN_LEFT_CHUNKS + 1) * CHUNK
PAD_ROWS = ATTN_LEFT_CHUNKS * CHUNK
REL_CLIP = 2 * CHUNK
N_REL = 2 * REL_CLIP + 1
REL_MIN = REL_CLIP - (CHUNK - 1)
LIN_HEADS = 4
LIN_KEY_DIM = 128
LIN_VAL_DIM = 256
GLA_RANK = 16
GLA_GATE_NORMALIZER = 16.0
ROPE_BASE = 10000.0
D_FF = 4 * D_MODEL
LN_EPS = 1e-5
NEG_INF = -1e30
ALPHA = (2 * DEPTH) ** 0.25
ADAM_LR, ADAM_B1, ADAM_B2, ADAM_EPS, ADAM_WD, ADAM_STEP = 0.001, 0.9, 0.999, 1e-08, 0.01, 10

PROJ_WIDTH = 12304
GLR_SRC = 9216
GATES_SRC = GLR_SRC + GLA_RANK
MAIN_W = 9216
GATES_W = 3 * D_MODEL
GATES_OFF = MAIN_W
GLR_OFF = MAIN_W + GATES_W
PROJ_PAD = 12800
GLR_BLOCK = 128
COL = dict(aq=0, ak=1024, av=2048, rq=3072, rk=3584, rv=4096, rg=5120, gq=6144, gk=6656, gv=7168, gg=8192)

SEG_CHUNKS = 8
SEG = SEG_CHUNKS * CHUNK
VMEM_BIG = 56 * 1024 * 1024

_NN = (((1,), (0,)), ((), ()))
_NT = (((1,), (1,)), ((), ()))
_TN = (((0,), (0,)), ((), ()))


def _mxu(a, b, dn):
    return lax.dot_general(a.astype(MXU_DTYPE), b.astype(MXU_DTYPE), dn, preferred_element_type=F32)


@jax.custom_vjp
def dot_nn(a, b):
    return _mxu(a, b, _NN)


def _dot_nn_fwd(a, b):
    return _mxu(a, b, _NN), (a, b)


def _dot_nn_bwd(res, g):
    a, b = res
    return _mxu(g, b, _NT).astype(a.dtype), _mxu(a, g, _TN).astype(b.dtype)


dot_nn.defvjp(_dot_nn_fwd, _dot_nn_bwd)


@jax.custom_vjp
def dot_nt(a, b):
    return _mxu(a, b, _NT)


def _dot_nt_fwd(a, b):
    return _mxu(a, b, _NT), (a, b)


def _dot_nt_bwd(res, g):
    a, b = res
    return _mxu(g, b, _NN).astype(a.dtype), _mxu(g, a, _TN).astype(b.dtype)


dot_nt.defvjp(_dot_nt_fwd, _dot_nt_bwd)


@jax.custom_vjp
def dot_tn(a, b):
    return _mxu(a, b, _TN)


def _dot_tn_fwd(a, b):
    return _mxu(a, b, _TN), (a, b)


def _dot_tn_bwd(res, g):
    a, b = res
    return _mxu(b, g, _NT).astype(a.dtype), _mxu(a, g, _NN).astype(b.dtype)


dot_tn.defvjp(_dot_tn_fwd, _dot_tn_bwd)


def _tri(n, lower):
    r = lax.broadcasted_iota(jnp.int32, (n, n), 0)
    c = lax.broadcasted_iota(jnp.int32, (n, n), 1)
    return (r >= c) if lower else (r <= c)


def _tri_dot(lower, x):
    tri = _tri(x.shape[0], lower).astype(jnp.bfloat16)
    hi = x.astype(jnp.bfloat16)
    rest = x - hi.astype(F32)
    mid = rest.astype(jnp.bfloat16)
    lo = (rest - mid.astype(F32)).astype(jnp.bfloat16)
    dot = lambda part: lax.dot_general(tri, part, _NN, preferred_element_type=F32)
    return dot(hi) + dot(mid) + dot(lo)


@jax.custom_vjp
def chunk_cumsum(x):
    return _tri_dot(True, x)


def _chunk_cumsum_fwd(x):
    return chunk_cumsum(x), None


def _chunk_cumsum_bwd(_, g):
    return (_tri_dot(False, g),)


chunk_cumsum.defvjp(_chunk_cumsum_fwd, _chunk_cumsum_bwd)


@jax.custom_vjp
def swap_halves(t):
    return pltpu.roll(t, t.shape[-1] // 2, t.ndim - 1)


def _swap_fwd(t):
    return swap_halves(t), None


def _swap_bwd(_, g):
    return (swap_halves(g),)


swap_halves.defvjp(_swap_fwd, _swap_bwd)


def mm(name, a, b, dims, *, tm, tn, tk, out_dtype=F32, extras=(), epi=None, n_out=1, out_dtypes=None, side=None,
       column_slabs=False):
    if dims == "nn":
        (M, K), N = a.shape, b.shape[1]
    elif dims == "nt":
        (M, K), N = a.shape, b.shape[0]
    else:
        (K, M), N = a.shape, b.shape[1]
    tm, tn, tk = min(tm, M), min(tn, N), min(tk, K)
    assert M % tm == 0 and N % tn == 0 and K % tk == 0, (name, M, N, K, tm, tn, tk)
    nk = K // tk
    dn = {"nn": _NN, "nt": _NT, "tn": _TN}[dims]
    a_spec = pl.BlockSpec((tk, tm), lambda i, j, k: (k, i)) if dims == "tn" else pl.BlockSpec((tm, tk), lambda i, j, k: (i, k))
    b_spec = pl.BlockSpec((tn, tk), lambda i, j, k: (j, k)) if dims == "nt" else pl.BlockSpec((tk, tn), lambda i, j, k: (k, j))
    o_spec = pl.BlockSpec((tm, tn), lambda i, j, k: (i, j))
    n_ex = len(extras)
    out_dtypes = out_dtypes or (out_dtype,) * n_out

    def finish(r, ex_refs, o_refs):
        if epi is not None:
            r = epi(r, *[e[...] for e in ex_refs])
        r = r if isinstance(r, tuple) else (r,)
        for o, v in zip(o_refs, r):
            o[...] = v.astype(o.dtype)

    def body(*refs):
        a_ref, b_ref = refs[:2]
        ex_refs = refs[2:2 + n_ex]
        o_refs = refs[2 + n_ex:2 + n_ex + n_out]
        part = _mxu(a_ref[...], b_ref[...], dn)
        if nk == 1:
            finish(part, ex_refs, o_refs)
            return
        acc = refs[-1]
        k = pl.program_id(2)

        @pl.when(k == 0)
        def _():
            acc[...] = part

        @pl.when(k > 0)
        def _():
            acc[...] += part

        @pl.when(k == nk - 1)
        def _():
            finish(acc[...], ex_refs, o_refs)

    outs = _side_call(
        side, body,
        name=name,
        grid=(M // tm, N // tn, nk),
        in_specs=[a_spec, b_spec] + [o_spec] * n_ex,
        out_specs=[pl.BlockSpec((None, tm, tn), lambda i, j, k: (j, i, 0))] * n_out if column_slabs else [o_spec] * n_out,
        out_shape=[jax.ShapeDtypeStruct((N // tn, M, tn) if column_slabs else (M, N), dt) for dt in out_dtypes],
        scratch_shapes=[pltpu.VMEM((tm, tn), F32)] if nk > 1 else [],
        args=[a, b, *extras],
        semantics=("parallel", "parallel", "arbitrary"),
    )
    return outs[0] if n_out == 1 else outs


class Row:
    def __init__(self, arr, width=None, col=0, lead=(), rowmap=None):
        self.arr, self.lead, self.col = arr, tuple(lead), col
        self.width = arr.shape[-1] if width is None else width
        self.rowmap = rowmap

    def spec(self, tm):
        lead, col, rowmap = self.lead, self.col, self.rowmap or (lambda i: i)
        return pl.BlockSpec((None,) * len(lead) + (tm, self.width), lambda i: (*lead, rowmap(i), col))


def row_call(name, fn, rows, params, new_outs, red_shapes=(), *, tm, into=None, into_cols=(), stack=None, stacked_onto=None):
    R = rows[0].arr.shape[len(rows[0].lead)]
    tm = min(tm, R)
    assert R % tm == 0, (name, R, tm)
    n_row, n_par, n_new, n_into, n_red = len(rows), len(params), len(new_outs), len(into_cols), len(red_shapes)
    fresh = isinstance(into, jax.ShapeDtypeStruct)

    def body(*refs):
        row_refs = refs[:n_row]
        par_refs = refs[n_row:n_row + n_par]
        pos = n_row + n_par + (1 if (into is not None and not fresh) else 0) + (n_new if stacked_onto is not None else 0)
        new_refs = refs[pos:pos + n_new]
        red_refs = refs[pos + n_new:pos + n_new + n_red]
        pos2 = pos + n_new + n_red
        into_ref = refs[pos2] if into is not None else None
        stage = refs[pos2 + 1:] if into is not None else ()
        i = pl.program_id(0)
        vals = fn(*[r[...] for r in row_refs], *[p[...] for p in par_refs])
        vals = vals if isinstance(vals, (tuple, list)) else (vals,)
        for o, v in zip(new_refs, vals[:n_new]):
            o[...] = v.astype(o.dtype)
        for (width, off), buf, v in zip(into_cols, stage, vals[n_new:n_new + n_into]):
            buf[...] = v.astype(buf.dtype)
            pltpu.sync_copy(buf, into_ref.at[pl.ds(pl.multiple_of(i * tm, tm), tm), pl.ds(off, width)])
        for o, v in zip(red_refs, vals[n_new + n_into:]):
            @pl.when(i == 0)
            def _(o=o, v=v):
                o[...] = v.astype(o.dtype)

            @pl.when(i > 0)
            def _(o=o, v=v):
                o[...] += v.astype(o.dtype)

    in_specs = [r.spec(tm) for r in rows] + [pl.BlockSpec(p.shape, lambda i, nd=p.ndim: (0,) * nd) for p in params]
    args = [r.arr for r in rows] + list(params)
    if stack is None:
        out_specs = [pl.BlockSpec((tm, w), lambda i: (i, 0)) for w, _ in new_outs]
        out_shape = [jax.ShapeDtypeStruct((R, w), dt) for w, dt in new_outs]
    else:
        index, depth = stack
        out_specs = [pl.BlockSpec((None, tm, w), lambda i: (index, i, 0)) for w, _ in new_outs]
        out_shape = [jax.ShapeDtypeStruct((depth, R, w), dt) for w, dt in new_outs]
    out_specs += [pl.BlockSpec(s, lambda i, nd=len(s): (0,) * nd) for s in red_shapes]
    out_shape += [jax.ShapeDtypeStruct(s, F32) for s in red_shapes]
    aliases, scratch = {}, []
    if into is not None:
        out_specs.append(pl.BlockSpec(memory_space=pl.ANY))
        out_shape.append(jax.ShapeDtypeStruct(into.shape, into.dtype))
        if not fresh:
            in_specs.append(pl.BlockSpec(memory_space=pl.ANY))
            args.append(into)
            aliases = {len(args) - 1: len(out_shape) - 1}
        scratch = [pltpu.VMEM((tm, w), into.dtype) for w, _ in into_cols]
    if stacked_onto is not None:
        for j, prev in enumerate(stacked_onto):
            in_specs.append(pl.BlockSpec(memory_space=pl.ANY))
            args.append(prev)
            aliases[len(args) - 1] = j
    outs = pl.pallas_call(
        body,
        name=name,
        grid=(R // tm,),
        in_specs=in_specs,
        out_specs=out_specs,
        out_shape=out_shape,
        scratch_shapes=scratch,
        input_output_aliases=aliases,
        compiler_params=pltpu.CompilerParams(dimension_semantics=("arbitrary",), vmem_limit_bytes=VMEM_BIG),
    )(*args)
    return list(outs)


def vjp_fn(fn, n_row, n_out, passthrough=0):
    def bwd(*args):
        rows = [a.astype(F32) for a in args[:n_row]]
        cts = args[n_row:n_row + n_out]
        thru = args[n_row + n_out:n_row + n_out + passthrough]
        pars = args[n_row + n_out + passthrough:]
        outs, pull = jax.vjp(lambda *xs: tuple(fn(*xs)), *rows, *pars)
        grads = pull(tuple(c.astype(o.dtype) for c, o in zip(cts, outs)))
        return tuple(grads[:n_row]) + tuple(thru) + tuple(grads[n_row:])

    return bwd


def _ln(x, g, b):
    mu = jnp.mean(x, axis=-1, keepdims=True)
    xc = x - mu
    var = jnp.mean(xc * xc, axis=-1, keepdims=True)
    return xc * lax.rsqrt(var + LN_EPS) * g + b


def f_ln_in(x, g, b):
    return (_ln(x, g, b),)


def f_deepnorm(x, y, g, b):
    return (_ln(ALPHA * x + y, g, b),)


def _twice(fn):
    return lambda *args: fn(*args) * 2


STREAM = [(D_MODEL, F32), (D_MODEL, MXU_DTYPE)]


def _rope(t, cos, sin):
    parts = []
    for h in range(LIN_HEADS):
        th = t[:, h * LIN_KEY_DIM:(h + 1) * LIN_KEY_DIM]
        parts.append(th * cos + swap_halves(th) * sin)
    return jnp.concatenate(parts, axis=1)


def f_ret_pre(rq, rk, cos, sin):
    return _rope(rq, cos, sin), _rope(rk, cos, sin) * LIN_KEY_DIM ** -0.5


def _log_sigmoid(x):
    return jnp.minimum(x, 0.0) - jnp.log(1.0 + jnp.exp(-jnp.abs(x)))


def f_gla_pre(gq, glr, wlr, blr):
    gate_pre = dot_nn(glr, wlr) + blr
    return gq * LIN_KEY_DIM ** -0.5, _log_sigmoid(gate_pre) / GLA_GATE_NORMALIZER


def f_normgate(ret_raw, rg, gla_raw, gg, gnorm):
    rets, glas = [], []
    for h in range(LIN_HEADS):
        sl = slice(h * LIN_VAL_DIM, (h + 1) * LIN_VAL_DIM)
        t = ret_raw[:, sl]
        mu = jnp.mean(t, axis=-1, keepdims=True)
        tc = t - mu
        rets.append(tc * lax.rsqrt(jnp.mean(tc * tc, axis=-1, keepdims=True) + LN_EPS))
        u = gla_raw[:, sl]
        glas.append(u * lax.rsqrt(jnp.mean(u * u, axis=-1, keepdims=True) + LN_EPS) * gnorm)
    ret = jax.nn.silu(rg) * jnp.concatenate(rets, axis=1)
    gla = jax.nn.silu(gg) * jnp.concatenate(glas, axis=1)
    return ret, gla


def f_merge(p0, p1, p2, gl):
    out = jax.nn.sigmoid(gl[:, :D_MODEL]) * p0
    out = out + jax.nn.sigmoid(gl[:, D_MODEL:2 * D_MODEL]) * p1
    return (out + jax.nn.sigmoid(gl[:, 2 * D_MODEL:]) * p2,)


def f_loss(y, tgt):
    diff = y - tgt
    sq = jnp.sum(diff * diff, axis=0, keepdims=True)
    lanes = sq[:, :128]
    for j in range(1, D_MODEL // 128):
        lanes = lanes + sq[:, j * 128:(j + 1) * 128]
    return diff * (1.0 / D_MODEL), lanes * (0.5 / D_MODEL)


def _adam(g, w, m, v):
    m = ADAM_B1 * m + (1.0 - ADAM_B1) * g
    v = ADAM_B2 * v + (1.0 - ADAM_B2) * jnp.square(g)
    m_hat = m / (1.0 - ADAM_B1 ** ADAM_STEP)
    v_hat = v / (1.0 - ADAM_B2 ** ADAM_STEP)
    delta = -ADAM_LR * (m_hat / (jnp.sqrt(v_hat) + ADAM_EPS) + ADAM_WD * w)
    return delta, m, v


def f_sum_adam(*args):
    g = args[0].astype(F32)
    for p in args[1:-3]:
        g = g + p.astype(F32)
    return (g,) + _adam(g, *args[-3:])


def f_adam(g, w, m, v):
    return _adam(g, w, m, v)


GROUP = 4
GROUP_ROWS = GROUP * CHUNK
GROUP_KEYS = (GROUP + ATTN_LEFT_CHUNKS) * CHUNK
DIAG_ROWS = REL_CLIP + CHUNK
GRAD_ROWS = 200


def _blocks_by_delta():
    blocks = {}
    for qi in range(GROUP):
        for J in range(GROUP_KEYS // 128):
            blocks.setdefault(qi - 2 * J, []).append((qi, J))
    return blocks


def bias_table(rel_bias, side=None):
    n_tab = rel_bias.shape[0] * ATTN_HEADS

    def body(rb_ref, o_ref):
        h = pl.program_id(0)
        rm = lax.broadcasted_iota(jnp.int32, (CHUNK, 128), 0)
        cm = lax.broadcasted_iota(jnp.int32, (CHUNK, 128), 1)
        ahead = jnp.where(cm >= CHUNK, 1, 0)
        for delta, where in _blocks_by_delta().items():
            base = 64 * delta + PAD_ROWS
            rel = jnp.minimum(base + rm - cm, REL_CLIP) + REL_CLIP
            in_band = (ahead - delta >= 0) & (ahead - delta <= ATTN_LEFT_CHUNKS)
            lo = max(REL_MIN, min(base - 127, REL_CLIP) + REL_CLIP)
            hi = min(base + CHUNK - 1, REL_CLIP) + REL_CLIP

            def step(r, acc, rel=rel):
                return jnp.where(rel == r, rb_ref[h * N_REL + r], acc)

            blk = jnp.zeros((CHUNK, 128), F32) if hi < lo else lax.fori_loop(lo, hi + 1, step, jnp.zeros((CHUNK, 128), F32))
            blk = jnp.where(in_band, blk, NEG_INF)
            for qi, J in where:
                o_ref[pl.ds(qi * CHUNK, CHUNK), pl.ds(J * 128, 128)] = blk

    return _side_call(
        side, body,
        name="bias_table",
        grid=(n_tab,),
        in_specs=[pl.BlockSpec(memory_space=pltpu.SMEM)],
        out_specs=[pl.BlockSpec((None, GROUP_ROWS, GROUP_KEYS), lambda h: (h, 0, 0))],
        out_shape=[jax.ShapeDtypeStruct((n_tab, GROUP_ROWS, GROUP_KEYS), F32)],
        scratch_shapes=[],
        args=[rel_bias.reshape(n_tab * N_REL)],
    )[0]


def bias_table_grad(dbias):
    blocks = _blocks_by_delta()
    below_clip = range(-(CHUNK - 1), REL_CLIP)
    deltas = [d for d in sorted(blocks) if any(-127 <= pos - 64 * d - PAD_ROWS <= CHUNK - 1 for pos in below_clip)]

    def body(d_ref, o_ref, diag):
        sums = []
        for delta in deltas:
            acc = None
            for qi, J in blocks[delta]:
                blk = d_ref[pl.ds(qi * CHUNK, CHUNK), pl.ds(J * 128, 128)]
                acc = blk if acc is None else acc + blk
            sums.append(acc)
        off = lax.broadcasted_iota(jnp.int32, (CHUNK, 128), 0) - lax.broadcasted_iota(jnp.int32, (CHUNK, 128), 1)
        diag[...] = jnp.zeros_like(diag)

        def step(t, carry):
            pos = t - (CHUNK - 1)
            picked = jnp.zeros((CHUNK, 128), F32)
            for delta, s in zip(deltas, sums):
                picked = picked + jnp.where(off == pos - 64 * delta - PAD_ROWS, s, 0.0)
            diag[pl.ds(t, 1), :] = jnp.sum(picked, axis=0, keepdims=True)
            return carry

        lax.fori_loop(0, DIAG_ROWS, step, 0, unroll=8)
        d = d_ref[...]
        total = jnp.sum(jnp.sum(d, axis=1, keepdims=True), axis=0, keepdims=True)
        o_ref[pl.ds(0, DIAG_ROWS), :] = jnp.broadcast_to(jnp.sum(diag[...], axis=1, keepdims=True), (DIAG_ROWS, 128))
        o_ref[pl.ds(DIAG_ROWS, GRAD_ROWS - DIAG_ROWS), :] = jnp.broadcast_to(total, (GRAD_ROWS - DIAG_ROWS, 128))

    out = pl.pallas_call(
        body,
        name="bias_table_grad",
        grid=(ATTN_HEADS,),
        in_specs=[pl.BlockSpec((None, GROUP_ROWS, GROUP_KEYS), lambda h: (h, 0, 0))],
        out_specs=pl.BlockSpec((None, GRAD_ROWS, 128), lambda h: (h, 0, 0)),
        out_shape=jax.ShapeDtypeStruct((ATTN_HEADS, GRAD_ROWS, 128), F32),
        scratch_shapes=[pltpu.VMEM((DIAG_ROWS, 128), F32)],
        compiler_params=pltpu.CompilerParams(dimension_semantics=("arbitrary",)),
    )(dbias)
    below = out[:, :DIAG_ROWS - 1, 0]
    clipped = out[:, DIAG_ROWS, 0] - jnp.sum(below, axis=1)
    return jnp.concatenate([jnp.zeros((ATTN_HEADS, REL_MIN), F32), below, clipped[:, None]], axis=1)


def _attn_stage(q_ref, k_ref, v_ref, qs, kp, vp, T):
    qs[...] = (q_ref[...] * ATTN_HEAD_DIM ** -0.5).astype(qs.dtype)
    for dst, src in ((kp, k_ref), (vp, v_ref)):
        dst[pl.ds(0, PAD_ROWS), :] = jnp.zeros((PAD_ROWS, ATTN_HEAD_DIM), dst.dtype)
        dst[pl.ds(PAD_ROWS, T), :] = src[...].astype(dst.dtype)


def _attn_probs(g, first_groups, qs, kp, b_ref):
    r0 = g * GROUP_ROWS if first_groups else pl.multiple_of(g * GROUP_ROWS, GROUP_ROWS)
    qc = qs[pl.ds(r0, GROUP_ROWS), :]
    kb = kp[pl.ds(r0, GROUP_KEYS), :]
    s = _mxu(qc, kb, _NT) + b_ref[...]
    if first_groups:
        k_chunk = lax.shift_right_logical(lax.broadcasted_iota(jnp.int32, (GROUP_ROWS, GROUP_KEYS), 1), 6)
        s = jnp.where(g * GROUP + k_chunk >= ATTN_LEFT_CHUNKS, s, NEG_INF)
    e = jnp.exp(s - jnp.max(s, axis=-1, keepdims=True))
    return r0, qc, kb, e / jnp.sum(e, axis=-1, keepdims=True)


def _for_groups(T, step):
    n_first = min(ATTN_LEFT_CHUNKS // GROUP, T // GROUP_ROWS)
    for g in range(n_first):
        step(g, True)
    n_loop = T // GROUP_ROWS - n_first
    lax.fori_loop(n_first, T // GROUP_ROWS, lambda g, c: (step(g, False), c)[1], 0, unroll=2 if n_loop % 2 == 0 else 1)


def _head_spec(T, base):
    return pl.BlockSpec((T, ATTN_HEAD_DIM), lambda h: (0, base + h))


def _attn_scratch(T):
    return [pltpu.VMEM((T, ATTN_HEAD_DIM), MXU_DTYPE)] + [pltpu.VMEM((T + PAD_ROWS, ATTN_HEAD_DIM), MXU_DTYPE)] * 2


def attn_fwd(proj, bias_tab, first_tab, side=None):
    T = proj.shape[0]

    def body(q_ref, k_ref, v_ref, b_ref, o_ref, qs, kp, vp):
        _attn_stage(q_ref, k_ref, v_ref, qs, kp, vp, T)

        def step(g, first):
            r0, _, _, p = _attn_probs(g, first, qs, kp, b_ref)
            o_ref[pl.ds(r0, GROUP_ROWS), :] = _mxu(p, vp[pl.ds(r0, GROUP_KEYS), :], _NN).astype(o_ref.dtype)

        _for_groups(T, step)

    out, = _side_call(
        side, body,
        name="attn_fwd",
        grid=(ATTN_HEADS,),
        in_specs=[_head_spec(T, COL["aq"] // 128), _head_spec(T, COL["ak"] // 128), _head_spec(T, COL["av"] // 128),
                  pl.BlockSpec((None, GROUP_ROWS, GROUP_KEYS), lambda h: (first_tab + h, 0, 0))],
        out_specs=[pl.BlockSpec((T, ATTN_HEAD_DIM), lambda h: (0, h))],
        out_shape=[jax.ShapeDtypeStruct((T, ATTN_HEADS * ATTN_HEAD_DIM), MXU_DTYPE)],
        scratch_shapes=_attn_scratch(T),
        args=[proj, proj, proj, bias_tab],
    )
    return out


def attn_bwd(proj, bias_tab, first_tab, dout, dproj, side=None):
    T = proj.shape[0]

    def body(q_ref, k_ref, v_ref, b_ref, do_ref, dproj_in, dbias_ref, dproj_ref, qs, kp, vp, dkp, dvp, dq, stage):
        del dproj_in
        h = pl.program_id(0)
        _attn_stage(q_ref, k_ref, v_ref, qs, kp, vp, T)
        dkp[...] = jnp.zeros_like(dkp)
        dvp[...] = jnp.zeros_like(dvp)
        dbias_ref[...] = jnp.zeros_like(dbias_ref)

        def step(g, first):
            r0, qc, kb, p = _attn_probs(g, first, qs, kp, b_ref)
            do = do_ref[pl.ds(r0, GROUP_ROWS), :]
            dvp[pl.ds(r0, GROUP_KEYS), :] += _mxu(p, do, _TN)
            dp = _mxu(do, vp[pl.ds(r0, GROUP_KEYS), :], _NT)
            ds = p * (dp - jnp.sum(dp * p, axis=-1, keepdims=True))
            dq[pl.ds(r0, GROUP_ROWS), :] = _mxu(ds, kb, _NN) * ATTN_HEAD_DIM ** -0.5
            dkp[pl.ds(r0, GROUP_KEYS), :] += _mxu(ds, qc, _TN)
            dbias_ref[...] += ds

        _for_groups(T, step)
        col = pl.multiple_of(h * ATTN_HEAD_DIM, ATTN_HEAD_DIM)
        for grad, first_row, base in ((dq, 0, "aq"), (dkp, PAD_ROWS, "ak"), (dvp, PAD_ROWS, "av")):
            stage[...] = grad[pl.ds(first_row, T), :].astype(stage.dtype)
            pltpu.sync_copy(stage, dproj_ref.at[:, pl.ds(COL[base] + col, ATTN_HEAD_DIM)])

    dbias, dproj = _side_call(
        side, body,
        name="attn_bwd",
        grid=(ATTN_HEADS,),
        in_specs=[_head_spec(T, COL["aq"] // 128), _head_spec(T, COL["ak"] // 128), _head_spec(T, COL["av"] // 128),
                  pl.BlockSpec((None, GROUP_ROWS, GROUP_KEYS), lambda h: (first_tab + h, 0, 0)),
                  pl.BlockSpec((T, ATTN_HEAD_DIM), lambda h: (0, h)),
                  pl.BlockSpec(memory_space=pl.ANY)],
        out_specs=[pl.BlockSpec((None, GROUP_ROWS, GROUP_KEYS), lambda h: (h, 0, 0)), pl.BlockSpec(memory_space=pl.ANY)],
        out_shape=[jax.ShapeDtypeStruct((ATTN_HEADS, GROUP_ROWS, GROUP_KEYS), F32),
                   jax.ShapeDtypeStruct(dproj.shape, dproj.dtype)],
        scratch_shapes=_attn_scratch(T) + [pltpu.VMEM((T + PAD_ROWS, ATTN_HEAD_DIM), F32)] * 2
        + [pltpu.VMEM((T, ATTN_HEAD_DIM), F32), pltpu.VMEM((T, ATTN_HEAD_DIM), dproj.dtype)],
        args=[proj, proj, proj, bias_tab, dout, dproj],
        aliases={5: 1},
    )
    return dproj, dbias


def _lin_segment(q, k, v, la, st):
    lower = _tri(CHUNK, True)
    steps = (lax.broadcasted_iota(jnp.int32, (CHUNK, 1), 0) + 1).astype(F32)
    outs = []
    for i in range(SEG_CHUNKS):
        sl = slice(i * CHUNK, (i + 1) * CHUNK)
        qi, ki, vi = q[sl], k[sl], v[sl]
        if la.shape[0] == 1:
            cum, last = la * steps, la * float(CHUNK)
        else:
            cum, last = chunk_cumsum(la[sl]), jnp.sum(la[sl], axis=0, keepdims=True)
        e_pos, e_neg = jnp.exp(cum), jnp.exp(-cum)
        q_fwd = qi * e_pos
        s_fwd = dot_nt(q_fwd, ki * e_neg)
        s_bwd = dot_nt(qi * e_neg, ki * e_pos)
        o = dot_nn(jnp.where(lower, s_fwd, s_bwd), vi) + dot_nt(q_fwd, st)
        st = jnp.exp(last) * st + dot_tn(vi, ki * jnp.exp(last - cum))
        outs.append(o)
    return jnp.concatenate(outs, axis=0), st


class Win:
    def __init__(self, arr, width, base=0):
        self.arr, self.width, self.base = arr, width, base


HEADS_PER_STEP = 2


def _lin_spec(w, seg_of):
    assert w.base % HEADS_PER_STEP == 0
    first, width = w.base // HEADS_PER_STEP, HEADS_PER_STEP * w.width
    if w.arr.shape[0] == 1:
        return pl.BlockSpec((1, width), lambda p, s: (0, first + p))
    return pl.BlockSpec((SEG, width), lambda p, s: (seg_of(s), first + p))


def _head_cols(ref, j, width):
    return ref[:, j * width:(j + 1) * width]


def _decay_rows(la_ref, j):
    return _head_cols(la_ref, j, LIN_KEY_DIM)


_STATE_BLOCK = (HEADS_PER_STEP, None, LIN_VAL_DIM, LIN_KEY_DIM)


def lin_fwd(name, q, k, v, la, side=None):
    T = q.arr.shape[0]
    nseg = T // SEG

    def body(q_ref, k_ref, v_ref, la_ref, o_ref, st_ref, st):
        @pl.when(pl.program_id(1) == 0)
        def _():
            st[...] = jnp.zeros_like(st)

        for j in range(HEADS_PER_STEP):
            st0 = st[j]
            st_ref[j] = st0
            o, st1 = _lin_segment(_head_cols(q_ref, j, LIN_KEY_DIM), _head_cols(k_ref, j, LIN_KEY_DIM),
                                  _head_cols(v_ref, j, LIN_VAL_DIM), _decay_rows(la_ref, j), st0)
            o_ref[:, j * LIN_VAL_DIM:(j + 1) * LIN_VAL_DIM] = o
            st[j] = st1

    fwd = lambda s: s
    out, states = _side_call(
        side, body,
        name=name,
        grid=(LIN_HEADS // HEADS_PER_STEP, nseg),
        in_specs=[_lin_spec(w, fwd) for w in (q, k, v, la)],
        out_specs=[pl.BlockSpec((SEG, HEADS_PER_STEP * LIN_VAL_DIM), lambda p, s: (s, p)),
                   pl.BlockSpec(_STATE_BLOCK, lambda p, s: (p, s, 0, 0))],
        out_shape=[jax.ShapeDtypeStruct((T, LIN_HEADS * LIN_VAL_DIM), F32),
                   jax.ShapeDtypeStruct((LIN_HEADS, nseg, LIN_VAL_DIM, LIN_KEY_DIM), F32)],
        scratch_shapes=[pltpu.VMEM((HEADS_PER_STEP, LIN_VAL_DIM, LIN_KEY_DIM), F32)],
        args=[q.arr, k.arr, v.arr, la.arr],
    )
    return out, states


def lin_bwd(name, q, k, v, la, states, dout, with_decay_grad, side=None):
    T = q.arr.shape[0]
    nseg = T // SEG
    n_out = 4 if with_decay_grad else 3

    widths = [LIN_KEY_DIM, LIN_KEY_DIM, LIN_VAL_DIM] + ([LIN_KEY_DIM] if with_decay_grad else [])

    def body(q_ref, k_ref, v_ref, la_ref, st_ref, do_ref, *rest):
        outs, dst = rest[:n_out], rest[n_out]

        @pl.when(pl.program_id(1) == 0)
        def _():
            dst[...] = jnp.zeros_like(dst)

        for j in range(HEADS_PER_STEP):
            _, pull = jax.vjp(_lin_segment, _head_cols(q_ref, j, LIN_KEY_DIM).astype(F32),
                              _head_cols(k_ref, j, LIN_KEY_DIM).astype(F32), _head_cols(v_ref, j, LIN_VAL_DIM).astype(F32),
                              _decay_rows(la_ref, j), st_ref[j])
            grads = pull((_head_cols(do_ref, j, LIN_VAL_DIM), dst[j]))
            for o, g, w in zip(outs, grads[:n_out], widths):
                o[:, j * w:(j + 1) * w] = g
            dst[j] = grads[4]

    rev = lambda s: nseg - 1 - s
    key_spec = pl.BlockSpec((SEG, HEADS_PER_STEP * LIN_KEY_DIM), lambda p, s: (rev(s), p))
    val_spec = pl.BlockSpec((SEG, HEADS_PER_STEP * LIN_VAL_DIM), lambda p, s: (rev(s), p))
    out_specs = [key_spec, key_spec, val_spec] + ([key_spec] if with_decay_grad else [])
    return _side_call(
        side, body,
        name=name,
        grid=(LIN_HEADS // HEADS_PER_STEP, nseg),
        in_specs=[_lin_spec(w, rev) for w in (q, k, v, la)] + [
            pl.BlockSpec(_STATE_BLOCK, lambda p, s: (p, rev(s), 0, 0)), val_spec],
        out_specs=out_specs,
        out_shape=[jax.ShapeDtypeStruct((T, LIN_HEADS * w), F32) for w in widths],
        scratch_shapes=[pltpu.VMEM((HEADS_PER_STEP, LIN_VAL_DIM, LIN_KEY_DIM), F32)],
        args=[q.arr, k.arr, v.arr, la.arr, states, dout],
    )


def _coords():
    return lax.axis_index("x"), lax.axis_index("y"), lax.axis_index("c")


def _peer(k):
    x, y, c = _coords()
    px = 1 - x if k & 4 else x
    py = 1 - y if k & 2 else y
    pc = 1 - c if k & 1 else c
    return (px, py, pc), 4 * px + 2 * py + pc


def _remote(src, dst, send_sem, recv_sem, device):
    return pltpu.make_async_remote_copy(src_ref=src, dst_ref=dst, send_sem=send_sem, recv_sem=recv_sem,
                                        device_id=device, device_id_type=pl.DeviceIdType.MESH)


class Side:
    def __init__(self, kind, arrays):
        self.kind, self.arrays, self.n = kind, list(arrays), len(arrays)
        self.outs_per = 1
        self.slots = {"gather": N_DEV - 1, "pair": 4, "chips": 3}[kind]

    def out_shape(self):
        if self.kind == "gather":
            return [jax.ShapeDtypeStruct((N_DEV,) + a.shape, a.dtype) for a in self.arrays]
        return [jax.ShapeDtypeStruct((4,) + a.shape[1:], a.dtype) for a in self.arrays]

    def sems(self):
        return [pltpu.SemaphoreType.DMA((self.n, self.slots)), pltpu.SemaphoreType.DMA((self.n, self.slots)),
                pltpu.SemaphoreType.DMA((self.n,))]

    def start(self, ins, outs, sems):
        send, recv, own = sems
        x, y, c = _coords()
        me = 4 * x + 2 * y + c
        for a in range(self.n):
            if self.kind == "gather":
                pltpu.make_async_copy(ins[a], outs[a].at[me], own.at[a]).start()
                for k in (1, 2, 4, 6):
                    peer, _ = _peer(k)
                    _remote(ins[a], outs[a].at[me], send.at[a, k - 1], recv.at[a, k - 1], peer).start()
            elif self.kind == "pair":
                for j in range(4):
                    _remote(ins[a].at[2 * j + 1 - c], outs[a].at[j], send.at[a, j], recv.at[a, j], (x, y, 1 - c)).start()
            else:
                chip = 2 * x + y
                pltpu.make_async_copy(ins[a].at[chip], outs[a].at[chip], own.at[a]).start()
                for d in (1, 2, 3):
                    tx, ty = (1 - x if d & 2 else x), (1 - y if d & 1 else y)
                    _remote(ins[a].at[2 * tx + ty], outs[a].at[chip], send.at[a, d - 1], recv.at[a, d - 1], (tx, ty, c)).start()

    def finish(self, ins, outs, sems):
        send, recv, own = sems
        x, y, c = _coords()
        me = 4 * x + 2 * y + c
        sibling = (x, y, 1 - c)
        for a in range(self.n):
            if self.kind == "gather":
                blk = outs[a].at[me]
                for k in (2, 4, 6):
                    _, pidx = _peer(k)
                    _remote(blk, blk, send.at[a, k - 1], recv.at[a, k - 1], sibling).wait_recv()
                    _remote(outs[a].at[pidx], outs[a].at[pidx], send.at[a, k], recv.at[a, k], sibling).start()
                pltpu.make_async_copy(ins[a], blk, own.at[a]).wait()
                for k in range(1, N_DEV):
                    _remote(blk, blk, send.at[a, k - 1], recv.at[a, k - 1], sibling).wait_send()
                for k in (1, 3, 5, 7):
                    _remote(blk, blk, send.at[a, k - 1], recv.at[a, k - 1], sibling).wait_recv()
            elif self.kind == "pair":
                for j in range(4):
                    _remote(ins[a].at[j], outs[a].at[j], send.at[a, j], recv.at[a, j], sibling).wait()
            else:
                slab = outs[a].at[0]
                pltpu.make_async_copy(slab, slab, own.at[a]).wait()
                for d in (1, 2, 3):
                    _remote(slab, slab, send.at[a, d - 1], recv.at[a, d - 1], sibling).wait()


def _side_call(side, body, *, name, grid, in_specs, out_specs, out_shape, scratch_shapes, args, semantics=None,
               aliases=None):
    in_specs, out_specs, out_shape = list(in_specs), list(out_specs), list(out_shape)
    scratch_shapes, args = list(scratch_shapes), list(args)
    n_in, n_out = len(in_specs), len(out_specs)
    kernel_body = body
    if side is not None:
        n_so = side.n * side.outs_per

        def kernel_body(*refs):
            ins, side_ins = refs[:n_in], refs[n_in:n_in + side.n]
            pos = n_in + side.n
            outs, side_outs = refs[pos:pos + n_out], refs[pos + n_out:pos + n_out + n_so]
            rest = refs[pos + n_out + n_so:]
            scratch, sems = rest[:-3], rest[-3:]
            ids = [pl.program_id(ax) for ax in range(len(grid))]
            first = functools.reduce(jnp.logical_and, [i == 0 for i in ids])
            last = functools.reduce(jnp.logical_and, [i == g - 1 for i, g in zip(ids, grid)])

            @pl.when(first)
            def _():
                side.start(side_ins, side_outs, sems)

            body(*ins, *outs, *scratch)

            @pl.when(last)
            def _():
                side.finish(side_ins, side_outs, sems)

        any_spec = pl.BlockSpec(memory_space=pl.ANY)
        in_specs += [any_spec] * side.n
        args += side.arrays
        out_specs += [any_spec] * n_so
        out_shape += side.out_shape()
        scratch_shapes += side.sems()
        semantics = ("arbitrary",) * len(grid)
    outs = pl.pallas_call(
        kernel_body,
        name=name,
        grid=grid,
        in_specs=in_specs,
        out_specs=out_specs,
        out_shape=out_shape,
        scratch_shapes=scratch_shapes,
        input_output_aliases=aliases or {},
        compiler_params=pltpu.CompilerParams(dimension_semantics=semantics or ("arbitrary",) * len(grid),
                                             vmem_limit_bytes=VMEM_BIG),
    )(*args)
    if side is not None:
        side.results = list(outs[n_out:])
    return list(outs[:n_out])


def carrier(name, side):
    _side_call(side, lambda: None, name=name, grid=(1,), in_specs=[], out_specs=[], out_shape=[], scratch_shapes=[], args=[])
    return side.results


def all_reduce_small(vec):
    def body(v_ref, o_ref, land, send_sems, recv_sems):
        x, y, c = _coords()
        me = 4 * x + 2 * y + c
        land[me] = v_ref[...]
        copies = []
        for k in range(1, N_DEV):
            peer, _ = _peer(k)
            cp = pltpu.make_async_remote_copy(
                src_ref=v_ref, dst_ref=land.at[me], send_sem=send_sems.at[k - 1], recv_sem=recv_sems.at[k - 1],
                device_id=peer, device_id_type=pl.DeviceIdType.MESH)
            cp.start()
            copies.append(cp)
        for cp in copies:
            cp.wait()
        total = land[0]
        for s in range(1, N_DEV):
            total = total + land[s]
        o_ref[...] = total

    return pl.pallas_call(
        body,
        name="all_reduce_small",
        in_specs=[pl.BlockSpec(memory_space=pltpu.VMEM)],
        out_specs=pl.BlockSpec(memory_space=pltpu.VMEM),
        out_shape=jax.ShapeDtypeStruct(vec.shape, F32),
        scratch_shapes=[pltpu.VMEM((N_DEV,) + vec.shape, F32), pltpu.SemaphoreType.DMA((N_DEV - 1,)),
                        pltpu.SemaphoreType.DMA((N_DEV - 1,))],
    )(vec)


def _pack(parts):
    flat = jnp.concatenate([p.reshape(-1).astype(F32) for p in parts])
    n = flat.shape[0]
    rows = -(-n // 1024) * 8
    return jnp.pad(flat, (0, rows * 128 - n)).reshape(rows, 128)


def _unpack(packed, shapes):
    flat, out, pos = packed.reshape(-1), [], 0
    for s in shapes:
        n = math.prod(s)
        out.append(flat[pos:pos + n].reshape(s))
        pos += n
    return out


def _row2(v):
    return v.reshape(1, -1)


def _rider(hooks, name, ctx=None):
    if name not in hooks:
        return None, lambda: None
    side, done = hooks[name](ctx)
    return side, lambda: done(side.results)


def _layer_forward(h, hb, W, consts, hooks):
    T = h.shape[0]
    cos, sin, ret_decay = consts
    side, landed = _rider(hooks, "proj")
    proj = mm("proj", hb, W["in"], "nn", tm=4096, tn=512, tk=1024, side=side)
    landed()
    bias_tab, first_tab = W["bias_tab"]
    side, landed = _rider(hooks, "attn_fwd")
    attn = attn_fwd(proj, bias_tab, first_tab, side=side)
    landed()
    rq2, rk2 = row_call("ret_pre", f_ret_pre, [Row(proj, 512, COL["rq"] // 512), Row(proj, 512, COL["rk"] // 512),
                                              Row(cos), Row(sin)], [], [(512, F32), (512, F32)], tm=512)
    side, landed = _rider(hooks, "ret_fwd")
    ret_raw, ret_st = lin_fwd("ret_fwd", Win(rq2, 128), Win(rk2, 128), Win(proj, 256, COL["rv"] // 256), Win(ret_decay, 128),
                              side=side)
    landed()
    gq2, gla_la = row_call("gla_pre", f_gla_pre, [Row(proj, 512, COL["gq"] // 512), Row(proj, GLR_BLOCK, GLR_OFF // GLR_BLOCK)],
                           [W["wlr"], W["blr"]], [(512, F32), (512, F32)], tm=512)
    side, landed = _rider(hooks, "gla_fwd")
    gla_raw, gla_st = lin_fwd("gla_fwd", Win(gq2, 128), Win(proj, 128, COL["gk"] // 128), Win(proj, 256, COL["gv"] // 256),
                              Win(gla_la, 128), side=side)
    landed()
    ret_b, gla_b = row_call("normgate", f_normgate, [Row(ret_raw), Row(proj, 1024, COL["rg"] // 1024), Row(gla_raw),
                                                    Row(proj, 1024, COL["gg"] // 1024)], [W["gnorm"]],
                            [(1024, MXU_DTYPE), (1024, MXU_DTYPE)], tm=256)
    branches = (attn, ret_b, gla_b)
    p = [mm("branch_proj", branches[n], W["branch"][n], "nn", tm=1024, tn=1024, tk=1024) for n in range(3)]
    merged, = row_call("merge", f_merge, [Row(p[0]), Row(p[1]), Row(p[2]), Row(proj, GATES_W, GATES_OFF // GATES_W)], [],
                       [(1024, MXU_DTYPE)], tm=256)
    mix = mm("out_proj", merged, W["out"], "nn", tm=1024, tn=1024, tk=1024)
    x1, x1b = row_call("ln1", _twice(f_deepnorm), [Row(h), Row(mix)], [W["ln1_g"], W["ln1_b"]], STREAM, tm=512)
    side, landed = _rider(hooks, "mlp_up")
    u, act = mm("mlp_up", x1b, W["up"], "nn", tm=2048, tn=512, tk=1024, n_out=2, out_dtypes=(F32, MXU_DTYPE),
                epi=lambda r: (r, jnp.square(jnp.maximum(r, 0.0))), side=side)
    landed()
    y = mm("mlp_down", act, W["down"], "nn", tm=1024, tn=1024, tk=2048)
    x2, x2b = row_call("ln2", _twice(f_deepnorm), [Row(x1), Row(y)], [W["ln2_g"], W["ln2_b"]], STREAM, tm=512)
    saved = dict(h=h, hb=hb, proj=proj, rq2=rq2, rk2=rk2, ret_st=ret_st, gq2=gq2, gla_la=gla_la,
                 gla_st=gla_st, ret_raw=ret_raw, gla_raw=gla_raw, branches=branches, p=p, merged=merged, mix=mix, x1=x1, x1b=x1b,
                 u=u, act=act, y=y)
    return x2, x2b, saved


def _layer_backward(dx2, S, W, consts, hooks):
    cos, sin, ret_decay = consts
    h, proj, T = S["h"], S["proj"], S["h"].shape[0]
    add = lambda r, e: r + e
    dx1a, dy, dg2, db2 = row_call("ln2_bwd", vjp_fn(f_deepnorm, 2, 1), [Row(S["x1"]), Row(S["y"]), Row(dx2)],
                                  [W["ln2_g"], W["ln2_b"]], [(1024, F32), (1024, MXU_DTYPE)], [(1, 1024), (1, 1024)], tm=256)
    side, landed = _rider(hooks, "mlp_down_bwd")
    du = mm("mlp_down_bwd", dy, W["down"], "nt", tm=1024, tn=512, tk=1024, extras=(S["u"],),
            epi=lambda r, u: r * (2.0 * jnp.maximum(u, 0.0)), out_dtype=MXU_DTYPE, side=side)
    landed()
    g_down = mm("mlp_down_wgrad", S["act"], dy, "tn", tm=1024, tn=1024, tk=2048, out_dtype=WIRE_DTYPE)
    dx1 = mm("mlp_up_bwd", du, W["up"], "nt", tm=1024, tn=1024, tk=2048, extras=(dx1a,), epi=add)
    g_up = mm("mlp_up_wgrad", jnp.transpose(S["x1b"]), du, "nn", tm=1024, tn=D_FF // N_DEV, tk=2048, out_dtype=WIRE_DTYPE,
              column_slabs=True)
    dha, dmix, dg1, db1 = row_call("ln1_bwd", vjp_fn(f_deepnorm, 2, 1), [Row(h), Row(S["mix"]), Row(dx1)],
                                   [W["ln1_g"], W["ln1_b"]], [(1024, F32), (1024, MXU_DTYPE)], [(1, 1024), (1, 1024)], tm=256)
    dmerged = mm("out_proj_bwd", dmix, W["out"], "nt", tm=1024, tn=1024, tk=1024)
    g_out = mm("out_proj_wgrad", S["merged"], dmix, "tn", tm=1024, tn=1024, tk=2048, out_dtype=WIRE_DTYPE)
    dproj = jax.ShapeDtypeStruct((T, PROJ_PAD), MXU_DTYPE)
    p = S["p"]
    dp0, dp1, dp2, dproj = row_call(
        "merge_bwd", vjp_fn(f_merge, 4, 1), [Row(p[0]), Row(p[1]), Row(p[2]), Row(proj, GATES_W, GATES_OFF // GATES_W), Row(dmerged)],
        [], [(1024, MXU_DTYPE)] * 3, tm=256, into=dproj, into_cols=[(GATES_W, GATES_OFF)])
    dps = (dp0, dp1, dp2)
    dbr = [mm("branch_proj_bwd", dps[n], W["branch"][n], "nt", tm=1024, tn=1024, tk=1024,
              out_dtype=MXU_DTYPE if n == 0 else F32) for n in range(3)]
    g_branch = [mm("branch_proj_wgrad", S["branches"][n], dps[n], "tn", tm=1024, tn=1024, tk=2048, out_dtype=WIRE_DTYPE)
                for n in range(3)]
    dret_raw, dgla_raw, dgnorm, dproj = row_call(
        "normgate_bwd", vjp_fn(lambda a, b, c, d, g: _reorder_normgate(a, b, c, d, g), 4, 2),
        [Row(S["ret_raw"]), Row(S["gla_raw"]), Row(proj, 1024, COL["rg"] // 1024), Row(proj, 1024, COL["gg"] // 1024),
         Row(dbr[1]), Row(dbr[2])], [W["gnorm"]], [(1024, F32), (1024, F32)], [(1, LIN_VAL_DIM)], tm=256,
        into=dproj, into_cols=[(1024, COL["rg"]), (1024, COL["gg"])])
    side, landed = _rider(hooks, "gla_bwd")
    dgq2, dgk, dgv, dgla_la = lin_bwd("gla_bwd", Win(S["gq2"], 128), Win(proj, 128, COL["gk"] // 128),
                                      Win(proj, 256, COL["gv"] // 256), Win(S["gla_la"], 128), S["gla_st"], dgla_raw, True,
                                      side=side)
    landed()
    dwlr, dblr, dproj = row_call(
        "gla_pre_bwd", _gla_pre_bwd, [Row(proj, 512, COL["gq"] // 512), Row(proj, GLR_BLOCK, GLR_OFF // GLR_BLOCK),
                                      Row(dgq2), Row(dgla_la), Row(dgk), Row(dgv)], [W["wlr"], W["blr"]], [],
        [(GLR_BLOCK, 512), (1, 512)], tm=256, into=dproj,
        into_cols=[(512, COL["gq"]), (PROJ_PAD - GLR_OFF, GLR_OFF), (512, COL["gk"]), (1024, COL["gv"])])
    drq2, drk2, drv = lin_bwd("ret_bwd", Win(S["rq2"], 128), Win(S["rk2"], 128), Win(proj, 256, COL["rv"] // 256),
                              Win(ret_decay, 128), S["ret_st"], dret_raw, False)
    dproj, = row_call(
        "ret_pre_bwd", _ret_pre_bwd, [Row(proj, 512, COL["rq"] // 512), Row(proj, 512, COL["rk"] // 512), Row(cos), Row(sin),
                                      Row(drq2), Row(drk2), Row(drv)], [], [], tm=256, into=dproj,
        into_cols=[(512, COL["rq"]), (512, COL["rk"]), (1024, COL["rv"])])
    side, landed = _rider(hooks, "attn_bwd")
    dproj, dbias = attn_bwd(proj, *W["bias_tab"], dbr[0], dproj, side=side)
    landed()
    d_rel_bias = bias_table_grad(dbias)
    side, landed = _rider(hooks, "proj_wgrad", dict(branch=g_branch, out=g_out, up=g_up, down=g_down))
    g_in = mm("proj_wgrad", jnp.transpose(S["hb"]), dproj, "nn", tm=1024, tn=1280, tk=2048, out_dtype=WIRE_DTYPE, side=side)
    landed()
    side, landed = _rider(hooks, "proj_bwd", {"in": g_in})
    dh = mm("proj_bwd", dproj, W["in"], "nt", tm=1024, tn=1024, tk=2560, extras=(dha,), epi=add, side=side)
    landed()
    big = dict(g_in=g_in, branch=g_branch, out=g_out, up=g_up, down=g_down)
    small = dict(rel_bias=d_rel_bias, wlr=dwlr[:GLA_RANK], blr=dblr, gnorm=dgnorm, ln1_g=dg1, ln1_b=db1, ln2_g=dg2, ln2_b=db2)
    return dh, big, small


def _reorder_normgate(ret_raw, gla_raw, rg, gg, gnorm):
    return f_normgate(ret_raw, rg, gla_raw, gg, gnorm)


def _gla_pre_bwd(gq, glr, dq2, dla, dgk, dgv, wlr, blr):
    _, pull = jax.vjp(f_gla_pre, gq, glr, wlr, blr)
    dgq, dglr, dwlr, dblr = pull((dq2, dla))
    dglr = jnp.concatenate([dglr, jnp.zeros((dglr.shape[0], PROJ_PAD - GLR_OFF - GLR_BLOCK), F32)], axis=1)
    return dgq, dglr, dgk, dgv, dwlr, dblr


def _ret_pre_bwd(rq, rk, cos, sin, drq2, drk2, drv):
    _, pull = jax.vjp(lambda a, b: f_ret_pre(a, b, cos, sin), rq, rk)
    drq, drk = pull((drq2, drk2))
    return drq, drk, drv


SHARDED = ("in", "branch", "out", "up", "down")

_FULL = {
    "branch": lambda g: [g[:, n].reshape(D_MODEL, D_MODEL) for n in range(3)],
    "out": lambda g: g.reshape(D_MODEL, D_MODEL),
    "up": lambda g: jnp.transpose(g, (1, 0, 2)).reshape(D_MODEL, D_FF),
    "down": lambda g: g.reshape(D_FF, D_MODEL),
}


def _owner_pieces(s):
    shard = PROJ_WIDTH // N_DEV
    lo, hi = s * shard, (s + 1) * shard
    pieces = []
    for a, b, padded in ((0, GLR_SRC, 0), (GLR_SRC, GATES_SRC, GLR_OFF), (GATES_SRC, PROJ_WIDTH, GATES_OFF)):
        x0, x1 = max(lo, a), min(hi, b)
        if x0 < x1:
            pieces.append((x0 - lo, x1 - x0, padded + x0 - a))
    return pieces


def _full_in(g):
    placed = sorted((p, s, c, w) for s in range(N_DEV) for c, w, p in _owner_pieces(s))
    cols = [g[s][:, c:c + w] for _, s, c, w in placed]
    return jnp.concatenate(cols + [jnp.zeros((D_MODEL, PROJ_PAD - PROJ_WIDTH), g.dtype)], axis=1)


def _full_wlr(g):
    wlr = jnp.transpose(g, (1, 0, 2)).reshape(GLA_RANK, LIN_HEADS * LIN_KEY_DIM)
    return jnp.pad(wlr, ((0, GLR_BLOCK - GLA_RANK), (0, 0)))


def _slab_in(g):
    return jnp.stack([jnp.concatenate([g[:, p:p + w] for _, w, p in _owner_pieces(s)], axis=1) for s in range(N_DEV)])


_SLABS = {
    "in": _slab_in,
    "branch": lambda gs: jnp.stack([b.reshape(N_DEV, D_MODEL // N_DEV, D_MODEL) for b in gs], axis=1),
    "out": lambda g: g.reshape(N_DEV, D_MODEL // N_DEV, D_MODEL),
    "up": lambda g: g,
    "down": lambda g: g.reshape(N_DEV, D_FF // N_DEV, D_MODEL),
}


def _pair_sums(name, slabs, from_sibling=None):
    sums = []
    for mine, sibs in zip(slabs, from_sibling or carrier(name, Side("pair", slabs))):
        width = mine.shape[-1]
        rows = math.prod(mine.shape[1:-1])
        tm = 256 if rows % 256 == 0 else 128
        nb = rows // tm
        my_rows = Row(mine.reshape(-1, width), rowmap=lambda i, nb=nb: (2 * (i // nb) + lax.axis_index("c")) * nb + i % nb)
        out, = row_call("pair_sum", lambda q, p: (p.astype(F32) + q.astype(F32),),
                        [Row(sibs.reshape(-1, width)), my_rows], [], [(width, WIRE_DTYPE)], tm=tm)
        sums.append(out.reshape(sibs.shape))
    return sums


def _sum_adam(name, parts, w, m, v, layer, tm, earlier):
    C = w.shape[-1]
    rows = [Row(parts, lead=(s,)) for s in range(parts.shape[0])] + [Row(a, lead=(layer,)) for a in (w, m, v)]
    return row_call(name, f_sum_adam, rows, [], [(C, F32)] * 4, tm=tm, stack=(layer, DEPTH), stacked_onto=earlier)


def kernel(x, ln_in_g, ln_in_b, w_in, rel_bias, gla_w_lr, gla_b_lr, gla_norm_g, w_branch, w_out, ln1_g, ln1_b, w_up, w_down, ln2_g, ln2_b, loss_target, m_ln_in_g, m_ln_in_b, m_w_in, m_rel_bias, m_gla_w_lr, m_gla_b_lr, m_gla_norm_g, m_w_branch, m_w_out, m_ln1_g, m_ln1_b, m_w_up, m_w_down, m_ln2_g, m_ln2_b, v_ln_in_g, v_ln_in_b, v_w_in, v_rel_bias, v_gla_w_lr, v_gla_b_lr, v_gla_norm_g, v_w_branch, v_w_out, v_ln1_g, v_ln1_b, v_w_up, v_w_down, v_ln2_g, v_ln2_b):
    T = x.shape[1]
    x2d = x.reshape(T, D_MODEL)
    tgt = loss_target.reshape(T, D_MODEL)
    me = 4 * lax.axis_index("x") + 2 * lax.axis_index("y") + lax.axis_index("c")

    inv = ROPE_BASE ** (-jnp.arange(0, LIN_KEY_DIM, 2, dtype=F32) / LIN_KEY_DIM)
    ang = jnp.arange(T, dtype=F32)[:, None] * inv[None, :]
    cos = jnp.concatenate([jnp.cos(ang), jnp.cos(ang)], axis=1)
    sin = jnp.concatenate([-jnp.sin(ang), jnp.sin(ang)], axis=1)
    log_gamma = jnp.log1p(-jnp.exp2(-5.0 - jnp.arange(LIN_HEADS, dtype=F32)))
    ret_decay = jnp.repeat(log_gamma, LIN_KEY_DIM).reshape(1, LIN_HEADS * LIN_KEY_DIM)
    consts = (cos, sin, ret_decay)

    layers = [dict(rel_bias=rel_bias[l], blr=_row2(gla_b_lr[l]), gnorm=_row2(gla_norm_g[l]), ln1_g=_row2(ln1_g[l]),
                   ln1_b=_row2(ln1_b[l]), ln2_g=_row2(ln2_g[l]), ln2_b=_row2(ln2_b[l])) for l in range(DEPTH)]
    shards = dict(branch=w_branch, out=w_out, up=w_up, down=w_down)
    shards["in"] = w_in
    wire = [{n: shards[n][l].astype(WIRE_DTYPE) for n in SHARDED} for l in range(DEPTH)]

    def gather_of(l, names):
        def landed(res):
            for n, g in zip(names, res):
                layers[l][n] = _full_in(g) if n == "in" else _FULL[n](g)
        return lambda ctx: (Side("gather", [wire[l][n] for n in names]), landed)

    first = Side("gather", [wire[0]["in"]] + [gla_w_lr[l] for l in range(DEPTH)])
    bias_tabs = bias_table(rel_bias, side=first)
    layers[0]["in"] = _full_in(first.results[0])
    for l in range(DEPTH):
        layers[l]["wlr"] = _full_wlr(first.results[1 + l])
        layers[l]["bias_tab"] = (bias_tabs, l * ATTN_HEADS)

    fwd_hooks = [{"proj": gather_of(1, ("in",)), "attn_fwd": gather_of(0, ("up", "down")),
                  "ret_fwd": gather_of(0, ("branch", "out"))},
                 {"proj": gather_of(1, ("up", "down")), "attn_fwd": gather_of(1, ("branch", "out"))}]
    h, hb = row_call("ln_in", _twice(f_ln_in), [Row(x2d)], [_row2(ln_in_g), _row2(ln_in_b)], STREAM, tm=512)
    saved = []
    for l in range(DEPTH):
        h, hb, s = _layer_forward(h, hb, layers[l], consts, fwd_hooks[l])
        saved.append(s)
    dh, loss_lanes = row_call("loss", f_loss, [Row(h), Row(tgt)], [], [(D_MODEL, F32)], [(1, 128)], tm=512)

    received = [dict() for _ in range(DEPTH)]

    def scatter_of(l, names, sums=None):
        def make(ctx):
            parts = [sums[n] for n in names] if sums is not None else _pair_sums(
                "pair_grads", [_SLABS[n](ctx[n]) for n in names])

            def landed(res):
                received[l].update(zip(names, res))
            return Side("chips", parts), landed
        return make

    small = [None] * DEPTH
    dh, big, small[1] = _layer_backward(dh, saved[1], layers[1], consts, {})
    slabs1 = [_SLABS[n](big["g_in"] if n == "in" else big[n]) for n in SHARDED]
    sums1 = {}

    def pair_exchange_1(ctx):
        def landed(res):
            sums1.update(zip(SHARDED, _pair_sums("pair_grads", slabs1, from_sibling=res)))
        return Side("pair", slabs1), landed

    rest = SHARDED[1:]
    dh, big, small[0] = _layer_backward(dh, saved[0], layers[0], consts, {
        "mlp_down_bwd": pair_exchange_1, "gla_bwd": scatter_of(1, ("in",), sums1), "attn_bwd": scatter_of(1, rest, sums1),
        "proj_wgrad": scatter_of(0, rest), "proj_bwd": scatter_of(0, ("in",))})
    grad_x, d_in_g, d_in_b = row_call("ln_in_bwd", vjp_fn(f_ln_in, 1, 1), [Row(x2d), Row(dh)],
                                      [_row2(ln_in_g), _row2(ln_in_b)], [(D_MODEL, F32)], [(1, D_MODEL), (1, D_MODEL)], tm=256)

    small_names = ("rel_bias", "wlr", "blr", "gnorm", "ln1_g", "ln1_b", "ln2_g", "ln2_b")
    parts = [d_in_g, d_in_b] + [small[l][n] for l in range(DEPTH) for n in small_names] + [loss_lanes]
    shapes = [(D_MODEL,), (D_MODEL,)] + [s for l in range(DEPTH) for s in (
        (ATTN_HEADS, N_REL), (GLA_RANK, 512), (512,), (LIN_VAL_DIM,), (D_MODEL,), (D_MODEL,), (D_MODEL,), (D_MODEL,))] + [(128,)]
    red = _unpack(all_reduce_small(_pack(parts)), shapes)
    loss = jnp.sum(red[-1])
    g_small = {"ln_in_g": red[0], "ln_in_b": red[1]}
    for l in range(DEPTH):
        for j, n in enumerate(small_names):
            g_small[(l, n)] = red[2 + l * len(small_names) + j]
    shard = LIN_HEADS * LIN_KEY_DIM // N_DEV
    g_wlr = jnp.stack([lax.dynamic_slice_in_dim(g_small[(l, "wlr")], me * shard, shard, axis=1) for l in range(DEPTH)])
    stack = lambda n: jnp.stack([g_small[(l, n)] for l in range(DEPTH)])
    g_rep = [g_small["ln_in_g"], g_small["ln_in_b"], stack("rel_bias"), g_wlr, stack("blr"), stack("gnorm"),
             stack("ln1_g"), stack("ln1_b"), stack("ln2_g"), stack("ln2_b")]
    w_rep = [ln_in_g, ln_in_b, rel_bias, gla_w_lr, gla_b_lr, gla_norm_g, ln1_g, ln1_b, ln2_g, ln2_b]
    m_rep = [m_ln_in_g, m_ln_in_b, m_rel_bias, m_gla_w_lr, m_gla_b_lr, m_gla_norm_g, m_ln1_g, m_ln1_b, m_ln2_g, m_ln2_b]
    v_rep = [v_ln_in_g, v_ln_in_b, v_rel_bias, v_gla_w_lr, v_gla_b_lr, v_gla_norm_g, v_ln1_g, v_ln1_b, v_ln2_g, v_ln2_b]
    rep_shapes = [w.shape for w in w_rep]
    d_pack, m_pack, v_pack = row_call("adam_small", f_adam, [Row(_pack(g_rep)), Row(_pack(w_rep)), Row(_pack(m_rep)), Row(_pack(v_rep))],
                                      [], [(128, F32)] * 3, tm=256)
    d_rep, nm_rep, nv_rep = _unpack(d_pack, rep_shapes), _unpack(m_pack, rep_shapes), _unpack(v_pack, rep_shapes)

    def owned(key, w, m, v, name, tm):
        C = w.shape[-1]
        flat = lambda a: a.reshape(DEPTH, -1, C)
        outs = None
        for l in range(DEPTH):
            outs = _sum_adam(name, received[l][key].reshape(4, -1, C), flat(w), flat(m), flat(v), l, tm, outs)
        return [o.reshape(w.shape) for o in outs]

    o_in = owned("in", w_in, m_w_in, v_w_in, "adam_w_in", 256)
    o_br = owned("branch", w_branch, m_w_branch, v_w_branch, "adam_w_branch", 128)
    o_out = owned("out", w_out, m_w_out, v_w_out, "adam_w_out", 128)
    o_up = owned("up", w_up, m_w_up, v_w_up, "adam_w_up", 256)
    o_down = owned("down", w_down, m_w_down, v_w_down, "adam_w_down", 256)

    def tree(j, rep):
        return [rep[0], rep[1], o_in[j], rep[2], rep[3], rep[4], rep[5], o_br[j], o_out[j], rep[6], rep[7], o_up[j], o_down[j],
                rep[8], rep[9]]

    return (loss, grad_x.reshape(1, T, D_MODEL), *tree(0, g_rep), *tree(1, d_rep), *tree(2, nm_rep), *tree(3, nv_rep))
```

```python
import functools
import math

import jax
import jax.numpy as jnp
from jax import lax
from jax.experimental import pallas as pl
from jax.experimental.pallas import tpu as pltpu

F32 = jnp.float32
MXU_DTYPE = jnp.bfloat16
WIRE_DTYPE = jnp.bfloat16

N_DEV = 8
D_MODEL = 1024
DEPTH = 2
CHUNK = 64
ATTN_HEADS = 8
ATTN_HEAD_DIM = 128
ATTN_LEFT_CHUNKS = 8
BAND = (ATTN_LEFT_CHUNKS + 1) * CHUNK
PAD_ROWS = ATTN_LEFT_CHUNKS * CHUNK
REL_CLIP = 2 * CHUNK
N_REL = 2 * REL_CLIP + 1
REL_MIN = REL_CLIP - (CHUNK - 1)
LIN_HEADS = 4
LIN_KEY_DIM = 128
LIN_VAL_DIM = 256
GLA_RANK = 16
GLA_GATE_NORMALIZER = 16.0
ROPE_BASE = 10000.0
D_FF = 4 * D_MODEL
LN_EPS = 1e-5
NEG_INF = -1e30
ALPHA = (2 * DEPTH) ** 0.25
ADAM_LR, ADAM_B1, ADAM_B2, ADAM_EPS, ADAM_WD, ADAM_STEP = 0.001, 0.9, 0.999, 1e-08, 0.01, 10

PROJ_WIDTH = 12304
GLR_SRC = 9216
GATES_SRC = GLR_SRC + GLA_RANK
MAIN_W = 9216
GATES_W = 3 * D_MODEL
GATES_OFF = MAIN_W
GLR_OFF = MAIN_W + GATES_W
PROJ_PAD = 12800
GLR_BLOCK = 128
COL = dict(aq=0, ak=1024, av=2048, rq=3072, rk=3584, rv=4096, rg=5120, gq=6144, gk=6656, gv=7168, gg=8192)

SEG_CHUNKS = 8
SEG = SEG_CHUNKS * CHUNK
VMEM_BIG = 56 * 1024 * 1024

_NN = (((1,), (0,)), ((), ()))
_NT = (((1,), (1,)), ((), ()))
_TN = (((0,), (0,)), ((), ()))


def _mxu(a, b, dn):
    return lax.dot_general(a.astype(MXU_DTYPE), b.astype(MXU_DTYPE), dn, preferred_element_type=F32)


@jax.custom_vjp
def dot_nn(a, b):
    return _mxu(a, b, _NN)


def _dot_nn_fwd(a, b):
    return _mxu(a, b, _NN), (a, b)


def _dot_nn_bwd(res, g):
    a, b = res
    return _mxu(g, b, _NT).astype(a.dtype), _mxu(a, g, _TN).astype(b.dtype)


dot_nn.defvjp(_dot_nn_fwd, _dot_nn_bwd)


@jax.custom_vjp
def dot_nt(a, b):
    return _mxu(a, b, _NT)


def _dot_nt_fwd(a, b):
    return _mxu(a, b, _NT), (a, b)


def _dot_nt_bwd(res, g):
    a, b = res
    return _mxu(g, b, _NN).astype(a.dtype), _mxu(g, a, _TN).astype(b.dtype)


dot_nt.defvjp(_dot_nt_fwd, _dot_nt_bwd)


@jax.custom_vjp
def dot_tn(a, b):
    return _mxu(a, b, _TN)


def _dot_tn_fwd(a, b):
    return _mxu(a, b, _TN), (a, b)


def _dot_tn_bwd(res, g):
    a, b = res
    return _mxu(b, g, _NT).astype(a.dtype), _mxu(a, g, _NN).astype(b.dtype)


dot_tn.defvjp(_dot_tn_fwd, _dot_tn_bwd)


def _tri(n, lower):
    r = lax.broadcasted_iota(jnp.int32, (n, n), 0)
    c = lax.broadcasted_iota(jnp.int32, (n, n), 1)
    return (r >= c) if lower else (r <= c)


def _tri_dot(lower, x):
    tri = _tri(x.shape[0], lower).astype(jnp.bfloat16)
    hi = x.astype(jnp.bfloat16)
    rest = x - hi.astype(F32)
    mid = rest.astype(jnp.bfloat16)
    lo = (rest - mid.astype(F32)).astype(jnp.bfloat16)
    dot = lambda part: lax.dot_general(tri, part, _NN, preferred_element_type=F32)
    return dot(hi) + dot(mid) + dot(lo)


@jax.custom_vjp
def chunk_cumsum(x):
    return _tri_dot(True, x)


def _chunk_cumsum_fwd(x):
    return chunk_cumsum(x), None


def _chunk_cumsum_bwd(_, g):
    return (_tri_dot(False, g),)


chunk_cumsum.defvjp(_chunk_cumsum_fwd, _chunk_cumsum_bwd)


@jax.custom_vjp
def swap_halves(t):
    return pltpu.roll(t, t.shape[-1] // 2, t.ndim - 1)


def _swap_fwd(t):
    return swap_halves(t), None


def _swap_bwd(_, g):
    return (swap_halves(g),)


swap_halves.defvjp(_swap_fwd, _swap_bwd)


def mm(name, a, b, dims, *, tm, tn, tk, out_dtype=F32, extras=(), epi=None, n_out=1, out_dtypes=None, side=None,
       column_slabs=False):
    if dims == "nn":
        (M, K), N = a.shape, b.shape[1]
    elif dims == "nt":
        (M, K), N = a.shape, b.shape[0]
    else:
        (K, M), N = a.shape, b.shape[1]
    tm, tn, tk = min(tm, M), min(tn, N), min(tk, K)
    assert M % tm == 0 and N % tn == 0 and K % tk == 0, (name, M, N, K, tm, tn, tk)
    nk = K // tk
    dn = {"nn": _NN, "nt": _NT, "tn": _TN}[dims]
    a_spec = pl.BlockSpec((tk, tm), lambda i, j, k: (k, i)) if dims == "tn" else pl.BlockSpec((tm, tk), lambda i, j, k: (i, k))
    b_spec = pl.BlockSpec((tn, tk), lambda i, j, k: (j, k)) if dims == "nt" else pl.BlockSpec((tk, tn), lambda i, j, k: (k, j))
    o_spec = pl.BlockSpec((tm, tn), lambda i, j, k: (i, j))
    n_ex = len(extras)
    out_dtypes = out_dtypes or (out_dtype,) * n_out

    def finish(r, ex_refs, o_refs):
        if epi is not None:
            r = epi(r, *[e[...] for e in ex_refs])
        r = r if isinstance(r, tuple) else (r,)
        for o, v in zip(o_refs, r):
            o[...] = v.astype(o.dtype)

    def body(*refs):
        a_ref, b_ref = refs[:2]
        ex_refs = refs[2:2 + n_ex]
        o_refs = refs[2 + n_ex:2 + n_ex + n_out]
        part = _mxu(a_ref[...], b_ref[...], dn)
        if nk == 1:
            finish(part, ex_refs, o_refs)
            return
        acc = refs[-1]
        k = pl.program_id(2)

        @pl.when(k == 0)
        def _():
            acc[...] = part

        @pl.when(k > 0)
        def _():
            acc[...] += part

        @pl.when(k == nk - 1)
        def _():
            finish(acc[...], ex_refs, o_refs)

    outs = _side_call(
        side, body,
        name=name,
        grid=(M // tm, N // tn, nk),
        in_specs=[a_spec, b_spec] + [o_spec] * n_ex,
        out_specs=[pl.BlockSpec((None, tm, tn), lambda i, j, k: (j, i, 0))] * n_out if column_slabs else [o_spec] * n_out,
        out_shape=[jax.ShapeDtypeStruct((N // tn, M, tn) if column_slabs else (M, N), dt) for dt in out_dtypes],
        scratch_shapes=[pltpu.VMEM((tm, tn), F32)] if nk > 1 else [],
        args=[a, b, *extras],
        semantics=("parallel", "parallel", "arbitrary"),
    )
    return outs[0] if n_out == 1 else outs


class Row:
    def __init__(self, arr, width=None, col=0, lead=(), rowmap=None):
        self.arr, self.lead, self.col = arr, tuple(lead), col
        self.width = arr.shape[-1] if width is None else width
        self.rowmap = rowmap

    def spec(self, tm):
        lead, col, rowmap = self.lead, self.col, self.rowmap or (lambda i: i)
        return pl.BlockSpec((None,) * len(lead) + (tm, self.width), lambda i: (*lead, rowmap(i), col))


def row_call(name, fn, rows, params, new_outs, red_shapes=(), *, tm, into=None, into_cols=(), stack=None, stacked_onto=None):
    R = rows[0].arr.shape[len(rows[0].lead)]
    tm = min(tm, R)
    assert R % tm == 0, (name, R, tm)
    n_row, n_par, n_new, n_into, n_red = len(rows), len(params), len(new_outs), len(into_cols), len(red_shapes)
    fresh = isinstance(into, jax.ShapeDtypeStruct)

    def body(*refs):
        row_refs = refs[:n_row]
        par_refs = refs[n_row:n_row + n_par]
        pos = n_row + n_par + (1 if (into is not None and not fresh) else 0) + (n_new if stacked_onto is not None else 0)
        new_refs = refs[pos:pos + n_new]
        red_refs = refs[pos + n_new:pos + n_new + n_red]
        pos2 = pos + n_new + n_red
        into_ref = refs[pos2] if into is not None else None
        stage = refs[pos2 + 1:pos2 + 1 + n_into] if into is not None else ()
        i = pl.program_id(0)
        vals = fn(*[r[...] for r in row_refs], *[p[...] for p in par_refs])
        vals = vals if isinstance(vals, (tuple, list)) else (vals,)
        for o, v in zip(new_refs, vals[:n_new]):
            o[...] = v.astype(o.dtype)

        def window_copy(j, step):
            width, off = into_cols[j]
            rows_at = pl.ds(pl.multiple_of(step * tm, tm), tm)
            return pltpu.make_async_copy(stage[j], into_ref.at[rows_at, pl.ds(off, width)], refs[pos2 + 1 + n_into].at[j])

        for j, v in enumerate(vals[n_new:n_new + n_into]):
            @pl.when(i > 0)
            def _(j=j):
                window_copy(j, i - 1).wait()

            stage[j][...] = v.astype(stage[j].dtype)
            window_copy(j, i).start()

            @pl.when(i == R // tm - 1)
            def _(j=j):
                window_copy(j, i).wait()
        for o, v in zip(red_refs, vals[n_new + n_into:]):
            @pl.when(i == 0)
            def _(o=o, v=v):
                o[...] = v.astype(o.dtype)

            @pl.when(i > 0)
            def _(o=o, v=v):
                o[...] += v.astype(o.dtype)

    in_specs = [r.spec(tm) for r in rows] + [pl.BlockSpec(p.shape, lambda i, nd=p.ndim: (0,) * nd) for p in params]
    args = [r.arr for r in rows] + list(params)
    if stack is None:
        out_specs = [pl.BlockSpec((tm, w), lambda i: (i, 0)) for w, _ in new_outs]
        out_shape = [jax.ShapeDtypeStruct((R, w), dt) for w, dt in new_outs]
    else:
        index, depth = stack
        out_specs = [pl.BlockSpec((None, tm, w), lambda i: (index, i, 0)) for w, _ in new_outs]
        out_shape = [jax.ShapeDtypeStruct((depth, R, w), dt) for w, dt in new_outs]
    out_specs += [pl.BlockSpec(s, lambda i, nd=len(s): (0,) * nd) for s in red_shapes]
    out_shape += [jax.ShapeDtypeStruct(s, F32) for s in red_shapes]
    aliases, scratch = {}, []
    if into is not None:
        out_specs.append(pl.BlockSpec(memory_space=pl.ANY))
        out_shape.append(jax.ShapeDtypeStruct(into.shape, into.dtype))
        if not fresh:
            in_specs.append(pl.BlockSpec(memory_space=pl.ANY))
            args.append(into)
            aliases = {len(args) - 1: len(out_shape) - 1}
        scratch = [pltpu.VMEM((tm, w), into.dtype) for w, _ in into_cols] + [pltpu.SemaphoreType.DMA((n_into,))]
    if stacked_onto is not None:
        for j, prev in enumerate(stacked_onto):
            in_specs.append(pl.BlockSpec(memory_space=pl.ANY))
            args.append(prev)
            aliases[len(args) - 1] = j
    outs = pl.pallas_call(
        body,
        name=name,
        grid=(R // tm,),
        in_specs=in_specs,
        out_specs=out_specs,
        out_shape=out_shape,
        scratch_shapes=scratch,
        input_output_aliases=aliases,
        compiler_params=pltpu.CompilerParams(dimension_semantics=("arbitrary",), vmem_limit_bytes=VMEM_BIG),
    )(*args)
    return list(outs)


def vjp_fn(fn, n_row, n_out, passthrough=0):
    def bwd(*args):
        rows = [a.astype(F32) for a in args[:n_row]]
        cts = args[n_row:n_row + n_out]
        thru = args[n_row + n_out:n_row + n_out + passthrough]
        pars = args[n_row + n_out + passthrough:]
        outs, pull = jax.vjp(lambda *xs: tuple(fn(*xs)), *rows, *pars)
        grads = pull(tuple(c.astype(o.dtype) for c, o in zip(cts, outs)))
        return tuple(grads[:n_row]) + tuple(thru) + tuple(grads[n_row:])

    return bwd


def _ln(x, g, b):
    mu = jnp.mean(x, axis=-1, keepdims=True)
    xc = x - mu
    var = jnp.mean(xc * xc, axis=-1, keepdims=True)
    return xc * lax.rsqrt(var + LN_EPS) * g + b


def f_ln_in(x, g, b):
    return (_ln(x, g, b),)


def f_deepnorm(x, y, g, b):
    return (_ln(ALPHA * x + y, g, b),)


def _twice(fn):
    return lambda *args: fn(*args) * 2


STREAM = [(D_MODEL, F32), (D_MODEL, MXU_DTYPE)]


def _rope(t, cos, sin):
    parts = []
    for h in range(LIN_HEADS):
        th = t[:, h * LIN_KEY_DIM:(h + 1) * LIN_KEY_DIM]
        parts.append(th * cos + swap_halves(th) * sin)
    return jnp.concatenate(parts, axis=1)


def f_ret_pre(rq, rk, cos, sin):
    return _rope(rq, cos, sin), _rope(rk, cos, sin) * LIN_KEY_DIM ** -0.5


def _log_sigmoid(x):
    return jnp.minimum(x, 0.0) - jnp.log(1.0 + jnp.exp(-jnp.abs(x)))


def f_gla_pre(gq, glr, wlr, blr):
    gate_pre = dot_nn(glr, wlr) + blr
    return gq * LIN_KEY_DIM ** -0.5, _log_sigmoid(gate_pre) / GLA_GATE_NORMALIZER


def f_normgate(ret_raw, rg, gla_raw, gg, gnorm):
    rets, glas = [], []
    for h in range(LIN_HEADS):
        sl = slice(h * LIN_VAL_DIM, (h + 1) * LIN_VAL_DIM)
        t = ret_raw[:, sl]
        mu = jnp.mean(t, axis=-1, keepdims=True)
        tc = t - mu
        rets.append(tc * lax.rsqrt(jnp.mean(tc * tc, axis=-1, keepdims=True) + LN_EPS))
        u = gla_raw[:, sl]
        glas.append(u * lax.rsqrt(jnp.mean(u * u, axis=-1, keepdims=True) + LN_EPS) * gnorm)
    ret = jax.nn.silu(rg) * jnp.concatenate(rets, axis=1)
    gla = jax.nn.silu(gg) * jnp.concatenate(glas, axis=1)
    return ret, gla


def f_merge(p0, p1, p2, gl):
    out = jax.nn.sigmoid(gl[:, :D_MODEL]) * p0
    out = out + jax.nn.sigmoid(gl[:, D_MODEL:2 * D_MODEL]) * p1
    return (out + jax.nn.sigmoid(gl[:, 2 * D_MODEL:]) * p2,)


def f_loss(y, tgt):
    diff = y - tgt
    sq = jnp.sum(diff * diff, axis=0, keepdims=True)
    lanes = sq[:, :128]
    for j in range(1, D_MODEL // 128):
        lanes = lanes + sq[:, j * 128:(j + 1) * 128]
    return diff * (1.0 / D_MODEL), lanes * (0.5 / D_MODEL)


def _adam(g, w, m, v):
    m = ADAM_B1 * m + (1.0 - ADAM_B1) * g
    v = ADAM_B2 * v + (1.0 - ADAM_B2) * jnp.square(g)
    m_hat = m / (1.0 - ADAM_B1 ** ADAM_STEP)
    v_hat = v / (1.0 - ADAM_B2 ** ADAM_STEP)
    delta = -ADAM_LR * (m_hat / (jnp.sqrt(v_hat) + ADAM_EPS) + ADAM_WD * w)
    return delta, m, v


def f_sum_adam(*args):
    g = args[0].astype(F32)
    for p in args[1:-3]:
        g = g + p.astype(F32)
    return (g,) + _adam(g, *args[-3:])


def f_adam(g, w, m, v):
    return _adam(g, w, m, v)


GROUP = 4
GROUP_ROWS = GROUP * CHUNK
GROUP_KEYS = (GROUP + ATTN_LEFT_CHUNKS) * CHUNK
DIAG_ROWS = REL_CLIP + CHUNK
GRAD_ROWS = 200


def _blocks_by_delta():
    blocks = {}
    for qi in range(GROUP):
        for J in range(GROUP_KEYS // 128):
            blocks.setdefault(qi - 2 * J, []).append((qi, J))
    return blocks


def bias_table(rel_bias, side=None):
    n_tab = rel_bias.shape[0] * ATTN_HEADS

    def body(rb_ref, o_ref):
        h = pl.program_id(0)
        rm = lax.broadcasted_iota(jnp.int32, (CHUNK, 128), 0)
        cm = lax.broadcasted_iota(jnp.int32, (CHUNK, 128), 1)
        ahead = jnp.where(cm >= CHUNK, 1, 0)
        for delta, where in _blocks_by_delta().items():
            base = 64 * delta + PAD_ROWS
            rel = jnp.minimum(base + rm - cm, REL_CLIP) + REL_CLIP
            in_band = (ahead - delta >= 0) & (ahead - delta <= ATTN_LEFT_CHUNKS)
            lo = max(REL_MIN, min(base - 127, REL_CLIP) + REL_CLIP)
            hi = min(base + CHUNK - 1, REL_CLIP) + REL_CLIP

            def step(r, acc, rel=rel):
                return jnp.where(rel == r, rb_ref[h * N_REL + r], acc)

            blk = jnp.zeros((CHUNK, 128), F32) if hi < lo else lax.fori_loop(lo, hi + 1, step, jnp.zeros((CHUNK, 128), F32))
            blk = jnp.where(in_band, blk, NEG_INF)
            for qi, J in where:
                o_ref[pl.ds(qi * CHUNK, CHUNK), pl.ds(J * 128, 128)] = blk

    return _side_call(
        side, body,
        name="bias_table",
        grid=(n_tab,),
        in_specs=[pl.BlockSpec(memory_space=pltpu.SMEM)],
        out_specs=[pl.BlockSpec((None, GROUP_ROWS, GROUP_KEYS), lambda h: (h, 0, 0))],
        out_shape=[jax.ShapeDtypeStruct((n_tab, GROUP_ROWS, GROUP_KEYS), F32)],
        scratch_shapes=[],
        args=[rel_bias.reshape(n_tab * N_REL)],
    )[0]


def bias_table_grad(dbias):
    blocks = _blocks_by_delta()
    below_clip = range(-(CHUNK - 1), REL_CLIP)
    deltas = [d for d in sorted(blocks) if any(-127 <= pos - 64 * d - PAD_ROWS <= CHUNK - 1 for pos in below_clip)]

    def body(d_ref, o_ref, diag):
        sums = []
        for delta in deltas:
            acc = None
            for qi, J in blocks[delta]:
                blk = d_ref[pl.ds(qi * CHUNK, CHUNK), pl.ds(J * 128, 128)]
                acc = blk if acc is None else acc + blk
            sums.append(acc)
        off = lax.broadcasted_iota(jnp.int32, (CHUNK, 128), 0) - lax.broadcasted_iota(jnp.int32, (CHUNK, 128), 1)
        diag[...] = jnp.zeros_like(diag)

        def step(t, carry):
            pos = t - (CHUNK - 1)
            picked = jnp.zeros((CHUNK, 128), F32)
            for delta, s in zip(deltas, sums):
                picked = picked + jnp.where(off == pos - 64 * delta - PAD_ROWS, s, 0.0)
            diag[pl.ds(t, 1), :] = jnp.sum(picked, axis=0, keepdims=True)
            return carry

        lax.fori_loop(0, DIAG_ROWS, step, 0, unroll=8)
        d = d_ref[...]
        total = jnp.sum(jnp.sum(d, axis=1, keepdims=True), axis=0, keepdims=True)
        o_ref[pl.ds(0, DIAG_ROWS), :] = jnp.broadcast_to(jnp.sum(diag[...], axis=1, keepdims=True), (DIAG_ROWS, 128))
        o_ref[pl.ds(DIAG_ROWS, GRAD_ROWS - DIAG_ROWS), :] = jnp.broadcast_to(total, (GRAD_ROWS - DIAG_ROWS, 128))

    out = pl.pallas_call(
        body,
        name="bias_table_grad",
        grid=(ATTN_HEADS,),
        in_specs=[pl.BlockSpec((None, GROUP_ROWS, GROUP_KEYS), lambda h: (h, 0, 0))],
        out_specs=pl.BlockSpec((None, GRAD_ROWS, 128), lambda h: (h, 0, 0)),
        out_shape=jax.ShapeDtypeStruct((ATTN_HEADS, GRAD_ROWS, 128), F32),
        scratch_shapes=[pltpu.VMEM((DIAG_ROWS, 128), F32)],
        compiler_params=pltpu.CompilerParams(dimension_semantics=("arbitrary",)),
    )(dbias)
    below = out[:, :DIAG_ROWS - 1, 0]
    clipped = out[:, DIAG_ROWS, 0] - jnp.sum(below, axis=1)
    return jnp.concatenate([jnp.zeros((ATTN_HEADS, REL_MIN), F32), below, clipped[:, None]], axis=1)


def _attn_stage(q_ref, k_ref, v_ref, qs, kp, vp, T):
    qs[...] = (q_ref[...] * ATTN_HEAD_DIM ** -0.5).astype(qs.dtype)
    for dst, src in ((kp, k_ref), (vp, v_ref)):
        dst[pl.ds(0, PAD_ROWS), :] = jnp.zeros((PAD_ROWS, ATTN_HEAD_DIM), dst.dtype)
        dst[pl.ds(PAD_ROWS, T), :] = src[...].astype(dst.dtype)


def _attn_probs(g, first_groups, qs, kp, b_ref):
    r0 = g * GROUP_ROWS if first_groups else pl.multiple_of(g * GROUP_ROWS, GROUP_ROWS)
    qc = qs[pl.ds(r0, GROUP_ROWS), :]
    kb = kp[pl.ds(r0, GROUP_KEYS), :]
    s = _mxu(qc, kb, _NT) + b_ref[...]
    if first_groups:
        k_chunk = lax.shift_right_logical(lax.broadcasted_iota(jnp.int32, (GROUP_ROWS, GROUP_KEYS), 1), 6)
        s = jnp.where(g * GROUP + k_chunk >= ATTN_LEFT_CHUNKS, s, NEG_INF)
    e = jnp.exp(s - jnp.max(s, axis=-1, keepdims=True))
    return r0, qc, kb, e / jnp.sum(e, axis=-1, keepdims=True)


def _for_groups(T, step):
    n_first = min(ATTN_LEFT_CHUNKS // GROUP, T // GROUP_ROWS)
    for g in range(n_first):
        step(g, True)
    n_loop = T // GROUP_ROWS - n_first
    lax.fori_loop(n_first, T // GROUP_ROWS, lambda g, c: (step(g, False), c)[1], 0, unroll=2 if n_loop % 2 == 0 else 1)


def _head_spec(T, base):
    return pl.BlockSpec((T, ATTN_HEAD_DIM), lambda h: (0, base + h))


def _attn_scratch(T):
    return [pltpu.VMEM((T, ATTN_HEAD_DIM), MXU_DTYPE)] + [pltpu.VMEM((T + PAD_ROWS, ATTN_HEAD_DIM), MXU_DTYPE)] * 2


def attn_fwd(proj, bias_tab, first_tab, side=None):
    T = proj.shape[0]

    def body(q_ref, k_ref, v_ref, b_ref, o_ref, qs, kp, vp):
        _attn_stage(q_ref, k_ref, v_ref, qs, kp, vp, T)

        def step(g, first):
            r0, _, _, p = _attn_probs(g, first, qs, kp, b_ref)
            o_ref[pl.ds(r0, GROUP_ROWS), :] = _mxu(p, vp[pl.ds(r0, GROUP_KEYS), :], _NN).astype(o_ref.dtype)

        _for_groups(T, step)

    out, = _side_call(
        side, body,
        name="attn_fwd",
        grid=(ATTN_HEADS,),
        in_specs=[_head_spec(T, COL["aq"] // 128), _head_spec(T, COL["ak"] // 128), _head_spec(T, COL["av"] // 128),
                  pl.BlockSpec((None, GROUP_ROWS, GROUP_KEYS), lambda h: (first_tab + h, 0, 0))],
        out_specs=[pl.BlockSpec((T, ATTN_HEAD_DIM), lambda h: (0, h))],
        out_shape=[jax.ShapeDtypeStruct((T, ATTN_HEADS * ATTN_HEAD_DIM), MXU_DTYPE)],
        scratch_shapes=_attn_scratch(T),
        args=[proj, proj, proj, bias_tab],
    )
    return out


def attn_bwd(proj, bias_tab, first_tab, dout, dproj, side=None):
    T = proj.shape[0]

    def body(q_ref, k_ref, v_ref, b_ref, do_ref, dproj_in, dbias_ref, dproj_ref, qs, kp, vp, dkp, dvp, dq, stage):
        del dproj_in
        h = pl.program_id(0)
        _attn_stage(q_ref, k_ref, v_ref, qs, kp, vp, T)
        dkp[...] = jnp.zeros_like(dkp)
        dvp[...] = jnp.zeros_like(dvp)
        dbias_ref[...] = jnp.zeros_like(dbias_ref)

        def step(g, first):
            r0, qc, kb, p = _attn_probs(g, first, qs, kp, b_ref)
            do = do_ref[pl.ds(r0, GROUP_ROWS), :]
            dvp[pl.ds(r0, GROUP_KEYS), :] += _mxu(p, do, _TN)
            dp = _mxu(do, vp[pl.ds(r0, GROUP_KEYS), :], _NT)
            ds = p * (dp - jnp.sum(dp * p, axis=-1, keepdims=True))
            dq[pl.ds(r0, GROUP_ROWS), :] = _mxu(ds, kb, _NN) * ATTN_HEAD_DIM ** -0.5
            dkp[pl.ds(r0, GROUP_KEYS), :] += _mxu(ds, qc, _TN)
            dbias_ref[...] += ds

        _for_groups(T, step)
        col = pl.multiple_of(h * ATTN_HEAD_DIM, ATTN_HEAD_DIM)
        for grad, first_row, base in ((dq, 0, "aq"), (dkp, PAD_ROWS, "ak"), (dvp, PAD_ROWS, "av")):
            stage[...] = grad[pl.ds(first_row, T), :].astype(stage.dtype)
            pltpu.sync_copy(stage, dproj_ref.at[:, pl.ds(COL[base] + col, ATTN_HEAD_DIM)])

    dbias, dproj = _side_call(
        side, body,
        name="attn_bwd",
        grid=(ATTN_HEADS,),
        in_specs=[_head_spec(T, COL["aq"] // 128), _head_spec(T, COL["ak"] // 128), _head_spec(T, COL["av"] // 128),
                  pl.BlockSpec((None, GROUP_ROWS, GROUP_KEYS), lambda h: (first_tab + h, 0, 0)),
                  pl.BlockSpec((T, ATTN_HEAD_DIM), lambda h: (0, h)),
                  pl.BlockSpec(memory_space=pl.ANY)],
        out_specs=[pl.BlockSpec((None, GROUP_ROWS, GROUP_KEYS), lambda h: (h, 0, 0)), pl.BlockSpec(memory_space=pl.ANY)],
        out_shape=[jax.ShapeDtypeStruct((ATTN_HEADS, GROUP_ROWS, GROUP_KEYS), F32),
                   jax.ShapeDtypeStruct(dproj.shape, dproj.dtype)],
        scratch_shapes=_attn_scratch(T) + [pltpu.VMEM((T + PAD_ROWS, ATTN_HEAD_DIM), F32)] * 2
        + [pltpu.VMEM((T, ATTN_HEAD_DIM), F32), pltpu.VMEM((T, ATTN_HEAD_DIM), dproj.dtype)],
        args=[proj, proj, proj, bias_tab, dout, dproj],
        aliases={5: 1},
    )
    return dproj, dbias


def _lin_segment(q, k, v, la, st):
    lower = _tri(CHUNK, True)
    steps = (lax.broadcasted_iota(jnp.int32, (CHUNK, 1), 0) + 1).astype(F32)
    outs = []
    for i in range(SEG_CHUNKS):
        sl = slice(i * CHUNK, (i + 1) * CHUNK)
        qi, ki, vi = q[sl], k[sl], v[sl]
        if la.shape[0] == 1:
            cum, last = la * steps, la * float(CHUNK)
        else:
            cum, last = chunk_cumsum(la[sl]), jnp.sum(la[sl], axis=0, keepdims=True)
        e_pos, e_neg = jnp.exp(cum), jnp.exp(-cum)
        q_fwd = qi * e_pos
        s_fwd = dot_nt(q_fwd, ki * e_neg)
        s_bwd = dot_nt(qi * e_neg, ki * e_pos)
        o = dot_nn(jnp.where(lower, s_fwd, s_bwd), vi) + dot_nt(q_fwd, st)
        st = jnp.exp(last) * st + dot_tn(vi, ki * jnp.exp(last - cum))
        outs.append(o)
    return jnp.concatenate(outs, axis=0), st


class Win:
    def __init__(self, arr, width, base=0):
        self.arr, self.width, self.base = arr, width, base


HEADS_PER_STEP = 2


def _lin_spec(w, seg_of):
    assert w.base % HEADS_PER_STEP == 0
    first, width = w.base // HEADS_PER_STEP, HEADS_PER_STEP * w.width
    if w.arr.shape[0] == 1:
        return pl.BlockSpec((1, width), lambda p, s: (0, first + p))
    return pl.BlockSpec((SEG, width), lambda p, s: (seg_of(s), first + p))


def _head_cols(ref, j, width):
    return ref[:, j * width:(j + 1) * width]


def _decay_rows(la_ref, j):
    return _head_cols(la_ref, j, LIN_KEY_DIM)


_STATE_BLOCK = (HEADS_PER_STEP, None, LIN_VAL_DIM, LIN_KEY_DIM)


def lin_fwd(name, q, k, v, la, side=None):
    T = q.arr.shape[0]
    nseg = T // SEG

    def body(q_ref, k_ref, v_ref, la_ref, o_ref, st_ref, st):
        @pl.when(pl.program_id(1) == 0)
        def _():
            st[...] = jnp.zeros_like(st)

        for j in range(HEADS_PER_STEP):
            st0 = st[j]
            st_ref[j] = st0
            o, st1 = _lin_segment(_head_cols(q_ref, j, LIN_KEY_DIM), _head_cols(k_ref, j, LIN_KEY_DIM),
                                  _head_cols(v_ref, j, LIN_VAL_DIM), _decay_rows(la_ref, j), st0)
            o_ref[:, j * LIN_VAL_DIM:(j + 1) * LIN_VAL_DIM] = o
            st[j] = st1

    fwd = lambda s: s
    out, states = _side_call(
        side, body,
        name=name,
        grid=(LIN_HEADS // HEADS_PER_STEP, nseg),
        in_specs=[_lin_spec(w, fwd) for w in (q, k, v, la)],
        out_specs=[pl.BlockSpec((SEG, HEADS_PER_STEP * LIN_VAL_DIM), lambda p, s: (s, p)),
                   pl.BlockSpec(_STATE_BLOCK, lambda p, s: (p, s, 0, 0))],
        out_shape=[jax.ShapeDtypeStruct((T, LIN_HEADS * LIN_VAL_DIM), F32),
                   jax.ShapeDtypeStruct((LIN_HEADS, nseg, LIN_VAL_DIM, LIN_KEY_DIM), F32)],
        scratch_shapes=[pltpu.VMEM((HEADS_PER_STEP, LIN_VAL_DIM, LIN_KEY_DIM), F32)],
        args=[q.arr, k.arr, v.arr, la.arr],
    )
    return out, states


def lin_bwd(name, q, k, v, la, states, dout, with_decay_grad, side=None):
    T = q.arr.shape[0]
    nseg = T // SEG
    n_out = 4 if with_decay_grad else 3

    widths = [LIN_KEY_DIM, LIN_KEY_DIM, LIN_VAL_DIM] + ([LIN_KEY_DIM] if with_decay_grad else [])

    def body(q_ref, k_ref, v_ref, la_ref, st_ref, do_ref, *rest):
        outs, dst = rest[:n_out], rest[n_out]

        @pl.when(pl.program_id(1) == 0)
        def _():
            dst[...] = jnp.zeros_like(dst)

        for j in range(HEADS_PER_STEP):
            _, pull = jax.vjp(_lin_segment, _head_cols(q_ref, j, LIN_KEY_DIM).astype(F32),
                              _head_cols(k_ref, j, LIN_KEY_DIM).astype(F32), _head_cols(v_ref, j, LIN_VAL_DIM).astype(F32),
                              _decay_rows(la_ref, j), st_ref[j])
            grads = pull((_head_cols(do_ref, j, LIN_VAL_DIM), dst[j]))
            for o, g, w in zip(outs, grads[:n_out], widths):
                o[:, j * w:(j + 1) * w] = g
            dst[j] = grads[4]

    rev = lambda s: nseg - 1 - s
    key_spec = pl.BlockSpec((SEG, HEADS_PER_STEP * LIN_KEY_DIM), lambda p, s: (rev(s), p))
    val_spec = pl.BlockSpec((SEG, HEADS_PER_STEP * LIN_VAL_DIM), lambda p, s: (rev(s), p))
    out_specs = [key_spec, key_spec, val_spec] + ([key_spec] if with_decay_grad else [])
    return _side_call(
        side, body,
        name=name,
        grid=(LIN_HEADS // HEADS_PER_STEP, nseg),
        in_specs=[_lin_spec(w, rev) for w in (q, k, v, la)] + [
            pl.BlockSpec(_STATE_BLOCK, lambda p, s: (p, rev(s), 0, 0)), val_spec],
        out_specs=out_specs,
        out_shape=[jax.ShapeDtypeStruct((T, LIN_HEADS * w), F32) for w in widths],
        scratch_shapes=[pltpu.VMEM((HEADS_PER_STEP, LIN_VAL_DIM, LIN_KEY_DIM), F32)],
        args=[q.arr, k.arr, v.arr, la.arr, states, dout],
    )


def _coords():
    return lax.axis_index("x"), lax.axis_index("y"), lax.axis_index("c")


def _peer(k):
    x, y, c = _coords()
    px = 1 - x if k & 4 else x
    py = 1 - y if k & 2 else y
    pc = 1 - c if k & 1 else c
    return (px, py, pc), 4 * px + 2 * py + pc


def _remote(src, dst, send_sem, recv_sem, device):
    return pltpu.make_async_remote_copy(src_ref=src, dst_ref=dst, send_sem=send_sem, recv_sem=recv_sem,
                                        device_id=device, device_id_type=pl.DeviceIdType.MESH)


class Side:
    def __init__(self, kind, arrays):
        self.kind, self.arrays, self.n = kind, list(arrays), len(arrays)
        self.outs_per = 1
        self.slots = {"gather": N_DEV - 1, "pair": 4, "chips": 3}[kind]

    def out_shape(self):
        if self.kind == "gather":
            return [jax.ShapeDtypeStruct((N_DEV,) + a.shape, a.dtype) for a in self.arrays]
        return [jax.ShapeDtypeStruct((4,) + a.shape[1:], a.dtype) for a in self.arrays]

    def sems(self):
        return [pltpu.SemaphoreType.DMA((self.n, self.slots)), pltpu.SemaphoreType.DMA((self.n, self.slots)),
                pltpu.SemaphoreType.DMA((self.n,))]

    def start(self, ins, outs, sems):
        send, recv, own = sems
        x, y, c = _coords()
        me = 4 * x + 2 * y + c
        for a in range(self.n):
            if self.kind == "gather":
                pltpu.make_async_copy(ins[a], outs[a].at[me], own.at[a]).start()
                for k in (1, 2, 4, 6):
                    peer, _ = _peer(k)
                    _remote(ins[a], outs[a].at[me], send.at[a, k - 1], recv.at[a, k - 1], peer).start()
            elif self.kind == "pair":
                for j in range(4):
                    _remote(ins[a].at[2 * j + 1 - c], outs[a].at[j], send.at[a, j], recv.at[a, j], (x, y, 1 - c)).start()
            else:
                chip = 2 * x + y
                pltpu.make_async_copy(ins[a].at[chip], outs[a].at[chip], own.at[a]).start()
                for d in (1, 2, 3):
                    tx, ty = (1 - x if d & 2 else x), (1 - y if d & 1 else y)
                    _remote(ins[a].at[2 * tx + ty], outs[a].at[chip], send.at[a, d - 1], recv.at[a, d - 1], (tx, ty, c)).start()

    def finish(self, ins, outs, sems):
        send, recv, own = sems
        x, y, c = _coords()
        me = 4 * x + 2 * y + c
        sibling = (x, y, 1 - c)
        for a in range(self.n):
            if self.kind == "gather":
                blk = outs[a].at[me]
                for k in (2, 4, 6):
                    _, pidx = _peer(k)
                    _remote(blk, blk, send.at[a, k - 1], recv.at[a, k - 1], sibling).wait_recv()
                    _remote(outs[a].at[pidx], outs[a].at[pidx], send.at[a, k], recv.at[a, k], sibling).start()
                pltpu.make_async_copy(ins[a], blk, own.at[a]).wait()
                for k in range(1, N_DEV):
                    _remote(blk, blk, send.at[a, k - 1], recv.at[a, k - 1], sibling).wait_send()
                for k in (1, 3, 5, 7):
                    _remote(blk, blk, send.at[a, k - 1], recv.at[a, k - 1], sibling).wait_recv()
            elif self.kind == "pair":
                for j in range(4):
                    _remote(ins[a].at[j], outs[a].at[j], send.at[a, j], recv.at[a, j], sibling).wait()
            else:
                slab = outs[a].at[0]
                pltpu.make_async_copy(slab, slab, own.at[a]).wait()
                for d in (1, 2, 3):
                    _remote(slab, slab, send.at[a, d - 1], recv.at[a, d - 1], sibling).wait()


def _side_call(side, body, *, name, grid, in_specs, out_specs, out_shape, scratch_shapes, args, semantics=None,
               aliases=None):
    in_specs, out_specs, out_shape = list(in_specs), list(out_specs), list(out_shape)
    scratch_shapes, args = list(scratch_shapes), list(args)
    n_in, n_out = len(in_specs), len(out_specs)
    kernel_body = body
    if side is not None:
        n_so = side.n * side.outs_per

        def kernel_body(*refs):
            ins, side_ins = refs[:n_in], refs[n_in:n_in + side.n]
            pos = n_in + side.n
            outs, side_outs = refs[pos:pos + n_out], refs[pos + n_out:pos + n_out + n_so]
            rest = refs[pos + n_out + n_so:]
            scratch, sems = rest[:-3], rest[-3:]
            ids = [pl.program_id(ax) for ax in range(len(grid))]
            first = functools.reduce(jnp.logical_and, [i == 0 for i in ids])
            last = functools.reduce(jnp.logical_and, [i == g - 1 for i, g in zip(ids, grid)])

            @pl.when(first)
            def _():
                side.start(side_ins, side_outs, sems)

            body(*ins, *outs, *scratch)

            @pl.when(last)
            def _():
                side.finish(side_ins, side_outs, sems)

        any_spec = pl.BlockSpec(memory_space=pl.ANY)
        in_specs += [any_spec] * side.n
        args += side.arrays
        out_specs += [any_spec] * n_so
        out_shape += side.out_shape()
        scratch_shapes += side.sems()
        semantics = ("arbitrary",) * len(grid)
    outs = pl.pallas_call(
        kernel_body,
        name=name,
        grid=grid,
        in_specs=in_specs,
        out_specs=out_specs,
        out_shape=out_shape,
        scratch_shapes=scratch_shapes,
        input_output_aliases=aliases or {},
        compiler_params=pltpu.CompilerParams(dimension_semantics=semantics or ("arbitrary",) * len(grid),
                                             vmem_limit_bytes=VMEM_BIG),
    )(*args)
    if side is not None:
        side.results = list(outs[n_out:])
    return list(outs[:n_out])


def carrier(name, side):
    _side_call(side, lambda: None, name=name, grid=(1,), in_specs=[], out_specs=[], out_shape=[], scratch_shapes=[], args=[])
    return side.results


def all_reduce_small(vec):
    def body(v_ref, o_ref, land, send_sems, recv_sems):
        x, y, c = _coords()
        me = 4 * x + 2 * y + c
        land[me] = v_ref[...]
        copies = []
        for k in range(1, N_DEV):
            peer, _ = _peer(k)
            cp = pltpu.make_async_remote_copy(
                src_ref=v_ref, dst_ref=land.at[me], send_sem=send_sems.at[k - 1], recv_sem=recv_sems.at[k - 1],
                device_id=peer, device_id_type=pl.DeviceIdType.MESH)
            cp.start()
            copies.append(cp)
        for cp in copies:
            cp.wait()
        total = land[0]
        for s in range(1, N_DEV):
            total = total + land[s]
        o_ref[...] = total

    return pl.pallas_call(
        body,
        name="all_reduce_small",
        in_specs=[pl.BlockSpec(memory_space=pltpu.VMEM)],
        out_specs=pl.BlockSpec(memory_space=pltpu.VMEM),
        out_shape=jax.ShapeDtypeStruct(vec.shape, F32),
        scratch_shapes=[pltpu.VMEM((N_DEV,) + vec.shape, F32), pltpu.SemaphoreType.DMA((N_DEV - 1,)),
                        pltpu.SemaphoreType.DMA((N_DEV - 1,))],
    )(vec)


def _pack(parts):
    flat = jnp.concatenate([p.reshape(-1).astype(F32) for p in parts])
    n = flat.shape[0]
    rows = -(-n // 1024) * 8
    return jnp.pad(flat, (0, rows * 128 - n)).reshape(rows, 128)


def _unpack(packed, shapes):
    flat, out, pos = packed.reshape(-1), [], 0
    for s in shapes:
        n = math.prod(s)
        out.append(flat[pos:pos + n].reshape(s))
        pos += n
    return out


def _row2(v):
    return v.reshape(1, -1)


def _rider(hooks, name, ctx=None):
    if name not in hooks:
        return None, lambda: None
    side, done = hooks[name](ctx)
    return side, lambda: done(side.results)


def _layer_forward(h, hb, W, consts, hooks):
    T = h.shape[0]
    cos, sin, ret_decay = consts
    side, landed = _rider(hooks, "proj")
    proj = mm("proj", hb, W["in"], "nn", tm=4096, tn=640, tk=1024, side=side)
    landed()
    bias_tab, first_tab = W["bias_tab"]
    side, landed = _rider(hooks, "attn_fwd")
    attn = attn_fwd(proj, bias_tab, first_tab, side=side)
    landed()
    rq2, rk2 = row_call("ret_pre", f_ret_pre, [Row(proj, 512, COL["rq"] // 512), Row(proj, 512, COL["rk"] // 512),
                                              Row(cos), Row(sin)], [], [(512, F32), (512, F32)], tm=512)
    side, landed = _rider(hooks, "ret_fwd")
    ret_raw, ret_st = lin_fwd("ret_fwd", Win(rq2, 128), Win(rk2, 128), Win(proj, 256, COL["rv"] // 256), Win(ret_decay, 128),
                              side=side)
    landed()
    gq2, gla_la = row_call("gla_pre", f_gla_pre, [Row(proj, 512, COL["gq"] // 512), Row(proj, GLR_BLOCK, GLR_OFF // GLR_BLOCK)],
                           [W["wlr"], W["blr"]], [(512, F32), (512, F32)], tm=512)
    side, landed = _rider(hooks, "gla_fwd")
    gla_raw, gla_st = lin_fwd("gla_fwd", Win(gq2, 128), Win(proj, 128, COL["gk"] // 128), Win(proj, 256, COL["gv"] // 256),
                              Win(gla_la, 128), side=side)
    landed()
    ret_b, gla_b = row_call("normgate", f_normgate, [Row(ret_raw), Row(proj, 1024, COL["rg"] // 1024), Row(gla_raw),
                                                    Row(proj, 1024, COL["gg"] // 1024)], [W["gnorm"]],
                            [(1024, MXU_DTYPE), (1024, MXU_DTYPE)], tm=256)
    branches = (attn, ret_b, gla_b)
    p = [mm("branch_proj", branches[n], W["branch"][n], "nn", tm=1024, tn=1024, tk=1024) for n in range(3)]
    merged, = row_call("merge", f_merge, [Row(p[0]), Row(p[1]), Row(p[2]), Row(proj, GATES_W, GATES_OFF // GATES_W)], [],
                       [(1024, MXU_DTYPE)], tm=256)
    mix = mm("out_proj", merged, W["out"], "nn", tm=1024, tn=1024, tk=1024)
    x1, x1b = row_call("ln1", _twice(f_deepnorm), [Row(h), Row(mix)], [W["ln1_g"], W["ln1_b"]], STREAM, tm=512)
    side, landed = _rider(hooks, "mlp_up")
    u, act = mm("mlp_up", x1b, W["up"], "nn", tm=2048, tn=512, tk=1024, n_out=2, out_dtypes=(F32, MXU_DTYPE),
                epi=lambda r: (r, jnp.square(jnp.maximum(r, 0.0))), side=side)
    landed()
    y = mm("mlp_down", act, W["down"], "nn", tm=1024, tn=1024, tk=2048)
    x2, x2b = row_call("ln2", _twice(f_deepnorm), [Row(x1), Row(y)], [W["ln2_g"], W["ln2_b"]], STREAM, tm=512)
    saved = dict(h=h, hb=hb, proj=proj, rq2=rq2, rk2=rk2, ret_st=ret_st, gq2=gq2, gla_la=gla_la,
                 gla_st=gla_st, ret_raw=ret_raw, gla_raw=gla_raw, branches=branches, p=p, merged=merged, mix=mix, x1=x1, x1b=x1b,
                 u=u, act=act, y=y)
    return x2, x2b, saved


def _layer_backward(dx2, S, W, consts, hooks):
    cos, sin, ret_decay = consts
    h, proj, T = S["h"], S["proj"], S["h"].shape[0]
    add = lambda r, e: r + e
    dx1a, dy, dg2, db2 = row_call("ln2_bwd", vjp_fn(f_deepnorm, 2, 1), [Row(S["x1"]), Row(S["y"]), Row(dx2)],
                                  [W["ln2_g"], W["ln2_b"]], [(1024, F32), (1024, MXU_DTYPE)], [(1, 1024), (1, 1024)], tm=256)
    side, landed = _rider(hooks, "mlp_down_bwd")
    du = mm("mlp_down_bwd", dy, W["down"], "nt", tm=1024, tn=512, tk=1024, extras=(S["u"],),
            epi=lambda r, u: r * (2.0 * jnp.maximum(u, 0.0)), out_dtype=MXU_DTYPE, side=side)
    landed()
    g_down = mm("mlp_down_wgrad", S["act"], dy, "tn", tm=1024, tn=1024, tk=2048, out_dtype=WIRE_DTYPE)
    dx1 = mm("mlp_up_bwd", du, W["up"], "nt", tm=1024, tn=1024, tk=2048, extras=(dx1a,), epi=add)
    g_up = mm("mlp_up_wgrad", S["x1b"], du, "tn", tm=1024, tn=D_FF // N_DEV, tk=2048, out_dtype=WIRE_DTYPE,
              column_slabs=True)
    dha, dmix, dg1, db1 = row_call("ln1_bwd", vjp_fn(f_deepnorm, 2, 1), [Row(h), Row(S["mix"]), Row(dx1)],
                                   [W["ln1_g"], W["ln1_b"]], [(1024, F32), (1024, MXU_DTYPE)], [(1, 1024), (1, 1024)], tm=256)
    dmerged = mm("out_proj_bwd", dmix, W["out"], "nt", tm=1024, tn=1024, tk=1024)
    g_out = mm("out_proj_wgrad", S["merged"], dmix, "tn", tm=1024, tn=1024, tk=2048, out_dtype=WIRE_DTYPE)
    dproj = jax.ShapeDtypeStruct((T, PROJ_PAD), MXU_DTYPE)
    p = S["p"]
    dp0, dp1, dp2, dproj = row_call(
        "merge_bwd", vjp_fn(f_merge, 4, 1), [Row(p[0]), Row(p[1]), Row(p[2]), Row(proj, GATES_W, GATES_OFF // GATES_W), Row(dmerged)],
        [], [(1024, MXU_DTYPE)] * 3, tm=256, into=dproj, into_cols=[(GATES_W, GATES_OFF)])
    dps = (dp0, dp1, dp2)
    dbr = [mm("branch_proj_bwd", dps[n], W["branch"][n], "nt", tm=1024, tn=1024, tk=1024,
              out_dtype=MXU_DTYPE if n == 0 else F32) for n in range(3)]
    g_branch = [mm("branch_proj_wgrad", S["branches"][n], dps[n], "tn", tm=1024, tn=1024, tk=2048, out_dtype=WIRE_DTYPE)
                for n in range(3)]
    dret_raw, dgla_raw, dgnorm, dproj = row_call(
        "normgate_bwd", vjp_fn(lambda a, b, c, d, g: _reorder_normgate(a, b, c, d, g), 4, 2),
        [Row(S["ret_raw"]), Row(S["gla_raw"]), Row(proj, 1024, COL["rg"] // 1024), Row(proj, 1024, COL["gg"] // 1024),
         Row(dbr[1]), Row(dbr[2])], [W["gnorm"]], [(1024, F32), (1024, F32)], [(1, LIN_VAL_DIM)], tm=256,
        into=dproj, into_cols=[(1024, COL["rg"]), (1024, COL["gg"])])
    side, landed = _rider(hooks, "gla_bwd")
    dgq2, dgk, dgv, dgla_la = lin_bwd("gla_bwd", Win(S["gq2"], 128), Win(proj, 128, COL["gk"] // 128),
                                      Win(proj, 256, COL["gv"] // 256), Win(S["gla_la"], 128), S["gla_st"], dgla_raw, True,
                                      side=side)
    landed()
    dwlr, dblr, dproj = row_call(
        "gla_pre_bwd", _gla_pre_bwd, [Row(proj, 512, COL["gq"] // 512), Row(proj, GLR_BLOCK, GLR_OFF // GLR_BLOCK),
                                      Row(dgq2), Row(dgla_la), Row(dgk), Row(dgv)], [W["wlr"], W["blr"]], [],
        [(GLR_BLOCK, 512), (1, 512)], tm=256, into=dproj,
        into_cols=[(512, COL["gq"]), (PROJ_PAD - GLR_OFF, GLR_OFF), (512, COL["gk"]), (1024, COL["gv"])])
    drq2, drk2, drv = lin_bwd("ret_bwd", Win(S["rq2"], 128), Win(S["rk2"], 128), Win(proj, 256, COL["rv"] // 256),
                              Win(ret_decay, 128), S["ret_st"], dret_raw, False)
    dproj, = row_call(
        "ret_pre_bwd", _ret_pre_bwd, [Row(proj, 512, COL["rq"] // 512), Row(proj, 512, COL["rk"] // 512), Row(cos), Row(sin),
                                      Row(drq2), Row(drk2), Row(drv)], [], [], tm=256, into=dproj,
        into_cols=[(512, COL["rq"]), (512, COL["rk"]), (1024, COL["rv"])])
    side, landed = _rider(hooks, "attn_bwd")
    dproj, dbias = attn_bwd(proj, *W["bias_tab"], dbr[0], dproj, side=side)
    landed()
    d_rel_bias = bias_table_grad(dbias)
    side, landed = _rider(hooks, "proj_wgrad", dict(branch=g_branch, out=g_out, up=g_up, down=g_down))
    g_in = mm("proj_wgrad", S["hb"], dproj, "tn", tm=1024, tn=1280, tk=2048, out_dtype=WIRE_DTYPE, side=side)
    landed()
    side, landed = _rider(hooks, "proj_bwd", {"in": g_in})
    dh = mm("proj_bwd", dproj, W["in"], "nt", tm=1024, tn=1024, tk=2560, extras=(dha,), epi=add, side=side)
    landed()
    big = dict(g_in=g_in, branch=g_branch, out=g_out, up=g_up, down=g_down)
    small = dict(rel_bias=d_rel_bias, wlr=dwlr[:GLA_RANK], blr=dblr, gnorm=dgnorm, ln1_g=dg1, ln1_b=db1, ln2_g=dg2, ln2_b=db2)
    return dh, big, small


def _reorder_normgate(ret_raw, gla_raw, rg, gg, gnorm):
    return f_normgate(ret_raw, rg, gla_raw, gg, gnorm)


def _gla_pre_bwd(gq, glr, dq2, dla, dgk, dgv, wlr, blr):
    _, pull = jax.vjp(f_gla_pre, gq, glr, wlr, blr)
    dgq, dglr, dwlr, dblr = pull((dq2, dla))
    dglr = jnp.concatenate([dglr, jnp.zeros((dglr.shape[0], PROJ_PAD - GLR_OFF - GLR_BLOCK), F32)], axis=1)
    return dgq, dglr, dgk, dgv, dwlr, dblr


def _ret_pre_bwd(rq, rk, cos, sin, drq2, drk2, drv):
    _, pull = jax.vjp(lambda a, b: f_ret_pre(a, b, cos, sin), rq, rk)
    drq, drk = pull((drq2, drk2))
    return drq, drk, drv


SHARDED = ("in", "branch", "out", "up", "down")

_FULL = {
    "branch": lambda g: [g[:, n].reshape(D_MODEL, D_MODEL) for n in range(3)],
    "out": lambda g: g.reshape(D_MODEL, D_MODEL),
    "up": lambda g: jnp.transpose(g, (1, 0, 2)).reshape(D_MODEL, D_FF),
    "down": lambda g: g.reshape(D_FF, D_MODEL),
}


def _owner_pieces(s):
    shard = PROJ_WIDTH // N_DEV
    lo, hi = s * shard, (s + 1) * shard
    pieces = []
    for a, b, padded in ((0, GLR_SRC, 0), (GLR_SRC, GATES_SRC, GLR_OFF), (GATES_SRC, PROJ_WIDTH, GATES_OFF)):
        x0, x1 = max(lo, a), min(hi, b)
        if x0 < x1:
            pieces.append((x0 - lo, x1 - x0, padded + x0 - a))
    return pieces


def _full_in(g):
    placed = sorted((p, s, c, w) for s in range(N_DEV) for c, w, p in _owner_pieces(s))
    cols = [g[s][:, c:c + w] for _, s, c, w in placed]
    return jnp.concatenate(cols + [jnp.zeros((D_MODEL, PROJ_PAD - PROJ_WIDTH), g.dtype)], axis=1)


def _full_wlr(g):
    wlr = jnp.transpose(g, (1, 0, 2)).reshape(GLA_RANK, LIN_HEADS * LIN_KEY_DIM)
    return jnp.pad(wlr, ((0, GLR_BLOCK - GLA_RANK), (0, 0)))


def _slab_in(g):
    return jnp.stack([jnp.concatenate([g[:, p:p + w] for _, w, p in _owner_pieces(s)], axis=1) for s in range(N_DEV)])


_SLABS = {
    "in": _slab_in,
    "branch": lambda gs: jnp.stack([b.reshape(N_DEV, D_MODEL // N_DEV, D_MODEL) for b in gs], axis=1),
    "out": lambda g: g.reshape(N_DEV, D_MODEL // N_DEV, D_MODEL),
    "up": lambda g: g,
    "down": lambda g: g.reshape(N_DEV, D_FF // N_DEV, D_MODEL),
}


def _pair_sums(name, slabs, from_sibling=None):
    sums = []
    for mine, sibs in zip(slabs, from_sibling or carrier(name, Side("pair", slabs))):
        width = mine.shape[-1]
        rows = math.prod(mine.shape[1:-1])
        tm = 256 if rows % 256 == 0 else 128
        nb = rows // tm
        my_rows = Row(mine.reshape(-1, width), rowmap=lambda i, nb=nb: (2 * (i // nb) + lax.axis_index("c")) * nb + i % nb)
        out, = row_call("pair_sum", lambda q, p: (p.astype(F32) + q.astype(F32),),
                        [Row(sibs.reshape(-1, width)), my_rows], [], [(width, WIRE_DTYPE)], tm=tm)
        sums.append(out.reshape(sibs.shape))
    return sums


def _sum_adam(name, parts, w, m, v, layer, tm, earlier):
    C = w.shape[-1]
    rows = [Row(parts, lead=(s,)) for s in range(parts.shape[0])] + [Row(a, lead=(layer,)) for a in (w, m, v)]
    return row_call(name, f_sum_adam, rows, [], [(C, F32)] * 4, tm=tm, stack=(layer, DEPTH), stacked_onto=earlier)


def kernel(x, ln_in_g, ln_in_b, w_in, rel_bias, gla_w_lr, gla_b_lr, gla_norm_g, w_branch, w_out, ln1_g, ln1_b, w_up, w_down, ln2_g, ln2_b, loss_target, m_ln_in_g, m_ln_in_b, m_w_in, m_rel_bias, m_gla_w_lr, m_gla_b_lr, m_gla_norm_g, m_w_branch, m_w_out, m_ln1_g, m_ln1_b, m_w_up, m_w_down, m_ln2_g, m_ln2_b, v_ln_in_g, v_ln_in_b, v_w_in, v_rel_bias, v_gla_w_lr, v_gla_b_lr, v_gla_norm_g, v_w_branch, v_w_out, v_ln1_g, v_ln1_b, v_w_up, v_w_down, v_ln2_g, v_ln2_b):
    T = x.shape[1]
    x2d = x.reshape(T, D_MODEL)
    tgt = loss_target.reshape(T, D_MODEL)
    me = 4 * lax.axis_index("x") + 2 * lax.axis_index("y") + lax.axis_index("c")

    inv = ROPE_BASE ** (-jnp.arange(0, LIN_KEY_DIM, 2, dtype=F32) / LIN_KEY_DIM)
    ang = jnp.arange(T, dtype=F32)[:, None] * inv[None, :]
    cos = jnp.concatenate([jnp.cos(ang), jnp.cos(ang)], axis=1)
    sin = jnp.concatenate([-jnp.sin(ang), jnp.sin(ang)], axis=1)
    log_gamma = jnp.log1p(-jnp.exp2(-5.0 - jnp.arange(LIN_HEADS, dtype=F32)))
    ret_decay = jnp.repeat(log_gamma, LIN_KEY_DIM).reshape(1, LIN_HEADS * LIN_KEY_DIM)
    consts = (cos, sin, ret_decay)

    layers = [dict(rel_bias=rel_bias[l], blr=_row2(gla_b_lr[l]), gnorm=_row2(gla_norm_g[l]), ln1_g=_row2(ln1_g[l]),
                   ln1_b=_row2(ln1_b[l]), ln2_g=_row2(ln2_g[l]), ln2_b=_row2(ln2_b[l])) for l in range(DEPTH)]
    shards = dict(branch=w_branch, out=w_out, up=w_up, down=w_down)
    shards["in"] = w_in
    wire = [{n: shards[n][l].astype(WIRE_DTYPE) for n in SHARDED} for l in range(DEPTH)]

    def gather_of(l, names):
        def landed(res):
            for n, g in zip(names, res):
                layers[l][n] = _full_in(g) if n == "in" else _FULL[n](g)
        return lambda ctx: (Side("gather", [wire[l][n] for n in names]), landed)

    first = Side("gather", [wire[0]["in"]] + [gla_w_lr[l] for l in range(DEPTH)])
    bias_tabs = bias_table(rel_bias, side=first)
    layers[0]["in"] = _full_in(first.results[0])
    for l in range(DEPTH):
        layers[l]["wlr"] = _full_wlr(first.results[1 + l])
        layers[l]["bias_tab"] = (bias_tabs, l * ATTN_HEADS)

    fwd_hooks = [{"proj": gather_of(1, ("in",)), "attn_fwd": gather_of(0, ("up", "down")),
                  "ret_fwd": gather_of(0, ("branch", "out"))},
                 {"proj": gather_of(1, ("up", "down")), "attn_fwd": gather_of(1, ("branch", "out"))}]
    h, hb = row_call("ln_in", _twice(f_ln_in), [Row(x2d)], [_row2(ln_in_g), _row2(ln_in_b)], STREAM, tm=512)
    saved = []
    for l in range(DEPTH):
        h, hb, s = _layer_forward(h, hb, layers[l], consts, fwd_hooks[l])
        saved.append(s)
    dh, loss_lanes = row_call("loss", f_loss, [Row(h), Row(tgt)], [], [(D_MODEL, F32)], [(1, 128)], tm=512)

    received = [dict() for _ in range(DEPTH)]

    def scatter_of(l, names, sums=None):
        def make(ctx):
            parts = [sums[n] for n in names] if sums is not None else _pair_sums(
                "pair_grads", [_SLABS[n](ctx[n]) for n in names])

            def landed(res):
                received[l].update(zip(names, res))
            return Side("chips", parts), landed
        return make

    small = [None] * DEPTH
    dh, big, small[1] = _layer_backward(dh, saved[1], layers[1], consts, {})
    slabs1 = [_SLABS[n](big["g_in"] if n == "in" else big[n]) for n in SHARDED]
    sums1 = {}

    def pair_exchange_1(ctx):
        def landed(res):
            sums1.update(zip(SHARDED, _pair_sums("pair_grads", slabs1, from_sibling=res)))
        return Side("pair", slabs1), landed

    rest = SHARDED[1:]
    dh, big, small[0] = _layer_backward(dh, saved[0], layers[0], consts, {
        "mlp_down_bwd": pair_exchange_1, "gla_bwd": scatter_of(1, ("in",), sums1), "attn_bwd": scatter_of(1, rest, sums1),
        "proj_wgrad": scatter_of(0, rest), "proj_bwd": scatter_of(0, ("in",))})
    grad_x, d_in_g, d_in_b = row_call("ln_in_bwd", vjp_fn(f_ln_in, 1, 1), [Row(x2d), Row(dh)],
                                      [_row2(ln_in_g), _row2(ln_in_b)], [(D_MODEL, F32)], [(1, D_MODEL), (1, D_MODEL)], tm=256)

    small_names = ("rel_bias", "wlr", "blr", "gnorm", "ln1_g", "ln1_b", "ln2_g", "ln2_b")
    parts = [d_in_g, d_in_b] + [small[l][n] for l in range(DEPTH) for n in small_names] + [loss_lanes]
    shapes = [(D_MODEL,), (D_MODEL,)] + [s for l in range(DEPTH) for s in (
        (ATTN_HEADS, N_REL), (GLA_RANK, 512), (512,), (LIN_VAL_DIM,), (D_MODEL,), (D_MODEL,), (D_MODEL,), (D_MODEL,))] + [(128,)]
    red = _unpack(all_reduce_small(_pack(parts)), shapes)
    loss = jnp.sum(red[-1])
    g_small = {"ln_in_g": red[0], "ln_in_b": red[1]}
    for l in range(DEPTH):
        for j, n in enumerate(small_names):
            g_small[(l, n)] = red[2 + l * len(small_names) + j]
    shard = LIN_HEADS * LIN_KEY_DIM // N_DEV
    g_wlr = jnp.stack([lax.dynamic_slice_in_dim(g_small[(l, "wlr")], me * shard, shard, axis=1) for l in range(DEPTH)])
    stack = lambda n: jnp.stack([g_small[(l, n)] for l in range(DEPTH)])
    g_rep = [g_small["ln_in_g"], g_small["ln_in_b"], stack("rel_bias"), g_wlr, stack("blr"), stack("gnorm"),
             stack("ln1_g"), stack("ln1_b"), stack("ln2_g"), stack("ln2_b")]
    w_rep = [ln_in_g, ln_in_b, rel_bias, gla_w_lr, gla_b_lr, gla_norm_g, ln1_g, ln1_b, ln2_g, ln2_b]
    m_rep = [m_ln_in_g, m_ln_in_b, m_rel_bias, m_gla_w_lr, m_gla_b_lr, m_gla_norm_g, m_ln1_g, m_ln1_b, m_ln2_g, m_ln2_b]
    v_rep = [v_ln_in_g, v_ln_in_b, v_rel_bias, v_gla_w_lr, v_gla_b_lr, v_gla_norm_g, v_ln1_g, v_ln1_b, v_ln2_g, v_ln2_b]
    rep_shapes = [w.shape for w in w_rep]
    d_pack, m_pack, v_pack = row_call("adam_small", f_adam, [Row(_pack(g_rep)), Row(_pack(w_rep)), Row(_pack(m_rep)), Row(_pack(v_rep))],
                                      [], [(128, F32)] * 3, tm=256)
    d_rep, nm_rep, nv_rep = _unpack(d_pack, rep_shapes), _unpack(m_pack, rep_shapes), _unpack(v_pack, rep_shapes)

    def owned(key, w, m, v, name, tm):
        C = w.shape[-1]
        flat = lambda a: a.reshape(DEPTH, -1, C)
        outs = None
        for l in range(DEPTH):
            outs = _sum_adam(name, received[l][key].reshape(4, -1, C), flat(w), flat(m), flat(v), l, tm, outs)
        return [o.reshape(w.shape) for o in outs]

    o_in = owned("in", w_in, m_w_in, v_w_in, "adam_w_in", 256)
    o_br = owned("branch", w_branch, m_w_branch, v_w_branch, "adam_w_branch", 128)
    o_out = owned("out", w_out, m_w_out, v_w_out, "adam_w_out", 128)
    o_up = owned("up", w_up, m_w_up, v_w_up, "adam_w_up", 256)
    o_down = owned("down", w_down, m_w_down, v_w_down, "adam_w_down", 256)

    def tree(j, rep):
        return [rep[0], rep[1], o_in[j], rep[2], rep[3], rep[4], rep[5], o_br[j], o_out[j], rep[6], rep[7], o_up[j], o_down[j],
                rep[8], rep[9]]

    return (loss, grad_x.reshape(1, T, D_MODEL), *tree(0, g_rep), *tree(1, d_rep), *tree(2, nm_rep), *tree(3, nv_rep))
```

```python
import functools
import math

import jax
import jax.numpy as jnp
from jax import lax
from jax.experimental import pallas as pl
from jax.experimental.pallas import tpu as pltpu

F32 = jnp.float32
MXU_DTYPE = jnp.bfloat16
WIRE_DTYPE = jnp.bfloat16

N_DEV = 8
D_MODEL = 1024
DEPTH = 2
CHUNK = 64
ATTN_HEADS = 8
ATTN_HEAD_DIM = 128
ATTN_LEFT_CHUNKS = 8
BAND = (ATTN_LEFT_CHUNKS + 1) * CHUNK
PAD_ROWS = ATTN_LEFT_CHUNKS * CHUNK
REL_CLIP = 2 * CHUNK
N_REL = 2 * REL_CLIP + 1
REL_MIN = REL_CLIP - (CHUNK - 1)
LIN_HEADS = 4
LIN_KEY_DIM = 128
LIN_VAL_DIM = 256
GLA_RANK = 16
GLA_GATE_NORMALIZER = 16.0
ROPE_BASE = 10000.0
D_FF = 4 * D_MODEL
LN_EPS = 1e-5
NEG_INF = -1e30
ALPHA = (2 * DEPTH) ** 0.25
ADAM_LR, ADAM_B1, ADAM_B2, ADAM_EPS, ADAM_WD, ADAM_STEP = 0.001, 0.9, 0.999, 1e-08, 0.01, 10

PROJ_WIDTH = 12304
GLR_SRC = 9216
GATES_SRC = GLR_SRC + GLA_RANK
MAIN_W = 9216
GATES_W = 3 * D_MODEL
GATES_OFF = MAIN_W
GLR_OFF = MAIN_W + GATES_W
PROJ_PAD = 12800
GLR_BLOCK = 128
COL = dict(aq=0, ak=1024, av=2048, rq=3072, rk=3584, rv=4096, rg=5120, gq=6144, gk=6656, gv=7168, gg=8192)

SEG_CHUNKS = 8
SEG = SEG_CHUNKS * CHUNK
VMEM_BIG = 56 * 1024 * 1024

_NN = (((1,), (0,)), ((), ()))
_NT = (((1,), (1,)), ((), ()))
_TN = (((0,), (0,)), ((), ()))


def _mxu(a, b, dn):
    return lax.dot_general(a.astype(MXU_DTYPE), b.astype(MXU_DTYPE), dn, preferred_element_type=F32)


@jax.custom_vjp
def dot_nn(a, b):
    return _mxu(a, b, _NN)


def _dot_nn_fwd(a, b):
    return _mxu(a, b, _NN), (a, b)


def _dot_nn_bwd(res, g):
    a, b = res
    return _mxu(g, b, _NT).astype(a.dtype), _mxu(a, g, _TN).astype(b.dtype)


dot_nn.defvjp(_dot_nn_fwd, _dot_nn_bwd)


@jax.custom_vjp
def dot_nt(a, b):
    return _mxu(a, b, _NT)


def _dot_nt_fwd(a, b):
    return _mxu(a, b, _NT), (a, b)


def _dot_nt_bwd(res, g):
    a, b = res
    return _mxu(g, b, _NN).astype(a.dtype), _mxu(g, a, _TN).astype(b.dtype)


dot_nt.defvjp(_dot_nt_fwd, _dot_nt_bwd)


@jax.custom_vjp
def dot_tn(a, b):
    return _mxu(a, b, _TN)


def _dot_tn_fwd(a, b):
    return _mxu(a, b, _TN), (a, b)


def _dot_tn_bwd(res, g):
    a, b = res
    return _mxu(b, g, _NT).astype(a.dtype), _mxu(a, g, _NN).astype(b.dtype)


dot_tn.defvjp(_dot_tn_fwd, _dot_tn_bwd)


def _tri(n, lower):
    r = lax.broadcasted_iota(jnp.int32, (n, n), 0)
    c = lax.broadcasted_iota(jnp.int32, (n, n), 1)
    return (r >= c) if lower else (r <= c)


def _tri_dot(lower, x):
    tri = _tri(x.shape[0], lower).astype(jnp.bfloat16)
    hi = x.astype(jnp.bfloat16)
    rest = x - hi.astype(F32)
    mid = rest.astype(jnp.bfloat16)
    lo = (rest - mid.astype(F32)).astype(jnp.bfloat16)
    dot = lambda part: lax.dot_general(tri, part, _NN, preferred_element_type=F32)
    return dot(hi) + dot(mid) + dot(lo)


@jax.custom_vjp
def chunk_cumsum(x):
    return _tri_dot(True, x)


def _chunk_cumsum_fwd(x):
    return chunk_cumsum(x), None


def _chunk_cumsum_bwd(_, g):
    return (_tri_dot(False, g),)


chunk_cumsum.defvjp(_chunk_cumsum_fwd, _chunk_cumsum_bwd)


@jax.custom_vjp
def swap_halves(t):
    return pltpu.roll(t, t.shape[-1] // 2, t.ndim - 1)


def _swap_fwd(t):
    return swap_halves(t), None


def _swap_bwd(_, g):
    return (swap_halves(g),)


swap_halves.defvjp(_swap_fwd, _swap_bwd)


def mm(name, a, b, dims, *, tm, tn, tk, out_dtype=F32, extras=(), epi=None, n_out=1, out_dtypes=None, side=None,
       column_slabs=False):
    if dims == "nn":
        (M, K), N = a.shape, b.shape[1]
    elif dims == "nt":
        (M, K), N = a.shape, b.shape[0]
    else:
        (K, M), N = a.shape, b.shape[1]
    tm, tn, tk = min(tm, M), min(tn, N), min(tk, K)
    assert M % tm == 0 and N % tn == 0 and K % tk == 0, (name, M, N, K, tm, tn, tk)
    nk = K // tk
    dn = {"nn": _NN, "nt": _NT, "tn": _TN}[dims]
    a_spec = pl.BlockSpec((tk, tm), lambda i, j, k: (k, i)) if dims == "tn" else pl.BlockSpec((tm, tk), lambda i, j, k: (i, k))
    b_spec = pl.BlockSpec((tn, tk), lambda i, j, k: (j, k)) if dims == "nt" else pl.BlockSpec((tk, tn), lambda i, j, k: (k, j))
    o_spec = pl.BlockSpec((tm, tn), lambda i, j, k: (i, j))
    n_ex = len(extras)
    out_dtypes = out_dtypes or (out_dtype,) * n_out

    def finish(r, ex_refs, o_refs):
        if epi is not None:
            r = epi(r, *[e[...] for e in ex_refs])
        r = r if isinstance(r, tuple) else (r,)
        for o, v in zip(o_refs, r):
            o[...] = v.astype(o.dtype)

    def body(*refs):
        a_ref, b_ref = refs[:2]
        ex_refs = refs[2:2 + n_ex]
        o_refs = refs[2 + n_ex:2 + n_ex + n_out]
        part = _mxu(a_ref[...], b_ref[...], dn)
        if nk == 1:
            finish(part, ex_refs, o_refs)
            return
        acc = refs[-1]
        k = pl.program_id(2)

        @pl.when(k == 0)
        def _():
            acc[...] = part

        @pl.when(k > 0)
        def _():
            acc[...] += part

        @pl.when(k == nk - 1)
        def _():
            finish(acc[...], ex_refs, o_refs)

    outs = _side_call(
        side, body,
        name=name,
        grid=(M // tm, N // tn, nk),
        in_specs=[a_spec, b_spec] + [o_spec] * n_ex,
        out_specs=[pl.BlockSpec((None, tm, tn), lambda i, j, k: (j, i, 0))] * n_out if column_slabs else [o_spec] * n_out,
        out_shape=[jax.ShapeDtypeStruct((N // tn, M, tn) if column_slabs else (M, N), dt) for dt in out_dtypes],
        scratch_shapes=[pltpu.VMEM((tm, tn), F32)] if nk > 1 else [],
        args=[a, b, *extras],
        semantics=("parallel", "parallel", "arbitrary"),
    )
    return outs[0] if n_out == 1 else outs


class Row:
    def __init__(self, arr, width=None, col=0, lead=(), rowmap=None):
        self.arr, self.lead, self.col = arr, tuple(lead), col
        self.width = arr.shape[-1] if width is None else width
        self.rowmap = rowmap

    def spec(self, tm):
        lead, col, rowmap = self.lead, self.col, self.rowmap or (lambda i: i)
        return pl.BlockSpec((None,) * len(lead) + (tm, self.width), lambda i: (*lead, rowmap(i), col))


def row_call(name, fn, rows, params, new_outs, red_shapes=(), *, tm, into=None, into_cols=(), stack=None, stacked_onto=None):
    R = rows[0].arr.shape[len(rows[0].lead)]
    tm = min(tm, R)
    assert R % tm == 0, (name, R, tm)
    n_row, n_par, n_new, n_into, n_red = len(rows), len(params), len(new_outs), len(into_cols), len(red_shapes)
    fresh = isinstance(into, jax.ShapeDtypeStruct)

    def body(*refs):
        row_refs = refs[:n_row]
        par_refs = refs[n_row:n_row + n_par]
        pos = n_row + n_par + (1 if (into is not None and not fresh) else 0) + (n_new if stacked_onto is not None else 0)
        new_refs = refs[pos:pos + n_new]
        red_refs = refs[pos + n_new:pos + n_new + n_red]
        pos2 = pos + n_new + n_red
        into_ref = refs[pos2] if into is not None else None
        stage = refs[pos2 + 1:pos2 + 1 + n_into] if into is not None else ()
        i = pl.program_id(0)
        vals = fn(*[r[...] for r in row_refs], *[p[...] for p in par_refs])
        vals = vals if isinstance(vals, (tuple, list)) else (vals,)
        for o, v in zip(new_refs, vals[:n_new]):
            o[...] = v.astype(o.dtype)

        def window_copy(j, step):
            width, off = into_cols[j]
            rows_at = pl.ds(pl.multiple_of(step * tm, tm), tm)
            return pltpu.make_async_copy(stage[j], into_ref.at[rows_at, pl.ds(off, width)], refs[pos2 + 1 + n_into].at[j])

        for j, v in enumerate(vals[n_new:n_new + n_into]):
            @pl.when(i > 0)
            def _(j=j):
                window_copy(j, i - 1).wait()

            stage[j][...] = v.astype(stage[j].dtype)
            window_copy(j, i).start()

            @pl.when(i == R // tm - 1)
            def _(j=j):
                window_copy(j, i).wait()
        for o, v in zip(red_refs, vals[n_new + n_into:]):
            @pl.when(i == 0)
            def _(o=o, v=v):
                o[...] = v.astype(o.dtype)

            @pl.when(i > 0)
            def _(o=o, v=v):
                o[...] += v.astype(o.dtype)

    in_specs = [r.spec(tm) for r in rows] + [pl.BlockSpec(p.shape, lambda i, nd=p.ndim: (0,) * nd) for p in params]
    args = [r.arr for r in rows] + list(params)
    if stack is None:
        out_specs = [pl.BlockSpec((tm, w), lambda i: (i, 0)) for w, _ in new_outs]
        out_shape = [jax.ShapeDtypeStruct((R, w), dt) for w, dt in new_outs]
    else:
        index, depth = stack
        out_specs = [pl.BlockSpec((None, tm, w), lambda i: (index, i, 0)) for w, _ in new_outs]
        out_shape = [jax.ShapeDtypeStruct((depth, R, w), dt) for w, dt in new_outs]
    out_specs += [pl.BlockSpec(s, lambda i, nd=len(s): (0,) * nd) for s in red_shapes]
    out_shape += [jax.ShapeDtypeStruct(s, F32) for s in red_shapes]
    aliases, scratch = {}, []
    if into is not None:
        out_specs.append(pl.BlockSpec(memory_space=pl.ANY))
        out_shape.append(jax.ShapeDtypeStruct(into.shape, into.dtype))
        if not fresh:
            in_specs.append(pl.BlockSpec(memory_space=pl.ANY))
            args.append(into)
            aliases = {len(args) - 1: len(out_shape) - 1}
        scratch = [pltpu.VMEM((tm, w), into.dtype) for w, _ in into_cols] + [pltpu.SemaphoreType.DMA((n_into,))]
    if stacked_onto is not None:
        for j, prev in enumerate(stacked_onto):
            in_specs.append(pl.BlockSpec(memory_space=pl.ANY))
            args.append(prev)
            aliases[len(args) - 1] = j
    outs = pl.pallas_call(
        body,
        name=name,
        grid=(R // tm,),
        in_specs=in_specs,
        out_specs=out_specs,
        out_shape=out_shape,
        scratch_shapes=scratch,
        input_output_aliases=aliases,
        compiler_params=pltpu.CompilerParams(dimension_semantics=("arbitrary",), vmem_limit_bytes=VMEM_BIG),
    )(*args)
    return list(outs)


def vjp_fn(fn, n_row, n_out, passthrough=0):
    def bwd(*args):
        rows = [a.astype(F32) for a in args[:n_row]]
        cts = args[n_row:n_row + n_out]
        thru = args[n_row + n_out:n_row + n_out + passthrough]
        pars = args[n_row + n_out + passthrough:]
        outs, pull = jax.vjp(lambda *xs: tuple(fn(*xs)), *rows, *pars)
        grads = pull(tuple(c.astype(o.dtype) for c, o in zip(cts, outs)))
        return tuple(grads[:n_row]) + tuple(thru) + tuple(grads[n_row:])

    return bwd


def _ln(x, g, b):
    mu = jnp.mean(x, axis=-1, keepdims=True)
    xc = x - mu
    var = jnp.mean(xc * xc, axis=-1, keepdims=True)
    return xc * lax.rsqrt(var + LN_EPS) * g + b


def f_ln_in(x, g, b):
    return (_ln(x, g, b),)


def f_deepnorm(x, y, g, b):
    return (_ln(ALPHA * x + y, g, b),)


def _twice(fn):
    return lambda *args: fn(*args) * 2


STREAM = [(D_MODEL, F32), (D_MODEL, MXU_DTYPE)]


def _rope(t, cos, sin):
    parts = []
    for h in range(LIN_HEADS):
        th = t[:, h * LIN_KEY_DIM:(h + 1) * LIN_KEY_DIM]
        parts.append(th * cos + swap_halves(th) * sin)
    return jnp.concatenate(parts, axis=1)


def f_ret_pre(rq, rk, cos, sin):
    return _rope(rq, cos, sin), _rope(rk, cos, sin) * LIN_KEY_DIM ** -0.5


def _log_sigmoid(x):
    return jnp.minimum(x, 0.0) - jnp.log(1.0 + jnp.exp(-jnp.abs(x)))


def f_gla_pre(gq, glr, wlr, blr):
    gate_pre = dot_nn(glr, wlr) + blr
    return gq * LIN_KEY_DIM ** -0.5, _log_sigmoid(gate_pre) / GLA_GATE_NORMALIZER


def f_normgate(ret_raw, rg, gla_raw, gg, gnorm):
    rets, glas = [], []
    for h in range(LIN_HEADS):
        sl = slice(h * LIN_VAL_DIM, (h + 1) * LIN_VAL_DIM)
        t = ret_raw[:, sl]
        mu = jnp.mean(t, axis=-1, keepdims=True)
        tc = t - mu
        rets.append(tc * lax.rsqrt(jnp.mean(tc * tc, axis=-1, keepdims=True) + LN_EPS))
        u = gla_raw[:, sl]
        glas.append(u * lax.rsqrt(jnp.mean(u * u, axis=-1, keepdims=True) + LN_EPS) * gnorm)
    ret = jax.nn.silu(rg) * jnp.concatenate(rets, axis=1)
    gla = jax.nn.silu(gg) * jnp.concatenate(glas, axis=1)
    return ret, gla


def f_merge(p0, p1, p2, gl):
    out = jax.nn.sigmoid(gl[:, :D_MODEL]) * p0
    out = out + jax.nn.sigmoid(gl[:, D_MODEL:2 * D_MODEL]) * p1
    return (out + jax.nn.sigmoid(gl[:, 2 * D_MODEL:]) * p2,)


def f_loss(y, tgt):
    diff = y - tgt
    sq = jnp.sum(diff * diff, axis=0, keepdims=True)
    lanes = sq[:, :128]
    for j in range(1, D_MODEL // 128):
        lanes = lanes + sq[:, j * 128:(j + 1) * 128]
    return diff * (1.0 / D_MODEL), lanes * (0.5 / D_MODEL)


def _adam(g, w, m, v):
    m = ADAM_B1 * m + (1.0 - ADAM_B1) * g
    v = ADAM_B2 * v + (1.0 - ADAM_B2) * jnp.square(g)
    m_hat = m / (1.0 - ADAM_B1 ** ADAM_STEP)
    v_hat = v / (1.0 - ADAM_B2 ** ADAM_STEP)
    delta = -ADAM_LR * (m_hat / (jnp.sqrt(v_hat) + ADAM_EPS) + ADAM_WD * w)
    return delta, m, v


def f_sum_adam(*args):
    g = args[0].astype(F32)
    for p in args[1:-3]:
        g = g + p.astype(F32)
    return (g,) + _adam(g, *args[-3:])


def f_adam(g, w, m, v):
    return _adam(g, w, m, v)


GROUP = 4
GROUP_ROWS = GROUP * CHUNK
GROUP_KEYS = (GROUP + ATTN_LEFT_CHUNKS) * CHUNK
DIAG_ROWS = REL_CLIP + CHUNK
GRAD_ROWS = 200


def _blocks_by_delta():
    blocks = {}
    for qi in range(GROUP):
        for J in range(GROUP_KEYS // 128):
            blocks.setdefault(qi - 2 * J, []).append((qi, J))
    return blocks


def bias_table(rel_bias, side=None):
    n_tab = rel_bias.shape[0] * ATTN_HEADS

    def body(rb_ref, o_ref):
        h = pl.program_id(0)
        rm = lax.broadcasted_iota(jnp.int32, (CHUNK, 128), 0)
        cm = lax.broadcasted_iota(jnp.int32, (CHUNK, 128), 1)
        ahead = jnp.where(cm >= CHUNK, 1, 0)
        for delta, where in _blocks_by_delta().items():
            base = 64 * delta + PAD_ROWS
            rel = jnp.minimum(base + rm - cm, REL_CLIP) + REL_CLIP
            in_band = (ahead - delta >= 0) & (ahead - delta <= ATTN_LEFT_CHUNKS)
            lo = max(REL_MIN, min(base - 127, REL_CLIP) + REL_CLIP)
            hi = min(base + CHUNK - 1, REL_CLIP) + REL_CLIP

            def step(r, acc, rel=rel):
                return jnp.where(rel == r, rb_ref[h * N_REL + r], acc)

            blk = jnp.zeros((CHUNK, 128), F32) if hi < lo else lax.fori_loop(lo, hi + 1, step, jnp.zeros((CHUNK, 128), F32))
            blk = jnp.where(in_band, blk, NEG_INF)
            for qi, J in where:
                o_ref[pl.ds(qi * CHUNK, CHUNK), pl.ds(J * 128, 128)] = blk

    return _side_call(
        side, body,
        name="bias_table",
        grid=(n_tab,),
        in_specs=[pl.BlockSpec(memory_space=pltpu.SMEM)],
        out_specs=[pl.BlockSpec((None, GROUP_ROWS, GROUP_KEYS), lambda h: (h, 0, 0))],
        out_shape=[jax.ShapeDtypeStruct((n_tab, GROUP_ROWS, GROUP_KEYS), F32)],
        scratch_shapes=[],
        args=[rel_bias.reshape(n_tab * N_REL)],
    )[0]


def bias_table_grad(dbias):
    blocks = _blocks_by_delta()
    below_clip = range(-(CHUNK - 1), REL_CLIP)
    deltas = [d for d in sorted(blocks) if any(-127 <= pos - 64 * d - PAD_ROWS <= CHUNK - 1 for pos in below_clip)]

    def body(d_ref, o_ref, diag):
        sums = []
        for delta in deltas:
            acc = None
            for qi, J in blocks[delta]:
                blk = d_ref[pl.ds(qi * CHUNK, CHUNK), pl.ds(J * 128, 128)]
                acc = blk if acc is None else acc + blk
            sums.append(acc)
        off = lax.broadcasted_iota(jnp.int32, (CHUNK, 128), 0) - lax.broadcasted_iota(jnp.int32, (CHUNK, 128), 1)
        diag[...] = jnp.zeros_like(diag)

        def step(t, carry):
            pos = t - (CHUNK - 1)
            picked = jnp.zeros((CHUNK, 128), F32)
            for delta, s in zip(deltas, sums):
                picked = picked + jnp.where(off == pos - 64 * delta - PAD_ROWS, s, 0.0)
            diag[pl.ds(t, 1), :] = jnp.sum(picked, axis=0, keepdims=True)
            return carry

        lax.fori_loop(0, DIAG_ROWS, step, 0, unroll=8)
        d = d_ref[...]
        total = jnp.sum(jnp.sum(d, axis=1, keepdims=True), axis=0, keepdims=True)
        o_ref[pl.ds(0, DIAG_ROWS), :] = jnp.broadcast_to(jnp.sum(diag[...], axis=1, keepdims=True), (DIAG_ROWS, 128))
        o_ref[pl.ds(DIAG_ROWS, GRAD_ROWS - DIAG_ROWS), :] = jnp.broadcast_to(total, (GRAD_ROWS - DIAG_ROWS, 128))

    out = pl.pallas_call(
        body,
        name="bias_table_grad",
        grid=(ATTN_HEADS,),
        in_specs=[pl.BlockSpec((None, GROUP_ROWS, GROUP_KEYS), lambda h: (h, 0, 0))],
        out_specs=pl.BlockSpec((None, GRAD_ROWS, 128), lambda h: (h, 0, 0)),
        out_shape=jax.ShapeDtypeStruct((ATTN_HEADS, GRAD_ROWS, 128), F32),
        scratch_shapes=[pltpu.VMEM((DIAG_ROWS, 128), F32)],
        compiler_params=pltpu.CompilerParams(dimension_semantics=("arbitrary",)),
    )(dbias)
    below = out[:, :DIAG_ROWS - 1, 0]
    clipped = out[:, DIAG_ROWS, 0] - jnp.sum(below, axis=1)
    return jnp.concatenate([jnp.zeros((ATTN_HEADS, REL_MIN), F32), below, clipped[:, None]], axis=1)


def _attn_stage(q_ref, k_ref, v_ref, qs, kp, vp, T):
    qs[...] = (q_ref[...] * ATTN_HEAD_DIM ** -0.5).astype(qs.dtype)
    for dst, src in ((kp, k_ref), (vp, v_ref)):
        dst[pl.ds(0, PAD_ROWS), :] = jnp.zeros((PAD_ROWS, ATTN_HEAD_DIM), dst.dtype)
        dst[pl.ds(PAD_ROWS, T), :] = src[...].astype(dst.dtype)


def _attn_probs(g, first_groups, qs, kp, b_ref):
    r0 = g * GROUP_ROWS if first_groups else pl.multiple_of(g * GROUP_ROWS, GROUP_ROWS)
    qc = qs[pl.ds(r0, GROUP_ROWS), :]
    kb = kp[pl.ds(r0, GROUP_KEYS), :]
    s = _mxu(qc, kb, _NT) + b_ref[...]
    if first_groups:
        k_chunk = lax.shift_right_logical(lax.broadcasted_iota(jnp.int32, (GROUP_ROWS, GROUP_KEYS), 1), 6)
        s = jnp.where(g * GROUP + k_chunk >= ATTN_LEFT_CHUNKS, s, NEG_INF)
    e = jnp.exp(s - jnp.max(s, axis=-1, keepdims=True))
    return r0, qc, kb, e / jnp.sum(e, axis=-1, keepdims=True)


def _for_groups(T, step):
    n_first = min(ATTN_LEFT_CHUNKS // GROUP, T // GROUP_ROWS)
    for g in range(n_first):
        step(g, True)
    n_loop = T // GROUP_ROWS - n_first
    lax.fori_loop(n_first, T // GROUP_ROWS, lambda g, c: (step(g, False), c)[1], 0, unroll=2 if n_loop % 2 == 0 else 1)


def _head_spec(T, base):
    return pl.BlockSpec((T, ATTN_HEAD_DIM), lambda h: (0, base + h))


def _attn_scratch(T):
    return [pltpu.VMEM((T, ATTN_HEAD_DIM), MXU_DTYPE)] + [pltpu.VMEM((T + PAD_ROWS, ATTN_HEAD_DIM), MXU_DTYPE)] * 2


def attn_fwd(proj, bias_tab, first_tab, side=None):
    T = proj.shape[0]

    def body(q_ref, k_ref, v_ref, b_ref, o_ref, qs, kp, vp):
        _attn_stage(q_ref, k_ref, v_ref, qs, kp, vp, T)

        def step(g, first):
            r0, _, _, p = _attn_probs(g, first, qs, kp, b_ref)
            o_ref[pl.ds(r0, GROUP_ROWS), :] = _mxu(p, vp[pl.ds(r0, GROUP_KEYS), :], _NN).astype(o_ref.dtype)

        _for_groups(T, step)

    out, = _side_call(
        side, body,
        name="attn_fwd",
        grid=(ATTN_HEADS,),
        in_specs=[_head_spec(T, COL["aq"] // 128), _head_spec(T, COL["ak"] // 128), _head_spec(T, COL["av"] // 128),
                  pl.BlockSpec((None, GROUP_ROWS, GROUP_KEYS), lambda h: (first_tab + h, 0, 0))],
        out_specs=[pl.BlockSpec((T, ATTN_HEAD_DIM), lambda h: (0, h))],
        out_shape=[jax.ShapeDtypeStruct((T, ATTN_HEADS * ATTN_HEAD_DIM), MXU_DTYPE)],
        scratch_shapes=_attn_scratch(T),
        args=[proj, proj, proj, bias_tab],
    )
    return out


def attn_bwd(proj, bias_tab, first_tab, dout, dproj, side=None):
    T = proj.shape[0]

    def body(q_ref, k_ref, v_ref, b_ref, do_ref, dproj_in, dbias_ref, dproj_ref, qs, kp, vp, dkp, dvp, dq, stage):
        del dproj_in
        h = pl.program_id(0)
        _attn_stage(q_ref, k_ref, v_ref, qs, kp, vp, T)
        dkp[...] = jnp.zeros_like(dkp)
        dvp[...] = jnp.zeros_like(dvp)
        dbias_ref[...] = jnp.zeros_like(dbias_ref)

        def step(g, first):
            r0, qc, kb, p = _attn_probs(g, first, qs, kp, b_ref)
            do = do_ref[pl.ds(r0, GROUP_ROWS), :]
            dvp[pl.ds(r0, GROUP_KEYS), :] += _mxu(p, do, _TN)
            dp = _mxu(do, vp[pl.ds(r0, GROUP_KEYS), :], _NT)
            ds = p * (dp - jnp.sum(dp * p, axis=-1, keepdims=True))
            dq[pl.ds(r0, GROUP_ROWS), :] = _mxu(ds, kb, _NN) * ATTN_HEAD_DIM ** -0.5
            dkp[pl.ds(r0, GROUP_KEYS), :] += _mxu(ds, qc, _TN)
            dbias_ref[...] += ds

        _for_groups(T, step)
        col = pl.multiple_of(h * ATTN_HEAD_DIM, ATTN_HEAD_DIM)
        for grad, first_row, base in ((dq, 0, "aq"), (dkp, PAD_ROWS, "ak"), (dvp, PAD_ROWS, "av")):
            stage[...] = grad[pl.ds(first_row, T), :].astype(stage.dtype)
            pltpu.sync_copy(stage, dproj_ref.at[:, pl.ds(COL[base] + col, ATTN_HEAD_DIM)])

    dbias, dproj = _side_call(
        side, body,
        name="attn_bwd",
        grid=(ATTN_HEADS,),
        in_specs=[_head_spec(T, COL["aq"] // 128), _head_spec(T, COL["ak"] // 128), _head_spec(T, COL["av"] // 128),
                  pl.BlockSpec((None, GROUP_ROWS, GROUP_KEYS), lambda h: (first_tab + h, 0, 0)),
                  pl.BlockSpec((T, ATTN_HEAD_DIM), lambda h: (0, h)),
                  pl.BlockSpec(memory_space=pl.ANY)],
        out_specs=[pl.BlockSpec((None, GROUP_ROWS, GROUP_KEYS), lambda h: (h, 0, 0)), pl.BlockSpec(memory_space=pl.ANY)],
        out_shape=[jax.ShapeDtypeStruct((ATTN_HEADS, GROUP_ROWS, GROUP_KEYS), F32),
                   jax.ShapeDtypeStruct(dproj.shape, dproj.dtype)],
        scratch_shapes=_attn_scratch(T) + [pltpu.VMEM((T + PAD_ROWS, ATTN_HEAD_DIM), F32)] * 2
        + [pltpu.VMEM((T, ATTN_HEAD_DIM), F32), pltpu.VMEM((T, ATTN_HEAD_DIM), dproj.dtype)],
        args=[proj, proj, proj, bias_tab, dout, dproj],
        aliases={5: 1},
    )
    return dproj, dbias


def _lin_segment(q, k, v, la, st):
    lower = _tri(CHUNK, True)
    steps = (lax.broadcasted_iota(jnp.int32, (CHUNK, 1), 0) + 1).astype(F32)
    outs = []
    for i in range(SEG_CHUNKS):
        sl = slice(i * CHUNK, (i + 1) * CHUNK)
        qi, ki, vi = q[sl], k[sl], v[sl]
        if la.shape[0] == 1:
            cum, last = la * steps, la * float(CHUNK)
        else:
            cum, last = chunk_cumsum(la[sl]), jnp.sum(la[sl], axis=0, keepdims=True)
        e_pos, e_neg = jnp.exp(cum), jnp.exp(-cum)
        q_fwd = qi * e_pos
        s_fwd = dot_nt(q_fwd, ki * e_neg)
        s_bwd = dot_nt(qi * e_neg, ki * e_pos)
        o = dot_nn(jnp.where(lower, s_fwd, s_bwd), vi) + dot_nt(q_fwd, st)
        st = jnp.exp(last) * st + dot_tn(vi, ki * jnp.exp(last - cum))
        outs.append(o)
    return jnp.concatenate(outs, axis=0), st


class Win:
    def __init__(self, arr, width, base=0):
        self.arr, self.width, self.base = arr, width, base


HEADS_PER_STEP = 2


def _lin_spec(w, seg_of):
    assert w.base % HEADS_PER_STEP == 0
    first, width = w.base // HEADS_PER_STEP, HEADS_PER_STEP * w.width
    if w.arr.shape[0] == 1:
        return pl.BlockSpec((1, width), lambda p, s: (0, first + p))
    return pl.BlockSpec((SEG, width), lambda p, s: (seg_of(s), first + p))


def _head_cols(ref, j, width):
    return ref[:, j * width:(j + 1) * width]


def _decay_rows(la_ref, j):
    return _head_cols(la_ref, j, LIN_KEY_DIM)


_STATE_BLOCK = (HEADS_PER_STEP, None, LIN_VAL_DIM, LIN_KEY_DIM)


def lin_fwd(name, q, k, v, la, side=None):
    T = q.arr.shape[0]
    nseg = T // SEG

    def body(q_ref, k_ref, v_ref, la_ref, o_ref, st_ref, st):
        @pl.when(pl.program_id(1) == 0)
        def _():
            st[...] = jnp.zeros_like(st)

        for j in range(HEADS_PER_STEP):
            st0 = st[j]
            st_ref[j] = st0
            o, st1 = _lin_segment(_head_cols(q_ref, j, LIN_KEY_DIM), _head_cols(k_ref, j, LIN_KEY_DIM),
                                  _head_cols(v_ref, j, LIN_VAL_DIM), _decay_rows(la_ref, j), st0)
            o_ref[:, j * LIN_VAL_DIM:(j + 1) * LIN_VAL_DIM] = o
            st[j] = st1

    fwd = lambda s: s
    out, states = _side_call(
        side, body,
        name=name,
        grid=(LIN_HEADS // HEADS_PER_STEP, nseg),
        in_specs=[_lin_spec(w, fwd) for w in (q, k, v, la)],
        out_specs=[pl.BlockSpec((SEG, HEADS_PER_STEP * LIN_VAL_DIM), lambda p, s: (s, p)),
                   pl.BlockSpec(_STATE_BLOCK, lambda p, s: (p, s, 0, 0))],
        out_shape=[jax.ShapeDtypeStruct((T, LIN_HEADS * LIN_VAL_DIM), F32),
                   jax.ShapeDtypeStruct((LIN_HEADS, nseg, LIN_VAL_DIM, LIN_KEY_DIM), F32)],
        scratch_shapes=[pltpu.VMEM((HEADS_PER_STEP, LIN_VAL_DIM, LIN_KEY_DIM), F32)],
        args=[q.arr, k.arr, v.arr, la.arr],
    )
    return out, states


def lin_bwd(name, q, k, v, la, states, dout, with_decay_grad, side=None):
    T = q.arr.shape[0]
    nseg = T // SEG
    n_out = 4 if with_decay_grad else 3

    widths = [LIN_KEY_DIM, LIN_KEY_DIM, LIN_VAL_DIM] + ([LIN_KEY_DIM] if with_decay_grad else [])

    def body(q_ref, k_ref, v_ref, la_ref, st_ref, do_ref, *rest):
        outs, dst = rest[:n_out], rest[n_out]

        @pl.when(pl.program_id(1) == 0)
        def _():
            dst[...] = jnp.zeros_like(dst)

        for j in range(HEADS_PER_STEP):
            _, pull = jax.vjp(_lin_segment, _head_cols(q_ref, j, LIN_KEY_DIM).astype(F32),
                              _head_cols(k_ref, j, LIN_KEY_DIM).astype(F32), _head_cols(v_ref, j, LIN_VAL_DIM).astype(F32),
                              _decay_rows(la_ref, j), st_ref[j])
            grads = pull((_head_cols(do_ref, j, LIN_VAL_DIM), dst[j]))
            for o, g, w in zip(outs, grads[:n_out], widths):
                o[:, j * w:(j + 1) * w] = g
            dst[j] = grads[4]

    rev = lambda s: nseg - 1 - s
    key_spec = pl.BlockSpec((SEG, HEADS_PER_STEP * LIN_KEY_DIM), lambda p, s: (rev(s), p))
    val_spec = pl.BlockSpec((SEG, HEADS_PER_STEP * LIN_VAL_DIM), lambda p, s: (rev(s), p))
    out_specs = [key_spec, key_spec, val_spec] + ([key_spec] if with_decay_grad else [])
    return _side_call(
        side, body,
        name=name,
        grid=(LIN_HEADS // HEADS_PER_STEP, nseg),
        in_specs=[_lin_spec(w, rev) for w in (q, k, v, la)] + [
            pl.BlockSpec(_STATE_BLOCK, lambda p, s: (p, rev(s), 0, 0)), val_spec],
        out_specs=out_specs,
        out_shape=[jax.ShapeDtypeStruct((T, LIN_HEADS * w), F32) for w in widths],
        scratch_shapes=[pltpu.VMEM((HEADS_PER_STEP, LIN_VAL_DIM, LIN_KEY_DIM), F32)],
        args=[q.arr, k.arr, v.arr, la.arr, states, dout],
    )


def _coords():
    return lax.axis_index("x"), lax.axis_index("y"), lax.axis_index("c")


def _peer(k):
    x, y, c = _coords()
    px = 1 - x if k & 4 else x
    py = 1 - y if k & 2 else y
    pc = 1 - c if k & 1 else c
    return (px, py, pc), 4 * px + 2 * py + pc


def _remote(src, dst, send_sem, recv_sem, device):
    return pltpu.make_async_remote_copy(src_ref=src, dst_ref=dst, send_sem=send_sem, recv_sem=recv_sem,
                                        device_id=device, device_id_type=pl.DeviceIdType.MESH)


class Side:
    def __init__(self, kind, arrays):
        self.kind, self.arrays, self.n = kind, list(arrays), len(arrays)
        self.outs_per = 1
        self.slots = {"gather": N_DEV - 1, "pair": 4, "chips": 3}[kind]

    def out_shape(self):
        if self.kind == "gather":
            return [jax.ShapeDtypeStruct((N_DEV,) + a.shape, a.dtype) for a in self.arrays]
        return [jax.ShapeDtypeStruct((4,) + a.shape[1:], a.dtype) for a in self.arrays]

    def sems(self):
        return [pltpu.SemaphoreType.DMA((self.n, self.slots)), pltpu.SemaphoreType.DMA((self.n, self.slots)),
                pltpu.SemaphoreType.DMA((self.n,))]

    def start(self, ins, outs, sems):
        send, recv, own = sems
        x, y, c = _coords()
        me = 4 * x + 2 * y + c
        for a in range(self.n):
            if self.kind == "gather":
                pltpu.make_async_copy(ins[a], outs[a].at[me], own.at[a]).start()
                for k in (1, 2, 4, 6):
                    peer, _ = _peer(k)
                    _remote(ins[a], outs[a].at[me], send.at[a, k - 1], recv.at[a, k - 1], peer).start()
            elif self.kind == "pair":
                for j in range(4):
                    _remote(ins[a].at[2 * j + 1 - c], outs[a].at[j], send.at[a, j], recv.at[a, j], (x, y, 1 - c)).start()
            else:
                chip = 2 * x + y
                pltpu.make_async_copy(ins[a].at[chip], outs[a].at[chip], own.at[a]).start()
                for d in (1, 2, 3):
                    tx, ty = (1 - x if d & 2 else x), (1 - y if d & 1 else y)
                    _remote(ins[a].at[2 * tx + ty], outs[a].at[chip], send.at[a, d - 1], recv.at[a, d - 1], (tx, ty, c)).start()

    def finish(self, ins, outs, sems):
        send, recv, own = sems
        x, y, c = _coords()
        me = 4 * x + 2 * y + c
        sibling = (x, y, 1 - c)
        for a in range(self.n):
            if self.kind == "gather":
                blk = outs[a].at[me]
                for k in (2, 4, 6):
                    _, pidx = _peer(k)
                    _remote(blk, blk, send.at[a, k - 1], recv.at[a, k - 1], sibling).wait_recv()
                    _remote(outs[a].at[pidx], outs[a].at[pidx], send.at[a, k], recv.at[a, k], sibling).start()
                pltpu.make_async_copy(ins[a], blk, own.at[a]).wait()
                for k in range(1, N_DEV):
                    _remote(blk, blk, send.at[a, k - 1], recv.at[a, k - 1], sibling).wait_send()
                for k in (1, 3, 5, 7):
                    _remote(blk, blk, send.at[a, k - 1], recv.at[a, k - 1], sibling).wait_recv()
            elif self.kind == "pair":
                for j in range(4):
                    _remote(ins[a].at[j], outs[a].at[j], send.at[a, j], recv.at[a, j], sibling).wait()
            else:
                slab = outs[a].at[0]
                pltpu.make_async_copy(slab, slab, own.at[a]).wait()
                for d in (1, 2, 3):
                    _remote(slab, slab, send.at[a, d - 1], recv.at[a, d - 1], sibling).wait()


def _side_call(side, body, *, name, grid, in_specs, out_specs, out_shape, scratch_shapes, args, semantics=None,
               aliases=None):
    in_specs, out_specs, out_shape = list(in_specs), list(out_specs), list(out_shape)
    scratch_shapes, args = list(scratch_shapes), list(args)
    n_in, n_out = len(in_specs), len(out_specs)
    kernel_body = body
    if side is not None:
        n_so = side.n * side.outs_per

        def kernel_body(*refs):
            ins, side_ins = refs[:n_in], refs[n_in:n_in + side.n]
            pos = n_in + side.n
            outs, side_outs = refs[pos:pos + n_out], refs[pos + n_out:pos + n_out + n_so]
            rest = refs[pos + n_out + n_so:]
            scratch, sems = rest[:-3], rest[-3:]
            ids = [pl.program_id(ax) for ax in range(len(grid))]
            first = functools.reduce(jnp.logical_and, [i == 0 for i in ids])
            last = functools.reduce(jnp.logical_and, [i == g - 1 for i, g in zip(ids, grid)])

            @pl.when(first)
            def _():
                side.start(side_ins, side_outs, sems)

            body(*ins, *outs, *scratch)

            @pl.when(last)
            def _():
                side.finish(side_ins, side_outs, sems)

        any_spec = pl.BlockSpec(memory_space=pl.ANY)
        in_specs += [any_spec] * side.n
        args += side.arrays
        out_specs += [any_spec] * n_so
        out_shape += side.out_shape()
        scratch_shapes += side.sems()
        semantics = ("arbitrary",) * len(grid)
    outs = pl.pallas_call(
        kernel_body,
        name=name,
        grid=grid,
        in_specs=in_specs,
        out_specs=out_specs,
        out_shape=out_shape,
        scratch_shapes=scratch_shapes,
        input_output_aliases=aliases or {},
        compiler_params=pltpu.CompilerParams(dimension_semantics=semantics or ("arbitrary",) * len(grid),
                                             vmem_limit_bytes=VMEM_BIG),
    )(*args)
    if side is not None:
        side.results = list(outs[n_out:])
    return list(outs[:n_out])


def carrier(name, side):
    _side_call(side, lambda: None, name=name, grid=(1,), in_specs=[], out_specs=[], out_shape=[], scratch_shapes=[], args=[])
    return side.results


def all_reduce_small(vec):
    def body(v_ref, o_ref, land, send_sems, recv_sems):
        x, y, c = _coords()
        me = 4 * x + 2 * y + c
        land[me] = v_ref[...]
        copies = []
        for k in range(1, N_DEV):
            peer, _ = _peer(k)
            cp = pltpu.make_async_remote_copy(
                src_ref=v_ref, dst_ref=land.at[me], send_sem=send_sems.at[k - 1], recv_sem=recv_sems.at[k - 1],
                device_id=peer, device_id_type=pl.DeviceIdType.MESH)
            cp.start()
            copies.append(cp)
        for cp in copies:
            cp.wait()
        total = land[0]
        for s in range(1, N_DEV):
            total = total + land[s]
        o_ref[...] = total

    return pl.pallas_call(
        body,
        name="all_reduce_small",
        in_specs=[pl.BlockSpec(memory_space=pltpu.VMEM)],
        out_specs=pl.BlockSpec(memory_space=pltpu.VMEM),
        out_shape=jax.ShapeDtypeStruct(vec.shape, F32),
        scratch_shapes=[pltpu.VMEM((N_DEV,) + vec.shape, F32), pltpu.SemaphoreType.DMA((N_DEV - 1,)),
                        pltpu.SemaphoreType.DMA((N_DEV - 1,))],
    )(vec)


def _pack(parts):
    flat = jnp.concatenate([p.reshape(-1).astype(F32) for p in parts])
    n = flat.shape[0]
    rows = -(-n // 1024) * 8
    return jnp.pad(flat, (0, rows * 128 - n)).reshape(rows, 128)


def _unpack(packed, shapes):
    flat, out, pos = packed.reshape(-1), [], 0
    for s in shapes:
        n = math.prod(s)
        out.append(flat[pos:pos + n].reshape(s))
        pos += n
    return out


def _row2(v):
    return v.reshape(1, -1)


def _rider(hooks, name, ctx=None):
    if name not in hooks:
        return None, lambda: None
    side, done = hooks[name](ctx)
    return side, lambda: done(side.results)


def _layer_forward(h, hb, W, consts, hooks):
    T = h.shape[0]
    cos, sin, ret_decay = consts
    side, landed = _rider(hooks, "proj")
    proj = mm("proj", hb, W["in"], "nn", tm=4096, tn=512, tk=1024, side=side)
    landed()
    bias_tab, first_tab = W["bias_tab"]
    side, landed = _rider(hooks, "attn_fwd")
    attn = attn_fwd(proj, bias_tab, first_tab, side=side)
    landed()
    rq2, rk2 = row_call("ret_pre", f_ret_pre, [Row(proj, 512, COL["rq"] // 512), Row(proj, 512, COL["rk"] // 512),
                                              Row(cos), Row(sin)], [], [(512, F32), (512, F32)], tm=512)
    side, landed = _rider(hooks, "ret_fwd")
    ret_raw, ret_st = lin_fwd("ret_fwd", Win(rq2, 128), Win(rk2, 128), Win(proj, 256, COL["rv"] // 256), Win(ret_decay, 128),
                              side=side)
    landed()
    gq2, gla_la = row_call("gla_pre", f_gla_pre, [Row(proj, 512, COL["gq"] // 512), Row(proj, GLR_BLOCK, GLR_OFF // GLR_BLOCK)],
                           [W["wlr"], W["blr"]], [(512, F32), (512, F32)], tm=512)
    side, landed = _rider(hooks, "gla_fwd")
    gla_raw, gla_st = lin_fwd("gla_fwd", Win(gq2, 128), Win(proj, 128, COL["gk"] // 128), Win(proj, 256, COL["gv"] // 256),
                              Win(gla_la, 128), side=side)
    landed()
    ret_b, gla_b = row_call("normgate", f_normgate, [Row(ret_raw), Row(proj, 1024, COL["rg"] // 1024), Row(gla_raw),
                                                    Row(proj, 1024, COL["gg"] // 1024)], [W["gnorm"]],
                            [(1024, MXU_DTYPE), (1024, MXU_DTYPE)], tm=256)
    branches = (attn, ret_b, gla_b)
    p = [mm("branch_proj", branches[n], W["branch"][n], "nn", tm=1024, tn=1024, tk=1024) for n in range(3)]
    merged, = row_call("merge", f_merge, [Row(p[0]), Row(p[1]), Row(p[2]), Row(proj, GATES_W, GATES_OFF // GATES_W)], [],
                       [(1024, MXU_DTYPE)], tm=256)
    mix = mm("out_proj", merged, W["out"], "nn", tm=1024, tn=1024, tk=1024)
    x1, x1b = row_call("ln1", _twice(f_deepnorm), [Row(h), Row(mix)], [W["ln1_g"], W["ln1_b"]], STREAM, tm=512)
    side, landed = _rider(hooks, "mlp_up")
    u, act = mm("mlp_up", x1b, W["up"], "nn", tm=2048, tn=512, tk=1024, n_out=2, out_dtypes=(F32, MXU_DTYPE),
                epi=lambda r: (r, jnp.square(jnp.maximum(r, 0.0))), side=side)
    landed()
    y = mm("mlp_down", act, W["down"], "nn", tm=1024, tn=1024, tk=2048)
    x2, x2b = row_call("ln2", _twice(f_deepnorm), [Row(x1), Row(y)], [W["ln2_g"], W["ln2_b"]], STREAM, tm=512)
    saved = dict(h=h, hb=hb, proj=proj, rq2=rq2, rk2=rk2, ret_st=ret_st, gq2=gq2, gla_la=gla_la,
                 gla_st=gla_st, ret_raw=ret_raw, gla_raw=gla_raw, branches=branches, p=p, merged=merged, mix=mix, x1=x1, x1b=x1b,
                 u=u, act=act, y=y)
    return x2, x2b, saved


def _layer_backward(dx2, S, W, consts, hooks):
    cos, sin, ret_decay = consts
    h, proj, T = S["h"], S["proj"], S["h"].shape[0]
    add = lambda r, e: r + e
    dx1a, dy, dg2, db2 = row_call("ln2_bwd", vjp_fn(f_deepnorm, 2, 1), [Row(S["x1"]), Row(S["y"]), Row(dx2)],
                                  [W["ln2_g"], W["ln2_b"]], [(1024, F32), (1024, MXU_DTYPE)], [(1, 1024), (1, 1024)], tm=256)
    side, landed = _rider(hooks, "mlp_down_bwd")
    du = mm("mlp_down_bwd", dy, W["down"], "nt", tm=1024, tn=512, tk=1024, extras=(S["u"],),
            epi=lambda r, u: r * (2.0 * jnp.maximum(u, 0.0)), out_dtype=MXU_DTYPE, side=side)
    landed()
    g_down = mm("mlp_down_wgrad", S["act"], dy, "tn", tm=1024, tn=1024, tk=2048, out_dtype=WIRE_DTYPE)
    dx1 = mm("mlp_up_bwd", du, W["up"], "nt", tm=1024, tn=1024, tk=2048, extras=(dx1a,), epi=add)
    g_up = mm("mlp_up_wgrad", S["x1b"], du, "tn", tm=1024, tn=D_FF // N_DEV, tk=2048, out_dtype=WIRE_DTYPE,
              column_slabs=True)
    dha, dmix, dg1, db1 = row_call("ln1_bwd", vjp_fn(f_deepnorm, 2, 1), [Row(h), Row(S["mix"]), Row(dx1)],
                                   [W["ln1_g"], W["ln1_b"]], [(1024, F32), (1024, MXU_DTYPE)], [(1, 1024), (1, 1024)], tm=256)
    dmerged = mm("out_proj_bwd", dmix, W["out"], "nt", tm=1024, tn=1024, tk=1024)
    g_out = mm("out_proj_wgrad", S["merged"], dmix, "tn", tm=1024, tn=1024, tk=2048, out_dtype=WIRE_DTYPE)
    dproj = jax.ShapeDtypeStruct((T, PROJ_PAD), MXU_DTYPE)
    p = S["p"]
    dp0, dp1, dp2, dproj = row_call(
        "merge_bwd", vjp_fn(f_merge, 4, 1), [Row(p[0]), Row(p[1]), Row(p[2]), Row(proj, GATES_W, GATES_OFF // GATES_W), Row(dmerged)],
        [], [(1024, MXU_DTYPE)] * 3, tm=256, into=dproj, into_cols=[(GATES_W, GATES_OFF)])
    dps = (dp0, dp1, dp2)
    dbr = [mm("branch_proj_bwd", dps[n], W["branch"][n], "nt", tm=1024, tn=1024, tk=1024,
              out_dtype=MXU_DTYPE if n == 0 else F32) for n in range(3)]
    g_branch = [mm("branch_proj_wgrad", S["branches"][n], dps[n], "tn", tm=1024, tn=1024, tk=2048, out_dtype=WIRE_DTYPE)
                for n in range(3)]
    dret_raw, dgla_raw, dgnorm, dproj = row_call(
        "normgate_bwd", vjp_fn(lambda a, b, c, d, g: _reorder_normgate(a, b, c, d, g), 4, 2),
        [Row(S["ret_raw"]), Row(S["gla_raw"]), Row(proj, 1024, COL["rg"] // 1024), Row(proj, 1024, COL["gg"] // 1024),
         Row(dbr[1]), Row(dbr[2])], [W["gnorm"]], [(1024, F32), (1024, F32)], [(1, LIN_VAL_DIM)], tm=256,
        into=dproj, into_cols=[(1024, COL["rg"]), (1024, COL["gg"])])
    side, landed = _rider(hooks, "gla_bwd")
    dgq2, dgk, dgv, dgla_la = lin_bwd("gla_bwd", Win(S["gq2"], 128), Win(proj, 128, COL["gk"] // 128),
                                      Win(proj, 256, COL["gv"] // 256), Win(S["gla_la"], 128), S["gla_st"], dgla_raw, True,
                                      side=side)
    landed()
    dwlr, dblr, dproj = row_call(
        "gla_pre_bwd", _gla_pre_bwd, [Row(proj, 512, COL["gq"] // 512), Row(proj, GLR_BLOCK, GLR_OFF // GLR_BLOCK),
                                      Row(dgq2), Row(dgla_la), Row(dgk), Row(dgv)], [W["wlr"], W["blr"]], [],
        [(GLR_BLOCK, 512), (1, 512)], tm=256, into=dproj,
        into_cols=[(512, COL["gq"]), (PROJ_PAD - GLR_OFF, GLR_OFF), (512, COL["gk"]), (1024, COL["gv"])])
    side, landed = _rider(hooks, "ret_bwd", dict(branch=g_branch, out=g_out, up=g_up, down=g_down))
    drq2, drk2, drv = lin_bwd("ret_bwd", Win(S["rq2"], 128), Win(S["rk2"], 128), Win(proj, 256, COL["rv"] // 256),
                              Win(ret_decay, 128), S["ret_st"], dret_raw, False, side=side)
    landed()
    dproj, = row_call(
        "ret_pre_bwd", _ret_pre_bwd, [Row(proj, 512, COL["rq"] // 512), Row(proj, 512, COL["rk"] // 512), Row(cos), Row(sin),
                                      Row(drq2), Row(drk2), Row(drv)], [], [], tm=256, into=dproj,
        into_cols=[(512, COL["rq"]), (512, COL["rk"]), (1024, COL["rv"])])
    side, landed = _rider(hooks, "attn_bwd")
    dproj, dbias = attn_bwd(proj, *W["bias_tab"], dbr[0], dproj, side=side)
    landed()
    d_rel_bias = bias_table_grad(dbias)
    side, landed = _rider(hooks, "proj_wgrad", dict(branch=g_branch, out=g_out, up=g_up, down=g_down))
    g_in = mm("proj_wgrad", S["hb"], dproj, "tn", tm=1024, tn=1280, tk=2048, out_dtype=WIRE_DTYPE, side=side)
    landed()
    side, landed = _rider(hooks, "proj_bwd", {"in": g_in})
    dh = mm("proj_bwd", dproj, W["in"], "nt", tm=1024, tn=1024, tk=2560, extras=(dha,), epi=add, side=side)
    landed()
    big = dict(g_in=g_in, branch=g_branch, out=g_out, up=g_up, down=g_down)
    small = dict(rel_bias=d_rel_bias, wlr=dwlr[:GLA_RANK], blr=dblr, gnorm=dgnorm, ln1_g=dg1, ln1_b=db1, ln2_g=dg2, ln2_b=db2)
    return dh, big, small


def _reorder_normgate(ret_raw, gla_raw, rg, gg, gnorm):
    return f_normgate(ret_raw, rg, gla_raw, gg, gnorm)


def _gla_pre_bwd(gq, glr, dq2, dla, dgk, dgv, wlr, blr):
    _, pull = jax.vjp(f_gla_pre, gq, glr, wlr, blr)
    dgq, dglr, dwlr, dblr = pull((dq2, dla))
    dglr = jnp.concatenate([dglr, jnp.zeros((dglr.shape[0], PROJ_PAD - GLR_OFF - GLR_BLOCK), F32)], axis=1)
    return dgq, dglr, dgk, dgv, dwlr, dblr


def _ret_pre_bwd(rq, rk, cos, sin, drq2, drk2, drv):
    _, pull = jax.vjp(lambda a, b: f_ret_pre(a, b, cos, sin), rq, rk)
    drq, drk = pull((drq2, drk2))
    return drq, drk, drv


SHARDED = ("in", "branch", "out", "up", "down")

_FULL = {
    "branch": lambda g: [g[:, n].reshape(D_MODEL, D_MODEL) for n in range(3)],
    "out": lambda g: g.reshape(D_MODEL, D_MODEL),
    "up": lambda g: jnp.transpose(g, (1, 0, 2)).reshape(D_MODEL, D_FF),
    "down": lambda g: g.reshape(D_FF, D_MODEL),
}


def _owner_pieces(s):
    shard = PROJ_WIDTH // N_DEV
    lo, hi = s * shard, (s + 1) * shard
    pieces = []
    for a, b, padded in ((0, GLR_SRC, 0), (GLR_SRC, GATES_SRC, GLR_OFF), (GATES_SRC, PROJ_WIDTH, GATES_OFF)):
        x0, x1 = max(lo, a), min(hi, b)
        if x0 < x1:
            pieces.append((x0 - lo, x1 - x0, padded + x0 - a))
    return pieces


def _full_in(g):
    placed = sorted((p, s, c, w) for s in range(N_DEV) for c, w, p in _owner_pieces(s))
    cols = [g[s][:, c:c + w] for _, s, c, w in placed]
    return jnp.concatenate(cols + [jnp.zeros((D_MODEL, PROJ_PAD - PROJ_WIDTH), g.dtype)], axis=1)


def _full_wlr(g):
    wlr = jnp.transpose(g, (1, 0, 2)).reshape(GLA_RANK, LIN_HEADS * LIN_KEY_DIM)
    return jnp.pad(wlr, ((0, GLR_BLOCK - GLA_RANK), (0, 0)))


def _slab_in(g):
    return jnp.stack([jnp.concatenate([g[:, p:p + w] for _, w, p in _owner_pieces(s)], axis=1) for s in range(N_DEV)])


_SLABS = {
    "in": _slab_in,
    "branch": lambda gs: jnp.stack([b.reshape(N_DEV, D_MODEL // N_DEV, D_MODEL) for b in gs], axis=1),
    "out": lambda g: g.reshape(N_DEV, D_MODEL // N_DEV, D_MODEL),
    "up": lambda g: g,
    "down": lambda g: g.reshape(N_DEV, D_FF // N_DEV, D_MODEL),
}


def _pair_sums(name, slabs, from_sibling=None):
    sums = []
    for mine, sibs in zip(slabs, from_sibling or carrier(name, Side("pair", slabs))):
        width = mine.shape[-1]
        rows = math.prod(mine.shape[1:-1])
        tm = 256 if rows % 256 == 0 else 128
        nb = rows // tm
        my_rows = Row(mine.reshape(-1, width), rowmap=lambda i, nb=nb: (2 * (i // nb) + lax.axis_index("c")) * nb + i % nb)
        out, = row_call("pair_sum", lambda q, p: (p.astype(F32) + q.astype(F32),),
                        [Row(sibs.reshape(-1, width)), my_rows], [], [(width, WIRE_DTYPE)], tm=tm)
        sums.append(out.reshape(sibs.shape))
    return sums


def _sum_adam(name, parts, w, m, v, layer, tm, earlier):
    C = w.shape[-1]
    rows = [Row(parts, lead=(s,)) for s in range(parts.shape[0])] + [Row(a, lead=(layer,)) for a in (w, m, v)]
    return row_call(name, f_sum_adam, rows, [], [(C, F32)] * 4, tm=tm, stack=(layer, DEPTH), stacked_onto=earlier)


def kernel(x, ln_in_g, ln_in_b, w_in, rel_bias, gla_w_lr, gla_b_lr, gla_norm_g, w_branch, w_out, ln1_g, ln1_b, w_up, w_down, ln2_g, ln2_b, loss_target, m_ln_in_g, m_ln_in_b, m_w_in, m_rel_bias, m_gla_w_lr, m_gla_b_lr, m_gla_norm_g, m_w_branch, m_w_out, m_ln1_g, m_ln1_b, m_w_up, m_w_down, m_ln2_g, m_ln2_b, v_ln_in_g, v_ln_in_b, v_w_in, v_rel_bias, v_gla_w_lr, v_gla_b_lr, v_gla_norm_g, v_w_branch, v_w_out, v_ln1_g, v_ln1_b, v_w_up, v_w_down, v_ln2_g, v_ln2_b):
    T = x.shape[1]
    x2d = x.reshape(T, D_MODEL)
    tgt = loss_target.reshape(T, D_MODEL)
    me = 4 * lax.axis_index("x") + 2 * lax.axis_index("y") + lax.axis_index("c")

    inv = ROPE_BASE ** (-jnp.arange(0, LIN_KEY_DIM, 2, dtype=F32) / LIN_KEY_DIM)
    ang = jnp.arange(T, dtype=F32)[:, None] * inv[None, :]
    cos = jnp.concatenate([jnp.cos(ang), jnp.cos(ang)], axis=1)
    sin = jnp.concatenate([-jnp.sin(ang), jnp.sin(ang)], axis=1)
    log_gamma = jnp.log1p(-jnp.exp2(-5.0 - jnp.arange(LIN_HEADS, dtype=F32)))
    ret_decay = jnp.repeat(log_gamma, LIN_KEY_DIM).reshape(1, LIN_HEADS * LIN_KEY_DIM)
    consts = (cos, sin, ret_decay)

    layers = [dict(rel_bias=rel_bias[l], blr=_row2(gla_b_lr[l]), gnorm=_row2(gla_norm_g[l]), ln1_g=_row2(ln1_g[l]),
                   ln1_b=_row2(ln1_b[l]), ln2_g=_row2(ln2_g[l]), ln2_b=_row2(ln2_b[l])) for l in range(DEPTH)]
    shards = dict(branch=w_branch, out=w_out, up=w_up, down=w_down)
    shards["in"] = w_in
    wire = [{n: shards[n][l].astype(WIRE_DTYPE) for n in SHARDED} for l in range(DEPTH)]

    def gather_of(l, names):
        def landed(res):
            for n, g in zip(names, res):
                layers[l][n] = _full_in(g) if n == "in" else _FULL[n](g)
        return lambda ctx: (Side("gather", [wire[l][n] for n in names]), landed)

    first = Side("gather", [wire[0]["in"]] + [gla_w_lr[l] for l in range(DEPTH)])
    bias_tabs = bias_table(rel_bias, side=first)
    layers[0]["in"] = _full_in(first.results[0])
    for l in range(DEPTH):
        layers[l]["wlr"] = _full_wlr(first.results[1 + l])
        layers[l]["bias_tab"] = (bias_tabs, l * ATTN_HEADS)

    fwd_hooks = [{"proj": gather_of(1, ("in",)), "attn_fwd": gather_of(0, ("up", "down")),
                  "ret_fwd": gather_of(0, ("branch", "out"))},
                 {"proj": gather_of(1, ("up", "down")), "attn_fwd": gather_of(1, ("branch", "out"))}]
    h, hb = row_call("ln_in", _twice(f_ln_in), [Row(x2d)], [_row2(ln_in_g), _row2(ln_in_b)], STREAM, tm=512)
    saved = []
    for l in range(DEPTH):
        h, hb, s = _layer_forward(h, hb, layers[l], consts, fwd_hooks[l])
        saved.append(s)
    dh, loss_lanes = row_call("loss", f_loss, [Row(h), Row(tgt)], [], [(D_MODEL, F32)], [(1, 128)], tm=512)

    received = [dict() for _ in range(DEPTH)]

    def scatter_of(l, names, sums=None):
        def make(ctx):
            parts = [sums[n] for n in names] if sums is not None else _pair_sums(
                "pair_grads", [_SLABS[n](ctx[n]) for n in names])

            def landed(res):
                received[l].update(zip(names, res))
            return Side("chips", parts), landed
        return make

    small = [None] * DEPTH
    dh, big, small[1] = _layer_backward(dh, saved[1], layers[1], consts, {})
    slabs1 = [_SLABS[n](big["g_in"] if n == "in" else big[n]) for n in SHARDED]
    sums1 = {}

    def pair_exchange_1(ctx):
        def landed(res):
            sums1.update(zip(SHARDED, _pair_sums("pair_grads", slabs1, from_sibling=res)))
        return Side("pair", slabs1), landed

    rest = SHARDED[1:]
    sums0 = {}

    def pair_exchange_0(ctx):
        slabs0 = [_SLABS[n](ctx[n]) for n in rest]

        def landed(res):
            sums0.update(zip(rest, _pair_sums("pair_grads", slabs0, from_sibling=res)))
        return Side("pair", slabs0), landed

    dh, big, small[0] = _layer_backward(dh, saved[0], layers[0], consts, {
        "mlp_down_bwd": pair_exchange_1, "gla_bwd": scatter_of(1, ("in",), sums1), "ret_bwd": pair_exchange_0,
        "attn_bwd": scatter_of(1, rest, sums1), "proj_wgrad": scatter_of(0, rest, sums0), "proj_bwd": scatter_of(0, ("in",))})
    grad_x, d_in_g, d_in_b = row_call("ln_in_bwd", vjp_fn(f_ln_in, 1, 1), [Row(x2d), Row(dh)],
                                      [_row2(ln_in_g), _row2(ln_in_b)], [(D_MODEL, F32)], [(1, D_MODEL), (1, D_MODEL)], tm=256)

    small_names = ("rel_bias", "wlr", "blr", "gnorm", "ln1_g", "ln1_b", "ln2_g", "ln2_b")
    parts = [d_in_g, d_in_b] + [small[l][n] for l in range(DEPTH) for n in small_names] + [loss_lanes]
    shapes = [(D_MODEL,), (D_MODEL,)] + [s for l in range(DEPTH) for s in (
        (ATTN_HEADS, N_REL), (GLA_RANK, 512), (512,), (LIN_VAL_DIM,), (D_MODEL,), (D_MODEL,), (D_MODEL,), (D_MODEL,))] + [(128,)]
    red = _unpack(all_reduce_small(_pack(parts)), shapes)
    loss = jnp.sum(red[-1])
    g_small = {"ln_in_g": red[0], "ln_in_b": red[1]}
    for l in range(DEPTH):
        for j, n in enumerate(small_names):
            g_small[(l, n)] = red[2 + l * len(small_names) + j]
    shard = LIN_HEADS * LIN_KEY_DIM // N_DEV
    g_wlr = jnp.stack([lax.dynamic_slice_in_dim(g_small[(l, "wlr")], me * shard, shard, axis=1) for l in range(DEPTH)])
    stack = lambda n: jnp.stack([g_small[(l, n)] for l in range(DEPTH)])
    g_rep = [g_small["ln_in_g"], g_small["ln_in_b"], stack("rel_bias"), g_wlr, stack("blr"), stack("gnorm"),
             stack("ln1_g"), stack("ln1_b"), stack("ln2_g"), stack("ln2_b")]
    w_rep = [ln_in_g, ln_in_b, rel_bias, gla_w_lr, gla_b_lr, gla_norm_g, ln1_g, ln1_b, ln2_g, ln2_b]
    m_rep = [m_ln_in_g, m_ln_in_b, m_rel_bias, m_gla_w_lr, m_gla_b_lr, m_gla_norm_g, m_ln1_g, m_ln1_b, m_ln2_g, m_ln2_b]
    v_rep = [v_ln_in_g, v_ln_in_b, v_rel_bias, v_gla_w_lr, v_gla_b_lr, v_gla_norm_g, v_ln1_g, v_ln1_b, v_ln2_g, v_ln2_b]
    rep_shapes = [w.shape for w in w_rep]
    d_pack, m_pack, v_pack = row_call("adam_small", f_adam, [Row(_pack(g_rep)), Row(_pack(w_rep)), Row(_pack(m_rep)), Row(_pack(v_rep))],
                                      [], [(128, F32)] * 3, tm=256)
    d_rep, nm_rep, nv_rep = _unpack(d_pack, rep_shapes), _unpack(m_pack, rep_shapes), _unpack(v_pack, rep_shapes)

    def owned(key, w, m, v, name, tm):
        C = w.shape[-1]
        flat = lambda a: a.reshape(DEPTH, -1, C)
        outs = None
        for l in range(DEPTH):
            outs = _sum_adam(name, received[l][key].reshape(4, -1, C), flat(w), flat(m), flat(v), l, tm, outs)
        return [o.reshape(w.shape) for o in outs]

    o_in = owned("in", w_in, m_w_in, v_w_in, "adam_w_in", 256)
    o_br = owned("branch", w_branch, m_w_branch, v_w_branch, "adam_w_branch", 128)
    o_out = owned("out", w_out, m_w_out, v_w_out, "adam_w_out", 128)
    o_up = owned("up", w_up, m_w_up, v_w_up, "adam_w_up", 256)
    o_down = owned("down", w_down, m_w_down, v_w_down, "adam_w_down", 256)

    def tree(j, rep):
        return [rep[0], rep[1], o_in[j], rep[2], rep[3], rep[4], rep[5], o_br[j], o_out[j], rep[6], rep[7], o_up[j], o_down[j],
                rep[8], rep[9]]

    return (loss, grad_x.reshape(1, T, D_MODEL), *tree(0, g_rep), *tree(1, d_rep), *tree(2, nm_rep), *tree(3, nv_rep))
```

```python
import functools
import math

import jax
import jax.numpy as jnp
from jax import lax
from jax.experimental import pallas as pl
from jax.experimental.pallas import tpu as pltpu

F32 = jnp.float32
MXU_DTYPE = jnp.bfloat16
WIRE_DTYPE = jnp.bfloat16

N_DEV = 8
D_MODEL = 1024
DEPTH = 2
CHUNK = 64
ATTN_HEADS = 8
ATTN_HEAD_DIM = 128
ATTN_LEFT_CHUNKS = 8
BAND = (ATTN_LEFT_CHUNKS + 1) * CHUNK
PAD_ROWS = ATTN_LEFT_CHUNKS * CHUNK
REL_CLIP = 2 * CHUNK
N_REL = 2 * REL_CLIP + 1
REL_MIN = REL_CLIP - (CHUNK - 1)
LIN_HEADS = 4
LIN_KEY_DIM = 128
LIN_VAL_DIM = 256
GLA_RANK = 16
GLA_GATE_NORMALIZER = 16.0
ROPE_BASE = 10000.0
D_FF = 4 * D_MODEL
LN_EPS = 1e-5
NEG_INF = -1e30
ALPHA = (2 * DEPTH) ** 0.25
ADAM_LR, ADAM_B1, ADAM_B2, ADAM_EPS, ADAM_WD, ADAM_STEP = 0.001, 0.9, 0.999, 1e-08, 0.01, 10

PROJ_WIDTH = 12304
GLR_SRC = 9216
GATES_SRC = GLR_SRC + GLA_RANK
MAIN_W = 9216
GATES_W = 3 * D_MODEL
GATES_OFF = MAIN_W
GLR_OFF = MAIN_W + GATES_W
PROJ_PAD = 12800
GLR_BLOCK = 128
COL = dict(aq=0, ak=1024, av=2048, rq=3072, rk=3584, rv=4096, rg=5120, gq=6144, gk=6656, gv=7168, gg=8192)

SEG_CHUNKS = 8
SEG = SEG_CHUNKS * CHUNK
VMEM_BIG = 56 * 1024 * 1024

_NN = (((1,), (0,)), ((), ()))
_NT = (((1,), (1,)), ((), ()))
_TN = (((0,), (0,)), ((), ()))


def _mxu(a, b, dn):
    return lax.dot_general(a.astype(MXU_DTYPE), b.astype(MXU_DTYPE), dn, preferred_element_type=F32)


@jax.custom_vjp
def dot_nn(a, b):
    return _mxu(a, b, _NN)


def _dot_nn_fwd(a, b):
    return _mxu(a, b, _NN), (a, b)


def _dot_nn_bwd(res, g):
    a, b = res
    return _mxu(g, b, _NT).astype(a.dtype), _mxu(a, g, _TN).astype(b.dtype)


dot_nn.defvjp(_dot_nn_fwd, _dot_nn_bwd)


@jax.custom_vjp
def dot_nt(a, b):
    return _mxu(a, b, _NT)


def _dot_nt_fwd(a, b):
    return _mxu(a, b, _NT), (a, b)


def _dot_nt_bwd(res, g):
    a, b = res
    return _mxu(g, b, _NN).astype(a.dtype), _mxu(g, a, _TN).astype(b.dtype)


dot_nt.defvjp(_dot_nt_fwd, _dot_nt_bwd)


@jax.custom_vjp
def dot_tn(a, b):
    return _mxu(a, b, _TN)


def _dot_tn_fwd(a, b):
    return _mxu(a, b, _TN), (a, b)


def _dot_tn_bwd(res, g):
    a, b = res
    return _mxu(b, g, _NT).astype(a.dtype), _mxu(a, g, _NN).astype(b.dtype)


dot_tn.defvjp(_dot_tn_fwd, _dot_tn_bwd)


def _tri(n, lower):
    r = lax.broadcasted_iota(jnp.int32, (n, n), 0)
    c = lax.broadcasted_iota(jnp.int32, (n, n), 1)
    return (r >= c) if lower else (r <= c)


def _tri_dot(lower, x):
    tri = _tri(x.shape[0], lower).astype(jnp.bfloat16)
    hi = x.astype(jnp.bfloat16)
    rest = x - hi.astype(F32)
    mid = rest.astype(jnp.bfloat16)
    lo = (rest - mid.astype(F32)).astype(jnp.bfloat16)
    dot = lambda part: lax.dot_general(tri, part, _NN, preferred_element_type=F32)
    return dot(hi) + dot(mid) + dot(lo)


@jax.custom_vjp
def chunk_cumsum(x):
    return _tri_dot(True, x)


def _chunk_cumsum_fwd(x):
    return chunk_cumsum(x), None


def _chunk_cumsum_bwd(_, g):
    return (_tri_dot(False, g),)


chunk_cumsum.defvjp(_chunk_cumsum_fwd, _chunk_cumsum_bwd)


@jax.custom_vjp
def swap_halves(t):
    return pltpu.roll(t, t.shape[-1] // 2, t.ndim - 1)


def _swap_fwd(t):
    return swap_halves(t), None


def _swap_bwd(_, g):
    return (swap_halves(g),)


swap_halves.defvjp(_swap_fwd, _swap_bwd)


def mm(name, a, b, dims, *, tm, tn, tk, out_dtype=F32, extras=(), row_params=(), epi=None, n_out=1, out_dtypes=None,
       side=None, column_slabs=False):
    if dims == "nn":
        (M, K), N = a.shape, b.shape[1]
    elif dims == "nt":
        (M, K), N = a.shape, b.shape[0]
    else:
        (K, M), N = a.shape, b.shape[1]
    tm, tn, tk = min(tm, M), min(tn, N), min(tk, K)
    assert M % tm == 0 and N % tn == 0 and K % tk == 0, (name, M, N, K, tm, tn, tk)
    nk = K // tk
    dn = {"nn": _NN, "nt": _NT, "tn": _TN}[dims]
    a_spec = pl.BlockSpec((tk, tm), lambda i, j, k: (k, i)) if dims == "tn" else pl.BlockSpec((tm, tk), lambda i, j, k: (i, k))
    b_spec = pl.BlockSpec((tn, tk), lambda i, j, k: (j, k)) if dims == "nt" else pl.BlockSpec((tk, tn), lambda i, j, k: (k, j))
    o_spec = pl.BlockSpec((tm, tn), lambda i, j, k: (i, j))
    row_spec = pl.BlockSpec((1, tn), lambda i, j, k: (0, j))
    n_ex = len(extras) + len(row_params)
    out_dtypes = out_dtypes or (out_dtype,) * n_out

    def finish(r, ex_refs, o_refs):
        if epi is not None:
            r = epi(r, *[e[...] for e in ex_refs])
        r = r if isinstance(r, tuple) else (r,)
        for o, v in zip(o_refs, r):
            o[...] = v.astype(o.dtype)

    def body(*refs):
        a_ref, b_ref = refs[:2]
        ex_refs = refs[2:2 + n_ex]
        o_refs = refs[2 + n_ex:2 + n_ex + n_out]
        part = _mxu(a_ref[...], b_ref[...], dn)
        if nk == 1:
            finish(part, ex_refs, o_refs)
            return
        acc = refs[-1]
        k = pl.program_id(2)

        @pl.when(k == 0)
        def _():
            acc[...] = part

        @pl.when(k > 0)
        def _():
            acc[...] += part

        @pl.when(k == nk - 1)
        def _():
            finish(acc[...], ex_refs, o_refs)

    outs = _side_call(
        side, body,
        name=name,
        grid=(M // tm, N // tn, nk),
        in_specs=[a_spec, b_spec] + [o_spec] * len(extras) + [row_spec] * len(row_params),
        out_specs=[pl.BlockSpec((None, tm, tn), lambda i, j, k: (j, i, 0))] * n_out if column_slabs else [o_spec] * n_out,
        out_shape=[jax.ShapeDtypeStruct((N // tn, M, tn) if column_slabs else (M, N), dt) for dt in out_dtypes],
        scratch_shapes=[pltpu.VMEM((tm, tn), F32)] if nk > 1 else [],
        args=[a, b, *extras, *row_params],
        semantics=("parallel", "parallel", "arbitrary"),
    )
    return outs[0] if n_out == 1 else outs


class Row:
    def __init__(self, arr, width=None, col=0, lead=(), rowmap=None):
        self.arr, self.lead, self.col = arr, tuple(lead), col
        self.width = arr.shape[-1] if width is None else width
        self.rowmap = rowmap

    def spec(self, tm):
        lead, col, rowmap = self.lead, self.col, self.rowmap or (lambda i: i)
        return pl.BlockSpec((None,) * len(lead) + (tm, self.width), lambda i: (*lead, rowmap(i), col))


def row_call(name, fn, rows, params, new_outs, red_shapes=(), *, tm, into=None, into_cols=(), stack=None, stacked_onto=None):
    R = rows[0].arr.shape[len(rows[0].lead)]
    tm = min(tm, R)
    assert R % tm == 0, (name, R, tm)
    n_row, n_par, n_new, n_into, n_red = len(rows), len(params), len(new_outs), len(into_cols), len(red_shapes)
    fresh = isinstance(into, jax.ShapeDtypeStruct)

    def body(*refs):
        row_refs = refs[:n_row]
        par_refs = refs[n_row:n_row + n_par]
        pos = n_row + n_par + (1 if (into is not None and not fresh) else 0) + (n_new if stacked_onto is not None else 0)
        new_refs = refs[pos:pos + n_new]
        red_refs = refs[pos + n_new:pos + n_new + n_red]
        pos2 = pos + n_new + n_red
        into_ref = refs[pos2] if into is not None else None
        stage = refs[pos2 + 1:pos2 + 1 + n_into] if into is not None else ()
        i = pl.program_id(0)
        vals = fn(*[r[...] for r in row_refs], *[p[...] for p in par_refs])
        vals = vals if isinstance(vals, (tuple, list)) else (vals,)
        for o, v in zip(new_refs, vals[:n_new]):
            o[...] = v.astype(o.dtype)

        def window_copy(j, step):
            width, off = into_cols[j]
            rows_at = pl.ds(pl.multiple_of(step * tm, tm), tm)
            return pltpu.make_async_copy(stage[j], into_ref.at[rows_at, pl.ds(off, width)], refs[pos2 + 1 + n_into].at[j])

        for j, v in enumerate(vals[n_new:n_new + n_into]):
            @pl.when(i > 0)
            def _(j=j):
                window_copy(j, i - 1).wait()

            stage[j][...] = v.astype(stage[j].dtype)
            window_copy(j, i).start()

            @pl.when(i == R // tm - 1)
            def _(j=j):
                window_copy(j, i).wait()
        for o, v in zip(red_refs, vals[n_new + n_into:]):
            @pl.when(i == 0)
            def _(o=o, v=v):
                o[...] = v.astype(o.dtype)

            @pl.when(i > 0)
            def _(o=o, v=v):
                o[...] += v.astype(o.dtype)

    in_specs = [r.spec(tm) for r in rows] + [pl.BlockSpec(p.shape, lambda i, nd=p.ndim: (0,) * nd) for p in params]
    args = [r.arr for r in rows] + list(params)
    if stack is None:
        out_specs = [pl.BlockSpec((tm, w), lambda i: (i, 0)) for w, _ in new_outs]
        out_shape = [jax.ShapeDtypeStruct((R, w), dt) for w, dt in new_outs]
    else:
        index, depth = stack
        out_specs = [pl.BlockSpec((None, tm, w), lambda i: (index, i, 0)) for w, _ in new_outs]
        out_shape = [jax.ShapeDtypeStruct((depth, R, w), dt) for w, dt in new_outs]
    out_specs += [pl.BlockSpec(s, lambda i, nd=len(s): (0,) * nd) for s in red_shapes]
    out_shape += [jax.ShapeDtypeStruct(s, F32) for s in red_shapes]
    aliases, scratch = {}, []
    if into is not None:
        out_specs.append(pl.BlockSpec(memory_space=pl.ANY))
        out_shape.append(jax.ShapeDtypeStruct(into.shape, into.dtype))
        if not fresh:
            in_specs.append(pl.BlockSpec(memory_space=pl.ANY))
            args.append(into)
            aliases = {len(args) - 1: len(out_shape) - 1}
        scratch = [pltpu.VMEM((tm, w), into.dtype) for w, _ in into_cols] + [pltpu.SemaphoreType.DMA((n_into,))]
    if stacked_onto is not None:
        for j, prev in enumerate(stacked_onto):
            in_specs.append(pl.BlockSpec(memory_space=pl.ANY))
            args.append(prev)
            aliases[len(args) - 1] = j
    outs = pl.pallas_call(
        body,
        name=name,
        grid=(R // tm,),
        in_specs=in_specs,
        out_specs=out_specs,
        out_shape=out_shape,
        scratch_shapes=scratch,
        input_output_aliases=aliases,
        compiler_params=pltpu.CompilerParams(dimension_semantics=("arbitrary",), vmem_limit_bytes=VMEM_BIG),
    )(*args)
    return list(outs)


def vjp_fn(fn, n_row, n_out, passthrough=0):
    def bwd(*args):
        rows = [a.astype(F32) for a in args[:n_row]]
        cts = args[n_row:n_row + n_out]
        thru = args[n_row + n_out:n_row + n_out + passthrough]
        pars = args[n_row + n_out + passthrough:]
        outs, pull = jax.vjp(lambda *xs: tuple(fn(*xs)), *rows, *pars)
        grads = pull(tuple(c.astype(o.dtype) for c, o in zip(cts, outs)))
        return tuple(grads[:n_row]) + tuple(thru) + tuple(grads[n_row:])

    return bwd


def _ln(x, g, b):
    mu = jnp.mean(x, axis=-1, keepdims=True)
    xc = x - mu
    var = jnp.mean(xc * xc, axis=-1, keepdims=True)
    return xc * lax.rsqrt(var + LN_EPS) * g + b


def f_ln_in(x, g, b):
    return (_ln(x, g, b),)


def f_deepnorm(x, y, g, b):
    return (_ln(ALPHA * x + y, g, b),)


def _twice(fn):
    return lambda *args: fn(*args) * 2


STREAM = [(D_MODEL, F32), (D_MODEL, MXU_DTYPE)]


def _rope(t, cos, sin):
    parts = []
    for h in range(LIN_HEADS):
        th = t[:, h * LIN_KEY_DIM:(h + 1) * LIN_KEY_DIM]
        parts.append(th * cos + swap_halves(th) * sin)
    return jnp.concatenate(parts, axis=1)


def f_ret_pre(rq, rk, cos, sin):
    return _rope(rq, cos, sin), _rope(rk, cos, sin) * LIN_KEY_DIM ** -0.5


def _log_sigmoid(x):
    return jnp.minimum(x, 0.0) - jnp.log(1.0 + jnp.exp(-jnp.abs(x)))


def f_gla_pre(gq, glr, wlr, blr):
    gate_pre = dot_nn(glr, wlr) + blr
    return gq * LIN_KEY_DIM ** -0.5, _log_sigmoid(gate_pre) / GLA_GATE_NORMALIZER


def f_normgate(ret_raw, rg, gla_raw, gg, gnorm):
    rets, glas = [], []
    for h in range(LIN_HEADS):
        sl = slice(h * LIN_VAL_DIM, (h + 1) * LIN_VAL_DIM)
        t = ret_raw[:, sl]
        mu = jnp.mean(t, axis=-1, keepdims=True)
        tc = t - mu
        rets.append(tc * lax.rsqrt(jnp.mean(tc * tc, axis=-1, keepdims=True) + LN_EPS))
        u = gla_raw[:, sl]
        glas.append(u * lax.rsqrt(jnp.mean(u * u, axis=-1, keepdims=True) + LN_EPS) * gnorm)
    ret = jax.nn.silu(rg) * jnp.concatenate(rets, axis=1)
    gla = jax.nn.silu(gg) * jnp.concatenate(glas, axis=1)
    return ret, gla


def f_merge(p0, p1, p2, gl):
    out = jax.nn.sigmoid(gl[:, :D_MODEL]) * p0
    out = out + jax.nn.sigmoid(gl[:, D_MODEL:2 * D_MODEL]) * p1
    return (out + jax.nn.sigmoid(gl[:, 2 * D_MODEL:]) * p2,)


def f_loss(y, tgt):
    diff = y - tgt
    sq = jnp.sum(diff * diff, axis=0, keepdims=True)
    lanes = sq[:, :128]
    for j in range(1, D_MODEL // 128):
        lanes = lanes + sq[:, j * 128:(j + 1) * 128]
    return diff * (1.0 / D_MODEL), lanes * (0.5 / D_MODEL)


def _adam(g, w, m, v):
    m = ADAM_B1 * m + (1.0 - ADAM_B1) * g
    v = ADAM_B2 * v + (1.0 - ADAM_B2) * jnp.square(g)
    m_hat = m / (1.0 - ADAM_B1 ** ADAM_STEP)
    v_hat = v / (1.0 - ADAM_B2 ** ADAM_STEP)
    delta = -ADAM_LR * (m_hat / (jnp.sqrt(v_hat) + ADAM_EPS) + ADAM_WD * w)
    return delta, m, v


def f_sum_adam(*args):
    g = args[0].astype(F32)
    for p in args[1:-3]:
        g = g + p.astype(F32)
    return (g,) + _adam(g, *args[-3:])


def f_adam(g, w, m, v):
    return _adam(g, w, m, v)


GROUP = 4
GROUP_ROWS = GROUP * CHUNK
GROUP_KEYS = (GROUP + ATTN_LEFT_CHUNKS) * CHUNK
DIAG_ROWS = REL_CLIP + CHUNK
GRAD_ROWS = 200


def _blocks_by_delta():
    blocks = {}
    for qi in range(GROUP):
        for J in range(GROUP_KEYS // 128):
            blocks.setdefault(qi - 2 * J, []).append((qi, J))
    return blocks


def bias_table(rel_bias, side=None):
    n_tab = rel_bias.shape[0] * ATTN_HEADS

    def body(rb_ref, o_ref):
        h = pl.program_id(0)
        rm = lax.broadcasted_iota(jnp.int32, (CHUNK, 128), 0)
        cm = lax.broadcasted_iota(jnp.int32, (CHUNK, 128), 1)
        ahead = jnp.where(cm >= CHUNK, 1, 0)
        for delta, where in _blocks_by_delta().items():
            base = 64 * delta + PAD_ROWS
            rel = jnp.minimum(base + rm - cm, REL_CLIP) + REL_CLIP
            in_band = (ahead - delta >= 0) & (ahead - delta <= ATTN_LEFT_CHUNKS)
            lo = max(REL_MIN, min(base - 127, REL_CLIP) + REL_CLIP)
            hi = min(base + CHUNK - 1, REL_CLIP) + REL_CLIP

            def step(r, acc, rel=rel):
                return jnp.where(rel == r, rb_ref[h * N_REL + r], acc)

            blk = jnp.zeros((CHUNK, 128), F32) if hi < lo else lax.fori_loop(lo, hi + 1, step, jnp.zeros((CHUNK, 128), F32))
            blk = jnp.where(in_band, blk, NEG_INF)
            for qi, J in where:
                o_ref[pl.ds(qi * CHUNK, CHUNK), pl.ds(J * 128, 128)] = blk

    return _side_call(
        side, body,
        name="bias_table",
        grid=(n_tab,),
        in_specs=[pl.BlockSpec(memory_space=pltpu.SMEM)],
        out_specs=[pl.BlockSpec((None, GROUP_ROWS, GROUP_KEYS), lambda h: (h, 0, 0))],
        out_shape=[jax.ShapeDtypeStruct((n_tab, GROUP_ROWS, GROUP_KEYS), F32)],
        scratch_shapes=[],
        args=[rel_bias.reshape(n_tab * N_REL)],
    )[0]


def bias_table_grad(dbias):
    blocks = _blocks_by_delta()
    below_clip = range(-(CHUNK - 1), REL_CLIP)
    deltas = [d for d in sorted(blocks) if any(-127 <= pos - 64 * d - PAD_ROWS <= CHUNK - 1 for pos in below_clip)]

    def body(d_ref, o_ref, diag):
        sums = []
        for delta in deltas:
            acc = None
            for qi, J in blocks[delta]:
                blk = d_ref[pl.ds(qi * CHUNK, CHUNK), pl.ds(J * 128, 128)]
                acc = blk if acc is None else acc + blk
            sums.append(acc)
        off = lax.broadcasted_iota(jnp.int32, (CHUNK, 128), 0) - lax.broadcasted_iota(jnp.int32, (CHUNK, 128), 1)
        diag[...] = jnp.zeros_like(diag)

        def step(t, carry):
            pos = t - (CHUNK - 1)
            picked = jnp.zeros((CHUNK, 128), F32)
            for delta, s in zip(deltas, sums):
                picked = picked + jnp.where(off == pos - 64 * delta - PAD_ROWS, s, 0.0)
            diag[pl.ds(t, 1), :] = jnp.sum(picked, axis=0, keepdims=True)
            return carry

        lax.fori_loop(0, DIAG_ROWS, step, 0, unroll=8)
        d = d_ref[...]
        total = jnp.sum(jnp.sum(d, axis=1, keepdims=True), axis=0, keepdims=True)
        o_ref[pl.ds(0, DIAG_ROWS), :] = jnp.broadcast_to(jnp.sum(diag[...], axis=1, keepdims=True), (DIAG_ROWS, 128))
        o_ref[pl.ds(DIAG_ROWS, GRAD_ROWS - DIAG_ROWS), :] = jnp.broadcast_to(total, (GRAD_ROWS - DIAG_ROWS, 128))

    out = pl.pallas_call(
        body,
        name="bias_table_grad",
        grid=(ATTN_HEADS,),
        in_specs=[pl.BlockSpec((None, GROUP_ROWS, GROUP_KEYS), lambda h: (h, 0, 0))],
        out_specs=pl.BlockSpec((None, GRAD_ROWS, 128), lambda h: (h, 0, 0)),
        out_shape=jax.ShapeDtypeStruct((ATTN_HEADS, GRAD_ROWS, 128), F32),
        scratch_shapes=[pltpu.VMEM((DIAG_ROWS, 128), F32)],
        compiler_params=pltpu.CompilerParams(dimension_semantics=("arbitrary",)),
    )(dbias)
    below = out[:, :DIAG_ROWS - 1, 0]
    clipped = out[:, DIAG_ROWS, 0] - jnp.sum(below, axis=1)
    return jnp.concatenate([jnp.zeros((ATTN_HEADS, REL_MIN), F32), below, clipped[:, None]], axis=1)


def _attn_stage(q_ref, k_ref, v_ref, qs, kp, vp, T):
    qs[...] = (q_ref[...] * ATTN_HEAD_DIM ** -0.5).astype(qs.dtype)
    for dst, src in ((kp, k_ref), (vp, v_ref)):
        dst[pl.ds(0, PAD_ROWS), :] = jnp.zeros((PAD_ROWS, ATTN_HEAD_DIM), dst.dtype)
        dst[pl.ds(PAD_ROWS, T), :] = src[...].astype(dst.dtype)


def _attn_probs(g, first_groups, qs, kp, b_ref):
    r0 = g * GROUP_ROWS if first_groups else pl.multiple_of(g * GROUP_ROWS, GROUP_ROWS)
    qc = qs[pl.ds(r0, GROUP_ROWS), :]
    kb = kp[pl.ds(r0, GROUP_KEYS), :]
    s = _mxu(qc, kb, _NT) + b_ref[...]
    if first_groups:
        k_chunk = lax.shift_right_logical(lax.broadcasted_iota(jnp.int32, (GROUP_ROWS, GROUP_KEYS), 1), 6)
        s = jnp.where(g * GROUP + k_chunk >= ATTN_LEFT_CHUNKS, s, NEG_INF)
    e = jnp.exp(s - jnp.max(s, axis=-1, keepdims=True))
    return r0, qc, kb, e * (1.0 / jnp.sum(e, axis=-1, keepdims=True))


def _for_groups(T, step):
    n_first = min(ATTN_LEFT_CHUNKS // GROUP, T // GROUP_ROWS)
    for g in range(n_first):
        step(g, True)
    n_loop = T // GROUP_ROWS - n_first
    lax.fori_loop(n_first, T // GROUP_ROWS, lambda g, c: (step(g, False), c)[1], 0, unroll=2 if n_loop % 2 == 0 else 1)


def _head_spec(T, base):
    return pl.BlockSpec((T, ATTN_HEAD_DIM), lambda h: (0, base + h))


def _attn_scratch(T):
    return [pltpu.VMEM((T, ATTN_HEAD_DIM), MXU_DTYPE)] + [pltpu.VMEM((T + PAD_ROWS, ATTN_HEAD_DIM), MXU_DTYPE)] * 2


def attn_fwd(proj, bias_tab, first_tab, side=None):
    T = proj.shape[0]

    def body(q_ref, k_ref, v_ref, b_ref, o_ref, qs, kp, vp):
        _attn_stage(q_ref, k_ref, v_ref, qs, kp, vp, T)

        def step(g, first):
            r0, _, _, p = _attn_probs(g, first, qs, kp, b_ref)
            o_ref[pl.ds(r0, GROUP_ROWS), :] = _mxu(p, vp[pl.ds(r0, GROUP_KEYS), :], _NN).astype(o_ref.dtype)

        _for_groups(T, step)

    out, = _side_call(
        side, body,
        name="attn_fwd",
        grid=(ATTN_HEADS,),
        in_specs=[_head_spec(T, COL["aq"] // 128), _head_spec(T, COL["ak"] // 128), _head_spec(T, COL["av"] // 128),
                  pl.BlockSpec((None, GROUP_ROWS, GROUP_KEYS), lambda h: (first_tab + h, 0, 0))],
        out_specs=[pl.BlockSpec((T, ATTN_HEAD_DIM), lambda h: (0, h))],
        out_shape=[jax.ShapeDtypeStruct((T, ATTN_HEADS * ATTN_HEAD_DIM), MXU_DTYPE)],
        scratch_shapes=_attn_scratch(T),
        args=[proj, proj, proj, bias_tab],
    )
    return out


def attn_bwd(proj, bias_tab, first_tab, dout, dproj, side=None):
    T = proj.shape[0]

    def body(q_ref, k_ref, v_ref, b_ref, do_ref, dproj_in, dbias_ref, dproj_ref, qs, kp, vp, dkp, dvp, dq, stage):
        del dproj_in
        h = pl.program_id(0)
        _attn_stage(q_ref, k_ref, v_ref, qs, kp, vp, T)
        dkp[...] = jnp.zeros_like(dkp)
        dvp[...] = jnp.zeros_like(dvp)
        dbias_ref[...] = jnp.zeros_like(dbias_ref)

        def step(g, first):
            r0, qc, kb, p = _attn_probs(g, first, qs, kp, b_ref)
            do = do_ref[pl.ds(r0, GROUP_ROWS), :]
            dvp[pl.ds(r0, GROUP_KEYS), :] += _mxu(p, do, _TN)
            dp = _mxu(do, vp[pl.ds(r0, GROUP_KEYS), :], _NT)
            ds = p * (dp - jnp.sum(dp * p, axis=-1, keepdims=True))
            dq[pl.ds(r0, GROUP_ROWS), :] = _mxu(ds, kb, _NN) * ATTN_HEAD_DIM ** -0.5
            dkp[pl.ds(r0, GROUP_KEYS), :] += _mxu(ds, qc, _TN)
            dbias_ref[...] += ds

        _for_groups(T, step)
        col = pl.multiple_of(h * ATTN_HEAD_DIM, ATTN_HEAD_DIM)
        for grad, first_row, base in ((dq, 0, "aq"), (dkp, PAD_ROWS, "ak"), (dvp, PAD_ROWS, "av")):
            stage[...] = grad[pl.ds(first_row, T), :].astype(stage.dtype)
            pltpu.sync_copy(stage, dproj_ref.at[:, pl.ds(COL[base] + col, ATTN_HEAD_DIM)])

    dbias, dproj = _side_call(
        side, body,
        name="attn_bwd",
        grid=(ATTN_HEADS,),
        in_specs=[_head_spec(T, COL["aq"] // 128), _head_spec(T, COL["ak"] // 128), _head_spec(T, COL["av"] // 128),
                  pl.BlockSpec((None, GROUP_ROWS, GROUP_KEYS), lambda h: (first_tab + h, 0, 0)),
                  pl.BlockSpec((T, ATTN_HEAD_DIM), lambda h: (0, h)),
                  pl.BlockSpec(memory_space=pl.ANY)],
        out_specs=[pl.BlockSpec((None, GROUP_ROWS, GROUP_KEYS), lambda h: (h, 0, 0)), pl.BlockSpec(memory_space=pl.ANY)],
        out_shape=[jax.ShapeDtypeStruct((ATTN_HEADS, GROUP_ROWS, GROUP_KEYS), F32),
                   jax.ShapeDtypeStruct(dproj.shape, dproj.dtype)],
        scratch_shapes=_attn_scratch(T) + [pltpu.VMEM((T + PAD_ROWS, ATTN_HEAD_DIM), F32)] * 2
        + [pltpu.VMEM((T, ATTN_HEAD_DIM), F32), pltpu.VMEM((T, ATTN_HEAD_DIM), dproj.dtype)],
        args=[proj, proj, proj, bias_tab, dout, dproj],
        aliases={5: 1},
    )
    return dproj, dbias


def _lin_segment(q, k, v, la, st):
    lower = _tri(CHUNK, True)
    steps = (lax.broadcasted_iota(jnp.int32, (CHUNK, 1), 0) + 1).astype(F32)
    outs = []
    for i in range(SEG_CHUNKS):
        sl = slice(i * CHUNK, (i + 1) * CHUNK)
        qi, ki, vi = q[sl], k[sl], v[sl]
        if la.shape[0] == 1:
            cum, last = la * steps, la * float(CHUNK)
        else:
            cum, last = chunk_cumsum(la[sl]), jnp.sum(la[sl], axis=0, keepdims=True)
        e_pos, e_neg = jnp.exp(cum), jnp.exp(-cum)
        q_fwd = qi * e_pos
        s_fwd = dot_nt(q_fwd, ki * e_neg)
        s_bwd = dot_nt(qi * e_neg, ki * e_pos)
        o = dot_nn(jnp.where(lower, s_fwd, s_bwd), vi) + dot_nt(q_fwd, st)
        st = jnp.exp(last) * st + dot_tn(vi, ki * jnp.exp(last - cum))
        outs.append(o)
    return jnp.concatenate(outs, axis=0), st


class Win:
    def __init__(self, arr, width, base=0):
        self.arr, self.width, self.base = arr, width, base


HEADS_PER_STEP = 2


def _lin_spec(w, seg_of):
    assert w.base % HEADS_PER_STEP == 0
    first, width = w.base // HEADS_PER_STEP, HEADS_PER_STEP * w.width
    if w.arr.shape[0] == 1:
        return pl.BlockSpec((1, width), lambda p, s: (0, first + p))
    return pl.BlockSpec((SEG, width), lambda p, s: (seg_of(s), first + p))


def _head_cols(ref, j, width):
    return ref[:, j * width:(j + 1) * width]


def _decay_rows(la_ref, j):
    return _head_cols(la_ref, j, LIN_KEY_DIM)


_STATE_BLOCK = (HEADS_PER_STEP, None, LIN_VAL_DIM, LIN_KEY_DIM)


def lin_fwd(name, q, k, v, la, side=None):
    T = q.arr.shape[0]
    nseg = T // SEG

    def body(q_ref, k_ref, v_ref, la_ref, o_ref, st_ref, st):
        @pl.when(pl.program_id(1) == 0)
        def _():
            st[...] = jnp.zeros_like(st)

        for j in range(HEADS_PER_STEP):
            st0 = st[j]
            st_ref[j] = st0
            o, st1 = _lin_segment(_head_cols(q_ref, j, LIN_KEY_DIM), _head_cols(k_ref, j, LIN_KEY_DIM),
                                  _head_cols(v_ref, j, LIN_VAL_DIM), _decay_rows(la_ref, j), st0)
            o_ref[:, j * LIN_VAL_DIM:(j + 1) * LIN_VAL_DIM] = o
            st[j] = st1

    fwd = lambda s: s
    out, states = _side_call(
        side, body,
        name=name,
        grid=(LIN_HEADS // HEADS_PER_STEP, nseg),
        in_specs=[_lin_spec(w, fwd) for w in (q, k, v, la)],
        out_specs=[pl.BlockSpec((SEG, HEADS_PER_STEP * LIN_VAL_DIM), lambda p, s: (s, p)),
                   pl.BlockSpec(_STATE_BLOCK, lambda p, s: (p, s, 0, 0))],
        out_shape=[jax.ShapeDtypeStruct((T, LIN_HEADS * LIN_VAL_DIM), F32),
                   jax.ShapeDtypeStruct((LIN_HEADS, nseg, LIN_VAL_DIM, LIN_KEY_DIM), F32)],
        scratch_shapes=[pltpu.VMEM((HEADS_PER_STEP, LIN_VAL_DIM, LIN_KEY_DIM), F32)],
        args=[q.arr, k.arr, v.arr, la.arr],
    )
    return out, states


def lin_bwd(name, q, k, v, la, states, dout, with_decay_grad, side=None):
    T = q.arr.shape[0]
    nseg = T // SEG
    n_out = 4 if with_decay_grad else 3

    widths = [LIN_KEY_DIM, LIN_KEY_DIM, LIN_VAL_DIM] + ([LIN_KEY_DIM] if with_decay_grad else [])

    def body(q_ref, k_ref, v_ref, la_ref, st_ref, do_ref, *rest):
        outs, dst = rest[:n_out], rest[n_out]

        @pl.when(pl.program_id(1) == 0)
        def _():
            dst[...] = jnp.zeros_like(dst)

        for j in range(HEADS_PER_STEP):
            _, pull = jax.vjp(_lin_segment, _head_cols(q_ref, j, LIN_KEY_DIM).astype(F32),
                              _head_cols(k_ref, j, LIN_KEY_DIM).astype(F32), _head_cols(v_ref, j, LIN_VAL_DIM).astype(F32),
                              _decay_rows(la_ref, j), st_ref[j])
            grads = pull((_head_cols(do_ref, j, LIN_VAL_DIM), dst[j]))
            for o, g, w in zip(outs, grads[:n_out], widths):
                o[:, j * w:(j + 1) * w] = g
            dst[j] = grads[4]

    rev = lambda s: nseg - 1 - s
    key_spec = pl.BlockSpec((SEG, HEADS_PER_STEP * LIN_KEY_DIM), lambda p, s: (rev(s), p))
    val_spec = pl.BlockSpec((SEG, HEADS_PER_STEP * LIN_VAL_DIM), lambda p, s: (rev(s), p))
    out_specs = [key_spec, key_spec, val_spec] + ([key_spec] if with_decay_grad else [])
    return _side_call(
        side, body,
        name=name,
        grid=(LIN_HEADS // HEADS_PER_STEP, nseg),
        in_specs=[_lin_spec(w, rev) for w in (q, k, v, la)] + [
            pl.BlockSpec(_STATE_BLOCK, lambda p, s: (p, rev(s), 0, 0)), val_spec],
        out_specs=out_specs,
        out_shape=[jax.ShapeDtypeStruct((T, LIN_HEADS * w), F32) for w in widths],
        scratch_shapes=[pltpu.VMEM((HEADS_PER_STEP, LIN_VAL_DIM, LIN_KEY_DIM), F32)],
        args=[q.arr, k.arr, v.arr, la.arr, states, dout],
    )


def _coords():
    return lax.axis_index("x"), lax.axis_index("y"), lax.axis_index("c")


def _peer(k):
    x, y, c = _coords()
    px = 1 - x if k & 4 else x
    py = 1 - y if k & 2 else y
    pc = 1 - c if k & 1 else c
    return (px, py, pc), 4 * px + 2 * py + pc


def _remote(src, dst, send_sem, recv_sem, device):
    return pltpu.make_async_remote_copy(src_ref=src, dst_ref=dst, send_sem=send_sem, recv_sem=recv_sem,
                                        device_id=device, device_id_type=pl.DeviceIdType.MESH)


class Side:
    def __init__(self, kind, arrays):
        self.kind, self.arrays, self.n = kind, list(arrays), len(arrays)
        self.outs_per = 1
        self.slots = {"gather": N_DEV - 1, "pair": 4, "chips": 3}[kind]

    def out_shape(self):
        if self.kind == "gather":
            return [jax.ShapeDtypeStruct((N_DEV,) + a.shape, a.dtype) for a in self.arrays]
        return [jax.ShapeDtypeStruct((4,) + a.shape[1:], a.dtype) for a in self.arrays]

    def sems(self):
        return [pltpu.SemaphoreType.DMA((self.n, self.slots)), pltpu.SemaphoreType.DMA((self.n, self.slots)),
                pltpu.SemaphoreType.DMA((self.n,))]

    def start(self, ins, outs, sems):
        send, recv, own = sems
        x, y, c = _coords()
        me = 4 * x + 2 * y + c
        for a in range(self.n):
            if self.kind == "gather":
                pltpu.make_async_copy(ins[a], outs[a].at[me], own.at[a]).start()
                for k in (1, 2, 4, 6):
                    peer, _ = _peer(k)
                    _remote(ins[a], outs[a].at[me], send.at[a, k - 1], recv.at[a, k - 1], peer).start()
            elif self.kind == "pair":
                for j in range(4):
                    _remote(ins[a].at[2 * j + 1 - c], outs[a].at[j], send.at[a, j], recv.at[a, j], (x, y, 1 - c)).start()
            else:
                chip = 2 * x + y
                pltpu.make_async_copy(ins[a].at[chip], outs[a].at[chip], own.at[a]).start()
                for d in (1, 2, 3):
                    tx, ty = (1 - x if d & 2 else x), (1 - y if d & 1 else y)
                    _remote(ins[a].at[2 * tx + ty], outs[a].at[chip], send.at[a, d - 1], recv.at[a, d - 1], (tx, ty, c)).start()

    def finish(self, ins, outs, sems):
        send, recv, own = sems
        x, y, c = _coords()
        me = 4 * x + 2 * y + c
        sibling = (x, y, 1 - c)
        for a in range(self.n):
            if self.kind == "gather":
                blk = outs[a].at[me]
                for k in (2, 4, 6):
                    _, pidx = _peer(k)
                    _remote(blk, blk, send.at[a, k - 1], recv.at[a, k - 1], sibling).wait_recv()
                    _remote(outs[a].at[pidx], outs[a].at[pidx], send.at[a, k], recv.at[a, k], sibling).start()
                pltpu.make_async_copy(ins[a], blk, own.at[a]).wait()
                for k in range(1, N_DEV):
                    _remote(blk, blk, send.at[a, k - 1], recv.at[a, k - 1], sibling).wait_send()
                for k in (1, 3, 5, 7):
                    _remote(blk, blk, send.at[a, k - 1], recv.at[a, k - 1], sibling).wait_recv()
            elif self.kind == "pair":
                for j in range(4):
                    _remote(ins[a].at[j], outs[a].at[j], send.at[a, j], recv.at[a, j], sibling).wait()
            else:
                slab = outs[a].at[0]
                pltpu.make_async_copy(slab, slab, own.at[a]).wait()
                for d in (1, 2, 3):
                    _remote(slab, slab, send.at[a, d - 1], recv.at[a, d - 1], sibling).wait()


def _side_call(side, body, *, name, grid, in_specs, out_specs, out_shape, scratch_shapes, args, semantics=None,
               aliases=None):
    in_specs, out_specs, out_shape = list(in_specs), list(out_specs), list(out_shape)
    scratch_shapes, args = list(scratch_shapes), list(args)
    n_in, n_out = len(in_specs), len(out_specs)
    kernel_body = body
    if side is not None:
        n_so = side.n * side.outs_per

        def kernel_body(*refs):
            ins, side_ins = refs[:n_in], refs[n_in:n_in + side.n]
            pos = n_in + side.n
            outs, side_outs = refs[pos:pos + n_out], refs[pos + n_out:pos + n_out + n_so]
            rest = refs[pos + n_out + n_so:]
            scratch, sems = rest[:-3], rest[-3:]
            ids = [pl.program_id(ax) for ax in range(len(grid))]
            first = functools.reduce(jnp.logical_and, [i == 0 for i in ids])
            last = functools.reduce(jnp.logical_and, [i == g - 1 for i, g in zip(ids, grid)])

            @pl.when(first)
            def _():
                side.start(side_ins, side_outs, sems)

            body(*ins, *outs, *scratch)

            @pl.when(last)
            def _():
                side.finish(side_ins, side_outs, sems)

        any_spec = pl.BlockSpec(memory_space=pl.ANY)
        in_specs += [any_spec] * side.n
        args += side.arrays
        out_specs += [any_spec] * n_so
        out_shape += side.out_shape()
        scratch_shapes += side.sems()
        semantics = ("arbitrary",) * len(grid)
    outs = pl.pallas_call(
        kernel_body,
        name=name,
        grid=grid,
        in_specs=in_specs,
        out_specs=out_specs,
        out_shape=out_shape,
        scratch_shapes=scratch_shapes,
        input_output_aliases=aliases or {},
        compiler_params=pltpu.CompilerParams(dimension_semantics=semantics or ("arbitrary",) * len(grid),
                                             vmem_limit_bytes=VMEM_BIG),
    )(*args)
    if side is not None:
        side.results = list(outs[n_out:])
    return list(outs[:n_out])


def carrier(name, side):
    _side_call(side, lambda: None, name=name, grid=(1,), in_specs=[], out_specs=[], out_shape=[], scratch_shapes=[], args=[])
    return side.results


def all_reduce_small(vec):
    def body(v_ref, o_ref, land, send_sems, recv_sems):
        x, y, c = _coords()
        me = 4 * x + 2 * y + c
        land[me] = v_ref[...]
        copies = []
        for k in range(1, N_DEV):
            peer, _ = _peer(k)
            cp = pltpu.make_async_remote_copy(
                src_ref=v_ref, dst_ref=land.at[me], send_sem=send_sems.at[k - 1], recv_sem=recv_sems.at[k - 1],
                device_id=peer, device_id_type=pl.DeviceIdType.MESH)
            cp.start()
            copies.append(cp)
        for cp in copies:
            cp.wait()
        total = land[0]
        for s in range(1, N_DEV):
            total = total + land[s]
        o_ref[...] = total

    return pl.pallas_call(
        body,
        name="all_reduce_small",
        in_specs=[pl.BlockSpec(memory_space=pltpu.VMEM)],
        out_specs=pl.BlockSpec(memory_space=pltpu.VMEM),
        out_shape=jax.ShapeDtypeStruct(vec.shape, F32),
        scratch_shapes=[pltpu.VMEM((N_DEV,) + vec.shape, F32), pltpu.SemaphoreType.DMA((N_DEV - 1,)),
                        pltpu.SemaphoreType.DMA((N_DEV - 1,))],
    )(vec)


def _pack(parts):
    flat = jnp.concatenate([p.reshape(-1).astype(F32) for p in parts])
    n = flat.shape[0]
    rows = -(-n // 1024) * 8
    return jnp.pad(flat, (0, rows * 128 - n)).reshape(rows, 128)


def _unpack(packed, shapes):
    flat, out, pos = packed.reshape(-1), [], 0
    for s in shapes:
        n = math.prod(s)
        out.append(flat[pos:pos + n].reshape(s))
        pos += n
    return out


def _row2(v):
    return v.reshape(1, -1)


def _rider(hooks, name, ctx=None):
    if name not in hooks:
        return None, lambda: None
    side, done = hooks[name](ctx)
    return side, lambda: done(side.results)


def _layer_forward(h, hb, W, consts, hooks):
    T = h.shape[0]
    cos, sin, ret_decay = consts
    side, landed = _rider(hooks, "proj")
    proj = mm("proj", hb, W["in"], "nn", tm=4096, tn=512, tk=1024, side=side)
    landed()
    bias_tab, first_tab = W["bias_tab"]
    side, landed = _rider(hooks, "attn_fwd")
    attn = attn_fwd(proj, bias_tab, first_tab, side=side)
    landed()
    rq2, rk2 = row_call("ret_pre", f_ret_pre, [Row(proj, 512, COL["rq"] // 512), Row(proj, 512, COL["rk"] // 512),
                                              Row(cos), Row(sin)], [], [(512, F32), (512, F32)], tm=512)
    side, landed = _rider(hooks, "ret_fwd")
    ret_raw, ret_st = lin_fwd("ret_fwd", Win(rq2, 128), Win(rk2, 128), Win(proj, 256, COL["rv"] // 256), Win(ret_decay, 128),
                              side=side)
    landed()
    gq2, gla_la = row_call("gla_pre", f_gla_pre, [Row(proj, 512, COL["gq"] // 512), Row(proj, GLR_BLOCK, GLR_OFF // GLR_BLOCK)],
                           [W["wlr"], W["blr"]], [(512, F32), (512, F32)], tm=512)
    side, landed = _rider(hooks, "gla_fwd")
    gla_raw, gla_st = lin_fwd("gla_fwd", Win(gq2, 128), Win(proj, 128, COL["gk"] // 128), Win(proj, 256, COL["gv"] // 256),
                              Win(gla_la, 128), side=side)
    landed()
    ret_b, gla_b = row_call("normgate", f_normgate, [Row(ret_raw), Row(proj, 1024, COL["rg"] // 1024), Row(gla_raw),
                                                    Row(proj, 1024, COL["gg"] // 1024)], [W["gnorm"]],
                            [(1024, MXU_DTYPE), (1024, MXU_DTYPE)], tm=256)
    branches = (attn, ret_b, gla_b)
    p = [mm("branch_proj", branches[n], W["branch"][n], "nn", tm=1024, tn=1024, tk=1024) for n in range(3)]
    merged, = row_call("merge", f_merge, [Row(p[0]), Row(p[1]), Row(p[2]), Row(proj, GATES_W, GATES_OFF // GATES_W)], [],
                       [(1024, MXU_DTYPE)], tm=256)
    post_norm = lambda y, x, g, b: f_deepnorm(x, y, g, b) * 2 + (y,)
    x1, x1b, mix = mm("out_proj_ln1", merged, W["out"], "nn", tm=512, tn=D_MODEL, tk=1024, extras=(h,),
                      row_params=(W["ln1_g"], W["ln1_b"]), epi=post_norm, n_out=3, out_dtypes=(F32, MXU_DTYPE, F32))
    side, landed = _rider(hooks, "mlp_up")
    u, act = mm("mlp_up", x1b, W["up"], "nn", tm=2048, tn=512, tk=1024, n_out=2, out_dtypes=(F32, MXU_DTYPE),
                epi=lambda r: (r, jnp.square(jnp.maximum(r, 0.0))), side=side)
    landed()
    x2, x2b, y = mm("mlp_down_ln2", act, W["down"], "nn", tm=1024, tn=D_MODEL, tk=1024, extras=(x1,),
                    row_params=(W["ln2_g"], W["ln2_b"]), epi=post_norm, n_out=3, out_dtypes=(F32, MXU_DTYPE, F32))
    saved = dict(h=h, hb=hb, proj=proj, rq2=rq2, rk2=rk2, ret_st=ret_st, gq2=gq2, gla_la=gla_la,
                 gla_st=gla_st, ret_raw=ret_raw, gla_raw=gla_raw, branches=branches, p=p, merged=merged, mix=mix, x1=x1, x1b=x1b,
                 u=u, act=act, y=y)
    return x2, x2b, saved


def _layer_backward(dx2, S, W, consts, hooks):
    cos, sin, ret_decay = consts
    h, proj, T = S["h"], S["proj"], S["h"].shape[0]
    add = lambda r, e: r + e
    dx1a, dy, dg2, db2 = row_call("ln2_bwd", vjp_fn(f_deepnorm, 2, 1), [Row(S["x1"]), Row(S["y"]), Row(dx2)],
                                  [W["ln2_g"], W["ln2_b"]], [(1024, F32), (1024, MXU_DTYPE)], [(1, 1024), (1, 1024)], tm=256)
    side, landed = _rider(hooks, "mlp_down_bwd")
    du = mm("mlp_down_bwd", dy, W["down"], "nt", tm=1024, tn=512, tk=1024, extras=(S["u"],),
            epi=lambda r, u: r * (2.0 * jnp.maximum(u, 0.0)), out_dtype=MXU_DTYPE, side=side)
    landed()
    g_down = mm("mlp_down_wgrad", S["act"], dy, "tn", tm=1024, tn=1024, tk=2048, out_dtype=WIRE_DTYPE)
    dx1 = mm("mlp_up_bwd", du, W["up"], "nt", tm=1024, tn=1024, tk=2048, extras=(dx1a,), epi=add)
    g_up = mm("mlp_up_wgrad", S["x1b"], du, "tn", tm=1024, tn=D_FF // N_DEV, tk=2048, out_dtype=WIRE_DTYPE,
              column_slabs=True)
    dha, dmix, dg1, db1 = row_call("ln1_bwd", vjp_fn(f_deepnorm, 2, 1), [Row(h), Row(S["mix"]), Row(dx1)],
                                   [W["ln1_g"], W["ln1_b"]], [(1024, F32), (1024, MXU_DTYPE)], [(1, 1024), (1, 1024)], tm=256)
    dmerged = mm("out_proj_bwd", dmix, W["out"], "nt", tm=1024, tn=1024, tk=1024)
    g_out = mm("out_proj_wgrad", S["merged"], dmix, "tn", tm=1024, tn=1024, tk=2048, out_dtype=WIRE_DTYPE)
    dproj = jax.ShapeDtypeStruct((T, PROJ_PAD), MXU_DTYPE)
    p = S["p"]
    dp0, dp1, dp2, dproj = row_call(
        "merge_bwd", vjp_fn(f_merge, 4, 1), [Row(p[0]), Row(p[1]), Row(p[2]), Row(proj, GATES_W, GATES_OFF // GATES_W), Row(dmerged)],
        [], [(1024, MXU_DTYPE)] * 3, tm=256, into=dproj, into_cols=[(GATES_W, GATES_OFF)])
    dps = (dp0, dp1, dp2)
    dbr = [mm("branch_proj_bwd", dps[n], W["branch"][n], "nt", tm=1024, tn=1024, tk=1024,
              out_dtype=MXU_DTYPE if n == 0 else F32) for n in range(3)]
    g_branch = [mm("branch_proj_wgrad", S["branches"][n], dps[n], "tn", tm=1024, tn=1024, tk=2048, out_dtype=WIRE_DTYPE)
                for n in range(3)]
    dret_raw, dgla_raw, dgnorm, dproj = row_call(
        "normgate_bwd", vjp_fn(lambda a, b, c, d, g: _reorder_normgate(a, b, c, d, g), 4, 2),
        [Row(S["ret_raw"]), Row(S["gla_raw"]), Row(proj, 1024, COL["rg"] // 1024), Row(proj, 1024, COL["gg"] // 1024),
         Row(dbr[1]), Row(dbr[2])], [W["gnorm"]], [(1024, F32), (1024, F32)], [(1, LIN_VAL_DIM)], tm=256,
        into=dproj, into_cols=[(1024, COL["rg"]), (1024, COL["gg"])])
    side, landed = _rider(hooks, "gla_bwd")
    dgq2, dgk, dgv, dgla_la = lin_bwd("gla_bwd", Win(S["gq2"], 128), Win(proj, 128, COL["gk"] // 128),
                                      Win(proj, 256, COL["gv"] // 256), Win(S["gla_la"], 128), S["gla_st"], dgla_raw, True,
                                      side=side)
    landed()
    dwlr, dblr, dproj = row_call(
        "gla_pre_bwd", _gla_pre_bwd, [Row(proj, 512, COL["gq"] // 512), Row(proj, GLR_BLOCK, GLR_OFF // GLR_BLOCK),
                                      Row(dgq2), Row(dgla_la), Row(dgk), Row(dgv)], [W["wlr"], W["blr"]], [],
        [(GLR_BLOCK, 512), (1, 512)], tm=256, into=dproj,
        into_cols=[(512, COL["gq"]), (PROJ_PAD - GLR_OFF, GLR_OFF), (512, COL["gk"]), (1024, COL["gv"])])
    side, landed = _rider(hooks, "ret_bwd", dict(branch=g_branch, out=g_out, up=g_up, down=g_down))
    drq2, drk2, drv = lin_bwd("ret_bwd", Win(S["rq2"], 128), Win(S["rk2"], 128), Win(proj, 256, COL["rv"] // 256),
                              Win(ret_decay, 128), S["ret_st"], dret_raw, False, side=side)
    landed()
    dproj, = row_call(
        "ret_pre_bwd", _ret_pre_bwd, [Row(proj, 512, COL["rq"] // 512), Row(proj, 512, COL["rk"] // 512), Row(cos), Row(sin),
                                      Row(drq2), Row(drk2), Row(drv)], [], [], tm=256, into=dproj,
        into_cols=[(512, COL["rq"]), (512, COL["rk"]), (1024, COL["rv"])])
    side, landed = _rider(hooks, "attn_bwd")
    dproj, dbias = attn_bwd(proj, *W["bias_tab"], dbr[0], dproj, side=side)
    landed()
    d_rel_bias = bias_table_grad(dbias)
    side, landed = _rider(hooks, "proj_wgrad", dict(branch=g_branch, out=g_out, up=g_up, down=g_down))
    g_in = mm("proj_wgrad", S["hb"], dproj, "tn", tm=1024, tn=1280, tk=2048, out_dtype=WIRE_DTYPE, side=side)
    landed()
    side, landed = _rider(hooks, "proj_bwd", {"in": g_in})
    dh = mm("proj_bwd", dproj, W["in"], "nt", tm=1024, tn=1024, tk=2560, extras=(dha,), epi=add, side=side)
    landed()
    big = dict(g_in=g_in, branch=g_branch, out=g_out, up=g_up, down=g_down)
    small = dict(rel_bias=d_rel_bias, wlr=dwlr[:GLA_RANK], blr=dblr, gnorm=dgnorm, ln1_g=dg1, ln1_b=db1, ln2_g=dg2, ln2_b=db2)
    return dh, big, small


def _reorder_normgate(ret_raw, gla_raw, rg, gg, gnorm):
    return f_normgate(ret_raw, rg, gla_raw, gg, gnorm)


def _gla_pre_bwd(gq, glr, dq2, dla, dgk, dgv, wlr, blr):
    _, pull = jax.vjp(f_gla_pre, gq, glr, wlr, blr)
    dgq, dglr, dwlr, dblr = pull((dq2, dla))
    dglr = jnp.concatenate([dglr, jnp.zeros((dglr.shape[0], PROJ_PAD - GLR_OFF - GLR_BLOCK), F32)], axis=1)
    return dgq, dglr, dgk, dgv, dwlr, dblr


def _ret_pre_bwd(rq, rk, cos, sin, drq2, drk2, drv):
    _, pull = jax.vjp(lambda a, b: f_ret_pre(a, b, cos, sin), rq, rk)
    drq, drk = pull((drq2, drk2))
    return drq, drk, drv


SHARDED = ("in", "branch", "out", "up", "down")

_FULL = {
    "branch": lambda g: [g[:, n].reshape(D_MODEL, D_MODEL) for n in range(3)],
    "out": lambda g: g.reshape(D_MODEL, D_MODEL),
    "up": lambda g: jnp.transpose(g, (1, 0, 2)).reshape(D_MODEL, D_FF),
    "down": lambda g: g.reshape(D_FF, D_MODEL),
}


def _owner_pieces(s):
    shard = PROJ_WIDTH // N_DEV
    lo, hi = s * shard, (s + 1) * shard
    pieces = []
    for a, b, padded in ((0, GLR_SRC, 0), (GLR_SRC, GATES_SRC, GLR_OFF), (GATES_SRC, PROJ_WIDTH, GATES_OFF)):
        x0, x1 = max(lo, a), min(hi, b)
        if x0 < x1:
            pieces.append((x0 - lo, x1 - x0, padded + x0 - a))
    return pieces


def _full_in(g):
    placed = sorted((p, s, c, w) for s in range(N_DEV) for c, w, p in _owner_pieces(s))
    cols = [g[s][:, c:c + w] for _, s, c, w in placed]
    return jnp.concatenate(cols + [jnp.zeros((D_MODEL, PROJ_PAD - PROJ_WIDTH), g.dtype)], axis=1)


def _full_wlr(g):
    wlr = jnp.transpose(g, (1, 0, 2)).reshape(GLA_RANK, LIN_HEADS * LIN_KEY_DIM)
    return jnp.pad(wlr, ((0, GLR_BLOCK - GLA_RANK), (0, 0)))


def _slab_in(g):
    return jnp.stack([jnp.concatenate([g[:, p:p + w] for _, w, p in _owner_pieces(s)], axis=1) for s in range(N_DEV)])


_SLABS = {
    "in": _slab_in,
    "branch": lambda gs: jnp.stack([b.reshape(N_DEV, D_MODEL // N_DEV, D_MODEL) for b in gs], axis=1),
    "out": lambda g: g.reshape(N_DEV, D_MODEL // N_DEV, D_MODEL),
    "up": lambda g: g,
    "down": lambda g: g.reshape(N_DEV, D_FF // N_DEV, D_MODEL),
}


def _pair_sums(name, slabs, from_sibling=None):
    sums = []
    for mine, sibs in zip(slabs, from_sibling or carrier(name, Side("pair", slabs))):
        width = mine.shape[-1]
        rows = math.prod(mine.shape[1:-1])
        tm = 256 if rows % 256 == 0 else 128
        nb = rows // tm
        my_rows = Row(mine.reshape(-1, width), rowmap=lambda i, nb=nb: (2 * (i // nb) + lax.axis_index("c")) * nb + i % nb)
        out, = row_call("pair_sum", lambda q, p: (p.astype(F32) + q.astype(F32),),
                        [Row(sibs.reshape(-1, width)), my_rows], [], [(width, WIRE_DTYPE)], tm=tm)
        sums.append(out.reshape(sibs.shape))
    return sums


def _sum_adam(name, parts, w, m, v, layer, tm, earlier):
    C = w.shape[-1]
    rows = [Row(parts, lead=(s,)) for s in range(parts.shape[0])] + [Row(a, lead=(layer,)) for a in (w, m, v)]
    return row_call(name, f_sum_adam, rows, [], [(C, F32)] * 4, tm=tm, stack=(layer, DEPTH), stacked_onto=earlier)


def kernel(x, ln_in_g, ln_in_b, w_in, rel_bias, gla_w_lr, gla_b_lr, gla_norm_g, w_branch, w_out, ln1_g, ln1_b, w_up, w_down, ln2_g, ln2_b, loss_target, m_ln_in_g, m_ln_in_b, m_w_in, m_rel_bias, m_gla_w_lr, m_gla_b_lr, m_gla_norm_g, m_w_branch, m_w_out, m_ln1_g, m_ln1_b, m_w_up, m_w_down, m_ln2_g, m_ln2_b, v_ln_in_g, v_ln_in_b, v_w_in, v_rel_bias, v_gla_w_lr, v_gla_b_lr, v_gla_norm_g, v_w_branch, v_w_out, v_ln1_g, v_ln1_b, v_w_up, v_w_down, v_ln2_g, v_ln2_b):
    T = x.shape[1]
    x2d = x.reshape(T, D_MODEL)
    tgt = loss_target.reshape(T, D_MODEL)
    me = 4 * lax.axis_index("x") + 2 * lax.axis_index("y") + lax.axis_index("c")

    inv = ROPE_BASE ** (-jnp.arange(0, LIN_KEY_DIM, 2, dtype=F32) / LIN_KEY_DIM)
    ang = jnp.arange(T, dtype=F32)[:, None] * inv[None, :]
    cos = jnp.concatenate([jnp.cos(ang), jnp.cos(ang)], axis=1)
    sin = jnp.concatenate([-jnp.sin(ang), jnp.sin(ang)], axis=1)
    log_gamma = jnp.log1p(-jnp.exp2(-5.0 - jnp.arange(LIN_HEADS, dtype=F32)))
    ret_decay = jnp.repeat(log_gamma, LIN_KEY_DIM).reshape(1, LIN_HEADS * LIN_KEY_DIM)
    consts = (cos, sin, ret_decay)

    layers = [dict(rel_bias=rel_bias[l], blr=_row2(gla_b_lr[l]), gnorm=_row2(gla_norm_g[l]), ln1_g=_row2(ln1_g[l]),
                   ln1_b=_row2(ln1_b[l]), ln2_g=_row2(ln2_g[l]), ln2_b=_row2(ln2_b[l])) for l in range(DEPTH)]
    shards = dict(branch=w_branch, out=w_out, up=w_up, down=w_down)
    shards["in"] = w_in
    wire = [{n: shards[n][l].astype(WIRE_DTYPE) for n in SHARDED} for l in range(DEPTH)]

    def gather_of(l, names):
        def landed(res):
            for n, g in zip(names, res):
                layers[l][n] = _full_in(g) if n == "in" else _FULL[n](g)
        return lambda ctx: (Side("gather", [wire[l][n] for n in names]), landed)

    first = Side("gather", [wire[0]["in"]] + [gla_w_lr[l] for l in range(DEPTH)])
    bias_tabs = bias_table(rel_bias, side=first)
    layers[0]["in"] = _full_in(first.results[0])
    for l in range(DEPTH):
        layers[l]["wlr"] = _full_wlr(first.results[1 + l])
        layers[l]["bias_tab"] = (bias_tabs, l * ATTN_HEADS)

    fwd_hooks = [{"proj": gather_of(1, ("in",)), "attn_fwd": gather_of(0, ("up", "down")),
                  "ret_fwd": gather_of(0, ("branch", "out"))},
                 {"proj": gather_of(1, ("up", "down")), "attn_fwd": gather_of(1, ("branch", "out"))}]
    h, hb = row_call("ln_in", _twice(f_ln_in), [Row(x2d)], [_row2(ln_in_g), _row2(ln_in_b)], STREAM, tm=512)
    saved = []
    for l in range(DEPTH):
        h, hb, s = _layer_forward(h, hb, layers[l], consts, fwd_hooks[l])
        saved.append(s)
    dh, loss_lanes = row_call("loss", f_loss, [Row(h), Row(tgt)], [], [(D_MODEL, F32)], [(1, 128)], tm=512)

    received = [dict() for _ in range(DEPTH)]

    def scatter_of(l, names, sums=None):
        def make(ctx):
            parts = [sums[n] for n in names] if sums is not None else _pair_sums(
                "pair_grads", [_SLABS[n](ctx[n]) for n in names])

            def landed(res):
                received[l].update(zip(names, res))
            return Side("chips", parts), landed
        return make

    small = [None] * DEPTH
    dh, big, small[1] = _layer_backward(dh, saved[1], layers[1], consts, {})
    slabs1 = [_SLABS[n](big["g_in"] if n == "in" else big[n]) for n in SHARDED]
    sums1 = {}

    def pair_exchange_1(ctx):
        def landed(res):
            sums1.update(zip(SHARDED, _pair_sums("pair_grads", slabs1, from_sibling=res)))
        return Side("pair", slabs1), landed

    rest = SHARDED[1:]
    sums0 = {}

    def pair_exchange_0(ctx):
        slabs0 = [_SLABS[n](ctx[n]) for n in rest]

        def landed(res):
            sums0.update(zip(rest, _pair_sums("pair_grads", slabs0, from_sibling=res)))
        return Side("pair", slabs0), landed

    dh, big, small[0] = _layer_backward(dh, saved[0], layers[0], consts, {
        "mlp_down_bwd": pair_exchange_1, "gla_bwd": scatter_of(1, ("in",), sums1), "ret_bwd": pair_exchange_0,
        "attn_bwd": scatter_of(1, rest, sums1), "proj_wgrad": scatter_of(0, rest, sums0), "proj_bwd": scatter_of(0, ("in",))})
    grad_x, d_in_g, d_in_b = row_call("ln_in_bwd", vjp_fn(f_ln_in, 1, 1), [Row(x2d), Row(dh)],
                                      [_row2(ln_in_g), _row2(ln_in_b)], [(D_MODEL, F32)], [(1, D_MODEL), (1, D_MODEL)], tm=256)

    small_names = ("rel_bias", "wlr", "blr", "gnorm", "ln1_g", "ln1_b", "ln2_g", "ln2_b")
    parts = [d_in_g, d_in_b] + [small[l][n] for l in range(DEPTH) for n in small_names] + [loss_lanes]
    shapes = [(D_MODEL,), (D_MODEL,)] + [s for l in range(DEPTH) for s in (
        (ATTN_HEADS, N_REL), (GLA_RANK, 512), (512,), (LIN_VAL_DIM,), (D_MODEL,), (D_MODEL,), (D_MODEL,), (D_MODEL,))] + [(128,)]
    red = _unpack(all_reduce_small(_pack(parts)), shapes)
    loss = jnp.sum(red[-1])
    g_small = {"ln_in_g": red[0], "ln_in_b": red[1]}
    for l in range(DEPTH):
        for j, n in enumerate(small_names):
            g_small[(l, n)] = red[2 + l * len(small_names) + j]
    shard = LIN_HEADS * LIN_KEY_DIM // N_DEV
    g_wlr = jnp.stack([lax.dynamic_slice_in_dim(g_small[(l, "wlr")], me * shard, shard, axis=1) for l in range(DEPTH)])
    stack = lambda n: jnp.stack([g_small[(l, n)] for l in range(DEPTH)])
    g_rep = [g_small["ln_in_g"], g_small["ln_in_b"], stack("rel_bias"), g_wlr, stack("blr"), stack("gnorm"),
             stack("ln1_g"), stack("ln1_b"), stack("ln2_g"), stack("ln2_b")]
    w_rep = [ln_in_g, ln_in_b, rel_bias, gla_w_lr, gla_b_lr, gla_norm_g, ln1_g, ln1_b, ln2_g, ln2_b]
    m_rep = [m_ln_in_g, m_ln_in_b, m_rel_bias, m_gla_w_lr, m_gla_b_lr, m_gla_norm_g, m_ln1_g, m_ln1_b, m_ln2_g, m_ln2_b]
    v_rep = [v_ln_in_g, v_ln_in_b, v_rel_bias, v_gla_w_lr, v_gla_b_lr, v_gla_norm_g, v_ln1_g, v_ln1_b, v_ln2_g, v_ln2_b]
    rep_shapes = [w.shape for w in w_rep]
    d_pack, m_pack, v_pack = row_call("adam_small", f_adam, [Row(_pack(g_rep)), Row(_pack(w_rep)), Row(_pack(m_rep)), Row(_pack(v_rep))],
                                      [], [(128, F32)] * 3, tm=256)
    d_rep, nm_rep, nv_rep = _unpack(d_pack, rep_shapes), _unpack(m_pack, rep_shapes), _unpack(v_pack, rep_shapes)

    def owned(key, w, m, v, name, tm):
        C = w.shape[-1]
        flat = lambda a: a.reshape(DEPTH, -1, C)
        outs = None
        for l in range(DEPTH):
            outs = _sum_adam(name, received[l][key].reshape(4, -1, C), flat(w), flat(m), flat(v), l, tm, outs)
        return [o.reshape(w.shape) for o in outs]

    o_in = owned("in", w_in, m_w_in, v_w_in, "adam_w_in", 256)
    o_br = owned("branch", w_branch, m_w_branch, v_w_branch, "adam_w_branch", 128)
    o_out = owned("out", w_out, m_w_out, v_w_out, "adam_w_out", 128)
    o_up = owned("up", w_up, m_w_up, v_w_up, "adam_w_up", 256)
    o_down = owned("down", w_down, m_w_down, v_w_down, "adam_w_down", 256)

    def tree(j, rep):
        return [rep[0], rep[1], o_in[j], rep[2], rep[3], rep[4], rep[5], o_br[j], o_out[j], rep[6], rep[7], o_up[j], o_down[j],
                rep[8], rep[9]]

    return (loss, grad_x.reshape(1, T, D_MODEL), *tree(0, g_rep), *tree(1, d_rep), *tree(2, nm_rep), *tree(3, nv_rep))
```

```python
import functools
import math

import jax
import jax.numpy as jnp
from jax import lax
from jax.experimental import pallas as pl
from jax.experimental.pallas import tpu as pltpu

F32 = jnp.float32
MXU_DTYPE = jnp.bfloat16
WIRE_DTYPE = jnp.bfloat16

N_DEV = 8
D_MODEL = 1024
DEPTH = 2
CHUNK = 64
ATTN_HEADS = 8
ATTN_HEAD_DIM = 128
ATTN_LEFT_CHUNKS = 8
BAND = (ATTN_LEFT_CHUNKS + 1) * CHUNK
PAD_ROWS = ATTN_LEFT_CHUNKS * CHUNK
REL_CLIP = 2 * CHUNK
N_REL = 2 * REL_CLIP + 1
REL_MIN = REL_CLIP - (CHUNK - 1)
LIN_HEADS = 4
LIN_KEY_DIM = 128
LIN_VAL_DIM = 256
GLA_RANK = 16
GLA_GATE_NORMALIZER = 16.0
ROPE_BASE = 10000.0
D_FF = 4 * D_MODEL
LN_EPS = 1e-5
NEG_INF = -1e30
ALPHA = (2 * DEPTH) ** 0.25
ADAM_LR, ADAM_B1, ADAM_B2, ADAM_EPS, ADAM_WD, ADAM_STEP = 0.001, 0.9, 0.999, 1e-08, 0.01, 10

PROJ_WIDTH = 12304
GLR_SRC = 9216
GATES_SRC = GLR_SRC + GLA_RANK
MAIN_W = 9216
GATES_W = 3 * D_MODEL
GATES_OFF = MAIN_W
GLR_OFF = MAIN_W + GATES_W
PROJ_PAD = 12800
GLR_BLOCK = 128
COL = dict(aq=0, ak=1024, av=2048, rq=3072, rk=3584, rv=4096, rg=5120, gq=6144, gk=6656, gv=7168, gg=8192)

SEG_CHUNKS = 8
SEG = SEG_CHUNKS * CHUNK
VMEM_BIG = 56 * 1024 * 1024

_NN = (((1,), (0,)), ((), ()))
_NT = (((1,), (1,)), ((), ()))
_TN = (((0,), (0,)), ((), ()))


def _mxu(a, b, dn):
    return lax.dot_general(a.astype(MXU_DTYPE), b.astype(MXU_DTYPE), dn, preferred_element_type=F32)


@jax.custom_vjp
def dot_nn(a, b):
    return _mxu(a, b, _NN)


def _dot_nn_fwd(a, b):
    return _mxu(a, b, _NN), (a, b)


def _dot_nn_bwd(res, g):
    a, b = res
    return _mxu(g, b, _NT).astype(a.dtype), _mxu(a, g, _TN).astype(b.dtype)


dot_nn.defvjp(_dot_nn_fwd, _dot_nn_bwd)


@jax.custom_vjp
def dot_nt(a, b):
    return _mxu(a, b, _NT)


def _dot_nt_fwd(a, b):
    return _mxu(a, b, _NT), (a, b)


def _dot_nt_bwd(res, g):
    a, b = res
    return _mxu(g, b, _NN).astype(a.dtype), _mxu(g, a, _TN).astype(b.dtype)


dot_nt.defvjp(_dot_nt_fwd, _dot_nt_bwd)


@jax.custom_vjp
def dot_tn(a, b):
    return _mxu(a, b, _TN)


def _dot_tn_fwd(a, b):
    return _mxu(a, b, _TN), (a, b)


def _dot_tn_bwd(res, g):
    a, b = res
    return _mxu(b, g, _NT).astype(a.dtype), _mxu(a, g, _NN).astype(b.dtype)


dot_tn.defvjp(_dot_tn_fwd, _dot_tn_bwd)


def _tri(n, lower):
    r = lax.broadcasted_iota(jnp.int32, (n, n), 0)
    c = lax.broadcasted_iota(jnp.int32, (n, n), 1)
    return (r >= c) if lower else (r <= c)


def _tri_dot(lower, x):
    tri = _tri(x.shape[0], lower).astype(jnp.bfloat16)
    hi = x.astype(jnp.bfloat16)
    rest = x - hi.astype(F32)
    mid = rest.astype(jnp.bfloat16)
    lo = (rest - mid.astype(F32)).astype(jnp.bfloat16)
    dot = lambda part: lax.dot_general(tri, part, _NN, preferred_element_type=F32)
    return dot(hi) + dot(mid) + dot(lo)


@jax.custom_vjp
def chunk_cumsum(x):
    return _tri_dot(True, x)


def _chunk_cumsum_fwd(x):
    return chunk_cumsum(x), None


def _chunk_cumsum_bwd(_, g):
    return (_tri_dot(False, g),)


chunk_cumsum.defvjp(_chunk_cumsum_fwd, _chunk_cumsum_bwd)


@jax.custom_vjp
def swap_halves(t):
    return pltpu.roll(t, t.shape[-1] // 2, t.ndim - 1)


def _swap_fwd(t):
    return swap_halves(t), None


def _swap_bwd(_, g):
    return (swap_halves(g),)


swap_halves.defvjp(_swap_fwd, _swap_bwd)


def mm(name, a, b, dims, *, tm, tn, tk, out_dtype=F32, extras=(), row_params=(), epi=None, n_out=1, out_dtypes=None,
       side=None, column_slabs=False):
    if dims == "nn":
        (M, K), N = a.shape, b.shape[1]
    elif dims == "nt":
        (M, K), N = a.shape, b.shape[0]
    else:
        (K, M), N = a.shape, b.shape[1]
    tm, tn, tk = min(tm, M), min(tn, N), min(tk, K)
    assert M % tm == 0 and N % tn == 0 and K % tk == 0, (name, M, N, K, tm, tn, tk)
    nk = K // tk
    dn = {"nn": _NN, "nt": _NT, "tn": _TN}[dims]
    a_spec = pl.BlockSpec((tk, tm), lambda i, j, k: (k, i)) if dims == "tn" else pl.BlockSpec((tm, tk), lambda i, j, k: (i, k))
    b_spec = pl.BlockSpec((tn, tk), lambda i, j, k: (j, k)) if dims == "nt" else pl.BlockSpec((tk, tn), lambda i, j, k: (k, j))
    o_spec = pl.BlockSpec((tm, tn), lambda i, j, k: (i, j))
    row_spec = pl.BlockSpec((1, tn), lambda i, j, k: (0, j))
    n_ex = len(extras) + len(row_params)
    out_dtypes = out_dtypes or (out_dtype,) * n_out

    def finish(r, ex_refs, o_refs):
        if epi is not None:
            r = epi(r, *[e[...] for e in ex_refs])
        r = r if isinstance(r, tuple) else (r,)
        for o, v in zip(o_refs, r):
            o[...] = v.astype(o.dtype)

    def body(*refs):
        a_ref, b_ref = refs[:2]
        ex_refs = refs[2:2 + n_ex]
        o_refs = refs[2 + n_ex:2 + n_ex + n_out]
        part = _mxu(a_ref[...], b_ref[...], dn)
        if nk == 1:
            finish(part, ex_refs, o_refs)
            return
        acc = refs[-1]
        k = pl.program_id(2)

        @pl.when(k == 0)
        def _():
            acc[...] = part

        @pl.when(k > 0)
        def _():
            acc[...] += part

        @pl.when(k == nk - 1)
        def _():
            finish(acc[...], ex_refs, o_refs)

    outs = _side_call(
        side, body,
        name=name,
        grid=(M // tm, N // tn, nk),
        in_specs=[a_spec, b_spec] + [o_spec] * len(extras) + [row_spec] * len(row_params),
        out_specs=[pl.BlockSpec((None, tm, tn), lambda i, j, k: (j, i, 0))] * n_out if column_slabs else [o_spec] * n_out,
        out_shape=[jax.ShapeDtypeStruct((N // tn, M, tn) if column_slabs else (M, N), dt) for dt in out_dtypes],
        scratch_shapes=[pltpu.VMEM((tm, tn), F32)] if nk > 1 else [],
        args=[a, b, *extras, *row_params],
        semantics=("parallel", "parallel", "arbitrary"),
    )
    return outs[0] if n_out == 1 else outs


class Row:
    def __init__(self, arr, width=None, col=0, lead=(), rowmap=None):
        self.arr, self.lead, self.col = arr, tuple(lead), col
        self.width = arr.shape[-1] if width is None else width
        self.rowmap = rowmap

    def spec(self, tm):
        lead, col, rowmap = self.lead, self.col, self.rowmap or (lambda i: i)
        return pl.BlockSpec((None,) * len(lead) + (tm, self.width), lambda i: (*lead, rowmap(i), col))


def row_call(name, fn, rows, params, new_outs, red_shapes=(), *, tm, into=None, into_cols=(), stack=None, stacked_onto=None):
    R = rows[0].arr.shape[len(rows[0].lead)]
    tm = min(tm, R)
    assert R % tm == 0, (name, R, tm)
    n_row, n_par, n_new, n_into, n_red = len(rows), len(params), len(new_outs), len(into_cols), len(red_shapes)
    fresh = isinstance(into, jax.ShapeDtypeStruct)

    def body(*refs):
        row_refs = refs[:n_row]
        par_refs = refs[n_row:n_row + n_par]
        pos = n_row + n_par + (1 if (into is not None and not fresh) else 0) + (n_new if stacked_onto is not None else 0)
        new_refs = refs[pos:pos + n_new]
        red_refs = refs[pos + n_new:pos + n_new + n_red]
        pos2 = pos + n_new + n_red
        into_ref = refs[pos2] if into is not None else None
        stage = refs[pos2 + 1:pos2 + 1 + n_into] if into is not None else ()
        i = pl.program_id(0)
        vals = fn(*[r[...] for r in row_refs], *[p[...] for p in par_refs])
        vals = vals if isinstance(vals, (tuple, list)) else (vals,)
        for o, v in zip(new_refs, vals[:n_new]):
            o[...] = v.astype(o.dtype)

        def window_copy(j, step):
            width, off = into_cols[j]
            rows_at = pl.ds(pl.multiple_of(step * tm, tm), tm)
            return pltpu.make_async_copy(stage[j], into_ref.at[rows_at, pl.ds(off, width)], refs[pos2 + 1 + n_into].at[j])

        for j, v in enumerate(vals[n_new:n_new + n_into]):
            @pl.when(i > 0)
            def _(j=j):
                window_copy(j, i - 1).wait()

            stage[j][...] = v.astype(stage[j].dtype)
            window_copy(j, i).start()

            @pl.when(i == R // tm - 1)
            def _(j=j):
                window_copy(j, i).wait()
        for o, v in zip(red_refs, vals[n_new + n_into:]):
            @pl.when(i == 0)
            def _(o=o, v=v):
                o[...] = v.astype(o.dtype)

            @pl.when(i > 0)
            def _(o=o, v=v):
                o[...] += v.astype(o.dtype)

    in_specs = [r.spec(tm) for r in rows] + [pl.BlockSpec(p.shape, lambda i, nd=p.ndim: (0,) * nd) for p in params]
    args = [r.arr for r in rows] + list(params)
    if stack is None:
        out_specs = [pl.BlockSpec((tm, w), lambda i: (i, 0)) for w, _ in new_outs]
        out_shape = [jax.ShapeDtypeStruct((R, w), dt) for w, dt in new_outs]
    else:
        index, depth = stack
        out_specs = [pl.BlockSpec((None, tm, w), lambda i: (index, i, 0)) for w, _ in new_outs]
        out_shape = [jax.ShapeDtypeStruct((depth, R, w), dt) for w, dt in new_outs]
    out_specs += [pl.BlockSpec(s, lambda i, nd=len(s): (0,) * nd) for s in red_shapes]
    out_shape += [jax.ShapeDtypeStruct(s, F32) for s in red_shapes]
    aliases, scratch = {}, []
    if into is not None:
        out_specs.append(pl.BlockSpec(memory_space=pl.ANY))
        out_shape.append(jax.ShapeDtypeStruct(into.shape, into.dtype))
        if not fresh:
            in_specs.append(pl.BlockSpec(memory_space=pl.ANY))
            args.append(into)
            aliases = {len(args) - 1: len(out_shape) - 1}
        scratch = [pltpu.VMEM((tm, w), into.dtype) for w, _ in into_cols] + [pltpu.SemaphoreType.DMA((n_into,))]
    if stacked_onto is not None:
        for j, prev in enumerate(stacked_onto):
            in_specs.append(pl.BlockSpec(memory_space=pl.ANY))
            args.append(prev)
            aliases[len(args) - 1] = j
    outs = pl.pallas_call(
        body,
        name=name,
        grid=(R // tm,),
        in_specs=in_specs,
        out_specs=out_specs,
        out_shape=out_shape,
        scratch_shapes=scratch,
        input_output_aliases=aliases,
        compiler_params=pltpu.CompilerParams(dimension_semantics=("arbitrary",), vmem_limit_bytes=VMEM_BIG),
    )(*args)
    return list(outs)


def vjp_fn(fn, n_row, n_out, passthrough=0):
    def bwd(*args):
        rows = [a.astype(F32) for a in args[:n_row]]
        cts = args[n_row:n_row + n_out]
        thru = args[n_row + n_out:n_row + n_out + passthrough]
        pars = args[n_row + n_out + passthrough:]
        outs, pull = jax.vjp(lambda *xs: tuple(fn(*xs)), *rows, *pars)
        grads = pull(tuple(c.astype(o.dtype) for c, o in zip(cts, outs)))
        return tuple(grads[:n_row]) + tuple(thru) + tuple(grads[n_row:])

    return bwd


def _ln(x, g, b):
    mu = jnp.mean(x, axis=-1, keepdims=True)
    xc = x - mu
    var = jnp.mean(xc * xc, axis=-1, keepdims=True)
    return xc * lax.rsqrt(var + LN_EPS) * g + b


def f_ln_in(x, g, b):
    return (_ln(x, g, b),)


def f_deepnorm(x, y, g, b):
    return (_ln(ALPHA * x + y, g, b),)


def _twice(fn):
    return lambda *args: fn(*args) * 2


STREAM = [(D_MODEL, F32), (D_MODEL, MXU_DTYPE)]


def _rope(t, cos, sin):
    parts = []
    for h in range(LIN_HEADS):
        th = t[:, h * LIN_KEY_DIM:(h + 1) * LIN_KEY_DIM]
        parts.append(th * cos + swap_halves(th) * sin)
    return jnp.concatenate(parts, axis=1)


def f_ret_pre(rq, rk, cos, sin):
    return _rope(rq, cos, sin), _rope(rk, cos, sin) * LIN_KEY_DIM ** -0.5


def _log_sigmoid(x):
    return jnp.minimum(x, 0.0) - jnp.log(1.0 + jnp.exp(-jnp.abs(x)))


def f_gla_pre(gq, glr, wlr, blr):
    gate_pre = dot_nn(glr, wlr) + blr
    return gq * LIN_KEY_DIM ** -0.5, _log_sigmoid(gate_pre) / GLA_GATE_NORMALIZER


def f_normgate(ret_raw, rg, gla_raw, gg, gnorm):
    rets, glas = [], []
    for h in range(LIN_HEADS):
        sl = slice(h * LIN_VAL_DIM, (h + 1) * LIN_VAL_DIM)
        t = ret_raw[:, sl]
        mu = jnp.mean(t, axis=-1, keepdims=True)
        tc = t - mu
        rets.append(tc * lax.rsqrt(jnp.mean(tc * tc, axis=-1, keepdims=True) + LN_EPS))
        u = gla_raw[:, sl]
        glas.append(u * lax.rsqrt(jnp.mean(u * u, axis=-1, keepdims=True) + LN_EPS) * gnorm)
    ret = jax.nn.silu(rg) * jnp.concatenate(rets, axis=1)
    gla = jax.nn.silu(gg) * jnp.concatenate(glas, axis=1)
    return ret, gla


def f_merge(p0, p1, p2, gl):
    out = jax.nn.sigmoid(gl[:, :D_MODEL]) * p0
    out = out + jax.nn.sigmoid(gl[:, D_MODEL:2 * D_MODEL]) * p1
    return (out + jax.nn.sigmoid(gl[:, 2 * D_MODEL:]) * p2,)


def f_loss(y, tgt):
    diff = y - tgt
    sq = jnp.sum(diff * diff, axis=0, keepdims=True)
    lanes = sq[:, :128]
    for j in range(1, D_MODEL // 128):
        lanes = lanes + sq[:, j * 128:(j + 1) * 128]
    return diff * (1.0 / D_MODEL), lanes * (0.5 / D_MODEL)


def _adam(g, w, m, v):
    m = ADAM_B1 * m + (1.0 - ADAM_B1) * g
    v = ADAM_B2 * v + (1.0 - ADAM_B2) * jnp.square(g)
    m_hat = m / (1.0 - ADAM_B1 ** ADAM_STEP)
    v_hat = v / (1.0 - ADAM_B2 ** ADAM_STEP)
    delta = -ADAM_LR * (m_hat / (jnp.sqrt(v_hat) + ADAM_EPS) + ADAM_WD * w)
    return delta, m, v


def f_sum_adam(*args):
    g = args[0].astype(F32)
    for p in args[1:-3]:
        g = g + p.astype(F32)
    return (g,) + _adam(g, *args[-3:])


def f_adam(g, w, m, v):
    return _adam(g, w, m, v)


GROUP = 4
GROUP_ROWS = GROUP * CHUNK
GROUP_KEYS = (GROUP + ATTN_LEFT_CHUNKS) * CHUNK
DIAG_ROWS = REL_CLIP + CHUNK
GRAD_ROWS = 200


def _blocks_by_delta():
    blocks = {}
    for qi in range(GROUP):
        for J in range(GROUP_KEYS // 128):
            blocks.setdefault(qi - 2 * J, []).append((qi, J))
    return blocks


def bias_table(rel_bias, side=None):
    n_tab = rel_bias.shape[0] * ATTN_HEADS

    def body(rb_ref, o_ref):
        h = pl.program_id(0)
        rm = lax.broadcasted_iota(jnp.int32, (CHUNK, 128), 0)
        cm = lax.broadcasted_iota(jnp.int32, (CHUNK, 128), 1)
        ahead = jnp.where(cm >= CHUNK, 1, 0)
        for delta, where in _blocks_by_delta().items():
            base = 64 * delta + PAD_ROWS
            rel = jnp.minimum(base + rm - cm, REL_CLIP) + REL_CLIP
            in_band = (ahead - delta >= 0) & (ahead - delta <= ATTN_LEFT_CHUNKS)
            lo = max(REL_MIN, min(base - 127, REL_CLIP) + REL_CLIP)
            hi = min(base + CHUNK - 1, REL_CLIP) + REL_CLIP

            def step(r, acc, rel=rel):
                return jnp.where(rel == r, rb_ref[h * N_REL + r], acc)

            blk = jnp.zeros((CHUNK, 128), F32) if hi < lo else lax.fori_loop(lo, hi + 1, step, jnp.zeros((CHUNK, 128), F32))
            blk = jnp.where(in_band, blk, NEG_INF)
            for qi, J in where:
                o_ref[pl.ds(qi * CHUNK, CHUNK), pl.ds(J * 128, 128)] = blk

    return _side_call(
        side, body,
        name="bias_table",
        grid=(n_tab,),
        in_specs=[pl.BlockSpec(memory_space=pltpu.SMEM)],
        out_specs=[pl.BlockSpec((None, GROUP_ROWS, GROUP_KEYS), lambda h: (h, 0, 0))],
        out_shape=[jax.ShapeDtypeStruct((n_tab, GROUP_ROWS, GROUP_KEYS), F32)],
        scratch_shapes=[],
        args=[rel_bias.reshape(n_tab * N_REL)],
    )[0]


def bias_table_grad(dbias):
    blocks = _blocks_by_delta()
    below_clip = range(-(CHUNK - 1), REL_CLIP)
    deltas = [d for d in sorted(blocks) if any(-127 <= pos - 64 * d - PAD_ROWS <= CHUNK - 1 for pos in below_clip)]

    def body(d_ref, o_ref, diag):
        sums = []
        for delta in deltas:
            acc = None
            for qi, J in blocks[delta]:
                blk = d_ref[pl.ds(qi * CHUNK, CHUNK), pl.ds(J * 128, 128)]
                acc = blk if acc is None else acc + blk
            sums.append(acc)
        off = lax.broadcasted_iota(jnp.int32, (CHUNK, 128), 0) - lax.broadcasted_iota(jnp.int32, (CHUNK, 128), 1)
        diag[...] = jnp.zeros_like(diag)

        def step(t, carry):
            pos = t - (CHUNK - 1)
            picked = jnp.zeros((CHUNK, 128), F32)
            for delta, s in zip(deltas, sums):
                picked = picked + jnp.where(off == pos - 64 * delta - PAD_ROWS, s, 0.0)
            diag[pl.ds(t, 1), :] = jnp.sum(picked, axis=0, keepdims=True)
            return carry

        lax.fori_loop(0, DIAG_ROWS, step, 0, unroll=8)
        d = d_ref[...]
        total = jnp.sum(jnp.sum(d, axis=1, keepdims=True), axis=0, keepdims=True)
        o_ref[pl.ds(0, DIAG_ROWS), :] = jnp.broadcast_to(jnp.sum(diag[...], axis=1, keepdims=True), (DIAG_ROWS, 128))
        o_ref[pl.ds(DIAG_ROWS, GRAD_ROWS - DIAG_ROWS), :] = jnp.broadcast_to(total, (GRAD_ROWS - DIAG_ROWS, 128))

    out = pl.pallas_call(
        body,
        name="bias_table_grad",
        grid=(ATTN_HEADS,),
        in_specs=[pl.BlockSpec((None, GROUP_ROWS, GROUP_KEYS), lambda h: (h, 0, 0))],
        out_specs=pl.BlockSpec((None, GRAD_ROWS, 128), lambda h: (h, 0, 0)),
        out_shape=jax.ShapeDtypeStruct((ATTN_HEADS, GRAD_ROWS, 128), F32),
        scratch_shapes=[pltpu.VMEM((DIAG_ROWS, 128), F32)],
        compiler_params=pltpu.CompilerParams(dimension_semantics=("arbitrary",)),
    )(dbias)
    below = out[:, :DIAG_ROWS - 1, 0]
    clipped = out[:, DIAG_ROWS, 0] - jnp.sum(below, axis=1)
    return jnp.concatenate([jnp.zeros((ATTN_HEADS, REL_MIN), F32), below, clipped[:, None]], axis=1)


def _attn_stage(q_ref, k_ref, v_ref, qs, kp, vp, T):
    qs[...] = (q_ref[...] * ATTN_HEAD_DIM ** -0.5).astype(qs.dtype)
    for dst, src in ((kp, k_ref), (vp, v_ref)):
        dst[pl.ds(0, PAD_ROWS), :] = jnp.zeros((PAD_ROWS, ATTN_HEAD_DIM), dst.dtype)
        dst[pl.ds(PAD_ROWS, T), :] = src[...].astype(dst.dtype)


def _attn_probs(g, first_groups, qs, kp, b_ref):
    r0 = g * GROUP_ROWS if first_groups else pl.multiple_of(g * GROUP_ROWS, GROUP_ROWS)
    qc = qs[pl.ds(r0, GROUP_ROWS), :]
    kb = kp[pl.ds(r0, GROUP_KEYS), :]
    s = _mxu(qc, kb, _NT) + b_ref[...]
    if first_groups:
        k_chunk = lax.shift_right_logical(lax.broadcasted_iota(jnp.int32, (GROUP_ROWS, GROUP_KEYS), 1), 6)
        s = jnp.where(g * GROUP + k_chunk >= ATTN_LEFT_CHUNKS, s, NEG_INF)
    e = jnp.exp(s - jnp.max(s, axis=-1, keepdims=True))
    return r0, qc, kb, e * (1.0 / jnp.sum(e, axis=-1, keepdims=True))


def _for_groups(T, step):
    n_first = min(ATTN_LEFT_CHUNKS // GROUP, T // GROUP_ROWS)
    for g in range(n_first):
        step(g, True)
    n_loop = T // GROUP_ROWS - n_first
    lax.fori_loop(n_first, T // GROUP_ROWS, lambda g, c: (step(g, False), c)[1], 0, unroll=2 if n_loop % 2 == 0 else 1)


def _head_spec(T, base):
    return pl.BlockSpec((T, ATTN_HEAD_DIM), lambda h: (0, base + h))


def _attn_scratch(T):
    return [pltpu.VMEM((T, ATTN_HEAD_DIM), MXU_DTYPE)] + [pltpu.VMEM((T + PAD_ROWS, ATTN_HEAD_DIM), MXU_DTYPE)] * 2


def attn_fwd(proj, bias_tab, first_tab, side=None):
    T = proj.shape[0]

    def body(q_ref, k_ref, v_ref, b_ref, o_ref, qs, kp, vp):
        _attn_stage(q_ref, k_ref, v_ref, qs, kp, vp, T)

        def step(g, first):
            r0, _, _, p = _attn_probs(g, first, qs, kp, b_ref)
            o_ref[pl.ds(r0, GROUP_ROWS), :] = _mxu(p, vp[pl.ds(r0, GROUP_KEYS), :], _NN).astype(o_ref.dtype)

        _for_groups(T, step)

    out, = _side_call(
        side, body,
        name="attn_fwd",
        grid=(ATTN_HEADS,),
        in_specs=[_head_spec(T, COL["aq"] // 128), _head_spec(T, COL["ak"] // 128), _head_spec(T, COL["av"] // 128),
                  pl.BlockSpec((None, GROUP_ROWS, GROUP_KEYS), lambda h: (first_tab + h, 0, 0))],
        out_specs=[pl.BlockSpec((T, ATTN_HEAD_DIM), lambda h: (0, h))],
        out_shape=[jax.ShapeDtypeStruct((T, ATTN_HEADS * ATTN_HEAD_DIM), MXU_DTYPE)],
        scratch_shapes=_attn_scratch(T),
        args=[proj, proj, proj, bias_tab],
    )
    return out


def attn_bwd(proj, bias_tab, first_tab, dout, dproj, side=None):
    T = proj.shape[0]

    def body(q_ref, k_ref, v_ref, b_ref, do_ref, dproj_in, dbias_ref, dproj_ref, qs, kp, vp, dkp, dvp, dq, stage):
        del dproj_in
        h = pl.program_id(0)
        _attn_stage(q_ref, k_ref, v_ref, qs, kp, vp, T)
        dkp[...] = jnp.zeros_like(dkp)
        dvp[...] = jnp.zeros_like(dvp)
        dbias_ref[...] = jnp.zeros_like(dbias_ref)

        def step(g, first):
            r0, qc, kb, p = _attn_probs(g, first, qs, kp, b_ref)
            do = do_ref[pl.ds(r0, GROUP_ROWS), :]
            dvp[pl.ds(r0, GROUP_KEYS), :] += _mxu(p, do, _TN)
            dp = _mxu(do, vp[pl.ds(r0, GROUP_KEYS), :], _NT)
            ds = p * (dp - jnp.sum(dp * p, axis=-1, keepdims=True))
            dq[pl.ds(r0, GROUP_ROWS), :] = _mxu(ds, kb, _NN) * ATTN_HEAD_DIM ** -0.5
            dkp[pl.ds(r0, GROUP_KEYS), :] += _mxu(ds, qc, _TN)
            dbias_ref[...] += ds

        _for_groups(T, step)
        col = pl.multiple_of(h * ATTN_HEAD_DIM, ATTN_HEAD_DIM)
        for grad, first_row, base in ((dq, 0, "aq"), (dkp, PAD_ROWS, "ak"), (dvp, PAD_ROWS, "av")):
            stage[...] = grad[pl.ds(first_row, T), :].astype(stage.dtype)
            pltpu.sync_copy(stage, dproj_ref.at[:, pl.ds(COL[base] + col, ATTN_HEAD_DIM)])

    dbias, dproj = _side_call(
        side, body,
        name="attn_bwd",
        grid=(ATTN_HEADS,),
        in_specs=[_head_spec(T, COL["aq"] // 128), _head_spec(T, COL["ak"] // 128), _head_spec(T, COL["av"] // 128),
                  pl.BlockSpec((None, GROUP_ROWS, GROUP_KEYS), lambda h: (first_tab + h, 0, 0)),
                  pl.BlockSpec((T, ATTN_HEAD_DIM), lambda h: (0, h)),
                  pl.BlockSpec(memory_space=pl.ANY)],
        out_specs=[pl.BlockSpec((None, GROUP_ROWS, GROUP_KEYS), lambda h: (h, 0, 0)), pl.BlockSpec(memory_space=pl.ANY)],
        out_shape=[jax.ShapeDtypeStruct((ATTN_HEADS, GROUP_ROWS, GROUP_KEYS), F32),
                   jax.ShapeDtypeStruct(dproj.shape, dproj.dtype)],
        scratch_shapes=_attn_scratch(T) + [pltpu.VMEM((T + PAD_ROWS, ATTN_HEAD_DIM), F32)] * 2
        + [pltpu.VMEM((T, ATTN_HEAD_DIM), F32), pltpu.VMEM((T, ATTN_HEAD_DIM), dproj.dtype)],
        args=[proj, proj, proj, bias_tab, dout, dproj],
        aliases={5: 1},
    )
    return dproj, dbias


def _lin_segment(q, k, v, la, st):
    lower = _tri(CHUNK, True)
    steps = (lax.broadcasted_iota(jnp.int32, (CHUNK, 1), 0) + 1).astype(F32)
    outs = []
    for i in range(SEG_CHUNKS):
        sl = slice(i * CHUNK, (i + 1) * CHUNK)
        qi, ki, vi = q[sl], k[sl], v[sl]
        if la.shape[0] == 1:
            cum, last = la * steps, la * float(CHUNK)
        else:
            cum, last = chunk_cumsum(la[sl]), jnp.sum(la[sl], axis=0, keepdims=True)
        e_pos, e_neg = jnp.exp(cum), jnp.exp(-cum)
        q_fwd = qi * e_pos
        s_fwd = dot_nt(q_fwd, ki * e_neg)
        s_bwd = dot_nt(qi * e_neg, ki * e_pos)
        o = dot_nn(jnp.where(lower, s_fwd, s_bwd), vi) + dot_nt(q_fwd, st)
        st = jnp.exp(last) * st + dot_tn(vi, ki * jnp.exp(last - cum))
        outs.append(o)
    return jnp.concatenate(outs, axis=0), st


class Win:
    def __init__(self, arr, width, base=0):
        self.arr, self.width, self.base = arr, width, base


HEADS_PER_STEP = 2


def _lin_spec(w, seg_of):
    assert w.base % HEADS_PER_STEP == 0
    first, width = w.base // HEADS_PER_STEP, HEADS_PER_STEP * w.width
    if w.arr.shape[0] == 1:
        return pl.BlockSpec((1, width), lambda p, s: (0, first + p))
    return pl.BlockSpec((SEG, width), lambda p, s: (seg_of(s), first + p))


def _head_cols(ref, j, width):
    return ref[:, j * width:(j + 1) * width]


def _decay_rows(la_ref, j):
    return _head_cols(la_ref, j, LIN_KEY_DIM)


_STATE_BLOCK = (HEADS_PER_STEP, None, LIN_VAL_DIM, LIN_KEY_DIM)


def lin_fwd(name, q, k, v, la, side=None):
    T = q.arr.shape[0]
    nseg = T // SEG

    def body(q_ref, k_ref, v_ref, la_ref, o_ref, st_ref, st):
        @pl.when(pl.program_id(1) == 0)
        def _():
            st[...] = jnp.zeros_like(st)

        for j in range(HEADS_PER_STEP):
            st0 = st[j]
            st_ref[j] = st0
            o, st1 = _lin_segment(_head_cols(q_ref, j, LIN_KEY_DIM), _head_cols(k_ref, j, LIN_KEY_DIM),
                                  _head_cols(v_ref, j, LIN_VAL_DIM), _decay_rows(la_ref, j), st0)
            o_ref[:, j * LIN_VAL_DIM:(j + 1) * LIN_VAL_DIM] = o
            st[j] = st1

    fwd = lambda s: s
    out, states = _side_call(
        side, body,
        name=name,
        grid=(LIN_HEADS // HEADS_PER_STEP, nseg),
        in_specs=[_lin_spec(w, fwd) for w in (q, k, v, la)],
        out_specs=[pl.BlockSpec((SEG, HEADS_PER_STEP * LIN_VAL_DIM), lambda p, s: (s, p)),
                   pl.BlockSpec(_STATE_BLOCK, lambda p, s: (p, s, 0, 0))],
        out_shape=[jax.ShapeDtypeStruct((T, LIN_HEADS * LIN_VAL_DIM), F32),
                   jax.ShapeDtypeStruct((LIN_HEADS, nseg, LIN_VAL_DIM, LIN_KEY_DIM), F32)],
        scratch_shapes=[pltpu.VMEM((HEADS_PER_STEP, LIN_VAL_DIM, LIN_KEY_DIM), F32)],
        args=[q.arr, k.arr, v.arr, la.arr],
    )
    return out, states


def lin_bwd(name, q, k, v, la, states, dout, with_decay_grad, side=None):
    T = q.arr.shape[0]
    nseg = T // SEG
    n_out = 4 if with_decay_grad else 3

    widths = [LIN_KEY_DIM, LIN_KEY_DIM, LIN_VAL_DIM] + ([LIN_KEY_DIM] if with_decay_grad else [])

    def body(q_ref, k_ref, v_ref, la_ref, st_ref, do_ref, *rest):
        outs, dst = rest[:n_out], rest[n_out]

        @pl.when(pl.program_id(1) == 0)
        def _():
            dst[...] = jnp.zeros_like(dst)

        for j in range(HEADS_PER_STEP):
            _, pull = jax.vjp(_lin_segment, _head_cols(q_ref, j, LIN_KEY_DIM).astype(F32),
                              _head_cols(k_ref, j, LIN_KEY_DIM).astype(F32), _head_cols(v_ref, j, LIN_VAL_DIM).astype(F32),
                              _decay_rows(la_ref, j), st_ref[j])
            grads = pull((_head_cols(do_ref, j, LIN_VAL_DIM), dst[j]))
            for o, g, w in zip(outs, grads[:n_out], widths):
                o[:, j * w:(j + 1) * w] = g
            dst[j] = grads[4]

    rev = lambda s: nseg - 1 - s
    key_spec = pl.BlockSpec((SEG, HEADS_PER_STEP * LIN_KEY_DIM), lambda p, s: (rev(s), p))
    val_spec = pl.BlockSpec((SEG, HEADS_PER_STEP * LIN_VAL_DIM), lambda p, s: (rev(s), p))
    out_specs = [key_spec, key_spec, val_spec] + ([key_spec] if with_decay_grad else [])
    return _side_call(
        side, body,
        name=name,
        grid=(LIN_HEADS // HEADS_PER_STEP, nseg),
        in_specs=[_lin_spec(w, rev) for w in (q, k, v, la)] + [
            pl.BlockSpec(_STATE_BLOCK, lambda p, s: (p, rev(s), 0, 0)), val_spec],
        out_specs=out_specs,
        out_shape=[jax.ShapeDtypeStruct((T, LIN_HEADS * w), F32) for w in widths],
        scratch_shapes=[pltpu.VMEM((HEADS_PER_STEP, LIN_VAL_DIM, LIN_KEY_DIM), F32)],
        args=[q.arr, k.arr, v.arr, la.arr, states, dout],
    )


def _coords():
    return lax.axis_index("x"), lax.axis_index("y"), lax.axis_index("c")


def _peer(k):
    x, y, c = _coords()
    px = 1 - x if k & 4 else x
    py = 1 - y if k & 2 else y
    pc = 1 - c if k & 1 else c
    return (px, py, pc), 4 * px + 2 * py + pc


def _remote(src, dst, send_sem, recv_sem, device):
    return pltpu.make_async_remote_copy(src_ref=src, dst_ref=dst, send_sem=send_sem, recv_sem=recv_sem,
                                        device_id=device, device_id_type=pl.DeviceIdType.MESH)


class Side:
    def __init__(self, kind, arrays):
        self.kind, self.arrays, self.n = kind, list(arrays), len(arrays)
        self.outs_per = 1
        self.slots = {"gather": N_DEV - 1, "pair": 4, "chips": 3}[kind]

    def out_shape(self):
        if self.kind == "gather":
            return [jax.ShapeDtypeStruct((N_DEV,) + a.shape, a.dtype) for a in self.arrays]
        return [jax.ShapeDtypeStruct((4,) + a.shape[1:], a.dtype) for a in self.arrays]

    def sems(self):
        return [pltpu.SemaphoreType.DMA((self.n, self.slots)), pltpu.SemaphoreType.DMA((self.n, self.slots)),
                pltpu.SemaphoreType.DMA((self.n,))]

    def start(self, ins, outs, sems):
        send, recv, own = sems
        x, y, c = _coords()
        me = 4 * x + 2 * y + c
        for a in range(self.n):
            if self.kind == "gather":
                pltpu.make_async_copy(ins[a], outs[a].at[me], own.at[a]).start()
                for k in (1, 2, 4, 6):
                    peer, _ = _peer(k)
                    _remote(ins[a], outs[a].at[me], send.at[a, k - 1], recv.at[a, k - 1], peer).start()
            elif self.kind == "pair":
                for j in range(4):
                    _remote(ins[a].at[2 * j + 1 - c], outs[a].at[j], send.at[a, j], recv.at[a, j], (x, y, 1 - c)).start()
            else:
                chip = 2 * x + y
                pltpu.make_async_copy(ins[a].at[chip], outs[a].at[chip], own.at[a]).start()
                for d in (1, 2, 3):
                    tx, ty = (1 - x if d & 2 else x), (1 - y if d & 1 else y)
                    _remote(ins[a].at[2 * tx + ty], outs[a].at[chip], send.at[a, d - 1], recv.at[a, d - 1], (tx, ty, c)).start()

    def finish(self, ins, outs, sems):
        send, recv, own = sems
        x, y, c = _coords()
        me = 4 * x + 2 * y + c
        sibling = (x, y, 1 - c)
        for a in range(self.n):
            if self.kind == "gather":
                blk = outs[a].at[me]
                for k in (2, 4, 6):
                    _, pidx = _peer(k)
                    _remote(blk, blk, send.at[a, k - 1], recv.at[a, k - 1], sibling).wait_recv()
                    _remote(outs[a].at[pidx], outs[a].at[pidx], send.at[a, k], recv.at[a, k], sibling).start()
                pltpu.make_async_copy(ins[a], blk, own.at[a]).wait()
                for k in range(1, N_DEV):
                    _remote(blk, blk, send.at[a, k - 1], recv.at[a, k - 1], sibling).wait_send()
                for k in (1, 3, 5, 7):
                    _remote(blk, blk, send.at[a, k - 1], recv.at[a, k - 1], sibling).wait_recv()
            elif self.kind == "pair":
                for j in range(4):
                    _remote(ins[a].at[j], outs[a].at[j], send.at[a, j], recv.at[a, j], sibling).wait()
            else:
                slab = outs[a].at[0]
                pltpu.make_async_copy(slab, slab, own.at[a]).wait()
                for d in (1, 2, 3):
                    _remote(slab, slab, send.at[a, d - 1], recv.at[a, d - 1], sibling).wait()


def _side_call(side, body, *, name, grid, in_specs, out_specs, out_shape, scratch_shapes, args, semantics=None,
               aliases=None):
    in_specs, out_specs, out_shape = list(in_specs), list(out_specs), list(out_shape)
    scratch_shapes, args = list(scratch_shapes), list(args)
    n_in, n_out = len(in_specs), len(out_specs)
    kernel_body = body
    if side is not None:
        n_so = side.n * side.outs_per

        def kernel_body(*refs):
            ins, side_ins = refs[:n_in], refs[n_in:n_in + side.n]
            pos = n_in + side.n
            outs, side_outs = refs[pos:pos + n_out], refs[pos + n_out:pos + n_out + n_so]
            rest = refs[pos + n_out + n_so:]
            scratch, sems = rest[:-3], rest[-3:]
            ids = [pl.program_id(ax) for ax in range(len(grid))]
            first = functools.reduce(jnp.logical_and, [i == 0 for i in ids])
            last = functools.reduce(jnp.logical_and, [i == g - 1 for i, g in zip(ids, grid)])

            @pl.when(first)
            def _():
                side.start(side_ins, side_outs, sems)

            body(*ins, *outs, *scratch)

            @pl.when(last)
            def _():
                side.finish(side_ins, side_outs, sems)

        any_spec = pl.BlockSpec(memory_space=pl.ANY)
        in_specs += [any_spec] * side.n
        args += side.arrays
        out_specs += [any_spec] * n_so
        out_shape += side.out_shape()
        scratch_shapes += side.sems()
        semantics = ("arbitrary",) * len(grid)
    outs = pl.pallas_call(
        kernel_body,
        name=name,
        grid=grid,
        in_specs=in_specs,
        out_specs=out_specs,
        out_shape=out_shape,
        scratch_shapes=scratch_shapes,
        input_output_aliases=aliases or {},
        compiler_params=pltpu.CompilerParams(dimension_semantics=semantics or ("arbitrary",) * len(grid),
                                             vmem_limit_bytes=VMEM_BIG),
    )(*args)
    if side is not None:
        side.results = list(outs[n_out:])
    return list(outs[:n_out])


def carrier(name, side):
    _side_call(side, lambda: None, name=name, grid=(1,), in_specs=[], out_specs=[], out_shape=[], scratch_shapes=[], args=[])
    return side.results


def all_reduce_small(vec):
    def body(v_ref, o_ref, land, send_sems, recv_sems):
        x, y, c = _coords()
        me = 4 * x + 2 * y + c
        land[me] = v_ref[...]
        copies = []
        for k in range(1, N_DEV):
            peer, _ = _peer(k)
            cp = pltpu.make_async_remote_copy(
                src_ref=v_ref, dst_ref=land.at[me], send_sem=send_sems.at[k - 1], recv_sem=recv_sems.at[k - 1],
                device_id=peer, device_id_type=pl.DeviceIdType.MESH)
            cp.start()
            copies.append(cp)
        for cp in copies:
            cp.wait()
        total = land[0]
        for s in range(1, N_DEV):
            total = total + land[s]
        o_ref[...] = total

    return pl.pallas_call(
        body,
        name="all_reduce_small",
        in_specs=[pl.BlockSpec(memory_space=pltpu.VMEM)],
        out_specs=pl.BlockSpec(memory_space=pltpu.VMEM),
        out_shape=jax.ShapeDtypeStruct(vec.shape, F32),
        scratch_shapes=[pltpu.VMEM((N_DEV,) + vec.shape, F32), pltpu.SemaphoreType.DMA((N_DEV - 1,)),
                        pltpu.SemaphoreType.DMA((N_DEV - 1,))],
    )(vec)


def _pack(parts):
    flat = jnp.concatenate([p.reshape(-1).astype(F32) for p in parts])
    n = flat.shape[0]
    rows = -(-n // 1024) * 8
    return jnp.pad(flat, (0, rows * 128 - n)).reshape(rows, 128)


def _unpack(packed, shapes):
    flat, out, pos = packed.reshape(-1), [], 0
    for s in shapes:
        n = math.prod(s)
        out.append(flat[pos:pos + n].reshape(s))
        pos += n
    return out


def _row2(v):
    return v.reshape(1, -1)


def _rider(hooks, name, ctx=None):
    if name not in hooks:
        return None, lambda: None
    side, done = hooks[name](ctx)
    return side, lambda: done(side.results)


def _layer_forward(h, hb, W, consts, hooks):
    T = h.shape[0]
    cos, sin, ret_decay = consts
    side, landed = _rider(hooks, "proj")
    proj = mm("proj", hb, W["in"], "nn", tm=4096, tn=512, tk=1024, side=side)
    landed()
    bias_tab, first_tab = W["bias_tab"]
    side, landed = _rider(hooks, "attn_fwd")
    attn = attn_fwd(proj, bias_tab, first_tab, side=side)
    landed()
    rq2, rk2 = row_call("ret_pre", f_ret_pre, [Row(proj, 512, COL["rq"] // 512), Row(proj, 512, COL["rk"] // 512),
                                              Row(cos), Row(sin)], [], [(512, F32), (512, F32)], tm=512)
    side, landed = _rider(hooks, "ret_fwd")
    ret_raw, ret_st = lin_fwd("ret_fwd", Win(rq2, 128), Win(rk2, 128), Win(proj, 256, COL["rv"] // 256), Win(ret_decay, 128),
                              side=side)
    landed()
    gq2, gla_la = row_call("gla_pre", f_gla_pre, [Row(proj, 512, COL["gq"] // 512), Row(proj, GLR_BLOCK, GLR_OFF // GLR_BLOCK)],
                           [W["wlr"], W["blr"]], [(512, F32), (512, F32)], tm=512)
    side, landed = _rider(hooks, "gla_fwd")
    gla_raw, gla_st = lin_fwd("gla_fwd", Win(gq2, 128), Win(proj, 128, COL["gk"] // 128), Win(proj, 256, COL["gv"] // 256),
                              Win(gla_la, 128), side=side)
    landed()
    ret_b, gla_b = row_call("normgate", f_normgate, [Row(ret_raw), Row(proj, 1024, COL["rg"] // 1024), Row(gla_raw),
                                                    Row(proj, 1024, COL["gg"] // 1024)], [W["gnorm"]],
                            [(1024, MXU_DTYPE), (1024, MXU_DTYPE)], tm=512)
    branches = (attn, ret_b, gla_b)
    p = [mm("branch_proj", branches[n], W["branch"][n], "nn", tm=1024, tn=1024, tk=1024) for n in range(3)]
    merged, = row_call("merge", f_merge, [Row(p[0]), Row(p[1]), Row(p[2]), Row(proj, GATES_W, GATES_OFF // GATES_W)], [],
                       [(1024, MXU_DTYPE)], tm=512)
    post_norm = lambda y, x, g, b: f_deepnorm(x, y, g, b) * 2 + (y,)
    x1, x1b, mix = mm("out_proj_ln1", merged, W["out"], "nn", tm=512, tn=D_MODEL, tk=1024, extras=(h,),
                      row_params=(W["ln1_g"], W["ln1_b"]), epi=post_norm, n_out=3, out_dtypes=(F32, MXU_DTYPE, F32))
    side, landed = _rider(hooks, "mlp_up")
    u, act = mm("mlp_up", x1b, W["up"], "nn", tm=2048, tn=512, tk=1024, n_out=2, out_dtypes=(F32, MXU_DTYPE),
                epi=lambda r: (r, jnp.square(jnp.maximum(r, 0.0))), side=side)
    landed()
    x2, x2b, y = mm("mlp_down_ln2", act, W["down"], "nn", tm=1024, tn=D_MODEL, tk=1024, extras=(x1,),
                    row_params=(W["ln2_g"], W["ln2_b"]), epi=post_norm, n_out=3, out_dtypes=(F32, MXU_DTYPE, F32))
    saved = dict(h=h, hb=hb, proj=proj, rq2=rq2, rk2=rk2, ret_st=ret_st, gq2=gq2, gla_la=gla_la,
                 gla_st=gla_st, ret_raw=ret_raw, gla_raw=gla_raw, branches=branches, p=p, merged=merged, mix=mix, x1=x1, x1b=x1b,
                 u=u, act=act, y=y)
    return x2, x2b, saved


def _layer_backward(dx2, S, W, consts, hooks):
    cos, sin, ret_decay = consts
    h, proj, T = S["h"], S["proj"], S["h"].shape[0]
    add = lambda r, e: r + e
    dx1a, dy, dg2, db2 = row_call("ln2_bwd", vjp_fn(f_deepnorm, 2, 1), [Row(S["x1"]), Row(S["y"]), Row(dx2)],
                                  [W["ln2_g"], W["ln2_b"]], [(1024, F32), (1024, MXU_DTYPE)], [(1, 1024), (1, 1024)], tm=512)
    side, landed = _rider(hooks, "mlp_down_bwd")
    du = mm("mlp_down_bwd", dy, W["down"], "nt", tm=1024, tn=512, tk=1024, extras=(S["u"],),
            epi=lambda r, u: r * (2.0 * jnp.maximum(u, 0.0)), out_dtype=MXU_DTYPE, side=side)
    landed()
    g_down = mm("mlp_down_wgrad", S["act"], dy, "tn", tm=1024, tn=1024, tk=2048, out_dtype=WIRE_DTYPE)
    dx1 = mm("mlp_up_bwd", du, W["up"], "nt", tm=1024, tn=1024, tk=2048, extras=(dx1a,), epi=add)
    g_up = mm("mlp_up_wgrad", S["x1b"], du, "tn", tm=1024, tn=D_FF // N_DEV, tk=2048, out_dtype=WIRE_DTYPE,
              column_slabs=True)
    dha, dmix, dg1, db1 = row_call("ln1_bwd", vjp_fn(f_deepnorm, 2, 1), [Row(h), Row(S["mix"]), Row(dx1)],
                                   [W["ln1_g"], W["ln1_b"]], [(1024, F32), (1024, MXU_DTYPE)], [(1, 1024), (1, 1024)], tm=512)
    dmerged = mm("out_proj_bwd", dmix, W["out"], "nt", tm=1024, tn=1024, tk=1024)
    g_out = mm("out_proj_wgrad", S["merged"], dmix, "tn", tm=1024, tn=1024, tk=2048, out_dtype=WIRE_DTYPE)
    dproj = jax.ShapeDtypeStruct((T, PROJ_PAD), MXU_DTYPE)
    p = S["p"]
    dp0, dp1, dp2, dproj = row_call(
        "merge_bwd", vjp_fn(f_merge, 4, 1), [Row(p[0]), Row(p[1]), Row(p[2]), Row(proj, GATES_W, GATES_OFF // GATES_W), Row(dmerged)],
        [], [(1024, MXU_DTYPE)] * 3, tm=256, into=dproj, into_cols=[(GATES_W, GATES_OFF)])
    dps = (dp0, dp1, dp2)
    dbr = [mm("branch_proj_bwd", dps[n], W["branch"][n], "nt", tm=1024, tn=1024, tk=1024,
              out_dtype=MXU_DTYPE if n == 0 else F32) for n in range(3)]
    g_branch = [mm("branch_proj_wgrad", S["branches"][n], dps[n], "tn", tm=1024, tn=1024, tk=2048, out_dtype=WIRE_DTYPE)
                for n in range(3)]
    dret_raw, dgla_raw, dgnorm, dproj = row_call(
        "normgate_bwd", vjp_fn(lambda a, b, c, d, g: _reorder_normgate(a, b, c, d, g), 4, 2),
        [Row(S["ret_raw"]), Row(S["gla_raw"]), Row(proj, 1024, COL["rg"] // 1024), Row(proj, 1024, COL["gg"] // 1024),
         Row(dbr[1]), Row(dbr[2])], [W["gnorm"]], [(1024, F32), (1024, F32)], [(1, LIN_VAL_DIM)], tm=256,
        into=dproj, into_cols=[(1024, COL["rg"]), (1024, COL["gg"])])
    side, landed = _rider(hooks, "gla_bwd")
    dgq2, dgk, dgv, dgla_la = lin_bwd("gla_bwd", Win(S["gq2"], 128), Win(proj, 128, COL["gk"] // 128),
                                      Win(proj, 256, COL["gv"] // 256), Win(S["gla_la"], 128), S["gla_st"], dgla_raw, True,
                                      side=side)
    landed()
    dwlr, dblr, dproj = row_call(
        "gla_pre_bwd", _gla_pre_bwd, [Row(proj, 512, COL["gq"] // 512), Row(proj, GLR_BLOCK, GLR_OFF // GLR_BLOCK),
                                      Row(dgq2), Row(dgla_la), Row(dgk), Row(dgv)], [W["wlr"], W["blr"]], [],
        [(GLR_BLOCK, 512), (1, 512)], tm=512, into=dproj,
        into_cols=[(512, COL["gq"]), (PROJ_PAD - GLR_OFF, GLR_OFF), (512, COL["gk"]), (1024, COL["gv"])])
    side, landed = _rider(hooks, "ret_bwd", dict(branch=g_branch, out=g_out, up=g_up, down=g_down))
    drq2, drk2, drv = lin_bwd("ret_bwd", Win(S["rq2"], 128), Win(S["rk2"], 128), Win(proj, 256, COL["rv"] // 256),
                              Win(ret_decay, 128), S["ret_st"], dret_raw, False, side=side)
    landed()
    dproj, = row_call(
        "ret_pre_bwd", _ret_pre_bwd, [Row(proj, 512, COL["rq"] // 512), Row(proj, 512, COL["rk"] // 512), Row(cos), Row(sin),
                                      Row(drq2), Row(drk2), Row(drv)], [], [], tm=512, into=dproj,
        into_cols=[(512, COL["rq"]), (512, COL["rk"]), (1024, COL["rv"])])
    side, landed = _rider(hooks, "attn_bwd")
    dproj, dbias = attn_bwd(proj, *W["bias_tab"], dbr[0], dproj, side=side)
    landed()
    d_rel_bias = bias_table_grad(dbias)
    side, landed = _rider(hooks, "proj_wgrad", dict(branch=g_branch, out=g_out, up=g_up, down=g_down))
    g_in = mm("proj_wgrad", S["hb"], dproj, "tn", tm=1024, tn=1280, tk=2048, out_dtype=WIRE_DTYPE, side=side)
    landed()
    side, landed = _rider(hooks, "proj_bwd", {"in": g_in})
    dh = mm("proj_bwd", dproj, W["in"], "nt", tm=1024, tn=1024, tk=2560, extras=(dha,), epi=add, side=side)
    landed()
    big = dict(g_in=g_in, branch=g_branch, out=g_out, up=g_up, down=g_down)
    small = dict(rel_bias=d_rel_bias, wlr=dwlr[:GLA_RANK], blr=dblr, gnorm=dgnorm, ln1_g=dg1, ln1_b=db1, ln2_g=dg2, ln2_b=db2)
    return dh, big, small


def _reorder_normgate(ret_raw, gla_raw, rg, gg, gnorm):
    return f_normgate(ret_raw, rg, gla_raw, gg, gnorm)


def _gla_pre_bwd(gq, glr, dq2, dla, dgk, dgv, wlr, blr):
    _, pull = jax.vjp(f_gla_pre, gq, glr, wlr, blr)
    dgq, dglr, dwlr, dblr = pull((dq2, dla))
    dglr = jnp.concatenate([dglr, jnp.zeros((dglr.shape[0], PROJ_PAD - GLR_OFF - GLR_BLOCK), F32)], axis=1)
    return dgq, dglr, dgk, dgv, dwlr, dblr


def _ret_pre_bwd(rq, rk, cos, sin, drq2, drk2, drv):
    _, pull = jax.vjp(lambda a, b: f_ret_pre(a, b, cos, sin), rq, rk)
    drq, drk = pull((drq2, drk2))
    return drq, drk, drv


SHARDED = ("in", "branch", "out", "up", "down")

_FULL = {
    "branch": lambda g: [g[:, n].reshape(D_MODEL, D_MODEL) for n in range(3)],
    "out": lambda g: g.reshape(D_MODEL, D_MODEL),
    "up": lambda g: jnp.transpose(g, (1, 0, 2)).reshape(D_MODEL, D_FF),
    "down": lambda g: g.reshape(D_FF, D_MODEL),
}


def _owner_pieces(s):
    shard = PROJ_WIDTH // N_DEV
    lo, hi = s * shard, (s + 1) * shard
    pieces = []
    for a, b, padded in ((0, GLR_SRC, 0), (GLR_SRC, GATES_SRC, GLR_OFF), (GATES_SRC, PROJ_WIDTH, GATES_OFF)):
        x0, x1 = max(lo, a), min(hi, b)
        if x0 < x1:
            pieces.append((x0 - lo, x1 - x0, padded + x0 - a))
    return pieces


def _full_in(g):
    placed = sorted((p, s, c, w) for s in range(N_DEV) for c, w, p in _owner_pieces(s))
    cols = [g[s][:, c:c + w] for _, s, c, w in placed]
    return jnp.concatenate(cols + [jnp.zeros((D_MODEL, PROJ_PAD - PROJ_WIDTH), g.dtype)], axis=1)


def _full_wlr(g):
    wlr = jnp.transpose(g, (1, 0, 2)).reshape(GLA_RANK, LIN_HEADS * LIN_KEY_DIM)
    return jnp.pad(wlr, ((0, GLR_BLOCK - GLA_RANK), (0, 0)))


def _slab_in(g):
    return jnp.stack([jnp.concatenate([g[:, p:p + w] for _, w, p in _owner_pieces(s)], axis=1) for s in range(N_DEV)])


_SLABS = {
    "in": _slab_in,
    "branch": lambda gs: jnp.stack([b.reshape(N_DEV, D_MODEL // N_DEV, D_MODEL) for b in gs], axis=1),
    "out": lambda g: g.reshape(N_DEV, D_MODEL // N_DEV, D_MODEL),
    "up": lambda g: g,
    "down": lambda g: g.reshape(N_DEV, D_FF // N_DEV, D_MODEL),
}


def _pair_sums(name, slabs, from_sibling=None):
    sums = []
    for mine, sibs in zip(slabs, from_sibling or carrier(name, Side("pair", slabs))):
        width = mine.shape[-1]
        rows = math.prod(mine.shape[1:-1])
        tm = 256 if rows % 256 == 0 else 128
        nb = rows // tm
        my_rows = Row(mine.reshape(-1, width), rowmap=lambda i, nb=nb: (2 * (i // nb) + lax.axis_index("c")) * nb + i % nb)
        out, = row_call("pair_sum", lambda q, p: (p.astype(F32) + q.astype(F32),),
                        [Row(sibs.reshape(-1, width)), my_rows], [], [(width, WIRE_DTYPE)], tm=tm)
        sums.append(out.reshape(sibs.shape))
    return sums


def _sum_adam(name, parts, w, m, v, layer, tm, earlier):
    C = w.shape[-1]
    rows = [Row(parts, lead=(s,)) for s in range(parts.shape[0])] + [Row(a, lead=(layer,)) for a in (w, m, v)]
    return row_call(name, f_sum_adam, rows, [], [(C, F32)] * 4, tm=tm, stack=(layer, DEPTH), stacked_onto=earlier)


def kernel(x, ln_in_g, ln_in_b, w_in, rel_bias, gla_w_lr, gla_b_lr, gla_norm_g, w_branch, w_out, ln1_g, ln1_b, w_up, w_down, ln2_g, ln2_b, loss_target, m_ln_in_g, m_ln_in_b, m_w_in, m_rel_bias, m_gla_w_lr, m_gla_b_lr, m_gla_norm_g, m_w_branch, m_w_out, m_ln1_g, m_ln1_b, m_w_up, m_w_down, m_ln2_g, m_ln2_b, v_ln_in_g, v_ln_in_b, v_w_in, v_rel_bias, v_gla_w_lr, v_gla_b_lr, v_gla_norm_g, v_w_branch, v_w_out, v_ln1_g, v_ln1_b, v_w_up, v_w_down, v_ln2_g, v_ln2_b):
    T = x.shape[1]
    x2d = x.reshape(T, D_MODEL)
    tgt = loss_target.reshape(T, D_MODEL)
    me = 4 * lax.axis_index("x") + 2 * lax.axis_index("y") + lax.axis_index("c")

    inv = ROPE_BASE ** (-jnp.arange(0, LIN_KEY_DIM, 2, dtype=F32) / LIN_KEY_DIM)
    ang = jnp.arange(T, dtype=F32)[:, None] * inv[None, :]
    cos = jnp.concatenate([jnp.cos(ang), jnp.cos(ang)], axis=1)
    sin = jnp.concatenate([-jnp.sin(ang), jnp.sin(ang)], axis=1)
    log_gamma = jnp.log1p(-jnp.exp2(-5.0 - jnp.arange(LIN_HEADS, dtype=F32)))
    ret_decay = jnp.repeat(log_gamma, LIN_KEY_DIM).reshape(1, LIN_HEADS * LIN_KEY_DIM)
    consts = (cos, sin, ret_decay)

    layers = [dict(rel_bias=rel_bias[l], blr=_row2(gla_b_lr[l]), gnorm=_row2(gla_norm_g[l]), ln1_g=_row2(ln1_g[l]),
                   ln1_b=_row2(ln1_b[l]), ln2_g=_row2(ln2_g[l]), ln2_b=_row2(ln2_b[l])) for l in range(DEPTH)]
    shards = dict(branch=w_branch, out=w_out, up=w_up, down=w_down)
    shards["in"] = w_in
    wire = [{n: shards[n][l].astype(WIRE_DTYPE) for n in SHARDED} for l in range(DEPTH)]

    def gather_of(l, names):
        def landed(res):
            for n, g in zip(names, res):
                layers[l][n] = _full_in(g) if n == "in" else _FULL[n](g)
        return lambda ctx: (Side("gather", [wire[l][n] for n in names]), landed)

    first = Side("gather", [wire[0]["in"]] + [gla_w_lr[l] for l in range(DEPTH)])
    bias_tabs = bias_table(rel_bias, side=first)
    layers[0]["in"] = _full_in(first.results[0])
    for l in range(DEPTH):
        layers[l]["wlr"] = _full_wlr(first.results[1 + l])
        layers[l]["bias_tab"] = (bias_tabs, l * ATTN_HEADS)

    fwd_hooks = [{"proj": gather_of(1, ("in",)), "attn_fwd": gather_of(0, ("up", "down")),
                  "ret_fwd": gather_of(0, ("branch", "out"))},
                 {"proj": gather_of(1, ("up", "down")), "attn_fwd": gather_of(1, ("branch", "out"))}]
    h, hb = row_call("ln_in", _twice(f_ln_in), [Row(x2d)], [_row2(ln_in_g), _row2(ln_in_b)], STREAM, tm=512)
    saved = []
    for l in range(DEPTH):
        h, hb, s = _layer_forward(h, hb, layers[l], consts, fwd_hooks[l])
        saved.append(s)
    dh, loss_lanes = row_call("loss", f_loss, [Row(h), Row(tgt)], [], [(D_MODEL, F32)], [(1, 128)], tm=512)

    received = [dict() for _ in range(DEPTH)]

    def scatter_of(l, names, sums=None):
        def make(ctx):
            parts = [sums[n] for n in names] if sums is not None else _pair_sums(
                "pair_grads", [_SLABS[n](ctx[n]) for n in names])

            def landed(res):
                received[l].update(zip(names, res))
            return Side("chips", parts), landed
        return make

    small = [None] * DEPTH
    dh, big, small[1] = _layer_backward(dh, saved[1], layers[1], consts, {})
    slabs1 = [_SLABS[n](big["g_in"] if n == "in" else big[n]) for n in SHARDED]
    sums1 = {}

    def pair_exchange_1(ctx):
        def landed(res):
            sums1.update(zip(SHARDED, _pair_sums("pair_grads", slabs1, from_sibling=res)))
        return Side("pair", slabs1), landed

    rest = SHARDED[1:]
    sums0 = {}

    def pair_exchange_0(ctx):
        slabs0 = [_SLABS[n](ctx[n]) for n in rest]

        def landed(res):
            sums0.update(zip(rest, _pair_sums("pair_grads", slabs0, from_sibling=res)))
        return Side("pair", slabs0), landed

    dh, big, small[0] = _layer_backward(dh, saved[0], layers[0], consts, {
        "mlp_down_bwd": pair_exchange_1, "gla_bwd": scatter_of(1, ("in",), sums1), "ret_bwd": pair_exchange_0,
        "attn_bwd": scatter_of(1, rest, sums1), "proj_wgrad": scatter_of(0, rest, sums0), "proj_bwd": scatter_of(0, ("in",))})
    grad_x, d_in_g, d_in_b = row_call("ln_in_bwd", vjp_fn(f_ln_in, 1, 1), [Row(x2d), Row(dh)],
                                      [_row2(ln_in_g), _row2(ln_in_b)], [(D_MODEL, F32)], [(1, D_MODEL), (1, D_MODEL)], tm=256)

    small_names = ("rel_bias", "wlr", "blr", "gnorm", "ln1_g", "ln1_b", "ln2_g", "ln2_b")
    parts = [d_in_g, d_in_b] + [small[l][n] for l in range(DEPTH) for n in small_names] + [loss_lanes]
    shapes = [(D_MODEL,), (D_MODEL,)] + [s for l in range(DEPTH) for s in (
        (ATTN_HEADS, N_REL), (GLA_RANK, 512), (512,), (LIN_VAL_DIM,), (D_MODEL,), (D_MODEL,), (D_MODEL,), (D_MODEL,))] + [(128,)]
    red = _unpack(all_reduce_small(_pack(parts)), shapes)
    loss = jnp.sum(red[-1])
    g_small = {"ln_in_g": red[0], "ln_in_b": red[1]}
    for l in range(DEPTH):
        for j, n in enumerate(small_names):
            g_small[(l, n)] = red[2 + l * len(small_names) + j]
    shard = LIN_HEADS * LIN_KEY_DIM // N_DEV
    g_wlr = jnp.stack([lax.dynamic_slice_in_dim(g_small[(l, "wlr")], me * shard, shard, axis=1) for l in range(DEPTH)])
    stack = lambda n: jnp.stack([g_small[(l, n)] for l in range(DEPTH)])
    g_rep = [g_small["ln_in_g"], g_small["ln_in_b"], stack("rel_bias"), g_wlr, stack("blr"), stack("gnorm"),
             stack("ln1_g"), stack("ln1_b"), stack("ln2_g"), stack("ln2_b")]
    w_rep = [ln_in_g, ln_in_b, rel_bias, gla_w_lr, gla_b_lr, gla_norm_g, ln1_g, ln1_b, ln2_g, ln2_b]
    m_rep = [m_ln_in_g, m_ln_in_b, m_rel_bias, m_gla_w_lr, m_gla_b_lr, m_gla_norm_g, m_ln1_g, m_ln1_b, m_ln2_g, m_ln2_b]
    v_rep = [v_ln_in_g, v_ln_in_b, v_rel_bias, v_gla_w_lr, v_gla_b_lr, v_gla_norm_g, v_ln1_g, v_ln1_b, v_ln2_g, v_ln2_b]
    rep_shapes = [w.shape for w in w_rep]
    d_pack, m_pack, v_pack = row_call("adam_small", f_adam, [Row(_pack(g_rep)), Row(_pack(w_rep)), Row(_pack(m_rep)), Row(_pack(v_rep))],
                                      [], [(128, F32)] * 3, tm=256)
    d_rep, nm_rep, nv_rep = _unpack(d_pack, rep_shapes), _unpack(m_pack, rep_shapes), _unpack(v_pack, rep_shapes)

    def owned(key, w, m, v, name, tm):
        C = w.shape[-1]
        flat = lambda a: a.reshape(DEPTH, -1, C)
        outs = None
        for l in range(DEPTH):
            outs = _sum_adam(name, received[l][key].reshape(4, -1, C), flat(w), flat(m), flat(v), l, tm, outs)
        return [o.reshape(w.shape) for o in outs]

    o_in = owned("in", w_in, m_w_in, v_w_in, "adam_w_in", 256)
    o_br = owned("branch", w_branch, m_w_branch, v_w_branch, "adam_w_branch", 128)
    o_out = owned("out", w_out, m_w_out, v_w_out, "adam_w_out", 128)
    o_up = owned("up", w_up, m_w_up, v_w_up, "adam_w_up", 256)
    o_down = owned("down", w_down, m_w_down, v_w_down, "adam_w_down", 256)

    def tree(j, rep):
        return [rep[0], rep[1], o_in[j], rep[2], rep[3], rep[4], rep[5], o_br[j], o_out[j], rep[6], rep[7], o_up[j], o_down[j],
                rep[8], rep[9]]

    return (loss, grad_x.reshape(1, T, D_MODEL), *tree(0, g_rep), *tree(1, d_rep), *tree(2, nm_rep), *tree(3, nv_rep))
```

```python
import functools
import math

import jax
import jax.numpy as jnp
from jax import lax
from jax.experimental import pallas as pl
from jax.experimental.pallas import tpu as pltpu

F32 = jnp.float32
MXU_DTYPE = jnp.bfloat16
WIRE_DTYPE = jnp.bfloat16

N_DEV = 8
D_MODEL = 1024
DEPTH = 2
CHUNK = 64
ATTN_HEADS = 8
ATTN_HEAD_DIM = 128
ATTN_LEFT_CHUNKS = 8
BAND = (ATTN_LEFT_CHUNKS + 1) * CHUNK
PAD_ROWS = ATTN_LEFT_CHUNKS * CHUNK
REL_CLIP = 2 * CHUNK
N_REL = 2 * REL_CLIP + 1
REL_MIN = REL_CLIP - (CHUNK - 1)
LIN_HEADS = 4
LIN_KEY_DIM = 128
LIN_VAL_DIM = 256
GLA_RANK = 16
GLA_GATE_NORMALIZER = 16.0
ROPE_BASE = 10000.0
D_FF = 4 * D_MODEL
LN_EPS = 1e-5
NEG_INF = -1e30
ALPHA = (2 * DEPTH) ** 0.25
ADAM_LR, ADAM_B1, ADAM_B2, ADAM_EPS, ADAM_WD, ADAM_STEP = 0.001, 0.9, 0.999, 1e-08, 0.01, 10

PROJ_WIDTH = 12304
GLR_SRC = 9216
GATES_SRC = GLR_SRC + GLA_RANK
MAIN_W = 9216
GATES_W = 3 * D_MODEL
GATES_OFF = MAIN_W
GLR_OFF = MAIN_W + GATES_W
PROJ_PAD = 12800
GLR_BLOCK = 128
COL = dict(aq=0, ak=1024, av=2048, rq=3072, rk=3584, rv=4096, rg=5120, gq=6144, gk=6656, gv=7168, gg=8192)

SEG_CHUNKS = 8
SEG = SEG_CHUNKS * CHUNK
VMEM_BIG = 56 * 1024 * 1024

_NN = (((1,), (0,)), ((), ()))
_NT = (((1,), (1,)), ((), ()))
_TN = (((0,), (0,)), ((), ()))


def _mxu(a, b, dn):
    return lax.dot_general(a.astype(MXU_DTYPE), b.astype(MXU_DTYPE), dn, preferred_element_type=F32)


@jax.custom_vjp
def dot_nn(a, b):
    return _mxu(a, b, _NN)


def _dot_nn_fwd(a, b):
    return _mxu(a, b, _NN), (a, b)


def _dot_nn_bwd(res, g):
    a, b = res
    return _mxu(g, b, _NT).astype(a.dtype), _mxu(a, g, _TN).astype(b.dtype)


dot_nn.defvjp(_dot_nn_fwd, _dot_nn_bwd)


@jax.custom_vjp
def dot_nt(a, b):
    return _mxu(a, b, _NT)


def _dot_nt_fwd(a, b):
    return _mxu(a, b, _NT), (a, b)


def _dot_nt_bwd(res, g):
    a, b = res
    return _mxu(g, b, _NN).astype(a.dtype), _mxu(g, a, _TN).astype(b.dtype)


dot_nt.defvjp(_dot_nt_fwd, _dot_nt_bwd)


@jax.custom_vjp
def dot_tn(a, b):
    return _mxu(a, b, _TN)


def _dot_tn_fwd(a, b):
    return _mxu(a, b, _TN), (a, b)


def _dot_tn_bwd(res, g):
    a, b = res
    return _mxu(b, g, _NT).astype(a.dtype), _mxu(a, g, _NN).astype(b.dtype)


dot_tn.defvjp(_dot_tn_fwd, _dot_tn_bwd)


def _tri(n, lower):
    r = lax.broadcasted_iota(jnp.int32, (n, n), 0)
    c = lax.broadcasted_iota(jnp.int32, (n, n), 1)
    return (r >= c) if lower else (r <= c)


def _tri_dot(lower, x):
    tri = _tri(x.shape[0], lower).astype(jnp.bfloat16)
    hi = x.astype(jnp.bfloat16)
    rest = x - hi.astype(F32)
    mid = rest.astype(jnp.bfloat16)
    lo = (rest - mid.astype(F32)).astype(jnp.bfloat16)
    dot = lambda part: lax.dot_general(tri, part, _NN, preferred_element_type=F32)
    return dot(hi) + dot(mid) + dot(lo)


@jax.custom_vjp
def chunk_cumsum(x):
    return _tri_dot(True, x)


def _chunk_cumsum_fwd(x):
    return chunk_cumsum(x), None


def _chunk_cumsum_bwd(_, g):
    return (_tri_dot(False, g),)


chunk_cumsum.defvjp(_chunk_cumsum_fwd, _chunk_cumsum_bwd)


@jax.custom_vjp
def swap_halves(t):
    return pltpu.roll(t, t.shape[-1] // 2, t.ndim - 1)


def _swap_fwd(t):
    return swap_halves(t), None


def _swap_bwd(_, g):
    return (swap_halves(g),)


swap_halves.defvjp(_swap_fwd, _swap_bwd)


def mm(name, a, b, dims, *, tm, tn, tk, out_dtype=F32, extras=(), row_params=(), epi=None, n_out=1, out_dtypes=None,
       side=None, column_slabs=False):
    if dims == "nn":
        (M, K), N = a.shape, b.shape[1]
    elif dims == "nt":
        (M, K), N = a.shape, b.shape[0]
    else:
        (K, M), N = a.shape, b.shape[1]
    tm, tn, tk = min(tm, M), min(tn, N), min(tk, K)
    assert M % tm == 0 and N % tn == 0 and K % tk == 0, (name, M, N, K, tm, tn, tk)
    nk = K // tk
    dn = {"nn": _NN, "nt": _NT, "tn": _TN}[dims]
    a_spec = pl.BlockSpec((tk, tm), lambda i, j, k: (k, i)) if dims == "tn" else pl.BlockSpec((tm, tk), lambda i, j, k: (i, k))
    b_spec = pl.BlockSpec((tn, tk), lambda i, j, k: (j, k)) if dims == "nt" else pl.BlockSpec((tk, tn), lambda i, j, k: (k, j))
    o_spec = pl.BlockSpec((tm, tn), lambda i, j, k: (i, j))
    row_spec = pl.BlockSpec((1, tn), lambda i, j, k: (0, j))
    n_ex = len(extras) + len(row_params)
    out_dtypes = out_dtypes or (out_dtype,) * n_out

    def finish(r, ex_refs, o_refs):
        if epi is not None:
            r = epi(r, *[e[...] for e in ex_refs])
        r = r if isinstance(r, tuple) else (r,)
        for o, v in zip(o_refs, r):
            o[...] = v.astype(o.dtype)

    def body(*refs):
        a_ref, b_ref = refs[:2]
        ex_refs = refs[2:2 + n_ex]
        o_refs = refs[2 + n_ex:2 + n_ex + n_out]
        part = _mxu(a_ref[...], b_ref[...], dn)
        if nk == 1:
            finish(part, ex_refs, o_refs)
            return
        acc = refs[-1]
        k = pl.program_id(2)

        @pl.when(k == 0)
        def _():
            acc[...] = part

        @pl.when(k > 0)
        def _():
            acc[...] += part

        @pl.when(k == nk - 1)
        def _():
            finish(acc[...], ex_refs, o_refs)

    outs = _side_call(
        side, body,
        name=name,
        grid=(M // tm, N // tn, nk),
        in_specs=[a_spec, b_spec] + [o_spec] * len(extras) + [row_spec] * len(row_params),
        out_specs=[pl.BlockSpec((None, tm, tn), lambda i, j, k: (j, i, 0))] * n_out if column_slabs else [o_spec] * n_out,
        out_shape=[jax.ShapeDtypeStruct((N // tn, M, tn) if column_slabs else (M, N), dt) for dt in out_dtypes],
        scratch_shapes=[pltpu.VMEM((tm, tn), F32)] if nk > 1 else [],
        args=[a, b, *extras, *row_params],
        semantics=("parallel", "parallel", "arbitrary"),
    )
    return outs[0] if n_out == 1 else outs


class Row:
    def __init__(self, arr, width=None, col=0, lead=(), rowmap=None):
        self.arr, self.lead, self.col = arr, tuple(lead), col
        self.width = arr.shape[-1] if width is None else width
        self.rowmap = rowmap

    def spec(self, tm):
        lead, col, rowmap = self.lead, self.col, self.rowmap or (lambda i: i)
        return pl.BlockSpec((None,) * len(lead) + (tm, self.width), lambda i: (*lead, rowmap(i), col))


def row_call(name, fn, rows, params, new_outs, red_shapes=(), *, tm, into=None, into_cols=(), stack=None, stacked_onto=None):
    R = rows[0].arr.shape[len(rows[0].lead)]
    tm = min(tm, R)
    assert R % tm == 0, (name, R, tm)
    n_row, n_par, n_new, n_into, n_red = len(rows), len(params), len(new_outs), len(into_cols), len(red_shapes)
    fresh = isinstance(into, jax.ShapeDtypeStruct)

    def body(*refs):
        row_refs = refs[:n_row]
        par_refs = refs[n_row:n_row + n_par]
        pos = n_row + n_par + (1 if (into is not None and not fresh) else 0) + (n_new if stacked_onto is not None else 0)
        new_refs = refs[pos:pos + n_new]
        red_refs = refs[pos + n_new:pos + n_new + n_red]
        pos2 = pos + n_new + n_red
        into_ref = refs[pos2] if into is not None else None
        stage = refs[pos2 + 1:pos2 + 1 + n_into] if into is not None else ()
        i = pl.program_id(0)
        vals = fn(*[r[...] for r in row_refs], *[p[...] for p in par_refs])
        vals = vals if isinstance(vals, (tuple, list)) else (vals,)
        for o, v in zip(new_refs, vals[:n_new]):
            o[...] = v.astype(o.dtype)

        def window_copy(j, step):
            width, off = into_cols[j]
            rows_at = pl.ds(pl.multiple_of(step * tm, tm), tm)
            return pltpu.make_async_copy(stage[j], into_ref.at[rows_at, pl.ds(off, width)], refs[pos2 + 1 + n_into].at[j])

        for j, v in enumerate(vals[n_new:n_new + n_into]):
            @pl.when(i > 0)
            def _(j=j):
                window_copy(j, i - 1).wait()

            stage[j][...] = v.astype(stage[j].dtype)
            window_copy(j, i).start()

            @pl.when(i == R // tm - 1)
            def _(j=j):
                window_copy(j, i).wait()
        for o, v in zip(red_refs, vals[n_new + n_into:]):
            @pl.when(i == 0)
            def _(o=o, v=v):
                o[...] = v.astype(o.dtype)

            @pl.when(i > 0)
            def _(o=o, v=v):
                o[...] += v.astype(o.dtype)

    in_specs = [r.spec(tm) for r in rows] + [pl.BlockSpec(p.shape, lambda i, nd=p.ndim: (0,) * nd) for p in params]
    args = [r.arr for r in rows] + list(params)
    if stack is None:
        out_specs = [pl.BlockSpec((tm, w), lambda i: (i, 0)) for w, _ in new_outs]
        out_shape = [jax.ShapeDtypeStruct((R, w), dt) for w, dt in new_outs]
    else:
        index, depth = stack
        out_specs = [pl.BlockSpec((None, tm, w), lambda i: (index, i, 0)) for w, _ in new_outs]
        out_shape = [jax.ShapeDtypeStruct((depth, R, w), dt) for w, dt in new_outs]
    out_specs += [pl.BlockSpec(s, lambda i, nd=len(s): (0,) * nd) for s in red_shapes]
    out_shape += [jax.ShapeDtypeStruct(s, F32) for s in red_shapes]
    aliases, scratch = {}, []
    if into is not None:
        out_specs.append(pl.BlockSpec(memory_space=pl.ANY))
        out_shape.append(jax.ShapeDtypeStruct(into.shape, into.dtype))
        if not fresh:
            in_specs.append(pl.BlockSpec(memory_space=pl.ANY))
            args.append(into)
            aliases = {len(args) - 1: len(out_shape) - 1}
        scratch = [pltpu.VMEM((tm, w), into.dtype) for w, _ in into_cols] + [pltpu.SemaphoreType.DMA((n_into,))]
    if stacked_onto is not None:
        for j, prev in enumerate(stacked_onto):
            in_specs.append(pl.BlockSpec(memory_space=pl.ANY))
            args.append(prev)
            aliases[len(args) - 1] = j
    outs = pl.pallas_call(
        body,
        name=name,
        grid=(R // tm,),
        in_specs=in_specs,
        out_specs=out_specs,
        out_shape=out_shape,
        scratch_shapes=scratch,
        input_output_aliases=aliases,
        compiler_params=pltpu.CompilerParams(dimension_semantics=("arbitrary",), vmem_limit_bytes=VMEM_BIG),
    )(*args)
    return list(outs)


def vjp_fn(fn, n_row, n_out, passthrough=0):
    def bwd(*args):
        rows = [a.astype(F32) for a in args[:n_row]]
        cts = args[n_row:n_row + n_out]
        thru = args[n_row + n_out:n_row + n_out + passthrough]
        pars = args[n_row + n_out + passthrough:]
        outs, pull = jax.vjp(lambda *xs: tuple(fn(*xs)), *rows, *pars)
        grads = pull(tuple(c.astype(o.dtype) for c, o in zip(cts, outs)))
        return tuple(grads[:n_row]) + tuple(thru) + tuple(grads[n_row:])

    return bwd


def _ln(x, g, b):
    mu = jnp.mean(x, axis=-1, keepdims=True)
    xc = x - mu
    var = jnp.mean(xc * xc, axis=-1, keepdims=True)
    return xc * lax.rsqrt(var + LN_EPS) * g + b


def f_ln_in(x, g, b):
    return (_ln(x, g, b),)


def f_deepnorm(x, y, g, b):
    return (_ln(ALPHA * x + y, g, b),)


def _twice(fn):
    return lambda *args: fn(*args) * 2


STREAM = [(D_MODEL, F32), (D_MODEL, MXU_DTYPE)]


def _rope(t, cos, sin):
    parts = []
    for h in range(LIN_HEADS):
        th = t[:, h * LIN_KEY_DIM:(h + 1) * LIN_KEY_DIM]
        parts.append(th * cos + swap_halves(th) * sin)
    return jnp.concatenate(parts, axis=1)


def f_ret_pre(rq, rk, cos, sin):
    return _rope(rq, cos, sin), _rope(rk, cos, sin) * LIN_KEY_DIM ** -0.5


def _log_sigmoid(x):
    return jnp.minimum(x, 0.0) - jnp.log(1.0 + jnp.exp(-jnp.abs(x)))


def f_gla_pre(gq, glr, wlr, blr):
    gate_pre = dot_nn(glr, wlr) + blr
    return gq * LIN_KEY_DIM ** -0.5, _log_sigmoid(gate_pre) / GLA_GATE_NORMALIZER


def f_normgate(ret_raw, rg, gla_raw, gg, gnorm):
    rets, glas = [], []
    for h in range(LIN_HEADS):
        sl = slice(h * LIN_VAL_DIM, (h + 1) * LIN_VAL_DIM)
        t = ret_raw[:, sl]
        mu = jnp.mean(t, axis=-1, keepdims=True)
        tc = t - mu
        rets.append(tc * lax.rsqrt(jnp.mean(tc * tc, axis=-1, keepdims=True) + LN_EPS))
        u = gla_raw[:, sl]
        glas.append(u * lax.rsqrt(jnp.mean(u * u, axis=-1, keepdims=True) + LN_EPS) * gnorm)
    ret = jax.nn.silu(rg) * jnp.concatenate(rets, axis=1)
    gla = jax.nn.silu(gg) * jnp.concatenate(glas, axis=1)
    return ret, gla


def f_merge(p0, p1, p2, gl):
    out = jax.nn.sigmoid(gl[:, :D_MODEL]) * p0
    out = out + jax.nn.sigmoid(gl[:, D_MODEL:2 * D_MODEL]) * p1
    return (out + jax.nn.sigmoid(gl[:, 2 * D_MODEL:]) * p2,)


def f_loss(y, tgt):
    diff = y - tgt
    sq = jnp.sum(diff * diff, axis=0, keepdims=True)
    lanes = sq[:, :128]
    for j in range(1, D_MODEL // 128):
        lanes = lanes + sq[:, j * 128:(j + 1) * 128]
    return diff * (1.0 / D_MODEL), lanes * (0.5 / D_MODEL)


def _adam(g, w, m, v):
    m = ADAM_B1 * m + (1.0 - ADAM_B1) * g
    v = ADAM_B2 * v + (1.0 - ADAM_B2) * jnp.square(g)
    m_hat = m / (1.0 - ADAM_B1 ** ADAM_STEP)
    v_hat = v / (1.0 - ADAM_B2 ** ADAM_STEP)
    delta = -ADAM_LR * (m_hat / (jnp.sqrt(v_hat) + ADAM_EPS) + ADAM_WD * w)
    return delta, m, v


def f_sum_adam(*args):
    g = args[0].astype(F32)
    for p in args[1:-3]:
        g = g + p.astype(F32)
    return (g,) + _adam(g, *args[-3:])


def f_adam(g, w, m, v):
    return _adam(g, w, m, v)


GROUP = 4
GROUP_ROWS = GROUP * CHUNK
GROUP_KEYS = (GROUP + ATTN_LEFT_CHUNKS) * CHUNK
DIAG_ROWS = REL_CLIP + CHUNK
GRAD_ROWS = 200


def _blocks_by_delta():
    blocks = {}
    for qi in range(GROUP):
        for J in range(GROUP_KEYS // 128):
            blocks.setdefault(qi - 2 * J, []).append((qi, J))
    return blocks


def bias_table(rel_bias, side=None):
    n_tab = rel_bias.shape[0] * ATTN_HEADS

    def body(rb_ref, o_ref):
        h = pl.program_id(0)
        rm = lax.broadcasted_iota(jnp.int32, (CHUNK, 128), 0)
        cm = lax.broadcasted_iota(jnp.int32, (CHUNK, 128), 1)
        ahead = jnp.where(cm >= CHUNK, 1, 0)
        for delta, where in _blocks_by_delta().items():
            base = 64 * delta + PAD_ROWS
            rel = jnp.minimum(base + rm - cm, REL_CLIP) + REL_CLIP
            in_band = (ahead - delta >= 0) & (ahead - delta <= ATTN_LEFT_CHUNKS)
            lo = max(REL_MIN, min(base - 127, REL_CLIP) + REL_CLIP)
            hi = min(base + CHUNK - 1, REL_CLIP) + REL_CLIP

            def step(r, acc, rel=rel):
                return jnp.where(rel == r, rb_ref[h * N_REL + r], acc)

            blk = jnp.zeros((CHUNK, 128), F32) if hi < lo else lax.fori_loop(lo, hi + 1, step, jnp.zeros((CHUNK, 128), F32))
            blk = jnp.where(in_band, blk, NEG_INF)
            for qi, J in where:
                o_ref[pl.ds(qi * CHUNK, CHUNK), pl.ds(J * 128, 128)] = blk

    return _side_call(
        side, body,
        name="bias_table",
        grid=(n_tab,),
        in_specs=[pl.BlockSpec(memory_space=pltpu.SMEM)],
        out_specs=[pl.BlockSpec((None, GROUP_ROWS, GROUP_KEYS), lambda h: (h, 0, 0))],
        out_shape=[jax.ShapeDtypeStruct((n_tab, GROUP_ROWS, GROUP_KEYS), F32)],
        scratch_shapes=[],
        args=[rel_bias.reshape(n_tab * N_REL)],
    )[0]


def bias_table_grad(dbias):
    blocks = _blocks_by_delta()
    below_clip = range(-(CHUNK - 1), REL_CLIP)
    deltas = [d for d in sorted(blocks) if any(-127 <= pos - 64 * d - PAD_ROWS <= CHUNK - 1 for pos in below_clip)]

    def body(d_ref, o_ref, diag):
        sums = []
        for delta in deltas:
            acc = None
            for qi, J in blocks[delta]:
                blk = d_ref[pl.ds(qi * CHUNK, CHUNK), pl.ds(J * 128, 128)]
                acc = blk if acc is None else acc + blk
            sums.append(acc)
        off = lax.broadcasted_iota(jnp.int32, (CHUNK, 128), 0) - lax.broadcasted_iota(jnp.int32, (CHUNK, 128), 1)
        diag[...] = jnp.zeros_like(diag)

        def step(t, carry):
            pos = t - (CHUNK - 1)
            picked = jnp.zeros((CHUNK, 128), F32)
            for delta, s in zip(deltas, sums):
                picked = picked + jnp.where(off == pos - 64 * delta - PAD_ROWS, s, 0.0)
            diag[pl.ds(t, 1), :] = jnp.sum(picked, axis=0, keepdims=True)
            return carry

        lax.fori_loop(0, DIAG_ROWS, step, 0, unroll=8)
        d = d_ref[...]
        total = jnp.sum(jnp.sum(d, axis=1, keepdims=True), axis=0, keepdims=True)
        o_ref[pl.ds(0, DIAG_ROWS), :] = jnp.broadcast_to(jnp.sum(diag[...], axis=1, keepdims=True), (DIAG_ROWS, 128))
        o_ref[pl.ds(DIAG_ROWS, GRAD_ROWS - DIAG_ROWS), :] = jnp.broadcast_to(total, (GRAD_ROWS - DIAG_ROWS, 128))

    out = pl.pallas_call(
        body,
        name="bias_table_grad",
        grid=(ATTN_HEADS,),
        in_specs=[pl.BlockSpec((None, GROUP_ROWS, GROUP_KEYS), lambda h: (h, 0, 0))],
        out_specs=pl.BlockSpec((None, GRAD_ROWS, 128), lambda h: (h, 0, 0)),
        out_shape=jax.ShapeDtypeStruct((ATTN_HEADS, GRAD_ROWS, 128), F32),
        scratch_shapes=[pltpu.VMEM((DIAG_ROWS, 128), F32)],
        compiler_params=pltpu.CompilerParams(dimension_semantics=("arbitrary",)),
    )(dbias)
    below = out[:, :DIAG_ROWS - 1, 0]
    clipped = out[:, DIAG_ROWS, 0] - jnp.sum(below, axis=1)
    return jnp.concatenate([jnp.zeros((ATTN_HEADS, REL_MIN), F32), below, clipped[:, None]], axis=1)


def _attn_stage(q_ref, k_ref, v_ref, qs, kp, vp, T):
    qs[...] = (q_ref[...] * ATTN_HEAD_DIM ** -0.5).astype(qs.dtype)
    for dst, src in ((kp, k_ref), (vp, v_ref)):
        dst[pl.ds(0, PAD_ROWS), :] = jnp.zeros((PAD_ROWS, ATTN_HEAD_DIM), dst.dtype)
        dst[pl.ds(PAD_ROWS, T), :] = src[...].astype(dst.dtype)


def _attn_probs(g, first_groups, qs, kp, b_ref):
    r0 = g * GROUP_ROWS if first_groups else pl.multiple_of(g * GROUP_ROWS, GROUP_ROWS)
    qc = qs[pl.ds(r0, GROUP_ROWS), :]
    kb = kp[pl.ds(r0, GROUP_KEYS), :]
    s = _mxu(qc, kb, _NT) + b_ref[...]
    if first_groups:
        k_chunk = lax.shift_right_logical(lax.broadcasted_iota(jnp.int32, (GROUP_ROWS, GROUP_KEYS), 1), 6)
        s = jnp.where(g * GROUP + k_chunk >= ATTN_LEFT_CHUNKS, s, NEG_INF)
    e = jnp.exp(s - jnp.max(s, axis=-1, keepdims=True))
    return r0, qc, kb, e * (1.0 / jnp.sum(e, axis=-1, keepdims=True))


def _for_groups(T, step):
    n_first = min(ATTN_LEFT_CHUNKS // GROUP, T // GROUP_ROWS)
    for g in range(n_first):
        step(g, True)
    n_loop = T // GROUP_ROWS - n_first
    lax.fori_loop(n_first, T // GROUP_ROWS, lambda g, c: (step(g, False), c)[1], 0, unroll=2 if n_loop % 2 == 0 else 1)


def _head_spec(T, base):
    return pl.BlockSpec((T, ATTN_HEAD_DIM), lambda h: (0, base + h))


def _attn_scratch(T):
    return [pltpu.VMEM((T, ATTN_HEAD_DIM), MXU_DTYPE)] + [pltpu.VMEM((T + PAD_ROWS, ATTN_HEAD_DIM), MXU_DTYPE)] * 2


def attn_fwd(proj, bias_tab, first_tab, side=None):
    T = proj.shape[0]

    def body(q_ref, k_ref, v_ref, b_ref, o_ref, qs, kp, vp):
        _attn_stage(q_ref, k_ref, v_ref, qs, kp, vp, T)

        def step(g, first):
            r0, _, _, p = _attn_probs(g, first, qs, kp, b_ref)
            o_ref[pl.ds(r0, GROUP_ROWS), :] = _mxu(p, vp[pl.ds(r0, GROUP_KEYS), :], _NN).astype(o_ref.dtype)

        _for_groups(T, step)

    out, = _side_call(
        side, body,
        name="attn_fwd",
        grid=(ATTN_HEADS,),
        in_specs=[_head_spec(T, COL["aq"] // 128), _head_spec(T, COL["ak"] // 128), _head_spec(T, COL["av"] // 128),
                  pl.BlockSpec((None, GROUP_ROWS, GROUP_KEYS), lambda h: (first_tab + h, 0, 0))],
        out_specs=[pl.BlockSpec((T, ATTN_HEAD_DIM), lambda h: (0, h))],
        out_shape=[jax.ShapeDtypeStruct((T, ATTN_HEADS * ATTN_HEAD_DIM), MXU_DTYPE)],
        scratch_shapes=_attn_scratch(T),
        args=[proj, proj, proj, bias_tab],
    )
    return out


def attn_bwd(proj, bias_tab, first_tab, dout, dproj, side=None):
    T = proj.shape[0]

    def body(q_ref, k_ref, v_ref, b_ref, do_ref, dproj_in, dbias_ref, dproj_ref, qs, kp, vp, dkp, dvp, dq, stage):
        del dproj_in
        h = pl.program_id(0)
        _attn_stage(q_ref, k_ref, v_ref, qs, kp, vp, T)
        dkp[...] = jnp.zeros_like(dkp)
        dvp[...] = jnp.zeros_like(dvp)
        dbias_ref[...] = jnp.zeros_like(dbias_ref)

        def step(g, first):
            r0, qc, kb, p = _attn_probs(g, first, qs, kp, b_ref)
            do = do_ref[pl.ds(r0, GROUP_ROWS), :]
            dvp[pl.ds(r0, GROUP_KEYS), :] += _mxu(p, do, _TN)
            dp = _mxu(do, vp[pl.ds(r0, GROUP_KEYS), :], _NT)
            ds = p * (dp - jnp.sum(dp * p, axis=-1, keepdims=True))
            dq[pl.ds(r0, GROUP_ROWS), :] = _mxu(ds, kb, _NN) * ATTN_HEAD_DIM ** -0.5
            dkp[pl.ds(r0, GROUP_KEYS), :] += _mxu(ds, qc, _TN)
            dbias_ref[...] += ds

        _for_groups(T, step)
        col = pl.multiple_of(h * ATTN_HEAD_DIM, ATTN_HEAD_DIM)
        for grad, first_row, base in ((dq, 0, "aq"), (dkp, PAD_ROWS, "ak"), (dvp, PAD_ROWS, "av")):
            stage[...] = grad[pl.ds(first_row, T), :].astype(stage.dtype)
            pltpu.sync_copy(stage, dproj_ref.at[:, pl.ds(COL[base] + col, ATTN_HEAD_DIM)])

    dbias, dproj = _side_call(
        side, body,
        name="attn_bwd",
        grid=(ATTN_HEADS,),
        in_specs=[_head_spec(T, COL["aq"] // 128), _head_spec(T, COL["ak"] // 128), _head_spec(T, COL["av"] // 128),
                  pl.BlockSpec((None, GROUP_ROWS, GROUP_KEYS), lambda h: (first_tab + h, 0, 0)),
                  pl.BlockSpec((T, ATTN_HEAD_DIM), lambda h: (0, h)),
                  pl.BlockSpec(memory_space=pl.ANY)],
        out_specs=[pl.BlockSpec((None, GROUP_ROWS, GROUP_KEYS), lambda h: (h, 0, 0)), pl.BlockSpec(memory_space=pl.ANY)],
        out_shape=[jax.ShapeDtypeStruct((ATTN_HEADS, GROUP_ROWS, GROUP_KEYS), F32),
                   jax.ShapeDtypeStruct(dproj.shape, dproj.dtype)],
        scratch_shapes=_attn_scratch(T) + [pltpu.VMEM((T + PAD_ROWS, ATTN_HEAD_DIM), F32)] * 2
        + [pltpu.VMEM((T, ATTN_HEAD_DIM), F32), pltpu.VMEM((T, ATTN_HEAD_DIM), dproj.dtype)],
        args=[proj, proj, proj, bias_tab, dout, dproj],
        aliases={5: 1},
    )
    return dproj, dbias


def _lin_segment(q, k, v, la, st):
    lower = _tri(CHUNK, True)
    steps = (lax.broadcasted_iota(jnp.int32, (CHUNK, 1), 0) + 1).astype(F32)
    outs = []
    for i in range(SEG_CHUNKS):
        sl = slice(i * CHUNK, (i + 1) * CHUNK)
        qi, ki, vi = q[sl], k[sl], v[sl]
        if la.shape[0] == 1:
            cum, last = la * steps, la * float(CHUNK)
        else:
            cum, last = chunk_cumsum(la[sl]), jnp.sum(la[sl], axis=0, keepdims=True)
        e_pos, e_neg = jnp.exp(cum), jnp.exp(-cum)
        q_fwd = qi * e_pos
        s_fwd = dot_nt(q_fwd, ki * e_neg)
        s_bwd = dot_nt(qi * e_neg, ki * e_pos)
        o = dot_nn(jnp.where(lower, s_fwd, s_bwd), vi) + dot_nt(q_fwd, st)
        st = jnp.exp(last) * st + dot_tn(vi, ki * jnp.exp(last - cum))
        outs.append(o)
    return jnp.concatenate(outs, axis=0), st


class Win:
    def __init__(self, arr, width, base=0):
        self.arr, self.width, self.base = arr, width, base


HEADS_PER_STEP = 2


def _lin_spec(w, seg_of):
    assert w.base % HEADS_PER_STEP == 0
    first, width = w.base // HEADS_PER_STEP, HEADS_PER_STEP * w.width
    if w.arr.shape[0] == 1:
        return pl.BlockSpec((1, width), lambda p, s: (0, first + p))
    return pl.BlockSpec((SEG, width), lambda p, s: (seg_of(s), first + p))


def _head_cols(ref, j, width):
    return ref[:, j * width:(j + 1) * width]


def _decay_rows(la_ref, j):
    return _head_cols(la_ref, j, LIN_KEY_DIM)


_STATE_BLOCK = (HEADS_PER_STEP, None, LIN_VAL_DIM, LIN_KEY_DIM)


def lin_fwd(name, q, k, v, la, side=None):
    T = q.arr.shape[0]
    nseg = T // SEG

    def body(q_ref, k_ref, v_ref, la_ref, o_ref, st_ref, st):
        @pl.when(pl.program_id(1) == 0)
        def _():
            st[...] = jnp.zeros_like(st)

        for j in range(HEADS_PER_STEP):
            st0 = st[j]
            st_ref[j] = st0
            o, st1 = _lin_segment(_head_cols(q_ref, j, LIN_KEY_DIM), _head_cols(k_ref, j, LIN_KEY_DIM),
                                  _head_cols(v_ref, j, LIN_VAL_DIM), _decay_rows(la_ref, j), st0)
            o_ref[:, j * LIN_VAL_DIM:(j + 1) * LIN_VAL_DIM] = o
            st[j] = st1

    fwd = lambda s: s
    out, states = _side_call(
        side, body,
        name=name,
        grid=(LIN_HEADS // HEADS_PER_STEP, nseg),
        in_specs=[_lin_spec(w, fwd) for w in (q, k, v, la)],
        out_specs=[pl.BlockSpec((SEG, HEADS_PER_STEP * LIN_VAL_DIM), lambda p, s: (s, p)),
                   pl.BlockSpec(_STATE_BLOCK, lambda p, s: (p, s, 0, 0))],
        out_shape=[jax.ShapeDtypeStruct((T, LIN_HEADS * LIN_VAL_DIM), F32),
                   jax.ShapeDtypeStruct((LIN_HEADS, nseg, LIN_VAL_DIM, LIN_KEY_DIM), F32)],
        scratch_shapes=[pltpu.VMEM((HEADS_PER_STEP, LIN_VAL_DIM, LIN_KEY_DIM), F32)],
        args=[q.arr, k.arr, v.arr, la.arr],
    )
    return out, states


def lin_bwd(name, q, k, v, la, states, dout, with_decay_grad, side=None, into=None, windows=None):
    T = q.arr.shape[0]
    nseg = T // SEG
    n_out = 4 if with_decay_grad else 3

    widths = [LIN_KEY_DIM, LIN_KEY_DIM, LIN_VAL_DIM] + ([LIN_KEY_DIM] if with_decay_grad else [])
    windows = dict(windows or {})
    fresh = [i for i in range(n_out) if i not in windows]
    placed = sorted(windows)
    n_pairs = LIN_HEADS // HEADS_PER_STEP
    rev = lambda s: nseg - 1 - s

    def body(q_ref, k_ref, v_ref, la_ref, st_ref, do_ref, *rest):
        rest = rest[1:] if placed else rest
        outs = dict(zip(fresh, rest[:len(fresh)]))
        rest = rest[len(fresh):]
        into_ref, rest = (rest[0], rest[1:]) if placed else (None, rest)
        dst, stages = rest[0], dict(zip(placed, rest[1:1 + len(placed)]))
        p, s = pl.program_id(0), pl.program_id(1)

        @pl.when(s == 0)
        def _():
            dst[...] = jnp.zeros_like(dst)

        def window_copy(n, i):
            width = HEADS_PER_STEP * widths[i]
            rows = pl.ds(pl.multiple_of(rev(s) * SEG, SEG), SEG)
            cols = pl.ds(pl.multiple_of(windows[i] + p * width, width), width)
            return pltpu.make_async_copy(stages[i], into_ref.at[rows, cols], rest[1 + len(placed)].at[n])

        for n, i in enumerate(placed):
            @pl.when((p > 0) | (s > 0))
            def _(n=n, i=i):
                window_copy(n, i).wait()

        for j in range(HEADS_PER_STEP):
            _, pull = jax.vjp(_lin_segment, _head_cols(q_ref, j, LIN_KEY_DIM).astype(F32),
                              _head_cols(k_ref, j, LIN_KEY_DIM).astype(F32), _head_cols(v_ref, j, LIN_VAL_DIM).astype(F32),
                              _decay_rows(la_ref, j), st_ref[j])
            grads = pull((_head_cols(do_ref, j, LIN_VAL_DIM), dst[j]))
            for i in range(n_out):
                dest = stages[i] if i in windows else outs[i]
                dest[:, j * widths[i]:(j + 1) * widths[i]] = grads[i].astype(dest.dtype)
            dst[j] = grads[4]

        for n, i in enumerate(placed):
            window_copy(n, i).start()

            @pl.when((p == n_pairs - 1) & (s == nseg - 1))
            def _(n=n, i=i):
                window_copy(n, i).wait()

    key_spec = pl.BlockSpec((SEG, HEADS_PER_STEP * LIN_KEY_DIM), lambda p, s: (rev(s), p))
    val_spec = pl.BlockSpec((SEG, HEADS_PER_STEP * LIN_VAL_DIM), lambda p, s: (rev(s), p))
    all_specs = [key_spec, key_spec, val_spec] + ([key_spec] if with_decay_grad else [])
    any_spec = pl.BlockSpec(memory_space=pl.ANY)
    outs = _side_call(
        side, body,
        name=name,
        grid=(n_pairs, nseg),
        in_specs=[_lin_spec(w, rev) for w in (q, k, v, la)] + [
            pl.BlockSpec(_STATE_BLOCK, lambda p, s: (p, rev(s), 0, 0)), val_spec] + ([any_spec] if placed else []),
        out_specs=[all_specs[i] for i in fresh] + ([any_spec] if placed else []),
        out_shape=[jax.ShapeDtypeStruct((T, LIN_HEADS * widths[i]), F32) for i in fresh]
        + ([jax.ShapeDtypeStruct(into.shape, into.dtype)] if placed else []),
        scratch_shapes=[pltpu.VMEM((HEADS_PER_STEP, LIN_VAL_DIM, LIN_KEY_DIM), F32)]
        + [pltpu.VMEM((SEG, HEADS_PER_STEP * widths[i]), into.dtype) for i in placed]
        + ([pltpu.SemaphoreType.DMA((len(placed),))] if placed else []),
        args=[q.arr, k.arr, v.arr, la.arr, states, dout] + ([into] if placed else []),
        aliases={6: len(fresh)} if placed else None,
    )
    return outs


def _coords():
    return lax.axis_index("x"), lax.axis_index("y"), lax.axis_index("c")


def _peer(k):
    x, y, c = _coords()
    px = 1 - x if k & 4 else x
    py = 1 - y if k & 2 else y
    pc = 1 - c if k & 1 else c
    return (px, py, pc), 4 * px + 2 * py + pc


def _remote(src, dst, send_sem, recv_sem, device):
    return pltpu.make_async_remote_copy(src_ref=src, dst_ref=dst, send_sem=send_sem, recv_sem=recv_sem,
                                        device_id=device, device_id_type=pl.DeviceIdType.MESH)


class Side:
    def __init__(self, kind, arrays):
        self.kind, self.arrays, self.n = kind, list(arrays), len(arrays)
        self.outs_per = 1
        self.slots = {"gather": N_DEV - 1, "pair": 4, "chips": 3}[kind]

    def out_shape(self):
        if self.kind == "gather":
            return [jax.ShapeDtypeStruct((N_DEV,) + a.shape, a.dtype) for a in self.arrays]
        return [jax.ShapeDtypeStruct((4,) + a.shape[1:], a.dtype) for a in self.arrays]

    def sems(self):
        return [pltpu.SemaphoreType.DMA((self.n, self.slots)), pltpu.SemaphoreType.DMA((self.n, self.slots)),
                pltpu.SemaphoreType.DMA((self.n,))]

    def start(self, ins, outs, sems):
        send, recv, own = sems
        x, y, c = _coords()
        me = 4 * x + 2 * y + c
        for a in range(self.n):
            if self.kind == "gather":
                pltpu.make_async_copy(ins[a], outs[a].at[me], own.at[a]).start()
                for k in (1, 2, 4, 6):
                    peer, _ = _peer(k)
                    _remote(ins[a], outs[a].at[me], send.at[a, k - 1], recv.at[a, k - 1], peer).start()
            elif self.kind == "pair":
                for j in range(4):
                    _remote(ins[a].at[2 * j + 1 - c], outs[a].at[j], send.at[a, j], recv.at[a, j], (x, y, 1 - c)).start()
            else:
                chip = 2 * x + y
                pltpu.make_async_copy(ins[a].at[chip], outs[a].at[chip], own.at[a]).start()
                for d in (1, 2, 3):
                    tx, ty = (1 - x if d & 2 else x), (1 - y if d & 1 else y)
                    _remote(ins[a].at[2 * tx + ty], outs[a].at[chip], send.at[a, d - 1], recv.at[a, d - 1], (tx, ty, c)).start()

    def finish(self, ins, outs, sems):
        send, recv, own = sems
        x, y, c = _coords()
        me = 4 * x + 2 * y + c
        sibling = (x, y, 1 - c)
        for a in range(self.n):
            if self.kind == "gather":
                blk = outs[a].at[me]
                for k in (2, 4, 6):
                    _, pidx = _peer(k)
                    _remote(blk, blk, send.at[a, k - 1], recv.at[a, k - 1], sibling).wait_recv()
                    _remote(outs[a].at[pidx], outs[a].at[pidx], send.at[a, k], recv.at[a, k], sibling).start()
                pltpu.make_async_copy(ins[a], blk, own.at[a]).wait()
                for k in range(1, N_DEV):
                    _remote(blk, blk, send.at[a, k - 1], recv.at[a, k - 1], sibling).wait_send()
                for k in (1, 3, 5, 7):
                    _remote(blk, blk, send.at[a, k - 1], recv.at[a, k - 1], sibling).wait_recv()
            elif self.kind == "pair":
                for j in range(4):
                    _remote(ins[a].at[j], outs[a].at[j], send.at[a, j], recv.at[a, j], sibling).wait()
            else:
                slab = outs[a].at[0]
                pltpu.make_async_copy(slab, slab, own.at[a]).wait()
                for d in (1, 2, 3):
                    _remote(slab, slab, send.at[a, d - 1], recv.at[a, d - 1], sibling).wait()


def _side_call(side, body, *, name, grid, in_specs, out_specs, out_shape, scratch_shapes, args, semantics=None,
               aliases=None):
    in_specs, out_specs, out_shape = list(in_specs), list(out_specs), list(out_shape)
    scratch_shapes, args = list(scratch_shapes), list(args)
    n_in, n_out = len(in_specs), len(out_specs)
    kernel_body = body
    if side is not None:
        n_so = side.n * side.outs_per

        def kernel_body(*refs):
            ins, side_ins = refs[:n_in], refs[n_in:n_in + side.n]
            pos = n_in + side.n
            outs, side_outs = refs[pos:pos + n_out], refs[pos + n_out:pos + n_out + n_so]
            rest = refs[pos + n_out + n_so:]
            scratch, sems = rest[:-3], rest[-3:]
            ids = [pl.program_id(ax) for ax in range(len(grid))]
            first = functools.reduce(jnp.logical_and, [i == 0 for i in ids])
            last = functools.reduce(jnp.logical_and, [i == g - 1 for i, g in zip(ids, grid)])

            @pl.when(first)
            def _():
                side.start(side_ins, side_outs, sems)

            body(*ins, *outs, *scratch)

            @pl.when(last)
            def _():
                side.finish(side_ins, side_outs, sems)

        any_spec = pl.BlockSpec(memory_space=pl.ANY)
        in_specs += [any_spec] * side.n
        args += side.arrays
        out_specs += [any_spec] * n_so
        out_shape += side.out_shape()
        scratch_shapes += side.sems()
        semantics = ("arbitrary",) * len(grid)
    outs = pl.pallas_call(
        kernel_body,
        name=name,
        grid=grid,
        in_specs=in_specs,
        out_specs=out_specs,
        out_shape=out_shape,
        scratch_shapes=scratch_shapes,
        input_output_aliases=aliases or {},
        compiler_params=pltpu.CompilerParams(dimension_semantics=semantics or ("arbitrary",) * len(grid),
                                             vmem_limit_bytes=VMEM_BIG),
    )(*args)
    if side is not None:
        side.results = list(outs[n_out:])
    return list(outs[:n_out])


def carrier(name, side):
    _side_call(side, lambda: None, name=name, grid=(1,), in_specs=[], out_specs=[], out_shape=[], scratch_shapes=[], args=[])
    return side.results


def all_reduce_small(vec):
    def body(v_ref, o_ref, land, send_sems, recv_sems):
        x, y, c = _coords()
        me = 4 * x + 2 * y + c
        land[me] = v_ref[...]
        copies = []
        for k in range(1, N_DEV):
            peer, _ = _peer(k)
            cp = pltpu.make_async_remote_copy(
                src_ref=v_ref, dst_ref=land.at[me], send_sem=send_sems.at[k - 1], recv_sem=recv_sems.at[k - 1],
                device_id=peer, device_id_type=pl.DeviceIdType.MESH)
            cp.start()
            copies.append(cp)
        for cp in copies:
            cp.wait()
        total = land[0]
        for s in range(1, N_DEV):
            total = total + land[s]
        o_ref[...] = total

    return pl.pallas_call(
        body,
        name="all_reduce_small",
        in_specs=[pl.BlockSpec(memory_space=pltpu.VMEM)],
        out_specs=pl.BlockSpec(memory_space=pltpu.VMEM),
        out_shape=jax.ShapeDtypeStruct(vec.shape, F32),
        scratch_shapes=[pltpu.VMEM((N_DEV,) + vec.shape, F32), pltpu.SemaphoreType.DMA((N_DEV - 1,)),
                        pltpu.SemaphoreType.DMA((N_DEV - 1,))],
    )(vec)


def _pack(parts):
    flat = jnp.concatenate([p.reshape(-1).astype(F32) for p in parts])
    n = flat.shape[0]
    rows = -(-n // 1024) * 8
    return jnp.pad(flat, (0, rows * 128 - n)).reshape(rows, 128)


def _unpack(packed, shapes):
    flat, out, pos = packed.reshape(-1), [], 0
    for s in shapes:
        n = math.prod(s)
        out.append(flat[pos:pos + n].reshape(s))
        pos += n
    return out


def _row2(v):
    return v.reshape(1, -1)


def _rider(hooks, name, ctx=None):
    if name not in hooks:
        return None, lambda: None
    side, done = hooks[name](ctx)
    return side, lambda: done(side.results)


def _layer_forward(h, hb, W, consts, hooks):
    cos, sin, ret_decay = consts
    side, landed = _rider(hooks, "proj")
    proj = mm("proj", hb, W["in"], "nn", tm=4096, tn=512, tk=1024, side=side)
    landed()
    bias_tab, first_tab = W["bias_tab"]
    side, landed = _rider(hooks, "attn_fwd")
    attn = attn_fwd(proj, bias_tab, first_tab, side=side)
    landed()
    rq2, rk2 = row_call("ret_pre", f_ret_pre, [Row(proj, 512, COL["rq"] // 512), Row(proj, 512, COL["rk"] // 512),
                                              Row(cos), Row(sin)], [], [(512, F32), (512, F32)], tm=512)
    side, landed = _rider(hooks, "ret_fwd")
    ret_raw, ret_st = lin_fwd("ret_fwd", Win(rq2, 128), Win(rk2, 128), Win(proj, 256, COL["rv"] // 256), Win(ret_decay, 128),
                              side=side)
    landed()
    gq2, gla_la = row_call("gla_pre", f_gla_pre, [Row(proj, 512, COL["gq"] // 512), Row(proj, GLR_BLOCK, GLR_OFF // GLR_BLOCK)],
                           [W["wlr"], W["blr"]], [(512, F32), (512, F32)], tm=512)
    side, landed = _rider(hooks, "gla_fwd")
    gla_raw, gla_st = lin_fwd("gla_fwd", Win(gq2, 128), Win(proj, 128, COL["gk"] // 128), Win(proj, 256, COL["gv"] // 256),
                              Win(gla_la, 128), side=side)
    landed()
    ret_b, gla_b = row_call("normgate", f_normgate, [Row(ret_raw), Row(proj, 1024, COL["rg"] // 1024), Row(gla_raw),
                                                    Row(proj, 1024, COL["gg"] // 1024)], [W["gnorm"]],
                            [(1024, MXU_DTYPE), (1024, MXU_DTYPE)], tm=512)
    branches = (attn, ret_b, gla_b)
    p = [mm("branch_proj", branches[n], W["branch"][n], "nn", tm=1024, tn=1024, tk=1024) for n in range(3)]
    merged, = row_call("merge", f_merge, [Row(p[0]), Row(p[1]), Row(p[2]), Row(proj, GATES_W, GATES_OFF // GATES_W)], [],
                       [(1024, MXU_DTYPE)], tm=512)
    post_norm = lambda y, x, g, b: f_deepnorm(x, y, g, b) * 2 + (y,)
    x1, x1b, mix = mm("out_proj_ln1", merged, W["out"], "nn", tm=512, tn=D_MODEL, tk=1024, extras=(h,),
                      row_params=(W["ln1_g"], W["ln1_b"]), epi=post_norm, n_out=3, out_dtypes=(F32, MXU_DTYPE, F32))
    side, landed = _rider(hooks, "mlp_up")
    u, act = mm("mlp_up", x1b, W["up"], "nn", tm=2048, tn=512, tk=1024, n_out=2, out_dtypes=(F32, MXU_DTYPE),
                epi=lambda r: (r, jnp.square(jnp.maximum(r, 0.0))), side=side)
    landed()
    x2, x2b, y = mm("mlp_down_ln2", act, W["down"], "nn", tm=1024, tn=D_MODEL, tk=1024, extras=(x1,),
                    row_params=(W["ln2_g"], W["ln2_b"]), epi=post_norm, n_out=3, out_dtypes=(F32, MXU_DTYPE, F32))
    saved = dict(h=h, hb=hb, proj=proj, rq2=rq2, rk2=rk2, ret_st=ret_st, gq2=gq2, gla_la=gla_la,
                 gla_st=gla_st, ret_raw=ret_raw, gla_raw=gla_raw, branches=branches, p=p, merged=merged, mix=mix, x1=x1, x1b=x1b,
                 u=u, act=act, y=y)
    return x2, x2b, saved


def _layer_backward(dx2, S, W, consts, hooks):
    cos, sin, ret_decay = consts
    h, proj, T = S["h"], S["proj"], S["h"].shape[0]
    add = lambda r, e: r + e
    dx1a, dy, dg2, db2 = row_call("ln2_bwd", vjp_fn(f_deepnorm, 2, 1), [Row(S["x1"]), Row(S["y"]), Row(dx2)],
                                  [W["ln2_g"], W["ln2_b"]], [(1024, F32), (1024, MXU_DTYPE)], [(1, 1024), (1, 1024)], tm=512)
    side, landed = _rider(hooks, "mlp_down_bwd")
    du = mm("mlp_down_bwd", dy, W["down"], "nt", tm=1024, tn=512, tk=1024, extras=(S["u"],),
            epi=lambda r, u: r * (2.0 * jnp.maximum(u, 0.0)), out_dtype=MXU_DTYPE, side=side)
    landed()
    g_down = mm("mlp_down_wgrad", S["act"], dy, "tn", tm=1024, tn=1024, tk=2048, out_dtype=WIRE_DTYPE)
    dx1 = mm("mlp_up_bwd", du, W["up"], "nt", tm=1024, tn=1024, tk=2048, extras=(dx1a,), epi=add)
    g_up = mm("mlp_up_wgrad", S["x1b"], du, "tn", tm=1024, tn=D_FF // N_DEV, tk=2048, out_dtype=WIRE_DTYPE,
              column_slabs=True)
    dha, dmix, dg1, db1 = row_call("ln1_bwd", vjp_fn(f_deepnorm, 2, 1), [Row(h), Row(S["mix"]), Row(dx1)],
                                   [W["ln1_g"], W["ln1_b"]], [(1024, F32), (1024, MXU_DTYPE)], [(1, 1024), (1, 1024)], tm=512)
    dmerged = mm("out_proj_bwd", dmix, W["out"], "nt", tm=1024, tn=1024, tk=1024)
    g_out = mm("out_proj_wgrad", S["merged"], dmix, "tn", tm=1024, tn=1024, tk=2048, out_dtype=WIRE_DTYPE)
    dproj = jax.ShapeDtypeStruct((T, PROJ_PAD), MXU_DTYPE)
    p = S["p"]
    dp0, dp1, dp2, dproj = row_call(
        "merge_bwd", vjp_fn(f_merge, 4, 1), [Row(p[0]), Row(p[1]), Row(p[2]), Row(proj, GATES_W, GATES_OFF // GATES_W), Row(dmerged)],
        [], [(1024, MXU_DTYPE)] * 3, tm=256, into=dproj, into_cols=[(GATES_W, GATES_OFF)])
    dps = (dp0, dp1, dp2)
    dbr = [mm("branch_proj_bwd", dps[n], W["branch"][n], "nt", tm=1024, tn=1024, tk=1024,
              out_dtype=MXU_DTYPE if n == 0 else F32) for n in range(3)]
    g_branch = [mm("branch_proj_wgrad", S["branches"][n], dps[n], "tn", tm=1024, tn=1024, tk=2048, out_dtype=WIRE_DTYPE)
                for n in range(3)]
    dret_raw, dgla_raw, dgnorm, dproj = row_call(
        "normgate_bwd", vjp_fn(lambda a, b, c, d, g: _reorder_normgate(a, b, c, d, g), 4, 2),
        [Row(S["ret_raw"]), Row(S["gla_raw"]), Row(proj, 1024, COL["rg"] // 1024), Row(proj, 1024, COL["gg"] // 1024),
         Row(dbr[1]), Row(dbr[2])], [W["gnorm"]], [(1024, F32), (1024, F32)], [(1, LIN_VAL_DIM)], tm=256,
        into=dproj, into_cols=[(1024, COL["rg"]), (1024, COL["gg"])])
    side, landed = _rider(hooks, "gla_bwd")
    dgq2, dgla_la, dproj = lin_bwd("gla_bwd", Win(S["gq2"], 128), Win(proj, 128, COL["gk"] // 128),
                                   Win(proj, 256, COL["gv"] // 256), Win(S["gla_la"], 128), S["gla_st"], dgla_raw, True,
                                   side=side, into=dproj, windows={1: COL["gk"], 2: COL["gv"]})
    landed()
    dwlr, dblr, dproj = row_call(
        "gla_pre_bwd", _gla_pre_bwd, [Row(proj, 512, COL["gq"] // 512), Row(proj, GLR_BLOCK, GLR_OFF // GLR_BLOCK),
                                      Row(dgq2), Row(dgla_la)], [W["wlr"], W["blr"]], [],
        [(GLR_BLOCK, 512), (1, 512)], tm=512, into=dproj,
        into_cols=[(512, COL["gq"]), (PROJ_PAD - GLR_OFF, GLR_OFF)])
    side, landed = _rider(hooks, "ret_bwd", dict(branch=g_branch, out=g_out, up=g_up, down=g_down))
    drq2, drk2, dproj = lin_bwd("ret_bwd", Win(S["rq2"], 128), Win(S["rk2"], 128), Win(proj, 256, COL["rv"] // 256),
                                Win(ret_decay, 128), S["ret_st"], dret_raw, False, side=side, into=dproj,
                                windows={2: COL["rv"]})
    landed()
    dproj, = row_call(
        "ret_pre_bwd", _ret_pre_bwd, [Row(proj, 512, COL["rq"] // 512), Row(proj, 512, COL["rk"] // 512), Row(cos), Row(sin),
                                      Row(drq2), Row(drk2)], [], [], tm=512, into=dproj,
        into_cols=[(512, COL["rq"]), (512, COL["rk"])])
    side, landed = _rider(hooks, "attn_bwd")
    dproj, dbias = attn_bwd(proj, *W["bias_tab"], dbr[0], dproj, side=side)
    landed()
    d_rel_bias = bias_table_grad(dbias)
    side, landed = _rider(hooks, "proj_wgrad", dict(branch=g_branch, out=g_out, up=g_up, down=g_down))
    g_in = mm("proj_wgrad", S["hb"], dproj, "tn", tm=1024, tn=1280, tk=2048, out_dtype=WIRE_DTYPE, side=side)
    landed()
    side, landed = _rider(hooks, "proj_bwd", {"in": g_in})
    dh = mm("proj_bwd", dproj, W["in"], "nt", tm=1024, tn=1024, tk=2560, extras=(dha,), epi=add, side=side)
    landed()
    big = dict(g_in=g_in, branch=g_branch, out=g_out, up=g_up, down=g_down)
    small = dict(rel_bias=d_rel_bias, wlr=dwlr[:GLA_RANK], blr=dblr, gnorm=dgnorm, ln1_g=dg1, ln1_b=db1, ln2_g=dg2, ln2_b=db2)
    return dh, big, small


def _reorder_normgate(ret_raw, gla_raw, rg, gg, gnorm):
    return f_normgate(ret_raw, rg, gla_raw, gg, gnorm)


def _gla_pre_bwd(gq, glr, dq2, dla, wlr, blr):
    _, pull = jax.vjp(f_gla_pre, gq, glr, wlr, blr)
    dgq, dglr, dwlr, dblr = pull((dq2, dla))
    dglr = jnp.concatenate([dglr, jnp.zeros((dglr.shape[0], PROJ_PAD - GLR_OFF - GLR_BLOCK), F32)], axis=1)
    return dgq, dglr, dwlr, dblr


def _ret_pre_bwd(rq, rk, cos, sin, drq2, drk2):
    _, pull = jax.vjp(lambda a, b: f_ret_pre(a, b, cos, sin), rq, rk)
    return pull((drq2, drk2))


SHARDED = ("in", "branch", "out", "up", "down")

_FULL = {
    "branch": lambda g: [g[:, n].reshape(D_MODEL, D_MODEL) for n in range(3)],
    "out": lambda g: g.reshape(D_MODEL, D_MODEL),
    "up": lambda g: jnp.transpose(g, (1, 0, 2)).reshape(D_MODEL, D_FF),
    "down": lambda g: g.reshape(D_FF, D_MODEL),
}


def _owner_pieces(s):
    shard = PROJ_WIDTH // N_DEV
    lo, hi = s * shard, (s + 1) * shard
    pieces = []
    for a, b, padded in ((0, GLR_SRC, 0), (GLR_SRC, GATES_SRC, GLR_OFF), (GATES_SRC, PROJ_WIDTH, GATES_OFF)):
        x0, x1 = max(lo, a), min(hi, b)
        if x0 < x1:
            pieces.append((x0 - lo, x1 - x0, padded + x0 - a))
    return pieces


def _full_in(g):
    placed = sorted((p, s, c, w) for s in range(N_DEV) for c, w, p in _owner_pieces(s))
    cols = [g[s][:, c:c + w] for _, s, c, w in placed]
    return jnp.concatenate(cols + [jnp.zeros((D_MODEL, PROJ_PAD - PROJ_WIDTH), g.dtype)], axis=1)


def _full_wlr(g):
    wlr = jnp.transpose(g, (1, 0, 2)).reshape(GLA_RANK, LIN_HEADS * LIN_KEY_DIM)
    return jnp.pad(wlr, ((0, GLR_BLOCK - GLA_RANK), (0, 0)))


def _slab_in(g):
    return jnp.stack([jnp.concatenate([g[:, p:p + w] for _, w, p in _owner_pieces(s)], axis=1) for s in range(N_DEV)])


_SLABS = {
    "in": _slab_in,
    "branch": lambda gs: jnp.stack([b.reshape(N_DEV, D_MODEL // N_DEV, D_MODEL) for b in gs], axis=1),
    "out": lambda g: g.reshape(N_DEV, D_MODEL // N_DEV, D_MODEL),
    "up": lambda g: g,
    "down": lambda g: g.reshape(N_DEV, D_FF // N_DEV, D_MODEL),
}


def _pair_sums(name, slabs, from_sibling=None):
    sums = []
    for mine, sibs in zip(slabs, from_sibling or carrier(name, Side("pair", slabs))):
        width = mine.shape[-1]
        rows = math.prod(mine.shape[1:-1])
        tm = 256 if rows % 256 == 0 else 128
        nb = rows // tm
        my_rows = Row(mine.reshape(-1, width), rowmap=lambda i, nb=nb: (2 * (i // nb) + lax.axis_index("c")) * nb + i % nb)
        out, = row_call("pair_sum", lambda q, p: (p.astype(F32) + q.astype(F32),),
                        [Row(sibs.reshape(-1, width)), my_rows], [], [(width, WIRE_DTYPE)], tm=tm)
        sums.append(out.reshape(sibs.shape))
    return sums


def _sum_adam(name, parts, w, m, v, layer, tm, earlier):
    C = w.shape[-1]
    rows = [Row(parts, lead=(s,)) for s in range(parts.shape[0])] + [Row(a, lead=(layer,)) for a in (w, m, v)]
    return row_call(name, f_sum_adam, rows, [], [(C, F32)] * 4, tm=tm, stack=(layer, DEPTH), stacked_onto=earlier)


def kernel(x, ln_in_g, ln_in_b, w_in, rel_bias, gla_w_lr, gla_b_lr, gla_norm_g, w_branch, w_out, ln1_g, ln1_b, w_up, w_down, ln2_g, ln2_b, loss_target, m_ln_in_g, m_ln_in_b, m_w_in, m_rel_bias, m_gla_w_lr, m_gla_b_lr, m_gla_norm_g, m_w_branch, m_w_out, m_ln1_g, m_ln1_b, m_w_up, m_w_down, m_ln2_g, m_ln2_b, v_ln_in_g, v_ln_in_b, v_w_in, v_rel_bias, v_gla_w_lr, v_gla_b_lr, v_gla_norm_g, v_w_branch, v_w_out, v_ln1_g, v_ln1_b, v_w_up, v_w_down, v_ln2_g, v_ln2_b):
    T = x.shape[1]
    x2d = x.reshape(T, D_MODEL)
    tgt = loss_target.reshape(T, D_MODEL)
    me = 4 * lax.axis_index("x") + 2 * lax.axis_index("y") + lax.axis_index("c")

    inv = ROPE_BASE ** (-jnp.arange(0, LIN_KEY_DIM, 2, dtype=F32) / LIN_KEY_DIM)
    ang = jnp.arange(T, dtype=F32)[:, None] * inv[None, :]
    cos = jnp.concatenate([jnp.cos(ang), jnp.cos(ang)], axis=1)
    sin = jnp.concatenate([-jnp.sin(ang), jnp.sin(ang)], axis=1)
    log_gamma = jnp.log1p(-jnp.exp2(-5.0 - jnp.arange(LIN_HEADS, dtype=F32)))
    ret_decay = jnp.repeat(log_gamma, LIN_KEY_DIM).reshape(1, LIN_HEADS * LIN_KEY_DIM)
    consts = (cos, sin, ret_decay)

    layers = [dict(blr=_row2(gla_b_lr[l]), gnorm=_row2(gla_norm_g[l]), ln1_g=_row2(ln1_g[l]),
                   ln1_b=_row2(ln1_b[l]), ln2_g=_row2(ln2_g[l]), ln2_b=_row2(ln2_b[l])) for l in range(DEPTH)]
    shards = dict(branch=w_branch, out=w_out, up=w_up, down=w_down)
    shards["in"] = w_in
    wire = [{n: shards[n][l].astype(WIRE_DTYPE) for n in SHARDED} for l in range(DEPTH)]

    def gather_of(l, names):
        def landed(res):
            for n, g in zip(names, res):
                layers[l][n] = _full_in(g) if n == "in" else _FULL[n](g)
        return lambda ctx: (Side("gather", [wire[l][n] for n in names]), landed)

    first = Side("gather", [wire[0]["in"]] + [gla_w_lr[l] for l in range(DEPTH)])
    bias_tabs = bias_table(rel_bias, side=first)
    layers[0]["in"] = _full_in(first.results[0])
    for l in range(DEPTH):
        layers[l]["wlr"] = _full_wlr(first.results[1 + l])
        layers[l]["bias_tab"] = (bias_tabs, l * ATTN_HEADS)

    fwd_hooks = [{"proj": gather_of(1, ("in",)), "attn_fwd": gather_of(0, ("up", "down")),
                  "ret_fwd": gather_of(0, ("branch", "out"))},
                 {"proj": gather_of(1, ("up", "down")), "attn_fwd": gather_of(1, ("branch", "out"))}]
    h, hb = row_call("ln_in", _twice(f_ln_in), [Row(x2d)], [_row2(ln_in_g), _row2(ln_in_b)], STREAM, tm=512)
    saved = []
    for l in range(DEPTH):
        h, hb, s = _layer_forward(h, hb, layers[l], consts, fwd_hooks[l])
        saved.append(s)
    dh, loss_lanes = row_call("loss", f_loss, [Row(h), Row(tgt)], [], [(D_MODEL, F32)], [(1, 128)], tm=512)

    received = [dict() for _ in range(DEPTH)]

    def scatter_of(l, names, sums=None):
        def make(ctx):
            parts = [sums[n] for n in names] if sums is not None else _pair_sums(
                "pair_grads", [_SLABS[n](ctx[n]) for n in names])

            def landed(res):
                received[l].update(zip(names, res))
            return Side("chips", parts), landed
        return make

    small = [None] * DEPTH
    dh, big, small[1] = _layer_backward(dh, saved[1], layers[1], consts, {})
    slabs1 = [_SLABS[n](big["g_in"] if n == "in" else big[n]) for n in SHARDED]
    sums1 = {}

    def pair_exchange_1(ctx):
        def landed(res):
            sums1.update(zip(SHARDED, _pair_sums("pair_grads", slabs1, from_sibling=res)))
        return Side("pair", slabs1), landed

    rest = SHARDED[1:]
    sums0 = {}

    def pair_exchange_0(ctx):
        slabs0 = [_SLABS[n](ctx[n]) for n in rest]

        def landed(res):
            sums0.update(zip(rest, _pair_sums("pair_grads", slabs0, from_sibling=res)))
        return Side("pair", slabs0), landed

    dh, big, small[0] = _layer_backward(dh, saved[0], layers[0], consts, {
        "mlp_down_bwd": pair_exchange_1, "gla_bwd": scatter_of(1, ("in",), sums1), "ret_bwd": pair_exchange_0,
        "attn_bwd": scatter_of(1, rest, sums1), "proj_wgrad": scatter_of(0, rest, sums0), "proj_bwd": scatter_of(0, ("in",))})
    grad_x, d_in_g, d_in_b = row_call("ln_in_bwd", vjp_fn(f_ln_in, 1, 1), [Row(x2d), Row(dh)],
                                      [_row2(ln_in_g), _row2(ln_in_b)], [(D_MODEL, F32)], [(1, D_MODEL), (1, D_MODEL)], tm=256)

    small_names = ("rel_bias", "wlr", "blr", "gnorm", "ln1_g", "ln1_b", "ln2_g", "ln2_b")
    parts = [d_in_g, d_in_b] + [small[l][n] for l in range(DEPTH) for n in small_names] + [loss_lanes]
    shapes = [(D_MODEL,), (D_MODEL,)] + [s for l in range(DEPTH) for s in (
        (ATTN_HEADS, N_REL), (GLA_RANK, 512), (512,), (LIN_VAL_DIM,), (D_MODEL,), (D_MODEL,), (D_MODEL,), (D_MODEL,))] + [(128,)]
    red = _unpack(all_reduce_small(_pack(parts)), shapes)
    loss = jnp.sum(red[-1])
    g_small = {"ln_in_g": red[0], "ln_in_b": red[1]}
    for l in range(DEPTH):
        for j, n in enumerate(small_names):
            g_small[(l, n)] = red[2 + l * len(small_names) + j]
    shard = LIN_HEADS * LIN_KEY_DIM // N_DEV
    g_wlr = jnp.stack([lax.dynamic_slice_in_dim(g_small[(l, "wlr")], me * shard, shard, axis=1) for l in range(DEPTH)])
    stack = lambda n: jnp.stack([g_small[(l, n)] for l in range(DEPTH)])
    g_rep = [g_small["ln_in_g"], g_small["ln_in_b"], stack("rel_bias"), g_wlr, stack("blr"), stack("gnorm"),
             stack("ln1_g"), stack("ln1_b"), stack("ln2_g"), stack("ln2_b")]
    w_rep = [ln_in_g, ln_in_b, rel_bias, gla_w_lr, gla_b_lr, gla_norm_g, ln1_g, ln1_b, ln2_g, ln2_b]
    m_rep = [m_ln_in_g, m_ln_in_b, m_rel_bias, m_gla_w_lr, m_gla_b_lr, m_gla_norm_g, m_ln1_g, m_ln1_b, m_ln2_g, m_ln2_b]
    v_rep = [v_ln_in_g, v_ln_in_b, v_rel_bias, v_gla_w_lr, v_gla_b_lr, v_gla_norm_g, v_ln1_g, v_ln1_b, v_ln2_g, v_ln2_b]
    rep_shapes = [w.shape for w in w_rep]
    d_pack, m_pack, v_pack = row_call("adam_small", f_adam, [Row(_pack(g_rep)), Row(_pack(w_rep)), Row(_pack(m_rep)), Row(_pack(v_rep))],
                                      [], [(128, F32)] * 3, tm=256)
    d_rep, nm_rep, nv_rep = _unpack(d_pack, rep_shapes), _unpack(m_pack, rep_shapes), _unpack(v_pack, rep_shapes)

    def owned(key, w, m, v, name, tm):
        C = w.shape[-1]
        flat = lambda a: a.reshape(DEPTH, -1, C)
        outs = None
        for l in range(DEPTH):
            outs = _sum_adam(name, received[l][key].reshape(4, -1, C), flat(w), flat(m), flat(v), l, tm, outs)
        return [o.reshape(w.shape) for o in outs]

    o_in = owned("in", w_in, m_w_in, v_w_in, "adam_w_in", 256)
    o_br = owned("branch", w_branch, m_w_branch, v_w_branch, "adam_w_branch", 128)
    o_out = owned("out", w_out, m_w_out, v_w_out, "adam_w_out", 128)
    o_up = owned("up", w_up, m_w_up, v_w_up, "adam_w_up", 256)
    o_down = owned("down", w_down, m_w_down, v_w_down, "adam_w_down", 256)

    def tree(j, rep):
        return [rep[0], rep[1], o_in[j], rep[2], rep[3], rep[4], rep[5], o_br[j], o_out[j], rep[6], rep[7], o_up[j], o_down[j],
                rep[8], rep[9]]

    return (loss, grad_x.reshape(1, T, D_MODEL), *tree(0, g_rep), *tree(1, d_rep), *tree(2, nm_rep), *tree(3, nv_rep))
```

```python
import functools
import math

import jax
import jax.numpy as jnp
from jax import lax
from jax.experimental import pallas as pl
from jax.experimental.pallas import tpu as pltpu

F32 = jnp.float32
MXU_DTYPE = jnp.bfloat16
WIRE_DTYPE = jnp.bfloat16
PROJ_DTYPE = jnp.bfloat16

N_DEV = 8
D_MODEL = 1024
DEPTH = 2
CHUNK = 64
ATTN_HEADS = 8
ATTN_HEAD_DIM = 128
ATTN_LEFT_CHUNKS = 8
BAND = (ATTN_LEFT_CHUNKS + 1) * CHUNK
PAD_ROWS = ATTN_LEFT_CHUNKS * CHUNK
REL_CLIP = 2 * CHUNK
N_REL = 2 * REL_CLIP + 1
REL_MIN = REL_CLIP - (CHUNK - 1)
LIN_HEADS = 4
LIN_KEY_DIM = 128
LIN_VAL_DIM = 256
GLA_RANK = 16
GLA_GATE_NORMALIZER = 16.0
ROPE_BASE = 10000.0
D_FF = 4 * D_MODEL
LN_EPS = 1e-5
NEG_INF = -1e30
ALPHA = (2 * DEPTH) ** 0.25
ADAM_LR, ADAM_B1, ADAM_B2, ADAM_EPS, ADAM_WD, ADAM_STEP = 0.001, 0.9, 0.999, 1e-08, 0.01, 10

PROJ_WIDTH = 12304
GLR_SRC = 9216
GATES_SRC = GLR_SRC + GLA_RANK
MAIN_W = 9216
GATES_W = 3 * D_MODEL
GATES_OFF = MAIN_W
GLR_OFF = MAIN_W + GATES_W
PROJ_PAD = 12800
GLR_BLOCK = 128
COL = dict(aq=0, ak=1024, av=2048, rq=3072, rk=3584, rv=4096, rg=5120, gq=6144, gk=6656, gv=7168, gg=8192)

SEG_CHUNKS = 8
SEG = SEG_CHUNKS * CHUNK
VMEM_BIG = 56 * 1024 * 1024

_NN = (((1,), (0,)), ((), ()))
_NT = (((1,), (1,)), ((), ()))
_TN = (((0,), (0,)), ((), ()))


def _mxu(a, b, dn):
    return lax.dot_general(a.astype(MXU_DTYPE), b.astype(MXU_DTYPE), dn, preferred_element_type=F32)


@jax.custom_vjp
def dot_nn(a, b):
    return _mxu(a, b, _NN)


def _dot_nn_fwd(a, b):
    return _mxu(a, b, _NN), (a, b)


def _dot_nn_bwd(res, g):
    a, b = res
    return _mxu(g, b, _NT).astype(a.dtype), _mxu(a, g, _TN).astype(b.dtype)


dot_nn.defvjp(_dot_nn_fwd, _dot_nn_bwd)


@jax.custom_vjp
def dot_nt(a, b):
    return _mxu(a, b, _NT)


def _dot_nt_fwd(a, b):
    return _mxu(a, b, _NT), (a, b)


def _dot_nt_bwd(res, g):
    a, b = res
    return _mxu(g, b, _NN).astype(a.dtype), _mxu(g, a, _TN).astype(b.dtype)


dot_nt.defvjp(_dot_nt_fwd, _dot_nt_bwd)


@jax.custom_vjp
def dot_tn(a, b):
    return _mxu(a, b, _TN)


def _dot_tn_fwd(a, b):
    return _mxu(a, b, _TN), (a, b)


def _dot_tn_bwd(res, g):
    a, b = res
    return _mxu(b, g, _NT).astype(a.dtype), _mxu(a, g, _NN).astype(b.dtype)


dot_tn.defvjp(_dot_tn_fwd, _dot_tn_bwd)


def _tri(n, lower):
    r = lax.broadcasted_iota(jnp.int32, (n, n), 0)
    c = lax.broadcasted_iota(jnp.int32, (n, n), 1)
    return (r >= c) if lower else (r <= c)


def _tri_dot(lower, x):
    tri = _tri(x.shape[0], lower).astype(jnp.bfloat16)
    hi = x.astype(jnp.bfloat16)
    rest = x - hi.astype(F32)
    mid = rest.astype(jnp.bfloat16)
    lo = (rest - mid.astype(F32)).astype(jnp.bfloat16)
    dot = lambda part: lax.dot_general(tri, part, _NN, preferred_element_type=F32)
    return dot(hi) + dot(mid) + dot(lo)


@jax.custom_vjp
def chunk_cumsum(x):
    return _tri_dot(True, x)


def _chunk_cumsum_fwd(x):
    return chunk_cumsum(x), None


def _chunk_cumsum_bwd(_, g):
    return (_tri_dot(False, g),)


chunk_cumsum.defvjp(_chunk_cumsum_fwd, _chunk_cumsum_bwd)


@jax.custom_vjp
def swap_halves(t):
    return pltpu.roll(t, t.shape[-1] // 2, t.ndim - 1)


def _swap_fwd(t):
    return swap_halves(t), None


def _swap_bwd(_, g):
    return (swap_halves(g),)


swap_halves.defvjp(_swap_fwd, _swap_bwd)


def mm(name, a, b, dims, *, tm, tn, tk, out_dtype=F32, extras=(), row_params=(), epi=None, n_out=1, out_dtypes=None,
       side=None, column_slabs=False):
    if dims == "nn":
        (M, K), N = a.shape, b.shape[1]
    elif dims == "nt":
        (M, K), N = a.shape, b.shape[0]
    else:
        (K, M), N = a.shape, b.shape[1]
    tm, tn, tk = min(tm, M), min(tn, N), min(tk, K)
    assert M % tm == 0 and N % tn == 0 and K % tk == 0, (name, M, N, K, tm, tn, tk)
    nk = K // tk
    dn = {"nn": _NN, "nt": _NT, "tn": _TN}[dims]
    a_spec = pl.BlockSpec((tk, tm), lambda i, j, k: (k, i)) if dims == "tn" else pl.BlockSpec((tm, tk), lambda i, j, k: (i, k))
    b_spec = pl.BlockSpec((tn, tk), lambda i, j, k: (j, k)) if dims == "nt" else pl.BlockSpec((tk, tn), lambda i, j, k: (k, j))
    o_spec = pl.BlockSpec((tm, tn), lambda i, j, k: (i, j))
    row_spec = pl.BlockSpec((1, tn), lambda i, j, k: (0, j))
    n_ex = len(extras) + len(row_params)
    out_dtypes = out_dtypes or (out_dtype,) * n_out

    def finish(r, ex_refs, o_refs):
        if epi is not None:
            r = epi(r, *[e[...] for e in ex_refs])
        r = r if isinstance(r, tuple) else (r,)
        for o, v in zip(o_refs, r):
            o[...] = v.astype(o.dtype)

    def body(*refs):
        a_ref, b_ref = refs[:2]
        ex_refs = refs[2:2 + n_ex]
        o_refs = refs[2 + n_ex:2 + n_ex + n_out]
        part = _mxu(a_ref[...], b_ref[...], dn)
        if nk == 1:
            finish(part, ex_refs, o_refs)
            return
        acc = refs[-1]
        k = pl.program_id(2)

        @pl.when(k == 0)
        def _():
            acc[...] = part

        @pl.when(k > 0)
        def _():
            acc[...] += part

        @pl.when(k == nk - 1)
        def _():
            finish(acc[...], ex_refs, o_refs)

    outs = _side_call(
        side, body,
        name=name,
        grid=(M // tm, N // tn, nk),
        in_specs=[a_spec, b_spec] + [o_spec] * len(extras) + [row_spec] * len(row_params),
        out_specs=[pl.BlockSpec((None, tm, tn), lambda i, j, k: (j, i, 0))] * n_out if column_slabs else [o_spec] * n_out,
        out_shape=[jax.ShapeDtypeStruct((N // tn, M, tn) if column_slabs else (M, N), dt) for dt in out_dtypes],
        scratch_shapes=[pltpu.VMEM((tm, tn), F32)] if nk > 1 else [],
        args=[a, b, *extras, *row_params],
        semantics=("parallel", "parallel", "arbitrary"),
    )
    return outs[0] if n_out == 1 else outs


class Row:
    def __init__(self, arr, width=None, col=0, lead=(), rowmap=None):
        self.arr, self.lead, self.col = arr, tuple(lead), col
        self.width = arr.shape[-1] if width is None else width
        self.rowmap = rowmap

    def spec(self, tm):
        lead, col, rowmap = self.lead, self.col, self.rowmap or (lambda i: i)
        return pl.BlockSpec((None,) * len(lead) + (tm, self.width), lambda i: (*lead, rowmap(i), col))


def row_call(name, fn, rows, params, new_outs, red_shapes=(), *, tm, into=None, into_cols=(), stack=None, stacked_onto=None):
    R = rows[0].arr.shape[len(rows[0].lead)]
    tm = min(tm, R)
    assert R % tm == 0, (name, R, tm)
    n_row, n_par, n_new, n_into, n_red = len(rows), len(params), len(new_outs), len(into_cols), len(red_shapes)
    fresh = isinstance(into, jax.ShapeDtypeStruct)

    def body(*refs):
        row_refs = refs[:n_row]
        par_refs = refs[n_row:n_row + n_par]
        pos = n_row + n_par + (1 if (into is not None and not fresh) else 0) + (n_new if stacked_onto is not None else 0)
        new_refs = refs[pos:pos + n_new]
        red_refs = refs[pos + n_new:pos + n_new + n_red]
        pos2 = pos + n_new + n_red
        into_ref = refs[pos2] if into is not None else None
        stage = refs[pos2 + 1:pos2 + 1 + n_into] if into is not None else ()
        i = pl.program_id(0)
        widen = lambda t: t.astype(F32) if jnp.issubdtype(t.dtype, jnp.floating) else t
        vals = fn(*[widen(r[...]) for r in row_refs], *[p[...] for p in par_refs])
        vals = vals if isinstance(vals, (tuple, list)) else (vals,)
        for o, v in zip(new_refs, vals[:n_new]):
            o[...] = v.astype(o.dtype)

        def window_copy(j, step):
            width, off = into_cols[j]
            rows_at = pl.ds(pl.multiple_of(step * tm, tm), tm)
            return pltpu.make_async_copy(stage[j], into_ref.at[rows_at, pl.ds(off, width)], refs[pos2 + 1 + n_into].at[j])

        for j, v in enumerate(vals[n_new:n_new + n_into]):
            @pl.when(i > 0)
            def _(j=j):
                window_copy(j, i - 1).wait()

            stage[j][...] = v.astype(stage[j].dtype)
            window_copy(j, i).start()

            @pl.when(i == R // tm - 1)
            def _(j=j):
                window_copy(j, i).wait()
        for o, v in zip(red_refs, vals[n_new + n_into:]):
            @pl.when(i == 0)
            def _(o=o, v=v):
                o[...] = v.astype(o.dtype)

            @pl.when(i > 0)
            def _(o=o, v=v):
                o[...] += v.astype(o.dtype)

    in_specs = [r.spec(tm) for r in rows] + [pl.BlockSpec(p.shape, lambda i, nd=p.ndim: (0,) * nd) for p in params]
    args = [r.arr for r in rows] + list(params)
    if stack is None:
        out_specs = [pl.BlockSpec((tm, w), lambda i: (i, 0)) for w, _ in new_outs]
        out_shape = [jax.ShapeDtypeStruct((R, w), dt) for w, dt in new_outs]
    else:
        index, depth = stack
        out_specs = [pl.BlockSpec((None, tm, w), lambda i: (index, i, 0)) for w, _ in new_outs]
        out_shape = [jax.ShapeDtypeStruct((depth, R, w), dt) for w, dt in new_outs]
    out_specs += [pl.BlockSpec(s, lambda i, nd=len(s): (0,) * nd) for s in red_shapes]
    out_shape += [jax.ShapeDtypeStruct(s, F32) for s in red_shapes]
    aliases, scratch = {}, []
    if into is not None:
        out_specs.append(pl.BlockSpec(memory_space=pl.ANY))
        out_shape.append(jax.ShapeDtypeStruct(into.shape, into.dtype))
        if not fresh:
            in_specs.append(pl.BlockSpec(memory_space=pl.ANY))
            args.append(into)
            aliases = {len(args) - 1: len(out_shape) - 1}
        scratch = [pltpu.VMEM((tm, w), into.dtype) for w, _ in into_cols] + [pltpu.SemaphoreType.DMA((n_into,))]
    if stacked_onto is not None:
        for j, prev in enumerate(stacked_onto):
            in_specs.append(pl.BlockSpec(memory_space=pl.ANY))
            args.append(prev)
            aliases[len(args) - 1] = j
    outs = pl.pallas_call(
        body,
        name=name,
        grid=(R // tm,),
        in_specs=in_specs,
        out_specs=out_specs,
        out_shape=out_shape,
        scratch_shapes=scratch,
        input_output_aliases=aliases,
        compiler_params=pltpu.CompilerParams(dimension_semantics=("arbitrary",), vmem_limit_bytes=VMEM_BIG),
    )(*args)
    return list(outs)


def vjp_fn(fn, n_row, n_out, passthrough=0):
    def bwd(*args):
        rows = [a.astype(F32) for a in args[:n_row]]
        cts = args[n_row:n_row + n_out]
        thru = args[n_row + n_out:n_row + n_out + passthrough]
        pars = args[n_row + n_out + passthrough:]
        outs, pull = jax.vjp(lambda *xs: tuple(fn(*xs)), *rows, *pars)
        grads = pull(tuple(c.astype(o.dtype) for c, o in zip(cts, outs)))
        return tuple(grads[:n_row]) + tuple(thru) + tuple(grads[n_row:])

    return bwd


def _ln(x, g, b):
    mu = jnp.mean(x, axis=-1, keepdims=True)
    xc = x - mu
    var = jnp.mean(xc * xc, axis=-1, keepdims=True)
    return xc * lax.rsqrt(var + LN_EPS) * g + b


def f_ln_in(x, g, b):
    return (_ln(x, g, b),)


def f_deepnorm(x, y, g, b):
    return (_ln(ALPHA * x + y, g, b),)


def _twice(fn):
    return lambda *args: fn(*args) * 2


STREAM = [(D_MODEL, F32), (D_MODEL, MXU_DTYPE)]


def _rope(t, cos, sin):
    parts = []
    for h in range(LIN_HEADS):
        th = t[:, h * LIN_KEY_DIM:(h + 1) * LIN_KEY_DIM]
        parts.append(th * cos + swap_halves(th) * sin)
    return jnp.concatenate(parts, axis=1)


def f_ret_pre(rq, rk, cos, sin):
    return _rope(rq, cos, sin), _rope(rk, cos, sin) * LIN_KEY_DIM ** -0.5


def _log_sigmoid(x):
    return jnp.minimum(x, 0.0) - jnp.log(1.0 + jnp.exp(-jnp.abs(x)))


def f_gla_pre(gq, glr, wlr, blr):
    gate_pre = dot_nn(glr, wlr) + blr
    return gq * LIN_KEY_DIM ** -0.5, _log_sigmoid(gate_pre) / GLA_GATE_NORMALIZER


def f_normgate(ret_raw, rg, gla_raw, gg, gnorm):
    rets, glas = [], []
    for h in range(LIN_HEADS):
        sl = slice(h * LIN_VAL_DIM, (h + 1) * LIN_VAL_DIM)
        t = ret_raw[:, sl]
        mu = jnp.mean(t, axis=-1, keepdims=True)
        tc = t - mu
        rets.append(tc * lax.rsqrt(jnp.mean(tc * tc, axis=-1, keepdims=True) + LN_EPS))
        u = gla_raw[:, sl]
        glas.append(u * lax.rsqrt(jnp.mean(u * u, axis=-1, keepdims=True) + LN_EPS) * gnorm)
    ret = jax.nn.silu(rg) * jnp.concatenate(rets, axis=1)
    gla = jax.nn.silu(gg) * jnp.concatenate(glas, axis=1)
    return ret, gla


def f_merge(p0, p1, p2, gl):
    out = jax.nn.sigmoid(gl[:, :D_MODEL]) * p0
    out = out + jax.nn.sigmoid(gl[:, D_MODEL:2 * D_MODEL]) * p1
    return (out + jax.nn.sigmoid(gl[:, 2 * D_MODEL:]) * p2,)


def f_loss(y, tgt):
    diff = y - tgt
    sq = jnp.sum(diff * diff, axis=0, keepdims=True)
    lanes = sq[:, :128]
    for j in range(1, D_MODEL // 128):
        lanes = lanes + sq[:, j * 128:(j + 1) * 128]
    return diff * (1.0 / D_MODEL), lanes * (0.5 / D_MODEL)


def _adam(g, w, m, v):
    m = ADAM_B1 * m + (1.0 - ADAM_B1) * g
    v = ADAM_B2 * v + (1.0 - ADAM_B2) * jnp.square(g)
    m_hat = m / (1.0 - ADAM_B1 ** ADAM_STEP)
    v_hat = v / (1.0 - ADAM_B2 ** ADAM_STEP)
    delta = -ADAM_LR * (m_hat / (jnp.sqrt(v_hat) + ADAM_EPS) + ADAM_WD * w)
    return delta, m, v


def f_sum_adam(*args):
    g = args[0].astype(F32)
    for p in args[1:-3]:
        g = g + p.astype(F32)
    return (g,) + _adam(g, *args[-3:])


def f_adam(g, w, m, v):
    return _adam(g, w, m, v)


GROUP = 4
GROUP_ROWS = GROUP * CHUNK
GROUP_KEYS = (GROUP + ATTN_LEFT_CHUNKS) * CHUNK
DIAG_ROWS = REL_CLIP + CHUNK
GRAD_ROWS = 200


def _blocks_by_delta():
    blocks = {}
    for qi in range(GROUP):
        for J in range(GROUP_KEYS // 128):
            blocks.setdefault(qi - 2 * J, []).append((qi, J))
    return blocks


def bias_table(rel_bias, side=None):
    n_tab = rel_bias.shape[0] * ATTN_HEADS

    def body(rb_ref, o_ref):
        h = pl.program_id(0)
        rm = lax.broadcasted_iota(jnp.int32, (CHUNK, 128), 0)
        cm = lax.broadcasted_iota(jnp.int32, (CHUNK, 128), 1)
        ahead = jnp.where(cm >= CHUNK, 1, 0)
        for delta, where in _blocks_by_delta().items():
            base = 64 * delta + PAD_ROWS
            rel = jnp.minimum(base + rm - cm, REL_CLIP) + REL_CLIP
            in_band = (ahead - delta >= 0) & (ahead - delta <= ATTN_LEFT_CHUNKS)
            lo = max(REL_MIN, min(base - 127, REL_CLIP) + REL_CLIP)
            hi = min(base + CHUNK - 1, REL_CLIP) + REL_CLIP

            def step(r, acc, rel=rel):
                return jnp.where(rel == r, rb_ref[h * N_REL + r], acc)

            blk = jnp.zeros((CHUNK, 128), F32) if hi < lo else lax.fori_loop(lo, hi + 1, step, jnp.zeros((CHUNK, 128), F32))
            blk = jnp.where(in_band, blk, NEG_INF)
            for qi, J in where:
                o_ref[pl.ds(qi * CHUNK, CHUNK), pl.ds(J * 128, 128)] = blk

    return _side_call(
        side, body,
        name="bias_table",
        grid=(n_tab,),
        in_specs=[pl.BlockSpec(memory_space=pltpu.SMEM)],
        out_specs=[pl.BlockSpec((None, GROUP_ROWS, GROUP_KEYS), lambda h: (h, 0, 0))],
        out_shape=[jax.ShapeDtypeStruct((n_tab, GROUP_ROWS, GROUP_KEYS), F32)],
        scratch_shapes=[],
        args=[rel_bias.reshape(n_tab * N_REL)],
    )[0]


def bias_table_grad(dbias):
    blocks = _blocks_by_delta()
    below_clip = range(-(CHUNK - 1), REL_CLIP)
    deltas = [d for d in sorted(blocks) if any(-127 <= pos - 64 * d - PAD_ROWS <= CHUNK - 1 for pos in below_clip)]

    def body(d_ref, o_ref, diag):
        sums = []
        for delta in deltas:
            acc = None
            for qi, J in blocks[delta]:
                blk = d_ref[pl.ds(qi * CHUNK, CHUNK), pl.ds(J * 128, 128)]
                acc = blk if acc is None else acc + blk
            sums.append(acc)
        off = lax.broadcasted_iota(jnp.int32, (CHUNK, 128), 0) - lax.broadcasted_iota(jnp.int32, (CHUNK, 128), 1)
        diag[...] = jnp.zeros_like(diag)

        def step(t, carry):
            pos = t - (CHUNK - 1)
            picked = jnp.zeros((CHUNK, 128), F32)
            for delta, s in zip(deltas, sums):
                picked = picked + jnp.where(off == pos - 64 * delta - PAD_ROWS, s, 0.0)
            diag[pl.ds(t, 1), :] = jnp.sum(picked, axis=0, keepdims=True)
            return carry

        lax.fori_loop(0, DIAG_ROWS, step, 0, unroll=8)
        d = d_ref[...]
        total = jnp.sum(jnp.sum(d, axis=1, keepdims=True), axis=0, keepdims=True)
        o_ref[pl.ds(0, DIAG_ROWS), :] = jnp.broadcast_to(jnp.sum(diag[...], axis=1, keepdims=True), (DIAG_ROWS, 128))
        o_ref[pl.ds(DIAG_ROWS, GRAD_ROWS - DIAG_ROWS), :] = jnp.broadcast_to(total, (GRAD_ROWS - DIAG_ROWS, 128))

    out = pl.pallas_call(
        body,
        name="bias_table_grad",
        grid=(ATTN_HEADS,),
        in_specs=[pl.BlockSpec((None, GROUP_ROWS, GROUP_KEYS), lambda h: (h, 0, 0))],
        out_specs=pl.BlockSpec((None, GRAD_ROWS, 128), lambda h: (h, 0, 0)),
        out_shape=jax.ShapeDtypeStruct((ATTN_HEADS, GRAD_ROWS, 128), F32),
        scratch_shapes=[pltpu.VMEM((DIAG_ROWS, 128), F32)],
        compiler_params=pltpu.CompilerParams(dimension_semantics=("arbitrary",)),
    )(dbias)
    below = out[:, :DIAG_ROWS - 1, 0]
    clipped = out[:, DIAG_ROWS, 0] - jnp.sum(below, axis=1)
    return jnp.concatenate([jnp.zeros((ATTN_HEADS, REL_MIN), F32), below, clipped[:, None]], axis=1)


def _attn_stage(q_ref, k_ref, v_ref, qs, kp, vp, T):
    qs[...] = (q_ref[...] * ATTN_HEAD_DIM ** -0.5).astype(qs.dtype)
    for dst, src in ((kp, k_ref), (vp, v_ref)):
        dst[pl.ds(0, PAD_ROWS), :] = jnp.zeros((PAD_ROWS, ATTN_HEAD_DIM), dst.dtype)
        dst[pl.ds(PAD_ROWS, T), :] = src[...].astype(dst.dtype)


def _attn_probs(g, first_groups, qs, kp, b_ref):
    r0 = g * GROUP_ROWS if first_groups else pl.multiple_of(g * GROUP_ROWS, GROUP_ROWS)
    qc = qs[pl.ds(r0, GROUP_ROWS), :]
    kb = kp[pl.ds(r0, GROUP_KEYS), :]
    s = _mxu(qc, kb, _NT) + b_ref[...]
    if first_groups:
        k_chunk = lax.shift_right_logical(lax.broadcasted_iota(jnp.int32, (GROUP_ROWS, GROUP_KEYS), 1), 6)
        s = jnp.where(g * GROUP + k_chunk >= ATTN_LEFT_CHUNKS, s, NEG_INF)
    e = jnp.exp(s - jnp.max(s, axis=-1, keepdims=True))
    return r0, qc, kb, e * (1.0 / jnp.sum(e, axis=-1, keepdims=True))


def _for_groups(T, step):
    n_first = min(ATTN_LEFT_CHUNKS // GROUP, T // GROUP_ROWS)
    for g in range(n_first):
        step(g, True)
    n_loop = T // GROUP_ROWS - n_first
    lax.fori_loop(n_first, T // GROUP_ROWS, lambda g, c: (step(g, False), c)[1], 0, unroll=2 if n_loop % 2 == 0 else 1)


def _head_spec(T, base):
    return pl.BlockSpec((T, ATTN_HEAD_DIM), lambda h: (0, base + h))


def _attn_scratch(T):
    return [pltpu.VMEM((T, ATTN_HEAD_DIM), MXU_DTYPE)] + [pltpu.VMEM((T + PAD_ROWS, ATTN_HEAD_DIM), MXU_DTYPE)] * 2


def attn_fwd(proj, bias_tab, first_tab, side=None):
    T = proj.shape[0]

    def body(q_ref, k_ref, v_ref, b_ref, o_ref, qs, kp, vp):
        _attn_stage(q_ref, k_ref, v_ref, qs, kp, vp, T)

        def step(g, first):
            r0, _, _, p = _attn_probs(g, first, qs, kp, b_ref)
            o_ref[pl.ds(r0, GROUP_ROWS), :] = _mxu(p, vp[pl.ds(r0, GROUP_KEYS), :], _NN).astype(o_ref.dtype)

        _for_groups(T, step)

    out, = _side_call(
        side, body,
        name="attn_fwd",
        grid=(ATTN_HEADS,),
        in_specs=[_head_spec(T, COL["aq"] // 128), _head_spec(T, COL["ak"] // 128), _head_spec(T, COL["av"] // 128),
                  pl.BlockSpec((None, GROUP_ROWS, GROUP_KEYS), lambda h: (first_tab + h, 0, 0))],
        out_specs=[pl.BlockSpec((T, ATTN_HEAD_DIM), lambda h: (0, h))],
        out_shape=[jax.ShapeDtypeStruct((T, ATTN_HEADS * ATTN_HEAD_DIM), MXU_DTYPE)],
        scratch_shapes=_attn_scratch(T),
        args=[proj, proj, proj, bias_tab],
    )
    return out


def attn_bwd(proj, bias_tab, first_tab, dout, dproj, side=None):
    T = proj.shape[0]

    def body(q_ref, k_ref, v_ref, b_ref, do_ref, dproj_in, dbias_ref, dproj_ref, qs, kp, vp, dkp, dvp, dq, stage):
        del dproj_in
        h = pl.program_id(0)
        _attn_stage(q_ref, k_ref, v_ref, qs, kp, vp, T)
        dkp[...] = jnp.zeros_like(dkp)
        dvp[...] = jnp.zeros_like(dvp)
        dbias_ref[...] = jnp.zeros_like(dbias_ref)

        def step(g, first):
            r0, qc, kb, p = _attn_probs(g, first, qs, kp, b_ref)
            do = do_ref[pl.ds(r0, GROUP_ROWS), :]
            dvp[pl.ds(r0, GROUP_KEYS), :] += _mxu(p, do, _TN)
            dp = _mxu(do, vp[pl.ds(r0, GROUP_KEYS), :], _NT)
            ds = p * (dp - jnp.sum(dp * p, axis=-1, keepdims=True))
            dq[pl.ds(r0, GROUP_ROWS), :] = _mxu(ds, kb, _NN) * ATTN_HEAD_DIM ** -0.5
            dkp[pl.ds(r0, GROUP_KEYS), :] += _mxu(ds, qc, _TN)
            dbias_ref[...] += ds

        _for_groups(T, step)
        col = pl.multiple_of(h * ATTN_HEAD_DIM, ATTN_HEAD_DIM)
        for grad, first_row, base in ((dq, 0, "aq"), (dkp, PAD_ROWS, "ak"), (dvp, PAD_ROWS, "av")):
            stage[...] = grad[pl.ds(first_row, T), :].astype(stage.dtype)
            pltpu.sync_copy(stage, dproj_ref.at[:, pl.ds(COL[base] + col, ATTN_HEAD_DIM)])

    dbias, dproj = _side_call(
        side, body,
        name="attn_bwd",
        grid=(ATTN_HEADS,),
        in_specs=[_head_spec(T, COL["aq"] // 128), _head_spec(T, COL["ak"] // 128), _head_spec(T, COL["av"] // 128),
                  pl.BlockSpec((None, GROUP_ROWS, GROUP_KEYS), lambda h: (first_tab + h, 0, 0)),
                  pl.BlockSpec((T, ATTN_HEAD_DIM), lambda h: (0, h)),
                  pl.BlockSpec(memory_space=pl.ANY)],
        out_specs=[pl.BlockSpec((None, GROUP_ROWS, GROUP_KEYS), lambda h: (h, 0, 0)), pl.BlockSpec(memory_space=pl.ANY)],
        out_shape=[jax.ShapeDtypeStruct((ATTN_HEADS, GROUP_ROWS, GROUP_KEYS), F32),
                   jax.ShapeDtypeStruct(dproj.shape, dproj.dtype)],
        scratch_shapes=_attn_scratch(T) + [pltpu.VMEM((T + PAD_ROWS, ATTN_HEAD_DIM), F32)] * 2
        + [pltpu.VMEM((T, ATTN_HEAD_DIM), F32), pltpu.VMEM((T, ATTN_HEAD_DIM), dproj.dtype)],
        args=[proj, proj, proj, bias_tab, dout, dproj],
        aliases={5: 1},
    )
    return dproj, dbias


def _lin_segment(q, k, v, la, st):
    lower = _tri(CHUNK, True)
    steps = (lax.broadcasted_iota(jnp.int32, (CHUNK, 1), 0) + 1).astype(F32)
    outs = []
    for i in range(SEG_CHUNKS):
        sl = slice(i * CHUNK, (i + 1) * CHUNK)
        qi, ki, vi = q[sl], k[sl], v[sl]
        if la.shape[0] == 1:
            cum, last = la * steps, la * float(CHUNK)
        else:
            cum, last = chunk_cumsum(la[sl]), jnp.sum(la[sl], axis=0, keepdims=True)
        e_pos, e_neg = jnp.exp(cum), jnp.exp(-cum)
        q_fwd = qi * e_pos
        s_fwd = dot_nt(q_fwd, ki * e_neg)
        s_bwd = dot_nt(qi * e_neg, ki * e_pos)
        o = dot_nn(jnp.where(lower, s_fwd, s_bwd), vi) + dot_nt(q_fwd, st)
        st = jnp.exp(last) * st + dot_tn(vi, ki * jnp.exp(last - cum))
        outs.append(o)
    return jnp.concatenate(outs, axis=0), st


class Win:
    def __init__(self, arr, width, base=0):
        self.arr, self.width, self.base = arr, width, base


HEADS_PER_STEP = 2


def _lin_spec(w, seg_of):
    assert w.base % HEADS_PER_STEP == 0
    first, width = w.base // HEADS_PER_STEP, HEADS_PER_STEP * w.width
    if w.arr.shape[0] == 1:
        return pl.BlockSpec((1, width), lambda p, s: (0, first + p))
    return pl.BlockSpec((SEG, width), lambda p, s: (seg_of(s), first + p))


def _head_cols(ref, j, width):
    return ref[:, j * width:(j + 1) * width]


def _decay_rows(la_ref, j):
    return _head_cols(la_ref, j, LIN_KEY_DIM)


_STATE_BLOCK = (HEADS_PER_STEP, None, LIN_VAL_DIM, LIN_KEY_DIM)


def lin_fwd(name, q, k, v, la, side=None):
    T = q.arr.shape[0]
    nseg = T // SEG

    def body(q_ref, k_ref, v_ref, la_ref, o_ref, st_ref, st):
        @pl.when(pl.program_id(1) == 0)
        def _():
            st[...] = jnp.zeros_like(st)

        for j in range(HEADS_PER_STEP):
            st0 = st[j]
            st_ref[j] = st0
            o, st1 = _lin_segment(_head_cols(q_ref, j, LIN_KEY_DIM), _head_cols(k_ref, j, LIN_KEY_DIM),
                                  _head_cols(v_ref, j, LIN_VAL_DIM), _decay_rows(la_ref, j), st0)
            o_ref[:, j * LIN_VAL_DIM:(j + 1) * LIN_VAL_DIM] = o
            st[j] = st1

    fwd = lambda s: s
    out, states = _side_call(
        side, body,
        name=name,
        grid=(LIN_HEADS // HEADS_PER_STEP, nseg),
        in_specs=[_lin_spec(w, fwd) for w in (q, k, v, la)],
        out_specs=[pl.BlockSpec((SEG, HEADS_PER_STEP * LIN_VAL_DIM), lambda p, s: (s, p)),
                   pl.BlockSpec(_STATE_BLOCK, lambda p, s: (p, s, 0, 0))],
        out_shape=[jax.ShapeDtypeStruct((T, LIN_HEADS * LIN_VAL_DIM), F32),
                   jax.ShapeDtypeStruct((LIN_HEADS, nseg, LIN_VAL_DIM, LIN_KEY_DIM), F32)],
        scratch_shapes=[pltpu.VMEM((HEADS_PER_STEP, LIN_VAL_DIM, LIN_KEY_DIM), F32)],
        args=[q.arr, k.arr, v.arr, la.arr],
    )
    return out, states


def lin_bwd(name, q, k, v, la, states, dout, with_decay_grad, side=None, into=None, windows=None):
    T = q.arr.shape[0]
    nseg = T // SEG
    n_out = 4 if with_decay_grad else 3

    widths = [LIN_KEY_DIM, LIN_KEY_DIM, LIN_VAL_DIM] + ([LIN_KEY_DIM] if with_decay_grad else [])
    windows = dict(windows or {})
    fresh = [i for i in range(n_out) if i not in windows]
    placed = sorted(windows)
    n_pairs = LIN_HEADS // HEADS_PER_STEP
    rev = lambda s: nseg - 1 - s

    def body(q_ref, k_ref, v_ref, la_ref, st_ref, do_ref, *rest):
        rest = rest[1:] if placed else rest
        outs = dict(zip(fresh, rest[:len(fresh)]))
        rest = rest[len(fresh):]
        into_ref, rest = (rest[0], rest[1:]) if placed else (None, rest)
        dst, stages = rest[0], dict(zip(placed, rest[1:1 + len(placed)]))
        p, s = pl.program_id(0), pl.program_id(1)

        @pl.when(s == 0)
        def _():
            dst[...] = jnp.zeros_like(dst)

        def window_copy(n, i):
            width = HEADS_PER_STEP * widths[i]
            rows = pl.ds(pl.multiple_of(rev(s) * SEG, SEG), SEG)
            cols = pl.ds(pl.multiple_of(windows[i] + p * width, width), width)
            return pltpu.make_async_copy(stages[i], into_ref.at[rows, cols], rest[1 + len(placed)].at[n])

        for n, i in enumerate(placed):
            @pl.when((p > 0) | (s > 0))
            def _(n=n, i=i):
                window_copy(n, i).wait()

        for j in range(HEADS_PER_STEP):
            _, pull = jax.vjp(_lin_segment, _head_cols(q_ref, j, LIN_KEY_DIM).astype(F32),
                              _head_cols(k_ref, j, LIN_KEY_DIM).astype(F32), _head_cols(v_ref, j, LIN_VAL_DIM).astype(F32),
                              _decay_rows(la_ref, j), st_ref[j])
            grads = pull((_head_cols(do_ref, j, LIN_VAL_DIM), dst[j]))
            for i in range(n_out):
                dest = stages[i] if i in windows else outs[i]
                dest[:, j * widths[i]:(j + 1) * widths[i]] = grads[i].astype(dest.dtype)
            dst[j] = grads[4]

        for n, i in enumerate(placed):
            window_copy(n, i).start()

            @pl.when((p == n_pairs - 1) & (s == nseg - 1))
            def _(n=n, i=i):
                window_copy(n, i).wait()

    key_spec = pl.BlockSpec((SEG, HEADS_PER_STEP * LIN_KEY_DIM), lambda p, s: (rev(s), p))
    val_spec = pl.BlockSpec((SEG, HEADS_PER_STEP * LIN_VAL_DIM), lambda p, s: (rev(s), p))
    all_specs = [key_spec, key_spec, val_spec] + ([key_spec] if with_decay_grad else [])
    any_spec = pl.BlockSpec(memory_space=pl.ANY)
    outs = _side_call(
        side, body,
        name=name,
        grid=(n_pairs, nseg),
        in_specs=[_lin_spec(w, rev) for w in (q, k, v, la)] + [
            pl.BlockSpec(_STATE_BLOCK, lambda p, s: (p, rev(s), 0, 0)), val_spec] + ([any_spec] if placed else []),
        out_specs=[all_specs[i] for i in fresh] + ([any_spec] if placed else []),
        out_shape=[jax.ShapeDtypeStruct((T, LIN_HEADS * widths[i]), F32) for i in fresh]
        + ([jax.ShapeDtypeStruct(into.shape, into.dtype)] if placed else []),
        scratch_shapes=[pltpu.VMEM((HEADS_PER_STEP, LIN_VAL_DIM, LIN_KEY_DIM), F32)]
        + [pltpu.VMEM((SEG, HEADS_PER_STEP * widths[i]), into.dtype) for i in placed]
        + ([pltpu.SemaphoreType.DMA((len(placed),))] if placed else []),
        args=[q.arr, k.arr, v.arr, la.arr, states, dout] + ([into] if placed else []),
        aliases={6: len(fresh)} if placed else None,
    )
    return outs


def _coords():
    return lax.axis_index("x"), lax.axis_index("y"), lax.axis_index("c")


def _peer(k):
    x, y, c = _coords()
    px = 1 - x if k & 4 else x
    py = 1 - y if k & 2 else y
    pc = 1 - c if k & 1 else c
    return (px, py, pc), 4 * px + 2 * py + pc


def _remote(src, dst, send_sem, recv_sem, device):
    return pltpu.make_async_remote_copy(src_ref=src, dst_ref=dst, send_sem=send_sem, recv_sem=recv_sem,
                                        device_id=device, device_id_type=pl.DeviceIdType.MESH)


class Side:
    def __init__(self, kind, arrays):
        self.kind, self.arrays, self.n = kind, list(arrays), len(arrays)
        self.outs_per = 1
        self.slots = {"gather": N_DEV - 1, "pair": 4, "chips": 3}[kind]

    def out_shape(self):
        if self.kind == "gather":
            return [jax.ShapeDtypeStruct((N_DEV,) + a.shape, a.dtype) for a in self.arrays]
        return [jax.ShapeDtypeStruct((4,) + a.shape[1:], a.dtype) for a in self.arrays]

    def sems(self):
        return [pltpu.SemaphoreType.DMA((self.n, self.slots)), pltpu.SemaphoreType.DMA((self.n, self.slots)),
                pltpu.SemaphoreType.DMA((self.n,))]

    def start(self, ins, outs, sems):
        send, recv, own = sems
        x, y, c = _coords()
        me = 4 * x + 2 * y + c
        for a in range(self.n):
            if self.kind == "gather":
                pltpu.make_async_copy(ins[a], outs[a].at[me], own.at[a]).start()
                for k in (1, 2, 4, 6):
                    peer, _ = _peer(k)
                    _remote(ins[a], outs[a].at[me], send.at[a, k - 1], recv.at[a, k - 1], peer).start()
            elif self.kind == "pair":
                for j in range(4):
                    _remote(ins[a].at[2 * j + 1 - c], outs[a].at[j], send.at[a, j], recv.at[a, j], (x, y, 1 - c)).start()
            else:
                chip = 2 * x + y
                pltpu.make_async_copy(ins[a].at[chip], outs[a].at[chip], own.at[a]).start()
                for d in (1, 2, 3):
                    tx, ty = (1 - x if d & 2 else x), (1 - y if d & 1 else y)
                    _remote(ins[a].at[2 * tx + ty], outs[a].at[chip], send.at[a, d - 1], recv.at[a, d - 1], (tx, ty, c)).start()

    def finish(self, ins, outs, sems):
        send, recv, own = sems
        x, y, c = _coords()
        me = 4 * x + 2 * y + c
        sibling = (x, y, 1 - c)
        for a in range(self.n):
            if self.kind == "gather":
                blk = outs[a].at[me]
                for k in (2, 4, 6):
                    _, pidx = _peer(k)
                    _remote(blk, blk, send.at[a, k - 1], recv.at[a, k - 1], sibling).wait_recv()
                    _remote(outs[a].at[pidx], outs[a].at[pidx], send.at[a, k], recv.at[a, k], sibling).start()
                pltpu.make_async_copy(ins[a], blk, own.at[a]).wait()
                for k in range(1, N_DEV):
                    _remote(blk, blk, send.at[a, k - 1], recv.at[a, k - 1], sibling).wait_send()
                for k in (1, 3, 5, 7):
                    _remote(blk, blk, send.at[a, k - 1], recv.at[a, k - 1], sibling).wait_recv()
            elif self.kind == "pair":
                for j in range(4):
                    _remote(ins[a].at[j], outs[a].at[j], send.at[a, j], recv.at[a, j], sibling).wait()
            else:
                slab = outs[a].at[0]
                pltpu.make_async_copy(slab, slab, own.at[a]).wait()
                for d in (1, 2, 3):
                    _remote(slab, slab, send.at[a, d - 1], recv.at[a, d - 1], sibling).wait()


def _side_call(side, body, *, name, grid, in_specs, out_specs, out_shape, scratch_shapes, args, semantics=None,
               aliases=None):
    in_specs, out_specs, out_shape = list(in_specs), list(out_specs), list(out_shape)
    scratch_shapes, args = list(scratch_shapes), list(args)
    n_in, n_out = len(in_specs), len(out_specs)
    kernel_body = body
    if side is not None:
        n_so = side.n * side.outs_per

        def kernel_body(*refs):
            ins, side_ins = refs[:n_in], refs[n_in:n_in + side.n]
            pos = n_in + side.n
            outs, side_outs = refs[pos:pos + n_out], refs[pos + n_out:pos + n_out + n_so]
            rest = refs[pos + n_out + n_so:]
            scratch, sems = rest[:-3], rest[-3:]
            ids = [pl.program_id(ax) for ax in range(len(grid))]
            first = functools.reduce(jnp.logical_and, [i == 0 for i in ids])
            last = functools.reduce(jnp.logical_and, [i == g - 1 for i, g in zip(ids, grid)])

            @pl.when(first)
            def _():
                side.start(side_ins, side_outs, sems)

            body(*ins, *outs, *scratch)

            @pl.when(last)
            def _():
                side.finish(side_ins, side_outs, sems)

        any_spec = pl.BlockSpec(memory_space=pl.ANY)
        in_specs += [any_spec] * side.n
        args += side.arrays
        out_specs += [any_spec] * n_so
        out_shape += side.out_shape()
        scratch_shapes += side.sems()
        semantics = ("arbitrary",) * len(grid)
    outs = pl.pallas_call(
        kernel_body,
        name=name,
        grid=grid,
        in_specs=in_specs,
        out_specs=out_specs,
        out_shape=out_shape,
        scratch_shapes=scratch_shapes,
        input_output_aliases=aliases or {},
        compiler_params=pltpu.CompilerParams(dimension_semantics=semantics or ("arbitrary",) * len(grid),
                                             vmem_limit_bytes=VMEM_BIG),
    )(*args)
    if side is not None:
        side.results = list(outs[n_out:])
    return list(outs[:n_out])


def carrier(name, side):
    _side_call(side, lambda: None, name=name, grid=(1,), in_specs=[], out_specs=[], out_shape=[], scratch_shapes=[], args=[])
    return side.results


def all_reduce_small(vec):
    def body(v_ref, o_ref, land, send_sems, recv_sems):
        x, y, c = _coords()
        me = 4 * x + 2 * y + c
        land[me] = v_ref[...]
        copies = []
        for k in range(1, N_DEV):
            peer, _ = _peer(k)
            cp = pltpu.make_async_remote_copy(
                src_ref=v_ref, dst_ref=land.at[me], send_sem=send_sems.at[k - 1], recv_sem=recv_sems.at[k - 1],
                device_id=peer, device_id_type=pl.DeviceIdType.MESH)
            cp.start()
            copies.append(cp)
        for cp in copies:
            cp.wait()
        total = land[0]
        for s in range(1, N_DEV):
            total = total + land[s]
        o_ref[...] = total

    return pl.pallas_call(
        body,
        name="all_reduce_small",
        in_specs=[pl.BlockSpec(memory_space=pltpu.VMEM)],
        out_specs=pl.BlockSpec(memory_space=pltpu.VMEM),
        out_shape=jax.ShapeDtypeStruct(vec.shape, F32),
        scratch_shapes=[pltpu.VMEM((N_DEV,) + vec.shape, F32), pltpu.SemaphoreType.DMA((N_DEV - 1,)),
                        pltpu.SemaphoreType.DMA((N_DEV - 1,))],
    )(vec)


def _pack(parts):
    flat = jnp.concatenate([p.reshape(-1).astype(F32) for p in parts])
    n = flat.shape[0]
    rows = -(-n // 1024) * 8
    return jnp.pad(flat, (0, rows * 128 - n)).reshape(rows, 128)


def _unpack(packed, shapes):
    flat, out, pos = packed.reshape(-1), [], 0
    for s in shapes:
        n = math.prod(s)
        out.append(flat[pos:pos + n].reshape(s))
        pos += n
    return out


def _row2(v):
    return v.reshape(1, -1)


def _rider(hooks, name, ctx=None):
    if name not in hooks:
        return None, lambda: None
    side, done = hooks[name](ctx)
    return side, lambda: done(side.results)


def _layer_forward(h, hb, W, consts, hooks):
    cos, sin, ret_decay = consts
    side, landed = _rider(hooks, "proj")
    proj = mm("proj", hb, W["in"], "nn", tm=4096, tn=512, tk=1024, side=side, out_dtype=PROJ_DTYPE)
    landed()
    bias_tab, first_tab = W["bias_tab"]
    side, landed = _rider(hooks, "attn_fwd")
    attn = attn_fwd(proj, bias_tab, first_tab, side=side)
    landed()
    rq2, rk2 = row_call("ret_pre", f_ret_pre, [Row(proj, 512, COL["rq"] // 512), Row(proj, 512, COL["rk"] // 512),
                                              Row(cos), Row(sin)], [], [(512, F32), (512, F32)], tm=512)
    side, landed = _rider(hooks, "ret_fwd")
    ret_raw, ret_st = lin_fwd("ret_fwd", Win(rq2, 128), Win(rk2, 128), Win(proj, 256, COL["rv"] // 256), Win(ret_decay, 128),
                              side=side)
    landed()
    gq2, gla_la = row_call("gla_pre", f_gla_pre, [Row(proj, 512, COL["gq"] // 512), Row(proj, GLR_BLOCK, GLR_OFF // GLR_BLOCK)],
                           [W["wlr"], W["blr"]], [(512, F32), (512, F32)], tm=512)
    side, landed = _rider(hooks, "gla_fwd")
    gla_raw, gla_st = lin_fwd("gla_fwd", Win(gq2, 128), Win(proj, 128, COL["gk"] // 128), Win(proj, 256, COL["gv"] // 256),
                              Win(gla_la, 128), side=side)
    landed()
    ret_b, gla_b = row_call("normgate", f_normgate, [Row(ret_raw), Row(proj, 1024, COL["rg"] // 1024), Row(gla_raw),
                                                    Row(proj, 1024, COL["gg"] // 1024)], [W["gnorm"]],
                            [(1024, MXU_DTYPE), (1024, MXU_DTYPE)], tm=512)
    branches = (attn, ret_b, gla_b)
    p = [mm("branch_proj", branches[n], W["branch"][n], "nn", tm=1024, tn=1024, tk=1024) for n in range(3)]
    merged, = row_call("merge", f_merge, [Row(p[0]), Row(p[1]), Row(p[2]), Row(proj, GATES_W, GATES_OFF // GATES_W)], [],
                       [(1024, MXU_DTYPE)], tm=512)
    post_norm = lambda y, x, g, b: f_deepnorm(x, y, g, b) * 2 + (y,)
    x1, x1b, mix = mm("out_proj_ln1", merged, W["out"], "nn", tm=512, tn=D_MODEL, tk=1024, extras=(h,),
                      row_params=(W["ln1_g"], W["ln1_b"]), epi=post_norm, n_out=3, out_dtypes=(F32, MXU_DTYPE, F32))
    side, landed = _rider(hooks, "mlp_up")
    u, act = mm("mlp_up", x1b, W["up"], "nn", tm=2048, tn=512, tk=1024, n_out=2, out_dtypes=(F32, MXU_DTYPE),
                epi=lambda r: (r, jnp.square(jnp.maximum(r, 0.0))), side=side)
    landed()
    x2, x2b, y = mm("mlp_down_ln2", act, W["down"], "nn", tm=1024, tn=D_MODEL, tk=1024, extras=(x1,),
                    row_params=(W["ln2_g"], W["ln2_b"]), epi=post_norm, n_out=3, out_dtypes=(F32, MXU_DTYPE, F32))
    saved = dict(h=h, hb=hb, proj=proj, rq2=rq2, rk2=rk2, ret_st=ret_st, gq2=gq2, gla_la=gla_la,
                 gla_st=gla_st, ret_raw=ret_raw, gla_raw=gla_raw, branches=branches, p=p, merged=merged, mix=mix, x1=x1, x1b=x1b,
                 u=u, act=act, y=y)
    return x2, x2b, saved


def _layer_backward(dx2, S, W, consts, hooks):
    cos, sin, ret_decay = consts
    h, proj, T = S["h"], S["proj"], S["h"].shape[0]
    add = lambda r, e: r + e
    dx1a, dy, dg2, db2 = row_call("ln2_bwd", vjp_fn(f_deepnorm, 2, 1), [Row(S["x1"]), Row(S["y"]), Row(dx2)],
                                  [W["ln2_g"], W["ln2_b"]], [(1024, F32), (1024, MXU_DTYPE)], [(1, 1024), (1, 1024)], tm=512)
    side, landed = _rider(hooks, "mlp_down_bwd")
    du = mm("mlp_down_bwd", dy, W["down"], "nt", tm=1024, tn=512, tk=1024, extras=(S["u"],),
            epi=lambda r, u: r * (2.0 * jnp.maximum(u, 0.0)), out_dtype=MXU_DTYPE, side=side)
    landed()
    g_down = mm("mlp_down_wgrad", S["act"], dy, "tn", tm=1024, tn=1024, tk=2048, out_dtype=WIRE_DTYPE)
    dx1 = mm("mlp_up_bwd", du, W["up"], "nt", tm=1024, tn=1024, tk=2048, extras=(dx1a,), epi=add)
    g_up = mm("mlp_up_wgrad", S["x1b"], du, "tn", tm=1024, tn=D_FF // N_DEV, tk=2048, out_dtype=WIRE_DTYPE,
              column_slabs=True)
    dha, dmix, dg1, db1 = row_call("ln1_bwd", vjp_fn(f_deepnorm, 2, 1), [Row(h), Row(S["mix"]), Row(dx1)],
                                   [W["ln1_g"], W["ln1_b"]], [(1024, F32), (1024, MXU_DTYPE)], [(1, 1024), (1, 1024)], tm=512)
    dmerged = mm("out_proj_bwd", dmix, W["out"], "nt", tm=1024, tn=1024, tk=1024)
    g_out = mm("out_proj_wgrad", S["merged"], dmix, "tn", tm=1024, tn=1024, tk=2048, out_dtype=WIRE_DTYPE)
    dproj = jax.ShapeDtypeStruct((T, PROJ_PAD), MXU_DTYPE)
    p = S["p"]
    dp0, dp1, dp2, dproj = row_call(
        "merge_bwd", vjp_fn(f_merge, 4, 1), [Row(p[0]), Row(p[1]), Row(p[2]), Row(proj, GATES_W, GATES_OFF // GATES_W), Row(dmerged)],
        [], [(1024, MXU_DTYPE)] * 3, tm=256, into=dproj, into_cols=[(GATES_W, GATES_OFF)])
    dps = (dp0, dp1, dp2)
    dbr = [mm("branch_proj_bwd", dps[n], W["branch"][n], "nt", tm=1024, tn=1024, tk=1024,
              out_dtype=MXU_DTYPE if n == 0 else F32) for n in range(3)]
    g_branch = [mm("branch_proj_wgrad", S["branches"][n], dps[n], "tn", tm=1024, tn=1024, tk=2048, out_dtype=WIRE_DTYPE)
                for n in range(3)]
    dret_raw, dgla_raw, dgnorm, dproj = row_call(
        "normgate_bwd", vjp_fn(lambda a, b, c, d, g: _reorder_normgate(a, b, c, d, g), 4, 2),
        [Row(S["ret_raw"]), Row(S["gla_raw"]), Row(proj, 1024, COL["rg"] // 1024), Row(proj, 1024, COL["gg"] // 1024),
         Row(dbr[1]), Row(dbr[2])], [W["gnorm"]], [(1024, F32), (1024, F32)], [(1, LIN_VAL_DIM)], tm=256,
        into=dproj, into_cols=[(1024, COL["rg"]), (1024, COL["gg"])])
    side, landed = _rider(hooks, "gla_bwd")
    dgq2, dgla_la, dproj = lin_bwd("gla_bwd", Win(S["gq2"], 128), Win(proj, 128, COL["gk"] // 128),
                                   Win(proj, 256, COL["gv"] // 256), Win(S["gla_la"], 128), S["gla_st"], dgla_raw, True,
                                   side=side, into=dproj, windows={1: COL["gk"], 2: COL["gv"]})
    landed()
    dwlr, dblr, dproj = row_call(
        "gla_pre_bwd", _gla_pre_bwd, [Row(proj, 512, COL["gq"] // 512), Row(proj, GLR_BLOCK, GLR_OFF // GLR_BLOCK),
                                      Row(dgq2), Row(dgla_la)], [W["wlr"], W["blr"]], [],
        [(GLR_BLOCK, 512), (1, 512)], tm=512, into=dproj,
        into_cols=[(512, COL["gq"]), (PROJ_PAD - GLR_OFF, GLR_OFF)])
    side, landed = _rider(hooks, "ret_bwd", dict(branch=g_branch, out=g_out, up=g_up, down=g_down))
    drq2, drk2, dproj = lin_bwd("ret_bwd", Win(S["rq2"], 128), Win(S["rk2"], 128), Win(proj, 256, COL["rv"] // 256),
                                Win(ret_decay, 128), S["ret_st"], dret_raw, False, side=side, into=dproj,
                                windows={2: COL["rv"]})
    landed()
    dproj, = row_call(
        "ret_pre_bwd", _ret_pre_bwd, [Row(proj, 512, COL["rq"] // 512), Row(proj, 512, COL["rk"] // 512), Row(cos), Row(sin),
                                      Row(drq2), Row(drk2)], [], [], tm=512, into=dproj,
        into_cols=[(512, COL["rq"]), (512, COL["rk"])])
    side, landed = _rider(hooks, "attn_bwd")
    dproj, dbias = attn_bwd(proj, *W["bias_tab"], dbr[0], dproj, side=side)
    landed()
    d_rel_bias = bias_table_grad(dbias)
    side, landed = _rider(hooks, "proj_wgrad", dict(branch=g_branch, out=g_out, up=g_up, down=g_down))
    g_in = mm("proj_wgrad", S["hb"], dproj, "tn", tm=1024, tn=1280, tk=2048, out_dtype=WIRE_DTYPE, side=side)
    landed()
    side, landed = _rider(hooks, "proj_bwd", {"in": g_in})
    dh = mm("proj_bwd", dproj, W["in"], "nt", tm=1024, tn=1024, tk=2560, extras=(dha,), epi=add, side=side)
    landed()
    big = dict(g_in=g_in, branch=g_branch, out=g_out, up=g_up, down=g_down)
    small = dict(rel_bias=d_rel_bias, wlr=dwlr[:GLA_RANK], blr=dblr, gnorm=dgnorm, ln1_g=dg1, ln1_b=db1, ln2_g=dg2, ln2_b=db2)
    return dh, big, small


def _reorder_normgate(ret_raw, gla_raw, rg, gg, gnorm):
    return f_normgate(ret_raw, rg, gla_raw, gg, gnorm)


def _gla_pre_bwd(gq, glr, dq2, dla, wlr, blr):
    _, pull = jax.vjp(f_gla_pre, gq, glr, wlr, blr)
    dgq, dglr, dwlr, dblr = pull((dq2, dla))
    dglr = jnp.concatenate([dglr, jnp.zeros((dglr.shape[0], PROJ_PAD - GLR_OFF - GLR_BLOCK), F32)], axis=1)
    return dgq, dglr, dwlr, dblr


def _ret_pre_bwd(rq, rk, cos, sin, drq2, drk2):
    _, pull = jax.vjp(lambda a, b: f_ret_pre(a, b, cos, sin), rq, rk)
    return pull((drq2, drk2))


SHARDED = ("in", "branch", "out", "up", "down")

_FULL = {
    "branch": lambda g: [g[:, n].reshape(D_MODEL, D_MODEL) for n in range(3)],
    "out": lambda g: g.reshape(D_MODEL, D_MODEL),
    "up": lambda g: jnp.transpose(g, (1, 0, 2)).reshape(D_MODEL, D_FF),
    "down": lambda g: g.reshape(D_FF, D_MODEL),
}


def _owner_pieces(s):
    shard = PROJ_WIDTH // N_DEV
    lo, hi = s * shard, (s + 1) * shard
    pieces = []
    for a, b, padded in ((0, GLR_SRC, 0), (GLR_SRC, GATES_SRC, GLR_OFF), (GATES_SRC, PROJ_WIDTH, GATES_OFF)):
        x0, x1 = max(lo, a), min(hi, b)
        if x0 < x1:
            pieces.append((x0 - lo, x1 - x0, padded + x0 - a))
    return pieces


def _full_in(g):
    placed = sorted((p, s, c, w) for s in range(N_DEV) for c, w, p in _owner_pieces(s))
    cols = [g[s][:, c:c + w] for _, s, c, w in placed]
    return jnp.concatenate(cols + [jnp.zeros((D_MODEL, PROJ_PAD - PROJ_WIDTH), g.dtype)], axis=1)


def _full_wlr(g):
    wlr = jnp.transpose(g, (1, 0, 2)).reshape(GLA_RANK, LIN_HEADS * LIN_KEY_DIM)
    return jnp.pad(wlr, ((0, GLR_BLOCK - GLA_RANK), (0, 0)))


def _slab_in(g):
    return jnp.stack([jnp.concatenate([g[:, p:p + w] for _, w, p in _owner_pieces(s)], axis=1) for s in range(N_DEV)])


_SLABS = {
    "in": _slab_in,
    "branch": lambda gs: jnp.stack([b.reshape(N_DEV, D_MODEL // N_DEV, D_MODEL) for b in gs], axis=1),
    "out": lambda g: g.reshape(N_DEV, D_MODEL // N_DEV, D_MODEL),
    "up": lambda g: g,
    "down": lambda g: g.reshape(N_DEV, D_FF // N_DEV, D_MODEL),
}


def _pair_sums(name, slabs, from_sibling=None):
    sums = []
    for mine, sibs in zip(slabs, from_sibling or carrier(name, Side("pair", slabs))):
        width = mine.shape[-1]
        rows = math.prod(mine.shape[1:-1])
        tm = 256 if rows % 256 == 0 else 128
        nb = rows // tm
        my_rows = Row(mine.reshape(-1, width), rowmap=lambda i, nb=nb: (2 * (i // nb) + lax.axis_index("c")) * nb + i % nb)
        out, = row_call("pair_sum", lambda q, p: (p.astype(F32) + q.astype(F32),),
                        [Row(sibs.reshape(-1, width)), my_rows], [], [(width, WIRE_DTYPE)], tm=tm)
        sums.append(out.reshape(sibs.shape))
    return sums


def _sum_adam(name, parts, w, m, v, layer, tm, earlier):
    C = w.shape[-1]
    rows = [Row(parts, lead=(s,)) for s in range(parts.shape[0])] + [Row(a, lead=(layer,)) for a in (w, m, v)]
    return row_call(name, f_sum_adam, rows, [], [(C, F32)] * 4, tm=tm, stack=(layer, DEPTH), stacked_onto=earlier)


def kernel(x, ln_in_g, ln_in_b, w_in, rel_bias, gla_w_lr, gla_b_lr, gla_norm_g, w_branch, w_out, ln1_g, ln1_b, w_up, w_down, ln2_g, ln2_b, loss_target, m_ln_in_g, m_ln_in_b, m_w_in, m_rel_bias, m_gla_w_lr, m_gla_b_lr, m_gla_norm_g, m_w_branch, m_w_out, m_ln1_g, m_ln1_b, m_w_up, m_w_down, m_ln2_g, m_ln2_b, v_ln_in_g, v_ln_in_b, v_w_in, v_rel_bias, v_gla_w_lr, v_gla_b_lr, v_gla_norm_g, v_w_branch, v_w_out, v_ln1_g, v_ln1_b, v_w_up, v_w_down, v_ln2_g, v_ln2_b):
    T = x.shape[1]
    x2d = x.reshape(T, D_MODEL)
    tgt = loss_target.reshape(T, D_MODEL)
    me = 4 * lax.axis_index("x") + 2 * lax.axis_index("y") + lax.axis_index("c")

    inv = ROPE_BASE ** (-jnp.arange(0, LIN_KEY_DIM, 2, dtype=F32) / LIN_KEY_DIM)
    ang = jnp.arange(T, dtype=F32)[:, None] * inv[None, :]
    cos = jnp.concatenate([jnp.cos(ang), jnp.cos(ang)], axis=1)
    sin = jnp.concatenate([-jnp.sin(ang), jnp.sin(ang)], axis=1)
    log_gamma = jnp.log1p(-jnp.exp2(-5.0 - jnp.arange(LIN_HEADS, dtype=F32)))
    ret_decay = jnp.repeat(log_gamma, LIN_KEY_DIM).reshape(1, LIN_HEADS * LIN_KEY_DIM)
    consts = (cos, sin, ret_decay)

    layers = [dict(blr=_row2(gla_b_lr[l]), gnorm=_row2(gla_norm_g[l]), ln1_g=_row2(ln1_g[l]),
                   ln1_b=_row2(ln1_b[l]), ln2_g=_row2(ln2_g[l]), ln2_b=_row2(ln2_b[l])) for l in range(DEPTH)]
    shards = dict(branch=w_branch, out=w_out, up=w_up, down=w_down)
    shards["in"] = w_in
    wire = [{n: shards[n][l].astype(WIRE_DTYPE) for n in SHARDED} for l in range(DEPTH)]

    def gather_of(l, names):
        def landed(res):
            for n, g in zip(names, res):
                layers[l][n] = _full_in(g) if n == "in" else _FULL[n](g)
        return lambda ctx: (Side("gather", [wire[l][n] for n in names]), landed)

    first = Side("gather", [wire[0]["in"]] + [gla_w_lr[l] for l in range(DEPTH)])
    bias_tabs = bias_table(rel_bias, side=first)
    layers[0]["in"] = _full_in(first.results[0])
    for l in range(DEPTH):
        layers[l]["wlr"] = _full_wlr(first.results[1 + l])
        layers[l]["bias_tab"] = (bias_tabs, l * ATTN_HEADS)

    fwd_hooks = [{"proj": gather_of(1, ("in",)), "attn_fwd": gather_of(0, ("up", "down")),
                  "ret_fwd": gather_of(0, ("branch", "out"))},
                 {"proj": gather_of(1, ("up", "down")), "attn_fwd": gather_of(1, ("branch", "out"))}]
    h, hb = row_call("ln_in", _twice(f_ln_in), [Row(x2d)], [_row2(ln_in_g), _row2(ln_in_b)], STREAM, tm=512)
    saved = []
    for l in range(DEPTH):
        h, hb, s = _layer_forward(h, hb, layers[l], consts, fwd_hooks[l])
        saved.append(s)
    dh, loss_lanes = row_call("loss", f_loss, [Row(h), Row(tgt)], [], [(D_MODEL, F32)], [(1, 128)], tm=512)

    received = [dict() for _ in range(DEPTH)]

    def scatter_of(l, names, sums=None):
        def make(ctx):
            parts = [sums[n] for n in names] if sums is not None else _pair_sums(
                "pair_grads", [_SLABS[n](ctx[n]) for n in names])

            def landed(res):
                received[l].update(zip(names, res))
            return Side("chips", parts), landed
        return make

    small = [None] * DEPTH
    dh, big, small[1] = _layer_backward(dh, saved[1], layers[1], consts, {})
    slabs1 = [_SLABS[n](big["g_in"] if n == "in" else big[n]) for n in SHARDED]
    sums1 = {}

    def pair_exchange_1(ctx):
        def landed(res):
            sums1.update(zip(SHARDED, _pair_sums("pair_grads", slabs1, from_sibling=res)))
        return Side("pair", slabs1), landed

    rest = SHARDED[1:]
    sums0 = {}

    def pair_exchange_0(ctx):
        slabs0 = [_SLABS[n](ctx[n]) for n in rest]

        def landed(res):
            sums0.update(zip(rest, _pair_sums("pair_grads", slabs0, from_sibling=res)))
        return Side("pair", slabs0), landed

    dh, big, small[0] = _layer_backward(dh, saved[0], layers[0], consts, {
        "mlp_down_bwd": pair_exchange_1, "gla_bwd": scatter_of(1, ("in",), sums1), "ret_bwd": pair_exchange_0,
        "attn_bwd": scatter_of(1, rest, sums1), "proj_wgrad": scatter_of(0, rest, sums0), "proj_bwd": scatter_of(0, ("in",))})
    grad_x, d_in_g, d_in_b = row_call("ln_in_bwd", vjp_fn(f_ln_in, 1, 1), [Row(x2d), Row(dh)],
                                      [_row2(ln_in_g), _row2(ln_in_b)], [(D_MODEL, F32)], [(1, D_MODEL), (1, D_MODEL)], tm=256)

    small_names = ("rel_bias", "wlr", "blr", "gnorm", "ln1_g", "ln1_b", "ln2_g", "ln2_b")
    parts = [d_in_g, d_in_b] + [small[l][n] for l in range(DEPTH) for n in small_names] + [loss_lanes]
    shapes = [(D_MODEL,), (D_MODEL,)] + [s for l in range(DEPTH) for s in (
        (ATTN_HEADS, N_REL), (GLA_RANK, 512), (512,), (LIN_VAL_DIM,), (D_MODEL,), (D_MODEL,), (D_MODEL,), (D_MODEL,))] + [(128,)]
    red = _unpack(all_reduce_small(_pack(parts)), shapes)
    loss = jnp.sum(red[-1])
    g_small = {"ln_in_g": red[0], "ln_in_b": red[1]}
    for l in range(DEPTH):
        for j, n in enumerate(small_names):
            g_small[(l, n)] = red[2 + l * len(small_names) + j]
    shard = LIN_HEADS * LIN_KEY_DIM // N_DEV
    g_wlr = jnp.stack([lax.dynamic_slice_in_dim(g_small[(l, "wlr")], me * shard, shard, axis=1) for l in range(DEPTH)])
    stack = lambda n: jnp.stack([g_small[(l, n)] for l in range(DEPTH)])
    g_rep = [g_small["ln_in_g"], g_small["ln_in_b"], stack("rel_bias"), g_wlr, stack("blr"), stack("gnorm"),
             stack("ln1_g"), stack("ln1_b"), stack("ln2_g"), stack("ln2_b")]
    w_rep = [ln_in_g, ln_in_b, rel_bias, gla_w_lr, gla_b_lr, gla_norm_g, ln1_g, ln1_b, ln2_g, ln2_b]
    m_rep = [m_ln_in_g, m_ln_in_b, m_rel_bias, m_gla_w_lr, m_gla_b_lr, m_gla_norm_g, m_ln1_g, m_ln1_b, m_ln2_g, m_ln2_b]
    v_rep = [v_ln_in_g, v_ln_in_b, v_rel_bias, v_gla_w_lr, v_gla_b_lr, v_gla_norm_g, v_ln1_g, v_ln1_b, v_ln2_g, v_ln2_b]
    rep_shapes = [w.shape for w in w_rep]
    d_pack, m_pack, v_pack = row_call("adam_small", f_adam, [Row(_pack(g_rep)), Row(_pack(w_rep)), Row(_pack(m_rep)), Row(_pack(v_rep))],
                                      [], [(128, F32)] * 3, tm=256)
    d_rep, nm_rep, nv_rep = _unpack(d_pack, rep_shapes), _unpack(m_pack, rep_shapes), _unpack(v_pack, rep_shapes)

    def owned(key, w, m, v, name, tm):
        C = w.shape[-1]
        flat = lambda a: a.reshape(DEPTH, -1, C)
        outs = None
        for l in range(DEPTH):
            outs = _sum_adam(name, received[l][key].reshape(4, -1, C), flat(w), flat(m), flat(v), l, tm, outs)
        return [o.reshape(w.shape) for o in outs]

    o_in = owned("in", w_in, m_w_in, v_w_in, "adam_w_in", 256)
    o_br = owned("branch", w_branch, m_w_branch, v_w_branch, "adam_w_branch", 128)
    o_out = owned("out", w_out, m_w_out, v_w_out, "adam_w_out", 128)
    o_up = owned("up", w_up, m_w_up, v_w_up, "adam_w_up", 256)
    o_down = owned("down", w_down, m_w_down, v_w_down, "adam_w_down", 256)

    def tree(j, rep):
        return [rep[0], rep[1], o_in[j], rep[2], rep[3], rep[4], rep[5], o_br[j], o_out[j], rep[6], rep[7], o_up[j], o_down[j],
                rep[8], rep[9]]

    return (loss, grad_x.reshape(1, T, D_MODEL), *tree(0, g_rep), *tree(1, d_rep), *tree(2, nm_rep), *tree(3, nv_rep))
```

```python
import functools
import math

import jax
import jax.numpy as jnp
from jax import lax
from jax.experimental import pallas as pl
from jax.experimental.pallas import tpu as pltpu

F32 = jnp.float32
MXU_DTYPE = jnp.bfloat16
WIRE_DTYPE = jnp.bfloat16
PROJ_DTYPE = jnp.bfloat16

N_DEV = 8
D_MODEL = 1024
DEPTH = 2
CHUNK = 64
ATTN_HEADS = 8
ATTN_HEAD_DIM = 128
ATTN_LEFT_CHUNKS = 8
BAND = (ATTN_LEFT_CHUNKS + 1) * CHUNK
PAD_ROWS = ATTN_LEFT_CHUNKS * CHUNK
REL_CLIP = 2 * CHUNK
N_REL = 2 * REL_CLIP + 1
REL_MIN = REL_CLIP - (CHUNK - 1)
LIN_HEADS = 4
LIN_KEY_DIM = 128
LIN_VAL_DIM = 256
GLA_RANK = 16
GLA_GATE_NORMALIZER = 16.0
ROPE_BASE = 10000.0
D_FF = 4 * D_MODEL
LN_EPS = 1e-5
NEG_INF = -1e30
ALPHA = (2 * DEPTH) ** 0.25
ADAM_LR, ADAM_B1, ADAM_B2, ADAM_EPS, ADAM_WD, ADAM_STEP = 0.001, 0.9, 0.999, 1e-08, 0.01, 10

PROJ_WIDTH = 12304
GLR_SRC = 9216
GATES_SRC = GLR_SRC + GLA_RANK
MAIN_W = 9216
GATES_W = 3 * D_MODEL
GATES_OFF = MAIN_W
GLR_OFF = MAIN_W + GATES_W
PROJ_PAD = 12800
GLR_BLOCK = 128
COL = dict(aq=0, ak=1024, av=2048, rq=3072, rk=3584, rv=4096, rg=5120, gq=6144, gk=6656, gv=7168, gg=8192)

SEG_CHUNKS = 8
SEG = SEG_CHUNKS * CHUNK
VMEM_BIG = 56 * 1024 * 1024

_NN = (((1,), (0,)), ((), ()))
_NT = (((1,), (1,)), ((), ()))
_TN = (((0,), (0,)), ((), ()))


def _mxu(a, b, dn):
    return lax.dot_general(a.astype(MXU_DTYPE), b.astype(MXU_DTYPE), dn, preferred_element_type=F32)


@jax.custom_vjp
def dot_nn(a, b):
    return _mxu(a, b, _NN)


def _dot_nn_fwd(a, b):
    return _mxu(a, b, _NN), (a, b)


def _dot_nn_bwd(res, g):
    a, b = res
    return _mxu(g, b, _NT).astype(a.dtype), _mxu(a, g, _TN).astype(b.dtype)


dot_nn.defvjp(_dot_nn_fwd, _dot_nn_bwd)


@jax.custom_vjp
def dot_nt(a, b):
    return _mxu(a, b, _NT)


def _dot_nt_fwd(a, b):
    return _mxu(a, b, _NT), (a, b)


def _dot_nt_bwd(res, g):
    a, b = res
    return _mxu(g, b, _NN).astype(a.dtype), _mxu(g, a, _TN).astype(b.dtype)


dot_nt.defvjp(_dot_nt_fwd, _dot_nt_bwd)


@jax.custom_vjp
def dot_tn(a, b):
    return _mxu(a, b, _TN)


def _dot_tn_fwd(a, b):
    return _mxu(a, b, _TN), (a, b)


def _dot_tn_bwd(res, g):
    a, b = res
    return _mxu(b, g, _NT).astype(a.dtype), _mxu(a, g, _NN).astype(b.dtype)


dot_tn.defvjp(_dot_tn_fwd, _dot_tn_bwd)


def _tri(n, lower):
    r = lax.broadcasted_iota(jnp.int32, (n, n), 0)
    c = lax.broadcasted_iota(jnp.int32, (n, n), 1)
    return (r >= c) if lower else (r <= c)


def _tri_dot(lower, x):
    tri = _tri(x.shape[0], lower).astype(jnp.bfloat16)
    hi = x.astype(jnp.bfloat16)
    rest = x - hi.astype(F32)
    mid = rest.astype(jnp.bfloat16)
    lo = (rest - mid.astype(F32)).astype(jnp.bfloat16)
    dot = lambda part: lax.dot_general(tri, part, _NN, preferred_element_type=F32)
    return dot(hi) + dot(mid) + dot(lo)


@jax.custom_vjp
def chunk_cumsum(x):
    return _tri_dot(True, x)


def _chunk_cumsum_fwd(x):
    return chunk_cumsum(x), None


def _chunk_cumsum_bwd(_, g):
    return (_tri_dot(False, g),)


chunk_cumsum.defvjp(_chunk_cumsum_fwd, _chunk_cumsum_bwd)


@jax.custom_vjp
def swap_halves(t):
    return pltpu.roll(t, t.shape[-1] // 2, t.ndim - 1)


def _swap_fwd(t):
    return swap_halves(t), None


def _swap_bwd(_, g):
    return (swap_halves(g),)


swap_halves.defvjp(_swap_fwd, _swap_bwd)


def mm(name, a, b, dims, *, tm, tn, tk, out_dtype=F32, extras=(), row_params=(), epi=None, n_out=1, out_dtypes=None,
       side=None, column_slabs=False):
    if dims == "nn":
        (M, K), N = a.shape, b.shape[1]
    elif dims == "nt":
        (M, K), N = a.shape, b.shape[0]
    else:
        (K, M), N = a.shape, b.shape[1]
    tm, tn, tk = min(tm, M), min(tn, N), min(tk, K)
    assert M % tm == 0 and N % tn == 0 and K % tk == 0, (name, M, N, K, tm, tn, tk)
    nk = K // tk
    dn = {"nn": _NN, "nt": _NT, "tn": _TN}[dims]
    a_spec = pl.BlockSpec((tk, tm), lambda i, j, k: (k, i)) if dims == "tn" else pl.BlockSpec((tm, tk), lambda i, j, k: (i, k))
    b_spec = pl.BlockSpec((tn, tk), lambda i, j, k: (j, k)) if dims == "nt" else pl.BlockSpec((tk, tn), lambda i, j, k: (k, j))
    o_spec = pl.BlockSpec((tm, tn), lambda i, j, k: (i, j))
    row_spec = pl.BlockSpec((1, tn), lambda i, j, k: (0, j))
    n_ex = len(extras) + len(row_params)
    out_dtypes = out_dtypes or (out_dtype,) * n_out

    def finish(r, ex_refs, o_refs):
        if epi is not None:
            r = epi(r, *[e[...] for e in ex_refs])
        r = r if isinstance(r, tuple) else (r,)
        for o, v in zip(o_refs, r):
            o[...] = v.astype(o.dtype)

    def body(*refs):
        a_ref, b_ref = refs[:2]
        ex_refs = refs[2:2 + n_ex]
        o_refs = refs[2 + n_ex:2 + n_ex + n_out]
        part = _mxu(a_ref[...], b_ref[...], dn)
        if nk == 1:
            finish(part, ex_refs, o_refs)
            return
        acc = refs[-1]
        k = pl.program_id(2)

        @pl.when(k == 0)
        def _():
            acc[...] = part

        @pl.when(k > 0)
        def _():
            acc[...] += part

        @pl.when(k == nk - 1)
        def _():
            finish(acc[...], ex_refs, o_refs)

    outs = _side_call(
        side, body,
        name=name,
        grid=(M // tm, N // tn, nk),
        in_specs=[a_spec, b_spec] + [o_spec] * len(extras) + [row_spec] * len(row_params),
        out_specs=[pl.BlockSpec((None, tm, tn), lambda i, j, k: (j, i, 0))] * n_out if column_slabs else [o_spec] * n_out,
        out_shape=[jax.ShapeDtypeStruct((N // tn, M, tn) if column_slabs else (M, N), dt) for dt in out_dtypes],
        scratch_shapes=[pltpu.VMEM((tm, tn), F32)] if nk > 1 else [],
        args=[a, b, *extras, *row_params],
        semantics=("parallel", "parallel", "arbitrary"),
    )
    return outs[0] if n_out == 1 else outs


class Row:
    def __init__(self, arr, width=None, col=0, lead=(), rowmap=None):
        self.arr, self.lead, self.col = arr, tuple(lead), col
        self.width = arr.shape[-1] if width is None else width
        self.rowmap = rowmap

    def spec(self, tm):
        lead, col, rowmap = self.lead, self.col, self.rowmap or (lambda i: i)
        return pl.BlockSpec((None,) * len(lead) + (tm, self.width), lambda i: (*lead, rowmap(i), col))


def row_call(name, fn, rows, params, new_outs, red_shapes=(), *, tm, into=None, into_cols=(), stack=None, stacked_onto=None):
    R = rows[0].arr.shape[len(rows[0].lead)]
    tm = min(tm, R)
    assert R % tm == 0, (name, R, tm)
    n_row, n_par, n_new, n_into, n_red = len(rows), len(params), len(new_outs), len(into_cols), len(red_shapes)
    fresh = isinstance(into, jax.ShapeDtypeStruct)

    def body(*refs):
        row_refs = refs[:n_row]
        par_refs = refs[n_row:n_row + n_par]
        pos = n_row + n_par + (1 if (into is not None and not fresh) else 0) + (n_new if stacked_onto is not None else 0)
        new_refs = refs[pos:pos + n_new]
        red_refs = refs[pos + n_new:pos + n_new + n_red]
        pos2 = pos + n_new + n_red
        into_ref = refs[pos2] if into is not None else None
        stage = refs[pos2 + 1:pos2 + 1 + n_into] if into is not None else ()
        i = pl.program_id(0)
        widen = lambda t: t.astype(F32) if jnp.issubdtype(t.dtype, jnp.floating) else t
        vals = fn(*[widen(r[...]) for r in row_refs], *[p[...] for p in par_refs])
        vals = vals if isinstance(vals, (tuple, list)) else (vals,)
        for o, v in zip(new_refs, vals[:n_new]):
            o[...] = v.astype(o.dtype)

        def window_copy(j, step):
            width, off = into_cols[j]
            rows_at = pl.ds(pl.multiple_of(step * tm, tm), tm)
            return pltpu.make_async_copy(stage[j], into_ref.at[rows_at, pl.ds(off, width)], refs[pos2 + 1 + n_into].at[j])

        for j, v in enumerate(vals[n_new:n_new + n_into]):
            @pl.when(i > 0)
            def _(j=j):
                window_copy(j, i - 1).wait()

            stage[j][...] = v.astype(stage[j].dtype)
            window_copy(j, i).start()

            @pl.when(i == R // tm - 1)
            def _(j=j):
                window_copy(j, i).wait()
        for o, v in zip(red_refs, vals[n_new + n_into:]):
            @pl.when(i == 0)
            def _(o=o, v=v):
                o[...] = v.astype(o.dtype)

            @pl.when(i > 0)
            def _(o=o, v=v):
                o[...] += v.astype(o.dtype)

    in_specs = [r.spec(tm) for r in rows] + [pl.BlockSpec(p.shape, lambda i, nd=p.ndim: (0,) * nd) for p in params]
    args = [r.arr for r in rows] + list(params)
    if stack is None:
        out_specs = [pl.BlockSpec((tm, w), lambda i: (i, 0)) for w, _ in new_outs]
        out_shape = [jax.ShapeDtypeStruct((R, w), dt) for w, dt in new_outs]
    else:
        index, depth = stack
        out_specs = [pl.BlockSpec((None, tm, w), lambda i: (index, i, 0)) for w, _ in new_outs]
        out_shape = [jax.ShapeDtypeStruct((depth, R, w), dt) for w, dt in new_outs]
    out_specs += [pl.BlockSpec(s, lambda i, nd=len(s): (0,) * nd) for s in red_shapes]
    out_shape += [jax.ShapeDtypeStruct(s, F32) for s in red_shapes]
    aliases, scratch = {}, []
    if into is not None:
        out_specs.append(pl.BlockSpec(memory_space=pl.ANY))
        out_shape.append(jax.ShapeDtypeStruct(into.shape, into.dtype))
        if not fresh:
            in_specs.append(pl.BlockSpec(memory_space=pl.ANY))
            args.append(into)
            aliases = {len(args) - 1: len(out_shape) - 1}
        scratch = [pltpu.VMEM((tm, w), into.dtype) for w, _ in into_cols] + [pltpu.SemaphoreType.DMA((n_into,))]
    if stacked_onto is not None:
        for j, prev in enumerate(stacked_onto):
            in_specs.append(pl.BlockSpec(memory_space=pl.ANY))
            args.append(prev)
            aliases[len(args) - 1] = j
    outs = pl.pallas_call(
        body,
        name=name,
        grid=(R // tm,),
        in_specs=in_specs,
        out_specs=out_specs,
        out_shape=out_shape,
        scratch_shapes=scratch,
        input_output_aliases=aliases,
        compiler_params=pltpu.CompilerParams(dimension_semantics=("arbitrary",), vmem_limit_bytes=VMEM_BIG),
    )(*args)
    return list(outs)


def vjp_fn(fn, n_row, n_out, passthrough=0):
    def bwd(*args):
        rows = [a.astype(F32) for a in args[:n_row]]
        cts = args[n_row:n_row + n_out]
        thru = args[n_row + n_out:n_row + n_out + passthrough]
        pars = args[n_row + n_out + passthrough:]
        outs, pull = jax.vjp(lambda *xs: tuple(fn(*xs)), *rows, *pars)
        grads = pull(tuple(c.astype(o.dtype) for c, o in zip(cts, outs)))
        return tuple(grads[:n_row]) + tuple(thru) + tuple(grads[n_row:])

    return bwd


def _ln(x, g, b):
    mu = jnp.mean(x, axis=-1, keepdims=True)
    xc = x - mu
    var = jnp.mean(xc * xc, axis=-1, keepdims=True)
    return xc * lax.rsqrt(var + LN_EPS) * g + b


def f_ln_in(x, g, b):
    return (_ln(x, g, b),)


def f_deepnorm(x, y, g, b):
    return (_ln(ALPHA * x + y, g, b),)


def _twice(fn):
    return lambda *args: fn(*args) * 2


STREAM = [(D_MODEL, F32), (D_MODEL, MXU_DTYPE)]


def _rope(t, cos, sin):
    parts = []
    for h in range(LIN_HEADS):
        th = t[:, h * LIN_KEY_DIM:(h + 1) * LIN_KEY_DIM]
        parts.append(th * cos + swap_halves(th) * sin)
    return jnp.concatenate(parts, axis=1)


def f_ret_pre(rq, rk, cos, sin):
    return _rope(rq, cos, sin), _rope(rk, cos, sin) * LIN_KEY_DIM ** -0.5


def _log_sigmoid(x):
    return jnp.minimum(x, 0.0) - jnp.log(1.0 + jnp.exp(-jnp.abs(x)))


def f_gla_pre(gq, glr, wlr, blr):
    gate_pre = dot_nn(glr, wlr) + blr
    return gq * LIN_KEY_DIM ** -0.5, _log_sigmoid(gate_pre) / GLA_GATE_NORMALIZER


def f_normgate(ret_raw, rg, gla_raw, gg, gnorm):
    rets, glas = [], []
    for h in range(LIN_HEADS):
        sl = slice(h * LIN_VAL_DIM, (h + 1) * LIN_VAL_DIM)
        t = ret_raw[:, sl]
        mu = jnp.mean(t, axis=-1, keepdims=True)
        tc = t - mu
        rets.append(tc * lax.rsqrt(jnp.mean(tc * tc, axis=-1, keepdims=True) + LN_EPS))
        u = gla_raw[:, sl]
        glas.append(u * lax.rsqrt(jnp.mean(u * u, axis=-1, keepdims=True) + LN_EPS) * gnorm)
    ret = jax.nn.silu(rg) * jnp.concatenate(rets, axis=1)
    gla = jax.nn.silu(gg) * jnp.concatenate(glas, axis=1)
    return ret, gla


def f_merge(p0, p1, p2, gl):
    out = jax.nn.sigmoid(gl[:, :D_MODEL]) * p0
    out = out + jax.nn.sigmoid(gl[:, D_MODEL:2 * D_MODEL]) * p1
    return (out + jax.nn.sigmoid(gl[:, 2 * D_MODEL:]) * p2,)


def f_loss(y, tgt):
    diff = y - tgt
    sq = jnp.sum(diff * diff, axis=0, keepdims=True)
    lanes = sq[:, :128]
    for j in range(1, D_MODEL // 128):
        lanes = lanes + sq[:, j * 128:(j + 1) * 128]
    return diff * (1.0 / D_MODEL), lanes * (0.5 / D_MODEL)


def _adam(g, w, m, v):
    m = ADAM_B1 * m + (1.0 - ADAM_B1) * g
    v = ADAM_B2 * v + (1.0 - ADAM_B2) * jnp.square(g)
    m_hat = m / (1.0 - ADAM_B1 ** ADAM_STEP)
    v_hat = v / (1.0 - ADAM_B2 ** ADAM_STEP)
    delta = -ADAM_LR * (m_hat / (jnp.sqrt(v_hat) + ADAM_EPS) + ADAM_WD * w)
    return delta, m, v


def f_sum_adam(*args):
    g = args[0].astype(F32)
    for p in args[1:-3]:
        g = g + p.astype(F32)
    return (g,) + _adam(g, *args[-3:])


def f_adam(g, w, m, v):
    return _adam(g, w, m, v)


GROUP = 4
GROUP_ROWS = GROUP * CHUNK
GROUP_KEYS = (GROUP + ATTN_LEFT_CHUNKS) * CHUNK
DIAG_ROWS = REL_CLIP + CHUNK
GRAD_ROWS = 200


def _blocks_by_delta():
    blocks = {}
    for qi in range(GROUP):
        for J in range(GROUP_KEYS // 128):
            blocks.setdefault(qi - 2 * J, []).append((qi, J))
    return blocks


def bias_table(rel_bias, side=None):
    n_tab = rel_bias.shape[0] * ATTN_HEADS

    def body(rb_ref, o_ref):
        h = pl.program_id(0)
        rm = lax.broadcasted_iota(jnp.int32, (CHUNK, 128), 0)
        cm = lax.broadcasted_iota(jnp.int32, (CHUNK, 128), 1)
        ahead = jnp.where(cm >= CHUNK, 1, 0)
        for delta, where in _blocks_by_delta().items():
            base = 64 * delta + PAD_ROWS
            rel = jnp.minimum(base + rm - cm, REL_CLIP) + REL_CLIP
            in_band = (ahead - delta >= 0) & (ahead - delta <= ATTN_LEFT_CHUNKS)
            lo = max(REL_MIN, min(base - 127, REL_CLIP) + REL_CLIP)
            hi = min(base + CHUNK - 1, REL_CLIP) + REL_CLIP

            def step(r, acc, rel=rel):
                return jnp.where(rel == r, rb_ref[h * N_REL + r], acc)

            blk = jnp.zeros((CHUNK, 128), F32) if hi < lo else lax.fori_loop(lo, hi + 1, step, jnp.zeros((CHUNK, 128), F32))
            blk = jnp.where(in_band, blk, NEG_INF)
            for qi, J in where:
                o_ref[pl.ds(qi * CHUNK, CHUNK), pl.ds(J * 128, 128)] = blk

    return _side_call(
        side, body,
        name="bias_table",
        grid=(n_tab,),
        in_specs=[pl.BlockSpec(memory_space=pltpu.SMEM)],
        out_specs=[pl.BlockSpec((None, GROUP_ROWS, GROUP_KEYS), lambda h: (h, 0, 0))],
        out_shape=[jax.ShapeDtypeStruct((n_tab, GROUP_ROWS, GROUP_KEYS), F32)],
        scratch_shapes=[],
        args=[rel_bias.reshape(n_tab * N_REL)],
    )[0]


def bias_table_grad(dbias):
    blocks = _blocks_by_delta()
    below_clip = range(-(CHUNK - 1), REL_CLIP)
    deltas = [d for d in sorted(blocks) if any(-127 <= pos - 64 * d - PAD_ROWS <= CHUNK - 1 for pos in below_clip)]

    def body(d_ref, o_ref, diag):
        sums = []
        for delta in deltas:
            acc = None
            for qi, J in blocks[delta]:
                blk = d_ref[pl.ds(qi * CHUNK, CHUNK), pl.ds(J * 128, 128)]
                acc = blk if acc is None else acc + blk
            sums.append(acc)
        off = lax.broadcasted_iota(jnp.int32, (CHUNK, 128), 0) - lax.broadcasted_iota(jnp.int32, (CHUNK, 128), 1)
        diag[...] = jnp.zeros_like(diag)

        def step(t, carry):
            pos = t - (CHUNK - 1)
            picked = jnp.zeros((CHUNK, 128), F32)
            for delta, s in zip(deltas, sums):
                picked = picked + jnp.where(off == pos - 64 * delta - PAD_ROWS, s, 0.0)
            diag[pl.ds(t, 1), :] = jnp.sum(picked, axis=0, keepdims=True)
            return carry

        lax.fori_loop(0, DIAG_ROWS, step, 0, unroll=8)
        d = d_ref[...]
        total = jnp.sum(jnp.sum(d, axis=1, keepdims=True), axis=0, keepdims=True)
        o_ref[pl.ds(0, DIAG_ROWS), :] = jnp.broadcast_to(jnp.sum(diag[...], axis=1, keepdims=True), (DIAG_ROWS, 128))
        o_ref[pl.ds(DIAG_ROWS, GRAD_ROWS - DIAG_ROWS), :] = jnp.broadcast_to(total, (GRAD_ROWS - DIAG_ROWS, 128))

    out = pl.pallas_call(
        body,
        name="bias_table_grad",
        grid=(ATTN_HEADS,),
        in_specs=[pl.BlockSpec((None, GROUP_ROWS, GROUP_KEYS), lambda h: (h, 0, 0))],
        out_specs=pl.BlockSpec((None, GRAD_ROWS, 128), lambda h: (h, 0, 0)),
        out_shape=jax.ShapeDtypeStruct((ATTN_HEADS, GRAD_ROWS, 128), F32),
        scratch_shapes=[pltpu.VMEM((DIAG_ROWS, 128), F32)],
        compiler_params=pltpu.CompilerParams(dimension_semantics=("arbitrary",)),
    )(dbias)
    below = out[:, :DIAG_ROWS - 1, 0]
    clipped = out[:, DIAG_ROWS, 0] - jnp.sum(below, axis=1)
    return jnp.concatenate([jnp.zeros((ATTN_HEADS, REL_MIN), F32), below, clipped[:, None]], axis=1)


def _attn_stage(q_ref, k_ref, v_ref, qs, kp, vp, T):
    qs[...] = (q_ref[...] * ATTN_HEAD_DIM ** -0.5).astype(qs.dtype)
    for dst, src in ((kp, k_ref), (vp, v_ref)):
        dst[pl.ds(0, PAD_ROWS), :] = jnp.zeros((PAD_ROWS, ATTN_HEAD_DIM), dst.dtype)
        dst[pl.ds(PAD_ROWS, T), :] = src[...].astype(dst.dtype)


def _attn_probs(g, first_groups, qs, kp, b_ref):
    r0 = g * GROUP_ROWS if first_groups else pl.multiple_of(g * GROUP_ROWS, GROUP_ROWS)
    qc = qs[pl.ds(r0, GROUP_ROWS), :]
    kb = kp[pl.ds(r0, GROUP_KEYS), :]
    s = _mxu(qc, kb, _NT) + b_ref[...]
    if first_groups:
        k_chunk = lax.shift_right_logical(lax.broadcasted_iota(jnp.int32, (GROUP_ROWS, GROUP_KEYS), 1), 6)
        s = jnp.where(g * GROUP + k_chunk >= ATTN_LEFT_CHUNKS, s, NEG_INF)
    m = jnp.max(s, axis=-1, keepdims=True)
    e = jnp.exp(s - m)
    l = jnp.sum(e, axis=-1, keepdims=True)
    return r0, e * (1.0 / l), m + jnp.log(l)


def _for_groups(T, step):
    n_first = min(ATTN_LEFT_CHUNKS // GROUP, T // GROUP_ROWS)
    for g in range(n_first):
        step(g, True)
    n_loop = T // GROUP_ROWS - n_first
    lax.fori_loop(n_first, T // GROUP_ROWS, lambda g, c: (step(g, False), c)[1], 0, unroll=2 if n_loop % 2 == 0 else 1)


def _head_spec(T, base):
    return pl.BlockSpec((T, ATTN_HEAD_DIM), lambda h: (0, base + h))


def _attn_scratch(T):
    return [pltpu.VMEM((T, ATTN_HEAD_DIM), MXU_DTYPE)] + [pltpu.VMEM((T + PAD_ROWS, ATTN_HEAD_DIM), MXU_DTYPE)] * 2


def attn_fwd(proj, bias_tab, first_tab, side=None):
    T = proj.shape[0]

    def body(q_ref, k_ref, v_ref, b_ref, o_ref, lse_ref, qs, kp, vp):
        _attn_stage(q_ref, k_ref, v_ref, qs, kp, vp, T)

        def step(g, first):
            r0, p, lse = _attn_probs(g, first, qs, kp, b_ref)
            o_ref[pl.ds(r0, GROUP_ROWS), :] = _mxu(p, vp[pl.ds(r0, GROUP_KEYS), :], _NN).astype(o_ref.dtype)
            lse_ref[pl.ds(r0, GROUP_ROWS), :] = jnp.broadcast_to(lse, (GROUP_ROWS, ATTN_HEAD_DIM))

        _for_groups(T, step)

    out, lse = _side_call(
        side, body,
        name="attn_fwd",
        grid=(ATTN_HEADS,),
        in_specs=[_head_spec(T, COL["aq"] // 128), _head_spec(T, COL["ak"] // 128), _head_spec(T, COL["av"] // 128),
                  pl.BlockSpec((None, GROUP_ROWS, GROUP_KEYS), lambda h: (first_tab + h, 0, 0))],
        out_specs=[pl.BlockSpec((T, ATTN_HEAD_DIM), lambda h: (0, h))] * 2,
        out_shape=[jax.ShapeDtypeStruct((T, ATTN_HEADS * ATTN_HEAD_DIM), MXU_DTYPE),
                   jax.ShapeDtypeStruct((T, ATTN_HEADS * ATTN_HEAD_DIM), F32)],
        scratch_shapes=_attn_scratch(T),
        args=[proj, proj, proj, bias_tab],
    )
    return out, lse


STRIP = 256


def attn_bwd(proj, bias_tab, first_tab, out, lse, dout, dproj, side=None):
    T = proj.shape[0]

    def body(q_ref, k_ref, v_ref, b_ref, do_ref, o_ref, lse_ref, dproj_in, dbias_ref, dproj_ref, qs, kp, vp, dkp, dvp, dq, stage):
        del dproj_in
        h = pl.program_id(0)
        _attn_stage(q_ref, k_ref, v_ref, qs, kp, vp, T)
        dkp[...] = jnp.zeros_like(dkp)
        dvp[...] = jnp.zeros_like(dvp)
        dbias_ref[...] = jnp.zeros_like(dbias_ref)

        def step(g, first):
            r0 = g * GROUP_ROWS if first else pl.multiple_of(g * GROUP_ROWS, GROUP_ROWS)
            rows = pl.ds(r0, GROUP_ROWS)
            qc, do = qs[rows, :], do_ref[rows, :]
            lse = jnp.concatenate([lse_ref[rows, :]] * (STRIP // ATTN_HEAD_DIM), axis=1)
            row_term = jnp.sum(do.astype(F32) * o_ref[rows, :].astype(F32), axis=-1, keepdims=True)
            dq_acc = jnp.zeros((GROUP_ROWS, ATTN_HEAD_DIM), F32)
            for t in range(GROUP_KEYS // STRIP):
                keys = pl.ds(r0 + t * STRIP if first else pl.multiple_of(r0 + t * STRIP, STRIP), STRIP)
                cols = slice(t * STRIP, (t + 1) * STRIP)
                kb, vb = kp[keys, :], vp[keys, :]
                s = _mxu(qc, kb, _NT) + b_ref[:, cols]
                if first:
                    k_chunk = lax.shift_right_logical(lax.broadcasted_iota(jnp.int32, (GROUP_ROWS, STRIP), 1), 6)
                    s = jnp.where(g * GROUP + t * (STRIP // CHUNK) + k_chunk >= ATTN_LEFT_CHUNKS, s, NEG_INF)
                p = jnp.exp(s - lse)
                dvp[keys, :] += _mxu(p, do, _TN)
                ds = p * (_mxu(do, vb, _NT) - row_term)
                dq_acc = dq_acc + _mxu(ds, kb, _NN)
                dkp[keys, :] += _mxu(ds, qc, _TN)
                dbias_ref[:, cols] += ds
            dq[rows, :] = dq_acc * ATTN_HEAD_DIM ** -0.5

        _for_groups(T, step)
        col = pl.multiple_of(h * ATTN_HEAD_DIM, ATTN_HEAD_DIM)
        for grad, first_row, base in ((dq, 0, "aq"), (dkp, PAD_ROWS, "ak"), (dvp, PAD_ROWS, "av")):
            stage[...] = grad[pl.ds(first_row, T), :].astype(stage.dtype)
            pltpu.sync_copy(stage, dproj_ref.at[:, pl.ds(COL[base] + col, ATTN_HEAD_DIM)])

    dbias, dproj = _side_call(
        side, body,
        name="attn_bwd",
        grid=(ATTN_HEADS,),
        in_specs=[_head_spec(T, COL["aq"] // 128), _head_spec(T, COL["ak"] // 128), _head_spec(T, COL["av"] // 128),
                  pl.BlockSpec((None, GROUP_ROWS, GROUP_KEYS), lambda h: (first_tab + h, 0, 0))]
        + [pl.BlockSpec((T, ATTN_HEAD_DIM), lambda h: (0, h))] * 3 + [pl.BlockSpec(memory_space=pl.ANY)],
        out_specs=[pl.BlockSpec((None, GROUP_ROWS, GROUP_KEYS), lambda h: (h, 0, 0)), pl.BlockSpec(memory_space=pl.ANY)],
        out_shape=[jax.ShapeDtypeStruct((ATTN_HEADS, GROUP_ROWS, GROUP_KEYS), F32),
                   jax.ShapeDtypeStruct(dproj.shape, dproj.dtype)],
        scratch_shapes=_attn_scratch(T) + [pltpu.VMEM((T + PAD_ROWS, ATTN_HEAD_DIM), F32)] * 2
        + [pltpu.VMEM((T, ATTN_HEAD_DIM), F32), pltpu.VMEM((T, ATTN_HEAD_DIM), dproj.dtype)],
        args=[proj, proj, proj, bias_tab, dout, out, lse, dproj],
        aliases={7: 1},
    )
    return dproj, dbias


def _lin_segment(q, k, v, la, st):
    lower = _tri(CHUNK, True)
    steps = (lax.broadcasted_iota(jnp.int32, (CHUNK, 1), 0) + 1).astype(F32)
    outs = []
    for i in range(SEG_CHUNKS):
        sl = slice(i * CHUNK, (i + 1) * CHUNK)
        qi, ki, vi = q[sl], k[sl], v[sl]
        if la.shape[0] == 1:
            cum, last = la * steps, la * float(CHUNK)
        else:
            cum, last = chunk_cumsum(la[sl]), jnp.sum(la[sl], axis=0, keepdims=True)
        e_pos, e_neg = jnp.exp(cum), jnp.exp(-cum)
        q_fwd = qi * e_pos
        s_fwd = dot_nt(q_fwd, ki * e_neg)
        s_bwd = dot_nt(qi * e_neg, ki * e_pos)
        o = dot_nn(jnp.where(lower, s_fwd, s_bwd), vi) + dot_nt(q_fwd, st)
        st = jnp.exp(last) * st + dot_tn(vi, ki * jnp.exp(last - cum))
        outs.append(o)
    return jnp.concatenate(outs, axis=0), st


class Win:
    def __init__(self, arr, width, base=0):
        self.arr, self.width, self.base = arr, width, base


HEADS_PER_STEP = 2


def _lin_spec(w, seg_of):
    assert w.base % HEADS_PER_STEP == 0
    first, width = w.base // HEADS_PER_STEP, HEADS_PER_STEP * w.width
    if w.arr.shape[0] == 1:
        return pl.BlockSpec((1, width), lambda p, s: (0, first + p))
    return pl.BlockSpec((SEG, width), lambda p, s: (seg_of(s), first + p))


def _head_cols(ref, j, width):
    return ref[:, j * width:(j + 1) * width]


def _decay_rows(la_ref, j):
    return _head_cols(la_ref, j, LIN_KEY_DIM)


_STATE_BLOCK = (HEADS_PER_STEP, None, LIN_VAL_DIM, LIN_KEY_DIM)


def lin_fwd(name, q, k, v, la, side=None):
    T = q.arr.shape[0]
    nseg = T // SEG

    def body(q_ref, k_ref, v_ref, la_ref, o_ref, st_ref, st):
        @pl.when(pl.program_id(1) == 0)
        def _():
            st[...] = jnp.zeros_like(st)

        for j in range(HEADS_PER_STEP):
            st0 = st[j]
            st_ref[j] = st0
            o, st1 = _lin_segment(_head_cols(q_ref, j, LIN_KEY_DIM), _head_cols(k_ref, j, LIN_KEY_DIM),
                                  _head_cols(v_ref, j, LIN_VAL_DIM), _decay_rows(la_ref, j), st0)
            o_ref[:, j * LIN_VAL_DIM:(j + 1) * LIN_VAL_DIM] = o
            st[j] = st1

    fwd = lambda s: s
    out, states = _side_call(
        side, body,
        name=name,
        grid=(LIN_HEADS // HEADS_PER_STEP, nseg),
        in_specs=[_lin_spec(w, fwd) for w in (q, k, v, la)],
        out_specs=[pl.BlockSpec((SEG, HEADS_PER_STEP * LIN_VAL_DIM), lambda p, s: (s, p)),
                   pl.BlockSpec(_STATE_BLOCK, lambda p, s: (p, s, 0, 0))],
        out_shape=[jax.ShapeDtypeStruct((T, LIN_HEADS * LIN_VAL_DIM), F32),
                   jax.ShapeDtypeStruct((LIN_HEADS, nseg, LIN_VAL_DIM, LIN_KEY_DIM), F32)],
        scratch_shapes=[pltpu.VMEM((HEADS_PER_STEP, LIN_VAL_DIM, LIN_KEY_DIM), F32)],
        args=[q.arr, k.arr, v.arr, la.arr],
    )
    return out, states


def lin_bwd(name, q, k, v, la, states, dout, with_decay_grad, side=None, into=None, windows=None):
    T = q.arr.shape[0]
    nseg = T // SEG
    n_out = 4 if with_decay_grad else 3

    widths = [LIN_KEY_DIM, LIN_KEY_DIM, LIN_VAL_DIM] + ([LIN_KEY_DIM] if with_decay_grad else [])
    windows = dict(windows or {})
    fresh = [i for i in range(n_out) if i not in windows]
    placed = sorted(windows)
    n_pairs = LIN_HEADS // HEADS_PER_STEP
    rev = lambda s: nseg - 1 - s

    def body(q_ref, k_ref, v_ref, la_ref, st_ref, do_ref, *rest):
        rest = rest[1:] if placed else rest
        outs = dict(zip(fresh, rest[:len(fresh)]))
        rest = rest[len(fresh):]
        into_ref, rest = (rest[0], rest[1:]) if placed else (None, rest)
        dst, stages = rest[0], dict(zip(placed, rest[1:1 + len(placed)]))
        p, s = pl.program_id(0), pl.program_id(1)

        @pl.when(s == 0)
        def _():
            dst[...] = jnp.zeros_like(dst)

        def window_copy(n, i):
            width = HEADS_PER_STEP * widths[i]
            rows = pl.ds(pl.multiple_of(rev(s) * SEG, SEG), SEG)
            cols = pl.ds(pl.multiple_of(windows[i] + p * width, width), width)
            return pltpu.make_async_copy(stages[i], into_ref.at[rows, cols], rest[1 + len(placed)].at[n])

        for n, i in enumerate(placed):
            @pl.when((p > 0) | (s > 0))
            def _(n=n, i=i):
                window_copy(n, i).wait()

        for j in range(HEADS_PER_STEP):
            _, pull = jax.vjp(_lin_segment, _head_cols(q_ref, j, LIN_KEY_DIM).astype(F32),
                              _head_cols(k_ref, j, LIN_KEY_DIM).astype(F32), _head_cols(v_ref, j, LIN_VAL_DIM).astype(F32),
                              _decay_rows(la_ref, j), st_ref[j])
            grads = pull((_head_cols(do_ref, j, LIN_VAL_DIM), dst[j]))
            for i in range(n_out):
                dest = stages[i] if i in windows else outs[i]
                dest[:, j * widths[i]:(j + 1) * widths[i]] = grads[i].astype(dest.dtype)
            dst[j] = grads[4]

        for n, i in enumerate(placed):
            window_copy(n, i).start()

            @pl.when((p == n_pairs - 1) & (s == nseg - 1))
            def _(n=n, i=i):
                window_copy(n, i).wait()

    key_spec = pl.BlockSpec((SEG, HEADS_PER_STEP * LIN_KEY_DIM), lambda p, s: (rev(s), p))
    val_spec = pl.BlockSpec((SEG, HEADS_PER_STEP * LIN_VAL_DIM), lambda p, s: (rev(s), p))
    all_specs = [key_spec, key_spec, val_spec] + ([key_spec] if with_decay_grad else [])
    any_spec = pl.BlockSpec(memory_space=pl.ANY)
    outs = _side_call(
        side, body,
        name=name,
        grid=(n_pairs, nseg),
        in_specs=[_lin_spec(w, rev) for w in (q, k, v, la)] + [
            pl.BlockSpec(_STATE_BLOCK, lambda p, s: (p, rev(s), 0, 0)), val_spec] + ([any_spec] if placed else []),
        out_specs=[all_specs[i] for i in fresh] + ([any_spec] if placed else []),
        out_shape=[jax.ShapeDtypeStruct((T, LIN_HEADS * widths[i]), F32) for i in fresh]
        + ([jax.ShapeDtypeStruct(into.shape, into.dtype)] if placed else []),
        scratch_shapes=[pltpu.VMEM((HEADS_PER_STEP, LIN_VAL_DIM, LIN_KEY_DIM), F32)]
        + [pltpu.VMEM((SEG, HEADS_PER_STEP * widths[i]), into.dtype) for i in placed]
        + ([pltpu.SemaphoreType.DMA((len(placed),))] if placed else []),
        args=[q.arr, k.arr, v.arr, la.arr, states, dout] + ([into] if placed else []),
        aliases={6: len(fresh)} if placed else None,
    )
    return outs


def _coords():
    return lax.axis_index("x"), lax.axis_index("y"), lax.axis_index("c")


def _peer(k):
    x, y, c = _coords()
    px = 1 - x if k & 4 else x
    py = 1 - y if k & 2 else y
    pc = 1 - c if k & 1 else c
    return (px, py, pc), 4 * px + 2 * py + pc


def _remote(src, dst, send_sem, recv_sem, device):
    return pltpu.make_async_remote_copy(src_ref=src, dst_ref=dst, send_sem=send_sem, recv_sem=recv_sem,
                                        device_id=device, device_id_type=pl.DeviceIdType.MESH)


class Side:
    def __init__(self, kind, arrays):
        self.kind, self.arrays, self.n = kind, list(arrays), len(arrays)
        self.outs_per = 1
        self.slots = {"gather": N_DEV - 1, "pair": 4, "chips": 3}[kind]

    def out_shape(self):
        if self.kind == "gather":
            return [jax.ShapeDtypeStruct((N_DEV,) + a.shape, a.dtype) for a in self.arrays]
        return [jax.ShapeDtypeStruct((4,) + a.shape[1:], a.dtype) for a in self.arrays]

    def sems(self):
        return [pltpu.SemaphoreType.DMA((self.n, self.slots)), pltpu.SemaphoreType.DMA((self.n, self.slots)),
                pltpu.SemaphoreType.DMA((self.n,))]

    def start(self, ins, outs, sems):
        send, recv, own = sems
        x, y, c = _coords()
        me = 4 * x + 2 * y + c
        for a in range(self.n):
            if self.kind == "gather":
                pltpu.make_async_copy(ins[a], outs[a].at[me], own.at[a]).start()
                for k in (1, 2, 4, 6):
                    peer, _ = _peer(k)
                    _remote(ins[a], outs[a].at[me], send.at[a, k - 1], recv.at[a, k - 1], peer).start()
            elif self.kind == "pair":
                for j in range(4):
                    _remote(ins[a].at[2 * j + 1 - c], outs[a].at[j], send.at[a, j], recv.at[a, j], (x, y, 1 - c)).start()
            else:
                chip = 2 * x + y
                pltpu.make_async_copy(ins[a].at[chip], outs[a].at[chip], own.at[a]).start()
                for d in (1, 2, 3):
                    tx, ty = (1 - x if d & 2 else x), (1 - y if d & 1 else y)
                    _remote(ins[a].at[2 * tx + ty], outs[a].at[chip], send.at[a, d - 1], recv.at[a, d - 1], (tx, ty, c)).start()

    def finish(self, ins, outs, sems):
        send, recv, own = sems
        x, y, c = _coords()
        me = 4 * x + 2 * y + c
        sibling = (x, y, 1 - c)
        for a in range(self.n):
            if self.kind == "gather":
                blk = outs[a].at[me]
                for k in (2, 4, 6):
                    _, pidx = _peer(k)
                    _remote(blk, blk, send.at[a, k - 1], recv.at[a, k - 1], sibling).wait_recv()
                    _remote(outs[a].at[pidx], outs[a].at[pidx], send.at[a, k], recv.at[a, k], sibling).start()
                pltpu.make_async_copy(ins[a], blk, own.at[a]).wait()
                for k in range(1, N_DEV):
                    _remote(blk, blk, send.at[a, k - 1], recv.at[a, k - 1], sibling).wait_send()
                for k in (1, 3, 5, 7):
                    _remote(blk, blk, send.at[a, k - 1], recv.at[a, k - 1], sibling).wait_recv()
            elif self.kind == "pair":
                for j in range(4):
                    _remote(ins[a].at[j], outs[a].at[j], send.at[a, j], recv.at[a, j], sibling).wait()
            else:
                slab = outs[a].at[0]
                pltpu.make_async_copy(slab, slab, own.at[a]).wait()
                for d in (1, 2, 3):
                    _remote(slab, slab, send.at[a, d - 1], recv.at[a, d - 1], sibling).wait()


def _side_call(side, body, *, name, grid, in_specs, out_specs, out_shape, scratch_shapes, args, semantics=None,
               aliases=None):
    in_specs, out_specs, out_shape = list(in_specs), list(out_specs), list(out_shape)
    scratch_shapes, args = list(scratch_shapes), list(args)
    n_in, n_out = len(in_specs), len(out_specs)
    kernel_body = body
    if side is not None:
        n_so = side.n * side.outs_per

        def kernel_body(*refs):
            ins, side_ins = refs[:n_in], refs[n_in:n_in + side.n]
            pos = n_in + side.n
            outs, side_outs = refs[pos:pos + n_out], refs[pos + n_out:pos + n_out + n_so]
            rest = refs[pos + n_out + n_so:]
            scratch, sems = rest[:-3], rest[-3:]
            ids = [pl.program_id(ax) for ax in range(len(grid))]
            first = functools.reduce(jnp.logical_and, [i == 0 for i in ids])
            last = functools.reduce(jnp.logical_and, [i == g - 1 for i, g in zip(ids, grid)])

            @pl.when(first)
            def _():
                side.start(side_ins, side_outs, sems)

            body(*ins, *outs, *scratch)

            @pl.when(last)
            def _():
                side.finish(side_ins, side_outs, sems)

        any_spec = pl.BlockSpec(memory_space=pl.ANY)
        in_specs += [any_spec] * side.n
        args += side.arrays
        out_specs += [any_spec] * n_so
        out_shape += side.out_shape()
        scratch_shapes += side.sems()
        semantics = ("arbitrary",) * len(grid)
    outs = pl.pallas_call(
        kernel_body,
        name=name,
        grid=grid,
        in_specs=in_specs,
        out_specs=out_specs,
        out_shape=out_shape,
        scratch_shapes=scratch_shapes,
        input_output_aliases=aliases or {},
        compiler_params=pltpu.CompilerParams(dimension_semantics=semantics or ("arbitrary",) * len(grid),
                                             vmem_limit_bytes=VMEM_BIG),
    )(*args)
    if side is not None:
        side.results = list(outs[n_out:])
    return list(outs[:n_out])


def carrier(name, side):
    _side_call(side, lambda: None, name=name, grid=(1,), in_specs=[], out_specs=[], out_shape=[], scratch_shapes=[], args=[])
    return side.results


def all_reduce_small(vec):
    def body(v_ref, o_ref, land, send_sems, recv_sems):
        x, y, c = _coords()
        me = 4 * x + 2 * y + c
        land[me] = v_ref[...]
        copies = []
        for k in range(1, N_DEV):
            peer, _ = _peer(k)
            cp = pltpu.make_async_remote_copy(
                src_ref=v_ref, dst_ref=land.at[me], send_sem=send_sems.at[k - 1], recv_sem=recv_sems.at[k - 1],
                device_id=peer, device_id_type=pl.DeviceIdType.MESH)
            cp.start()
            copies.append(cp)
        for cp in copies:
            cp.wait()
        total = land[0]
        for s in range(1, N_DEV):
            total = total + land[s]
        o_ref[...] = total

    return pl.pallas_call(
        body,
        name="all_reduce_small",
        in_specs=[pl.BlockSpec(memory_space=pltpu.VMEM)],
        out_specs=pl.BlockSpec(memory_space=pltpu.VMEM),
        out_shape=jax.ShapeDtypeStruct(vec.shape, F32),
        scratch_shapes=[pltpu.VMEM((N_DEV,) + vec.shape, F32), pltpu.SemaphoreType.DMA((N_DEV - 1,)),
                        pltpu.SemaphoreType.DMA((N_DEV - 1,))],
    )(vec)


def _pack(parts):
    flat = jnp.concatenate([p.reshape(-1).astype(F32) for p in parts])
    n = flat.shape[0]
    rows = -(-n // 1024) * 8
    return jnp.pad(flat, (0, rows * 128 - n)).reshape(rows, 128)


def _unpack(packed, shapes):
    flat, out, pos = packed.reshape(-1), [], 0
    for s in shapes:
        n = math.prod(s)
        out.append(flat[pos:pos + n].reshape(s))
        pos += n
    return out


def _row2(v):
    return v.reshape(1, -1)


def _rider(hooks, name, ctx=None):
    if name not in hooks:
        return None, lambda: None
    side, done = hooks[name](ctx)
    return side, lambda: done(side.results)


def _layer_forward(h, hb, W, consts, hooks):
    cos, sin, ret_decay = consts
    side, landed = _rider(hooks, "proj")
    proj = mm("proj", hb, W["in"], "nn", tm=4096, tn=512, tk=1024, side=side, out_dtype=PROJ_DTYPE)
    landed()
    bias_tab, first_tab = W["bias_tab"]
    side, landed = _rider(hooks, "attn_fwd")
    attn, attn_lse = attn_fwd(proj, bias_tab, first_tab, side=side)
    landed()
    rq2, rk2 = row_call("ret_pre", f_ret_pre, [Row(proj, 512, COL["rq"] // 512), Row(proj, 512, COL["rk"] // 512),
                                              Row(cos), Row(sin)], [], [(512, F32), (512, F32)], tm=512)
    side, landed = _rider(hooks, "ret_fwd")
    ret_raw, ret_st = lin_fwd("ret_fwd", Win(rq2, 128), Win(rk2, 128), Win(proj, 256, COL["rv"] // 256), Win(ret_decay, 128),
                              side=side)
    landed()
    gq2, gla_la = row_call("gla_pre", f_gla_pre, [Row(proj, 512, COL["gq"] // 512), Row(proj, GLR_BLOCK, GLR_OFF // GLR_BLOCK)],
                           [W["wlr"], W["blr"]], [(512, F32), (512, F32)], tm=512)
    side, landed = _rider(hooks, "gla_fwd")
    gla_raw, gla_st = lin_fwd("gla_fwd", Win(gq2, 128), Win(proj, 128, COL["gk"] // 128), Win(proj, 256, COL["gv"] // 256),
                              Win(gla_la, 128), side=side)
    landed()
    ret_b, gla_b = row_call("normgate", f_normgate, [Row(ret_raw), Row(proj, 1024, COL["rg"] // 1024), Row(gla_raw),
                                                    Row(proj, 1024, COL["gg"] // 1024)], [W["gnorm"]],
                            [(1024, MXU_DTYPE), (1024, MXU_DTYPE)], tm=512)
    branches = (attn, ret_b, gla_b)
    p = [mm("branch_proj", branches[n], W["branch"][n], "nn", tm=1024, tn=1024, tk=1024) for n in range(3)]
    merged, = row_call("merge", f_merge, [Row(p[0]), Row(p[1]), Row(p[2]), Row(proj, GATES_W, GATES_OFF // GATES_W)], [],
                       [(1024, MXU_DTYPE)], tm=512)
    post_norm = lambda y, x, g, b: f_deepnorm(x, y, g, b) * 2 + (y,)
    x1, x1b, mix = mm("out_proj_ln1", merged, W["out"], "nn", tm=512, tn=D_MODEL, tk=1024, extras=(h,),
                      row_params=(W["ln1_g"], W["ln1_b"]), epi=post_norm, n_out=3, out_dtypes=(F32, MXU_DTYPE, F32))
    side, landed = _rider(hooks, "mlp_up")
    u, act = mm("mlp_up", x1b, W["up"], "nn", tm=2048, tn=512, tk=1024, n_out=2, out_dtypes=(F32, MXU_DTYPE),
                epi=lambda r: (r, jnp.square(jnp.maximum(r, 0.0))), side=side)
    landed()
    x2, x2b, y = mm("mlp_down_ln2", act, W["down"], "nn", tm=1024, tn=D_MODEL, tk=1024, extras=(x1,),
                    row_params=(W["ln2_g"], W["ln2_b"]), epi=post_norm, n_out=3, out_dtypes=(F32, MXU_DTYPE, F32))
    saved = dict(h=h, hb=hb, proj=proj, attn_lse=attn_lse, rq2=rq2, rk2=rk2, ret_st=ret_st, gq2=gq2, gla_la=gla_la,
                 gla_st=gla_st, ret_raw=ret_raw, gla_raw=gla_raw, branches=branches, p=p, merged=merged, mix=mix, x1=x1, x1b=x1b,
                 u=u, act=act, y=y)
    return x2, x2b, saved


def _layer_backward(dx2, S, W, consts, hooks):
    cos, sin, ret_decay = consts
    h, proj, T = S["h"], S["proj"], S["h"].shape[0]
    add = lambda r, e: r + e
    dx1a, dy, dg2, db2 = row_call("ln2_bwd", vjp_fn(f_deepnorm, 2, 1), [Row(S["x1"]), Row(S["y"]), Row(dx2)],
                                  [W["ln2_g"], W["ln2_b"]], [(1024, F32), (1024, MXU_DTYPE)], [(1, 1024), (1, 1024)], tm=512)
    side, landed = _rider(hooks, "mlp_down_bwd")
    du = mm("mlp_down_bwd", dy, W["down"], "nt", tm=1024, tn=512, tk=1024, extras=(S["u"],),
            epi=lambda r, u: r * (2.0 * jnp.maximum(u, 0.0)), out_dtype=MXU_DTYPE, side=side)
    landed()
    g_down = mm("mlp_down_wgrad", S["act"], dy, "tn", tm=1024, tn=1024, tk=2048, out_dtype=WIRE_DTYPE)
    dx1 = mm("mlp_up_bwd", du, W["up"], "nt", tm=1024, tn=1024, tk=2048, extras=(dx1a,), epi=add)
    g_up = mm("mlp_up_wgrad", S["x1b"], du, "tn", tm=1024, tn=D_FF // N_DEV, tk=2048, out_dtype=WIRE_DTYPE,
              column_slabs=True)
    dha, dmix, dg1, db1 = row_call("ln1_bwd", vjp_fn(f_deepnorm, 2, 1), [Row(h), Row(S["mix"]), Row(dx1)],
                                   [W["ln1_g"], W["ln1_b"]], [(1024, F32), (1024, MXU_DTYPE)], [(1, 1024), (1, 1024)], tm=512)
    dmerged = mm("out_proj_bwd", dmix, W["out"], "nt", tm=1024, tn=1024, tk=1024)
    g_out = mm("out_proj_wgrad", S["merged"], dmix, "tn", tm=1024, tn=1024, tk=2048, out_dtype=WIRE_DTYPE)
    dproj = jax.ShapeDtypeStruct((T, PROJ_PAD), MXU_DTYPE)
    p = S["p"]
    dp0, dp1, dp2, dproj = row_call(
        "merge_bwd", vjp_fn(f_merge, 4, 1), [Row(p[0]), Row(p[1]), Row(p[2]), Row(proj, GATES_W, GATES_OFF // GATES_W), Row(dmerged)],
        [], [(1024, MXU_DTYPE)] * 3, tm=256, into=dproj, into_cols=[(GATES_W, GATES_OFF)])
    dps = (dp0, dp1, dp2)
    dbr = [mm("branch_proj_bwd", dps[n], W["branch"][n], "nt", tm=1024, tn=1024, tk=1024,
              out_dtype=MXU_DTYPE if n == 0 else F32) for n in range(3)]
    g_branch = [mm("branch_proj_wgrad", S["branches"][n], dps[n], "tn", tm=1024, tn=1024, tk=2048, out_dtype=WIRE_DTYPE)
                for n in range(3)]
    dret_raw, dgla_raw, dgnorm, dproj = row_call(
        "normgate_bwd", vjp_fn(lambda a, b, c, d, g: _reorder_normgate(a, b, c, d, g), 4, 2),
        [Row(S["ret_raw"]), Row(S["gla_raw"]), Row(proj, 1024, COL["rg"] // 1024), Row(proj, 1024, COL["gg"] // 1024),
         Row(dbr[1]), Row(dbr[2])], [W["gnorm"]], [(1024, F32), (1024, F32)], [(1, LIN_VAL_DIM)], tm=256,
        into=dproj, into_cols=[(1024, COL["rg"]), (1024, COL["gg"])])
    side, landed = _rider(hooks, "gla_bwd")
    dgq2, dgla_la, dproj = lin_bwd("gla_bwd", Win(S["gq2"], 128), Win(proj, 128, COL["gk"] // 128),
                                   Win(proj, 256, COL["gv"] // 256), Win(S["gla_la"], 128), S["gla_st"], dgla_raw, True,
                                   side=side, into=dproj, windows={1: COL["gk"], 2: COL["gv"]})
    landed()
    dwlr, dblr, dproj = row_call(
        "gla_pre_bwd", _gla_pre_bwd, [Row(proj, 512, COL["gq"] // 512), Row(proj, GLR_BLOCK, GLR_OFF // GLR_BLOCK),
                                      Row(dgq2), Row(dgla_la)], [W["wlr"], W["blr"]], [],
        [(GLR_BLOCK, 512), (1, 512)], tm=512, into=dproj,
        into_cols=[(512, COL["gq"]), (PROJ_PAD - GLR_OFF, GLR_OFF)])
    side, landed = _rider(hooks, "ret_bwd", dict(branch=g_branch, out=g_out, up=g_up, down=g_down))
    drq2, drk2, dproj = lin_bwd("ret_bwd", Win(S["rq2"], 128), Win(S["rk2"], 128), Win(proj, 256, COL["rv"] // 256),
                                Win(ret_decay, 128), S["ret_st"], dret_raw, False, side=side, into=dproj,
                                windows={2: COL["rv"]})
    landed()
    dproj, = row_call(
        "ret_pre_bwd", _ret_pre_bwd, [Row(proj, 512, COL["rq"] // 512), Row(proj, 512, COL["rk"] // 512), Row(cos), Row(sin),
                                      Row(drq2), Row(drk2)], [], [], tm=512, into=dproj,
        into_cols=[(512, COL["rq"]), (512, COL["rk"])])
    side, landed = _rider(hooks, "attn_bwd")
    dproj, dbias = attn_bwd(proj, *W["bias_tab"], S["branches"][0], S["attn_lse"], dbr[0], dproj, side=side)
    landed()
    d_rel_bias = bias_table_grad(dbias)
    side, landed = _rider(hooks, "proj_wgrad", dict(branch=g_branch, out=g_out, up=g_up, down=g_down))
    g_in = mm("proj_wgrad", S["hb"], dproj, "tn", tm=1024, tn=1280, tk=2048, out_dtype=WIRE_DTYPE, side=side)
    landed()
    side, landed = _rider(hooks, "proj_bwd", {"in": g_in})
    dh = mm("proj_bwd", dproj, W["in"], "nt", tm=1024, tn=1024, tk=2560, extras=(dha,), epi=add, side=side)
    landed()
    big = dict(g_in=g_in, branch=g_branch, out=g_out, up=g_up, down=g_down)
    small = dict(rel_bias=d_rel_bias, wlr=dwlr[:GLA_RANK], blr=dblr, gnorm=dgnorm, ln1_g=dg1, ln1_b=db1, ln2_g=dg2, ln2_b=db2)
    return dh, big, small


def _reorder_normgate(ret_raw, gla_raw, rg, gg, gnorm):
    return f_normgate(ret_raw, rg, gla_raw, gg, gnorm)


def _gla_pre_bwd(gq, glr, dq2, dla, wlr, blr):
    _, pull = jax.vjp(f_gla_pre, gq, glr, wlr, blr)
    dgq, dglr, dwlr, dblr = pull((dq2, dla))
    dglr = jnp.concatenate([dglr, jnp.zeros((dglr.shape[0], PROJ_PAD - GLR_OFF - GLR_BLOCK), F32)], axis=1)
    return dgq, dglr, dwlr, dblr


def _ret_pre_bwd(rq, rk, cos, sin, drq2, drk2):
    _, pull = jax.vjp(lambda a, b: f_ret_pre(a, b, cos, sin), rq, rk)
    return pull((drq2, drk2))


SHARDED = ("in", "branch", "out", "up", "down")

_FULL = {
    "branch": lambda g: [g[:, n].reshape(D_MODEL, D_MODEL) for n in range(3)],
    "out": lambda g: g.reshape(D_MODEL, D_MODEL),
    "up": lambda g: jnp.transpose(g, (1, 0, 2)).reshape(D_MODEL, D_FF),
    "down": lambda g: g.reshape(D_FF, D_MODEL),
}


def _owner_pieces(s):
    shard = PROJ_WIDTH // N_DEV
    lo, hi = s * shard, (s + 1) * shard
    pieces = []
    for a, b, padded in ((0, GLR_SRC, 0), (GLR_SRC, GATES_SRC, GLR_OFF), (GATES_SRC, PROJ_WIDTH, GATES_OFF)):
        x0, x1 = max(lo, a), min(hi, b)
        if x0 < x1:
            pieces.append((x0 - lo, x1 - x0, padded + x0 - a))
    return pieces


def _full_in(g):
    placed = sorted((p, s, c, w) for s in range(N_DEV) for c, w, p in _owner_pieces(s))
    cols = [g[s][:, c:c + w] for _, s, c, w in placed]
    return jnp.concatenate(cols + [jnp.zeros((D_MODEL, PROJ_PAD - PROJ_WIDTH), g.dtype)], axis=1)


def _full_wlr(g):
    wlr = jnp.transpose(g, (1, 0, 2)).reshape(GLA_RANK, LIN_HEADS * LIN_KEY_DIM)
    return jnp.pad(wlr, ((0, GLR_BLOCK - GLA_RANK), (0, 0)))


def _slab_in(g):
    return jnp.stack([jnp.concatenate([g[:, p:p + w] for _, w, p in _owner_pieces(s)], axis=1) for s in range(N_DEV)])


_SLABS = {
    "in": _slab_in,
    "branch": lambda gs: jnp.stack([b.reshape(N_DEV, D_MODEL // N_DEV, D_MODEL) for b in gs], axis=1),
    "out": lambda g: g.reshape(N_DEV, D_MODEL // N_DEV, D_MODEL),
    "up": lambda g: g,
    "down": lambda g: g.reshape(N_DEV, D_FF // N_DEV, D_MODEL),
}


def _pair_sums(name, slabs, from_sibling=None):
    sums = []
    for mine, sibs in zip(slabs, from_sibling or carrier(name, Side("pair", slabs))):
        width = mine.shape[-1]
        rows = math.prod(mine.shape[1:-1])
        tm = 256 if rows % 256 == 0 else 128
        nb = rows // tm
        my_rows = Row(mine.reshape(-1, width), rowmap=lambda i, nb=nb: (2 * (i // nb) + lax.axis_index("c")) * nb + i % nb)
        out, = row_call("pair_sum", lambda q, p: (p.astype(F32) + q.astype(F32),),
                        [Row(sibs.reshape(-1, width)), my_rows], [], [(width, WIRE_DTYPE)], tm=tm)
        sums.append(out.reshape(sibs.shape))
    return sums


def _sum_adam(name, parts, w, m, v, layer, tm, earlier):
    C = w.shape[-1]
    rows = [Row(parts, lead=(s,)) for s in range(parts.shape[0])] + [Row(a, lead=(layer,)) for a in (w, m, v)]
    return row_call(name, f_sum_adam, rows, [], [(C, F32)] * 4, tm=tm, stack=(layer, DEPTH), stacked_onto=earlier)


def kernel(x, ln_in_g, ln_in_b, w_in, rel_bias, gla_w_lr, gla_b_lr, gla_norm_g, w_branch, w_out, ln1_g, ln1_b, w_up, w_down, ln2_g, ln2_b, loss_target, m_ln_in_g, m_ln_in_b, m_w_in, m_rel_bias, m_gla_w_lr, m_gla_b_lr, m_gla_norm_g, m_w_branch, m_w_out, m_ln1_g, m_ln1_b, m_w_up, m_w_down, m_ln2_g, m_ln2_b, v_ln_in_g, v_ln_in_b, v_w_in, v_rel_bias, v_gla_w_lr, v_gla_b_lr, v_gla_norm_g, v_w_branch, v_w_out, v_ln1_g, v_ln1_b, v_w_up, v_w_down, v_ln2_g, v_ln2_b):
    T = x.shape[1]
    x2d = x.reshape(T, D_MODEL)
    tgt = loss_target.reshape(T, D_MODEL)
    me = 4 * lax.axis_index("x") + 2 * lax.axis_index("y") + lax.axis_index("c")

    inv = ROPE_BASE ** (-jnp.arange(0, LIN_KEY_DIM, 2, dtype=F32) / LIN_KEY_DIM)
    ang = jnp.arange(T, dtype=F32)[:, None] * inv[None, :]
    cos = jnp.concatenate([jnp.cos(ang), jnp.cos(ang)], axis=1)
    sin = jnp.concatenate([-jnp.sin(ang), jnp.sin(ang)], axis=1)
    log_gamma = jnp.log1p(-jnp.exp2(-5.0 - jnp.arange(LIN_HEADS, dtype=F32)))
    ret_decay = jnp.repeat(log_gamma, LIN_KEY_DIM).reshape(1, LIN_HEADS * LIN_KEY_DIM)
    consts = (cos, sin, ret_decay)

    layers = [dict(blr=_row2(gla_b_lr[l]), gnorm=_row2(gla_norm_g[l]), ln1_g=_row2(ln1_g[l]),
                   ln1_b=_row2(ln1_b[l]), ln2_g=_row2(ln2_g[l]), ln2_b=_row2(ln2_b[l])) for l in range(DEPTH)]
    shards = dict(branch=w_branch, out=w_out, up=w_up, down=w_down)
    shards["in"] = w_in
    wire = [{n: shards[n][l].astype(WIRE_DTYPE) for n in SHARDED} for l in range(DEPTH)]

    def gather_of(l, names):
        def landed(res):
            for n, g in zip(names, res):
                layers[l][n] = _full_in(g) if n == "in" else _FULL[n](g)
        return lambda ctx: (Side("gather", [wire[l][n] for n in names]), landed)

    first = Side("gather", [wire[0]["in"]] + [gla_w_lr[l] for l in range(DEPTH)])
    bias_tabs = bias_table(rel_bias, side=first)
    layers[0]["in"] = _full_in(first.results[0])
    for l in range(DEPTH):
        layers[l]["wlr"] = _full_wlr(first.results[1 + l])
        layers[l]["bias_tab"] = (bias_tabs, l * ATTN_HEADS)

    fwd_hooks = [{"proj": gather_of(1, ("in",)), "attn_fwd": gather_of(0, ("up", "down")),
                  "ret_fwd": gather_of(0, ("branch", "out"))},
                 {"proj": gather_of(1, ("up", "down")), "attn_fwd": gather_of(1, ("branch", "out"))}]
    h, hb = row_call("ln_in", _twice(f_ln_in), [Row(x2d)], [_row2(ln_in_g), _row2(ln_in_b)], STREAM, tm=512)
    saved = []
    for l in range(DEPTH):
        h, hb, s = _layer_forward(h, hb, layers[l], consts, fwd_hooks[l])
        saved.append(s)
    dh, loss_lanes = row_call("loss", f_loss, [Row(h), Row(tgt)], [], [(D_MODEL, F32)], [(1, 128)], tm=512)

    received = [dict() for _ in range(DEPTH)]

    def scatter_of(l, names, sums=None):
        def make(ctx):
            parts = [sums[n] for n in names] if sums is not None else _pair_sums(
                "pair_grads", [_SLABS[n](ctx[n]) for n in names])

            def landed(res):
                received[l].update(zip(names, res))
            return Side("chips", parts), landed
        return make

    small = [None] * DEPTH
    dh, big, small[1] = _layer_backward(dh, saved[1], layers[1], consts, {})
    slabs1 = [_SLABS[n](big["g_in"] if n == "in" else big[n]) for n in SHARDED]
    sums1 = {}

    def pair_exchange_1(ctx):
        def landed(res):
            sums1.update(zip(SHARDED, _pair_sums("pair_grads", slabs1, from_sibling=res)))
        return Side("pair", slabs1), landed

    rest = SHARDED[1:]
    sums0 = {}

    def pair_exchange_0(ctx):
        slabs0 = [_SLABS[n](ctx[n]) for n in rest]

        def landed(res):
            sums0.update(zip(rest, _pair_sums("pair_grads", slabs0, from_sibling=res)))
        return Side("pair", slabs0), landed

    dh, big, small[0] = _layer_backward(dh, saved[0], layers[0], consts, {
        "mlp_down_bwd": pair_exchange_1, "gla_bwd": scatter_of(1, ("in",), sums1), "ret_bwd": pair_exchange_0,
        "attn_bwd": scatter_of(1, rest, sums1), "proj_wgrad": scatter_of(0, rest, sums0), "proj_bwd": scatter_of(0, ("in",))})
    grad_x, d_in_g, d_in_b = row_call("ln_in_bwd", vjp_fn(f_ln_in, 1, 1), [Row(x2d), Row(dh)],
                                      [_row2(ln_in_g), _row2(ln_in_b)], [(D_MODEL, F32)], [(1, D_MODEL), (1, D_MODEL)], tm=256)

    small_names = ("rel_bias", "wlr", "blr", "gnorm", "ln1_g", "ln1_b", "ln2_g", "ln2_b")
    parts = [d_in_g, d_in_b] + [small[l][n] for l in range(DEPTH) for n in small_names] + [loss_lanes]
    shapes = [(D_MODEL,), (D_MODEL,)] + [s for l in range(DEPTH) for s in (
        (ATTN_HEADS, N_REL), (GLA_RANK, 512), (512,), (LIN_VAL_DIM,), (D_MODEL,), (D_MODEL,), (D_MODEL,), (D_MODEL,))] + [(128,)]
    red = _unpack(all_reduce_small(_pack(parts)), shapes)
    loss = jnp.sum(red[-1])
    g_small = {"ln_in_g": red[0], "ln_in_b": red[1]}
    for l in range(DEPTH):
        for j, n in enumerate(small_names):
            g_small[(l, n)] = red[2 + l * len(small_names) + j]
    shard = LIN_HEADS * LIN_KEY_DIM // N_DEV
    g_wlr = jnp.stack([lax.dynamic_slice_in_dim(g_small[(l, "wlr")], me * shard, shard, axis=1) for l in range(DEPTH)])
    stack = lambda n: jnp.stack([g_small[(l, n)] for l in range(DEPTH)])
    g_rep = [g_small["ln_in_g"], g_small["ln_in_b"], stack("rel_bias"), g_wlr, stack("blr"), stack("gnorm"),
             stack("ln1_g"), stack("ln1_b"), stack("ln2_g"), stack("ln2_b")]
    w_rep = [ln_in_g, ln_in_b, rel_bias, gla_w_lr, gla_b_lr, gla_norm_g, ln1_g, ln1_b, ln2_g, ln2_b]
    m_rep = [m_ln_in_g, m_ln_in_b, m_rel_bias, m_gla_w_lr, m_gla_b_lr, m_gla_norm_g, m_ln1_g, m_ln1_b, m_ln2_g, m_ln2_b]
    v_rep = [v_ln_in_g, v_ln_in_b, v_rel_bias, v_gla_w_lr, v_gla_b_lr, v_gla_norm_g, v_ln1_g, v_ln1_b, v_ln2_g, v_ln2_b]
    rep_shapes = [w.shape for w in w_rep]
    d_pack, m_pack, v_pack = row_call("adam_small", f_adam, [Row(_pack(g_rep)), Row(_pack(w_rep)), Row(_pack(m_rep)), Row(_pack(v_rep))],
                                      [], [(128, F32)] * 3, tm=256)
    d_rep, nm_rep, nv_rep = _unpack(d_pack, rep_shapes), _unpack(m_pack, rep_shapes), _unpack(v_pack, rep_shapes)

    def owned(key, w, m, v, name, tm):
        C = w.shape[-1]
        flat = lambda a: a.reshape(DEPTH, -1, C)
        outs = None
        for l in range(DEPTH):
            outs = _sum_adam(name, received[l][key].reshape(4, -1, C), flat(w), flat(m), flat(v), l, tm, outs)
        return [o.reshape(w.shape) for o in outs]

    o_in = owned("in", w_in, m_w_in, v_w_in, "adam_w_in", 256)
    o_br = owned("branch", w_branch, m_w_branch, v_w_branch, "adam_w_branch", 128)
    o_out = owned("out", w_out, m_w_out, v_w_out, "adam_w_out", 128)
    o_up = owned("up", w_up, m_w_up, v_w_up, "adam_w_up", 256)
    o_down = owned("down", w_down, m_w_down, v_w_down, "adam_w_down", 256)

    def tree(j, rep):
        return [rep[0], rep[1], o_in[j], rep[2], rep[3], rep[4], rep[5], o_br[j], o_out[j], rep[6], rep[7], o_up[j], o_down[j],
                rep[8], rep[9]]

    return (loss, grad_x.reshape(1, T, D_MODEL), *tree(0, g_rep), *tree(1, d_rep), *tree(2, nm_rep), *tree(3, nv_rep))
```

```python
import functools
import math

import jax
import jax.numpy as jnp
from jax import lax
from jax.experimental import pallas as pl
from jax.experimental.pallas import tpu as pltpu

F32 = jnp.float32
MXU_DTYPE = jnp.bfloat16
WIRE_DTYPE = jnp.bfloat16
PROJ_DTYPE = jnp.bfloat16

N_DEV = 8
D_MODEL = 1024
DEPTH = 2
CHUNK = 64
ATTN_HEADS = 8
ATTN_HEAD_DIM = 128
ATTN_LEFT_CHUNKS = 8
BAND = (ATTN_LEFT_CHUNKS + 1) * CHUNK
PAD_ROWS = ATTN_LEFT_CHUNKS * CHUNK
REL_CLIP = 2 * CHUNK
N_REL = 2 * REL_CLIP + 1
REL_MIN = REL_CLIP - (CHUNK - 1)
LIN_HEADS = 4
LIN_KEY_DIM = 128
LIN_VAL_DIM = 256
GLA_RANK = 16
GLA_GATE_NORMALIZER = 16.0
ROPE_BASE = 10000.0
D_FF = 4 * D_MODEL
LN_EPS = 1e-5
NEG_INF = -1e30
ALPHA = (2 * DEPTH) ** 0.25
ADAM_LR, ADAM_B1, ADAM_B2, ADAM_EPS, ADAM_WD, ADAM_STEP = 0.001, 0.9, 0.999, 1e-08, 0.01, 10

PROJ_WIDTH = 12304
GLR_SRC = 9216
GATES_SRC = GLR_SRC + GLA_RANK
MAIN_W = 9216
GATES_W = 3 * D_MODEL
GATES_OFF = MAIN_W
GLR_OFF = MAIN_W + GATES_W
PROJ_PAD = 12800
GLR_BLOCK = 128
COL = dict(aq=0, ak=1024, av=2048, rq=3072, rk=3584, rv=4096, rg=5120, gq=6144, gk=6656, gv=7168, gg=8192)

SEG_CHUNKS = 8
SEG = SEG_CHUNKS * CHUNK
VMEM_BIG = 56 * 1024 * 1024

_NN = (((1,), (0,)), ((), ()))
_NT = (((1,), (1,)), ((), ()))
_TN = (((0,), (0,)), ((), ()))


def _mxu(a, b, dn):
    return lax.dot_general(a.astype(MXU_DTYPE), b.astype(MXU_DTYPE), dn, preferred_element_type=F32)


@jax.custom_vjp
def dot_nn(a, b):
    return _mxu(a, b, _NN)


def _dot_nn_fwd(a, b):
    return _mxu(a, b, _NN), (a, b)


def _dot_nn_bwd(res, g):
    a, b = res
    return _mxu(g, b, _NT).astype(a.dtype), _mxu(a, g, _TN).astype(b.dtype)


dot_nn.defvjp(_dot_nn_fwd, _dot_nn_bwd)


@jax.custom_vjp
def dot_nt(a, b):
    return _mxu(a, b, _NT)


def _dot_nt_fwd(a, b):
    return _mxu(a, b, _NT), (a, b)


def _dot_nt_bwd(res, g):
    a, b = res
    return _mxu(g, b, _NN).astype(a.dtype), _mxu(g, a, _TN).astype(b.dtype)


dot_nt.defvjp(_dot_nt_fwd, _dot_nt_bwd)


@jax.custom_vjp
def dot_tn(a, b):
    return _mxu(a, b, _TN)


def _dot_tn_fwd(a, b):
    return _mxu(a, b, _TN), (a, b)


def _dot_tn_bwd(res, g):
    a, b = res
    return _mxu(b, g, _NT).astype(a.dtype), _mxu(a, g, _NN).astype(b.dtype)


dot_tn.defvjp(_dot_tn_fwd, _dot_tn_bwd)


def _tri(n, lower):
    r = lax.broadcasted_iota(jnp.int32, (n, n), 0)
    c = lax.broadcasted_iota(jnp.int32, (n, n), 1)
    return (r >= c) if lower else (r <= c)


def _tri_dot(lower, x):
    tri = _tri(x.shape[0], lower).astype(jnp.bfloat16)
    hi = x.astype(jnp.bfloat16)
    rest = x - hi.astype(F32)
    mid = rest.astype(jnp.bfloat16)
    lo = (rest - mid.astype(F32)).astype(jnp.bfloat16)
    dot = lambda part: lax.dot_general(tri, part, _NN, preferred_element_type=F32)
    return dot(hi) + dot(mid) + dot(lo)


@jax.custom_vjp
def chunk_cumsum(x):
    return _tri_dot(True, x)


def _chunk_cumsum_fwd(x):
    return chunk_cumsum(x), None


def _chunk_cumsum_bwd(_, g):
    return (_tri_dot(False, g),)


chunk_cumsum.defvjp(_chunk_cumsum_fwd, _chunk_cumsum_bwd)


@jax.custom_vjp
def swap_halves(t):
    return pltpu.roll(t, t.shape[-1] // 2, t.ndim - 1)


def _swap_fwd(t):
    return swap_halves(t), None


def _swap_bwd(_, g):
    return (swap_halves(g),)


swap_halves.defvjp(_swap_fwd, _swap_bwd)


def mm(name, a, b, dims, *, tm, tn, tk, out_dtype=F32, extras=(), row_params=(), epi=None, n_out=1, out_dtypes=None,
       side=None, column_slabs=False):
    if dims == "nn":
        (M, K), N = a.shape, b.shape[1]
    elif dims == "nt":
        (M, K), N = a.shape, b.shape[0]
    else:
        (K, M), N = a.shape, b.shape[1]
    tm, tn, tk = min(tm, M), min(tn, N), min(tk, K)
    assert M % tm == 0 and N % tn == 0 and K % tk == 0, (name, M, N, K, tm, tn, tk)
    nk = K // tk
    dn = {"nn": _NN, "nt": _NT, "tn": _TN}[dims]
    a_spec = pl.BlockSpec((tk, tm), lambda i, j, k: (k, i)) if dims == "tn" else pl.BlockSpec((tm, tk), lambda i, j, k: (i, k))
    b_spec = pl.BlockSpec((tn, tk), lambda i, j, k: (j, k)) if dims == "nt" else pl.BlockSpec((tk, tn), lambda i, j, k: (k, j))
    o_spec = pl.BlockSpec((tm, tn), lambda i, j, k: (i, j))
    row_spec = pl.BlockSpec((1, tn), lambda i, j, k: (0, j))
    n_ex = len(extras) + len(row_params)
    out_dtypes = out_dtypes or (out_dtype,) * n_out

    def finish(r, ex_refs, o_refs):
        if epi is not None:
            r = epi(r, *[e[...] for e in ex_refs])
        r = r if isinstance(r, tuple) else (r,)
        for o, v in zip(o_refs, r):
            o[...] = v.astype(o.dtype)

    def body(*refs):
        a_ref, b_ref = refs[:2]
        ex_refs = refs[2:2 + n_ex]
        o_refs = refs[2 + n_ex:2 + n_ex + n_out]
        part = _mxu(a_ref[...], b_ref[...], dn)
        if nk == 1:
            finish(part, ex_refs, o_refs)
            return
        acc = refs[-1]
        k = pl.program_id(2)

        @pl.when(k == 0)
        def _():
            acc[...] = part

        @pl.when(k > 0)
        def _():
            acc[...] += part

        @pl.when(k == nk - 1)
        def _():
            finish(acc[...], ex_refs, o_refs)

    outs = _side_call(
        side, body,
        name=name,
        grid=(M // tm, N // tn, nk),
        in_specs=[a_spec, b_spec] + [o_spec] * len(extras) + [row_spec] * len(row_params),
        out_specs=[pl.BlockSpec((None, tm, tn), lambda i, j, k: (j, i, 0))] * n_out if column_slabs else [o_spec] * n_out,
        out_shape=[jax.ShapeDtypeStruct((N // tn, M, tn) if column_slabs else (M, N), dt) for dt in out_dtypes],
        scratch_shapes=[pltpu.VMEM((tm, tn), F32)] if nk > 1 else [],
        args=[a, b, *extras, *row_params],
        semantics=("parallel", "parallel", "arbitrary"),
    )
    return outs[0] if n_out == 1 else outs


class Row:
    def __init__(self, arr, width=None, col=0, lead=(), rowmap=None):
        self.arr, self.lead, self.col = arr, tuple(lead), col
        self.width = arr.shape[-1] if width is None else width
        self.rowmap = rowmap

    def spec(self, tm):
        lead, col, rowmap = self.lead, self.col, self.rowmap or (lambda i: i)
        return pl.BlockSpec((None,) * len(lead) + (tm, self.width), lambda i: (*lead, rowmap(i), col))


def row_call(name, fn, rows, params, new_outs, red_shapes=(), *, tm, into=None, into_cols=(), stack=None, stacked_onto=None):
    R = rows[0].arr.shape[len(rows[0].lead)]
    tm = min(tm, R)
    assert R % tm == 0, (name, R, tm)
    n_row, n_par, n_new, n_into, n_red = len(rows), len(params), len(new_outs), len(into_cols), len(red_shapes)
    fresh = isinstance(into, jax.ShapeDtypeStruct)

    def body(*refs):
        row_refs = refs[:n_row]
        par_refs = refs[n_row:n_row + n_par]
        pos = n_row + n_par + (1 if (into is not None and not fresh) else 0) + (n_new if stacked_onto is not None else 0)
        new_refs = refs[pos:pos + n_new]
        red_refs = refs[pos + n_new:pos + n_new + n_red]
        pos2 = pos + n_new + n_red
        into_ref = refs[pos2] if into is not None else None
        stage = refs[pos2 + 1:pos2 + 1 + n_into] if into is not None else ()
        i = pl.program_id(0)
        widen = lambda t: t.astype(F32) if jnp.issubdtype(t.dtype, jnp.floating) else t
        vals = fn(*[widen(r[...]) for r in row_refs], *[p[...] for p in par_refs])
        vals = vals if isinstance(vals, (tuple, list)) else (vals,)
        for o, v in zip(new_refs, vals[:n_new]):
            o[...] = v.astype(o.dtype)

        def window_copy(j, step):
            width, off = into_cols[j]
            rows_at = pl.ds(pl.multiple_of(step * tm, tm), tm)
            return pltpu.make_async_copy(stage[j], into_ref.at[rows_at, pl.ds(off, width)], refs[pos2 + 1 + n_into].at[j])

        for j, v in enumerate(vals[n_new:n_new + n_into]):
            @pl.when(i > 0)
            def _(j=j):
                window_copy(j, i - 1).wait()

            stage[j][...] = v.astype(stage[j].dtype)
            window_copy(j, i).start()

            @pl.when(i == R // tm - 1)
            def _(j=j):
                window_copy(j, i).wait()
        for o, v in zip(red_refs, vals[n_new + n_into:]):
            @pl.when(i == 0)
            def _(o=o, v=v):
                o[...] = v.astype(o.dtype)

            @pl.when(i > 0)
            def _(o=o, v=v):
                o[...] += v.astype(o.dtype)

    in_specs = [r.spec(tm) for r in rows] + [pl.BlockSpec(p.shape, lambda i, nd=p.ndim: (0,) * nd) for p in params]
    args = [r.arr for r in rows] + list(params)
    if stack is None:
        out_specs = [pl.BlockSpec((tm, w), lambda i: (i, 0)) for w, _ in new_outs]
        out_shape = [jax.ShapeDtypeStruct((R, w), dt) for w, dt in new_outs]
    else:
        index, depth = stack
        out_specs = [pl.BlockSpec((None, tm, w), lambda i: (index, i, 0)) for w, _ in new_outs]
        out_shape = [jax.ShapeDtypeStruct((depth, R, w), dt) for w, dt in new_outs]
    out_specs += [pl.BlockSpec(s, lambda i, nd=len(s): (0,) * nd) for s in red_shapes]
    out_shape += [jax.ShapeDtypeStruct(s, F32) for s in red_shapes]
    aliases, scratch = {}, []
    if into is not None:
        out_specs.append(pl.BlockSpec(memory_space=pl.ANY))
        out_shape.append(jax.ShapeDtypeStruct(into.shape, into.dtype))
        if not fresh:
            in_specs.append(pl.BlockSpec(memory_space=pl.ANY))
            args.append(into)
            aliases = {len(args) - 1: len(out_shape) - 1}
        scratch = [pltpu.VMEM((tm, w), into.dtype) for w, _ in into_cols] + [pltpu.SemaphoreType.DMA((n_into,))]
    if stacked_onto is not None:
        for j, prev in enumerate(stacked_onto):
            in_specs.append(pl.BlockSpec(memory_space=pl.ANY))
            args.append(prev)
            aliases[len(args) - 1] = j
    outs = pl.pallas_call(
        body,
        name=name,
        grid=(R // tm,),
        in_specs=in_specs,
        out_specs=out_specs,
        out_shape=out_shape,
        scratch_shapes=scratch,
        input_output_aliases=aliases,
        compiler_params=pltpu.CompilerParams(dimension_semantics=("arbitrary",), vmem_limit_bytes=VMEM_BIG),
    )(*args)
    return list(outs)


def vjp_fn(fn, n_row, n_out, passthrough=0):
    def bwd(*args):
        rows = [a.astype(F32) for a in args[:n_row]]
        cts = args[n_row:n_row + n_out]
        thru = args[n_row + n_out:n_row + n_out + passthrough]
        pars = args[n_row + n_out + passthrough:]
        outs, pull = jax.vjp(lambda *xs: tuple(fn(*xs)), *rows, *pars)
        grads = pull(tuple(c.astype(o.dtype) for c, o in zip(cts, outs)))
        return tuple(grads[:n_row]) + tuple(thru) + tuple(grads[n_row:])

    return bwd


def _ln(x, g, b):
    mu = jnp.mean(x, axis=-1, keepdims=True)
    xc = x - mu
    var = jnp.mean(xc * xc, axis=-1, keepdims=True)
    return xc * lax.rsqrt(var + LN_EPS) * g + b


def f_ln_in(x, g, b):
    return (_ln(x, g, b),)


def f_deepnorm(x, y, g, b):
    return (_ln(ALPHA * x + y, g, b),)


def _twice(fn):
    return lambda *args: fn(*args) * 2


STREAM = [(D_MODEL, F32), (D_MODEL, MXU_DTYPE)]


def _rope(t, cos, sin):
    parts = []
    for h in range(LIN_HEADS):
        th = t[:, h * LIN_KEY_DIM:(h + 1) * LIN_KEY_DIM]
        parts.append(th * cos + swap_halves(th) * sin)
    return jnp.concatenate(parts, axis=1)


def f_ret_pre(rq, rk, cos, sin):
    return _rope(rq, cos, sin), _rope(rk, cos, sin) * LIN_KEY_DIM ** -0.5


def _log_sigmoid(x):
    return jnp.minimum(x, 0.0) - jnp.log(1.0 + jnp.exp(-jnp.abs(x)))


def f_gla_pre(gq, glr, wlr, blr):
    gate_pre = dot_nn(glr, wlr) + blr
    return gq * LIN_KEY_DIM ** -0.5, _log_sigmoid(gate_pre) / GLA_GATE_NORMALIZER


def f_normgate(ret_raw, rg, gla_raw, gg, gnorm):
    rets, glas = [], []
    for h in range(LIN_HEADS):
        sl = slice(h * LIN_VAL_DIM, (h + 1) * LIN_VAL_DIM)
        t = ret_raw[:, sl]
        mu = jnp.mean(t, axis=-1, keepdims=True)
        tc = t - mu
        rets.append(tc * lax.rsqrt(jnp.mean(tc * tc, axis=-1, keepdims=True) + LN_EPS))
        u = gla_raw[:, sl]
        glas.append(u * lax.rsqrt(jnp.mean(u * u, axis=-1, keepdims=True) + LN_EPS) * gnorm)
    ret = jax.nn.silu(rg) * jnp.concatenate(rets, axis=1)
    gla = jax.nn.silu(gg) * jnp.concatenate(glas, axis=1)
    return ret, gla


def f_merge(p0, p1, p2, gl):
    out = jax.nn.sigmoid(gl[:, :D_MODEL]) * p0
    out = out + jax.nn.sigmoid(gl[:, D_MODEL:2 * D_MODEL]) * p1
    return (out + jax.nn.sigmoid(gl[:, 2 * D_MODEL:]) * p2,)


def f_loss(y, tgt):
    diff = y - tgt
    sq = jnp.sum(diff * diff, axis=0, keepdims=True)
    lanes = sq[:, :128]
    for j in range(1, D_MODEL // 128):
        lanes = lanes + sq[:, j * 128:(j + 1) * 128]
    return diff * (1.0 / D_MODEL), lanes * (0.5 / D_MODEL)


def _adam(g, w, m, v):
    m = ADAM_B1 * m + (1.0 - ADAM_B1) * g
    v = ADAM_B2 * v + (1.0 - ADAM_B2) * jnp.square(g)
    m_hat = m / (1.0 - ADAM_B1 ** ADAM_STEP)
    v_hat = v / (1.0 - ADAM_B2 ** ADAM_STEP)
    delta = -ADAM_LR * (m_hat / (jnp.sqrt(v_hat) + ADAM_EPS) + ADAM_WD * w)
    return delta, m, v


def f_sum_adam(*args):
    g = args[0].astype(F32)
    for p in args[1:-3]:
        g = g + p.astype(F32)
    return (g,) + _adam(g, *args[-3:])


def f_adam(g, w, m, v):
    return _adam(g, w, m, v)


GROUP = 4
GROUP_ROWS = GROUP * CHUNK
GROUP_KEYS = (GROUP + ATTN_LEFT_CHUNKS) * CHUNK
DIAG_ROWS = REL_CLIP + CHUNK
GRAD_ROWS = 200


def _blocks_by_delta():
    blocks = {}
    for qi in range(GROUP):
        for J in range(GROUP_KEYS // 128):
            blocks.setdefault(qi - 2 * J, []).append((qi, J))
    return blocks


def bias_table(rel_bias, side=None):
    n_tab = rel_bias.shape[0] * ATTN_HEADS

    def body(rb_ref, o_ref):
        h = pl.program_id(0)
        rm = lax.broadcasted_iota(jnp.int32, (CHUNK, 128), 0)
        cm = lax.broadcasted_iota(jnp.int32, (CHUNK, 128), 1)
        ahead = jnp.where(cm >= CHUNK, 1, 0)
        for delta, where in _blocks_by_delta().items():
            base = 64 * delta + PAD_ROWS
            rel = jnp.minimum(base + rm - cm, REL_CLIP) + REL_CLIP
            in_band = (ahead - delta >= 0) & (ahead - delta <= ATTN_LEFT_CHUNKS)
            lo = max(REL_MIN, min(base - 127, REL_CLIP) + REL_CLIP)
            hi = min(base + CHUNK - 1, REL_CLIP) + REL_CLIP

            def step(r, acc, rel=rel):
                return jnp.where(rel == r, rb_ref[h * N_REL + r], acc)

            blk = jnp.zeros((CHUNK, 128), F32) if hi < lo else lax.fori_loop(lo, hi + 1, step, jnp.zeros((CHUNK, 128), F32))
            blk = jnp.where(in_band, blk, NEG_INF)
            for qi, J in where:
                o_ref[pl.ds(qi * CHUNK, CHUNK), pl.ds(J * 128, 128)] = blk

    return _side_call(
        side, body,
        name="bias_table",
        grid=(n_tab,),
        in_specs=[pl.BlockSpec(memory_space=pltpu.SMEM)],
        out_specs=[pl.BlockSpec((None, GROUP_ROWS, GROUP_KEYS), lambda h: (h, 0, 0))],
        out_shape=[jax.ShapeDtypeStruct((n_tab, GROUP_ROWS, GROUP_KEYS), F32)],
        scratch_shapes=[],
        args=[rel_bias.reshape(n_tab * N_REL)],
    )[0]


def bias_table_grad(dbias):
    blocks = _blocks_by_delta()
    below_clip = range(-(CHUNK - 1), REL_CLIP)
    deltas = [d for d in sorted(blocks) if any(-127 <= pos - 64 * d - PAD_ROWS <= CHUNK - 1 for pos in below_clip)]

    def body(d_ref, o_ref, diag):
        sums = []
        for delta in deltas:
            acc = None
            for qi, J in blocks[delta]:
                blk = d_ref[pl.ds(qi * CHUNK, CHUNK), pl.ds(J * 128, 128)]
                acc = blk if acc is None else acc + blk
            sums.append(acc)
        off = lax.broadcasted_iota(jnp.int32, (CHUNK, 128), 0) - lax.broadcasted_iota(jnp.int32, (CHUNK, 128), 1)
        diag[...] = jnp.zeros_like(diag)

        def step(t, carry):
            pos = t - (CHUNK - 1)
            picked = jnp.zeros((CHUNK, 128), F32)
            for delta, s in zip(deltas, sums):
                picked = picked + jnp.where(off == pos - 64 * delta - PAD_ROWS, s, 0.0)
            diag[pl.ds(t, 1), :] = jnp.sum(picked, axis=0, keepdims=True)
            return carry

        lax.fori_loop(0, DIAG_ROWS, step, 0, unroll=8)
        d = d_ref[...]
        total = jnp.sum(jnp.sum(d, axis=1, keepdims=True), axis=0, keepdims=True)
        o_ref[pl.ds(0, DIAG_ROWS), :] = jnp.broadcast_to(jnp.sum(diag[...], axis=1, keepdims=True), (DIAG_ROWS, 128))
        o_ref[pl.ds(DIAG_ROWS, GRAD_ROWS - DIAG_ROWS), :] = jnp.broadcast_to(total, (GRAD_ROWS - DIAG_ROWS, 128))

    out = pl.pallas_call(
        body,
        name="bias_table_grad",
        grid=(ATTN_HEADS,),
        in_specs=[pl.BlockSpec((None, GROUP_ROWS, GROUP_KEYS), lambda h: (h, 0, 0))],
        out_specs=pl.BlockSpec((None, GRAD_ROWS, 128), lambda h: (h, 0, 0)),
        out_shape=jax.ShapeDtypeStruct((ATTN_HEADS, GRAD_ROWS, 128), F32),
        scratch_shapes=[pltpu.VMEM((DIAG_ROWS, 128), F32)],
        compiler_params=pltpu.CompilerParams(dimension_semantics=("arbitrary",)),
    )(dbias)
    below = out[:, :DIAG_ROWS - 1, 0]
    clipped = out[:, DIAG_ROWS, 0] - jnp.sum(below, axis=1)
    return jnp.concatenate([jnp.zeros((ATTN_HEADS, REL_MIN), F32), below, clipped[:, None]], axis=1)


def _attn_stage(q_ref, k_ref, v_ref, qs, kp, vp, T):
    qs[...] = (q_ref[...] * ATTN_HEAD_DIM ** -0.5).astype(qs.dtype)
    for dst, src in ((kp, k_ref), (vp, v_ref)):
        dst[pl.ds(0, PAD_ROWS), :] = jnp.zeros((PAD_ROWS, ATTN_HEAD_DIM), dst.dtype)
        dst[pl.ds(PAD_ROWS, T), :] = src[...].astype(dst.dtype)


STRIP = 256


def _strip_scores(g, t, first_groups, r0, qc, kp, b_ref):
    keys = pl.ds(r0 + t * STRIP if first_groups else pl.multiple_of(r0 + t * STRIP, STRIP), STRIP)
    s = _mxu(qc, kp[keys, :], _NT) + b_ref[:, t * STRIP:(t + 1) * STRIP]
    if first_groups:
        k_chunk = lax.shift_right_logical(lax.broadcasted_iota(jnp.int32, (GROUP_ROWS, STRIP), 1), 6)
        s = jnp.where(g * GROUP + t * (STRIP // CHUNK) + k_chunk >= ATTN_LEFT_CHUNKS, s, NEG_INF)
    return s, keys


def _for_groups(T, step):
    n_first = min(ATTN_LEFT_CHUNKS // GROUP, T // GROUP_ROWS)
    for g in range(n_first):
        step(g, True)
    n_loop = T // GROUP_ROWS - n_first
    lax.fori_loop(n_first, T // GROUP_ROWS, lambda g, c: (step(g, False), c)[1], 0, unroll=2 if n_loop % 2 == 0 else 1)


def _head_spec(T, base):
    return pl.BlockSpec((T, ATTN_HEAD_DIM), lambda h: (0, base + h))


def _attn_scratch(T):
    return [pltpu.VMEM((T, ATTN_HEAD_DIM), MXU_DTYPE)] + [pltpu.VMEM((T + PAD_ROWS, ATTN_HEAD_DIM), MXU_DTYPE)] * 2


def attn_fwd(proj, bias_tab, first_tab, side=None):
    T = proj.shape[0]

    def body(q_ref, k_ref, v_ref, b_ref, o_ref, lse_ref, qs, kp, vp):
        _attn_stage(q_ref, k_ref, v_ref, qs, kp, vp, T)

        def step(g, first):
            r0 = g * GROUP_ROWS if first else pl.multiple_of(g * GROUP_ROWS, GROUP_ROWS)
            rows = pl.ds(r0, GROUP_ROWS)
            qc = qs[rows, :]
            m = jnp.full((GROUP_ROWS, 1), NEG_INF, F32)
            l = jnp.zeros((GROUP_ROWS, 1), F32)
            acc = jnp.zeros((GROUP_ROWS, ATTN_HEAD_DIM), F32)
            for t in range(GROUP_KEYS // STRIP):
                s, keys = _strip_scores(g, t, first, r0, qc, kp, b_ref)
                m_new = jnp.maximum(m, jnp.max(s, axis=-1, keepdims=True))
                keep = jnp.exp(m - m_new)
                e = jnp.exp(s - m_new)
                l = l * keep + jnp.sum(e, axis=-1, keepdims=True)
                acc = acc * keep + _mxu(e, vp[keys, :], _NN)
                m = m_new
            o_ref[rows, :] = (acc * (1.0 / l)).astype(o_ref.dtype)
            lse_ref[rows, :] = jnp.broadcast_to(m + jnp.log(l), (GROUP_ROWS, ATTN_HEAD_DIM))

        _for_groups(T, step)

    out, lse = _side_call(
        side, body,
        name="attn_fwd",
        grid=(ATTN_HEADS,),
        in_specs=[_head_spec(T, COL["aq"] // 128), _head_spec(T, COL["ak"] // 128), _head_spec(T, COL["av"] // 128),
                  pl.BlockSpec((None, GROUP_ROWS, GROUP_KEYS), lambda h: (first_tab + h, 0, 0))],
        out_specs=[pl.BlockSpec((T, ATTN_HEAD_DIM), lambda h: (0, h))] * 2,
        out_shape=[jax.ShapeDtypeStruct((T, ATTN_HEADS * ATTN_HEAD_DIM), MXU_DTYPE),
                   jax.ShapeDtypeStruct((T, ATTN_HEADS * ATTN_HEAD_DIM), F32)],
        scratch_shapes=_attn_scratch(T),
        args=[proj, proj, proj, bias_tab],
    )
    return out, lse


def attn_bwd(proj, bias_tab, first_tab, out, lse, dout, dproj, side=None):
    T = proj.shape[0]

    def body(q_ref, k_ref, v_ref, b_ref, do_ref, o_ref, lse_ref, dproj_in, dbias_ref, dproj_ref, qs, kp, vp, dkp, dvp, dq, stage):
        del dproj_in
        h = pl.program_id(0)
        _attn_stage(q_ref, k_ref, v_ref, qs, kp, vp, T)
        dkp[...] = jnp.zeros_like(dkp)
        dvp[...] = jnp.zeros_like(dvp)
        dbias_ref[...] = jnp.zeros_like(dbias_ref)

        def step(g, first):
            r0 = g * GROUP_ROWS if first else pl.multiple_of(g * GROUP_ROWS, GROUP_ROWS)
            rows = pl.ds(r0, GROUP_ROWS)
            qc, do = qs[rows, :], do_ref[rows, :]
            lse = jnp.concatenate([lse_ref[rows, :]] * (STRIP // ATTN_HEAD_DIM), axis=1)
            row_term = jnp.sum(do.astype(F32) * o_ref[rows, :].astype(F32), axis=-1, keepdims=True)
            dq_acc = jnp.zeros((GROUP_ROWS, ATTN_HEAD_DIM), F32)
            for t in range(GROUP_KEYS // STRIP):
                s, keys = _strip_scores(g, t, first, r0, qc, kp, b_ref)
                cols = slice(t * STRIP, (t + 1) * STRIP)
                kb, vb = kp[keys, :], vp[keys, :]
                p = jnp.exp(s - lse)
                dvp[keys, :] += _mxu(p, do, _TN)
                ds = p * (_mxu(do, vb, _NT) - row_term)
                dq_acc = dq_acc + _mxu(ds, kb, _NN)
                dkp[keys, :] += _mxu(ds, qc, _TN)
                dbias_ref[:, cols] += ds
            dq[rows, :] = dq_acc * ATTN_HEAD_DIM ** -0.5

        _for_groups(T, step)
        col = pl.multiple_of(h * ATTN_HEAD_DIM, ATTN_HEAD_DIM)
        for grad, first_row, base in ((dq, 0, "aq"), (dkp, PAD_ROWS, "ak"), (dvp, PAD_ROWS, "av")):
            stage[...] = grad[pl.ds(first_row, T), :].astype(stage.dtype)
            pltpu.sync_copy(stage, dproj_ref.at[:, pl.ds(COL[base] + col, ATTN_HEAD_DIM)])

    dbias, dproj = _side_call(
        side, body,
        name="attn_bwd",
        grid=(ATTN_HEADS,),
        in_specs=[_head_spec(T, COL["aq"] // 128), _head_spec(T, COL["ak"] // 128), _head_spec(T, COL["av"] // 128),
                  pl.BlockSpec((None, GROUP_ROWS, GROUP_KEYS), lambda h: (first_tab + h, 0, 0))]
        + [pl.BlockSpec((T, ATTN_HEAD_DIM), lambda h: (0, h))] * 3 + [pl.BlockSpec(memory_space=pl.ANY)],
        out_specs=[pl.BlockSpec((None, GROUP_ROWS, GROUP_KEYS), lambda h: (h, 0, 0)), pl.BlockSpec(memory_space=pl.ANY)],
        out_shape=[jax.ShapeDtypeStruct((ATTN_HEADS, GROUP_ROWS, GROUP_KEYS), F32),
                   jax.ShapeDtypeStruct(dproj.shape, dproj.dtype)],
        scratch_shapes=_attn_scratch(T) + [pltpu.VMEM((T + PAD_ROWS, ATTN_HEAD_DIM), F32)] * 2
        + [pltpu.VMEM((T, ATTN_HEAD_DIM), F32), pltpu.VMEM((T, ATTN_HEAD_DIM), dproj.dtype)],
        args=[proj, proj, proj, bias_tab, dout, out, lse, dproj],
        aliases={7: 1},
    )
    return dproj, dbias


def _lin_segment(q, k, v, la, st):
    lower = _tri(CHUNK, True)
    steps = (lax.broadcasted_iota(jnp.int32, (CHUNK, 1), 0) + 1).astype(F32)
    outs = []
    for i in range(SEG_CHUNKS):
        sl = slice(i * CHUNK, (i + 1) * CHUNK)
        qi, ki, vi = q[sl], k[sl], v[sl]
        if la.shape[0] == 1:
            cum, last = la * steps, la * float(CHUNK)
        else:
            cum, last = chunk_cumsum(la[sl]), jnp.sum(la[sl], axis=0, keepdims=True)
        e_pos, e_neg = jnp.exp(cum), jnp.exp(-cum)
        q_fwd = qi * e_pos
        s_fwd = dot_nt(q_fwd, ki * e_neg)
        s_bwd = dot_nt(qi * e_neg, ki * e_pos)
        o = dot_nn(jnp.where(lower, s_fwd, s_bwd), vi) + dot_nt(q_fwd, st)
        st = jnp.exp(last) * st + dot_tn(vi, ki * jnp.exp(last - cum))
        outs.append(o)
    return jnp.concatenate(outs, axis=0), st


class Win:
    def __init__(self, arr, width, base=0):
        self.arr, self.width, self.base = arr, width, base


HEADS_PER_STEP = 2


def _lin_spec(w, seg_of):
    assert w.base % HEADS_PER_STEP == 0
    first, width = w.base // HEADS_PER_STEP, HEADS_PER_STEP * w.width
    if w.arr.shape[0] == 1:
        return pl.BlockSpec((1, width), lambda p, s: (0, first + p))
    return pl.BlockSpec((SEG, width), lambda p, s: (seg_of(s), first + p))


def _head_cols(ref, j, width):
    return ref[:, j * width:(j + 1) * width]


def _decay_rows(la_ref, j):
    return _head_cols(la_ref, j, LIN_KEY_DIM)


_STATE_BLOCK = (HEADS_PER_STEP, None, LIN_VAL_DIM, LIN_KEY_DIM)


def lin_fwd(name, q, k, v, la, side=None):
    T = q.arr.shape[0]
    nseg = T // SEG

    def body(q_ref, k_ref, v_ref, la_ref, o_ref, st_ref, st):
        @pl.when(pl.program_id(1) == 0)
        def _():
            st[...] = jnp.zeros_like(st)

        for j in range(HEADS_PER_STEP):
            st0 = st[j]
            st_ref[j] = st0
            o, st1 = _lin_segment(_head_cols(q_ref, j, LIN_KEY_DIM), _head_cols(k_ref, j, LIN_KEY_DIM),
                                  _head_cols(v_ref, j, LIN_VAL_DIM), _decay_rows(la_ref, j), st0)
            o_ref[:, j * LIN_VAL_DIM:(j + 1) * LIN_VAL_DIM] = o
            st[j] = st1

    fwd = lambda s: s
    out, states = _side_call(
        side, body,
        name=name,
        grid=(LIN_HEADS // HEADS_PER_STEP, nseg),
        in_specs=[_lin_spec(w, fwd) for w in (q, k, v, la)],
        out_specs=[pl.BlockSpec((SEG, HEADS_PER_STEP * LIN_VAL_DIM), lambda p, s: (s, p)),
                   pl.BlockSpec(_STATE_BLOCK, lambda p, s: (p, s, 0, 0))],
        out_shape=[jax.ShapeDtypeStruct((T, LIN_HEADS * LIN_VAL_DIM), F32),
                   jax.ShapeDtypeStruct((LIN_HEADS, nseg, LIN_VAL_DIM, LIN_KEY_DIM), F32)],
        scratch_shapes=[pltpu.VMEM((HEADS_PER_STEP, LIN_VAL_DIM, LIN_KEY_DIM), F32)],
        args=[q.arr, k.arr, v.arr, la.arr],
    )
    return out, states


def lin_bwd(name, q, k, v, la, states, dout, with_decay_grad, side=None, into=None, windows=None):
    T = q.arr.shape[0]
    nseg = T // SEG
    n_out = 4 if with_decay_grad else 3

    widths = [LIN_KEY_DIM, LIN_KEY_DIM, LIN_VAL_DIM] + ([LIN_KEY_DIM] if with_decay_grad else [])
    windows = dict(windows or {})
    fresh = [i for i in range(n_out) if i not in windows]
    placed = sorted(windows)
    n_pairs = LIN_HEADS // HEADS_PER_STEP
    rev = lambda s: nseg - 1 - s

    def body(q_ref, k_ref, v_ref, la_ref, st_ref, do_ref, *rest):
        rest = rest[1:] if placed else rest
        outs = dict(zip(fresh, rest[:len(fresh)]))
        rest = rest[len(fresh):]
        into_ref, rest = (rest[0], rest[1:]) if placed else (None, rest)
        dst, stages = rest[0], dict(zip(placed, rest[1:1 + len(placed)]))
        p, s = pl.program_id(0), pl.program_id(1)

        @pl.when(s == 0)
        def _():
            dst[...] = jnp.zeros_like(dst)

        def window_copy(n, i):
            width = HEADS_PER_STEP * widths[i]
            rows = pl.ds(pl.multiple_of(rev(s) * SEG, SEG), SEG)
            cols = pl.ds(pl.multiple_of(windows[i] + p * width, width), width)
            return pltpu.make_async_copy(stages[i], into_ref.at[rows, cols], rest[1 + len(placed)].at[n])

        for n, i in enumerate(placed):
            @pl.when((p > 0) | (s > 0))
            def _(n=n, i=i):
                window_copy(n, i).wait()

        for j in range(HEADS_PER_STEP):
            _, pull = jax.vjp(_lin_segment, _head_cols(q_ref, j, LIN_KEY_DIM).astype(F32),
                              _head_cols(k_ref, j, LIN_KEY_DIM).astype(F32), _head_cols(v_ref, j, LIN_VAL_DIM).astype(F32),
                              _decay_rows(la_ref, j), st_ref[j])
            grads = pull((_head_cols(do_ref, j, LIN_VAL_DIM), dst[j]))
            for i in range(n_out):
                dest = stages[i] if i in windows else outs[i]
                dest[:, j * widths[i]:(j + 1) * widths[i]] = grads[i].astype(dest.dtype)
            dst[j] = grads[4]

        for n, i in enumerate(placed):
            window_copy(n, i).start()

            @pl.when((p == n_pairs - 1) & (s == nseg - 1))
            def _(n=n, i=i):
                window_copy(n, i).wait()

    key_spec = pl.BlockSpec((SEG, HEADS_PER_STEP * LIN_KEY_DIM), lambda p, s: (rev(s), p))
    val_spec = pl.BlockSpec((SEG, HEADS_PER_STEP * LIN_VAL_DIM), lambda p, s: (rev(s), p))
    all_specs = [key_spec, key_spec, val_spec] + ([key_spec] if with_decay_grad else [])
    any_spec = pl.BlockSpec(memory_space=pl.ANY)
    outs = _side_call(
        side, body,
        name=name,
        grid=(n_pairs, nseg),
        in_specs=[_lin_spec(w, rev) for w in (q, k, v, la)] + [
            pl.BlockSpec(_STATE_BLOCK, lambda p, s: (p, rev(s), 0, 0)), val_spec] + ([any_spec] if placed else []),
        out_specs=[all_specs[i] for i in fresh] + ([any_spec] if placed else []),
        out_shape=[jax.ShapeDtypeStruct((T, LIN_HEADS * widths[i]), F32) for i in fresh]
        + ([jax.ShapeDtypeStruct(into.shape, into.dtype)] if placed else []),
        scratch_shapes=[pltpu.VMEM((HEADS_PER_STEP, LIN_VAL_DIM, LIN_KEY_DIM), F32)]
        + [pltpu.VMEM((SEG, HEADS_PER_STEP * widths[i]), into.dtype) for i in placed]
        + ([pltpu.SemaphoreType.DMA((len(placed),))] if placed else []),
        args=[q.arr, k.arr, v.arr, la.arr, states, dout] + ([into] if placed else []),
        aliases={6: len(fresh)} if placed else None,
    )
    return outs


def _coords():
    return lax.axis_index("x"), lax.axis_index("y"), lax.axis_index("c")


def _peer(k):
    x, y, c = _coords()
    px = 1 - x if k & 4 else x
    py = 1 - y if k & 2 else y
    pc = 1 - c if k & 1 else c
    return (px, py, pc), 4 * px + 2 * py + pc


def _remote(src, dst, send_sem, recv_sem, device):
    return pltpu.make_async_remote_copy(src_ref=src, dst_ref=dst, send_sem=send_sem, recv_sem=recv_sem,
                                        device_id=device, device_id_type=pl.DeviceIdType.MESH)


class Side:
    def __init__(self, kind, arrays):
        self.kind, self.arrays, self.n = kind, list(arrays), len(arrays)
        self.outs_per = 1
        self.slots = {"gather": N_DEV - 1, "pair": 4, "chips": 3}[kind]

    def out_shape(self):
        if self.kind == "gather":
            return [jax.ShapeDtypeStruct((N_DEV,) + a.shape, a.dtype) for a in self.arrays]
        return [jax.ShapeDtypeStruct((4,) + a.shape[1:], a.dtype) for a in self.arrays]

    def sems(self):
        return [pltpu.SemaphoreType.DMA((self.n, self.slots)), pltpu.SemaphoreType.DMA((self.n, self.slots)),
                pltpu.SemaphoreType.DMA((self.n,))]

    def start(self, ins, outs, sems):
        send, recv, own = sems
        x, y, c = _coords()
        me = 4 * x + 2 * y + c
        for a in range(self.n):
            if self.kind == "gather":
                pltpu.make_async_copy(ins[a], outs[a].at[me], own.at[a]).start()
                for k in (1, 2, 4, 6):
                    peer, _ = _peer(k)
                    _remote(ins[a], outs[a].at[me], send.at[a, k - 1], recv.at[a, k - 1], peer).start()
            elif self.kind == "pair":
                for j in range(4):
                    _remote(ins[a].at[2 * j + 1 - c], outs[a].at[j], send.at[a, j], recv.at[a, j], (x, y, 1 - c)).start()
            else:
                chip = 2 * x + y
                pltpu.make_async_copy(ins[a].at[chip], outs[a].at[chip], own.at[a]).start()
                for d in (1, 2, 3):
                    tx, ty = (1 - x if d & 2 else x), (1 - y if d & 1 else y)
                    _remote(ins[a].at[2 * tx + ty], outs[a].at[chip], send.at[a, d - 1], recv.at[a, d - 1], (tx, ty, c)).start()

    def finish(self, ins, outs, sems):
        send, recv, own = sems
        x, y, c = _coords()
        me = 4 * x + 2 * y + c
        sibling = (x, y, 1 - c)
        for a in range(self.n):
            if self.kind == "gather":
                blk = outs[a].at[me]
                for k in (2, 4, 6):
                    _, pidx = _peer(k)
                    _remote(blk, blk, send.at[a, k - 1], recv.at[a, k - 1], sibling).wait_recv()
                    _remote(outs[a].at[pidx], outs[a].at[pidx], send.at[a, k], recv.at[a, k], sibling).start()
                pltpu.make_async_copy(ins[a], blk, own.at[a]).wait()
                for k in range(1, N_DEV):
                    _remote(blk, blk, send.at[a, k - 1], recv.at[a, k - 1], sibling).wait_send()
                for k in (1, 3, 5, 7):
                    _remote(blk, blk, send.at[a, k - 1], recv.at[a, k - 1], sibling).wait_recv()
            elif self.kind == "pair":
                for j in range(4):
                    _remote(ins[a].at[j], outs[a].at[j], send.at[a, j], recv.at[a, j], sibling).wait()
            else:
                slab = outs[a].at[0]
                pltpu.make_async_copy(slab, slab, own.at[a]).wait()
                for d in (1, 2, 3):
                    _remote(slab, slab, send.at[a, d - 1], recv.at[a, d - 1], sibling).wait()


def _side_call(side, body, *, name, grid, in_specs, out_specs, out_shape, scratch_shapes, args, semantics=None,
               aliases=None):
    in_specs, out_specs, out_shape = list(in_specs), list(out_specs), list(out_shape)
    scratch_shapes, args = list(scratch_shapes), list(args)
    n_in, n_out = len(in_specs), len(out_specs)
    kernel_body = body
    if side is not None:
        n_so = side.n * side.outs_per

        def kernel_body(*refs):
            ins, side_ins = refs[:n_in], refs[n_in:n_in + side.n]
            pos = n_in + side.n
            outs, side_outs = refs[pos:pos + n_out], refs[pos + n_out:pos + n_out + n_so]
            rest = refs[pos + n_out + n_so:]
            scratch, sems = rest[:-3], rest[-3:]
            ids = [pl.program_id(ax) for ax in range(len(grid))]
            first = functools.reduce(jnp.logical_and, [i == 0 for i in ids])
            last = functools.reduce(jnp.logical_and, [i == g - 1 for i, g in zip(ids, grid)])

            @pl.when(first)
            def _():
                side.start(side_ins, side_outs, sems)

            body(*ins, *outs, *scratch)

            @pl.when(last)
            def _():
                side.finish(side_ins, side_outs, sems)

        any_spec = pl.BlockSpec(memory_space=pl.ANY)
        in_specs += [any_spec] * side.n
        args += side.arrays
        out_specs += [any_spec] * n_so
        out_shape += side.out_shape()
        scratch_shapes += side.sems()
        semantics = ("arbitrary",) * len(grid)
    outs = pl.pallas_call(
        kernel_body,
        name=name,
        grid=grid,
        in_specs=in_specs,
        out_specs=out_specs,
        out_shape=out_shape,
        scratch_shapes=scratch_shapes,
        input_output_aliases=aliases or {},
        compiler_params=pltpu.CompilerParams(dimension_semantics=semantics or ("arbitrary",) * len(grid),
                                             vmem_limit_bytes=VMEM_BIG),
    )(*args)
    if side is not None:
        side.results = list(outs[n_out:])
    return list(outs[:n_out])


def carrier(name, side):
    _side_call(side, lambda: None, name=name, grid=(1,), in_specs=[], out_specs=[], out_shape=[], scratch_shapes=[], args=[])
    return side.results


def all_reduce_small(vec):
    def body(v_ref, o_ref, land, send_sems, recv_sems):
        x, y, c = _coords()
        me = 4 * x + 2 * y + c
        land[me] = v_ref[...]
        copies = []
        for k in range(1, N_DEV):
            peer, _ = _peer(k)
            cp = pltpu.make_async_remote_copy(
                src_ref=v_ref, dst_ref=land.at[me], send_sem=send_sems.at[k - 1], recv_sem=recv_sems.at[k - 1],
                device_id=peer, device_id_type=pl.DeviceIdType.MESH)
            cp.start()
            copies.append(cp)
        for cp in copies:
            cp.wait()
        total = land[0]
        for s in range(1, N_DEV):
            total = total + land[s]
        o_ref[...] = total

    return pl.pallas_call(
        body,
        name="all_reduce_small",
        in_specs=[pl.BlockSpec(memory_space=pltpu.VMEM)],
        out_specs=pl.BlockSpec(memory_space=pltpu.VMEM),
        out_shape=jax.ShapeDtypeStruct(vec.shape, F32),
        scratch_shapes=[pltpu.VMEM((N_DEV,) + vec.shape, F32), pltpu.SemaphoreType.DMA((N_DEV - 1,)),
                        pltpu.SemaphoreType.DMA((N_DEV - 1,))],
    )(vec)


def _pack(parts):
    flat = jnp.concatenate([p.reshape(-1).astype(F32) for p in parts])
    n = flat.shape[0]
    rows = -(-n // 1024) * 8
    return jnp.pad(flat, (0, rows * 128 - n)).reshape(rows, 128)


def _unpack(packed, shapes):
    flat, out, pos = packed.reshape(-1), [], 0
    for s in shapes:
        n = math.prod(s)
        out.append(flat[pos:pos + n].reshape(s))
        pos += n
    return out


def _row2(v):
    return v.reshape(1, -1)


def _rider(hooks, name, ctx=None):
    if name not in hooks:
        return None, lambda: None
    side, done = hooks[name](ctx)
    return side, lambda: done(side.results)


def _layer_forward(h, hb, W, consts, hooks):
    cos, sin, ret_decay = consts
    side, landed = _rider(hooks, "proj")
    proj = mm("proj", hb, W["in"], "nn", tm=4096, tn=512, tk=1024, side=side, out_dtype=PROJ_DTYPE)
    landed()
    bias_tab, first_tab = W["bias_tab"]
    side, landed = _rider(hooks, "attn_fwd")
    attn, attn_lse = attn_fwd(proj, bias_tab, first_tab, side=side)
    landed()
    rq2, rk2 = row_call("ret_pre", f_ret_pre, [Row(proj, 512, COL["rq"] // 512), Row(proj, 512, COL["rk"] // 512),
                                              Row(cos), Row(sin)], [], [(512, F32), (512, F32)], tm=512)
    side, landed = _rider(hooks, "ret_fwd")
    ret_raw, ret_st = lin_fwd("ret_fwd", Win(rq2, 128), Win(rk2, 128), Win(proj, 256, COL["rv"] // 256), Win(ret_decay, 128),
                              side=side)
    landed()
    gq2, gla_la = row_call("gla_pre", f_gla_pre, [Row(proj, 512, COL["gq"] // 512), Row(proj, GLR_BLOCK, GLR_OFF // GLR_BLOCK)],
                           [W["wlr"], W["blr"]], [(512, F32), (512, F32)], tm=512)
    side, landed = _rider(hooks, "gla_fwd")
    gla_raw, gla_st = lin_fwd("gla_fwd", Win(gq2, 128), Win(proj, 128, COL["gk"] // 128), Win(proj, 256, COL["gv"] // 256),
                              Win(gla_la, 128), side=side)
    landed()
    ret_b, gla_b = row_call("normgate", f_normgate, [Row(ret_raw), Row(proj, 1024, COL["rg"] // 1024), Row(gla_raw),
                                                    Row(proj, 1024, COL["gg"] // 1024)], [W["gnorm"]],
                            [(1024, MXU_DTYPE), (1024, MXU_DTYPE)], tm=512)
    branches = (attn, ret_b, gla_b)
    p = [mm("branch_proj", branches[n], W["branch"][n], "nn", tm=1024, tn=1024, tk=1024) for n in range(3)]
    merged, = row_call("merge", f_merge, [Row(p[0]), Row(p[1]), Row(p[2]), Row(proj, GATES_W, GATES_OFF // GATES_W)], [],
                       [(1024, MXU_DTYPE)], tm=512)
    post_norm = lambda y, x, g, b: f_deepnorm(x, y, g, b) * 2 + (y,)
    x1, x1b, mix = mm("out_proj_ln1", merged, W["out"], "nn", tm=512, tn=D_MODEL, tk=1024, extras=(h,),
                      row_params=(W["ln1_g"], W["ln1_b"]), epi=post_norm, n_out=3, out_dtypes=(F32, MXU_DTYPE, F32))
    side, landed = _rider(hooks, "mlp_up")
    u, act = mm("mlp_up", x1b, W["up"], "nn", tm=2048, tn=512, tk=1024, n_out=2, out_dtypes=(F32, MXU_DTYPE),
                epi=lambda r: (r, jnp.square(jnp.maximum(r, 0.0))), side=side)
    landed()
    x2, x2b, y = mm("mlp_down_ln2", act, W["down"], "nn", tm=1024, tn=D_MODEL, tk=1024, extras=(x1,),
                    row_params=(W["ln2_g"], W["ln2_b"]), epi=post_norm, n_out=3, out_dtypes=(F32, MXU_DTYPE, F32))
    saved = dict(h=h, hb=hb, proj=proj, attn_lse=attn_lse, rq2=rq2, rk2=rk2, ret_st=ret_st, gq2=gq2, gla_la=gla_la,
                 gla_st=gla_st, ret_raw=ret_raw, gla_raw=gla_raw, branches=branches, p=p, merged=merged, mix=mix, x1=x1, x1b=x1b,
                 u=u, act=act, y=y)
    return x2, x2b, saved


def _layer_backward(dx2, S, W, consts, hooks):
    cos, sin, ret_decay = consts
    h, proj, T = S["h"], S["proj"], S["h"].shape[0]
    add = lambda r, e: r + e
    dx1a, dy, dg2, db2 = row_call("ln2_bwd", vjp_fn(f_deepnorm, 2, 1), [Row(S["x1"]), Row(S["y"]), Row(dx2)],
                                  [W["ln2_g"], W["ln2_b"]], [(1024, F32), (1024, MXU_DTYPE)], [(1, 1024), (1, 1024)], tm=512)
    side, landed = _rider(hooks, "mlp_down_bwd")
    du = mm("mlp_down_bwd", dy, W["down"], "nt", tm=1024, tn=512, tk=1024, extras=(S["u"],),
            epi=lambda r, u: r * (2.0 * jnp.maximum(u, 0.0)), out_dtype=MXU_DTYPE, side=side)
    landed()
    g_down = mm("mlp_down_wgrad", S["act"], dy, "tn", tm=1024, tn=1024, tk=2048, out_dtype=WIRE_DTYPE)
    dx1 = mm("mlp_up_bwd", du, W["up"], "nt", tm=1024, tn=1024, tk=2048, extras=(dx1a,), epi=add)
    g_up = mm("mlp_up_wgrad", S["x1b"], du, "tn", tm=1024, tn=D_FF // N_DEV, tk=2048, out_dtype=WIRE_DTYPE,
              column_slabs=True)
    dha, dmix, dg1, db1 = row_call("ln1_bwd", vjp_fn(f_deepnorm, 2, 1), [Row(h), Row(S["mix"]), Row(dx1)],
                                   [W["ln1_g"], W["ln1_b"]], [(1024, F32), (1024, MXU_DTYPE)], [(1, 1024), (1, 1024)], tm=512)
    dmerged = mm("out_proj_bwd", dmix, W["out"], "nt", tm=1024, tn=1024, tk=1024)
    g_out = mm("out_proj_wgrad", S["merged"], dmix, "tn", tm=1024, tn=1024, tk=2048, out_dtype=WIRE_DTYPE)
    dproj = jax.ShapeDtypeStruct((T, PROJ_PAD), MXU_DTYPE)
    p = S["p"]
    dp0, dp1, dp2, dproj = row_call(
        "merge_bwd", vjp_fn(f_merge, 4, 1), [Row(p[0]), Row(p[1]), Row(p[2]), Row(proj, GATES_W, GATES_OFF // GATES_W), Row(dmerged)],
        [], [(1024, MXU_DTYPE)] * 3, tm=256, into=dproj, into_cols=[(GATES_W, GATES_OFF)])
    dps = (dp0, dp1, dp2)
    dbr = [mm("branch_proj_bwd", dps[n], W["branch"][n], "nt", tm=1024, tn=1024, tk=1024,
              out_dtype=MXU_DTYPE if n == 0 else F32) for n in range(3)]
    g_branch = [mm("branch_proj_wgrad", S["branches"][n], dps[n], "tn", tm=1024, tn=1024, tk=2048, out_dtype=WIRE_DTYPE)
                for n in range(3)]
    dret_raw, dgla_raw, dgnorm, dproj = row_call(
        "normgate_bwd", vjp_fn(lambda a, b, c, d, g: _reorder_normgate(a, b, c, d, g), 4, 2),
        [Row(S["ret_raw"]), Row(S["gla_raw"]), Row(proj, 1024, COL["rg"] // 1024), Row(proj, 1024, COL["gg"] // 1024),
         Row(dbr[1]), Row(dbr[2])], [W["gnorm"]], [(1024, F32), (1024, F32)], [(1, LIN_VAL_DIM)], tm=256,
        into=dproj, into_cols=[(1024, COL["rg"]), (1024, COL["gg"])])
    side, landed = _rider(hooks, "gla_bwd")
    dgq2, dgla_la, dproj = lin_bwd("gla_bwd", Win(S["gq2"], 128), Win(proj, 128, COL["gk"] // 128),
                                   Win(proj, 256, COL["gv"] // 256), Win(S["gla_la"], 128), S["gla_st"], dgla_raw, True,
                                   side=side, into=dproj, windows={1: COL["gk"], 2: COL["gv"]})
    landed()
    dwlr, dblr, dproj = row_call(
        "gla_pre_bwd", _gla_pre_bwd, [Row(proj, 512, COL["gq"] // 512), Row(proj, GLR_BLOCK, GLR_OFF // GLR_BLOCK),
                                      Row(dgq2), Row(dgla_la)], [W["wlr"], W["blr"]], [],
        [(GLR_BLOCK, 512), (1, 512)], tm=512, into=dproj,
        into_cols=[(512, COL["gq"]), (PROJ_PAD - GLR_OFF, GLR_OFF)])
    side, landed = _rider(hooks, "ret_bwd", dict(branch=g_branch, out=g_out, up=g_up, down=g_down))
    drq2, drk2, dproj = lin_bwd("ret_bwd", Win(S["rq2"], 128), Win(S["rk2"], 128), Win(proj, 256, COL["rv"] // 256),
                                Win(ret_decay, 128), S["ret_st"], dret_raw, False, side=side, into=dproj,
                                windows={2: COL["rv"]})
    landed()
    dproj, = row_call(
        "ret_pre_bwd", _ret_pre_bwd, [Row(proj, 512, COL["rq"] // 512), Row(proj, 512, COL["rk"] // 512), Row(cos), Row(sin),
                                      Row(drq2), Row(drk2)], [], [], tm=512, into=dproj,
        into_cols=[(512, COL["rq"]), (512, COL["rk"])])
    side, landed = _rider(hooks, "attn_bwd")
    dproj, dbias = attn_bwd(proj, *W["bias_tab"], S["branches"][0], S["attn_lse"], dbr[0], dproj, side=side)
    landed()
    d_rel_bias = bias_table_grad(dbias)
    side, landed = _rider(hooks, "proj_wgrad", dict(branch=g_branch, out=g_out, up=g_up, down=g_down))
    g_in = mm("proj_wgrad", S["hb"], dproj, "tn", tm=1024, tn=1280, tk=2048, out_dtype=WIRE_DTYPE, side=side)
    landed()
    side, landed = _rider(hooks, "proj_bwd", {"in": g_in})
    dh = mm("proj_bwd", dproj, W["in"], "nt", tm=1024, tn=1024, tk=2560, extras=(dha,), epi=add, side=side)
    landed()
    big = dict(g_in=g_in, branch=g_branch, out=g_out, up=g_up, down=g_down)
    small = dict(rel_bias=d_rel_bias, wlr=dwlr[:GLA_RANK], blr=dblr, gnorm=dgnorm, ln1_g=dg1, ln1_b=db1, ln2_g=dg2, ln2_b=db2)
    return dh, big, small


def _reorder_normgate(ret_raw, gla_raw, rg, gg, gnorm):
    return f_normgate(ret_raw, rg, gla_raw, gg, gnorm)


def _gla_pre_bwd(gq, glr, dq2, dla, wlr, blr):
    _, pull = jax.vjp(f_gla_pre, gq, glr, wlr, blr)
    dgq, dglr, dwlr, dblr = pull((dq2, dla))
    dglr = jnp.concatenate([dglr, jnp.zeros((dglr.shape[0], PROJ_PAD - GLR_OFF - GLR_BLOCK), F32)], axis=1)
    return dgq, dglr, dwlr, dblr


def _ret_pre_bwd(rq, rk, cos, sin, drq2, drk2):
    _, pull = jax.vjp(lambda a, b: f_ret_pre(a, b, cos, sin), rq, rk)
    return pull((drq2, drk2))


SHARDED = ("in", "branch", "out", "up", "down")

_FULL = {
    "branch": lambda g: [g[:, n].reshape(D_MODEL, D_MODEL) for n in range(3)],
    "out": lambda g: g.reshape(D_MODEL, D_MODEL),
    "up": lambda g: jnp.transpose(g, (1, 0, 2)).reshape(D_MODEL, D_FF),
    "down": lambda g: g.reshape(D_FF, D_MODEL),
}


def _owner_pieces(s):
    shard = PROJ_WIDTH // N_DEV
    lo, hi = s * shard, (s + 1) * shard
    pieces = []
    for a, b, padded in ((0, GLR_SRC, 0), (GLR_SRC, GATES_SRC, GLR_OFF), (GATES_SRC, PROJ_WIDTH, GATES_OFF)):
        x0, x1 = max(lo, a), min(hi, b)
        if x0 < x1:
            pieces.append((x0 - lo, x1 - x0, padded + x0 - a))
    return pieces


def _full_in(g):
    placed = sorted((p, s, c, w) for s in range(N_DEV) for c, w, p in _owner_pieces(s))
    cols = [g[s][:, c:c + w] for _, s, c, w in placed]
    return jnp.concatenate(cols + [jnp.zeros((D_MODEL, PROJ_PAD - PROJ_WIDTH), g.dtype)], axis=1)


def _full_wlr(g):
    wlr = jnp.transpose(g, (1, 0, 2)).reshape(GLA_RANK, LIN_HEADS * LIN_KEY_DIM)
    return jnp.pad(wlr, ((0, GLR_BLOCK - GLA_RANK), (0, 0)))


def _slab_in(g):
    return jnp.stack([jnp.concatenate([g[:, p:p + w] for _, w, p in _owner_pieces(s)], axis=1) for s in range(N_DEV)])


_SLABS = {
    "in": _slab_in,
    "branch": lambda gs: jnp.stack([b.reshape(N_DEV, D_MODEL // N_DEV, D_MODEL) for b in gs], axis=1),
    "out": lambda g: g.reshape(N_DEV, D_MODEL // N_DEV, D_MODEL),
    "up": lambda g: g,
    "down": lambda g: g.reshape(N_DEV, D_FF // N_DEV, D_MODEL),
}


def _pair_sums(name, slabs, from_sibling=None):
    sums = []
    for mine, sibs in zip(slabs, from_sibling or carrier(name, Side("pair", slabs))):
        width = mine.shape[-1]
        rows = math.prod(mine.shape[1:-1])
        tm = 256 if rows % 256 == 0 else 128
        nb = rows // tm
        my_rows = Row(mine.reshape(-1, width), rowmap=lambda i, nb=nb: (2 * (i // nb) + lax.axis_index("c")) * nb + i % nb)
        out, = row_call("pair_sum", lambda q, p: (p.astype(F32) + q.astype(F32),),
                        [Row(sibs.reshape(-1, width)), my_rows], [], [(width, WIRE_DTYPE)], tm=tm)
        sums.append(out.reshape(sibs.shape))
    return sums


def _sum_adam(name, parts, w, m, v, layer, tm, earlier):
    C = w.shape[-1]
    rows = [Row(parts, lead=(s,)) for s in range(parts.shape[0])] + [Row(a, lead=(layer,)) for a in (w, m, v)]
    return row_call(name, f_sum_adam, rows, [], [(C, F32)] * 4, tm=tm, stack=(layer, DEPTH), stacked_onto=earlier)


def kernel(x, ln_in_g, ln_in_b, w_in, rel_bias, gla_w_lr, gla_b_lr, gla_norm_g, w_branch, w_out, ln1_g, ln1_b, w_up, w_down, ln2_g, ln2_b, loss_target, m_ln_in_g, m_ln_in_b, m_w_in, m_rel_bias, m_gla_w_lr, m_gla_b_lr, m_gla_norm_g, m_w_branch, m_w_out, m_ln1_g, m_ln1_b, m_w_up, m_w_down, m_ln2_g, m_ln2_b, v_ln_in_g, v_ln_in_b, v_w_in, v_rel_bias, v_gla_w_lr, v_gla_b_lr, v_gla_norm_g, v_w_branch, v_w_out, v_ln1_g, v_ln1_b, v_w_up, v_w_down, v_ln2_g, v_ln2_b):
    T = x.shape[1]
    x2d = x.reshape(T, D_MODEL)
    tgt = loss_target.reshape(T, D_MODEL)
    me = 4 * lax.axis_index("x") + 2 * lax.axis_index("y") + lax.axis_index("c")

    inv = ROPE_BASE ** (-jnp.arange(0, LIN_KEY_DIM, 2, dtype=F32) / LIN_KEY_DIM)
    ang = jnp.arange(T, dtype=F32)[:, None] * inv[None, :]
    cos = jnp.concatenate([jnp.cos(ang), jnp.cos(ang)], axis=1)
    sin = jnp.concatenate([-jnp.sin(ang), jnp.sin(ang)], axis=1)
    log_gamma = jnp.log1p(-jnp.exp2(-5.0 - jnp.arange(LIN_HEADS, dtype=F32)))
    ret_decay = jnp.repeat(log_gamma, LIN_KEY_DIM).reshape(1, LIN_HEADS * LIN_KEY_DIM)
    consts = (cos, sin, ret_decay)

    layers = [dict(blr=_row2(gla_b_lr[l]), gnorm=_row2(gla_norm_g[l]), ln1_g=_row2(ln1_g[l]),
                   ln1_b=_row2(ln1_b[l]), ln2_g=_row2(ln2_g[l]), ln2_b=_row2(ln2_b[l])) for l in range(DEPTH)]
    shards = dict(branch=w_branch, out=w_out, up=w_up, down=w_down)
    shards["in"] = w_in
    wire = [{n: shards[n][l].astype(WIRE_DTYPE) for n in SHARDED} for l in range(DEPTH)]

    def gather_of(l, names):
        def landed(res):
            for n, g in zip(names, res):
                layers[l][n] = _full_in(g) if n == "in" else _FULL[n](g)
        return lambda ctx: (Side("gather", [wire[l][n] for n in names]), landed)

    first = Side("gather", [wire[0]["in"]] + [gla_w_lr[l] for l in range(DEPTH)])
    bias_tabs = bias_table(rel_bias, side=first)
    layers[0]["in"] = _full_in(first.results[0])
    for l in range(DEPTH):
        layers[l]["wlr"] = _full_wlr(first.results[1 + l])
        layers[l]["bias_tab"] = (bias_tabs, l * ATTN_HEADS)

    fwd_hooks = [{"proj": gather_of(1, ("in",)), "attn_fwd": gather_of(0, ("up", "down")),
                  "ret_fwd": gather_of(0, ("branch", "out"))},
                 {"proj": gather_of(1, ("up", "down")), "attn_fwd": gather_of(1, ("branch", "out"))}]
    h, hb = row_call("ln_in", _twice(f_ln_in), [Row(x2d)], [_row2(ln_in_g), _row2(ln_in_b)], STREAM, tm=512)
    saved = []
    for l in range(DEPTH):
        h, hb, s = _layer_forward(h, hb, layers[l], consts, fwd_hooks[l])
        saved.append(s)
    dh, loss_lanes = row_call("loss", f_loss, [Row(h), Row(tgt)], [], [(D_MODEL, F32)], [(1, 128)], tm=512)

    received = [dict() for _ in range(DEPTH)]

    def scatter_of(l, names, sums=None):
        def make(ctx):
            parts = [sums[n] for n in names] if sums is not None else _pair_sums(
                "pair_grads", [_SLABS[n](ctx[n]) for n in names])

            def landed(res):
                received[l].update(zip(names, res))
            return Side("chips", parts), landed
        return make

    small = [None] * DEPTH
    dh, big, small[1] = _layer_backward(dh, saved[1], layers[1], consts, {})
    slabs1 = [_SLABS[n](big["g_in"] if n == "in" else big[n]) for n in SHARDED]
    sums1 = {}

    def pair_exchange_1(ctx):
        def landed(res):
            sums1.update(zip(SHARDED, _pair_sums("pair_grads", slabs1, from_sibling=res)))
        return Side("pair", slabs1), landed

    rest = SHARDED[1:]
    sums0 = {}

    def pair_exchange_0(ctx):
        slabs0 = [_SLABS[n](ctx[n]) for n in rest]

        def landed(res):
            sums0.update(zip(rest, _pair_sums("pair_grads", slabs0, from_sibling=res)))
        return Side("pair", slabs0), landed

    dh, big, small[0] = _layer_backward(dh, saved[0], layers[0], consts, {
        "mlp_down_bwd": pair_exchange_1, "gla_bwd": scatter_of(1, ("in",), sums1), "ret_bwd": pair_exchange_0,
        "attn_bwd": scatter_of(1, rest, sums1), "proj_wgrad": scatter_of(0, rest, sums0), "proj_bwd": scatter_of(0, ("in",))})
    grad_x, d_in_g, d_in_b = row_call("ln_in_bwd", vjp_fn(f_ln_in, 1, 1), [Row(x2d), Row(dh)],
                                      [_row2(ln_in_g), _row2(ln_in_b)], [(D_MODEL, F32)], [(1, D_MODEL), (1, D_MODEL)], tm=256)

    small_names = ("rel_bias", "wlr", "blr", "gnorm", "ln1_g", "ln1_b", "ln2_g", "ln2_b")
    parts = [d_in_g, d_in_b] + [small[l][n] for l in range(DEPTH) for n in small_names] + [loss_lanes]
    shapes = [(D_MODEL,), (D_MODEL,)] + [s for l in range(DEPTH) for s in (
        (ATTN_HEADS, N_REL), (GLA_RANK, 512), (512,), (LIN_VAL_DIM,), (D_MODEL,), (D_MODEL,), (D_MODEL,), (D_MODEL,))] + [(128,)]
    red = _unpack(all_reduce_small(_pack(parts)), shapes)
    loss = jnp.sum(red[-1])
    g_small = {"ln_in_g": red[0], "ln_in_b": red[1]}
    for l in range(DEPTH):
        for j, n in enumerate(small_names):
            g_small[(l, n)] = red[2 + l * len(small_names) + j]
    shard = LIN_HEADS * LIN_KEY_DIM // N_DEV
    g_wlr = jnp.stack([lax.dynamic_slice_in_dim(g_small[(l, "wlr")], me * shard, shard, axis=1) for l in range(DEPTH)])
    stack = lambda n: jnp.stack([g_small[(l, n)] for l in range(DEPTH)])
    g_rep = [g_small["ln_in_g"], g_small["ln_in_b"], stack("rel_bias"), g_wlr, stack("blr"), stack("gnorm"),
             stack("ln1_g"), stack("ln1_b"), stack("ln2_g"), stack("ln2_b")]
    w_rep = [ln_in_g, ln_in_b, rel_bias, gla_w_lr, gla_b_lr, gla_norm_g, ln1_g, ln1_b, ln2_g, ln2_b]
    m_rep = [m_ln_in_g, m_ln_in_b, m_rel_bias, m_gla_w_lr, m_gla_b_lr, m_gla_norm_g, m_ln1_g, m_ln1_b, m_ln2_g, m_ln2_b]
    v_rep = [v_ln_in_g, v_ln_in_b, v_rel_bias, v_gla_w_lr, v_gla_b_lr, v_gla_norm_g, v_ln1_g, v_ln1_b, v_ln2_g, v_ln2_b]
    rep_shapes = [w.shape for w in w_rep]
    d_pack, m_pack, v_pack = row_call("adam_small", f_adam, [Row(_pack(g_rep)), Row(_pack(w_rep)), Row(_pack(m_rep)), Row(_pack(v_rep))],
                                      [], [(128, F32)] * 3, tm=256)
    d_rep, nm_rep, nv_rep = _unpack(d_pack, rep_shapes), _unpack(m_pack, rep_shapes), _unpack(v_pack, rep_shapes)

    def owned(key, w, m, v, name, tm):
        C = w.shape[-1]
        flat = lambda a: a.reshape(DEPTH, -1, C)
        outs = None
        for l in range(DEPTH):
            outs = _sum_adam(name, received[l][key].reshape(4, -1, C), flat(w), flat(m), flat(v), l, tm, outs)
        return [o.reshape(w.shape) for o in outs]

    o_in = owned("in", w_in, m_w_in, v_w_in, "adam_w_in", 256)
    o_br = owned("branch", w_branch, m_w_branch, v_w_branch, "adam_w_branch", 128)
    o_out = owned("out", w_out, m_w_out, v_w_out, "adam_w_out", 128)
    o_up = owned("up", w_up, m_w_up, v_w_up, "adam_w_up", 256)
    o_down = owned("down", w_down, m_w_down, v_w_down, "adam_w_down", 256)

    def tree(j, rep):
        return [rep[0], rep[1], o_in[j], rep[2], rep[3], rep[4], rep[5], o_br[j], o_out[j], rep[6], rep[7], o_up[j], o_down[j],
                rep[8], rep[9]]

    return (loss, grad_x.reshape(1, T, D_MODEL), *tree(0, g_rep), *tree(1, d_rep), *tree(2, nm_rep), *tree(3, nv_rep))
```

```python
import functools
import math

import jax
import jax.numpy as jnp
from jax import lax
from jax.experimental import pallas as pl
from jax.experimental.pallas import tpu as pltpu

F32 = jnp.float32
MXU_DTYPE = jnp.bfloat16
WIRE_DTYPE = jnp.bfloat16
PROJ_DTYPE = jnp.bfloat16

N_DEV = 8
D_MODEL = 1024
DEPTH = 2
CHUNK = 64
ATTN_HEADS = 8
ATTN_HEAD_DIM = 128
ATTN_LEFT_CHUNKS = 8
BAND = (ATTN_LEFT_CHUNKS + 1) * CHUNK
PAD_ROWS = ATTN_LEFT_CHUNKS * CHUNK
REL_CLIP = 2 * CHUNK
N_REL = 2 * REL_CLIP + 1
REL_MIN = REL_CLIP - (CHUNK - 1)
LIN_HEADS = 4
LIN_KEY_DIM = 128
LIN_VAL_DIM = 256
GLA_RANK = 16
GLA_GATE_NORMALIZER = 16.0
ROPE_BASE = 10000.0
D_FF = 4 * D_MODEL
LN_EPS = 1e-5
NEG_INF = -1e30
ALPHA = (2 * DEPTH) ** 0.25
ADAM_LR, ADAM_B1, ADAM_B2, ADAM_EPS, ADAM_WD, ADAM_STEP = 0.001, 0.9, 0.999, 1e-08, 0.01, 10

PROJ_WIDTH = 12304
GLR_SRC = 9216
GATES_SRC = GLR_SRC + GLA_RANK
MAIN_W = 9216
GATES_W = 3 * D_MODEL
GATES_OFF = MAIN_W
GLR_OFF = MAIN_W + GATES_W
PROJ_PAD = 12800
GLR_BLOCK = 128
COL = dict(aq=0, ak=1024, av=2048, rq=3072, rk=3584, rv=4096, rg=5120, gq=6144, gk=6656, gv=7168, gg=8192)

SEG_CHUNKS = 8
SEG = SEG_CHUNKS * CHUNK
VMEM_BIG = 56 * 1024 * 1024

_NN = (((1,), (0,)), ((), ()))
_NT = (((1,), (1,)), ((), ()))
_TN = (((0,), (0,)), ((), ()))


def _mxu(a, b, dn):
    return lax.dot_general(a.astype(MXU_DTYPE), b.astype(MXU_DTYPE), dn, preferred_element_type=F32)


@jax.custom_vjp
def dot_nn(a, b):
    return _mxu(a, b, _NN)


def _dot_nn_fwd(a, b):
    return _mxu(a, b, _NN), (a, b)


def _dot_nn_bwd(res, g):
    a, b = res
    return _mxu(g, b, _NT).astype(a.dtype), _mxu(a, g, _TN).astype(b.dtype)


dot_nn.defvjp(_dot_nn_fwd, _dot_nn_bwd)


@jax.custom_vjp
def dot_nt(a, b):
    return _mxu(a, b, _NT)


def _dot_nt_fwd(a, b):
    return _mxu(a, b, _NT), (a, b)


def _dot_nt_bwd(res, g):
    a, b = res
    return _mxu(g, b, _NN).astype(a.dtype), _mxu(g, a, _TN).astype(b.dtype)


dot_nt.defvjp(_dot_nt_fwd, _dot_nt_bwd)


@jax.custom_vjp
def dot_tn(a, b):
    return _mxu(a, b, _TN)


def _dot_tn_fwd(a, b):
    return _mxu(a, b, _TN), (a, b)


def _dot_tn_bwd(res, g):
    a, b = res
    return _mxu(b, g, _NT).astype(a.dtype), _mxu(a, g, _NN).astype(b.dtype)


dot_tn.defvjp(_dot_tn_fwd, _dot_tn_bwd)


def _tri(n, lower):
    r = lax.broadcasted_iota(jnp.int32, (n, n), 0)
    c = lax.broadcasted_iota(jnp.int32, (n, n), 1)
    return (r >= c) if lower else (r <= c)


def _tri_dot(lower, x):
    tri = _tri(x.shape[0], lower).astype(jnp.bfloat16)
    hi = x.astype(jnp.bfloat16)
    rest = x - hi.astype(F32)
    mid = rest.astype(jnp.bfloat16)
    lo = (rest - mid.astype(F32)).astype(jnp.bfloat16)
    dot = lambda part: lax.dot_general(tri, part, _NN, preferred_element_type=F32)
    return dot(hi) + dot(mid) + dot(lo)


@jax.custom_vjp
def chunk_cumsum(x):
    return _tri_dot(True, x)


def _chunk_cumsum_fwd(x):
    return chunk_cumsum(x), None


def _chunk_cumsum_bwd(_, g):
    return (_tri_dot(False, g),)


chunk_cumsum.defvjp(_chunk_cumsum_fwd, _chunk_cumsum_bwd)


@jax.custom_vjp
def swap_halves(t):
    return pltpu.roll(t, t.shape[-1] // 2, t.ndim - 1)


def _swap_fwd(t):
    return swap_halves(t), None


def _swap_bwd(_, g):
    return (swap_halves(g),)


swap_halves.defvjp(_swap_fwd, _swap_bwd)


def mm(name, a, b, dims, *, tm, tn, tk, out_dtype=F32, extras=(), row_params=(), epi=None, n_out=1, out_dtypes=None,
       side=None, column_slabs=False):
    if dims == "nn":
        (M, K), N = a.shape, b.shape[1]
    elif dims == "nt":
        (M, K), N = a.shape, b.shape[0]
    else:
        (K, M), N = a.shape, b.shape[1]
    tm, tn, tk = min(tm, M), min(tn, N), min(tk, K)
    assert M % tm == 0 and N % tn == 0 and K % tk == 0, (name, M, N, K, tm, tn, tk)
    nk = K // tk
    dn = {"nn": _NN, "nt": _NT, "tn": _TN}[dims]
    a_spec = pl.BlockSpec((tk, tm), lambda i, j, k: (k, i)) if dims == "tn" else pl.BlockSpec((tm, tk), lambda i, j, k: (i, k))
    b_spec = pl.BlockSpec((tn, tk), lambda i, j, k: (j, k)) if dims == "nt" else pl.BlockSpec((tk, tn), lambda i, j, k: (k, j))
    o_spec = pl.BlockSpec((tm, tn), lambda i, j, k: (i, j))
    row_spec = pl.BlockSpec((1, tn), lambda i, j, k: (0, j))
    n_ex = len(extras) + len(row_params)
    out_dtypes = out_dtypes or (out_dtype,) * n_out

    def finish(r, ex_refs, o_refs):
        if epi is not None:
            r = epi(r, *[e[...] for e in ex_refs])
        r = r if isinstance(r, tuple) else (r,)
        for o, v in zip(o_refs, r):
            o[...] = v.astype(o.dtype)

    def body(*refs):
        a_ref, b_ref = refs[:2]
        ex_refs = refs[2:2 + n_ex]
        o_refs = refs[2 + n_ex:2 + n_ex + n_out]
        part = _mxu(a_ref[...], b_ref[...], dn)
        if nk == 1:
            finish(part, ex_refs, o_refs)
            return
        acc = refs[-1]
        k = pl.program_id(2)

        @pl.when(k == 0)
        def _():
            acc[...] = part

        @pl.when(k > 0)
        def _():
            acc[...] += part

        @pl.when(k == nk - 1)
        def _():
            finish(acc[...], ex_refs, o_refs)

    outs = _side_call(
        side, body,
        name=name,
        grid=(M // tm, N // tn, nk),
        in_specs=[a_spec, b_spec] + [o_spec] * len(extras) + [row_spec] * len(row_params),
        out_specs=[pl.BlockSpec((None, tm, tn), lambda i, j, k: (j, i, 0))] * n_out if column_slabs else [o_spec] * n_out,
        out_shape=[jax.ShapeDtypeStruct((N // tn, M, tn) if column_slabs else (M, N), dt) for dt in out_dtypes],
        scratch_shapes=[pltpu.VMEM((tm, tn), F32)] if nk > 1 else [],
        args=[a, b, *extras, *row_params],
        semantics=("parallel", "parallel", "arbitrary"),
    )
    return outs[0] if n_out == 1 else outs


class Row:
    def __init__(self, arr, width=None, col=0, lead=(), rowmap=None):
        self.arr, self.lead, self.col = arr, tuple(lead), col
        self.width = arr.shape[-1] if width is None else width
        self.rowmap = rowmap

    def spec(self, tm):
        lead, col, rowmap = self.lead, self.col, self.rowmap or (lambda i: i)
        return pl.BlockSpec((None,) * len(lead) + (tm, self.width), lambda i: (*lead, rowmap(i), col))


def row_call(name, fn, rows, params, new_outs, red_shapes=(), *, tm, into=None, into_cols=(), stack=None, stacked_onto=None):
    R = rows[0].arr.shape[len(rows[0].lead)]
    tm = min(tm, R)
    assert R % tm == 0, (name, R, tm)
    n_row, n_par, n_new, n_into, n_red = len(rows), len(params), len(new_outs), len(into_cols), len(red_shapes)
    fresh = isinstance(into, jax.ShapeDtypeStruct)

    def body(*refs):
        row_refs = refs[:n_row]
        par_refs = refs[n_row:n_row + n_par]
        pos = n_row + n_par + (1 if (into is not None and not fresh) else 0) + (n_new if stacked_onto is not None else 0)
        new_refs = refs[pos:pos + n_new]
        red_refs = refs[pos + n_new:pos + n_new + n_red]
        pos2 = pos + n_new + n_red
        into_ref = refs[pos2] if into is not None else None
        stage = refs[pos2 + 1:pos2 + 1 + n_into] if into is not None else ()
        i = pl.program_id(0)
        widen = lambda t: t.astype(F32) if jnp.issubdtype(t.dtype, jnp.floating) else t
        vals = fn(*[widen(r[...]) for r in row_refs], *[p[...] for p in par_refs])
        vals = vals if isinstance(vals, (tuple, list)) else (vals,)
        for o, v in zip(new_refs, vals[:n_new]):
            o[...] = v.astype(o.dtype)

        def window_copy(j, step):
            width, off = into_cols[j]
            rows_at = pl.ds(pl.multiple_of(step * tm, tm), tm)
            return pltpu.make_async_copy(stage[j], into_ref.at[rows_at, pl.ds(off, width)], refs[pos2 + 1 + n_into].at[j])

        for j, v in enumerate(vals[n_new:n_new + n_into]):
            @pl.when(i > 0)
            def _(j=j):
                window_copy(j, i - 1).wait()

            stage[j][...] = v.astype(stage[j].dtype)
            window_copy(j, i).start()

            @pl.when(i == R // tm - 1)
            def _(j=j):
                window_copy(j, i).wait()
        for o, v in zip(red_refs, vals[n_new + n_into:]):
            @pl.when(i == 0)
            def _(o=o, v=v):
                o[...] = v.astype(o.dtype)

            @pl.when(i > 0)
            def _(o=o, v=v):
                o[...] += v.astype(o.dtype)

    in_specs = [r.spec(tm) for r in rows] + [pl.BlockSpec(p.shape, lambda i, nd=p.ndim: (0,) * nd) for p in params]
    args = [r.arr for r in rows] + list(params)
    if stack is None:
        out_specs = [pl.BlockSpec((tm, w), lambda i: (i, 0)) for w, _ in new_outs]
        out_shape = [jax.ShapeDtypeStruct((R, w), dt) for w, dt in new_outs]
    else:
        index, depth = stack
        out_specs = [pl.BlockSpec((None, tm, w), lambda i: (index, i, 0)) for w, _ in new_outs]
        out_shape = [jax.ShapeDtypeStruct((depth, R, w), dt) for w, dt in new_outs]
    out_specs += [pl.BlockSpec(s, lambda i, nd=len(s): (0,) * nd) for s in red_shapes]
    out_shape += [jax.ShapeDtypeStruct(s, F32) for s in red_shapes]
    aliases, scratch = {}, []
    if into is not None:
        out_specs.append(pl.BlockSpec(memory_space=pl.ANY))
        out_shape.append(jax.ShapeDtypeStruct(into.shape, into.dtype))
        if not fresh:
            in_specs.append(pl.BlockSpec(memory_space=pl.ANY))
            args.append(into)
            aliases = {len(args) - 1: len(out_shape) - 1}
        scratch = [pltpu.VMEM((tm, w), into.dtype) for w, _ in into_cols] + [pltpu.SemaphoreType.DMA((n_into,))]
    if stacked_onto is not None:
        for j, prev in enumerate(stacked_onto):
            in_specs.append(pl.BlockSpec(memory_space=pl.ANY))
            args.append(prev)
            aliases[len(args) - 1] = j
    outs = pl.pallas_call(
        body,
        name=name,
        grid=(R // tm,),
        in_specs=in_specs,
        out_specs=out_specs,
        out_shape=out_shape,
        scratch_shapes=scratch,
        input_output_aliases=aliases,
        compiler_params=pltpu.CompilerParams(dimension_semantics=("arbitrary",), vmem_limit_bytes=VMEM_BIG),
    )(*args)
    return list(outs)


def vjp_fn(fn, n_row, n_out, passthrough=0):
    def bwd(*args):
        rows = [a.astype(F32) for a in args[:n_row]]
        cts = args[n_row:n_row + n_out]
        thru = args[n_row + n_out:n_row + n_out + passthrough]
        pars = args[n_row + n_out + passthrough:]
        outs, pull = jax.vjp(lambda *xs: tuple(fn(*xs)), *rows, *pars)
        grads = pull(tuple(c.astype(o.dtype) for c, o in zip(cts, outs)))
        return tuple(grads[:n_row]) + tuple(thru) + tuple(grads[n_row:])

    return bwd


def _ln(x, g, b):
    mu = jnp.mean(x, axis=-1, keepdims=True)
    xc = x - mu
    var = jnp.mean(xc * xc, axis=-1, keepdims=True)
    return xc * lax.rsqrt(var + LN_EPS) * g + b


def f_ln_in(x, g, b):
    return (_ln(x, g, b),)


def f_deepnorm(x, y, g, b):
    return (_ln(ALPHA * x + y, g, b),)


def _twice(fn):
    return lambda *args: fn(*args) * 2


STREAM = [(D_MODEL, F32), (D_MODEL, MXU_DTYPE)]


def _rope(t, cos, sin):
    parts = []
    for h in range(LIN_HEADS):
        th = t[:, h * LIN_KEY_DIM:(h + 1) * LIN_KEY_DIM]
        parts.append(th * cos + swap_halves(th) * sin)
    return jnp.concatenate(parts, axis=1)


def f_ret_pre(rq, rk, cos, sin):
    return _rope(rq, cos, sin), _rope(rk, cos, sin) * LIN_KEY_DIM ** -0.5


def _log_sigmoid(x):
    return jnp.minimum(x, 0.0) - jnp.log(1.0 + jnp.exp(-jnp.abs(x)))


def f_gla_pre(gq, glr, wlr, blr):
    gate_pre = dot_nn(glr, wlr) + blr
    return gq * LIN_KEY_DIM ** -0.5, _log_sigmoid(gate_pre) / GLA_GATE_NORMALIZER


def f_normgate(ret_raw, rg, gla_raw, gg, gnorm):
    rets, glas = [], []
    for h in range(LIN_HEADS):
        sl = slice(h * LIN_VAL_DIM, (h + 1) * LIN_VAL_DIM)
        t = ret_raw[:, sl]
        mu = jnp.mean(t, axis=-1, keepdims=True)
        tc = t - mu
        rets.append(tc * lax.rsqrt(jnp.mean(tc * tc, axis=-1, keepdims=True) + LN_EPS))
        u = gla_raw[:, sl]
        glas.append(u * lax.rsqrt(jnp.mean(u * u, axis=-1, keepdims=True) + LN_EPS) * gnorm)
    ret = jax.nn.silu(rg) * jnp.concatenate(rets, axis=1)
    gla = jax.nn.silu(gg) * jnp.concatenate(glas, axis=1)
    return ret, gla


def f_merge(p0, p1, p2, gl):
    out = jax.nn.sigmoid(gl[:, :D_MODEL]) * p0
    out = out + jax.nn.sigmoid(gl[:, D_MODEL:2 * D_MODEL]) * p1
    return (out + jax.nn.sigmoid(gl[:, 2 * D_MODEL:]) * p2,)


def f_loss(y, tgt):
    diff = y - tgt
    sq = jnp.sum(diff * diff, axis=0, keepdims=True)
    lanes = sq[:, :128]
    for j in range(1, D_MODEL // 128):
        lanes = lanes + sq[:, j * 128:(j + 1) * 128]
    return diff * (1.0 / D_MODEL), lanes * (0.5 / D_MODEL)


def _adam(g, w, m, v):
    m = ADAM_B1 * m + (1.0 - ADAM_B1) * g
    v = ADAM_B2 * v + (1.0 - ADAM_B2) * jnp.square(g)
    m_hat = m / (1.0 - ADAM_B1 ** ADAM_STEP)
    v_hat = v / (1.0 - ADAM_B2 ** ADAM_STEP)
    delta = -ADAM_LR * (m_hat / (jnp.sqrt(v_hat) + ADAM_EPS) + ADAM_WD * w)
    return delta, m, v


def f_sum_adam(*args):
    g = args[0].astype(F32)
    for p in args[1:-3]:
        g = g + p.astype(F32)
    return (g,) + _adam(g, *args[-3:])


def f_adam(g, w, m, v):
    return _adam(g, w, m, v)


GROUP = 4
GROUP_ROWS = GROUP * CHUNK
GROUP_KEYS = (GROUP + ATTN_LEFT_CHUNKS) * CHUNK
DIAG_ROWS = REL_CLIP + CHUNK
GRAD_ROWS = 200


def _blocks_by_delta():
    blocks = {}
    for qi in range(GROUP):
        for J in range(GROUP_KEYS // 128):
            blocks.setdefault(qi - 2 * J, []).append((qi, J))
    return blocks


def bias_table(rel_bias, side=None):
    n_tab = rel_bias.shape[0] * ATTN_HEADS

    def body(rb_ref, o_ref):
        h = pl.program_id(0)
        rm = lax.broadcasted_iota(jnp.int32, (CHUNK, 128), 0)
        cm = lax.broadcasted_iota(jnp.int32, (CHUNK, 128), 1)
        ahead = jnp.where(cm >= CHUNK, 1, 0)
        for delta, where in _blocks_by_delta().items():
            base = 64 * delta + PAD_ROWS
            rel = jnp.minimum(base + rm - cm, REL_CLIP) + REL_CLIP
            in_band = (ahead - delta >= 0) & (ahead - delta <= ATTN_LEFT_CHUNKS)
            lo = max(REL_MIN, min(base - 127, REL_CLIP) + REL_CLIP)
            hi = min(base + CHUNK - 1, REL_CLIP) + REL_CLIP

            def step(r, acc, rel=rel):
                return jnp.where(rel == r, rb_ref[h * N_REL + r], acc)

            blk = jnp.zeros((CHUNK, 128), F32) if hi < lo else lax.fori_loop(lo, hi + 1, step, jnp.zeros((CHUNK, 128), F32))
            blk = jnp.where(in_band, blk, NEG_INF)
            for qi, J in where:
                o_ref[pl.ds(qi * CHUNK, CHUNK), pl.ds(J * 128, 128)] = blk

    return _side_call(
        side, body,
        name="bias_table",
        grid=(n_tab,),
        in_specs=[pl.BlockSpec(memory_space=pltpu.SMEM)],
        out_specs=[pl.BlockSpec((None, GROUP_ROWS, GROUP_KEYS), lambda h: (h, 0, 0))],
        out_shape=[jax.ShapeDtypeStruct((n_tab, GROUP_ROWS, GROUP_KEYS), F32)],
        scratch_shapes=[],
        args=[rel_bias.reshape(n_tab * N_REL)],
    )[0]


def bias_table_grad(dbias):
    blocks = _blocks_by_delta()
    below_clip = range(-(CHUNK - 1), REL_CLIP)
    deltas = [d for d in sorted(blocks) if any(-127 <= pos - 64 * d - PAD_ROWS <= CHUNK - 1 for pos in below_clip)]

    def body(d_ref, o_ref, diag):
        sums = []
        for delta in deltas:
            acc = None
            for qi, J in blocks[delta]:
                blk = d_ref[pl.ds(qi * CHUNK, CHUNK), pl.ds(J * 128, 128)]
                acc = blk if acc is None else acc + blk
            sums.append(acc)
        off = lax.broadcasted_iota(jnp.int32, (CHUNK, 128), 0) - lax.broadcasted_iota(jnp.int32, (CHUNK, 128), 1)
        diag[...] = jnp.zeros_like(diag)

        def step(t, carry):
            pos = t - (CHUNK - 1)
            picked = jnp.zeros((CHUNK, 128), F32)
            for delta, s in zip(deltas, sums):
                picked = picked + jnp.where(off == pos - 64 * delta - PAD_ROWS, s, 0.0)
            diag[pl.ds(t, 1), :] = jnp.sum(picked, axis=0, keepdims=True)
            return carry

        lax.fori_loop(0, DIAG_ROWS, step, 0, unroll=8)
        d = d_ref[...]
        total = jnp.sum(jnp.sum(d, axis=1, keepdims=True), axis=0, keepdims=True)
        o_ref[pl.ds(0, DIAG_ROWS), :] = jnp.broadcast_to(jnp.sum(diag[...], axis=1, keepdims=True), (DIAG_ROWS, 128))
        o_ref[pl.ds(DIAG_ROWS, GRAD_ROWS - DIAG_ROWS), :] = jnp.broadcast_to(total, (GRAD_ROWS - DIAG_ROWS, 128))

    out = pl.pallas_call(
        body,
        name="bias_table_grad",
        grid=(ATTN_HEADS,),
        in_specs=[pl.BlockSpec((None, GROUP_ROWS, GROUP_KEYS), lambda h: (h, 0, 0))],
        out_specs=pl.BlockSpec((None, GRAD_ROWS, 128), lambda h: (h, 0, 0)),
        out_shape=jax.ShapeDtypeStruct((ATTN_HEADS, GRAD_ROWS, 128), F32),
        scratch_shapes=[pltpu.VMEM((DIAG_ROWS, 128), F32)],
        compiler_params=pltpu.CompilerParams(dimension_semantics=("arbitrary",)),
    )(dbias)
    below = out[:, :DIAG_ROWS - 1, 0]
    clipped = out[:, DIAG_ROWS, 0] - jnp.sum(below, axis=1)
    return jnp.concatenate([jnp.zeros((ATTN_HEADS, REL_MIN), F32), below, clipped[:, None]], axis=1)


def _attn_stage(q_ref, k_ref, v_ref, qs, kp, vp, T):
    qs[...] = (q_ref[...] * ATTN_HEAD_DIM ** -0.5).astype(qs.dtype)
    for dst, src in ((kp, k_ref), (vp, v_ref)):
        dst[pl.ds(0, PAD_ROWS), :] = jnp.zeros((PAD_ROWS, ATTN_HEAD_DIM), dst.dtype)
        dst[pl.ds(PAD_ROWS, T), :] = src[...].astype(dst.dtype)


STRIP = 256


def _strip_scores(g, t, first_groups, r0, qc, kp, b_ref):
    keys = pl.ds(r0 + t * STRIP if first_groups else pl.multiple_of(r0 + t * STRIP, STRIP), STRIP)
    s = _mxu(qc, kp[keys, :], _NT) + b_ref[:, t * STRIP:(t + 1) * STRIP]
    if first_groups:
        k_chunk = lax.shift_right_logical(lax.broadcasted_iota(jnp.int32, (GROUP_ROWS, STRIP), 1), 6)
        s = jnp.where(g * GROUP + t * (STRIP // CHUNK) + k_chunk >= ATTN_LEFT_CHUNKS, s, NEG_INF)
    return s, keys


def _for_groups(T, step):
    n_first = min(ATTN_LEFT_CHUNKS // GROUP, T // GROUP_ROWS)
    for g in range(n_first):
        step(g, True)
    n_loop = T // GROUP_ROWS - n_first
    lax.fori_loop(n_first, T // GROUP_ROWS, lambda g, c: (step(g, False), c)[1], 0, unroll=2 if n_loop % 2 == 0 else 1)


def _head_spec(T, base):
    return pl.BlockSpec((T, ATTN_HEAD_DIM), lambda h: (0, base + h))


def _attn_scratch(T):
    return [pltpu.VMEM((T, ATTN_HEAD_DIM), MXU_DTYPE)] + [pltpu.VMEM((T + PAD_ROWS, ATTN_HEAD_DIM), MXU_DTYPE)] * 2


def attn_fwd(proj, bias_tab, first_tab, side=None):
    T = proj.shape[0]

    def body(q_ref, k_ref, v_ref, b_ref, o_ref, lse_ref, qs, kp, vp):
        _attn_stage(q_ref, k_ref, v_ref, qs, kp, vp, T)

        def step(g, first):
            r0 = g * GROUP_ROWS if first else pl.multiple_of(g * GROUP_ROWS, GROUP_ROWS)
            rows = pl.ds(r0, GROUP_ROWS)
            qc = qs[rows, :]
            m = jnp.full((GROUP_ROWS, 1), NEG_INF, F32)
            l = jnp.zeros((GROUP_ROWS, 1), F32)
            acc = jnp.zeros((GROUP_ROWS, ATTN_HEAD_DIM), F32)
            for t in range(GROUP_KEYS // STRIP):
                s, keys = _strip_scores(g, t, first, r0, qc, kp, b_ref)
                m_new = jnp.maximum(m, jnp.max(s, axis=-1, keepdims=True))
                keep = jnp.exp(m - m_new)
                e = jnp.exp(s - m_new)
                l = l * keep + jnp.sum(e, axis=-1, keepdims=True)
                acc = acc * keep + _mxu(e, vp[keys, :], _NN)
                m = m_new
            o_ref[rows, :] = (acc * (1.0 / l)).astype(o_ref.dtype)
            lse_ref[rows, :] = jnp.broadcast_to(m + jnp.log(l), (GROUP_ROWS, ATTN_HEAD_DIM))

        _for_groups(T, step)

    out, lse = _side_call(
        side, body,
        name="attn_fwd",
        grid=(ATTN_HEADS,),
        in_specs=[_head_spec(T, COL["aq"] // 128), _head_spec(T, COL["ak"] // 128), _head_spec(T, COL["av"] // 128),
                  pl.BlockSpec((None, GROUP_ROWS, GROUP_KEYS), lambda h: (first_tab + h, 0, 0))],
        out_specs=[pl.BlockSpec((T, ATTN_HEAD_DIM), lambda h: (0, h))] * 2,
        out_shape=[jax.ShapeDtypeStruct((T, ATTN_HEADS * ATTN_HEAD_DIM), MXU_DTYPE),
                   jax.ShapeDtypeStruct((T, ATTN_HEADS * ATTN_HEAD_DIM), F32)],
        scratch_shapes=_attn_scratch(T),
        args=[proj, proj, proj, bias_tab],
    )
    return out, lse


def attn_bwd(proj, bias_tab, first_tab, out, lse, dout, dproj, side=None):
    T = proj.shape[0]

    def body(q_ref, k_ref, v_ref, b_ref, do_ref, o_ref, lse_ref, dproj_in, dbias_ref, dproj_ref, qs, kp, vp, dkp, dvp, dq, stage):
        del dproj_in
        h = pl.program_id(0)
        _attn_stage(q_ref, k_ref, v_ref, qs, kp, vp, T)
        dkp[...] = jnp.zeros_like(dkp)
        dvp[...] = jnp.zeros_like(dvp)
        dbias_ref[...] = jnp.zeros_like(dbias_ref)

        def step(g, first):
            r0 = g * GROUP_ROWS if first else pl.multiple_of(g * GROUP_ROWS, GROUP_ROWS)
            rows = pl.ds(r0, GROUP_ROWS)
            qc, do = qs[rows, :], do_ref[rows, :]
            lse = jnp.concatenate([lse_ref[rows, :]] * (STRIP // ATTN_HEAD_DIM), axis=1)
            row_term = jnp.sum(do.astype(F32) * o_ref[rows, :].astype(F32), axis=-1, keepdims=True)
            dq_acc = jnp.zeros((GROUP_ROWS, ATTN_HEAD_DIM), F32)
            for t in range(GROUP_KEYS // STRIP):
                s, keys = _strip_scores(g, t, first, r0, qc, kp, b_ref)
                cols = slice(t * STRIP, (t + 1) * STRIP)
                kb, vb = kp[keys, :], vp[keys, :]
                p = jnp.exp(s - lse)
                dvp[keys, :] += _mxu(p, do, _TN)
                ds = p * (_mxu(do, vb, _NT) - row_term)
                dq_acc = dq_acc + _mxu(ds, kb, _NN)
                dkp[keys, :] += _mxu(ds, qc, _TN)
                dbias_ref[:, cols] += ds
            dq[rows, :] = dq_acc * ATTN_HEAD_DIM ** -0.5

        _for_groups(T, step)
        col = pl.multiple_of(h * ATTN_HEAD_DIM, ATTN_HEAD_DIM)
        for grad, first_row, base in ((dq, 0, "aq"), (dkp, PAD_ROWS, "ak"), (dvp, PAD_ROWS, "av")):
            stage[...] = grad[pl.ds(first_row, T), :].astype(stage.dtype)
            pltpu.sync_copy(stage, dproj_ref.at[:, pl.ds(COL[base] + col, ATTN_HEAD_DIM)])

    dbias, dproj = _side_call(
        side, body,
        name="attn_bwd",
        grid=(ATTN_HEADS,),
        in_specs=[_head_spec(T, COL["aq"] // 128), _head_spec(T, COL["ak"] // 128), _head_spec(T, COL["av"] // 128),
                  pl.BlockSpec((None, GROUP_ROWS, GROUP_KEYS), lambda h: (first_tab + h, 0, 0))]
        + [pl.BlockSpec((T, ATTN_HEAD_DIM), lambda h: (0, h))] * 3 + [pl.BlockSpec(memory_space=pl.ANY)],
        out_specs=[pl.BlockSpec((None, GROUP_ROWS, GROUP_KEYS), lambda h: (h, 0, 0)), pl.BlockSpec(memory_space=pl.ANY)],
        out_shape=[jax.ShapeDtypeStruct((ATTN_HEADS, GROUP_ROWS, GROUP_KEYS), F32),
                   jax.ShapeDtypeStruct(dproj.shape, dproj.dtype)],
        scratch_shapes=_attn_scratch(T) + [pltpu.VMEM((T + PAD_ROWS, ATTN_HEAD_DIM), F32)] * 2
        + [pltpu.VMEM((T, ATTN_HEAD_DIM), F32), pltpu.VMEM((T, ATTN_HEAD_DIM), dproj.dtype)],
        args=[proj, proj, proj, bias_tab, dout, out, lse, dproj],
        aliases={7: 1},
    )
    return dproj, dbias


def _lin_segment(q, k, v, la, st):
    lower = _tri(CHUNK, True)
    steps = (lax.broadcasted_iota(jnp.int32, (CHUNK, 1), 0) + 1).astype(F32)
    outs = []
    for i in range(SEG_CHUNKS):
        sl = slice(i * CHUNK, (i + 1) * CHUNK)
        qi, ki, vi = q[sl], k[sl], v[sl]
        if la.shape[0] == 1:
            cum, last = la * steps, la * float(CHUNK)
        else:
            cum, last = chunk_cumsum(la[sl]), jnp.sum(la[sl], axis=0, keepdims=True)
        e_pos, e_neg = jnp.exp(cum), jnp.exp(-cum)
        q_fwd = qi * e_pos
        s_fwd = dot_nt(q_fwd, ki * e_neg)
        s_bwd = dot_nt(qi * e_neg, ki * e_pos)
        o = dot_nn(jnp.where(lower, s_fwd, s_bwd), vi) + dot_nt(q_fwd, st)
        st = jnp.exp(last) * st + dot_tn(vi, ki * jnp.exp(last - cum))
        outs.append(o)
    return jnp.concatenate(outs, axis=0), st


class Win:
    def __init__(self, arr, width, base=0):
        self.arr, self.width, self.base = arr, width, base


HEADS_PER_STEP = 2


def _lin_spec(w, seg_of):
    assert w.base % HEADS_PER_STEP == 0
    first, width = w.base // HEADS_PER_STEP, HEADS_PER_STEP * w.width
    if w.arr.shape[0] == 1:
        return pl.BlockSpec((1, width), lambda p, s: (0, first + p))
    return pl.BlockSpec((SEG, width), lambda p, s: (seg_of(s), first + p))


def _head_cols(ref, j, width):
    return ref[:, j * width:(j + 1) * width]


def _decay_rows(la_ref, j):
    return _head_cols(la_ref, j, LIN_KEY_DIM)


_STATE_BLOCK = (HEADS_PER_STEP, None, LIN_VAL_DIM, LIN_KEY_DIM)


def lin_fwd(name, q, k, v, la, side=None):
    T = q.arr.shape[0]
    nseg = T // SEG

    def body(q_ref, k_ref, v_ref, la_ref, o_ref, st_ref, st):
        @pl.when(pl.program_id(1) == 0)
        def _():
            st[...] = jnp.zeros_like(st)

        for j in range(HEADS_PER_STEP):
            st0 = st[j]
            st_ref[j] = st0
            o, st1 = _lin_segment(_head_cols(q_ref, j, LIN_KEY_DIM), _head_cols(k_ref, j, LIN_KEY_DIM),
                                  _head_cols(v_ref, j, LIN_VAL_DIM), _decay_rows(la_ref, j), st0)
            o_ref[:, j * LIN_VAL_DIM:(j + 1) * LIN_VAL_DIM] = o
            st[j] = st1

    fwd = lambda s: s
    out, states = _side_call(
        side, body,
        name=name,
        grid=(LIN_HEADS // HEADS_PER_STEP, nseg),
        in_specs=[_lin_spec(w, fwd) for w in (q, k, v, la)],
        out_specs=[pl.BlockSpec((SEG, HEADS_PER_STEP * LIN_VAL_DIM), lambda p, s: (s, p)),
                   pl.BlockSpec(_STATE_BLOCK, lambda p, s: (p, s, 0, 0))],
        out_shape=[jax.ShapeDtypeStruct((T, LIN_HEADS * LIN_VAL_DIM), F32),
                   jax.ShapeDtypeStruct((LIN_HEADS, nseg, LIN_VAL_DIM, LIN_KEY_DIM), F32)],
        scratch_shapes=[pltpu.VMEM((HEADS_PER_STEP, LIN_VAL_DIM, LIN_KEY_DIM), F32)],
        args=[q.arr, k.arr, v.arr, la.arr],
    )
    return out, states


def lin_bwd(name, q, k, v, la, states, dout, with_decay_grad, side=None, into=None, windows=None):
    T = q.arr.shape[0]
    nseg = T // SEG
    n_out = 4 if with_decay_grad else 3

    widths = [LIN_KEY_DIM, LIN_KEY_DIM, LIN_VAL_DIM] + ([LIN_KEY_DIM] if with_decay_grad else [])
    windows = dict(windows or {})
    fresh = [i for i in range(n_out) if i not in windows]
    placed = sorted(windows)
    n_pairs = LIN_HEADS // HEADS_PER_STEP
    rev = lambda s: nseg - 1 - s

    def body(q_ref, k_ref, v_ref, la_ref, st_ref, do_ref, *rest):
        rest = rest[1:] if placed else rest
        outs = dict(zip(fresh, rest[:len(fresh)]))
        rest = rest[len(fresh):]
        into_ref, rest = (rest[0], rest[1:]) if placed else (None, rest)
        dst, stages = rest[0], dict(zip(placed, rest[1:1 + len(placed)]))
        p, s = pl.program_id(0), pl.program_id(1)

        @pl.when(s == 0)
        def _():
            dst[...] = jnp.zeros_like(dst)

        def window_copy(n, i):
            width = HEADS_PER_STEP * widths[i]
            rows = pl.ds(pl.multiple_of(rev(s) * SEG, SEG), SEG)
            cols = pl.ds(pl.multiple_of(windows[i] + p * width, width), width)
            return pltpu.make_async_copy(stages[i], into_ref.at[rows, cols], rest[1 + len(placed)].at[n])

        for n, i in enumerate(placed):
            @pl.when((p > 0) | (s > 0))
            def _(n=n, i=i):
                window_copy(n, i).wait()

        for j in range(HEADS_PER_STEP):
            _, pull = jax.vjp(_lin_segment, _head_cols(q_ref, j, LIN_KEY_DIM).astype(F32),
                              _head_cols(k_ref, j, LIN_KEY_DIM).astype(F32), _head_cols(v_ref, j, LIN_VAL_DIM).astype(F32),
                              _decay_rows(la_ref, j), st_ref[j])
            grads = pull((_head_cols(do_ref, j, LIN_VAL_DIM), dst[j]))
            for i in range(n_out):
                dest = stages[i] if i in windows else outs[i]
                dest[:, j * widths[i]:(j + 1) * widths[i]] = grads[i].astype(dest.dtype)
            dst[j] = grads[4]

        for n, i in enumerate(placed):
            window_copy(n, i).start()

            @pl.when((p == n_pairs - 1) & (s == nseg - 1))
            def _(n=n, i=i):
                window_copy(n, i).wait()

    key_spec = pl.BlockSpec((SEG, HEADS_PER_STEP * LIN_KEY_DIM), lambda p, s: (rev(s), p))
    val_spec = pl.BlockSpec((SEG, HEADS_PER_STEP * LIN_VAL_DIM), lambda p, s: (rev(s), p))
    all_specs = [key_spec, key_spec, val_spec] + ([key_spec] if with_decay_grad else [])
    any_spec = pl.BlockSpec(memory_space=pl.ANY)
    outs = _side_call(
        side, body,
        name=name,
        grid=(n_pairs, nseg),
        in_specs=[_lin_spec(w, rev) for w in (q, k, v, la)] + [
            pl.BlockSpec(_STATE_BLOCK, lambda p, s: (p, rev(s), 0, 0)), val_spec] + ([any_spec] if placed else []),
        out_specs=[all_specs[i] for i in fresh] + ([any_spec] if placed else []),
        out_shape=[jax.ShapeDtypeStruct((T, LIN_HEADS * widths[i]), F32) for i in fresh]
        + ([jax.ShapeDtypeStruct(into.shape, into.dtype)] if placed else []),
        scratch_shapes=[pltpu.VMEM((HEADS_PER_STEP, LIN_VAL_DIM, LIN_KEY_DIM), F32)]
        + [pltpu.VMEM((SEG, HEADS_PER_STEP * widths[i]), into.dtype) for i in placed]
        + ([pltpu.SemaphoreType.DMA((len(placed),))] if placed else []),
        args=[q.arr, k.arr, v.arr, la.arr, states, dout] + ([into] if placed else []),
        aliases={6: len(fresh)} if placed else None,
    )
    return outs


def _coords():
    return lax.axis_index("x"), lax.axis_index("y"), lax.axis_index("c")


def _peer(k):
    x, y, c = _coords()
    px = 1 - x if k & 4 else x
    py = 1 - y if k & 2 else y
    pc = 1 - c if k & 1 else c
    return (px, py, pc), 4 * px + 2 * py + pc


def _remote(src, dst, send_sem, recv_sem, device):
    return pltpu.make_async_remote_copy(src_ref=src, dst_ref=dst, send_sem=send_sem, recv_sem=recv_sem,
                                        device_id=device, device_id_type=pl.DeviceIdType.MESH)


class Side:
    def __init__(self, kind, arrays):
        self.kind, self.arrays, self.n = kind, list(arrays), len(arrays)
        self.outs_per = 1
        self.slots = {"gather": N_DEV - 1, "pair": 4, "chips": 3}[kind]

    def out_shape(self):
        if self.kind == "gather":
            return [jax.ShapeDtypeStruct((N_DEV,) + a.shape, a.dtype) for a in self.arrays]
        return [jax.ShapeDtypeStruct((4,) + a.shape[1:], a.dtype) for a in self.arrays]

    def sems(self):
        return [pltpu.SemaphoreType.DMA((self.n, self.slots)), pltpu.SemaphoreType.DMA((self.n, self.slots)),
                pltpu.SemaphoreType.DMA((self.n,))]

    def start(self, ins, outs, sems):
        send, recv, own = sems
        x, y, c = _coords()
        me = 4 * x + 2 * y + c
        for a in range(self.n):
            if self.kind == "gather":
                pltpu.make_async_copy(ins[a], outs[a].at[me], own.at[a]).start()
                for k in (1, 2, 4, 6):
                    peer, _ = _peer(k)
                    _remote(ins[a], outs[a].at[me], send.at[a, k - 1], recv.at[a, k - 1], peer).start()
            elif self.kind == "pair":
                for j in range(4):
                    _remote(ins[a].at[2 * j + 1 - c], outs[a].at[j], send.at[a, j], recv.at[a, j], (x, y, 1 - c)).start()
            else:
                chip = 2 * x + y
                pltpu.make_async_copy(ins[a].at[chip], outs[a].at[chip], own.at[a]).start()
                for d in (1, 2, 3):
                    tx, ty = (1 - x if d & 2 else x), (1 - y if d & 1 else y)
                    _remote(ins[a].at[2 * tx + ty], outs[a].at[chip], send.at[a, d - 1], recv.at[a, d - 1], (tx, ty, c)).start()

    def finish(self, ins, outs, sems):
        send, recv, own = sems
        x, y, c = _coords()
        me = 4 * x + 2 * y + c
        sibling = (x, y, 1 - c)
        for a in range(self.n):
            if self.kind == "gather":
                blk = outs[a].at[me]
                for k in (2, 4, 6):
                    _, pidx = _peer(k)
                    _remote(blk, blk, send.at[a, k - 1], recv.at[a, k - 1], sibling).wait_recv()
                    _remote(outs[a].at[pidx], outs[a].at[pidx], send.at[a, k], recv.at[a, k], sibling).start()
                pltpu.make_async_copy(ins[a], blk, own.at[a]).wait()
                for k in range(1, N_DEV):
                    _remote(blk, blk, send.at[a, k - 1], recv.at[a, k - 1], sibling).wait_send()
                for k in (1, 3, 5, 7):
                    _remote(blk, blk, send.at[a, k - 1], recv.at[a, k - 1], sibling).wait_recv()
            elif self.kind == "pair":
                for j in range(4):
                    _remote(ins[a].at[j], outs[a].at[j], send.at[a, j], recv.at[a, j], sibling).wait()
            else:
                slab = outs[a].at[0]
                pltpu.make_async_copy(slab, slab, own.at[a]).wait()
                for d in (1, 2, 3):
                    _remote(slab, slab, send.at[a, d - 1], recv.at[a, d - 1], sibling).wait()


def _side_call(side, body, *, name, grid, in_specs, out_specs, out_shape, scratch_shapes, args, semantics=None,
               aliases=None):
    in_specs, out_specs, out_shape = list(in_specs), list(out_specs), list(out_shape)
    scratch_shapes, args = list(scratch_shapes), list(args)
    n_in, n_out = len(in_specs), len(out_specs)
    kernel_body = body
    if side is not None:
        n_so = side.n * side.outs_per

        def kernel_body(*refs):
            ins, side_ins = refs[:n_in], refs[n_in:n_in + side.n]
            pos = n_in + side.n
            outs, side_outs = refs[pos:pos + n_out], refs[pos + n_out:pos + n_out + n_so]
            rest = refs[pos + n_out + n_so:]
            scratch, sems = rest[:-3], rest[-3:]
            ids = [pl.program_id(ax) for ax in range(len(grid))]
            first = functools.reduce(jnp.logical_and, [i == 0 for i in ids])
            last = functools.reduce(jnp.logical_and, [i == g - 1 for i, g in zip(ids, grid)])

            @pl.when(first)
            def _():
                side.start(side_ins, side_outs, sems)

            body(*ins, *outs, *scratch)

            @pl.when(last)
            def _():
                side.finish(side_ins, side_outs, sems)

        any_spec = pl.BlockSpec(memory_space=pl.ANY)
        in_specs += [any_spec] * side.n
        args += side.arrays
        out_specs += [any_spec] * n_so
        out_shape += side.out_shape()
        scratch_shapes += side.sems()
        semantics = ("arbitrary",) * len(grid)
    outs = pl.pallas_call(
        kernel_body,
        name=name,
        grid=grid,
        in_specs=in_specs,
        out_specs=out_specs,
        out_shape=out_shape,
        scratch_shapes=scratch_shapes,
        input_output_aliases=aliases or {},
        compiler_params=pltpu.CompilerParams(dimension_semantics=semantics or ("arbitrary",) * len(grid),
                                             vmem_limit_bytes=VMEM_BIG),
    )(*args)
    if side is not None:
        side.results = list(outs[n_out:])
    return list(outs[:n_out])


def carrier(name, side):
    _side_call(side, lambda: None, name=name, grid=(1,), in_specs=[], out_specs=[], out_shape=[], scratch_shapes=[], args=[])
    return side.results


def all_reduce_small(vec):
    def body(v_ref, o_ref, land, send_sems, recv_sems):
        x, y, c = _coords()
        me = 4 * x + 2 * y + c
        land[me] = v_ref[...]
        copies = []
        for k in range(1, N_DEV):
            peer, _ = _peer(k)
            cp = pltpu.make_async_remote_copy(
                src_ref=v_ref, dst_ref=land.at[me], send_sem=send_sems.at[k - 1], recv_sem=recv_sems.at[k - 1],
                device_id=peer, device_id_type=pl.DeviceIdType.MESH)
            cp.start()
            copies.append(cp)
        for cp in copies:
            cp.wait()
        total = land[0]
        for s in range(1, N_DEV):
            total = total + land[s]
        o_ref[...] = total

    return pl.pallas_call(
        body,
        name="all_reduce_small",
        in_specs=[pl.BlockSpec(memory_space=pltpu.VMEM)],
        out_specs=pl.BlockSpec(memory_space=pltpu.VMEM),
        out_shape=jax.ShapeDtypeStruct(vec.shape, F32),
        scratch_shapes=[pltpu.VMEM((N_DEV,) + vec.shape, F32), pltpu.SemaphoreType.DMA((N_DEV - 1,)),
                        pltpu.SemaphoreType.DMA((N_DEV - 1,))],
    )(vec)


def _pack(parts):
    flat = jnp.concatenate([p.reshape(-1).astype(F32) for p in parts])
    n = flat.shape[0]
    rows = -(-n // 1024) * 8
    return jnp.pad(flat, (0, rows * 128 - n)).reshape(rows, 128)


def _unpack(packed, shapes):
    flat, out, pos = packed.reshape(-1), [], 0
    for s in shapes:
        n = math.prod(s)
        out.append(flat[pos:pos + n].reshape(s))
        pos += n
    return out


def _row2(v):
    return v.reshape(1, -1)


def _rider(hooks, name, ctx=None):
    if name not in hooks:
        return None, lambda: None
    side, done = hooks[name](ctx)
    return side, lambda: done(side.results)


def _layer_forward(h, hb, W, consts, hooks):
    cos, sin, ret_decay = consts
    side, landed = _rider(hooks, "proj")
    proj = mm("proj", hb, W["in"], "nn", tm=4096, tn=512, tk=1024, side=side, out_dtype=PROJ_DTYPE)
    landed()
    bias_tab, first_tab = W["bias_tab"]
    side, landed = _rider(hooks, "attn_fwd")
    attn, attn_lse = attn_fwd(proj, bias_tab, first_tab, side=side)
    landed()
    rq2, rk2 = row_call("ret_pre", f_ret_pre, [Row(proj, 512, COL["rq"] // 512), Row(proj, 512, COL["rk"] // 512),
                                              Row(cos), Row(sin)], [], [(512, F32), (512, F32)], tm=512)
    side, landed = _rider(hooks, "ret_fwd")
    ret_raw, ret_st = lin_fwd("ret_fwd", Win(rq2, 128), Win(rk2, 128), Win(proj, 256, COL["rv"] // 256), Win(ret_decay, 128),
                              side=side)
    landed()
    gq2, gla_la = row_call("gla_pre", f_gla_pre, [Row(proj, 512, COL["gq"] // 512), Row(proj, GLR_BLOCK, GLR_OFF // GLR_BLOCK)],
                           [W["wlr"], W["blr"]], [(512, F32), (512, F32)], tm=512)
    side, landed = _rider(hooks, "gla_fwd")
    gla_raw, gla_st = lin_fwd("gla_fwd", Win(gq2, 128), Win(proj, 128, COL["gk"] // 128), Win(proj, 256, COL["gv"] // 256),
                              Win(gla_la, 128), side=side)
    landed()
    ret_b, gla_b = row_call("normgate", f_normgate, [Row(ret_raw), Row(proj, 1024, COL["rg"] // 1024), Row(gla_raw),
                                                    Row(proj, 1024, COL["gg"] // 1024)], [W["gnorm"]],
                            [(1024, MXU_DTYPE), (1024, MXU_DTYPE)], tm=512)
    branches = (attn, ret_b, gla_b)
    p = [mm("branch_proj", branches[n], W["branch"][n], "nn", tm=1024, tn=1024, tk=1024) for n in range(3)]
    merged, = row_call("merge", f_merge, [Row(p[0]), Row(p[1]), Row(p[2]), Row(proj, GATES_W, GATES_OFF // GATES_W)], [],
                       [(1024, MXU_DTYPE)], tm=512)
    post_norm = lambda y, x, g, b: f_deepnorm(x, y, g, b) * 2 + (y,)
    x1, x1b, mix = mm("out_proj_ln1", merged, W["out"], "nn", tm=512, tn=D_MODEL, tk=1024, extras=(h,),
                      row_params=(W["ln1_g"], W["ln1_b"]), epi=post_norm, n_out=3, out_dtypes=(F32, MXU_DTYPE, F32))
    side, landed = _rider(hooks, "mlp_up")
    u, act = mm("mlp_up", x1b, W["up"], "nn", tm=4096, tn=512, tk=1024, n_out=2, out_dtypes=(F32, MXU_DTYPE),
                epi=lambda r: (r, jnp.square(jnp.maximum(r, 0.0))), side=side)
    landed()
    x2, x2b, y = mm("mlp_down_ln2", act, W["down"], "nn", tm=1024, tn=D_MODEL, tk=1024, extras=(x1,),
                    row_params=(W["ln2_g"], W["ln2_b"]), epi=post_norm, n_out=3, out_dtypes=(F32, MXU_DTYPE, F32))
    saved = dict(h=h, hb=hb, proj=proj, attn_lse=attn_lse, rq2=rq2, rk2=rk2, ret_st=ret_st, gq2=gq2, gla_la=gla_la,
                 gla_st=gla_st, ret_raw=ret_raw, gla_raw=gla_raw, branches=branches, p=p, merged=merged, mix=mix, x1=x1, x1b=x1b,
                 u=u, act=act, y=y)
    return x2, x2b, saved


def _layer_backward(dx2, S, W, consts, hooks):
    cos, sin, ret_decay = consts
    h, proj, T = S["h"], S["proj"], S["h"].shape[0]
    add = lambda r, e: r + e
    dx1a, dy, dg2, db2 = row_call("ln2_bwd", vjp_fn(f_deepnorm, 2, 1), [Row(S["x1"]), Row(S["y"]), Row(dx2)],
                                  [W["ln2_g"], W["ln2_b"]], [(1024, F32), (1024, MXU_DTYPE)], [(1, 1024), (1, 1024)], tm=512)
    side, landed = _rider(hooks, "mlp_down_bwd")
    du = mm("mlp_down_bwd", dy, W["down"], "nt", tm=1024, tn=512, tk=1024, extras=(S["u"],),
            epi=lambda r, u: r * (2.0 * jnp.maximum(u, 0.0)), out_dtype=MXU_DTYPE, side=side)
    landed()
    g_down = mm("mlp_down_wgrad", S["act"], dy, "tn", tm=1024, tn=1024, tk=2048, out_dtype=WIRE_DTYPE)
    dx1 = mm("mlp_up_bwd", du, W["up"], "nt", tm=1024, tn=1024, tk=2048, extras=(dx1a,), epi=add)
    g_up = mm("mlp_up_wgrad", S["x1b"], du, "tn", tm=1024, tn=D_FF // N_DEV, tk=2048, out_dtype=WIRE_DTYPE,
              column_slabs=True)
    dha, dmix, dg1, db1 = row_call("ln1_bwd", vjp_fn(f_deepnorm, 2, 1), [Row(h), Row(S["mix"]), Row(dx1)],
                                   [W["ln1_g"], W["ln1_b"]], [(1024, F32), (1024, MXU_DTYPE)], [(1, 1024), (1, 1024)], tm=512)
    dmerged = mm("out_proj_bwd", dmix, W["out"], "nt", tm=1024, tn=1024, tk=1024)
    g_out = mm("out_proj_wgrad", S["merged"], dmix, "tn", tm=1024, tn=1024, tk=2048, out_dtype=WIRE_DTYPE)
    dproj = jax.ShapeDtypeStruct((T, PROJ_PAD), MXU_DTYPE)
    p = S["p"]
    dp0, dp1, dp2, dproj = row_call(
        "merge_bwd", vjp_fn(f_merge, 4, 1), [Row(p[0]), Row(p[1]), Row(p[2]), Row(proj, GATES_W, GATES_OFF // GATES_W), Row(dmerged)],
        [], [(1024, MXU_DTYPE)] * 3, tm=512, into=dproj, into_cols=[(GATES_W, GATES_OFF)])
    dps = (dp0, dp1, dp2)
    dbr = [mm("branch_proj_bwd", dps[n], W["branch"][n], "nt", tm=1024, tn=1024, tk=1024,
              out_dtype=MXU_DTYPE if n == 0 else F32) for n in range(3)]
    g_branch = [mm("branch_proj_wgrad", S["branches"][n], dps[n], "tn", tm=1024, tn=1024, tk=2048, out_dtype=WIRE_DTYPE)
                for n in range(3)]
    dret_raw, dgla_raw, dgnorm, dproj = row_call(
        "normgate_bwd", vjp_fn(lambda a, b, c, d, g: _reorder_normgate(a, b, c, d, g), 4, 2),
        [Row(S["ret_raw"]), Row(S["gla_raw"]), Row(proj, 1024, COL["rg"] // 1024), Row(proj, 1024, COL["gg"] // 1024),
         Row(dbr[1]), Row(dbr[2])], [W["gnorm"]], [(1024, F32), (1024, F32)], [(1, LIN_VAL_DIM)], tm=512,
        into=dproj, into_cols=[(1024, COL["rg"]), (1024, COL["gg"])])
    side, landed = _rider(hooks, "gla_bwd")
    dgq2, dgla_la, dproj = lin_bwd("gla_bwd", Win(S["gq2"], 128), Win(proj, 128, COL["gk"] // 128),
                                   Win(proj, 256, COL["gv"] // 256), Win(S["gla_la"], 128), S["gla_st"], dgla_raw, True,
                                   side=side, into=dproj, windows={1: COL["gk"], 2: COL["gv"]})
    landed()
    dwlr, dblr, dproj = row_call(
        "gla_pre_bwd", _gla_pre_bwd, [Row(proj, 512, COL["gq"] // 512), Row(proj, GLR_BLOCK, GLR_OFF // GLR_BLOCK),
                                      Row(dgq2), Row(dgla_la)], [W["wlr"], W["blr"]], [],
        [(GLR_BLOCK, 512), (1, 512)], tm=512, into=dproj,
        into_cols=[(512, COL["gq"]), (PROJ_PAD - GLR_OFF, GLR_OFF)])
    side, landed = _rider(hooks, "ret_bwd", dict(branch=g_branch, out=g_out, up=g_up, down=g_down))
    drq2, drk2, dproj = lin_bwd("ret_bwd", Win(S["rq2"], 128), Win(S["rk2"], 128), Win(proj, 256, COL["rv"] // 256),
                                Win(ret_decay, 128), S["ret_st"], dret_raw, False, side=side, into=dproj,
                                windows={2: COL["rv"]})
    landed()
    dproj, = row_call(
        "ret_pre_bwd", _ret_pre_bwd, [Row(proj, 512, COL["rq"] // 512), Row(proj, 512, COL["rk"] // 512), Row(cos), Row(sin),
                                      Row(drq2), Row(drk2)], [], [], tm=512, into=dproj,
        into_cols=[(512, COL["rq"]), (512, COL["rk"])])
    side, landed = _rider(hooks, "attn_bwd")
    dproj, dbias = attn_bwd(proj, *W["bias_tab"], S["branches"][0], S["attn_lse"], dbr[0], dproj, side=side)
    landed()
    d_rel_bias = bias_table_grad(dbias)
    side, landed = _rider(hooks, "proj_wgrad", dict(branch=g_branch, out=g_out, up=g_up, down=g_down))
    g_in = mm("proj_wgrad", S["hb"], dproj, "tn", tm=1024, tn=1280, tk=2048, out_dtype=WIRE_DTYPE, side=side)
    landed()
    side, landed = _rider(hooks, "proj_bwd", {"in": g_in})
    dh = mm("proj_bwd", dproj, W["in"], "nt", tm=1024, tn=1024, tk=2560, extras=(dha,), epi=add, side=side)
    landed()
    big = dict(g_in=g_in, branch=g_branch, out=g_out, up=g_up, down=g_down)
    small = dict(rel_bias=d_rel_bias, wlr=dwlr[:GLA_RANK], blr=dblr, gnorm=dgnorm, ln1_g=dg1, ln1_b=db1, ln2_g=dg2, ln2_b=db2)
    return dh, big, small


def _reorder_normgate(ret_raw, gla_raw, rg, gg, gnorm):
    return f_normgate(ret_raw, rg, gla_raw, gg, gnorm)


def _gla_pre_bwd(gq, glr, dq2, dla, wlr, blr):
    _, pull = jax.vjp(f_gla_pre, gq, glr, wlr, blr)
    dgq, dglr, dwlr, dblr = pull((dq2, dla))
    dglr = jnp.concatenate([dglr, jnp.zeros((dglr.shape[0], PROJ_PAD - GLR_OFF - GLR_BLOCK), F32)], axis=1)
    return dgq, dglr, dwlr, dblr


def _ret_pre_bwd(rq, rk, cos, sin, drq2, drk2):
    _, pull = jax.vjp(lambda a, b: f_ret_pre(a, b, cos, sin), rq, rk)
    return pull((drq2, drk2))


SHARDED = ("in", "branch", "out", "up", "down")

_FULL = {
    "branch": lambda g: [g[:, n].reshape(D_MODEL, D_MODEL) for n in range(3)],
    "out": lambda g: g.reshape(D_MODEL, D_MODEL),
    "up": lambda g: jnp.transpose(g, (1, 0, 2)).reshape(D_MODEL, D_FF),
    "down": lambda g: g.reshape(D_FF, D_MODEL),
}


def _owner_pieces(s):
    shard = PROJ_WIDTH // N_DEV
    lo, hi = s * shard, (s + 1) * shard
    pieces = []
    for a, b, padded in ((0, GLR_SRC, 0), (GLR_SRC, GATES_SRC, GLR_OFF), (GATES_SRC, PROJ_WIDTH, GATES_OFF)):
        x0, x1 = max(lo, a), min(hi, b)
        if x0 < x1:
            pieces.append((x0 - lo, x1 - x0, padded + x0 - a))
    return pieces


def _full_in(g):
    placed = sorted((p, s, c, w) for s in range(N_DEV) for c, w, p in _owner_pieces(s))
    cols = [g[s][:, c:c + w] for _, s, c, w in placed]
    return jnp.concatenate(cols + [jnp.zeros((D_MODEL, PROJ_PAD - PROJ_WIDTH), g.dtype)], axis=1)


def _full_wlr(g):
    wlr = jnp.transpose(g, (1, 0, 2)).reshape(GLA_RANK, LIN_HEADS * LIN_KEY_DIM)
    return jnp.pad(wlr, ((0, GLR_BLOCK - GLA_RANK), (0, 0)))


def _slab_in(g):
    return jnp.stack([jnp.concatenate([g[:, p:p + w] for _, w, p in _owner_pieces(s)], axis=1) for s in range(N_DEV)])


_SLABS = {
    "in": _slab_in,
    "branch": lambda gs: jnp.stack([b.reshape(N_DEV, D_MODEL // N_DEV, D_MODEL) for b in gs], axis=1),
    "out": lambda g: g.reshape(N_DEV, D_MODEL // N_DEV, D_MODEL),
    "up": lambda g: g,
    "down": lambda g: g.reshape(N_DEV, D_FF // N_DEV, D_MODEL),
}


def _pair_sums(name, slabs, from_sibling=None):
    sums = []
    for mine, sibs in zip(slabs, from_sibling or carrier(name, Side("pair", slabs))):
        width = mine.shape[-1]
        rows = math.prod(mine.shape[1:-1])
        tm = 256 if rows % 256 == 0 else 128
        nb = rows // tm
        my_rows = Row(mine.reshape(-1, width), rowmap=lambda i, nb=nb: (2 * (i // nb) + lax.axis_index("c")) * nb + i % nb)
        out, = row_call("pair_sum", lambda q, p: (p.astype(F32) + q.astype(F32),),
                        [Row(sibs.reshape(-1, width)), my_rows], [], [(width, WIRE_DTYPE)], tm=tm)
        sums.append(out.reshape(sibs.shape))
    return sums


def _sum_adam(name, parts, w, m, v, layer, tm, earlier):
    C = w.shape[-1]
    rows = [Row(parts, lead=(s,)) for s in range(parts.shape[0])] + [Row(a, lead=(layer,)) for a in (w, m, v)]
    return row_call(name, f_sum_adam, rows, [], [(C, F32)] * 4, tm=tm, stack=(layer, DEPTH), stacked_onto=earlier)


def kernel(x, ln_in_g, ln_in_b, w_in, rel_bias, gla_w_lr, gla_b_lr, gla_norm_g, w_branch, w_out, ln1_g, ln1_b, w_up, w_down, ln2_g, ln2_b, loss_target, m_ln_in_g, m_ln_in_b, m_w_in, m_rel_bias, m_gla_w_lr, m_gla_b_lr, m_gla_norm_g, m_w_branch, m_w_out, m_ln1_g, m_ln1_b, m_w_up, m_w_down, m_ln2_g, m_ln2_b, v_ln_in_g, v_ln_in_b, v_w_in, v_rel_bias, v_gla_w_lr, v_gla_b_lr, v_gla_norm_g, v_w_branch, v_w_out, v_ln1_g, v_ln1_b, v_w_up, v_w_down, v_ln2_g, v_ln2_b):
    T = x.shape[1]
    x2d = x.reshape(T, D_MODEL)
    tgt = loss_target.reshape(T, D_MODEL)
    me = 4 * lax.axis_index("x") + 2 * lax.axis_index("y") + lax.axis_index("c")

    inv = ROPE_BASE ** (-jnp.arange(0, LIN_KEY_DIM, 2, dtype=F32) / LIN_KEY_DIM)
    ang = jnp.arange(T, dtype=F32)[:, None] * inv[None, :]
    cos = jnp.concatenate([jnp.cos(ang), jnp.cos(ang)], axis=1)
    sin = jnp.concatenate([-jnp.sin(ang), jnp.sin(ang)], axis=1)
    log_gamma = jnp.log1p(-jnp.exp2(-5.0 - jnp.arange(LIN_HEADS, dtype=F32)))
    ret_decay = jnp.repeat(log_gamma, LIN_KEY_DIM).reshape(1, LIN_HEADS * LIN_KEY_DIM)
    consts = (cos, sin, ret_decay)

    layers = [dict(blr=_row2(gla_b_lr[l]), gnorm=_row2(gla_norm_g[l]), ln1_g=_row2(ln1_g[l]),
                   ln1_b=_row2(ln1_b[l]), ln2_g=_row2(ln2_g[l]), ln2_b=_row2(ln2_b[l])) for l in range(DEPTH)]
    shards = dict(branch=w_branch, out=w_out, up=w_up, down=w_down)
    shards["in"] = w_in
    wire = [{n: shards[n][l].astype(WIRE_DTYPE) for n in SHARDED} for l in range(DEPTH)]

    def gather_of(l, names):
        def landed(res):
            for n, g in zip(names, res):
                layers[l][n] = _full_in(g) if n == "in" else _FULL[n](g)
        return lambda ctx: (Side("gather", [wire[l][n] for n in names]), landed)

    first = Side("gather", [wire[0]["in"]] + [gla_w_lr[l] for l in range(DEPTH)])
    bias_tabs = bias_table(rel_bias, side=first)
    layers[0]["in"] = _full_in(first.results[0])
    for l in range(DEPTH):
        layers[l]["wlr"] = _full_wlr(first.results[1 + l])
        layers[l]["bias_tab"] = (bias_tabs, l * ATTN_HEADS)

    fwd_hooks = [{"proj": gather_of(1, ("in",)), "attn_fwd": gather_of(0, ("up", "down")),
                  "ret_fwd": gather_of(0, ("branch", "out"))},
                 {"proj": gather_of(1, ("up", "down")), "attn_fwd": gather_of(1, ("branch", "out"))}]
    h, hb = row_call("ln_in", _twice(f_ln_in), [Row(x2d)], [_row2(ln_in_g), _row2(ln_in_b)], STREAM, tm=512)
    saved = []
    for l in range(DEPTH):
        h, hb, s = _layer_forward(h, hb, layers[l], consts, fwd_hooks[l])
        saved.append(s)
    dh, loss_lanes = row_call("loss", f_loss, [Row(h), Row(tgt)], [], [(D_MODEL, F32)], [(1, 128)], tm=512)

    received = [dict() for _ in range(DEPTH)]

    def scatter_of(l, names, sums=None):
        def make(ctx):
            parts = [sums[n] for n in names] if sums is not None else _pair_sums(
                "pair_grads", [_SLABS[n](ctx[n]) for n in names])

            def landed(res):
                received[l].update(zip(names, res))
            return Side("chips", parts), landed
        return make

    small = [None] * DEPTH
    dh, big, small[1] = _layer_backward(dh, saved[1], layers[1], consts, {})
    slabs1 = [_SLABS[n](big["g_in"] if n == "in" else big[n]) for n in SHARDED]
    sums1 = {}

    def pair_exchange_1(ctx):
        def landed(res):
            sums1.update(zip(SHARDED, _pair_sums("pair_grads", slabs1, from_sibling=res)))
        return Side("pair", slabs1), landed

    rest = SHARDED[1:]
    sums0 = {}

    def pair_exchange_0(ctx):
        slabs0 = [_SLABS[n](ctx[n]) for n in rest]

        def landed(res):
            sums0.update(zip(rest, _pair_sums("pair_grads", slabs0, from_sibling=res)))
        return Side("pair", slabs0), landed

    dh, big, small[0] = _layer_backward(dh, saved[0], layers[0], consts, {
        "mlp_down_bwd": pair_exchange_1, "gla_bwd": scatter_of(1, ("in",), sums1), "ret_bwd": pair_exchange_0,
        "attn_bwd": scatter_of(1, rest, sums1), "proj_wgrad": scatter_of(0, rest, sums0), "proj_bwd": scatter_of(0, ("in",))})
    grad_x, d_in_g, d_in_b = row_call("ln_in_bwd", vjp_fn(f_ln_in, 1, 1), [Row(x2d), Row(dh)],
                                      [_row2(ln_in_g), _row2(ln_in_b)], [(D_MODEL, F32)], [(1, D_MODEL), (1, D_MODEL)], tm=256)

    small_names = ("rel_bias", "wlr", "blr", "gnorm", "ln1_g", "ln1_b", "ln2_g", "ln2_b")
    parts = [d_in_g, d_in_b] + [small[l][n] for l in range(DEPTH) for n in small_names] + [loss_lanes]
    shapes = [(D_MODEL,), (D_MODEL,)] + [s for l in range(DEPTH) for s in (
        (ATTN_HEADS, N_REL), (GLA_RANK, 512), (512,), (LIN_VAL_DIM,), (D_MODEL,), (D_MODEL,), (D_MODEL,), (D_MODEL,))] + [(128,)]
    red = _unpack(all_reduce_small(_pack(parts)), shapes)
    loss = jnp.sum(red[-1])
    g_small = {"ln_in_g": red[0], "ln_in_b": red[1]}
    for l in range(DEPTH):
        for j, n in enumerate(small_names):
            g_small[(l, n)] = red[2 + l * len(small_names) + j]
    shard = LIN_HEADS * LIN_KEY_DIM // N_DEV
    g_wlr = jnp.stack([lax.dynamic_slice_in_dim(g_small[(l, "wlr")], me * shard, shard, axis=1) for l in range(DEPTH)])
    stack = lambda n: jnp.stack([g_small[(l, n)] for l in range(DEPTH)])
    g_rep = [g_small["ln_in_g"], g_small["ln_in_b"], stack("rel_bias"), g_wlr, stack("blr"), stack("gnorm"),
             stack("ln1_g"), stack("ln1_b"), stack("ln2_g"), stack("ln2_b")]
    w_rep = [ln_in_g, ln_in_b, rel_bias, gla_w_lr, gla_b_lr, gla_norm_g, ln1_g, ln1_b, ln2_g, ln2_b]
    m_rep = [m_ln_in_g, m_ln_in_b, m_rel_bias, m_gla_w_lr, m_gla_b_lr, m_gla_norm_g, m_ln1_g, m_ln1_b, m_ln2_g, m_ln2_b]
    v_rep = [v_ln_in_g, v_ln_in_b, v_rel_bias, v_gla_w_lr, v_gla_b_lr, v_gla_norm_g, v_ln1_g, v_ln1_b, v_ln2_g, v_ln2_b]
    rep_shapes = [w.shape for w in w_rep]
    d_pack, m_pack, v_pack = row_call("adam_small", f_adam, [Row(_pack(g_rep)), Row(_pack(w_rep)), Row(_pack(m_rep)), Row(_pack(v_rep))],
                                      [], [(128, F32)] * 3, tm=256)
    d_rep, nm_rep, nv_rep = _unpack(d_pack, rep_shapes), _unpack(m_pack, rep_shapes), _unpack(v_pack, rep_shapes)

    def owned(key, w, m, v, name, tm):
        C = w.shape[-1]
        flat = lambda a: a.reshape(DEPTH, -1, C)
        outs = None
        for l in range(DEPTH):
            outs = _sum_adam(name, received[l][key].reshape(4, -1, C), flat(w), flat(m), flat(v), l, tm, outs)
        return [o.reshape(w.shape) for o in outs]

    o_in = owned("in", w_in, m_w_in, v_w_in, "adam_w_in", 256)
    o_br = owned("branch", w_branch, m_w_branch, v_w_branch, "adam_w_branch", 128)
    o_out = owned("out", w_out, m_w_out, v_w_out, "adam_w_out", 128)
    o_up = owned("up", w_up, m_w_up, v_w_up, "adam_w_up", 256)
    o_down = owned("down", w_down, m_w_down, v_w_down, "adam_w_down", 256)

    def tree(j, rep):
        return [rep[0], rep[1], o_in[j], rep[2], rep[3], rep[4], rep[5], o_br[j], o_out[j], rep[6], rep[7], o_up[j], o_down[j],
                rep[8], rep[9]]

    return (loss, grad_x.reshape(1, T, D_MODEL), *tree(0, g_rep), *tree(1, d_rep), *tree(2, nm_rep), *tree(3, nv_rep))
```

```python
import functools
import math

import jax
import jax.numpy as jnp
from jax import lax
from jax.experimental import pallas as pl
from jax.experimental.pallas import tpu as pltpu

F32 = jnp.float32
MXU_DTYPE = jnp.bfloat16
WIRE_DTYPE = jnp.bfloat16
PROJ_DTYPE = jnp.bfloat16

N_DEV = 8
D_MODEL = 1024
DEPTH = 2
CHUNK = 64
ATTN_HEADS = 8
ATTN_HEAD_DIM = 128
ATTN_LEFT_CHUNKS = 8
BAND = (ATTN_LEFT_CHUNKS + 1) * CHUNK
PAD_ROWS = ATTN_LEFT_CHUNKS * CHUNK
REL_CLIP = 2 * CHUNK
N_REL = 2 * REL_CLIP + 1
REL_MIN = REL_CLIP - (CHUNK - 1)
LIN_HEADS = 4
LIN_KEY_DIM = 128
LIN_VAL_DIM = 256
GLA_RANK = 16
GLA_GATE_NORMALIZER = 16.0
ROPE_BASE = 10000.0
D_FF = 4 * D_MODEL
LN_EPS = 1e-5
NEG_INF = -1e30
ALPHA = (2 * DEPTH) ** 0.25
ADAM_LR, ADAM_B1, ADAM_B2, ADAM_EPS, ADAM_WD, ADAM_STEP = 0.001, 0.9, 0.999, 1e-08, 0.01, 10

PROJ_WIDTH = 12304
GLR_SRC = 9216
GATES_SRC = GLR_SRC + GLA_RANK
MAIN_W = 9216
GATES_W = 3 * D_MODEL
GATES_OFF = MAIN_W
GLR_OFF = MAIN_W + GATES_W
PROJ_PAD = 12800
GLR_BLOCK = 128
COL = dict(aq=0, ak=1024, av=2048, rq=3072, rk=3584, rv=4096, rg=5120, gq=6144, gk=6656, gv=7168, gg=8192)

SEG_CHUNKS = 8
SEG = SEG_CHUNKS * CHUNK
VMEM_BIG = 56 * 1024 * 1024

_NN = (((1,), (0,)), ((), ()))
_NT = (((1,), (1,)), ((), ()))
_TN = (((0,), (0,)), ((), ()))


def _mxu(a, b, dn):
    return lax.dot_general(a.astype(MXU_DTYPE), b.astype(MXU_DTYPE), dn, preferred_element_type=F32)


@jax.custom_vjp
def dot_nn(a, b):
    return _mxu(a, b, _NN)


def _dot_nn_fwd(a, b):
    return _mxu(a, b, _NN), (a, b)


def _dot_nn_bwd(res, g):
    a, b = res
    return _mxu(g, b, _NT).astype(a.dtype), _mxu(a, g, _TN).astype(b.dtype)


dot_nn.defvjp(_dot_nn_fwd, _dot_nn_bwd)


@jax.custom_vjp
def dot_nt(a, b):
    return _mxu(a, b, _NT)


def _dot_nt_fwd(a, b):
    return _mxu(a, b, _NT), (a, b)


def _dot_nt_bwd(res, g):
    a, b = res
    return _mxu(g, b, _NN).astype(a.dtype), _mxu(g, a, _TN).astype(b.dtype)


dot_nt.defvjp(_dot_nt_fwd, _dot_nt_bwd)


@jax.custom_vjp
def dot_tn(a, b):
    return _mxu(a, b, _TN)


def _dot_tn_fwd(a, b):
    return _mxu(a, b, _TN), (a, b)


def _dot_tn_bwd(res, g):
    a, b = res
    return _mxu(b, g, _NT).astype(a.dtype), _mxu(a, g, _NN).astype(b.dtype)


dot_tn.defvjp(_dot_tn_fwd, _dot_tn_bwd)


def _tri(n, lower):
    r = lax.broadcasted_iota(jnp.int32, (n, n), 0)
    c = lax.broadcasted_iota(jnp.int32, (n, n), 1)
    return (r >= c) if lower else (r <= c)


def _tri_dot(lower, x):
    tri = _tri(x.shape[0], lower).astype(jnp.bfloat16)
    hi = x.astype(jnp.bfloat16)
    rest = x - hi.astype(F32)
    mid = rest.astype(jnp.bfloat16)
    lo = (rest - mid.astype(F32)).astype(jnp.bfloat16)
    dot = lambda part: lax.dot_general(tri, part, _NN, preferred_element_type=F32)
    return dot(hi) + dot(mid) + dot(lo)


@jax.custom_vjp
def chunk_cumsum(x):
    return _tri_dot(True, x)


def _chunk_cumsum_fwd(x):
    return chunk_cumsum(x), None


def _chunk_cumsum_bwd(_, g):
    return (_tri_dot(False, g),)


chunk_cumsum.defvjp(_chunk_cumsum_fwd, _chunk_cumsum_bwd)


@jax.custom_vjp
def swap_halves(t):
    return pltpu.roll(t, t.shape[-1] // 2, t.ndim - 1)


def _swap_fwd(t):
    return swap_halves(t), None


def _swap_bwd(_, g):
    return (swap_halves(g),)


swap_halves.defvjp(_swap_fwd, _swap_bwd)


def mm(name, a, b, dims, *, tm, tn, tk, out_dtype=F32, extras=(), row_params=(), epi=None, n_out=1, out_dtypes=None,
       side=None, column_slabs=False):
    if dims == "nn":
        (M, K), N = a.shape, b.shape[1]
    elif dims == "nt":
        (M, K), N = a.shape, b.shape[0]
    else:
        (K, M), N = a.shape, b.shape[1]
    tm, tn, tk = min(tm, M), min(tn, N), min(tk, K)
    assert M % tm == 0 and N % tn == 0 and K % tk == 0, (name, M, N, K, tm, tn, tk)
    nk = K // tk
    dn = {"nn": _NN, "nt": _NT, "tn": _TN}[dims]
    a_spec = pl.BlockSpec((tk, tm), lambda i, j, k: (k, i)) if dims == "tn" else pl.BlockSpec((tm, tk), lambda i, j, k: (i, k))
    b_spec = pl.BlockSpec((tn, tk), lambda i, j, k: (j, k)) if dims == "nt" else pl.BlockSpec((tk, tn), lambda i, j, k: (k, j))
    o_spec = pl.BlockSpec((tm, tn), lambda i, j, k: (i, j))
    row_spec = pl.BlockSpec((1, tn), lambda i, j, k: (0, j))
    n_ex = len(extras) + len(row_params)
    out_dtypes = out_dtypes or (out_dtype,) * n_out

    def finish(r, ex_refs, o_refs):
        if epi is not None:
            r = epi(r, *[e[...] for e in ex_refs])
        r = r if isinstance(r, tuple) else (r,)
        for o, v in zip(o_refs, r):
            o[...] = v.astype(o.dtype)

    def body(*refs):
        a_ref, b_ref = refs[:2]
        ex_refs = refs[2:2 + n_ex]
        o_refs = refs[2 + n_ex:2 + n_ex + n_out]
        part = _mxu(a_ref[...], b_ref[...], dn)
        if nk == 1:
            finish(part, ex_refs, o_refs)
            return
        acc = refs[-1]
        k = pl.program_id(2)

        @pl.when(k == 0)
        def _():
            acc[...] = part

        @pl.when(k > 0)
        def _():
            acc[...] += part

        @pl.when(k == nk - 1)
        def _():
            finish(acc[...], ex_refs, o_refs)

    outs = _side_call(
        side, body,
        name=name,
        grid=(M // tm, N // tn, nk),
        in_specs=[a_spec, b_spec] + [o_spec] * len(extras) + [row_spec] * len(row_params),
        out_specs=[pl.BlockSpec((None, tm, tn), lambda i, j, k: (j, i, 0))] * n_out if column_slabs else [o_spec] * n_out,
        out_shape=[jax.ShapeDtypeStruct((N // tn, M, tn) if column_slabs else (M, N), dt) for dt in out_dtypes],
        scratch_shapes=[pltpu.VMEM((tm, tn), F32)] if nk > 1 else [],
        args=[a, b, *extras, *row_params],
        semantics=("parallel", "parallel", "arbitrary"),
    )
    return outs[0] if n_out == 1 else outs


class Row:
    def __init__(self, arr, width=None, col=0, lead=(), rowmap=None):
        self.arr, self.lead, self.col = arr, tuple(lead), col
        self.width = arr.shape[-1] if width is None else width
        self.rowmap = rowmap

    def spec(self, tm):
        lead, col, rowmap = self.lead, self.col, self.rowmap or (lambda i: i)
        return pl.BlockSpec((None,) * len(lead) + (tm, self.width), lambda i: (*lead, rowmap(i), col))


def row_call(name, fn, rows, params, new_outs, red_shapes=(), *, tm, into=None, into_cols=(), stack=None, stacked_onto=None):
    R = rows[0].arr.shape[len(rows[0].lead)]
    tm = min(tm, R)
    assert R % tm == 0, (name, R, tm)
    n_row, n_par, n_new, n_into, n_red = len(rows), len(params), len(new_outs), len(into_cols), len(red_shapes)
    fresh = isinstance(into, jax.ShapeDtypeStruct)

    def body(*refs):
        row_refs = refs[:n_row]
        par_refs = refs[n_row:n_row + n_par]
        pos = n_row + n_par + (1 if (into is not None and not fresh) else 0) + (n_new if stacked_onto is not None else 0)
        new_refs = refs[pos:pos + n_new]
        red_refs = refs[pos + n_new:pos + n_new + n_red]
        pos2 = pos + n_new + n_red
        into_ref = refs[pos2] if into is not None else None
        stage = refs[pos2 + 1:pos2 + 1 + n_into] if into is not None else ()
        i = pl.program_id(0)
        widen = lambda t: t.astype(F32) if jnp.issubdtype(t.dtype, jnp.floating) else t
        vals = fn(*[widen(r[...]) for r in row_refs], *[p[...] for p in par_refs])
        vals = vals if isinstance(vals, (tuple, list)) else (vals,)
        for o, v in zip(new_refs, vals[:n_new]):
            o[...] = v.astype(o.dtype)

        def window_copy(j, step):
            width, off = into_cols[j]
            rows_at = pl.ds(pl.multiple_of(step * tm, tm), tm)
            return pltpu.make_async_copy(stage[j], into_ref.at[rows_at, pl.ds(off, width)], refs[pos2 + 1 + n_into].at[j])

        for j, v in enumerate(vals[n_new:n_new + n_into]):
            @pl.when(i > 0)
            def _(j=j):
                window_copy(j, i - 1).wait()

            stage[j][...] = v.astype(stage[j].dtype)
            window_copy(j, i).start()

            @pl.when(i == R // tm - 1)
            def _(j=j):
                window_copy(j, i).wait()
        for o, v in zip(red_refs, vals[n_new + n_into:]):
            @pl.when(i == 0)
            def _(o=o, v=v):
                o[...] = v.astype(o.dtype)

            @pl.when(i > 0)
            def _(o=o, v=v):
                o[...] += v.astype(o.dtype)

    in_specs = [r.spec(tm) for r in rows] + [pl.BlockSpec(p.shape, lambda i, nd=p.ndim: (0,) * nd) for p in params]
    args = [r.arr for r in rows] + list(params)
    if stack is None:
        out_specs = [pl.BlockSpec((tm, w), lambda i: (i, 0)) for w, _ in new_outs]
        out_shape = [jax.ShapeDtypeStruct((R, w), dt) for w, dt in new_outs]
    else:
        index, depth = stack
        out_specs = [pl.BlockSpec((None, tm, w), lambda i: (index, i, 0)) for w, _ in new_outs]
        out_shape = [jax.ShapeDtypeStruct((depth, R, w), dt) for w, dt in new_outs]
    out_specs += [pl.BlockSpec(s, lambda i, nd=len(s): (0,) * nd) for s in red_shapes]
    out_shape += [jax.ShapeDtypeStruct(s, F32) for s in red_shapes]
    aliases, scratch = {}, []
    if into is not None:
        out_specs.append(pl.BlockSpec(memory_space=pl.ANY))
        out_shape.append(jax.ShapeDtypeStruct(into.shape, into.dtype))
        if not fresh:
            in_specs.append(pl.BlockSpec(memory_space=pl.ANY))
            args.append(into)
            aliases = {len(args) - 1: len(out_shape) - 1}
        scratch = [pltpu.VMEM((tm, w), into.dtype) for w, _ in into_cols] + [pltpu.SemaphoreType.DMA((n_into,))]
    if stacked_onto is not None:
        for j, prev in enumerate(stacked_onto):
            in_specs.append(pl.BlockSpec(memory_space=pl.ANY))
            args.append(prev)
            aliases[len(args) - 1] = j
    outs = pl.pallas_call(
        body,
        name=name,
        grid=(R // tm,),
        in_specs=in_specs,
        out_specs=out_specs,
        out_shape=out_shape,
        scratch_shapes=scratch,
        input_output_aliases=aliases,
        compiler_params=pltpu.CompilerParams(dimension_semantics=("arbitrary",), vmem_limit_bytes=VMEM_BIG),
    )(*args)
    return list(outs)


def vjp_fn(fn, n_row, n_out, passthrough=0):
    def bwd(*args):
        rows = [a.astype(F32) for a in args[:n_row]]
        cts = args[n_row:n_row + n_out]
        thru = args[n_row + n_out:n_row + n_out + passthrough]
        pars = args[n_row + n_out + passthrough:]
        outs, pull = jax.vjp(lambda *xs: tuple(fn(*xs)), *rows, *pars)
        grads = pull(tuple(c.astype(o.dtype) for c, o in zip(cts, outs)))
        return tuple(grads[:n_row]) + tuple(thru) + tuple(grads[n_row:])

    return bwd


def _ln(x, g, b):
    mu = jnp.mean(x, axis=-1, keepdims=True)
    xc = x - mu
    var = jnp.mean(xc * xc, axis=-1, keepdims=True)
    return xc * lax.rsqrt(var + LN_EPS) * g + b


def f_ln_in(x, g, b):
    return (_ln(x, g, b),)


def f_deepnorm(x, y, g, b):
    return (_ln(ALPHA * x + y, g, b),)


def _twice(fn):
    return lambda *args: fn(*args) * 2


STREAM = [(D_MODEL, F32), (D_MODEL, MXU_DTYPE)]


def _rope(t, cos, sin):
    parts = []
    for h in range(LIN_HEADS):
        th = t[:, h * LIN_KEY_DIM:(h + 1) * LIN_KEY_DIM]
        parts.append(th * cos + swap_halves(th) * sin)
    return jnp.concatenate(parts, axis=1)


def f_ret_pre(rq, rk, cos, sin):
    return _rope(rq, cos, sin), _rope(rk, cos, sin) * LIN_KEY_DIM ** -0.5


def _log_sigmoid(x):
    return jnp.minimum(x, 0.0) - jnp.log(1.0 + jnp.exp(-jnp.abs(x)))


def f_gla_pre(gq, glr, wlr, blr):
    gate_pre = dot_nn(glr, wlr) + blr
    return gq * LIN_KEY_DIM ** -0.5, _log_sigmoid(gate_pre) / GLA_GATE_NORMALIZER


def f_normgate(ret_raw, rg, gla_raw, gg, gnorm):
    rets, glas = [], []
    for h in range(LIN_HEADS):
        sl = slice(h * LIN_VAL_DIM, (h + 1) * LIN_VAL_DIM)
        t = ret_raw[:, sl]
        mu = jnp.mean(t, axis=-1, keepdims=True)
        tc = t - mu
        rets.append(tc * lax.rsqrt(jnp.mean(tc * tc, axis=-1, keepdims=True) + LN_EPS))
        u = gla_raw[:, sl]
        glas.append(u * lax.rsqrt(jnp.mean(u * u, axis=-1, keepdims=True) + LN_EPS) * gnorm)
    ret = jax.nn.silu(rg) * jnp.concatenate(rets, axis=1)
    gla = jax.nn.silu(gg) * jnp.concatenate(glas, axis=1)
    return ret, gla


def f_merge(p0, p1, p2, gl):
    out = jax.nn.sigmoid(gl[:, :D_MODEL]) * p0
    out = out + jax.nn.sigmoid(gl[:, D_MODEL:2 * D_MODEL]) * p1
    return (out + jax.nn.sigmoid(gl[:, 2 * D_MODEL:]) * p2,)


def f_loss(y, tgt):
    diff = y - tgt
    sq = jnp.sum(diff * diff, axis=0, keepdims=True)
    lanes = sq[:, :128]
    for j in range(1, D_MODEL // 128):
        lanes = lanes + sq[:, j * 128:(j + 1) * 128]
    return diff * (1.0 / D_MODEL), lanes * (0.5 / D_MODEL)


def _adam(g, w, m, v):
    m = ADAM_B1 * m + (1.0 - ADAM_B1) * g
    v = ADAM_B2 * v + (1.0 - ADAM_B2) * jnp.square(g)
    m_hat = m / (1.0 - ADAM_B1 ** ADAM_STEP)
    v_hat = v / (1.0 - ADAM_B2 ** ADAM_STEP)
    delta = -ADAM_LR * (m_hat / (jnp.sqrt(v_hat) + ADAM_EPS) + ADAM_WD * w)
    return delta, m, v


def f_sum_adam(*args):
    g = args[0].astype(F32)
    for p in args[1:-3]:
        g = g + p.astype(F32)
    return (g,) + _adam(g, *args[-3:])


def f_adam(g, w, m, v):
    return _adam(g, w, m, v)


GROUP = 4
GROUP_ROWS = GROUP * CHUNK
GROUP_KEYS = (GROUP + ATTN_LEFT_CHUNKS) * CHUNK
DIAG_ROWS = REL_CLIP + CHUNK
GRAD_ROWS = 200


def _blocks_by_delta():
    blocks = {}
    for qi in range(GROUP):
        for J in range(GROUP_KEYS // 128):
            blocks.setdefault(qi - 2 * J, []).append((qi, J))
    return blocks


def bias_table(rel_bias, side=None):
    n_tab = rel_bias.shape[0] * ATTN_HEADS

    def body(rb_ref, o_ref):
        h = pl.program_id(0)
        rm = lax.broadcasted_iota(jnp.int32, (CHUNK, 128), 0)
        cm = lax.broadcasted_iota(jnp.int32, (CHUNK, 128), 1)
        ahead = jnp.where(cm >= CHUNK, 1, 0)
        for delta, where in _blocks_by_delta().items():
            base = 64 * delta + PAD_ROWS
            rel = jnp.minimum(base + rm - cm, REL_CLIP) + REL_CLIP
            in_band = (ahead - delta >= 0) & (ahead - delta <= ATTN_LEFT_CHUNKS)
            lo = max(REL_MIN, min(base - 127, REL_CLIP) + REL_CLIP)
            hi = min(base + CHUNK - 1, REL_CLIP) + REL_CLIP

            def step(r, acc, rel=rel):
                return jnp.where(rel == r, rb_ref[h * N_REL + r], acc)

            blk = jnp.zeros((CHUNK, 128), F32) if hi < lo else lax.fori_loop(lo, hi + 1, step, jnp.zeros((CHUNK, 128), F32))
            blk = jnp.where(in_band, blk, NEG_INF)
            for qi, J in where:
                o_ref[pl.ds(qi * CHUNK, CHUNK), pl.ds(J * 128, 128)] = blk

    return _side_call(
        side, body,
        name="bias_table",
        grid=(n_tab,),
        in_specs=[pl.BlockSpec(memory_space=pltpu.SMEM)],
        out_specs=[pl.BlockSpec((None, GROUP_ROWS, GROUP_KEYS), lambda h: (h, 0, 0))],
        out_shape=[jax.ShapeDtypeStruct((n_tab, GROUP_ROWS, GROUP_KEYS), F32)],
        scratch_shapes=[],
        args=[rel_bias.reshape(n_tab * N_REL)],
    )[0]


def bias_table_grad(dbias):
    blocks = _blocks_by_delta()
    below_clip = range(-(CHUNK - 1), REL_CLIP)
    deltas = [d for d in sorted(blocks) if any(-127 <= pos - 64 * d - PAD_ROWS <= CHUNK - 1 for pos in below_clip)]

    def body(d_ref, o_ref, diag):
        sums = []
        for delta in deltas:
            acc = None
            for qi, J in blocks[delta]:
                blk = d_ref[pl.ds(qi * CHUNK, CHUNK), pl.ds(J * 128, 128)]
                acc = blk if acc is None else acc + blk
            sums.append(acc)
        off = lax.broadcasted_iota(jnp.int32, (CHUNK, 128), 0) - lax.broadcasted_iota(jnp.int32, (CHUNK, 128), 1)
        diag[...] = jnp.zeros_like(diag)

        def step(t, carry):
            pos = t - (CHUNK - 1)
            picked = jnp.zeros((CHUNK, 128), F32)
            for delta, s in zip(deltas, sums):
                picked = picked + jnp.where(off == pos - 64 * delta - PAD_ROWS, s, 0.0)
            diag[pl.ds(t, 1), :] = jnp.sum(picked, axis=0, keepdims=True)
            return carry

        lax.fori_loop(0, DIAG_ROWS, step, 0, unroll=8)
        d = d_ref[...]
        total = jnp.sum(jnp.sum(d, axis=1, keepdims=True), axis=0, keepdims=True)
        o_ref[pl.ds(0, DIAG_ROWS), :] = jnp.broadcast_to(jnp.sum(diag[...], axis=1, keepdims=True), (DIAG_ROWS, 128))
        o_ref[pl.ds(DIAG_ROWS, GRAD_ROWS - DIAG_ROWS), :] = jnp.broadcast_to(total, (GRAD_ROWS - DIAG_ROWS, 128))

    out = pl.pallas_call(
        body,
        name="bias_table_grad",
        grid=(ATTN_HEADS,),
        in_specs=[pl.BlockSpec((None, GROUP_ROWS, GROUP_KEYS), lambda h: (h, 0, 0))],
        out_specs=pl.BlockSpec((None, GRAD_ROWS, 128), lambda h: (h, 0, 0)),
        out_shape=jax.ShapeDtypeStruct((ATTN_HEADS, GRAD_ROWS, 128), F32),
        scratch_shapes=[pltpu.VMEM((DIAG_ROWS, 128), F32)],
        compiler_params=pltpu.CompilerParams(dimension_semantics=("arbitrary",)),
    )(dbias)
    below = out[:, :DIAG_ROWS - 1, 0]
    clipped = out[:, DIAG_ROWS, 0] - jnp.sum(below, axis=1)
    return jnp.concatenate([jnp.zeros((ATTN_HEADS, REL_MIN), F32), below, clipped[:, None]], axis=1)


def _attn_stage(q_ref, k_ref, v_ref, qs, kp, vp, T):
    qs[...] = (q_ref[...] * ATTN_HEAD_DIM ** -0.5).astype(qs.dtype)
    for dst, src in ((kp, k_ref), (vp, v_ref)):
        dst[pl.ds(0, PAD_ROWS), :] = jnp.zeros((PAD_ROWS, ATTN_HEAD_DIM), dst.dtype)
        dst[pl.ds(PAD_ROWS, T), :] = src[...].astype(dst.dtype)


STRIP = 256


def _strip_scores(g, t, first_groups, r0, qc, kp, b_ref):
    keys = pl.ds(r0 + t * STRIP if first_groups else pl.multiple_of(r0 + t * STRIP, STRIP), STRIP)
    s = _mxu(qc, kp[keys, :], _NT) + b_ref[:, t * STRIP:(t + 1) * STRIP]
    if first_groups:
        k_chunk = lax.shift_right_logical(lax.broadcasted_iota(jnp.int32, (GROUP_ROWS, STRIP), 1), 6)
        s = jnp.where(g * GROUP + t * (STRIP // CHUNK) + k_chunk >= ATTN_LEFT_CHUNKS, s, NEG_INF)
    return s, keys


def _for_groups(T, step):
    n_first = min(ATTN_LEFT_CHUNKS // GROUP, T // GROUP_ROWS)
    for g in range(n_first):
        step(g, True)
    n_loop = T // GROUP_ROWS - n_first
    lax.fori_loop(n_first, T // GROUP_ROWS, lambda g, c: (step(g, False), c)[1], 0, unroll=2 if n_loop % 2 == 0 else 1)


def _head_spec(T, base):
    return pl.BlockSpec((T, ATTN_HEAD_DIM), lambda h: (0, base + h))


def _attn_scratch(T):
    return [pltpu.VMEM((T, ATTN_HEAD_DIM), MXU_DTYPE)] + [pltpu.VMEM((T + PAD_ROWS, ATTN_HEAD_DIM), MXU_DTYPE)] * 2


def attn_fwd(proj, bias_tab, first_tab, side=None):
    T = proj.shape[0]

    def body(q_ref, k_ref, v_ref, b_ref, o_ref, lse_ref, qs, kp, vp):
        _attn_stage(q_ref, k_ref, v_ref, qs, kp, vp, T)

        def step(g, first):
            r0 = g * GROUP_ROWS if first else pl.multiple_of(g * GROUP_ROWS, GROUP_ROWS)
            rows = pl.ds(r0, GROUP_ROWS)
            qc = qs[rows, :]
            m = jnp.full((GROUP_ROWS, 1), NEG_INF, F32)
            l = jnp.zeros((GROUP_ROWS, 1), F32)
            acc = jnp.zeros((GROUP_ROWS, ATTN_HEAD_DIM), F32)
            for t in range(GROUP_KEYS // STRIP):
                s, keys = _strip_scores(g, t, first, r0, qc, kp, b_ref)
                m_new = jnp.maximum(m, jnp.max(s, axis=-1, keepdims=True))
                keep = jnp.exp(m - m_new)
                e = jnp.exp(s - m_new)
                l = l * keep + jnp.sum(e, axis=-1, keepdims=True)
                acc = acc * keep + _mxu(e, vp[keys, :], _NN)
                m = m_new
            o_ref[rows, :] = (acc * (1.0 / l)).astype(o_ref.dtype)
            lse_ref[rows, :] = jnp.broadcast_to(m + jnp.log(l), (GROUP_ROWS, ATTN_HEAD_DIM))

        _for_groups(T, step)

    out, lse = _side_call(
        side, body,
        name="attn_fwd",
        grid=(ATTN_HEADS,),
        in_specs=[_head_spec(T, COL["aq"] // 128), _head_spec(T, COL["ak"] // 128), _head_spec(T, COL["av"] // 128),
                  pl.BlockSpec((None, GROUP_ROWS, GROUP_KEYS), lambda h: (first_tab + h, 0, 0))],
        out_specs=[pl.BlockSpec((T, ATTN_HEAD_DIM), lambda h: (0, h))] * 2,
        out_shape=[jax.ShapeDtypeStruct((T, ATTN_HEADS * ATTN_HEAD_DIM), MXU_DTYPE),
                   jax.ShapeDtypeStruct((T, ATTN_HEADS * ATTN_HEAD_DIM), F32)],
        scratch_shapes=_attn_scratch(T),
        args=[proj, proj, proj, bias_tab],
    )
    return out, lse


def attn_bwd(proj, bias_tab, first_tab, out, lse, dout, dproj, side=None):
    T = proj.shape[0]

    def body(q_ref, k_ref, v_ref, b_ref, do_ref, o_ref, lse_ref, dproj_in, dbias_ref, dproj_ref, qs, kp, vp, dkp, dvp, dq, stage):
        del dproj_in
        h = pl.program_id(0)
        _attn_stage(q_ref, k_ref, v_ref, qs, kp, vp, T)
        dkp[...] = jnp.zeros_like(dkp)
        dvp[...] = jnp.zeros_like(dvp)
        dbias_ref[...] = jnp.zeros_like(dbias_ref)

        def step(g, first):
            r0 = g * GROUP_ROWS if first else pl.multiple_of(g * GROUP_ROWS, GROUP_ROWS)
            rows = pl.ds(r0, GROUP_ROWS)
            qc, do = qs[rows, :], do_ref[rows, :]
            lse = jnp.concatenate([lse_ref[rows, :]] * (STRIP // ATTN_HEAD_DIM), axis=1)
            row_term = jnp.sum(do.astype(F32) * o_ref[rows, :].astype(F32), axis=-1, keepdims=True)
            dq_acc = jnp.zeros((GROUP_ROWS, ATTN_HEAD_DIM), F32)
            for t in range(GROUP_KEYS // STRIP):
                s, keys = _strip_scores(g, t, first, r0, qc, kp, b_ref)
                cols = slice(t * STRIP, (t + 1) * STRIP)
                kb, vb = kp[keys, :], vp[keys, :]
                p = jnp.exp(s - lse)
                dvp[keys, :] += _mxu(p, do, _TN)
                ds = p * (_mxu(do, vb, _NT) - row_term)
                dq_acc = dq_acc + _mxu(ds, kb, _NN)
                dkp[keys, :] += _mxu(ds, qc, _TN)
                dbias_ref[:, cols] += ds
            dq[rows, :] = dq_acc * ATTN_HEAD_DIM ** -0.5

        _for_groups(T, step)
        col = pl.multiple_of(h * ATTN_HEAD_DIM, ATTN_HEAD_DIM)
        for grad, first_row, base in ((dq, 0, "aq"), (dkp, PAD_ROWS, "ak"), (dvp, PAD_ROWS, "av")):
            stage[...] = grad[pl.ds(first_row, T), :].astype(stage.dtype)
            pltpu.sync_copy(stage, dproj_ref.at[:, pl.ds(COL[base] + col, ATTN_HEAD_DIM)])

    dbias, dproj = _side_call(
        side, body,
        name="attn_bwd",
        grid=(ATTN_HEADS,),
        in_specs=[_head_spec(T, COL["aq"] // 128), _head_spec(T, COL["ak"] // 128), _head_spec(T, COL["av"] // 128),
                  pl.BlockSpec((None, GROUP_ROWS, GROUP_KEYS), lambda h: (first_tab + h, 0, 0))]
        + [pl.BlockSpec((T, ATTN_HEAD_DIM), lambda h: (0, h))] * 3 + [pl.BlockSpec(memory_space=pl.ANY)],
        out_specs=[pl.BlockSpec((None, GROUP_ROWS, GROUP_KEYS), lambda h: (h, 0, 0)), pl.BlockSpec(memory_space=pl.ANY)],
        out_shape=[jax.ShapeDtypeStruct((ATTN_HEADS, GROUP_ROWS, GROUP_KEYS), F32),
                   jax.ShapeDtypeStruct(dproj.shape, dproj.dtype)],
        scratch_shapes=_attn_scratch(T) + [pltpu.VMEM((T + PAD_ROWS, ATTN_HEAD_DIM), F32)] * 2
        + [pltpu.VMEM((T, ATTN_HEAD_DIM), F32), pltpu.VMEM((T, ATTN_HEAD_DIM), dproj.dtype)],
        args=[proj, proj, proj, bias_tab, dout, out, lse, dproj],
        aliases={7: 1},
    )
    return dproj, dbias


def _lin_segment(q, k, v, la, st):
    lower = _tri(CHUNK, True)
    steps = (lax.broadcasted_iota(jnp.int32, (CHUNK, 1), 0) + 1).astype(F32)
    outs = []
    for i in range(SEG_CHUNKS):
        sl = slice(i * CHUNK, (i + 1) * CHUNK)
        qi, ki, vi = q[sl], k[sl], v[sl]
        if la.shape[0] == 1:
            cum, last = la * steps, la * float(CHUNK)
        else:
            cum, last = chunk_cumsum(la[sl]), jnp.sum(la[sl], axis=0, keepdims=True)
        e_pos, e_neg = jnp.exp(cum), jnp.exp(-cum)
        q_fwd = qi * e_pos
        s_fwd = dot_nt(q_fwd, ki * e_neg)
        s_bwd = dot_nt(qi * e_neg, ki * e_pos)
        o = dot_nn(jnp.where(lower, s_fwd, s_bwd), vi) + dot_nt(q_fwd, st)
        st = jnp.exp(last) * st + dot_tn(vi, ki * jnp.exp(last - cum))
        outs.append(o)
    return jnp.concatenate(outs, axis=0), st


class Win:
    def __init__(self, arr, width, base=0):
        self.arr, self.width, self.base = arr, width, base


HEADS_PER_STEP = 2


def _lin_spec(w, seg_of):
    assert w.base % HEADS_PER_STEP == 0
    first, width = w.base // HEADS_PER_STEP, HEADS_PER_STEP * w.width
    if w.arr.shape[0] == 1:
        return pl.BlockSpec((1, width), lambda p, s: (0, first + p))
    return pl.BlockSpec((SEG, width), lambda p, s: (seg_of(s), first + p))


def _head_cols(ref, j, width):
    return ref[:, j * width:(j + 1) * width]


def _decay_rows(la_ref, j):
    return _head_cols(la_ref, j, LIN_KEY_DIM)


_STATE_BLOCK = (HEADS_PER_STEP, None, LIN_VAL_DIM, LIN_KEY_DIM)


def lin_fwd(name, q, k, v, la, side=None):
    T = q.arr.shape[0]
    nseg = T // SEG

    def body(q_ref, k_ref, v_ref, la_ref, o_ref, st_ref, st):
        @pl.when(pl.program_id(1) == 0)
        def _():
            st[...] = jnp.zeros_like(st)

        for j in range(HEADS_PER_STEP):
            st0 = st[j]
            st_ref[j] = st0
            o, st1 = _lin_segment(_head_cols(q_ref, j, LIN_KEY_DIM), _head_cols(k_ref, j, LIN_KEY_DIM),
                                  _head_cols(v_ref, j, LIN_VAL_DIM), _decay_rows(la_ref, j), st0)
            o_ref[:, j * LIN_VAL_DIM:(j + 1) * LIN_VAL_DIM] = o
            st[j] = st1

    fwd = lambda s: s
    out, states = _side_call(
        side, body,
        name=name,
        grid=(LIN_HEADS // HEADS_PER_STEP, nseg),
        in_specs=[_lin_spec(w, fwd) for w in (q, k, v, la)],
        out_specs=[pl.BlockSpec((SEG, HEADS_PER_STEP * LIN_VAL_DIM), lambda p, s: (s, p)),
                   pl.BlockSpec(_STATE_BLOCK, lambda p, s: (p, s, 0, 0))],
        out_shape=[jax.ShapeDtypeStruct((T, LIN_HEADS * LIN_VAL_DIM), F32),
                   jax.ShapeDtypeStruct((LIN_HEADS, nseg, LIN_VAL_DIM, LIN_KEY_DIM), F32)],
        scratch_shapes=[pltpu.VMEM((HEADS_PER_STEP, LIN_VAL_DIM, LIN_KEY_DIM), F32)],
        args=[q.arr, k.arr, v.arr, la.arr],
    )
    return out, states


def lin_bwd(name, q, k, v, la, states, dout, with_decay_grad, side=None, into=None, windows=None):
    T = q.arr.shape[0]
    nseg = T // SEG
    n_out = 4 if with_decay_grad else 3

    widths = [LIN_KEY_DIM, LIN_KEY_DIM, LIN_VAL_DIM] + ([LIN_KEY_DIM] if with_decay_grad else [])
    windows = dict(windows or {})
    fresh = [i for i in range(n_out) if i not in windows]
    placed = sorted(windows)
    n_pairs = LIN_HEADS // HEADS_PER_STEP
    rev = lambda s: nseg - 1 - s

    def body(q_ref, k_ref, v_ref, la_ref, st_ref, do_ref, *rest):
        rest = rest[1:] if placed else rest
        outs = dict(zip(fresh, rest[:len(fresh)]))
        rest = rest[len(fresh):]
        into_ref, rest = (rest[0], rest[1:]) if placed else (None, rest)
        dst, stages = rest[0], dict(zip(placed, rest[1:1 + len(placed)]))
        p, s = pl.program_id(0), pl.program_id(1)

        @pl.when(s == 0)
        def _():
            dst[...] = jnp.zeros_like(dst)

        def window_copy(n, i):
            width = HEADS_PER_STEP * widths[i]
            rows = pl.ds(pl.multiple_of(rev(s) * SEG, SEG), SEG)
            cols = pl.ds(pl.multiple_of(windows[i] + p * width, width), width)
            return pltpu.make_async_copy(stages[i], into_ref.at[rows, cols], rest[1 + len(placed)].at[n])

        for n, i in enumerate(placed):
            @pl.when((p > 0) | (s > 0))
            def _(n=n, i=i):
                window_copy(n, i).wait()

        for j in range(HEADS_PER_STEP):
            _, pull = jax.vjp(_lin_segment, _head_cols(q_ref, j, LIN_KEY_DIM).astype(F32),
                              _head_cols(k_ref, j, LIN_KEY_DIM).astype(F32), _head_cols(v_ref, j, LIN_VAL_DIM).astype(F32),
                              _decay_rows(la_ref, j), st_ref[j])
            grads = pull((_head_cols(do_ref, j, LIN_VAL_DIM), dst[j]))
            for i in range(n_out):
                dest = stages[i] if i in windows else outs[i]
                dest[:, j * widths[i]:(j + 1) * widths[i]] = grads[i].astype(dest.dtype)
            dst[j] = grads[4]

        for n, i in enumerate(placed):
            window_copy(n, i).start()

            @pl.when((p == n_pairs - 1) & (s == nseg - 1))
            def _(n=n, i=i):
                window_copy(n, i).wait()

    key_spec = pl.BlockSpec((SEG, HEADS_PER_STEP * LIN_KEY_DIM), lambda p, s: (rev(s), p))
    val_spec = pl.BlockSpec((SEG, HEADS_PER_STEP * LIN_VAL_DIM), lambda p, s: (rev(s), p))
    all_specs = [key_spec, key_spec, val_spec] + ([key_spec] if with_decay_grad else [])
    any_spec = pl.BlockSpec(memory_space=pl.ANY)
    outs = _side_call(
        side, body,
        name=name,
        grid=(n_pairs, nseg),
        in_specs=[_lin_spec(w, rev) for w in (q, k, v, la)] + [
            pl.BlockSpec(_STATE_BLOCK, lambda p, s: (p, rev(s), 0, 0)), val_spec] + ([any_spec] if placed else []),
        out_specs=[all_specs[i] for i in fresh] + ([any_spec] if placed else []),
        out_shape=[jax.ShapeDtypeStruct((T, LIN_HEADS * widths[i]), F32) for i in fresh]
        + ([jax.ShapeDtypeStruct(into.shape, into.dtype)] if placed else []),
        scratch_shapes=[pltpu.VMEM((HEADS_PER_STEP, LIN_VAL_DIM, LIN_KEY_DIM), F32)]
        + [pltpu.VMEM((SEG, HEADS_PER_STEP * widths[i]), into.dtype) for i in placed]
        + ([pltpu.SemaphoreType.DMA((len(placed),))] if placed else []),
        args=[q.arr, k.arr, v.arr, la.arr, states, dout] + ([into] if placed else []),
        aliases={6: len(fresh)} if placed else None,
    )
    return outs


def _coords():
    return lax.axis_index("x"), lax.axis_index("y"), lax.axis_index("c")


def _peer(k):
    x, y, c = _coords()
    px = 1 - x if k & 4 else x
    py = 1 - y if k & 2 else y
    pc = 1 - c if k & 1 else c
    return (px, py, pc), 4 * px + 2 * py + pc


def _remote(src, dst, send_sem, recv_sem, device):
    return pltpu.make_async_remote_copy(src_ref=src, dst_ref=dst, send_sem=send_sem, recv_sem=recv_sem,
                                        device_id=device, device_id_type=pl.DeviceIdType.MESH)


class Side:
    def __init__(self, kind, arrays):
        self.kind, self.arrays, self.n = kind, list(arrays), len(arrays)
        self.outs_per = 1
        self.slots = {"gather": N_DEV - 1, "pair": 4, "chips": 3}[kind]

    def out_shape(self):
        if self.kind == "gather":
            return [jax.ShapeDtypeStruct((N_DEV,) + a.shape, a.dtype) for a in self.arrays]
        return [jax.ShapeDtypeStruct((4,) + a.shape[1:], a.dtype) for a in self.arrays]

    def sems(self):
        return [pltpu.SemaphoreType.DMA((self.n, self.slots)), pltpu.SemaphoreType.DMA((self.n, self.slots)),
                pltpu.SemaphoreType.DMA((self.n,))]

    def start(self, ins, outs, sems):
        send, recv, own = sems
        x, y, c = _coords()
        me = 4 * x + 2 * y + c
        for a in range(self.n):
            if self.kind == "gather":
                pltpu.make_async_copy(ins[a], outs[a].at[me], own.at[a]).start()
                for k in (1, 2, 4, 6):
                    peer, _ = _peer(k)
                    _remote(ins[a], outs[a].at[me], send.at[a, k - 1], recv.at[a, k - 1], peer).start()
            elif self.kind == "pair":
                for j in range(4):
                    _remote(ins[a].at[2 * j + 1 - c], outs[a].at[j], send.at[a, j], recv.at[a, j], (x, y, 1 - c)).start()
            else:
                chip = 2 * x + y
                pltpu.make_async_copy(ins[a].at[chip], outs[a].at[chip], own.at[a]).start()
                for d in (1, 2, 3):
                    tx, ty = (1 - x if d & 2 else x), (1 - y if d & 1 else y)
                    _remote(ins[a].at[2 * tx + ty], outs[a].at[chip], send.at[a, d - 1], recv.at[a, d - 1], (tx, ty, c)).start()

    def finish(self, ins, outs, sems):
        send, recv, own = sems
        x, y, c = _coords()
        me = 4 * x + 2 * y + c
        sibling = (x, y, 1 - c)
        for a in range(self.n):
            if self.kind == "gather":
                blk = outs[a].at[me]
                for k in (2, 4, 6):
                    _, pidx = _peer(k)
                    _remote(blk, blk, send.at[a, k - 1], recv.at[a, k - 1], sibling).wait_recv()
                    _remote(outs[a].at[pidx], outs[a].at[pidx], send.at[a, k], recv.at[a, k], sibling).start()
                pltpu.make_async_copy(ins[a], blk, own.at[a]).wait()
                for k in range(1, N_DEV):
                    _remote(blk, blk, send.at[a, k - 1], recv.at[a, k - 1], sibling).wait_send()
                for k in (1, 3, 5, 7):
                    _remote(blk, blk, send.at[a, k - 1], recv.at[a, k - 1], sibling).wait_recv()
            elif self.kind == "pair":
                for j in range(4):
                    _remote(ins[a].at[j], outs[a].at[j], send.at[a, j], recv.at[a, j], sibling).wait()
            else:
                slab = outs[a].at[0]
                pltpu.make_async_copy(slab, slab, own.at[a]).wait()
                for d in (1, 2, 3):
                    _remote(slab, slab, send.at[a, d - 1], recv.at[a, d - 1], sibling).wait()


def _side_call(side, body, *, name, grid, in_specs, out_specs, out_shape, scratch_shapes, args, semantics=None,
               aliases=None):
    in_specs, out_specs, out_shape = list(in_specs), list(out_specs), list(out_shape)
    scratch_shapes, args = list(scratch_shapes), list(args)
    n_in, n_out = len(in_specs), len(out_specs)
    kernel_body = body
    if side is not None:
        n_so = side.n * side.outs_per

        def kernel_body(*refs):
            ins, side_ins = refs[:n_in], refs[n_in:n_in + side.n]
            pos = n_in + side.n
            outs, side_outs = refs[pos:pos + n_out], refs[pos + n_out:pos + n_out + n_so]
            rest = refs[pos + n_out + n_so:]
            scratch, sems = rest[:-3], rest[-3:]
            ids = [pl.program_id(ax) for ax in range(len(grid))]
            first = functools.reduce(jnp.logical_and, [i == 0 for i in ids])
            last = functools.reduce(jnp.logical_and, [i == g - 1 for i, g in zip(ids, grid)])

            @pl.when(first)
            def _():
                side.start(side_ins, side_outs, sems)

            body(*ins, *outs, *scratch)

            @pl.when(last)
            def _():
                side.finish(side_ins, side_outs, sems)

        any_spec = pl.BlockSpec(memory_space=pl.ANY)
        in_specs += [any_spec] * side.n
        args += side.arrays
        out_specs += [any_spec] * n_so
        out_shape += side.out_shape()
        scratch_shapes += side.sems()
        semantics = ("arbitrary",) * len(grid)
    outs = pl.pallas_call(
        kernel_body,
        name=name,
        grid=grid,
        in_specs=in_specs,
        out_specs=out_specs,
        out_shape=out_shape,
        scratch_shapes=scratch_shapes,
        input_output_aliases=aliases or {},
        compiler_params=pltpu.CompilerParams(dimension_semantics=semantics or ("arbitrary",) * len(grid),
                                             vmem_limit_bytes=VMEM_BIG),
    )(*args)
    if side is not None:
        side.results = list(outs[n_out:])
    return list(outs[:n_out])


def carrier(name, side):
    _side_call(side, lambda: None, name=name, grid=(1,), in_specs=[], out_specs=[], out_shape=[], scratch_shapes=[], args=[])
    return side.results


def all_reduce_small(vec):
    def body(v_ref, o_ref, land, send_sems, recv_sems):
        x, y, c = _coords()
        me = 4 * x + 2 * y + c
        land[me] = v_ref[...]
        copies = []
        for k in range(1, N_DEV):
            peer, _ = _peer(k)
            cp = pltpu.make_async_remote_copy(
                src_ref=v_ref, dst_ref=land.at[me], send_sem=send_sems.at[k - 1], recv_sem=recv_sems.at[k - 1],
                device_id=peer, device_id_type=pl.DeviceIdType.MESH)
            cp.start()
            copies.append(cp)
        for cp in copies:
            cp.wait()
        total = land[0]
        for s in range(1, N_DEV):
            total = total + land[s]
        o_ref[...] = total

    return pl.pallas_call(
        body,
        name="all_reduce_small",
        in_specs=[pl.BlockSpec(memory_space=pltpu.VMEM)],
        out_specs=pl.BlockSpec(memory_space=pltpu.VMEM),
        out_shape=jax.ShapeDtypeStruct(vec.shape, F32),
        scratch_shapes=[pltpu.VMEM((N_DEV,) + vec.shape, F32), pltpu.SemaphoreType.DMA((N_DEV - 1,)),
                        pltpu.SemaphoreType.DMA((N_DEV - 1,))],
    )(vec)


def _pack(parts):
    flat = jnp.concatenate([p.reshape(-1).astype(F32) for p in parts])
    n = flat.shape[0]
    rows = -(-n // 1024) * 8
    return jnp.pad(flat, (0, rows * 128 - n)).reshape(rows, 128)


def _unpack(packed, shapes):
    flat, out, pos = packed.reshape(-1), [], 0
    for s in shapes:
        n = math.prod(s)
        out.append(flat[pos:pos + n].reshape(s))
        pos += n
    return out


def _row2(v):
    return v.reshape(1, -1)


def _rider(hooks, name, ctx=None):
    if name not in hooks:
        return None, lambda: None
    side, done = hooks[name](ctx)
    return side, lambda: done(side.results)


def _layer_forward(h, hb, W, consts, hooks):
    cos, sin, ret_decay = consts
    side, landed = _rider(hooks, "proj")
    proj = mm("proj", hb, W["in"], "nn", tm=4096, tn=512, tk=1024, side=side, out_dtype=PROJ_DTYPE)
    landed()
    bias_tab, first_tab = W["bias_tab"]
    side, landed = _rider(hooks, "attn_fwd")
    attn, attn_lse = attn_fwd(proj, bias_tab, first_tab, side=side)
    landed()
    rq2, rk2 = row_call("ret_pre", f_ret_pre, [Row(proj, 512, COL["rq"] // 512), Row(proj, 512, COL["rk"] // 512),
                                              Row(cos), Row(sin)], [], [(512, F32), (512, F32)], tm=512)
    side, landed = _rider(hooks, "ret_fwd")
    ret_raw, ret_st = lin_fwd("ret_fwd", Win(rq2, 128), Win(rk2, 128), Win(proj, 256, COL["rv"] // 256), Win(ret_decay, 128),
                              side=side)
    landed()
    gq2, gla_la = row_call("gla_pre", f_gla_pre, [Row(proj, 512, COL["gq"] // 512), Row(proj, GLR_BLOCK, GLR_OFF // GLR_BLOCK)],
                           [W["wlr"], W["blr"]], [(512, F32), (512, F32)], tm=512)
    side, landed = _rider(hooks, "gla_fwd")
    gla_raw, gla_st = lin_fwd("gla_fwd", Win(gq2, 128), Win(proj, 128, COL["gk"] // 128), Win(proj, 256, COL["gv"] // 256),
                              Win(gla_la, 128), side=side)
    landed()
    ret_b, gla_b = row_call("normgate", f_normgate, [Row(ret_raw), Row(proj, 1024, COL["rg"] // 1024), Row(gla_raw),
                                                    Row(proj, 1024, COL["gg"] // 1024)], [W["gnorm"]],
                            [(1024, MXU_DTYPE), (1024, MXU_DTYPE)], tm=512)
    branches = (attn, ret_b, gla_b)
    p = [mm("branch_proj", branches[n], W["branch"][n], "nn", tm=2048, tn=1024, tk=1024) for n in range(3)]
    merged, = row_call("merge", f_merge, [Row(p[0]), Row(p[1]), Row(p[2]), Row(proj, GATES_W, GATES_OFF // GATES_W)], [],
                       [(1024, MXU_DTYPE)], tm=512)
    post_norm = lambda y, x, g, b: f_deepnorm(x, y, g, b) * 2 + (y,)
    x1, x1b, mix = mm("out_proj_ln1", merged, W["out"], "nn", tm=512, tn=D_MODEL, tk=1024, extras=(h,),
                      row_params=(W["ln1_g"], W["ln1_b"]), epi=post_norm, n_out=3, out_dtypes=(F32, MXU_DTYPE, F32))
    side, landed = _rider(hooks, "mlp_up")
    u, act = mm("mlp_up", x1b, W["up"], "nn", tm=4096, tn=512, tk=1024, n_out=2, out_dtypes=(F32, MXU_DTYPE),
                epi=lambda r: (r, jnp.square(jnp.maximum(r, 0.0))), side=side)
    landed()
    x2, x2b, y = mm("mlp_down_ln2", act, W["down"], "nn", tm=1024, tn=D_MODEL, tk=1024, extras=(x1,),
                    row_params=(W["ln2_g"], W["ln2_b"]), epi=post_norm, n_out=3, out_dtypes=(F32, MXU_DTYPE, F32))
    saved = dict(h=h, hb=hb, proj=proj, attn_lse=attn_lse, rq2=rq2, rk2=rk2, ret_st=ret_st, gq2=gq2, gla_la=gla_la,
                 gla_st=gla_st, ret_raw=ret_raw, gla_raw=gla_raw, branches=branches, p=p, merged=merged, mix=mix, x1=x1, x1b=x1b,
                 u=u, act=act, y=y)
    return x2, x2b, saved


def _layer_backward(dx2, S, W, consts, hooks):
    cos, sin, ret_decay = consts
    h, proj, T = S["h"], S["proj"], S["h"].shape[0]
    add = lambda r, e: r + e
    dx1a, dy, dg2, db2 = row_call("ln2_bwd", vjp_fn(f_deepnorm, 2, 1), [Row(S["x1"]), Row(S["y"]), Row(dx2)],
                                  [W["ln2_g"], W["ln2_b"]], [(1024, F32), (1024, MXU_DTYPE)], [(1, 1024), (1, 1024)], tm=512)
    side, landed = _rider(hooks, "mlp_down_bwd")
    du = mm("mlp_down_bwd", dy, W["down"], "nt", tm=2048, tn=512, tk=1024, extras=(S["u"],),
            epi=lambda r, u: r * (2.0 * jnp.maximum(u, 0.0)), out_dtype=MXU_DTYPE, side=side)
    landed()
    g_down = mm("mlp_down_wgrad", S["act"], dy, "tn", tm=1024, tn=1024, tk=2048, out_dtype=WIRE_DTYPE)
    dx1 = mm("mlp_up_bwd", du, W["up"], "nt", tm=1024, tn=1024, tk=2048, extras=(dx1a,), epi=add)
    g_up = mm("mlp_up_wgrad", S["x1b"], du, "tn", tm=1024, tn=D_FF // N_DEV, tk=2048, out_dtype=WIRE_DTYPE,
              column_slabs=True)
    dha, dmix, dg1, db1 = row_call("ln1_bwd", vjp_fn(f_deepnorm, 2, 1), [Row(h), Row(S["mix"]), Row(dx1)],
                                   [W["ln1_g"], W["ln1_b"]], [(1024, F32), (1024, MXU_DTYPE)], [(1, 1024), (1, 1024)], tm=512)
    dmerged = mm("out_proj_bwd", dmix, W["out"], "nt", tm=1024, tn=1024, tk=1024)
    g_out = mm("out_proj_wgrad", S["merged"], dmix, "tn", tm=1024, tn=1024, tk=2048, out_dtype=WIRE_DTYPE)
    dproj = jax.ShapeDtypeStruct((T, PROJ_PAD), MXU_DTYPE)
    p = S["p"]
    dp0, dp1, dp2, dproj = row_call(
        "merge_bwd", vjp_fn(f_merge, 4, 1), [Row(p[0]), Row(p[1]), Row(p[2]), Row(proj, GATES_W, GATES_OFF // GATES_W), Row(dmerged)],
        [], [(1024, MXU_DTYPE)] * 3, tm=512, into=dproj, into_cols=[(GATES_W, GATES_OFF)])
    dps = (dp0, dp1, dp2)
    dbr = [mm("branch_proj_bwd", dps[n], W["branch"][n], "nt", tm=2048, tn=1024, tk=1024,
              out_dtype=MXU_DTYPE if n == 0 else F32) for n in range(3)]
    g_branch = [mm("branch_proj_wgrad", S["branches"][n], dps[n], "tn", tm=1024, tn=1024, tk=2048, out_dtype=WIRE_DTYPE)
                for n in range(3)]
    dret_raw, dgla_raw, dgnorm, dproj = row_call(
        "normgate_bwd", vjp_fn(lambda a, b, c, d, g: _reorder_normgate(a, b, c, d, g), 4, 2),
        [Row(S["ret_raw"]), Row(S["gla_raw"]), Row(proj, 1024, COL["rg"] // 1024), Row(proj, 1024, COL["gg"] // 1024),
         Row(dbr[1]), Row(dbr[2])], [W["gnorm"]], [(1024, F32), (1024, F32)], [(1, LIN_VAL_DIM)], tm=512,
        into=dproj, into_cols=[(1024, COL["rg"]), (1024, COL["gg"])])
    side, landed = _rider(hooks, "gla_bwd")
    dgq2, dgla_la, dproj = lin_bwd("gla_bwd", Win(S["gq2"], 128), Win(proj, 128, COL["gk"] // 128),
                                   Win(proj, 256, COL["gv"] // 256), Win(S["gla_la"], 128), S["gla_st"], dgla_raw, True,
                                   side=side, into=dproj, windows={1: COL["gk"], 2: COL["gv"]})
    landed()
    dwlr, dblr, dproj = row_call(
        "gla_pre_bwd", _gla_pre_bwd, [Row(proj, 512, COL["gq"] // 512), Row(proj, GLR_BLOCK, GLR_OFF // GLR_BLOCK),
                                      Row(dgq2), Row(dgla_la)], [W["wlr"], W["blr"]], [],
        [(GLR_BLOCK, 512), (1, 512)], tm=512, into=dproj,
        into_cols=[(512, COL["gq"]), (PROJ_PAD - GLR_OFF, GLR_OFF)])
    side, landed = _rider(hooks, "ret_bwd", dict(branch=g_branch, out=g_out, up=g_up, down=g_down))
    drq2, drk2, dproj = lin_bwd("ret_bwd", Win(S["rq2"], 128), Win(S["rk2"], 128), Win(proj, 256, COL["rv"] // 256),
                                Win(ret_decay, 128), S["ret_st"], dret_raw, False, side=side, into=dproj,
                                windows={2: COL["rv"]})
    landed()
    dproj, = row_call(
        "ret_pre_bwd", _ret_pre_bwd, [Row(proj, 512, COL["rq"] // 512), Row(proj, 512, COL["rk"] // 512), Row(cos), Row(sin),
                                      Row(drq2), Row(drk2)], [], [], tm=512, into=dproj,
        into_cols=[(512, COL["rq"]), (512, COL["rk"])])
    side, landed = _rider(hooks, "attn_bwd")
    dproj, dbias = attn_bwd(proj, *W["bias_tab"], S["branches"][0], S["attn_lse"], dbr[0], dproj, side=side)
    landed()
    d_rel_bias = bias_table_grad(dbias)
    side, landed = _rider(hooks, "proj_wgrad", dict(branch=g_branch, out=g_out, up=g_up, down=g_down))
    g_in = mm("proj_wgrad", S["hb"], dproj, "tn", tm=1024, tn=1280, tk=2048, out_dtype=WIRE_DTYPE, side=side)
    landed()
    side, landed = _rider(hooks, "proj_bwd", {"in": g_in})
    dh = mm("proj_bwd", dproj, W["in"], "nt", tm=1024, tn=1024, tk=2560, extras=(dha,), epi=add, side=side)
    landed()
    big = dict(g_in=g_in, branch=g_branch, out=g_out, up=g_up, down=g_down)
    small = dict(rel_bias=d_rel_bias, wlr=dwlr[:GLA_RANK], blr=dblr, gnorm=dgnorm, ln1_g=dg1, ln1_b=db1, ln2_g=dg2, ln2_b=db2)
    return dh, big, small


def _reorder_normgate(ret_raw, gla_raw, rg, gg, gnorm):
    return f_normgate(ret_raw, rg, gla_raw, gg, gnorm)


def _gla_pre_bwd(gq, glr, dq2, dla, wlr, blr):
    _, pull = jax.vjp(f_gla_pre, gq, glr, wlr, blr)
    dgq, dglr, dwlr, dblr = pull((dq2, dla))
    dglr = jnp.concatenate([dglr, jnp.zeros((dglr.shape[0], PROJ_PAD - GLR_OFF - GLR_BLOCK), F32)], axis=1)
    return dgq, dglr, dwlr, dblr


def _ret_pre_bwd(rq, rk, cos, sin, drq2, drk2):
    _, pull = jax.vjp(lambda a, b: f_ret_pre(a, b, cos, sin), rq, rk)
    return pull((drq2, drk2))


SHARDED = ("in", "branch", "out", "up", "down")

_FULL = {
    "branch": lambda g: [g[:, n].reshape(D_MODEL, D_MODEL) for n in range(3)],
    "out": lambda g: g.reshape(D_MODEL, D_MODEL),
    "up": lambda g: jnp.transpose(g, (1, 0, 2)).reshape(D_MODEL, D_FF),
    "down": lambda g: g.reshape(D_FF, D_MODEL),
}


def _owner_pieces(s):
    shard = PROJ_WIDTH // N_DEV
    lo, hi = s * shard, (s + 1) * shard
    pieces = []
    for a, b, padded in ((0, GLR_SRC, 0), (GLR_SRC, GATES_SRC, GLR_OFF), (GATES_SRC, PROJ_WIDTH, GATES_OFF)):
        x0, x1 = max(lo, a), min(hi, b)
        if x0 < x1:
            pieces.append((x0 - lo, x1 - x0, padded + x0 - a))
    return pieces


def _full_in(g):
    placed = sorted((p, s, c, w) for s in range(N_DEV) for c, w, p in _owner_pieces(s))
    cols = [g[s][:, c:c + w] for _, s, c, w in placed]
    return jnp.concatenate(cols + [jnp.zeros((D_MODEL, PROJ_PAD - PROJ_WIDTH), g.dtype)], axis=1)


def _full_wlr(g):
    wlr = jnp.transpose(g, (1, 0, 2)).reshape(GLA_RANK, LIN_HEADS * LIN_KEY_DIM)
    return jnp.pad(wlr, ((0, GLR_BLOCK - GLA_RANK), (0, 0)))


def _slab_in(g):
    return jnp.stack([jnp.concatenate([g[:, p:p + w] for _, w, p in _owner_pieces(s)], axis=1) for s in range(N_DEV)])


_SLABS = {
    "in": _slab_in,
    "branch": lambda gs: jnp.stack([b.reshape(N_DEV, D_MODEL // N_DEV, D_MODEL) for b in gs], axis=1),
    "out": lambda g: g.reshape(N_DEV, D_MODEL // N_DEV, D_MODEL),
    "up": lambda g: g,
    "down": lambda g: g.reshape(N_DEV, D_FF // N_DEV, D_MODEL),
}


def _pair_sums(name, slabs, from_sibling=None):
    sums = []
    for mine, sibs in zip(slabs, from_sibling or carrier(name, Side("pair", slabs))):
        width = mine.shape[-1]
        rows = math.prod(mine.shape[1:-1])
        tm = 256 if rows % 256 == 0 else 128
        nb = rows // tm
        my_rows = Row(mine.reshape(-1, width), rowmap=lambda i, nb=nb: (2 * (i // nb) + lax.axis_index("c")) * nb + i % nb)
        out, = row_call("pair_sum", lambda q, p: (p.astype(F32) + q.astype(F32),),
                        [Row(sibs.reshape(-1, width)), my_rows], [], [(width, WIRE_DTYPE)], tm=tm)
        sums.append(out.reshape(sibs.shape))
    return sums


def _sum_adam(name, parts, w, m, v, layer, tm, earlier):
    C = w.shape[-1]
    rows = [Row(parts, lead=(s,)) for s in range(parts.shape[0])] + [Row(a, lead=(layer,)) for a in (w, m, v)]
    return row_call(name, f_sum_adam, rows, [], [(C, F32)] * 4, tm=tm, stack=(layer, DEPTH), stacked_onto=earlier)


def kernel(x, ln_in_g, ln_in_b, w_in, rel_bias, gla_w_lr, gla_b_lr, gla_norm_g, w_branch, w_out, ln1_g, ln1_b, w_up, w_down, ln2_g, ln2_b, loss_target, m_ln_in_g, m_ln_in_b, m_w_in, m_rel_bias, m_gla_w_lr, m_gla_b_lr, m_gla_norm_g, m_w_branch, m_w_out, m_ln1_g, m_ln1_b, m_w_up, m_w_down, m_ln2_g, m_ln2_b, v_ln_in_g, v_ln_in_b, v_w_in, v_rel_bias, v_gla_w_lr, v_gla_b_lr, v_gla_norm_g, v_w_branch, v_w_out, v_ln1_g, v_ln1_b, v_w_up, v_w_down, v_ln2_g, v_ln2_b):
    T = x.shape[1]
    x2d = x.reshape(T, D_MODEL)
    tgt = loss_target.reshape(T, D_MODEL)
    me = 4 * lax.axis_index("x") + 2 * lax.axis_index("y") + lax.axis_index("c")

    inv = ROPE_BASE ** (-jnp.arange(0, LIN_KEY_DIM, 2, dtype=F32) / LIN_KEY_DIM)
    ang = jnp.arange(T, dtype=F32)[:, None] * inv[None, :]
    cos = jnp.concatenate([jnp.cos(ang), jnp.cos(ang)], axis=1)
    sin = jnp.concatenate([-jnp.sin(ang), jnp.sin(ang)], axis=1)
    log_gamma = jnp.log1p(-jnp.exp2(-5.0 - jnp.arange(LIN_HEADS, dtype=F32)))
    ret_decay = jnp.repeat(log_gamma, LIN_KEY_DIM).reshape(1, LIN_HEADS * LIN_KEY_DIM)
    consts = (cos, sin, ret_decay)

    layers = [dict(blr=_row2(gla_b_lr[l]), gnorm=_row2(gla_norm_g[l]), ln1_g=_row2(ln1_g[l]),
                   ln1_b=_row2(ln1_b[l]), ln2_g=_row2(ln2_g[l]), ln2_b=_row2(ln2_b[l])) for l in range(DEPTH)]
    shards = dict(branch=w_branch, out=w_out, up=w_up, down=w_down)
    shards["in"] = w_in
    wire = [{n: shards[n][l].astype(WIRE_DTYPE) for n in SHARDED} for l in range(DEPTH)]

    def gather_of(l, names):
        def landed(res):
            for n, g in zip(names, res):
                layers[l][n] = _full_in(g) if n == "in" else _FULL[n](g)
        return lambda ctx: (Side("gather", [wire[l][n] for n in names]), landed)

    first = Side("gather", [wire[0]["in"]] + [gla_w_lr[l] for l in range(DEPTH)])
    bias_tabs = bias_table(rel_bias, side=first)
    layers[0]["in"] = _full_in(first.results[0])
    for l in range(DEPTH):
        layers[l]["wlr"] = _full_wlr(first.results[1 + l])
        layers[l]["bias_tab"] = (bias_tabs, l * ATTN_HEADS)

    fwd_hooks = [{"proj": gather_of(1, ("in",)), "attn_fwd": gather_of(0, ("up", "down")),
                  "ret_fwd": gather_of(0, ("branch", "out"))},
                 {"proj": gather_of(1, ("up", "down")), "attn_fwd": gather_of(1, ("branch", "out"))}]
    h, hb = row_call("ln_in", _twice(f_ln_in), [Row(x2d)], [_row2(ln_in_g), _row2(ln_in_b)], STREAM, tm=512)
    saved = []
    for l in range(DEPTH):
        h, hb, s = _layer_forward(h, hb, layers[l], consts, fwd_hooks[l])
        saved.append(s)
    dh, loss_lanes = row_call("loss", f_loss, [Row(h), Row(tgt)], [], [(D_MODEL, F32)], [(1, 128)], tm=512)

    received = [dict() for _ in range(DEPTH)]

    def scatter_of(l, names, sums=None):
        def make(ctx):
            parts = [sums[n] for n in names] if sums is not None else _pair_sums(
                "pair_grads", [_SLABS[n](ctx[n]) for n in names])

            def landed(res):
                received[l].update(zip(names, res))
            return Side("chips", parts), landed
        return make

    small = [None] * DEPTH
    dh, big, small[1] = _layer_backward(dh, saved[1], layers[1], consts, {})
    slabs1 = [_SLABS[n](big["g_in"] if n == "in" else big[n]) for n in SHARDED]
    sums1 = {}

    def pair_exchange_1(ctx):
        def landed(res):
            sums1.update(zip(SHARDED, _pair_sums("pair_grads", slabs1, from_sibling=res)))
        return Side("pair", slabs1), landed

    rest = SHARDED[1:]
    sums0 = {}

    def pair_exchange_0(ctx):
        slabs0 = [_SLABS[n](ctx[n]) for n in rest]

        def landed(res):
            sums0.update(zip(rest, _pair_sums("pair_grads", slabs0, from_sibling=res)))
        return Side("pair", slabs0), landed

    dh, big, small[0] = _layer_backward(dh, saved[0], layers[0], consts, {
        "mlp_down_bwd": pair_exchange_1, "gla_bwd": scatter_of(1, ("in",), sums1), "ret_bwd": pair_exchange_0,
        "attn_bwd": scatter_of(1, rest, sums1), "proj_wgrad": scatter_of(0, rest, sums0), "proj_bwd": scatter_of(0, ("in",))})
    grad_x, d_in_g, d_in_b = row_call("ln_in_bwd", vjp_fn(f_ln_in, 1, 1), [Row(x2d), Row(dh)],
                                      [_row2(ln_in_g), _row2(ln_in_b)], [(D_MODEL, F32)], [(1, D_MODEL), (1, D_MODEL)], tm=256)

    small_names = ("rel_bias", "wlr", "blr", "gnorm", "ln1_g", "ln1_b", "ln2_g", "ln2_b")
    parts = [d_in_g, d_in_b] + [small[l][n] for l in range(DEPTH) for n in small_names] + [loss_lanes]
    shapes = [(D_MODEL,), (D_MODEL,)] + [s for l in range(DEPTH) for s in (
        (ATTN_HEADS, N_REL), (GLA_RANK, 512), (512,), (LIN_VAL_DIM,), (D_MODEL,), (D_MODEL,), (D_MODEL,), (D_MODEL,))] + [(128,)]
    red = _unpack(all_reduce_small(_pack(parts)), shapes)
    loss = jnp.sum(red[-1])
    g_small = {"ln_in_g": red[0], "ln_in_b": red[1]}
    for l in range(DEPTH):
        for j, n in enumerate(small_names):
            g_small[(l, n)] = red[2 + l * len(small_names) + j]
    shard = LIN_HEADS * LIN_KEY_DIM // N_DEV
    g_wlr = jnp.stack([lax.dynamic_slice_in_dim(g_small[(l, "wlr")], me * shard, shard, axis=1) for l in range(DEPTH)])
    stack = lambda n: jnp.stack([g_small[(l, n)] for l in range(DEPTH)])
    g_rep = [g_small["ln_in_g"], g_small["ln_in_b"], stack("rel_bias"), g_wlr, stack("blr"), stack("gnorm"),
             stack("ln1_g"), stack("ln1_b"), stack("ln2_g"), stack("ln2_b")]
    w_rep = [ln_in_g, ln_in_b, rel_bias, gla_w_lr, gla_b_lr, gla_norm_g, ln1_g, ln1_b, ln2_g, ln2_b]
    m_rep = [m_ln_in_g, m_ln_in_b, m_rel_bias, m_gla_w_lr, m_gla_b_lr, m_gla_norm_g, m_ln1_g, m_ln1_b, m_ln2_g, m_ln2_b]
    v_rep = [v_ln_in_g, v_ln_in_b, v_rel_bias, v_gla_w_lr, v_gla_b_lr, v_gla_norm_g, v_ln1_g, v_ln1_b, v_ln2_g, v_ln2_b]
    rep_shapes = [w.shape for w in w_rep]
    d_pack, m_pack, v_pack = row_call("adam_small", f_adam, [Row(_pack(g_rep)), Row(_pack(w_rep)), Row(_pack(m_rep)), Row(_pack(v_rep))],
                                      [], [(128, F32)] * 3, tm=256)
    d_rep, nm_rep, nv_rep = _unpack(d_pack, rep_shapes), _unpack(m_pack, rep_shapes), _unpack(v_pack, rep_shapes)

    def owned(key, w, m, v, name, tm):
        C = w.shape[-1]
        flat = lambda a: a.reshape(DEPTH, -1, C)
        outs = None
        for l in range(DEPTH):
            outs = _sum_adam(name, received[l][key].reshape(4, -1, C), flat(w), flat(m), flat(v), l, tm, outs)
        return [o.reshape(w.shape) for o in outs]

    o_in = owned("in", w_in, m_w_in, v_w_in, "adam_w_in", 256)
    o_br = owned("branch", w_branch, m_w_branch, v_w_branch, "adam_w_branch", 128)
    o_out = owned("out", w_out, m_w_out, v_w_out, "adam_w_out", 128)
    o_up = owned("up", w_up, m_w_up, v_w_up, "adam_w_up", 256)
    o_down = owned("down", w_down, m_w_down, v_w_down, "adam_w_down", 256)

    def tree(j, rep):
        return [rep[0], rep[1], o_in[j], rep[2], rep[3], rep[4], rep[5], o_br[j], o_out[j], rep[6], rep[7], o_up[j], o_down[j],
                rep[8], rep[9]]

    return (loss, grad_x.reshape(1, T, D_MODEL), *tree(0, g_rep), *tree(1, d_rep), *tree(2, nm_rep), *tree(3, nv_rep))
```

```python
import functools
import math

import jax
import jax.numpy as jnp
from jax import lax
from jax.experimental import pallas as pl
from jax.experimental.pallas import tpu as pltpu

F32 = jnp.float32
MXU_DTYPE = jnp.bfloat16
WIRE_DTYPE = jnp.bfloat16
PROJ_DTYPE = jnp.bfloat16

N_DEV = 8
D_MODEL = 1024
DEPTH = 2
CHUNK = 64
ATTN_HEADS = 8
ATTN_HEAD_DIM = 128
ATTN_LEFT_CHUNKS = 8
BAND = (ATTN_LEFT_CHUNKS + 1) * CHUNK
PAD_ROWS = ATTN_LEFT_CHUNKS * CHUNK
REL_CLIP = 2 * CHUNK
N_REL = 2 * REL_CLIP + 1
REL_MIN = REL_CLIP - (CHUNK - 1)
LIN_HEADS = 4
LIN_KEY_DIM = 128
LIN_VAL_DIM = 256
GLA_RANK = 16
GLA_GATE_NORMALIZER = 16.0
ROPE_BASE = 10000.0
D_FF = 4 * D_MODEL
LN_EPS = 1e-5
NEG_INF = -1e30
ALPHA = (2 * DEPTH) ** 0.25
ADAM_LR, ADAM_B1, ADAM_B2, ADAM_EPS, ADAM_WD, ADAM_STEP = 0.001, 0.9, 0.999, 1e-08, 0.01, 10

PROJ_WIDTH = 12304
GLR_SRC = 9216
GATES_SRC = GLR_SRC + GLA_RANK
MAIN_W = 9216
GATES_W = 3 * D_MODEL
GATES_OFF = MAIN_W
GLR_OFF = MAIN_W + GATES_W
PROJ_PAD = 12800
GLR_BLOCK = 128
COL = dict(aq=0, ak=1024, av=2048, rq=3072, rk=3584, rv=4096, rg=5120, gq=6144, gk=6656, gv=7168, gg=8192)

SEG_CHUNKS = 8
SEG = SEG_CHUNKS * CHUNK
VMEM_BIG = 56 * 1024 * 1024

_NN = (((1,), (0,)), ((), ()))
_NT = (((1,), (1,)), ((), ()))
_TN = (((0,), (0,)), ((), ()))


def _mxu(a, b, dn):
    return lax.dot_general(a.astype(MXU_DTYPE), b.astype(MXU_DTYPE), dn, preferred_element_type=F32)


@jax.custom_vjp
def dot_nn(a, b):
    return _mxu(a, b, _NN)


def _dot_nn_fwd(a, b):
    return _mxu(a, b, _NN), (a, b)


def _dot_nn_bwd(res, g):
    a, b = res
    return _mxu(g, b, _NT).astype(a.dtype), _mxu(a, g, _TN).astype(b.dtype)


dot_nn.defvjp(_dot_nn_fwd, _dot_nn_bwd)


@jax.custom_vjp
def dot_nt(a, b):
    return _mxu(a, b, _NT)


def _dot_nt_fwd(a, b):
    return _mxu(a, b, _NT), (a, b)


def _dot_nt_bwd(res, g):
    a, b = res
    return _mxu(g, b, _NN).astype(a.dtype), _mxu(g, a, _TN).astype(b.dtype)


dot_nt.defvjp(_dot_nt_fwd, _dot_nt_bwd)


@jax.custom_vjp
def dot_tn(a, b):
    return _mxu(a, b, _TN)


def _dot_tn_fwd(a, b):
    return _mxu(a, b, _TN), (a, b)


def _dot_tn_bwd(res, g):
    a, b = res
    return _mxu(b, g, _NT).astype(a.dtype), _mxu(a, g, _NN).astype(b.dtype)


dot_tn.defvjp(_dot_tn_fwd, _dot_tn_bwd)


def _tri(n, lower):
    r = lax.broadcasted_iota(jnp.int32, (n, n), 0)
    c = lax.broadcasted_iota(jnp.int32, (n, n), 1)
    return (r >= c) if lower else (r <= c)


def _tri_dot(lower, x):
    tri = _tri(x.shape[0], lower).astype(jnp.bfloat16)
    hi = x.astype(jnp.bfloat16)
    rest = x - hi.astype(F32)
    mid = rest.astype(jnp.bfloat16)
    lo = (rest - mid.astype(F32)).astype(jnp.bfloat16)
    dot = lambda part: lax.dot_general(tri, part, _NN, preferred_element_type=F32)
    return dot(hi) + dot(mid) + dot(lo)


@jax.custom_vjp
def chunk_cumsum(x):
    return _tri_dot(True, x)


def _chunk_cumsum_fwd(x):
    return chunk_cumsum(x), None


def _chunk_cumsum_bwd(_, g):
    return (_tri_dot(False, g),)


chunk_cumsum.defvjp(_chunk_cumsum_fwd, _chunk_cumsum_bwd)


@jax.custom_vjp
def swap_halves(t):
    return pltpu.roll(t, t.shape[-1] // 2, t.ndim - 1)


def _swap_fwd(t):
    return swap_halves(t), None


def _swap_bwd(_, g):
    return (swap_halves(g),)


swap_halves.defvjp(_swap_fwd, _swap_bwd)


def mm(name, a, b, dims, *, tm, tn, tk, out_dtype=F32, extras=(), row_params=(), epi=None, n_out=1, out_dtypes=None,
       side=None, column_slabs=False):
    if dims == "nn":
        (M, K), N = a.shape, b.shape[1]
    elif dims == "nt":
        (M, K), N = a.shape, b.shape[0]
    else:
        (K, M), N = a.shape, b.shape[1]
    tm, tn, tk = min(tm, M), min(tn, N), min(tk, K)
    assert M % tm == 0 and N % tn == 0 and K % tk == 0, (name, M, N, K, tm, tn, tk)
    nk = K // tk
    dn = {"nn": _NN, "nt": _NT, "tn": _TN}[dims]
    a_spec = pl.BlockSpec((tk, tm), lambda i, j, k: (k, i)) if dims == "tn" else pl.BlockSpec((tm, tk), lambda i, j, k: (i, k))
    b_spec = pl.BlockSpec((tn, tk), lambda i, j, k: (j, k)) if dims == "nt" else pl.BlockSpec((tk, tn), lambda i, j, k: (k, j))
    o_spec = pl.BlockSpec((tm, tn), lambda i, j, k: (i, j))
    row_spec = pl.BlockSpec((1, tn), lambda i, j, k: (0, j))
    n_ex = len(extras) + len(row_params)
    out_dtypes = out_dtypes or (out_dtype,) * n_out

    def finish(r, ex_refs, o_refs):
        if epi is not None:
            r = epi(r, *[e[...] for e in ex_refs])
        r = r if isinstance(r, tuple) else (r,)
        for o, v in zip(o_refs, r):
            o[...] = v.astype(o.dtype)

    def body(*refs):
        a_ref, b_ref = refs[:2]
        ex_refs = refs[2:2 + n_ex]
        o_refs = refs[2 + n_ex:2 + n_ex + n_out]
        part = _mxu(a_ref[...], b_ref[...], dn)
        if nk == 1:
            finish(part, ex_refs, o_refs)
            return
        acc = refs[-1]
        k = pl.program_id(2)

        @pl.when(k == 0)
        def _():
            acc[...] = part

        @pl.when(k > 0)
        def _():
            acc[...] += part

        @pl.when(k == nk - 1)
        def _():
            finish(acc[...], ex_refs, o_refs)

    outs = _side_call(
        side, body,
        name=name,
        grid=(M // tm, N // tn, nk),
        in_specs=[a_spec, b_spec] + [o_spec] * len(extras) + [row_spec] * len(row_params),
        out_specs=[pl.BlockSpec((None, tm, tn), lambda i, j, k: (j, i, 0))] * n_out if column_slabs else [o_spec] * n_out,
        out_shape=[jax.ShapeDtypeStruct((N // tn, M, tn) if column_slabs else (M, N), dt) for dt in out_dtypes],
        scratch_shapes=[pltpu.VMEM((tm, tn), F32)] if nk > 1 else [],
        args=[a, b, *extras, *row_params],
        semantics=("parallel", "parallel", "arbitrary"),
    )
    return outs[0] if n_out == 1 else outs


class Row:
    def __init__(self, arr, width=None, col=0, lead=(), rowmap=None):
        self.arr, self.lead, self.col = arr, tuple(lead), col
        self.width = arr.shape[-1] if width is None else width
        self.rowmap = rowmap

    def spec(self, tm):
        lead, col, rowmap = self.lead, self.col, self.rowmap or (lambda i: i)
        return pl.BlockSpec((None,) * len(lead) + (tm, self.width), lambda i: (*lead, rowmap(i), col))


def row_call(name, fn, rows, params, new_outs, red_shapes=(), *, tm, into=None, into_cols=(), stack=None, stacked_onto=None):
    R = rows[0].arr.shape[len(rows[0].lead)]
    tm = min(tm, R)
    assert R % tm == 0, (name, R, tm)
    n_row, n_par, n_new, n_into, n_red = len(rows), len(params), len(new_outs), len(into_cols), len(red_shapes)
    fresh = isinstance(into, jax.ShapeDtypeStruct)

    def body(*refs):
        row_refs = refs[:n_row]
        par_refs = refs[n_row:n_row + n_par]
        pos = n_row + n_par + (1 if (into is not None and not fresh) else 0) + (n_new if stacked_onto is not None else 0)
        new_refs = refs[pos:pos + n_new]
        red_refs = refs[pos + n_new:pos + n_new + n_red]
        pos2 = pos + n_new + n_red
        into_ref = refs[pos2] if into is not None else None
        stage = refs[pos2 + 1:pos2 + 1 + n_into] if into is not None else ()
        i = pl.program_id(0)
        widen = lambda t: t.astype(F32) if jnp.issubdtype(t.dtype, jnp.floating) else t
        vals = fn(*[widen(r[...]) for r in row_refs], *[p[...] for p in par_refs])
        vals = vals if isinstance(vals, (tuple, list)) else (vals,)
        for o, v in zip(new_refs, vals[:n_new]):
            o[...] = v.astype(o.dtype)

        def window_copy(j, step):
            width, off = into_cols[j]
            rows_at = pl.ds(pl.multiple_of(step * tm, tm), tm)
            return pltpu.make_async_copy(stage[j], into_ref.at[rows_at, pl.ds(off, width)], refs[pos2 + 1 + n_into].at[j])

        for j, v in enumerate(vals[n_new:n_new + n_into]):
            @pl.when(i > 0)
            def _(j=j):
                window_copy(j, i - 1).wait()

            stage[j][...] = v.astype(stage[j].dtype)
            window_copy(j, i).start()

            @pl.when(i == R // tm - 1)
            def _(j=j):
                window_copy(j, i).wait()
        for o, v in zip(red_refs, vals[n_new + n_into:]):
            @pl.when(i == 0)
            def _(o=o, v=v):
                o[...] = v.astype(o.dtype)

            @pl.when(i > 0)
            def _(o=o, v=v):
                o[...] += v.astype(o.dtype)

    in_specs = [r.spec(tm) for r in rows] + [pl.BlockSpec(p.shape, lambda i, nd=p.ndim: (0,) * nd) for p in params]
    args = [r.arr for r in rows] + list(params)
    if stack is None:
        out_specs = [pl.BlockSpec((tm, w), lambda i: (i, 0)) for w, _ in new_outs]
        out_shape = [jax.ShapeDtypeStruct((R, w), dt) for w, dt in new_outs]
    else:
        index, depth = stack
        out_specs = [pl.BlockSpec((None, tm, w), lambda i: (index, i, 0)) for w, _ in new_outs]
        out_shape = [jax.ShapeDtypeStruct((depth, R, w), dt) for w, dt in new_outs]
    out_specs += [pl.BlockSpec(s, lambda i, nd=len(s): (0,) * nd) for s in red_shapes]
    out_shape += [jax.ShapeDtypeStruct(s, F32) for s in red_shapes]
    aliases, scratch = {}, []
    if into is not None:
        out_specs.append(pl.BlockSpec(memory_space=pl.ANY))
        out_shape.append(jax.ShapeDtypeStruct(into.shape, into.dtype))
        if not fresh:
            in_specs.append(pl.BlockSpec(memory_space=pl.ANY))
            args.append(into)
            aliases = {len(args) - 1: len(out_shape) - 1}
        scratch = [pltpu.VMEM((tm, w), into.dtype) for w, _ in into_cols] + [pltpu.SemaphoreType.DMA((n_into,))]
    if stacked_onto is not None:
        for j, prev in enumerate(stacked_onto):
            in_specs.append(pl.BlockSpec(memory_space=pl.ANY))
            args.append(prev)
            aliases[len(args) - 1] = j
    outs = pl.pallas_call(
        body,
        name=name,
        grid=(R // tm,),
        in_specs=in_specs,
        out_specs=out_specs,
        out_shape=out_shape,
        scratch_shapes=scratch,
        input_output_aliases=aliases,
        compiler_params=pltpu.CompilerParams(dimension_semantics=("arbitrary",), vmem_limit_bytes=VMEM_BIG),
    )(*args)
    return list(outs)


def vjp_fn(fn, n_row, n_out, passthrough=0):
    def bwd(*args):
        rows = [a.astype(F32) for a in args[:n_row]]
        cts = args[n_row:n_row + n_out]
        thru = args[n_row + n_out:n_row + n_out + passthrough]
        pars = args[n_row + n_out + passthrough:]
        outs, pull = jax.vjp(lambda *xs: tuple(fn(*xs)), *rows, *pars)
        grads = pull(tuple(c.astype(o.dtype) for c, o in zip(cts, outs)))
        return tuple(grads[:n_row]) + tuple(thru) + tuple(grads[n_row:])

    return bwd


def _ln(x, g, b):
    mu = jnp.mean(x, axis=-1, keepdims=True)
    xc = x - mu
    var = jnp.mean(xc * xc, axis=-1, keepdims=True)
    return xc * lax.rsqrt(var + LN_EPS) * g + b


def f_ln_in(x, g, b):
    return (_ln(x, g, b),)


def f_deepnorm(x, y, g, b):
    return (_ln(ALPHA * x + y, g, b),)


def _twice(fn):
    return lambda *args: fn(*args) * 2


STREAM = [(D_MODEL, F32), (D_MODEL, MXU_DTYPE)]


def _rope(t, cos, sin):
    parts = []
    for h in range(LIN_HEADS):
        th = t[:, h * LIN_KEY_DIM:(h + 1) * LIN_KEY_DIM]
        parts.append(th * cos + swap_halves(th) * sin)
    return jnp.concatenate(parts, axis=1)


def f_ret_pre(rq, rk, cos, sin):
    return _rope(rq, cos, sin), _rope(rk, cos, sin) * LIN_KEY_DIM ** -0.5


def _log_sigmoid(x):
    return jnp.minimum(x, 0.0) - jnp.log(1.0 + jnp.exp(-jnp.abs(x)))


def f_gla_pre(gq, glr, wlr, blr):
    gate_pre = dot_nn(glr, wlr) + blr
    return gq * LIN_KEY_DIM ** -0.5, _log_sigmoid(gate_pre) / GLA_GATE_NORMALIZER


def f_normgate(ret_raw, rg, gla_raw, gg, gnorm):
    rets, glas = [], []
    for h in range(LIN_HEADS):
        sl = slice(h * LIN_VAL_DIM, (h + 1) * LIN_VAL_DIM)
        t = ret_raw[:, sl]
        mu = jnp.mean(t, axis=-1, keepdims=True)
        tc = t - mu
        rets.append(tc * lax.rsqrt(jnp.mean(tc * tc, axis=-1, keepdims=True) + LN_EPS))
        u = gla_raw[:, sl]
        glas.append(u * lax.rsqrt(jnp.mean(u * u, axis=-1, keepdims=True) + LN_EPS) * gnorm)
    ret = jax.nn.silu(rg) * jnp.concatenate(rets, axis=1)
    gla = jax.nn.silu(gg) * jnp.concatenate(glas, axis=1)
    return ret, gla


def f_merge(p0, p1, p2, gl):
    out = jax.nn.sigmoid(gl[:, :D_MODEL]) * p0
    out = out + jax.nn.sigmoid(gl[:, D_MODEL:2 * D_MODEL]) * p1
    return (out + jax.nn.sigmoid(gl[:, 2 * D_MODEL:]) * p2,)


def f_loss(y, tgt):
    diff = y - tgt
    sq = jnp.sum(diff * diff, axis=0, keepdims=True)
    lanes = sq[:, :128]
    for j in range(1, D_MODEL // 128):
        lanes = lanes + sq[:, j * 128:(j + 1) * 128]
    return diff * (1.0 / D_MODEL), lanes * (0.5 / D_MODEL)


def _adam(g, w, m, v):
    m = ADAM_B1 * m + (1.0 - ADAM_B1) * g
    v = ADAM_B2 * v + (1.0 - ADAM_B2) * jnp.square(g)
    m_hat = m / (1.0 - ADAM_B1 ** ADAM_STEP)
    v_hat = v / (1.0 - ADAM_B2 ** ADAM_STEP)
    delta = -ADAM_LR * (m_hat / (jnp.sqrt(v_hat) + ADAM_EPS) + ADAM_WD * w)
    return delta, m, v


def f_sum_adam(*args):
    g = args[0].astype(F32)
    for p in args[1:-3]:
        g = g + p.astype(F32)
    return (g,) + _adam(g, *args[-3:])


def f_adam(g, w, m, v):
    return _adam(g, w, m, v)


GROUP = 4
GROUP_ROWS = GROUP * CHUNK
GROUP_KEYS = (GROUP + ATTN_LEFT_CHUNKS) * CHUNK
DIAG_ROWS = REL_CLIP + CHUNK
GRAD_ROWS = 200


def _blocks_by_delta():
    blocks = {}
    for qi in range(GROUP):
        for J in range(GROUP_KEYS // 128):
            blocks.setdefault(qi - 2 * J, []).append((qi, J))
    return blocks


def bias_table(rel_bias, side=None):
    n_tab = rel_bias.shape[0] * ATTN_HEADS

    def body(rb_ref, o_ref):
        h = pl.program_id(0)
        rm = lax.broadcasted_iota(jnp.int32, (CHUNK, 128), 0)
        cm = lax.broadcasted_iota(jnp.int32, (CHUNK, 128), 1)
        ahead = jnp.where(cm >= CHUNK, 1, 0)
        for delta, where in _blocks_by_delta().items():
            base = 64 * delta + PAD_ROWS
            rel = jnp.minimum(base + rm - cm, REL_CLIP) + REL_CLIP
            in_band = (ahead - delta >= 0) & (ahead - delta <= ATTN_LEFT_CHUNKS)
            lo = max(REL_MIN, min(base - 127, REL_CLIP) + REL_CLIP)
            hi = min(base + CHUNK - 1, REL_CLIP) + REL_CLIP

            def step(r, acc, rel=rel):
                return jnp.where(rel == r, rb_ref[h * N_REL + r], acc)

            blk = jnp.zeros((CHUNK, 128), F32) if hi < lo else lax.fori_loop(lo, hi + 1, step, jnp.zeros((CHUNK, 128), F32))
            blk = jnp.where(in_band, blk, NEG_INF)
            for qi, J in where:
                o_ref[pl.ds(qi * CHUNK, CHUNK), pl.ds(J * 128, 128)] = blk

    return _side_call(
        side, body,
        name="bias_table",
        grid=(n_tab,),
        in_specs=[pl.BlockSpec(memory_space=pltpu.SMEM)],
        out_specs=[pl.BlockSpec((None, GROUP_ROWS, GROUP_KEYS), lambda h: (h, 0, 0))],
        out_shape=[jax.ShapeDtypeStruct((n_tab, GROUP_ROWS, GROUP_KEYS), F32)],
        scratch_shapes=[],
        args=[rel_bias.reshape(n_tab * N_REL)],
    )[0]


def bias_table_grad(dbias):
    blocks = _blocks_by_delta()
    below_clip = range(-(CHUNK - 1), REL_CLIP)
    deltas = [d for d in sorted(blocks) if any(-127 <= pos - 64 * d - PAD_ROWS <= CHUNK - 1 for pos in below_clip)]

    def body(d_ref, o_ref, diag):
        sums = []
        for delta in deltas:
            acc = None
            for qi, J in blocks[delta]:
                blk = d_ref[pl.ds(qi * CHUNK, CHUNK), pl.ds(J * 128, 128)]
                acc = blk if acc is None else acc + blk
            sums.append(acc)
        off = lax.broadcasted_iota(jnp.int32, (CHUNK, 128), 0) - lax.broadcasted_iota(jnp.int32, (CHUNK, 128), 1)
        diag[...] = jnp.zeros_like(diag)

        def step(t, carry):
            pos = t - (CHUNK - 1)
            picked = jnp.zeros((CHUNK, 128), F32)
            for delta, s in zip(deltas, sums):
                picked = picked + jnp.where(off == pos - 64 * delta - PAD_ROWS, s, 0.0)
            diag[pl.ds(t, 1), :] = jnp.sum(picked, axis=0, keepdims=True)
            return carry

        lax.fori_loop(0, DIAG_ROWS, step, 0, unroll=8)
        d = d_ref[...]
        total = jnp.sum(jnp.sum(d, axis=1, keepdims=True), axis=0, keepdims=True)
        o_ref[pl.ds(0, DIAG_ROWS), :] = jnp.broadcast_to(jnp.sum(diag[...], axis=1, keepdims=True), (DIAG_ROWS, 128))
        o_ref[pl.ds(DIAG_ROWS, GRAD_ROWS - DIAG_ROWS), :] = jnp.broadcast_to(total, (GRAD_ROWS - DIAG_ROWS, 128))

    out = pl.pallas_call(
        body,
        name="bias_table_grad",
        grid=(ATTN_HEADS,),
        in_specs=[pl.BlockSpec((None, GROUP_ROWS, GROUP_KEYS), lambda h: (h, 0, 0))],
        out_specs=pl.BlockSpec((None, GRAD_ROWS, 128), lambda h: (h, 0, 0)),
        out_shape=jax.ShapeDtypeStruct((ATTN_HEADS, GRAD_ROWS, 128), F32),
        scratch_shapes=[pltpu.VMEM((DIAG_ROWS, 128), F32)],
        compiler_params=pltpu.CompilerParams(dimension_semantics=("arbitrary",)),
    )(dbias)
    below = out[:, :DIAG_ROWS - 1, 0]
    clipped = out[:, DIAG_ROWS, 0] - jnp.sum(below, axis=1)
    return jnp.concatenate([jnp.zeros((ATTN_HEADS, REL_MIN), F32), below, clipped[:, None]], axis=1)


def _attn_stage(q_ref, k_ref, v_ref, qs, kp, vp, T):
    qs[...] = (q_ref[...] * ATTN_HEAD_DIM ** -0.5).astype(qs.dtype)
    for dst, src in ((kp, k_ref), (vp, v_ref)):
        dst[pl.ds(0, PAD_ROWS), :] = jnp.zeros((PAD_ROWS, ATTN_HEAD_DIM), dst.dtype)
        dst[pl.ds(PAD_ROWS, T), :] = src[...].astype(dst.dtype)


STRIP = 256


def _strip_scores(g, t, first_groups, r0, qc, kp, b_ref):
    keys = pl.ds(r0 + t * STRIP if first_groups else pl.multiple_of(r0 + t * STRIP, STRIP), STRIP)
    s = _mxu(qc, kp[keys, :], _NT) + b_ref[:, t * STRIP:(t + 1) * STRIP]
    if first_groups:
        k_chunk = lax.shift_right_logical(lax.broadcasted_iota(jnp.int32, (GROUP_ROWS, STRIP), 1), 6)
        s = jnp.where(g * GROUP + t * (STRIP // CHUNK) + k_chunk >= ATTN_LEFT_CHUNKS, s, NEG_INF)
    return s, keys


def _for_groups(T, step):
    n_first = min(ATTN_LEFT_CHUNKS // GROUP, T // GROUP_ROWS)
    for g in range(n_first):
        step(g, True)
    n_loop = T // GROUP_ROWS - n_first
    unroll = next(u for u in (7, 4, 2, 1) if n_loop % u == 0)
    lax.fori_loop(n_first, T // GROUP_ROWS, lambda g, c: (step(g, False), c)[1], 0, unroll=unroll)


def _head_spec(T, base):
    return pl.BlockSpec((T, ATTN_HEAD_DIM), lambda h: (0, base + h))


def _attn_scratch(T):
    return [pltpu.VMEM((T, ATTN_HEAD_DIM), MXU_DTYPE)] + [pltpu.VMEM((T + PAD_ROWS, ATTN_HEAD_DIM), MXU_DTYPE)] * 2


def attn_fwd(proj, bias_tab, first_tab, side=None):
    T = proj.shape[0]

    def body(q_ref, k_ref, v_ref, b_ref, o_ref, lse_ref, qs, kp, vp):
        _attn_stage(q_ref, k_ref, v_ref, qs, kp, vp, T)

        def step(g, first):
            r0 = g * GROUP_ROWS if first else pl.multiple_of(g * GROUP_ROWS, GROUP_ROWS)
            rows = pl.ds(r0, GROUP_ROWS)
            qc = qs[rows, :]
            m = jnp.full((GROUP_ROWS, 1), NEG_INF, F32)
            l = jnp.zeros((GROUP_ROWS, 1), F32)
            acc = jnp.zeros((GROUP_ROWS, ATTN_HEAD_DIM), F32)
            for t in range(GROUP_KEYS // STRIP):
                s, keys = _strip_scores(g, t, first, r0, qc, kp, b_ref)
                m_new = jnp.maximum(m, jnp.max(s, axis=-1, keepdims=True))
                keep = jnp.exp(m - m_new)
                e = jnp.exp(s - m_new)
                l = l * keep + jnp.sum(e, axis=-1, keepdims=True)
                acc = acc * keep + _mxu(e, vp[keys, :], _NN)
                m = m_new
            o_ref[rows, :] = (acc * (1.0 / l)).astype(o_ref.dtype)
            lse_ref[rows, :] = jnp.broadcast_to(m + jnp.log(l), (GROUP_ROWS, ATTN_HEAD_DIM))

        _for_groups(T, step)

    out, lse = _side_call(
        side, body,
        name="attn_fwd",
        grid=(ATTN_HEADS,),
        in_specs=[_head_spec(T, COL["aq"] // 128), _head_spec(T, COL["ak"] // 128), _head_spec(T, COL["av"] // 128),
                  pl.BlockSpec((None, GROUP_ROWS, GROUP_KEYS), lambda h: (first_tab + h, 0, 0))],
        out_specs=[pl.BlockSpec((T, ATTN_HEAD_DIM), lambda h: (0, h))] * 2,
        out_shape=[jax.ShapeDtypeStruct((T, ATTN_HEADS * ATTN_HEAD_DIM), MXU_DTYPE),
                   jax.ShapeDtypeStruct((T, ATTN_HEADS * ATTN_HEAD_DIM), F32)],
        scratch_shapes=_attn_scratch(T),
        args=[proj, proj, proj, bias_tab],
    )
    return out, lse


def attn_bwd(proj, bias_tab, first_tab, out, lse, dout, dproj, side=None):
    T = proj.shape[0]

    def body(q_ref, k_ref, v_ref, b_ref, do_ref, o_ref, lse_ref, dproj_in, dbias_ref, dproj_ref, qs, kp, vp, dkp, dvp, dq, stage):
        del dproj_in
        h = pl.program_id(0)
        _attn_stage(q_ref, k_ref, v_ref, qs, kp, vp, T)
        dkp[...] = jnp.zeros_like(dkp)
        dvp[...] = jnp.zeros_like(dvp)
        dbias_ref[...] = jnp.zeros_like(dbias_ref)

        def step(g, first):
            r0 = g * GROUP_ROWS if first else pl.multiple_of(g * GROUP_ROWS, GROUP_ROWS)
            rows = pl.ds(r0, GROUP_ROWS)
            qc, do = qs[rows, :], do_ref[rows, :]
            lse = jnp.concatenate([lse_ref[rows, :]] * (STRIP // ATTN_HEAD_DIM), axis=1)
            row_term = jnp.sum(do.astype(F32) * o_ref[rows, :].astype(F32), axis=-1, keepdims=True)
            dq_acc = jnp.zeros((GROUP_ROWS, ATTN_HEAD_DIM), F32)
            for t in range(GROUP_KEYS // STRIP):
                s, keys = _strip_scores(g, t, first, r0, qc, kp, b_ref)
                cols = slice(t * STRIP, (t + 1) * STRIP)
                kb, vb = kp[keys, :], vp[keys, :]
                p = jnp.exp(s - lse)
                dvp[keys, :] += _mxu(p, do, _TN)
                ds = p * (_mxu(do, vb, _NT) - row_term)
                dq_acc = dq_acc + _mxu(ds, kb, _NN)
                dkp[keys, :] += _mxu(ds, qc, _TN)
                dbias_ref[:, cols] += ds
            dq[rows, :] = dq_acc * ATTN_HEAD_DIM ** -0.5

        _for_groups(T, step)
        col = pl.multiple_of(h * ATTN_HEAD_DIM, ATTN_HEAD_DIM)
        for grad, first_row, base in ((dq, 0, "aq"), (dkp, PAD_ROWS, "ak"), (dvp, PAD_ROWS, "av")):
            stage[...] = grad[pl.ds(first_row, T), :].astype(stage.dtype)
            pltpu.sync_copy(stage, dproj_ref.at[:, pl.ds(COL[base] + col, ATTN_HEAD_DIM)])

    dbias, dproj = _side_call(
        side, body,
        name="attn_bwd",
        grid=(ATTN_HEADS,),
        in_specs=[_head_spec(T, COL["aq"] // 128), _head_spec(T, COL["ak"] // 128), _head_spec(T, COL["av"] // 128),
                  pl.BlockSpec((None, GROUP_ROWS, GROUP_KEYS), lambda h: (first_tab + h, 0, 0))]
        + [pl.BlockSpec((T, ATTN_HEAD_DIM), lambda h: (0, h))] * 3 + [pl.BlockSpec(memory_space=pl.ANY)],
        out_specs=[pl.BlockSpec((None, GROUP_ROWS, GROUP_KEYS), lambda h: (h, 0, 0)), pl.BlockSpec(memory_space=pl.ANY)],
        out_shape=[jax.ShapeDtypeStruct((ATTN_HEADS, GROUP_ROWS, GROUP_KEYS), F32),
                   jax.ShapeDtypeStruct(dproj.shape, dproj.dtype)],
        scratch_shapes=_attn_scratch(T) + [pltpu.VMEM((T + PAD_ROWS, ATTN_HEAD_DIM), F32)] * 2
        + [pltpu.VMEM((T, ATTN_HEAD_DIM), F32), pltpu.VMEM((T, ATTN_HEAD_DIM), dproj.dtype)],
        args=[proj, proj, proj, bias_tab, dout, out, lse, dproj],
        aliases={7: 1},
    )
    return dproj, dbias


def _lin_segment(q, k, v, la, st):
    lower = _tri(CHUNK, True)
    steps = (lax.broadcasted_iota(jnp.int32, (CHUNK, 1), 0) + 1).astype(F32)
    outs = []
    for i in range(SEG_CHUNKS):
        sl = slice(i * CHUNK, (i + 1) * CHUNK)
        qi, ki, vi = q[sl], k[sl], v[sl]
        if la.shape[0] == 1:
            cum, last = la * steps, la * float(CHUNK)
        else:
            cum, last = chunk_cumsum(la[sl]), jnp.sum(la[sl], axis=0, keepdims=True)
        e_pos, e_neg = jnp.exp(cum), jnp.exp(-cum)
        q_fwd = qi * e_pos
        s_fwd = dot_nt(q_fwd, ki * e_neg)
        s_bwd = dot_nt(qi * e_neg, ki * e_pos)
        o = dot_nn(jnp.where(lower, s_fwd, s_bwd), vi) + dot_nt(q_fwd, st)
        st = jnp.exp(last) * st + dot_tn(vi, ki * jnp.exp(last - cum))
        outs.append(o)
    return jnp.concatenate(outs, axis=0), st


class Win:
    def __init__(self, arr, width, base=0):
        self.arr, self.width, self.base = arr, width, base


HEADS_PER_STEP = 2


def _lin_spec(w, seg_of):
    assert w.base % HEADS_PER_STEP == 0
    first, width = w.base // HEADS_PER_STEP, HEADS_PER_STEP * w.width
    if w.arr.shape[0] == 1:
        return pl.BlockSpec((1, width), lambda p, s: (0, first + p))
    return pl.BlockSpec((SEG, width), lambda p, s: (seg_of(s), first + p))


def _head_cols(ref, j, width):
    return ref[:, j * width:(j + 1) * width]


def _decay_rows(la_ref, j):
    return _head_cols(la_ref, j, LIN_KEY_DIM)


_STATE_BLOCK = (HEADS_PER_STEP, None, LIN_VAL_DIM, LIN_KEY_DIM)


def lin_fwd(name, q, k, v, la, side=None):
    T = q.arr.shape[0]
    nseg = T // SEG

    def body(q_ref, k_ref, v_ref, la_ref, o_ref, st_ref, st):
        @pl.when(pl.program_id(1) == 0)
        def _():
            st[...] = jnp.zeros_like(st)

        for j in range(HEADS_PER_STEP):
            st0 = st[j]
            st_ref[j] = st0
            o, st1 = _lin_segment(_head_cols(q_ref, j, LIN_KEY_DIM), _head_cols(k_ref, j, LIN_KEY_DIM),
                                  _head_cols(v_ref, j, LIN_VAL_DIM), _decay_rows(la_ref, j), st0)
            o_ref[:, j * LIN_VAL_DIM:(j + 1) * LIN_VAL_DIM] = o
            st[j] = st1

    fwd = lambda s: s
    out, states = _side_call(
        side, body,
        name=name,
        grid=(LIN_HEADS // HEADS_PER_STEP, nseg),
        in_specs=[_lin_spec(w, fwd) for w in (q, k, v, la)],
        out_specs=[pl.BlockSpec((SEG, HEADS_PER_STEP * LIN_VAL_DIM), lambda p, s: (s, p)),
                   pl.BlockSpec(_STATE_BLOCK, lambda p, s: (p, s, 0, 0))],
        out_shape=[jax.ShapeDtypeStruct((T, LIN_HEADS * LIN_VAL_DIM), F32),
                   jax.ShapeDtypeStruct((LIN_HEADS, nseg, LIN_VAL_DIM, LIN_KEY_DIM), F32)],
        scratch_shapes=[pltpu.VMEM((HEADS_PER_STEP, LIN_VAL_DIM, LIN_KEY_DIM), F32)],
        args=[q.arr, k.arr, v.arr, la.arr],
    )
    return out, states


def lin_bwd(name, q, k, v, la, states, dout, with_decay_grad, side=None, into=None, windows=None):
    T = q.arr.shape[0]
    nseg = T // SEG
    n_out = 4 if with_decay_grad else 3

    widths = [LIN_KEY_DIM, LIN_KEY_DIM, LIN_VAL_DIM] + ([LIN_KEY_DIM] if with_decay_grad else [])
    windows = dict(windows or {})
    fresh = [i for i in range(n_out) if i not in windows]
    placed = sorted(windows)
    n_pairs = LIN_HEADS // HEADS_PER_STEP
    rev = lambda s: nseg - 1 - s

    def body(q_ref, k_ref, v_ref, la_ref, st_ref, do_ref, *rest):
        rest = rest[1:] if placed else rest
        outs = dict(zip(fresh, rest[:len(fresh)]))
        rest = rest[len(fresh):]
        into_ref, rest = (rest[0], rest[1:]) if placed else (None, rest)
        dst, stages = rest[0], dict(zip(placed, rest[1:1 + len(placed)]))
        p, s = pl.program_id(0), pl.program_id(1)

        @pl.when(s == 0)
        def _():
            dst[...] = jnp.zeros_like(dst)

        def window_copy(n, i):
            width = HEADS_PER_STEP * widths[i]
            rows = pl.ds(pl.multiple_of(rev(s) * SEG, SEG), SEG)
            cols = pl.ds(pl.multiple_of(windows[i] + p * width, width), width)
            return pltpu.make_async_copy(stages[i], into_ref.at[rows, cols], rest[1 + len(placed)].at[n])

        for n, i in enumerate(placed):
            @pl.when((p > 0) | (s > 0))
            def _(n=n, i=i):
                window_copy(n, i).wait()

        for j in range(HEADS_PER_STEP):
            _, pull = jax.vjp(_lin_segment, _head_cols(q_ref, j, LIN_KEY_DIM).astype(F32),
                              _head_cols(k_ref, j, LIN_KEY_DIM).astype(F32), _head_cols(v_ref, j, LIN_VAL_DIM).astype(F32),
                              _decay_rows(la_ref, j), st_ref[j])
            grads = pull((_head_cols(do_ref, j, LIN_VAL_DIM), dst[j]))
            for i in range(n_out):
                dest = stages[i] if i in windows else outs[i]
                dest[:, j * widths[i]:(j + 1) * widths[i]] = grads[i].astype(dest.dtype)
            dst[j] = grads[4]

        for n, i in enumerate(placed):
            window_copy(n, i).start()

            @pl.when((p == n_pairs - 1) & (s == nseg - 1))
            def _(n=n, i=i):
                window_copy(n, i).wait()

    key_spec = pl.BlockSpec((SEG, HEADS_PER_STEP * LIN_KEY_DIM), lambda p, s: (rev(s), p))
    val_spec = pl.BlockSpec((SEG, HEADS_PER_STEP * LIN_VAL_DIM), lambda p, s: (rev(s), p))
    all_specs = [key_spec, key_spec, val_spec] + ([key_spec] if with_decay_grad else [])
    any_spec = pl.BlockSpec(memory_space=pl.ANY)
    outs = _side_call(
        side, body,
        name=name,
        grid=(n_pairs, nseg),
        in_specs=[_lin_spec(w, rev) for w in (q, k, v, la)] + [
            pl.BlockSpec(_STATE_BLOCK, lambda p, s: (p, rev(s), 0, 0)), val_spec] + ([any_spec] if placed else []),
        out_specs=[all_specs[i] for i in fresh] + ([any_spec] if placed else []),
        out_shape=[jax.ShapeDtypeStruct((T, LIN_HEADS * widths[i]), F32) for i in fresh]
        + ([jax.ShapeDtypeStruct(into.shape, into.dtype)] if placed else []),
        scratch_shapes=[pltpu.VMEM((HEADS_PER_STEP, LIN_VAL_DIM, LIN_KEY_DIM), F32)]
        + [pltpu.VMEM((SEG, HEADS_PER_STEP * widths[i]), into.dtype) for i in placed]
        + ([pltpu.SemaphoreType.DMA((len(placed),))] if placed else []),
        args=[q.arr, k.arr, v.arr, la.arr, states, dout] + ([into] if placed else []),
        aliases={6: len(fresh)} if placed else None,
    )
    return outs


def _coords():
    return lax.axis_index("x"), lax.axis_index("y"), lax.axis_index("c")


def _peer(k):
    x, y, c = _coords()
    px = 1 - x if k & 4 else x
    py = 1 - y if k & 2 else y
    pc = 1 - c if k & 1 else c
    return (px, py, pc), 4 * px + 2 * py + pc


def _remote(src, dst, send_sem, recv_sem, device):
    return pltpu.make_async_remote_copy(src_ref=src, dst_ref=dst, send_sem=send_sem, recv_sem=recv_sem,
                                        device_id=device, device_id_type=pl.DeviceIdType.MESH)


class Side:
    def __init__(self, kind, arrays):
        self.kind, self.arrays, self.n = kind, list(arrays), len(arrays)
        self.outs_per = 1
        self.slots = {"gather": N_DEV - 1, "pair": 4, "chips": 3}[kind]

    def out_shape(self):
        if self.kind == "gather":
            return [jax.ShapeDtypeStruct((N_DEV,) + a.shape, a.dtype) for a in self.arrays]
        return [jax.ShapeDtypeStruct((4,) + a.shape[1:], a.dtype) for a in self.arrays]

    def sems(self):
        return [pltpu.SemaphoreType.DMA((self.n, self.slots)), pltpu.SemaphoreType.DMA((self.n, self.slots)),
                pltpu.SemaphoreType.DMA((self.n,))]

    def start(self, ins, outs, sems):
        send, recv, own = sems
        x, y, c = _coords()
        me = 4 * x + 2 * y + c
        for a in range(self.n):
            if self.kind == "gather":
                pltpu.make_async_copy(ins[a], outs[a].at[me], own.at[a]).start()
                for k in (1, 2, 4, 6):
                    peer, _ = _peer(k)
                    _remote(ins[a], outs[a].at[me], send.at[a, k - 1], recv.at[a, k - 1], peer).start()
            elif self.kind == "pair":
                for j in range(4):
                    _remote(ins[a].at[2 * j + 1 - c], outs[a].at[j], send.at[a, j], recv.at[a, j], (x, y, 1 - c)).start()
            else:
                chip = 2 * x + y
                pltpu.make_async_copy(ins[a].at[chip], outs[a].at[chip], own.at[a]).start()
                for d in (1, 2, 3):
                    tx, ty = (1 - x if d & 2 else x), (1 - y if d & 1 else y)
                    _remote(ins[a].at[2 * tx + ty], outs[a].at[chip], send.at[a, d - 1], recv.at[a, d - 1], (tx, ty, c)).start()

    def finish(self, ins, outs, sems):
        send, recv, own = sems
        x, y, c = _coords()
        me = 4 * x + 2 * y + c
        sibling = (x, y, 1 - c)
        for a in range(self.n):
            if self.kind == "gather":
                blk = outs[a].at[me]
                for k in (2, 4, 6):
                    _, pidx = _peer(k)
                    _remote(blk, blk, send.at[a, k - 1], recv.at[a, k - 1], sibling).wait_recv()
                    _remote(outs[a].at[pidx], outs[a].at[pidx], send.at[a, k], recv.at[a, k], sibling).start()
                pltpu.make_async_copy(ins[a], blk, own.at[a]).wait()
                for k in range(1, N_DEV):
                    _remote(blk, blk, send.at[a, k - 1], recv.at[a, k - 1], sibling).wait_send()
                for k in (1, 3, 5, 7):
                    _remote(blk, blk, send.at[a, k - 1], recv.at[a, k - 1], sibling).wait_recv()
            elif self.kind == "pair":
                for j in range(4):
                    _remote(ins[a].at[j], outs[a].at[j], send.at[a, j], recv.at[a, j], sibling).wait()
            else:
                slab = outs[a].at[0]
                pltpu.make_async_copy(slab, slab, own.at[a]).wait()
                for d in (1, 2, 3):
                    _remote(slab, slab, send.at[a, d - 1], recv.at[a, d - 1], sibling).wait()


def _side_call(side, body, *, name, grid, in_specs, out_specs, out_shape, scratch_shapes, args, semantics=None,
               aliases=None):
    in_specs, out_specs, out_shape = list(in_specs), list(out_specs), list(out_shape)
    scratch_shapes, args = list(scratch_shapes), list(args)
    n_in, n_out = len(in_specs), len(out_specs)
    kernel_body = body
    if side is not None:
        n_so = side.n * side.outs_per

        def kernel_body(*refs):
            ins, side_ins = refs[:n_in], refs[n_in:n_in + side.n]
            pos = n_in + side.n
            outs, side_outs = refs[pos:pos + n_out], refs[pos + n_out:pos + n_out + n_so]
            rest = refs[pos + n_out + n_so:]
            scratch, sems = rest[:-3], rest[-3:]
            ids = [pl.program_id(ax) for ax in range(len(grid))]
            first = functools.reduce(jnp.logical_and, [i == 0 for i in ids])
            last = functools.reduce(jnp.logical_and, [i == g - 1 for i, g in zip(ids, grid)])

            @pl.when(first)
            def _():
                side.start(side_ins, side_outs, sems)

            body(*ins, *outs, *scratch)

            @pl.when(last)
            def _():
                side.finish(side_ins, side_outs, sems)

        any_spec = pl.BlockSpec(memory_space=pl.ANY)
        in_specs += [any_spec] * side.n
        args += side.arrays
        out_specs += [any_spec] * n_so
        out_shape += side.out_shape()
        scratch_shapes += side.sems()
        semantics = ("arbitrary",) * len(grid)
    outs = pl.pallas_call(
        kernel_body,
        name=name,
        grid=grid,
        in_specs=in_specs,
        out_specs=out_specs,
        out_shape=out_shape,
        scratch_shapes=scratch_shapes,
        input_output_aliases=aliases or {},
        compiler_params=pltpu.CompilerParams(dimension_semantics=semantics or ("arbitrary",) * len(grid),
                                             vmem_limit_bytes=VMEM_BIG),
    )(*args)
    if side is not None:
        side.results = list(outs[n_out:])
    return list(outs[:n_out])


def carrier(name, side):
    _side_call(side, lambda: None, name=name, grid=(1,), in_specs=[], out_specs=[], out_shape=[], scratch_shapes=[], args=[])
    return side.results


def all_reduce_small(vec):
    def body(v_ref, o_ref, land, send_sems, recv_sems):
        x, y, c = _coords()
        me = 4 * x + 2 * y + c
        land[me] = v_ref[...]
        copies = []
        for k in range(1, N_DEV):
            peer, _ = _peer(k)
            cp = pltpu.make_async_remote_copy(
                src_ref=v_ref, dst_ref=land.at[me], send_sem=send_sems.at[k - 1], recv_sem=recv_sems.at[k - 1],
                device_id=peer, device_id_type=pl.DeviceIdType.MESH)
            cp.start()
            copies.append(cp)
        for cp in copies:
            cp.wait()
        total = land[0]
        for s in range(1, N_DEV):
            total = total + land[s]
        o_ref[...] = total

    return pl.pallas_call(
        body,
        name="all_reduce_small",
        in_specs=[pl.BlockSpec(memory_space=pltpu.VMEM)],
        out_specs=pl.BlockSpec(memory_space=pltpu.VMEM),
        out_shape=jax.ShapeDtypeStruct(vec.shape, F32),
        scratch_shapes=[pltpu.VMEM((N_DEV,) + vec.shape, F32), pltpu.SemaphoreType.DMA((N_DEV - 1,)),
                        pltpu.SemaphoreType.DMA((N_DEV - 1,))],
    )(vec)


def _pack(parts):
    flat = jnp.concatenate([p.reshape(-1).astype(F32) for p in parts])
    n = flat.shape[0]
    rows = -(-n // 1024) * 8
    return jnp.pad(flat, (0, rows * 128 - n)).reshape(rows, 128)


def _unpack(packed, shapes):
    flat, out, pos = packed.reshape(-1), [], 0
    for s in shapes:
        n = math.prod(s)
        out.append(flat[pos:pos + n].reshape(s))
        pos += n
    return out


def _row2(v):
    return v.reshape(1, -1)


def _rider(hooks, name, ctx=None):
    if name not in hooks:
        return None, lambda: None
    side, done = hooks[name](ctx)
    return side, lambda: done(side.results)


def _layer_forward(h, hb, W, consts, hooks):
    cos, sin, ret_decay = consts
    side, landed = _rider(hooks, "proj")
    proj = mm("proj", hb, W["in"], "nn", tm=4096, tn=512, tk=1024, side=side, out_dtype=PROJ_DTYPE)
    landed()
    bias_tab, first_tab = W["bias_tab"]
    side, landed = _rider(hooks, "attn_fwd")
    attn, attn_lse = attn_fwd(proj, bias_tab, first_tab, side=side)
    landed()
    rq2, rk2 = row_call("ret_pre", f_ret_pre, [Row(proj, 512, COL["rq"] // 512), Row(proj, 512, COL["rk"] // 512),
                                              Row(cos), Row(sin)], [], [(512, F32), (512, F32)], tm=512)
    side, landed = _rider(hooks, "ret_fwd")
    ret_raw, ret_st = lin_fwd("ret_fwd", Win(rq2, 128), Win(rk2, 128), Win(proj, 256, COL["rv"] // 256), Win(ret_decay, 128),
                              side=side)
    landed()
    gq2, gla_la = row_call("gla_pre", f_gla_pre, [Row(proj, 512, COL["gq"] // 512), Row(proj, GLR_BLOCK, GLR_OFF // GLR_BLOCK)],
                           [W["wlr"], W["blr"]], [(512, F32), (512, F32)], tm=512)
    side, landed = _rider(hooks, "gla_fwd")
    gla_raw, gla_st = lin_fwd("gla_fwd", Win(gq2, 128), Win(proj, 128, COL["gk"] // 128), Win(proj, 256, COL["gv"] // 256),
                              Win(gla_la, 128), side=side)
    landed()
    ret_b, gla_b = row_call("normgate", f_normgate, [Row(ret_raw), Row(proj, 1024, COL["rg"] // 1024), Row(gla_raw),
                                                    Row(proj, 1024, COL["gg"] // 1024)], [W["gnorm"]],
                            [(1024, MXU_DTYPE), (1024, MXU_DTYPE)], tm=512)
    branches = (attn, ret_b, gla_b)
    p = [mm("branch_proj", branches[n], W["branch"][n], "nn", tm=2048, tn=1024, tk=1024) for n in range(3)]
    merged, = row_call("merge", f_merge, [Row(p[0]), Row(p[1]), Row(p[2]), Row(proj, GATES_W, GATES_OFF // GATES_W)], [],
                       [(1024, MXU_DTYPE)], tm=512)
    post_norm = lambda y, x, g, b: f_deepnorm(x, y, g, b) * 2 + (y,)
    x1, x1b, mix = mm("out_proj_ln1", merged, W["out"], "nn", tm=512, tn=D_MODEL, tk=1024, extras=(h,),
                      row_params=(W["ln1_g"], W["ln1_b"]), epi=post_norm, n_out=3, out_dtypes=(F32, MXU_DTYPE, F32))
    side, landed = _rider(hooks, "mlp_up")
    u, act = mm("mlp_up", x1b, W["up"], "nn", tm=4096, tn=512, tk=1024, n_out=2, out_dtypes=(F32, MXU_DTYPE),
                epi=lambda r: (r, jnp.square(jnp.maximum(r, 0.0))), side=side)
    landed()
    x2, x2b, y = mm("mlp_down_ln2", act, W["down"], "nn", tm=1024, tn=D_MODEL, tk=1024, extras=(x1,),
                    row_params=(W["ln2_g"], W["ln2_b"]), epi=post_norm, n_out=3, out_dtypes=(F32, MXU_DTYPE, F32))
    saved = dict(h=h, hb=hb, proj=proj, attn_lse=attn_lse, rq2=rq2, rk2=rk2, ret_st=ret_st, gq2=gq2, gla_la=gla_la,
                 gla_st=gla_st, ret_raw=ret_raw, gla_raw=gla_raw, branches=branches, p=p, merged=merged, mix=mix, x1=x1, x1b=x1b,
                 u=u, act=act, y=y)
    return x2, x2b, saved


def _layer_backward(dx2, S, W, consts, hooks):
    cos, sin, ret_decay = consts
    h, proj, T = S["h"], S["proj"], S["h"].shape[0]
    add = lambda r, e: r + e
    dx1a, dy, dg2, db2 = row_call("ln2_bwd", vjp_fn(f_deepnorm, 2, 1), [Row(S["x1"]), Row(S["y"]), Row(dx2)],
                                  [W["ln2_g"], W["ln2_b"]], [(1024, F32), (1024, MXU_DTYPE)], [(1, 1024), (1, 1024)], tm=512)
    side, landed = _rider(hooks, "mlp_down_bwd")
    du = mm("mlp_down_bwd", dy, W["down"], "nt", tm=2048, tn=512, tk=1024, extras=(S["u"],),
            epi=lambda r, u: r * (2.0 * jnp.maximum(u, 0.0)), out_dtype=MXU_DTYPE, side=side)
    landed()
    g_down = mm("mlp_down_wgrad", S["act"], dy, "tn", tm=1024, tn=1024, tk=2048, out_dtype=WIRE_DTYPE)
    dx1 = mm("mlp_up_bwd", du, W["up"], "nt", tm=1024, tn=1024, tk=2048, extras=(dx1a,), epi=add)
    g_up = mm("mlp_up_wgrad", S["x1b"], du, "tn", tm=1024, tn=D_FF // N_DEV, tk=2048, out_dtype=WIRE_DTYPE,
              column_slabs=True)
    dha, dmix, dg1, db1 = row_call("ln1_bwd", vjp_fn(f_deepnorm, 2, 1), [Row(h), Row(S["mix"]), Row(dx1)],
                                   [W["ln1_g"], W["ln1_b"]], [(1024, F32), (1024, MXU_DTYPE)], [(1, 1024), (1, 1024)], tm=512)
    dmerged = mm("out_proj_bwd", dmix, W["out"], "nt", tm=1024, tn=1024, tk=1024)
    g_out = mm("out_proj_wgrad", S["merged"], dmix, "tn", tm=1024, tn=1024, tk=2048, out_dtype=WIRE_DTYPE)
    dproj = jax.ShapeDtypeStruct((T, PROJ_PAD), MXU_DTYPE)
    p = S["p"]
    dp0, dp1, dp2, dproj = row_call(
        "merge_bwd", vjp_fn(f_merge, 4, 1), [Row(p[0]), Row(p[1]), Row(p[2]), Row(proj, GATES_W, GATES_OFF // GATES_W), Row(dmerged)],
        [], [(1024, MXU_DTYPE)] * 3, tm=512, into=dproj, into_cols=[(GATES_W, GATES_OFF)])
    dps = (dp0, dp1, dp2)
    dbr = [mm("branch_proj_bwd", dps[n], W["branch"][n], "nt", tm=2048, tn=1024, tk=1024,
              out_dtype=MXU_DTYPE if n == 0 else F32) for n in range(3)]
    g_branch = [mm("branch_proj_wgrad", S["branches"][n], dps[n], "tn", tm=1024, tn=1024, tk=2048, out_dtype=WIRE_DTYPE)
                for n in range(3)]
    dret_raw, dgla_raw, dgnorm, dproj = row_call(
        "normgate_bwd", vjp_fn(lambda a, b, c, d, g: _reorder_normgate(a, b, c, d, g), 4, 2),
        [Row(S["ret_raw"]), Row(S["gla_raw"]), Row(proj, 1024, COL["rg"] // 1024), Row(proj, 1024, COL["gg"] // 1024),
         Row(dbr[1]), Row(dbr[2])], [W["gnorm"]], [(1024, F32), (1024, F32)], [(1, LIN_VAL_DIM)], tm=512,
        into=dproj, into_cols=[(1024, COL["rg"]), (1024, COL["gg"])])
    side, landed = _rider(hooks, "gla_bwd")
    dgq2, dgla_la, dproj = lin_bwd("gla_bwd", Win(S["gq2"], 128), Win(proj, 128, COL["gk"] // 128),
                                   Win(proj, 256, COL["gv"] // 256), Win(S["gla_la"], 128), S["gla_st"], dgla_raw, True,
                                   side=side, into=dproj, windows={1: COL["gk"], 2: COL["gv"]})
    landed()
    dwlr, dblr, dproj = row_call(
        "gla_pre_bwd", _gla_pre_bwd, [Row(proj, 512, COL["gq"] // 512), Row(proj, GLR_BLOCK, GLR_OFF // GLR_BLOCK),
                                      Row(dgq2), Row(dgla_la)], [W["wlr"], W["blr"]], [],
        [(GLR_BLOCK, 512), (1, 512)], tm=512, into=dproj,
        into_cols=[(512, COL["gq"]), (PROJ_PAD - GLR_OFF, GLR_OFF)])
    side, landed = _rider(hooks, "ret_bwd", dict(branch=g_branch, out=g_out, up=g_up, down=g_down))
    drq2, drk2, dproj = lin_bwd("ret_bwd", Win(S["rq2"], 128), Win(S["rk2"], 128), Win(proj, 256, COL["rv"] // 256),
                                Win(ret_decay, 128), S["ret_st"], dret_raw, False, side=side, into=dproj,
                                windows={2: COL["rv"]})
    landed()
    dproj, = row_call(
        "ret_pre_bwd", _ret_pre_bwd, [Row(proj, 512, COL["rq"] // 512), Row(proj, 512, COL["rk"] // 512), Row(cos), Row(sin),
                                      Row(drq2), Row(drk2)], [], [], tm=512, into=dproj,
        into_cols=[(512, COL["rq"]), (512, COL["rk"])])
    side, landed = _rider(hooks, "attn_bwd")
    dproj, dbias = attn_bwd(proj, *W["bias_tab"], S["branches"][0], S["attn_lse"], dbr[0], dproj, side=side)
    landed()
    d_rel_bias = bias_table_grad(dbias)
    side, landed = _rider(hooks, "proj_wgrad", dict(branch=g_branch, out=g_out, up=g_up, down=g_down))
    g_in = mm("proj_wgrad", S["hb"], dproj, "tn", tm=1024, tn=1280, tk=2048, out_dtype=WIRE_DTYPE, side=side)
    landed()
    side, landed = _rider(hooks, "proj_bwd", {"in": g_in})
    dh = mm("proj_bwd", dproj, W["in"], "nt", tm=1024, tn=1024, tk=2560, extras=(dha,), epi=add, side=side)
    landed()
    big = dict(g_in=g_in, branch=g_branch, out=g_out, up=g_up, down=g_down)
    small = dict(rel_bias=d_rel_bias, wlr=dwlr[:GLA_RANK], blr=dblr, gnorm=dgnorm, ln1_g=dg1, ln1_b=db1, ln2_g=dg2, ln2_b=db2)
    return dh, big, small


def _reorder_normgate(ret_raw, gla_raw, rg, gg, gnorm):
    return f_normgate(ret_raw, rg, gla_raw, gg, gnorm)


def _gla_pre_bwd(gq, glr, dq2, dla, wlr, blr):
    _, pull = jax.vjp(f_gla_pre, gq, glr, wlr, blr)
    dgq, dglr, dwlr, dblr = pull((dq2, dla))
    dglr = jnp.concatenate([dglr, jnp.zeros((dglr.shape[0], PROJ_PAD - GLR_OFF - GLR_BLOCK), F32)], axis=1)
    return dgq, dglr, dwlr, dblr


def _ret_pre_bwd(rq, rk, cos, sin, drq2, drk2):
    _, pull = jax.vjp(lambda a, b: f_ret_pre(a, b, cos, sin), rq, rk)
    return pull((drq2, drk2))


SHARDED = ("in", "branch", "out", "up", "down")

_FULL = {
    "branch": lambda g: [g[:, n].reshape(D_MODEL, D_MODEL) for n in range(3)],
    "out": lambda g: g.reshape(D_MODEL, D_MODEL),
    "up": lambda g: jnp.transpose(g, (1, 0, 2)).reshape(D_MODEL, D_FF),
    "down": lambda g: g.reshape(D_FF, D_MODEL),
}


def _owner_pieces(s):
    shard = PROJ_WIDTH // N_DEV
    lo, hi = s * shard, (s + 1) * shard
    pieces = []
    for a, b, padded in ((0, GLR_SRC, 0), (GLR_SRC, GATES_SRC, GLR_OFF), (GATES_SRC, PROJ_WIDTH, GATES_OFF)):
        x0, x1 = max(lo, a), min(hi, b)
        if x0 < x1:
            pieces.append((x0 - lo, x1 - x0, padded + x0 - a))
    return pieces


def _full_in(g):
    placed = sorted((p, s, c, w) for s in range(N_DEV) for c, w, p in _owner_pieces(s))
    cols = [g[s][:, c:c + w] for _, s, c, w in placed]
    return jnp.concatenate(cols + [jnp.zeros((D_MODEL, PROJ_PAD - PROJ_WIDTH), g.dtype)], axis=1)


def _full_wlr(g):
    wlr = jnp.transpose(g, (1, 0, 2)).reshape(GLA_RANK, LIN_HEADS * LIN_KEY_DIM)
    return jnp.pad(wlr, ((0, GLR_BLOCK - GLA_RANK), (0, 0)))


def _slab_in(g):
    return jnp.stack([jnp.concatenate([g[:, p:p + w] for _, w, p in _owner_pieces(s)], axis=1) for s in range(N_DEV)])


_SLABS = {
    "in": _slab_in,
    "branch": lambda gs: jnp.stack([b.reshape(N_DEV, D_MODEL // N_DEV, D_MODEL) for b in gs], axis=1),
    "out": lambda g: g.reshape(N_DEV, D_MODEL // N_DEV, D_MODEL),
    "up": lambda g: g,
    "down": lambda g: g.reshape(N_DEV, D_FF // N_DEV, D_MODEL),
}


def _pair_sums(name, slabs, from_sibling=None):
    sums = []
    for mine, sibs in zip(slabs, from_sibling or carrier(name, Side("pair", slabs))):
        width = mine.shape[-1]
        rows = math.prod(mine.shape[1:-1])
        tm = 256 if rows % 256 == 0 else 128
        nb = rows // tm
        my_rows = Row(mine.reshape(-1, width), rowmap=lambda i, nb=nb: (2 * (i // nb) + lax.axis_index("c")) * nb + i % nb)
        out, = row_call("pair_sum", lambda q, p: (p.astype(F32) + q.astype(F32),),
                        [Row(sibs.reshape(-1, width)), my_rows], [], [(width, WIRE_DTYPE)], tm=tm)
        sums.append(out.reshape(sibs.shape))
    return sums


def _sum_adam(name, parts, w, m, v, layer, tm, earlier):
    C = w.shape[-1]
    rows = [Row(parts, lead=(s,)) for s in range(parts.shape[0])] + [Row(a, lead=(layer,)) for a in (w, m, v)]
    return row_call(name, f_sum_adam, rows, [], [(C, F32)] * 4, tm=tm, stack=(layer, DEPTH), stacked_onto=earlier)


def kernel(x, ln_in_g, ln_in_b, w_in, rel_bias, gla_w_lr, gla_b_lr, gla_norm_g, w_branch, w_out, ln1_g, ln1_b, w_up, w_down, ln2_g, ln2_b, loss_target, m_ln_in_g, m_ln_in_b, m_w_in, m_rel_bias, m_gla_w_lr, m_gla_b_lr, m_gla_norm_g, m_w_branch, m_w_out, m_ln1_g, m_ln1_b, m_w_up, m_w_down, m_ln2_g, m_ln2_b, v_ln_in_g, v_ln_in_b, v_w_in, v_rel_bias, v_gla_w_lr, v_gla_b_lr, v_gla_norm_g, v_w_branch, v_w_out, v_ln1_g, v_ln1_b, v_w_up, v_w_down, v_ln2_g, v_ln2_b):
    T = x.shape[1]
    x2d = x.reshape(T, D_MODEL)
    tgt = loss_target.reshape(T, D_MODEL)
    me = 4 * lax.axis_index("x") + 2 * lax.axis_index("y") + lax.axis_index("c")

    inv = ROPE_BASE ** (-jnp.arange(0, LIN_KEY_DIM, 2, dtype=F32) / LIN_KEY_DIM)
    ang = jnp.arange(T, dtype=F32)[:, None] * inv[None, :]
    cos = jnp.concatenate([jnp.cos(ang), jnp.cos(ang)], axis=1)
    sin = jnp.concatenate([-jnp.sin(ang), jnp.sin(ang)], axis=1)
    log_gamma = jnp.log1p(-jnp.exp2(-5.0 - jnp.arange(LIN_HEADS, dtype=F32)))
    ret_decay = jnp.repeat(log_gamma, LIN_KEY_DIM).reshape(1, LIN_HEADS * LIN_KEY_DIM)
    consts = (cos, sin, ret_decay)

    layers = [dict(blr=_row2(gla_b_lr[l]), gnorm=_row2(gla_norm_g[l]), ln1_g=_row2(ln1_g[l]),
                   ln1_b=_row2(ln1_b[l]), ln2_g=_row2(ln2_g[l]), ln2_b=_row2(ln2_b[l])) for l in range(DEPTH)]
    shards = dict(branch=w_branch, out=w_out, up=w_up, down=w_down)
    shards["in"] = w_in
    wire = [{n: shards[n][l].astype(WIRE_DTYPE) for n in SHARDED} for l in range(DEPTH)]

    def gather_of(l, names):
        def landed(res):
            for n, g in zip(names, res):
                layers[l][n] = _full_in(g) if n == "in" else _FULL[n](g)
        return lambda ctx: (Side("gather", [wire[l][n] for n in names]), landed)

    first = Side("gather", [wire[0]["in"]] + [gla_w_lr[l] for l in range(DEPTH)])
    bias_tabs = bias_table(rel_bias, side=first)
    layers[0]["in"] = _full_in(first.results[0])
    for l in range(DEPTH):
        layers[l]["wlr"] = _full_wlr(first.results[1 + l])
        layers[l]["bias_tab"] = (bias_tabs, l * ATTN_HEADS)

    fwd_hooks = [{"proj": gather_of(1, ("in",)), "attn_fwd": gather_of(0, ("up", "down")),
                  "ret_fwd": gather_of(0, ("branch", "out"))},
                 {"proj": gather_of(1, ("up", "down")), "attn_fwd": gather_of(1, ("branch", "out"))}]
    h, hb = row_call("ln_in", _twice(f_ln_in), [Row(x2d)], [_row2(ln_in_g), _row2(ln_in_b)], STREAM, tm=512)
    saved = []
    for l in range(DEPTH):
        h, hb, s = _layer_forward(h, hb, layers[l], consts, fwd_hooks[l])
        saved.append(s)
    dh, loss_lanes = row_call("loss", f_loss, [Row(h), Row(tgt)], [], [(D_MODEL, F32)], [(1, 128)], tm=512)

    received = [dict() for _ in range(DEPTH)]

    def scatter_of(l, names, sums=None):
        def make(ctx):
            parts = [sums[n] for n in names] if sums is not None else _pair_sums(
                "pair_grads", [_SLABS[n](ctx[n]) for n in names])

            def landed(res):
                received[l].update(zip(names, res))
            return Side("chips", parts), landed
        return make

    small = [None] * DEPTH
    dh, big, small[1] = _layer_backward(dh, saved[1], layers[1], consts, {})
    slabs1 = [_SLABS[n](big["g_in"] if n == "in" else big[n]) for n in SHARDED]
    sums1 = {}

    def pair_exchange_1(ctx):
        def landed(res):
            sums1.update(zip(SHARDED, _pair_sums("pair_grads", slabs1, from_sibling=res)))
        return Side("pair", slabs1), landed

    rest = SHARDED[1:]
    sums0 = {}

    def pair_exchange_0(ctx):
        slabs0 = [_SLABS[n](ctx[n]) for n in rest]

        def landed(res):
            sums0.update(zip(rest, _pair_sums("pair_grads", slabs0, from_sibling=res)))
        return Side("pair", slabs0), landed

    dh, big, small[0] = _layer_backward(dh, saved[0], layers[0], consts, {
        "mlp_down_bwd": pair_exchange_1, "gla_bwd": scatter_of(1, ("in",), sums1), "ret_bwd": pair_exchange_0,
        "attn_bwd": scatter_of(1, rest, sums1), "proj_wgrad": scatter_of(0, rest, sums0), "proj_bwd": scatter_of(0, ("in",))})
    grad_x, d_in_g, d_in_b = row_call("ln_in_bwd", vjp_fn(f_ln_in, 1, 1), [Row(x2d), Row(dh)],
                                      [_row2(ln_in_g), _row2(ln_in_b)], [(D_MODEL, F32)], [(1, D_MODEL), (1, D_MODEL)], tm=256)

    small_names = ("rel_bias", "wlr", "blr", "gnorm", "ln1_g", "ln1_b", "ln2_g", "ln2_b")
    parts = [d_in_g, d_in_b] + [small[l][n] for l in range(DEPTH) for n in small_names] + [loss_lanes]
    shapes = [(D_MODEL,), (D_MODEL,)] + [s for l in range(DEPTH) for s in (
        (ATTN_HEADS, N_REL), (GLA_RANK, 512), (512,), (LIN_VAL_DIM,), (D_MODEL,), (D_MODEL,), (D_MODEL,), (D_MODEL,))] + [(128,)]
    red = _unpack(all_reduce_small(_pack(parts)), shapes)
    loss = jnp.sum(red[-1])
    g_small = {"ln_in_g": red[0], "ln_in_b": red[1]}
    for l in range(DEPTH):
        for j, n in enumerate(small_names):
            g_small[(l, n)] = red[2 + l * len(small_names) + j]
    shard = LIN_HEADS * LIN_KEY_DIM // N_DEV
    g_wlr = jnp.stack([lax.dynamic_slice_in_dim(g_small[(l, "wlr")], me * shard, shard, axis=1) for l in range(DEPTH)])
    stack = lambda n: jnp.stack([g_small[(l, n)] for l in range(DEPTH)])
    g_rep = [g_small["ln_in_g"], g_small["ln_in_b"], stack("rel_bias"), g_wlr, stack("blr"), stack("gnorm"),
             stack("ln1_g"), stack("ln1_b"), stack("ln2_g"), stack("ln2_b")]
    w_rep = [ln_in_g, ln_in_b, rel_bias, gla_w_lr, gla_b_lr, gla_norm_g, ln1_g, ln1_b, ln2_g, ln2_b]
    m_rep = [m_ln_in_g, m_ln_in_b, m_rel_bias, m_gla_w_lr, m_gla_b_lr, m_gla_norm_g, m_ln1_g, m_ln1_b, m_ln2_g, m_ln2_b]
    v_rep = [v_ln_in_g, v_ln_in_b, v_rel_bias, v_gla_w_lr, v_gla_b_lr, v_gla_norm_g, v_ln1_g, v_ln1_b, v_ln2_g, v_ln2_b]
    rep_shapes = [w.shape for w in w_rep]
    d_pack, m_pack, v_pack = row_call("adam_small", f_adam, [Row(_pack(g_rep)), Row(_pack(w_rep)), Row(_pack(m_rep)), Row(_pack(v_rep))],
                                      [], [(128, F32)] * 3, tm=256)
    d_rep, nm_rep, nv_rep = _unpack(d_pack, rep_shapes), _unpack(m_pack, rep_shapes), _unpack(v_pack, rep_shapes)

    def owned(key, w, m, v, name, tm):
        C = w.shape[-1]
        flat = lambda a: a.reshape(DEPTH, -1, C)
        outs = None
        for l in range(DEPTH):
            outs = _sum_adam(name, received[l][key].reshape(4, -1, C), flat(w), flat(m), flat(v), l, tm, outs)
        return [o.reshape(w.shape) for o in outs]

    o_in = owned("in", w_in, m_w_in, v_w_in, "adam_w_in", 256)
    o_br = owned("branch", w_branch, m_w_branch, v_w_branch, "adam_w_branch", 128)
    o_out = owned("out", w_out, m_w_out, v_w_out, "adam_w_out", 128)
    o_up = owned("up", w_up, m_w_up, v_w_up, "adam_w_up", 256)
    o_down = owned("down", w_down, m_w_down, v_w_down, "adam_w_down", 256)

    def tree(j, rep):
        return [rep[0], rep[1], o_in[j], rep[2], rep[3], rep[4], rep[5], o_br[j], o_out[j], rep[6], rep[7], o_up[j], o_down[j],
                rep[8], rep[9]]

    return (loss, grad_x.reshape(1, T, D_MODEL), *tree(0, g_rep), *tree(1, d_rep), *tree(2, nm_rep), *tree(3, nv_rep))
```

```python
import functools
import math

import jax
import jax.numpy as jnp
from jax import lax
from jax.experimental import pallas as pl
from jax.experimental.pallas import tpu as pltpu

F32 = jnp.float32
MXU_DTYPE = jnp.bfloat16
WIRE_DTYPE = jnp.bfloat16
PROJ_DTYPE = jnp.bfloat16

N_DEV = 8
D_MODEL = 1024
DEPTH = 2
CHUNK = 64
ATTN_HEADS = 8
ATTN_HEAD_DIM = 128
ATTN_LEFT_CHUNKS = 8
BAND = (ATTN_LEFT_CHUNKS + 1) * CHUNK
PAD_ROWS = ATTN_LEFT_CHUNKS * CHUNK
REL_CLIP = 2 * CHUNK
N_REL = 2 * REL_CLIP + 1
REL_MIN = REL_CLIP - (CHUNK - 1)
LIN_HEADS = 4
LIN_KEY_DIM = 128
LIN_VAL_DIM = 256
GLA_RANK = 16
GLA_GATE_NORMALIZER = 16.0
ROPE_BASE = 10000.0
D_FF = 4 * D_MODEL
LN_EPS = 1e-5
NEG_INF = -1e30
ALPHA = (2 * DEPTH) ** 0.25
ADAM_LR, ADAM_B1, ADAM_B2, ADAM_EPS, ADAM_WD, ADAM_STEP = 0.001, 0.9, 0.999, 1e-08, 0.01, 10

PROJ_WIDTH = 12304
GLR_SRC = 9216
GATES_SRC = GLR_SRC + GLA_RANK
MAIN_W = 9216
GATES_W = 3 * D_MODEL
GATES_OFF = MAIN_W
GLR_OFF = MAIN_W + GATES_W
PROJ_PAD = 12800
GLR_BLOCK = 128
COL = dict(aq=0, ak=1024, av=2048, rq=3072, rk=3584, rv=4096, rg=5120, gq=6144, gk=6656, gv=7168, gg=8192)

SEG_CHUNKS = 8
SEG = SEG_CHUNKS * CHUNK
VMEM_BIG = 56 * 1024 * 1024

_NN = (((1,), (0,)), ((), ()))
_NT = (((1,), (1,)), ((), ()))
_TN = (((0,), (0,)), ((), ()))


def _mxu(a, b, dn):
    return lax.dot_general(a.astype(MXU_DTYPE), b.astype(MXU_DTYPE), dn, preferred_element_type=F32)


@jax.custom_vjp
def dot_nn(a, b):
    return _mxu(a, b, _NN)


def _dot_nn_fwd(a, b):
    return _mxu(a, b, _NN), (a, b)


def _dot_nn_bwd(res, g):
    a, b = res
    return _mxu(g, b, _NT).astype(a.dtype), _mxu(a, g, _TN).astype(b.dtype)


dot_nn.defvjp(_dot_nn_fwd, _dot_nn_bwd)


@jax.custom_vjp
def dot_nt(a, b):
    return _mxu(a, b, _NT)


def _dot_nt_fwd(a, b):
    return _mxu(a, b, _NT), (a, b)


def _dot_nt_bwd(res, g):
    a, b = res
    return _mxu(g, b, _NN).astype(a.dtype), _mxu(g, a, _TN).astype(b.dtype)


dot_nt.defvjp(_dot_nt_fwd, _dot_nt_bwd)


@jax.custom_vjp
def dot_tn(a, b):
    return _mxu(a, b, _TN)


def _dot_tn_fwd(a, b):
    return _mxu(a, b, _TN), (a, b)


def _dot_tn_bwd(res, g):
    a, b = res
    return _mxu(b, g, _NT).astype(a.dtype), _mxu(a, g, _NN).astype(b.dtype)


dot_tn.defvjp(_dot_tn_fwd, _dot_tn_bwd)


def _tri(n, lower):
    r = lax.broadcasted_iota(jnp.int32, (n, n), 0)
    c = lax.broadcasted_iota(jnp.int32, (n, n), 1)
    return (r >= c) if lower else (r <= c)


def _tri_dot(lower, x):
    tri = _tri(x.shape[0], lower).astype(jnp.bfloat16)
    hi = x.astype(jnp.bfloat16)
    rest = x - hi.astype(F32)
    mid = rest.astype(jnp.bfloat16)
    lo = (rest - mid.astype(F32)).astype(jnp.bfloat16)
    dot = lambda part: lax.dot_general(tri, part, _NN, preferred_element_type=F32)
    return dot(hi) + dot(mid) + dot(lo)


@jax.custom_vjp
def chunk_cumsum(x):
    return _tri_dot(True, x)


def _chunk_cumsum_fwd(x):
    return chunk_cumsum(x), None


def _chunk_cumsum_bwd(_, g):
    return (_tri_dot(False, g),)


chunk_cumsum.defvjp(_chunk_cumsum_fwd, _chunk_cumsum_bwd)


@jax.custom_vjp
def swap_halves(t):
    return pltpu.roll(t, t.shape[-1] // 2, t.ndim - 1)


def _swap_fwd(t):
    return swap_halves(t), None


def _swap_bwd(_, g):
    return (swap_halves(g),)


swap_halves.defvjp(_swap_fwd, _swap_bwd)


def mm(name, a, b, dims, *, tm, tn, tk, out_dtype=F32, extras=(), row_params=(), epi=None, n_out=1, out_dtypes=None,
       side=None, column_slabs=False):
    if dims == "nn":
        (M, K), N = a.shape, b.shape[1]
    elif dims == "nt":
        (M, K), N = a.shape, b.shape[0]
    else:
        (K, M), N = a.shape, b.shape[1]
    tm, tn, tk = min(tm, M), min(tn, N), min(tk, K)
    assert M % tm == 0 and N % tn == 0 and K % tk == 0, (name, M, N, K, tm, tn, tk)
    nk = K // tk
    dn = {"nn": _NN, "nt": _NT, "tn": _TN}[dims]
    a_spec = pl.BlockSpec((tk, tm), lambda i, j, k: (k, i)) if dims == "tn" else pl.BlockSpec((tm, tk), lambda i, j, k: (i, k))
    b_spec = pl.BlockSpec((tn, tk), lambda i, j, k: (j, k)) if dims == "nt" else pl.BlockSpec((tk, tn), lambda i, j, k: (k, j))
    o_spec = pl.BlockSpec((tm, tn), lambda i, j, k: (i, j))
    row_spec = pl.BlockSpec((1, tn), lambda i, j, k: (0, j))
    n_ex = len(extras) + len(row_params)
    out_dtypes = out_dtypes or (out_dtype,) * n_out

    def finish(r, ex_refs, o_refs):
        if epi is not None:
            r = epi(r, *[e[...] for e in ex_refs])
        r = r if isinstance(r, tuple) else (r,)
        for o, v in zip(o_refs, r):
            o[...] = v.astype(o.dtype)

    def body(*refs):
        a_ref, b_ref = refs[:2]
        ex_refs = refs[2:2 + n_ex]
        o_refs = refs[2 + n_ex:2 + n_ex + n_out]
        part = _mxu(a_ref[...], b_ref[...], dn)
        if nk == 1:
            finish(part, ex_refs, o_refs)
            return
        acc = refs[-1]
        k = pl.program_id(2)

        @pl.when(k == 0)
        def _():
            acc[...] = part

        @pl.when(k > 0)
        def _():
            acc[...] += part

        @pl.when(k == nk - 1)
        def _():
            finish(acc[...], ex_refs, o_refs)

    outs = _side_call(
        side, body,
        name=name,
        grid=(M // tm, N // tn, nk),
        in_specs=[a_spec, b_spec] + [o_spec] * len(extras) + [row_spec] * len(row_params),
        out_specs=[pl.BlockSpec((None, tm, tn), lambda i, j, k: (j, i, 0))] * n_out if column_slabs else [o_spec] * n_out,
        out_shape=[jax.ShapeDtypeStruct((N // tn, M, tn) if column_slabs else (M, N), dt) for dt in out_dtypes],
        scratch_shapes=[pltpu.VMEM((tm, tn), F32)] if nk > 1 else [],
        args=[a, b, *extras, *row_params],
        semantics=("parallel", "parallel", "arbitrary"),
    )
    return outs[0] if n_out == 1 else outs


class Row:
    def __init__(self, arr, width=None, col=0, lead=(), rowmap=None):
        self.arr, self.lead, self.col = arr, tuple(lead), col
        self.width = arr.shape[-1] if width is None else width
        self.rowmap = rowmap

    def spec(self, tm):
        lead, col, rowmap = self.lead, self.col, self.rowmap or (lambda i: i)
        return pl.BlockSpec((None,) * len(lead) + (tm, self.width), lambda i: (*lead, rowmap(i), col))


def row_call(name, fn, rows, params, new_outs, red_shapes=(), *, tm, into=None, into_cols=(), stack=None, stacked_onto=None):
    R = rows[0].arr.shape[len(rows[0].lead)]
    tm = min(tm, R)
    assert R % tm == 0, (name, R, tm)
    n_row, n_par, n_new, n_into, n_red = len(rows), len(params), len(new_outs), len(into_cols), len(red_shapes)
    fresh = isinstance(into, jax.ShapeDtypeStruct)

    def body(*refs):
        row_refs = refs[:n_row]
        par_refs = refs[n_row:n_row + n_par]
        pos = n_row + n_par + (1 if (into is not None and not fresh) else 0) + (n_new if stacked_onto is not None else 0)
        new_refs = refs[pos:pos + n_new]
        red_refs = refs[pos + n_new:pos + n_new + n_red]
        pos2 = pos + n_new + n_red
        into_ref = refs[pos2] if into is not None else None
        stage = refs[pos2 + 1:pos2 + 1 + n_into] if into is not None else ()
        i = pl.program_id(0)
        widen = lambda t: t.astype(F32) if jnp.issubdtype(t.dtype, jnp.floating) else t
        vals = fn(*[widen(r[...]) for r in row_refs], *[p[...] for p in par_refs])
        vals = vals if isinstance(vals, (tuple, list)) else (vals,)
        for o, v in zip(new_refs, vals[:n_new]):
            o[...] = v.astype(o.dtype)

        def window_copy(j, step):
            width, off = into_cols[j]
            rows_at = pl.ds(pl.multiple_of(step * tm, tm), tm)
            return pltpu.make_async_copy(stage[j], into_ref.at[rows_at, pl.ds(off, width)], refs[pos2 + 1 + n_into].at[j])

        for j, v in enumerate(vals[n_new:n_new + n_into]):
            @pl.when(i > 0)
            def _(j=j):
                window_copy(j, i - 1).wait()

            stage[j][...] = v.astype(stage[j].dtype)
            window_copy(j, i).start()

            @pl.when(i == R // tm - 1)
            def _(j=j):
                window_copy(j, i).wait()
        for o, v in zip(red_refs, vals[n_new + n_into:]):
            @pl.when(i == 0)
            def _(o=o, v=v):
                o[...] = v.astype(o.dtype)

            @pl.when(i > 0)
            def _(o=o, v=v):
                o[...] += v.astype(o.dtype)

    in_specs = [r.spec(tm) for r in rows] + [pl.BlockSpec(p.shape, lambda i, nd=p.ndim: (0,) * nd) for p in params]
    args = [r.arr for r in rows] + list(params)
    if stack is None:
        out_specs = [pl.BlockSpec((tm, w), lambda i: (i, 0)) for w, _ in new_outs]
        out_shape = [jax.ShapeDtypeStruct((R, w), dt) for w, dt in new_outs]
    else:
        index, depth = stack
        out_specs = [pl.BlockSpec((None, tm, w), lambda i: (index, i, 0)) for w, _ in new_outs]
        out_shape = [jax.ShapeDtypeStruct((depth, R, w), dt) for w, dt in new_outs]
    out_specs += [pl.BlockSpec(s, lambda i, nd=len(s): (0,) * nd) for s in red_shapes]
    out_shape += [jax.ShapeDtypeStruct(s, F32) for s in red_shapes]
    aliases, scratch = {}, []
    if into is not None:
        out_specs.append(pl.BlockSpec(memory_space=pl.ANY))
        out_shape.append(jax.ShapeDtypeStruct(into.shape, into.dtype))
        if not fresh:
            in_specs.append(pl.BlockSpec(memory_space=pl.ANY))
            args.append(into)
            aliases = {len(args) - 1: len(out_shape) - 1}
        scratch = [pltpu.VMEM((tm, w), into.dtype) for w, _ in into_cols] + [pltpu.SemaphoreType.DMA((n_into,))]
    if stacked_onto is not None:
        for j, prev in enumerate(stacked_onto):
            in_specs.append(pl.BlockSpec(memory_space=pl.ANY))
            args.append(prev)
            aliases[len(args) - 1] = j
    outs = pl.pallas_call(
        body,
        name=name,
        grid=(R // tm,),
        in_specs=in_specs,
        out_specs=out_specs,
        out_shape=out_shape,
        scratch_shapes=scratch,
        input_output_aliases=aliases,
        compiler_params=pltpu.CompilerParams(dimension_semantics=("arbitrary",), vmem_limit_bytes=VMEM_BIG),
    )(*args)
    return list(outs)


def vjp_fn(fn, n_row, n_out, passthrough=0):
    def bwd(*args):
        rows = [a.astype(F32) for a in args[:n_row]]
        cts = args[n_row:n_row + n_out]
        thru = args[n_row + n_out:n_row + n_out + passthrough]
        pars = args[n_row + n_out + passthrough:]
        outs, pull = jax.vjp(lambda *xs: tuple(fn(*xs)), *rows, *pars)
        grads = pull(tuple(c.astype(o.dtype) for c, o in zip(cts, outs)))
        return tuple(grads[:n_row]) + tuple(thru) + tuple(grads[n_row:])

    return bwd


def _ln(x, g, b):
    mu = jnp.mean(x, axis=-1, keepdims=True)
    xc = x - mu
    var = jnp.mean(xc * xc, axis=-1, keepdims=True)
    return xc * lax.rsqrt(var + LN_EPS) * g + b


def f_ln_in(x, g, b):
    return (_ln(x, g, b),)


def f_deepnorm(x, y, g, b):
    return (_ln(ALPHA * x + y, g, b),)


def _twice(fn):
    return lambda *args: fn(*args) * 2


STREAM = [(D_MODEL, F32), (D_MODEL, MXU_DTYPE)]


def _rope(t, cos, sin):
    parts = []
    for h in range(LIN_HEADS):
        th = t[:, h * LIN_KEY_DIM:(h + 1) * LIN_KEY_DIM]
        parts.append(th * cos + swap_halves(th) * sin)
    return jnp.concatenate(parts, axis=1)


def f_ret_pre(rq, rk, cos, sin):
    return _rope(rq, cos, sin), _rope(rk, cos, sin) * LIN_KEY_DIM ** -0.5


def _log_sigmoid(x):
    return jnp.minimum(x, 0.0) - jnp.log(1.0 + jnp.exp(-jnp.abs(x)))


def f_gla_pre(gq, glr, wlr, blr):
    gate_pre = dot_nn(glr, wlr) + blr
    return gq * LIN_KEY_DIM ** -0.5, _log_sigmoid(gate_pre) / GLA_GATE_NORMALIZER


def f_normgate(ret_raw, rg, gla_raw, gg, gnorm):
    rets, glas = [], []
    for h in range(LIN_HEADS):
        sl = slice(h * LIN_VAL_DIM, (h + 1) * LIN_VAL_DIM)
        t = ret_raw[:, sl]
        mu = jnp.mean(t, axis=-1, keepdims=True)
        tc = t - mu
        rets.append(tc * lax.rsqrt(jnp.mean(tc * tc, axis=-1, keepdims=True) + LN_EPS))
        u = gla_raw[:, sl]
        glas.append(u * lax.rsqrt(jnp.mean(u * u, axis=-1, keepdims=True) + LN_EPS) * gnorm)
    ret = jax.nn.silu(rg) * jnp.concatenate(rets, axis=1)
    gla = jax.nn.silu(gg) * jnp.concatenate(glas, axis=1)
    return ret, gla


def f_merge(p0, p1, p2, gl):
    out = jax.nn.sigmoid(gl[:, :D_MODEL]) * p0
    out = out + jax.nn.sigmoid(gl[:, D_MODEL:2 * D_MODEL]) * p1
    return (out + jax.nn.sigmoid(gl[:, 2 * D_MODEL:]) * p2,)


def f_loss(y, tgt):
    diff = y - tgt
    sq = jnp.sum(diff * diff, axis=0, keepdims=True)
    lanes = sq[:, :128]
    for j in range(1, D_MODEL // 128):
        lanes = lanes + sq[:, j * 128:(j + 1) * 128]
    return diff * (1.0 / D_MODEL), lanes * (0.5 / D_MODEL)


def _adam(g, w, m, v):
    m = ADAM_B1 * m + (1.0 - ADAM_B1) * g
    v = ADAM_B2 * v + (1.0 - ADAM_B2) * jnp.square(g)
    m_hat = m / (1.0 - ADAM_B1 ** ADAM_STEP)
    v_hat = v / (1.0 - ADAM_B2 ** ADAM_STEP)
    delta = -ADAM_LR * (m_hat / (jnp.sqrt(v_hat) + ADAM_EPS) + ADAM_WD * w)
    return delta, m, v


def f_sum_adam(*args):
    g = args[0].astype(F32)
    for p in args[1:-3]:
        g = g + p.astype(F32)
    return (g,) + _adam(g, *args[-3:])


def f_adam(g, w, m, v):
    return _adam(g, w, m, v)


GROUP = 4
GROUP_ROWS = GROUP * CHUNK
GROUP_KEYS = (GROUP + ATTN_LEFT_CHUNKS) * CHUNK
DIAG_ROWS = REL_CLIP + CHUNK
GRAD_ROWS = 200


def _blocks_by_delta():
    blocks = {}
    for qi in range(GROUP):
        for J in range(GROUP_KEYS // 128):
            blocks.setdefault(qi - 2 * J, []).append((qi, J))
    return blocks


def bias_table(rel_bias, side=None):
    n_tab = rel_bias.shape[0] * ATTN_HEADS

    def body(rb_ref, o_ref):
        h = pl.program_id(0)
        rm = lax.broadcasted_iota(jnp.int32, (CHUNK, 128), 0)
        cm = lax.broadcasted_iota(jnp.int32, (CHUNK, 128), 1)
        ahead = jnp.where(cm >= CHUNK, 1, 0)
        for delta, where in _blocks_by_delta().items():
            base = 64 * delta + PAD_ROWS
            rel = jnp.minimum(base + rm - cm, REL_CLIP) + REL_CLIP
            in_band = (ahead - delta >= 0) & (ahead - delta <= ATTN_LEFT_CHUNKS)
            lo = max(REL_MIN, min(base - 127, REL_CLIP) + REL_CLIP)
            hi = min(base + CHUNK - 1, REL_CLIP) + REL_CLIP

            def step(r, acc, rel=rel):
                return jnp.where(rel == r, rb_ref[h * N_REL + r], acc)

            blk = jnp.zeros((CHUNK, 128), F32) if hi < lo else lax.fori_loop(lo, hi + 1, step, jnp.zeros((CHUNK, 128), F32))
            blk = jnp.where(in_band, blk, NEG_INF)
            for qi, J in where:
                o_ref[pl.ds(qi * CHUNK, CHUNK), pl.ds(J * 128, 128)] = blk

    return _side_call(
        side, body,
        name="bias_table",
        grid=(n_tab,),
        in_specs=[pl.BlockSpec(memory_space=pltpu.SMEM)],
        out_specs=[pl.BlockSpec((None, GROUP_ROWS, GROUP_KEYS), lambda h: (h, 0, 0))],
        out_shape=[jax.ShapeDtypeStruct((n_tab, GROUP_ROWS, GROUP_KEYS), F32)],
        scratch_shapes=[],
        args=[rel_bias.reshape(n_tab * N_REL)],
    )[0]


def bias_table_grad(dbias):
    blocks = _blocks_by_delta()
    below_clip = range(-(CHUNK - 1), REL_CLIP)
    deltas = [d for d in sorted(blocks) if any(-127 <= pos - 64 * d - PAD_ROWS <= CHUNK - 1 for pos in below_clip)]

    def body(d_ref, o_ref, diag):
        sums = []
        for delta in deltas:
            acc = None
            for qi, J in blocks[delta]:
                blk = d_ref[pl.ds(qi * CHUNK, CHUNK), pl.ds(J * 128, 128)]
                acc = blk if acc is None else acc + blk
            sums.append(acc)
        off = lax.broadcasted_iota(jnp.int32, (CHUNK, 128), 0) - lax.broadcasted_iota(jnp.int32, (CHUNK, 128), 1)
        diag[...] = jnp.zeros_like(diag)

        def step(t, carry):
            pos = t - (CHUNK - 1)
            picked = jnp.zeros((CHUNK, 128), F32)
            for delta, s in zip(deltas, sums):
                picked = picked + jnp.where(off == pos - 64 * delta - PAD_ROWS, s, 0.0)
            diag[pl.ds(t, 1), :] = jnp.sum(picked, axis=0, keepdims=True)
            return carry

        lax.fori_loop(0, DIAG_ROWS, step, 0, unroll=8)
        d = d_ref[...]
        total = jnp.sum(jnp.sum(d, axis=1, keepdims=True), axis=0, keepdims=True)
        o_ref[pl.ds(0, DIAG_ROWS), :] = jnp.broadcast_to(jnp.sum(diag[...], axis=1, keepdims=True), (DIAG_ROWS, 128))
        o_ref[pl.ds(DIAG_ROWS, GRAD_ROWS - DIAG_ROWS), :] = jnp.broadcast_to(total, (GRAD_ROWS - DIAG_ROWS, 128))

    out = pl.pallas_call(
        body,
        name="bias_table_grad",
        grid=(ATTN_HEADS,),
        in_specs=[pl.BlockSpec((None, GROUP_ROWS, GROUP_KEYS), lambda h: (h, 0, 0))],
        out_specs=pl.BlockSpec((None, GRAD_ROWS, 128), lambda h: (h, 0, 0)),
        out_shape=jax.ShapeDtypeStruct((ATTN_HEADS, GRAD_ROWS, 128), F32),
        scratch_shapes=[pltpu.VMEM((DIAG_ROWS, 128), F32)],
        compiler_params=pltpu.CompilerParams(dimension_semantics=("arbitrary",)),
    )(dbias)
    below = out[:, :DIAG_ROWS - 1, 0]
    clipped = out[:, DIAG_ROWS, 0] - jnp.sum(below, axis=1)
    return jnp.concatenate([jnp.zeros((ATTN_HEADS, REL_MIN), F32), below, clipped[:, None]], axis=1)


def _attn_stage(q_ref, k_ref, v_ref, qs, kp, vp, T):
    qs[...] = (q_ref[...] * ATTN_HEAD_DIM ** -0.5).astype(qs.dtype)
    for dst, src in ((kp, k_ref), (vp, v_ref)):
        dst[pl.ds(0, PAD_ROWS), :] = jnp.zeros((PAD_ROWS, ATTN_HEAD_DIM), dst.dtype)
        dst[pl.ds(PAD_ROWS, T), :] = src[...].astype(dst.dtype)


STRIP = 256


def _strip_scores(g, t, first_groups, r0, qc, kp, b_ref):
    keys = pl.ds(r0 + t * STRIP if first_groups else pl.multiple_of(r0 + t * STRIP, STRIP), STRIP)
    s = _mxu(qc, kp[keys, :], _NT) + b_ref[:, t * STRIP:(t + 1) * STRIP]
    if first_groups:
        k_chunk = lax.shift_right_logical(lax.broadcasted_iota(jnp.int32, (GROUP_ROWS, STRIP), 1), 6)
        s = jnp.where(g * GROUP + t * (STRIP // CHUNK) + k_chunk >= ATTN_LEFT_CHUNKS, s, NEG_INF)
    return s, keys


def _for_groups(T, step):
    n_first = min(ATTN_LEFT_CHUNKS // GROUP, T // GROUP_ROWS)
    for g in range(n_first):
        step(g, True)
    n_loop = T // GROUP_ROWS - n_first
    unroll = next(u for u in (14, 7, 4, 2, 1) if n_loop % u == 0)
    lax.fori_loop(n_first, T // GROUP_ROWS, lambda g, c: (step(g, False), c)[1], 0, unroll=unroll)


def _head_spec(T, base):
    return pl.BlockSpec((T, ATTN_HEAD_DIM), lambda h: (0, base + h))


def _attn_scratch(T):
    return [pltpu.VMEM((T, ATTN_HEAD_DIM), MXU_DTYPE)] + [pltpu.VMEM((T + PAD_ROWS, ATTN_HEAD_DIM), MXU_DTYPE)] * 2


def attn_fwd(proj, bias_tab, first_tab, side=None):
    T = proj.shape[0]

    def body(q_ref, k_ref, v_ref, b_ref, o_ref, lse_ref, qs, kp, vp):
        _attn_stage(q_ref, k_ref, v_ref, qs, kp, vp, T)

        def step(g, first):
            r0 = g * GROUP_ROWS if first else pl.multiple_of(g * GROUP_ROWS, GROUP_ROWS)
            rows = pl.ds(r0, GROUP_ROWS)
            qc = qs[rows, :]
            m = jnp.full((GROUP_ROWS, 1), NEG_INF, F32)
            l = jnp.zeros((GROUP_ROWS, 1), F32)
            acc = jnp.zeros((GROUP_ROWS, ATTN_HEAD_DIM), F32)
            for t in range(GROUP_KEYS // STRIP):
                s, keys = _strip_scores(g, t, first, r0, qc, kp, b_ref)
                m_new = jnp.maximum(m, jnp.max(s, axis=-1, keepdims=True))
                keep = jnp.exp(m - m_new)
                e = jnp.exp(s - m_new)
                l = l * keep + jnp.sum(e, axis=-1, keepdims=True)
                acc = acc * keep + _mxu(e, vp[keys, :], _NN)
                m = m_new
            o_ref[rows, :] = (acc * (1.0 / l)).astype(o_ref.dtype)
            lse_ref[rows, :] = jnp.broadcast_to(m + jnp.log(l), (GROUP_ROWS, ATTN_HEAD_DIM))

        _for_groups(T, step)

    out, lse = _side_call(
        side, body,
        name="attn_fwd",
        grid=(ATTN_HEADS,),
        in_specs=[_head_spec(T, COL["aq"] // 128), _head_spec(T, COL["ak"] // 128), _head_spec(T, COL["av"] // 128),
                  pl.BlockSpec((None, GROUP_ROWS, GROUP_KEYS), lambda h: (first_tab + h, 0, 0))],
        out_specs=[pl.BlockSpec((T, ATTN_HEAD_DIM), lambda h: (0, h))] * 2,
        out_shape=[jax.ShapeDtypeStruct((T, ATTN_HEADS * ATTN_HEAD_DIM), MXU_DTYPE),
                   jax.ShapeDtypeStruct((T, ATTN_HEADS * ATTN_HEAD_DIM), F32)],
        scratch_shapes=_attn_scratch(T),
        args=[proj, proj, proj, bias_tab],
    )
    return out, lse


def attn_bwd(proj, bias_tab, first_tab, out, lse, dout, dproj, side=None):
    T = proj.shape[0]

    def body(q_ref, k_ref, v_ref, b_ref, do_ref, o_ref, lse_ref, dproj_in, dbias_ref, dproj_ref, qs, kp, vp, dkp, dvp, dq, stage):
        del dproj_in
        h = pl.program_id(0)
        _attn_stage(q_ref, k_ref, v_ref, qs, kp, vp, T)
        dkp[...] = jnp.zeros_like(dkp)
        dvp[...] = jnp.zeros_like(dvp)
        dbias_ref[...] = jnp.zeros_like(dbias_ref)

        def step(g, first):
            r0 = g * GROUP_ROWS if first else pl.multiple_of(g * GROUP_ROWS, GROUP_ROWS)
            rows = pl.ds(r0, GROUP_ROWS)
            qc, do = qs[rows, :], do_ref[rows, :]
            lse = jnp.concatenate([lse_ref[rows, :]] * (STRIP // ATTN_HEAD_DIM), axis=1)
            row_term = jnp.sum(do.astype(F32) * o_ref[rows, :].astype(F32), axis=-1, keepdims=True)
            dq_acc = jnp.zeros((GROUP_ROWS, ATTN_HEAD_DIM), F32)
            for t in range(GROUP_KEYS // STRIP):
                s, keys = _strip_scores(g, t, first, r0, qc, kp, b_ref)
                cols = slice(t * STRIP, (t + 1) * STRIP)
                kb, vb = kp[keys, :], vp[keys, :]
                p = jnp.exp(s - lse)
                dvp[keys, :] += _mxu(p, do, _TN)
                ds = p * (_mxu(do, vb, _NT) - row_term)
                dq_acc = dq_acc + _mxu(ds, kb, _NN)
                dkp[keys, :] += _mxu(ds, qc, _TN)
                dbias_ref[:, cols] += ds
            dq[rows, :] = dq_acc * ATTN_HEAD_DIM ** -0.5

        _for_groups(T, step)
        col = pl.multiple_of(h * ATTN_HEAD_DIM, ATTN_HEAD_DIM)
        for grad, first_row, base in ((dq, 0, "aq"), (dkp, PAD_ROWS, "ak"), (dvp, PAD_ROWS, "av")):
            stage[...] = grad[pl.ds(first_row, T), :].astype(stage.dtype)
            pltpu.sync_copy(stage, dproj_ref.at[:, pl.ds(COL[base] + col, ATTN_HEAD_DIM)])

    dbias, dproj = _side_call(
        side, body,
        name="attn_bwd",
        grid=(ATTN_HEADS,),
        in_specs=[_head_spec(T, COL["aq"] // 128), _head_spec(T, COL["ak"] // 128), _head_spec(T, COL["av"] // 128),
                  pl.BlockSpec((None, GROUP_ROWS, GROUP_KEYS), lambda h: (first_tab + h, 0, 0))]
        + [pl.BlockSpec((T, ATTN_HEAD_DIM), lambda h: (0, h))] * 3 + [pl.BlockSpec(memory_space=pl.ANY)],
        out_specs=[pl.BlockSpec((None, GROUP_ROWS, GROUP_KEYS), lambda h: (h, 0, 0)), pl.BlockSpec(memory_space=pl.ANY)],
        out_shape=[jax.ShapeDtypeStruct((ATTN_HEADS, GROUP_ROWS, GROUP_KEYS), F32),
                   jax.ShapeDtypeStruct(dproj.shape, dproj.dtype)],
        scratch_shapes=_attn_scratch(T) + [pltpu.VMEM((T + PAD_ROWS, ATTN_HEAD_DIM), F32)] * 2
        + [pltpu.VMEM((T, ATTN_HEAD_DIM), F32), pltpu.VMEM((T, ATTN_HEAD_DIM), dproj.dtype)],
        args=[proj, proj, proj, bias_tab, dout, out, lse, dproj],
        aliases={7: 1},
    )
    return dproj, dbias


def _lin_segment(q, k, v, la, st):
    lower = _tri(CHUNK, True)
    steps = (lax.broadcasted_iota(jnp.int32, (CHUNK, 1), 0) + 1).astype(F32)
    outs = []
    for i in range(SEG_CHUNKS):
        sl = slice(i * CHUNK, (i + 1) * CHUNK)
        qi, ki, vi = q[sl], k[sl], v[sl]
        if la.shape[0] == 1:
            cum, last = la * steps, la * float(CHUNK)
        else:
            cum, last = chunk_cumsum(la[sl]), jnp.sum(la[sl], axis=0, keepdims=True)
        e_pos, e_neg = jnp.exp(cum), jnp.exp(-cum)
        q_fwd = qi * e_pos
        s_fwd = dot_nt(q_fwd, ki * e_neg)
        s_bwd = dot_nt(qi * e_neg, ki * e_pos)
        o = dot_nn(jnp.where(lower, s_fwd, s_bwd), vi) + dot_nt(q_fwd, st)
        st = jnp.exp(last) * st + dot_tn(vi, ki * jnp.exp(last - cum))
        outs.append(o)
    return jnp.concatenate(outs, axis=0), st


class Win:
    def __init__(self, arr, width, base=0):
        self.arr, self.width, self.base = arr, width, base


HEADS_PER_STEP = 2


def _lin_spec(w, seg_of):
    assert w.base % HEADS_PER_STEP == 0
    first, width = w.base // HEADS_PER_STEP, HEADS_PER_STEP * w.width
    if w.arr.shape[0] == 1:
        return pl.BlockSpec((1, width), lambda p, s: (0, first + p))
    return pl.BlockSpec((SEG, width), lambda p, s: (seg_of(s), first + p))


def _head_cols(ref, j, width):
    return ref[:, j * width:(j + 1) * width]


def _decay_rows(la_ref, j):
    return _head_cols(la_ref, j, LIN_KEY_DIM)


_STATE_BLOCK = (HEADS_PER_STEP, None, LIN_VAL_DIM, LIN_KEY_DIM)


def lin_fwd(name, q, k, v, la, side=None):
    T = q.arr.shape[0]
    nseg = T // SEG

    def body(q_ref, k_ref, v_ref, la_ref, o_ref, st_ref, st):
        @pl.when(pl.program_id(1) == 0)
        def _():
            st[...] = jnp.zeros_like(st)

        for j in range(HEADS_PER_STEP):
            st0 = st[j]
            st_ref[j] = st0
            o, st1 = _lin_segment(_head_cols(q_ref, j, LIN_KEY_DIM), _head_cols(k_ref, j, LIN_KEY_DIM),
                                  _head_cols(v_ref, j, LIN_VAL_DIM), _decay_rows(la_ref, j), st0)
            o_ref[:, j * LIN_VAL_DIM:(j + 1) * LIN_VAL_DIM] = o
            st[j] = st1

    fwd = lambda s: s
    out, states = _side_call(
        side, body,
        name=name,
        grid=(LIN_HEADS // HEADS_PER_STEP, nseg),
        in_specs=[_lin_spec(w, fwd) for w in (q, k, v, la)],
        out_specs=[pl.BlockSpec((SEG, HEADS_PER_STEP * LIN_VAL_DIM), lambda p, s: (s, p)),
                   pl.BlockSpec(_STATE_BLOCK, lambda p, s: (p, s, 0, 0))],
        out_shape=[jax.ShapeDtypeStruct((T, LIN_HEADS * LIN_VAL_DIM), F32),
                   jax.ShapeDtypeStruct((LIN_HEADS, nseg, LIN_VAL_DIM, LIN_KEY_DIM), F32)],
        scratch_shapes=[pltpu.VMEM((HEADS_PER_STEP, LIN_VAL_DIM, LIN_KEY_DIM), F32)],
        args=[q.arr, k.arr, v.arr, la.arr],
    )
    return out, states


def lin_bwd(name, q, k, v, la, states, dout, with_decay_grad, side=None, into=None, windows=None):
    T = q.arr.shape[0]
    nseg = T // SEG
    n_out = 4 if with_decay_grad else 3

    widths = [LIN_KEY_DIM, LIN_KEY_DIM, LIN_VAL_DIM] + ([LIN_KEY_DIM] if with_decay_grad else [])
    windows = dict(windows or {})
    fresh = [i for i in range(n_out) if i not in windows]
    placed = sorted(windows)
    n_pairs = LIN_HEADS // HEADS_PER_STEP
    rev = lambda s: nseg - 1 - s

    def body(q_ref, k_ref, v_ref, la_ref, st_ref, do_ref, *rest):
        rest = rest[1:] if placed else rest
        outs = dict(zip(fresh, rest[:len(fresh)]))
        rest = rest[len(fresh):]
        into_ref, rest = (rest[0], rest[1:]) if placed else (None, rest)
        dst, stages = rest[0], dict(zip(placed, rest[1:1 + len(placed)]))
        p, s = pl.program_id(0), pl.program_id(1)

        @pl.when(s == 0)
        def _():
            dst[...] = jnp.zeros_like(dst)

        def window_copy(n, i):
            width = HEADS_PER_STEP * widths[i]
            rows = pl.ds(pl.multiple_of(rev(s) * SEG, SEG), SEG)
            cols = pl.ds(pl.multiple_of(windows[i] + p * width, width), width)
            return pltpu.make_async_copy(stages[i], into_ref.at[rows, cols], rest[1 + len(placed)].at[n])

        for n, i in enumerate(placed):
            @pl.when((p > 0) | (s > 0))
            def _(n=n, i=i):
                window_copy(n, i).wait()

        for j in range(HEADS_PER_STEP):
            _, pull = jax.vjp(_lin_segment, _head_cols(q_ref, j, LIN_KEY_DIM).astype(F32),
                              _head_cols(k_ref, j, LIN_KEY_DIM).astype(F32), _head_cols(v_ref, j, LIN_VAL_DIM).astype(F32),
                              _decay_rows(la_ref, j), st_ref[j])
            grads = pull((_head_cols(do_ref, j, LIN_VAL_DIM), dst[j]))
            for i in range(n_out):
                dest = stages[i] if i in windows else outs[i]
                dest[:, j * widths[i]:(j + 1) * widths[i]] = grads[i].astype(dest.dtype)
            dst[j] = grads[4]

        for n, i in enumerate(placed):
            window_copy(n, i).start()

            @pl.when((p == n_pairs - 1) & (s == nseg - 1))
            def _(n=n, i=i):
                window_copy(n, i).wait()

    key_spec = pl.BlockSpec((SEG, HEADS_PER_STEP * LIN_KEY_DIM), lambda p, s: (rev(s), p))
    val_spec = pl.BlockSpec((SEG, HEADS_PER_STEP * LIN_VAL_DIM), lambda p, s: (rev(s), p))
    all_specs = [key_spec, key_spec, val_spec] + ([key_spec] if with_decay_grad else [])
    any_spec = pl.BlockSpec(memory_space=pl.ANY)
    outs = _side_call(
        side, body,
        name=name,
        grid=(n_pairs, nseg),
        in_specs=[_lin_spec(w, rev) for w in (q, k, v, la)] + [
            pl.BlockSpec(_STATE_BLOCK, lambda p, s: (p, rev(s), 0, 0)), val_spec] + ([any_spec] if placed else []),
        out_specs=[all_specs[i] for i in fresh] + ([any_spec] if placed else []),
        out_shape=[jax.ShapeDtypeStruct((T, LIN_HEADS * widths[i]), F32) for i in fresh]
        + ([jax.ShapeDtypeStruct(into.shape, into.dtype)] if placed else []),
        scratch_shapes=[pltpu.VMEM((HEADS_PER_STEP, LIN_VAL_DIM, LIN_KEY_DIM), F32)]
        + [pltpu.VMEM((SEG, HEADS_PER_STEP * widths[i]), into.dtype) for i in placed]
        + ([pltpu.SemaphoreType.DMA((len(placed),))] if placed else []),
        args=[q.arr, k.arr, v.arr, la.arr, states, dout] + ([into] if placed else []),
        aliases={6: len(fresh)} if placed else None,
    )
    return outs


def _coords():
    return lax.axis_index("x"), lax.axis_index("y"), lax.axis_index("c")


def _peer(k):
    x, y, c = _coords()
    px = 1 - x if k & 4 else x
    py = 1 - y if k & 2 else y
    pc = 1 - c if k & 1 else c
    return (px, py, pc), 4 * px + 2 * py + pc


def _remote(src, dst, send_sem, recv_sem, device):
    return pltpu.make_async_remote_copy(src_ref=src, dst_ref=dst, send_sem=send_sem, recv_sem=recv_sem,
                                        device_id=device, device_id_type=pl.DeviceIdType.MESH)


class Side:
    def __init__(self, kind, arrays):
        self.kind, self.arrays, self.n = kind, list(arrays), len(arrays)
        self.outs_per = 1
        self.slots = {"gather": N_DEV - 1, "pair": 4, "chips": 3}[kind]

    def out_shape(self):
        if self.kind == "gather":
            return [jax.ShapeDtypeStruct((N_DEV,) + a.shape, a.dtype) for a in self.arrays]
        return [jax.ShapeDtypeStruct((4,) + a.shape[1:], a.dtype) for a in self.arrays]

    def sems(self):
        return [pltpu.SemaphoreType.DMA((self.n, self.slots)), pltpu.SemaphoreType.DMA((self.n, self.slots)),
                pltpu.SemaphoreType.DMA((self.n,))]

    def start(self, ins, outs, sems):
        send, recv, own = sems
        x, y, c = _coords()
        me = 4 * x + 2 * y + c
        for a in range(self.n):
            if self.kind == "gather":
                pltpu.make_async_copy(ins[a], outs[a].at[me], own.at[a]).start()
                for k in (1, 2, 4, 6):
                    peer, _ = _peer(k)
                    _remote(ins[a], outs[a].at[me], send.at[a, k - 1], recv.at[a, k - 1], peer).start()
            elif self.kind == "pair":
                for j in range(4):
                    _remote(ins[a].at[2 * j + 1 - c], outs[a].at[j], send.at[a, j], recv.at[a, j], (x, y, 1 - c)).start()
            else:
                chip = 2 * x + y
                pltpu.make_async_copy(ins[a].at[chip], outs[a].at[chip], own.at[a]).start()
                for d in (1, 2, 3):
                    tx, ty = (1 - x if d & 2 else x), (1 - y if d & 1 else y)
                    _remote(ins[a].at[2 * tx + ty], outs[a].at[chip], send.at[a, d - 1], recv.at[a, d - 1], (tx, ty, c)).start()

    def finish(self, ins, outs, sems):
        send, recv, own = sems
        x, y, c = _coords()
        me = 4 * x + 2 * y + c
        sibling = (x, y, 1 - c)
        for a in range(self.n):
            if self.kind == "gather":
                blk = outs[a].at[me]
                for k in (2, 4, 6):
                    _, pidx = _peer(k)
                    _remote(blk, blk, send.at[a, k - 1], recv.at[a, k - 1], sibling).wait_recv()
                    _remote(outs[a].at[pidx], outs[a].at[pidx], send.at[a, k], recv.at[a, k], sibling).start()
                pltpu.make_async_copy(ins[a], blk, own.at[a]).wait()
                for k in range(1, N_DEV):
                    _remote(blk, blk, send.at[a, k - 1], recv.at[a, k - 1], sibling).wait_send()
                for k in (1, 3, 5, 7):
                    _remote(blk, blk, send.at[a, k - 1], recv.at[a, k - 1], sibling).wait_recv()
            elif self.kind == "pair":
                for j in range(4):
                    _remote(ins[a].at[j], outs[a].at[j], send.at[a, j], recv.at[a, j], sibling).wait()
            else:
                slab = outs[a].at[0]
                pltpu.make_async_copy(slab, slab, own.at[a]).wait()
                for d in (1, 2, 3):
                    _remote(slab, slab, send.at[a, d - 1], recv.at[a, d - 1], sibling).wait()


def _side_call(side, body, *, name, grid, in_specs, out_specs, out_shape, scratch_shapes, args, semantics=None,
               aliases=None):
    in_specs, out_specs, out_shape = list(in_specs), list(out_specs), list(out_shape)
    scratch_shapes, args = list(scratch_shapes), list(args)
    n_in, n_out = len(in_specs), len(out_specs)
    kernel_body = body
    if side is not None:
        n_so = side.n * side.outs_per

        def kernel_body(*refs):
            ins, side_ins = refs[:n_in], refs[n_in:n_in + side.n]
            pos = n_in + side.n
            outs, side_outs = refs[pos:pos + n_out], refs[pos + n_out:pos + n_out + n_so]
            rest = refs[pos + n_out + n_so:]
            scratch, sems = rest[:-3], rest[-3:]
            ids = [pl.program_id(ax) for ax in range(len(grid))]
            first = functools.reduce(jnp.logical_and, [i == 0 for i in ids])
            last = functools.reduce(jnp.logical_and, [i == g - 1 for i, g in zip(ids, grid)])

            @pl.when(first)
            def _():
                side.start(side_ins, side_outs, sems)

            body(*ins, *outs, *scratch)

            @pl.when(last)
            def _():
                side.finish(side_ins, side_outs, sems)

        any_spec = pl.BlockSpec(memory_space=pl.ANY)
        in_specs += [any_spec] * side.n
        args += side.arrays
        out_specs += [any_spec] * n_so
        out_shape += side.out_shape()
        scratch_shapes += side.sems()
        semantics = ("arbitrary",) * len(grid)
    outs = pl.pallas_call(
        kernel_body,
        name=name,
        grid=grid,
        in_specs=in_specs,
        out_specs=out_specs,
        out_shape=out_shape,
        scratch_shapes=scratch_shapes,
        input_output_aliases=aliases or {},
        compiler_params=pltpu.CompilerParams(dimension_semantics=semantics or ("arbitrary",) * len(grid),
                                             vmem_limit_bytes=VMEM_BIG),
    )(*args)
    if side is not None:
        side.results = list(outs[n_out:])
    return list(outs[:n_out])


def carrier(name, side):
    _side_call(side, lambda: None, name=name, grid=(1,), in_specs=[], out_specs=[], out_shape=[], scratch_shapes=[], args=[])
    return side.results


def all_reduce_small(vec):
    def body(v_ref, o_ref, land, send_sems, recv_sems):
        x, y, c = _coords()
        me = 4 * x + 2 * y + c
        land[me] = v_ref[...]
        copies = []
        for k in range(1, N_DEV):
            peer, _ = _peer(k)
            cp = pltpu.make_async_remote_copy(
                src_ref=v_ref, dst_ref=land.at[me], send_sem=send_sems.at[k - 1], recv_sem=recv_sems.at[k - 1],
                device_id=peer, device_id_type=pl.DeviceIdType.MESH)
            cp.start()
            copies.append(cp)
        for cp in copies:
            cp.wait()
        total = land[0]
        for s in range(1, N_DEV):
            total = total + land[s]
        o_ref[...] = total

    return pl.pallas_call(
        body,
        name="all_reduce_small",
        in_specs=[pl.BlockSpec(memory_space=pltpu.VMEM)],
        out_specs=pl.BlockSpec(memory_space=pltpu.VMEM),
        out_shape=jax.ShapeDtypeStruct(vec.shape, F32),
        scratch_shapes=[pltpu.VMEM((N_DEV,) + vec.shape, F32), pltpu.SemaphoreType.DMA((N_DEV - 1,)),
                        pltpu.SemaphoreType.DMA((N_DEV - 1,))],
    )(vec)


def _pack(parts):
    flat = jnp.concatenate([p.reshape(-1).astype(F32) for p in parts])
    n = flat.shape[0]
    rows = -(-n // 1024) * 8
    return jnp.pad(flat, (0, rows * 128 - n)).reshape(rows, 128)


def _unpack(packed, shapes):
    flat, out, pos = packed.reshape(-1), [], 0
    for s in shapes:
        n = math.prod(s)
        out.append(flat[pos:pos + n].reshape(s))
        pos += n
    return out


def _row2(v):
    return v.reshape(1, -1)


def _rider(hooks, name, ctx=None):
    if name not in hooks:
        return None, lambda: None
    side, done = hooks[name](ctx)
    return side, lambda: done(side.results)


def _layer_forward(h, hb, W, consts, hooks):
    cos, sin, ret_decay = consts
    side, landed = _rider(hooks, "proj")
    proj = mm("proj", hb, W["in"], "nn", tm=4096, tn=512, tk=1024, side=side, out_dtype=PROJ_DTYPE)
    landed()
    bias_tab, first_tab = W["bias_tab"]
    side, landed = _rider(hooks, "attn_fwd")
    attn, attn_lse = attn_fwd(proj, bias_tab, first_tab, side=side)
    landed()
    rq2, rk2 = row_call("ret_pre", f_ret_pre, [Row(proj, 512, COL["rq"] // 512), Row(proj, 512, COL["rk"] // 512),
                                              Row(cos), Row(sin)], [], [(512, F32), (512, F32)], tm=512)
    side, landed = _rider(hooks, "ret_fwd")
    ret_raw, ret_st = lin_fwd("ret_fwd", Win(rq2, 128), Win(rk2, 128), Win(proj, 256, COL["rv"] // 256), Win(ret_decay, 128),
                              side=side)
    landed()
    gq2, gla_la = row_call("gla_pre", f_gla_pre, [Row(proj, 512, COL["gq"] // 512), Row(proj, GLR_BLOCK, GLR_OFF // GLR_BLOCK)],
                           [W["wlr"], W["blr"]], [(512, F32), (512, F32)], tm=512)
    side, landed = _rider(hooks, "gla_fwd")
    gla_raw, gla_st = lin_fwd("gla_fwd", Win(gq2, 128), Win(proj, 128, COL["gk"] // 128), Win(proj, 256, COL["gv"] // 256),
                              Win(gla_la, 128), side=side)
    landed()
    ret_b, gla_b = row_call("normgate", f_normgate, [Row(ret_raw), Row(proj, 1024, COL["rg"] // 1024), Row(gla_raw),
                                                    Row(proj, 1024, COL["gg"] // 1024)], [W["gnorm"]],
                            [(1024, MXU_DTYPE), (1024, MXU_DTYPE)], tm=512)
    branches = (attn, ret_b, gla_b)
    p = [mm("branch_proj", branches[n], W["branch"][n], "nn", tm=2048, tn=1024, tk=1024) for n in range(3)]
    merged, = row_call("merge", f_merge, [Row(p[0]), Row(p[1]), Row(p[2]), Row(proj, GATES_W, GATES_OFF // GATES_W)], [],
                       [(1024, MXU_DTYPE)], tm=512)
    post_norm = lambda y, x, g, b: f_deepnorm(x, y, g, b) * 2 + (y,)
    x1, x1b, mix = mm("out_proj_ln1", merged, W["out"], "nn", tm=512, tn=D_MODEL, tk=1024, extras=(h,),
                      row_params=(W["ln1_g"], W["ln1_b"]), epi=post_norm, n_out=3, out_dtypes=(F32, MXU_DTYPE, F32))
    side, landed = _rider(hooks, "mlp_up")
    u, act = mm("mlp_up", x1b, W["up"], "nn", tm=4096, tn=512, tk=1024, n_out=2, out_dtypes=(F32, MXU_DTYPE),
                epi=lambda r: (r, jnp.square(jnp.maximum(r, 0.0))), side=side)
    landed()
    x2, x2b, y = mm("mlp_down_ln2", act, W["down"], "nn", tm=1024, tn=D_MODEL, tk=1024, extras=(x1,),
                    row_params=(W["ln2_g"], W["ln2_b"]), epi=post_norm, n_out=3, out_dtypes=(F32, MXU_DTYPE, F32))
    saved = dict(h=h, hb=hb, proj=proj, attn_lse=attn_lse, rq2=rq2, rk2=rk2, ret_st=ret_st, gq2=gq2, gla_la=gla_la,
                 gla_st=gla_st, ret_raw=ret_raw, gla_raw=gla_raw, branches=branches, p=p, merged=merged, mix=mix, x1=x1, x1b=x1b,
                 u=u, act=act, y=y)
    return x2, x2b, saved


def _layer_backward(dx2, S, W, consts, hooks):
    cos, sin, ret_decay = consts
    h, proj, T = S["h"], S["proj"], S["h"].shape[0]
    add = lambda r, e: r + e
    dx1a, dy, dg2, db2 = row_call("ln2_bwd", vjp_fn(f_deepnorm, 2, 1), [Row(S["x1"]), Row(S["y"]), Row(dx2)],
                                  [W["ln2_g"], W["ln2_b"]], [(1024, F32), (1024, MXU_DTYPE)], [(1, 1024), (1, 1024)], tm=512)
    side, landed = _rider(hooks, "mlp_down_bwd")
    du = mm("mlp_down_bwd", dy, W["down"], "nt", tm=2048, tn=512, tk=1024, extras=(S["u"],),
            epi=lambda r, u: r * (2.0 * jnp.maximum(u, 0.0)), out_dtype=MXU_DTYPE, side=side)
    landed()
    g_down = mm("mlp_down_wgrad", S["act"], dy, "tn", tm=1024, tn=1024, tk=2048, out_dtype=WIRE_DTYPE)
    dx1 = mm("mlp_up_bwd", du, W["up"], "nt", tm=1024, tn=1024, tk=2048, extras=(dx1a,), epi=add)
    g_up = mm("mlp_up_wgrad", S["x1b"], du, "tn", tm=1024, tn=D_FF // N_DEV, tk=2048, out_dtype=WIRE_DTYPE,
              column_slabs=True)
    dha, dmix, dg1, db1 = row_call("ln1_bwd", vjp_fn(f_deepnorm, 2, 1), [Row(h), Row(S["mix"]), Row(dx1)],
                                   [W["ln1_g"], W["ln1_b"]], [(1024, F32), (1024, MXU_DTYPE)], [(1, 1024), (1, 1024)], tm=512)
    dmerged = mm("out_proj_bwd", dmix, W["out"], "nt", tm=1024, tn=1024, tk=1024)
    g_out = mm("out_proj_wgrad", S["merged"], dmix, "tn", tm=1024, tn=1024, tk=2048, out_dtype=WIRE_DTYPE)
    dproj = jax.ShapeDtypeStruct((T, PROJ_PAD), MXU_DTYPE)
    p = S["p"]
    dp0, dp1, dp2, dproj = row_call(
        "merge_bwd", vjp_fn(f_merge, 4, 1), [Row(p[0]), Row(p[1]), Row(p[2]), Row(proj, GATES_W, GATES_OFF // GATES_W), Row(dmerged)],
        [], [(1024, MXU_DTYPE)] * 3, tm=512, into=dproj, into_cols=[(GATES_W, GATES_OFF)])
    dps = (dp0, dp1, dp2)
    dbr = [mm("branch_proj_bwd", dps[n], W["branch"][n], "nt", tm=2048, tn=1024, tk=1024,
              out_dtype=MXU_DTYPE if n == 0 else F32) for n in range(3)]
    g_branch = [mm("branch_proj_wgrad", S["branches"][n], dps[n], "tn", tm=1024, tn=1024, tk=2048, out_dtype=WIRE_DTYPE)
                for n in range(3)]
    dret_raw, dgla_raw, dgnorm, dproj = row_call(
        "normgate_bwd", vjp_fn(lambda a, b, c, d, g: _reorder_normgate(a, b, c, d, g), 4, 2),
        [Row(S["ret_raw"]), Row(S["gla_raw"]), Row(proj, 1024, COL["rg"] // 1024), Row(proj, 1024, COL["gg"] // 1024),
         Row(dbr[1]), Row(dbr[2])], [W["gnorm"]], [(1024, F32), (1024, F32)], [(1, LIN_VAL_DIM)], tm=512,
        into=dproj, into_cols=[(1024, COL["rg"]), (1024, COL["gg"])])
    side, landed = _rider(hooks, "gla_bwd")
    dgq2, dgla_la, dproj = lin_bwd("gla_bwd", Win(S["gq2"], 128), Win(proj, 128, COL["gk"] // 128),
                                   Win(proj, 256, COL["gv"] // 256), Win(S["gla_la"], 128), S["gla_st"], dgla_raw, True,
                                   side=side, into=dproj, windows={1: COL["gk"], 2: COL["gv"]})
    landed()
    dwlr, dblr, dproj = row_call(
        "gla_pre_bwd", _gla_pre_bwd, [Row(proj, 512, COL["gq"] // 512), Row(proj, GLR_BLOCK, GLR_OFF // GLR_BLOCK),
                                      Row(dgq2), Row(dgla_la)], [W["wlr"], W["blr"]], [],
        [(GLR_BLOCK, 512), (1, 512)], tm=512, into=dproj,
        into_cols=[(512, COL["gq"]), (PROJ_PAD - GLR_OFF, GLR_OFF)])
    side, landed = _rider(hooks, "ret_bwd", dict(branch=g_branch, out=g_out, up=g_up, down=g_down))
    drq2, drk2, dproj = lin_bwd("ret_bwd", Win(S["rq2"], 128), Win(S["rk2"], 128), Win(proj, 256, COL["rv"] // 256),
                                Win(ret_decay, 128), S["ret_st"], dret_raw, False, side=side, into=dproj,
                                windows={2: COL["rv"]})
    landed()
    dproj, = row_call(
        "ret_pre_bwd", _ret_pre_bwd, [Row(proj, 512, COL["rq"] // 512), Row(proj, 512, COL["rk"] // 512), Row(cos), Row(sin),
                                      Row(drq2), Row(drk2)], [], [], tm=512, into=dproj,
        into_cols=[(512, COL["rq"]), (512, COL["rk"])])
    side, landed = _rider(hooks, "attn_bwd")
    dproj, dbias = attn_bwd(proj, *W["bias_tab"], S["branches"][0], S["attn_lse"], dbr[0], dproj, side=side)
    landed()
    d_rel_bias = bias_table_grad(dbias)
    side, landed = _rider(hooks, "proj_wgrad", dict(branch=g_branch, out=g_out, up=g_up, down=g_down))
    g_in = mm("proj_wgrad", S["hb"], dproj, "tn", tm=1024, tn=1280, tk=2048, out_dtype=WIRE_DTYPE, side=side)
    landed()
    side, landed = _rider(hooks, "proj_bwd", {"in": g_in})
    dh = mm("proj_bwd", dproj, W["in"], "nt", tm=1024, tn=1024, tk=2560, extras=(dha,), epi=add, side=side)
    landed()
    big = dict(g_in=g_in, branch=g_branch, out=g_out, up=g_up, down=g_down)
    small = dict(rel_bias=d_rel_bias, wlr=dwlr[:GLA_RANK], blr=dblr, gnorm=dgnorm, ln1_g=dg1, ln1_b=db1, ln2_g=dg2, ln2_b=db2)
    return dh, big, small


def _reorder_normgate(ret_raw, gla_raw, rg, gg, gnorm):
    return f_normgate(ret_raw, rg, gla_raw, gg, gnorm)


def _gla_pre_bwd(gq, glr, dq2, dla, wlr, blr):
    _, pull = jax.vjp(f_gla_pre, gq, glr, wlr, blr)
    dgq, dglr, dwlr, dblr = pull((dq2, dla))
    dglr = jnp.concatenate([dglr, jnp.zeros((dglr.shape[0], PROJ_PAD - GLR_OFF - GLR_BLOCK), F32)], axis=1)
    return dgq, dglr, dwlr, dblr


def _ret_pre_bwd(rq, rk, cos, sin, drq2, drk2):
    _, pull = jax.vjp(lambda a, b: f_ret_pre(a, b, cos, sin), rq, rk)
    return pull((drq2, drk2))


SHARDED = ("in", "branch", "out", "up", "down")

_FULL = {
    "branch": lambda g: [g[:, n].reshape(D_MODEL, D_MODEL) for n in range(3)],
    "out": lambda g: g.reshape(D_MODEL, D_MODEL),
    "up": lambda g: jnp.transpose(g, (1, 0, 2)).reshape(D_MODEL, D_FF),
    "down": lambda g: g.reshape(D_FF, D_MODEL),
}


def _owner_pieces(s):
    shard = PROJ_WIDTH // N_DEV
    lo, hi = s * shard, (s + 1) * shard
    pieces = []
    for a, b, padded in ((0, GLR_SRC, 0), (GLR_SRC, GATES_SRC, GLR_OFF), (GATES_SRC, PROJ_WIDTH, GATES_OFF)):
        x0, x1 = max(lo, a), min(hi, b)
        if x0 < x1:
            pieces.append((x0 - lo, x1 - x0, padded + x0 - a))
    return pieces


def _full_in(g):
    placed = sorted((p, s, c, w) for s in range(N_DEV) for c, w, p in _owner_pieces(s))
    cols = [g[s][:, c:c + w] for _, s, c, w in placed]
    return jnp.concatenate(cols + [jnp.zeros((D_MODEL, PROJ_PAD - PROJ_WIDTH), g.dtype)], axis=1)


def _full_wlr(g):
    wlr = jnp.transpose(g, (1, 0, 2)).reshape(GLA_RANK, LIN_HEADS * LIN_KEY_DIM)
    return jnp.pad(wlr, ((0, GLR_BLOCK - GLA_RANK), (0, 0)))


def _slab_in(g):
    return jnp.stack([jnp.concatenate([g[:, p:p + w] for _, w, p in _owner_pieces(s)], axis=1) for s in range(N_DEV)])


_SLABS = {
    "in": _slab_in,
    "branch": lambda gs: jnp.stack([b.reshape(N_DEV, D_MODEL // N_DEV, D_MODEL) for b in gs], axis=1),
    "out": lambda g: g.reshape(N_DEV, D_MODEL // N_DEV, D_MODEL),
    "up": lambda g: g,
    "down": lambda g: g.reshape(N_DEV, D_FF // N_DEV, D_MODEL),
}


def _pair_sums(name, slabs, from_sibling=None):
    sums = []
    for mine, sibs in zip(slabs, from_sibling or carrier(name, Side("pair", slabs))):
        width = mine.shape[-1]
        rows = math.prod(mine.shape[1:-1])
        tm = 256 if rows % 256 == 0 else 128
        nb = rows // tm
        my_rows = Row(mine.reshape(-1, width), rowmap=lambda i, nb=nb: (2 * (i // nb) + lax.axis_index("c")) * nb + i % nb)
        out, = row_call("pair_sum", lambda q, p: (p.astype(F32) + q.astype(F32),),
                        [Row(sibs.reshape(-1, width)), my_rows], [], [(width, WIRE_DTYPE)], tm=tm)
        sums.append(out.reshape(sibs.shape))
    return sums


def _sum_adam(name, parts, w, m, v, layer, tm, earlier):
    C = w.shape[-1]
    rows = [Row(parts, lead=(s,)) for s in range(parts.shape[0])] + [Row(a, lead=(layer,)) for a in (w, m, v)]
    return row_call(name, f_sum_adam, rows, [], [(C, F32)] * 4, tm=tm, stack=(layer, DEPTH), stacked_onto=earlier)


def kernel(x, ln_in_g, ln_in_b, w_in, rel_bias, gla_w_lr, gla_b_lr, gla_norm_g, w_branch, w_out, ln1_g, ln1_b, w_up, w_down, ln2_g, ln2_b, loss_target, m_ln_in_g, m_ln_in_b, m_w_in, m_rel_bias, m_gla_w_lr, m_gla_b_lr, m_gla_norm_g, m_w_branch, m_w_out, m_ln1_g, m_ln1_b, m_w_up, m_w_down, m_ln2_g, m_ln2_b, v_ln_in_g, v_ln_in_b, v_w_in, v_rel_bias, v_gla_w_lr, v_gla_b_lr, v_gla_norm_g, v_w_branch, v_w_out, v_ln1_g, v_ln1_b, v_w_up, v_w_down, v_ln2_g, v_ln2_b):
    T = x.shape[1]
    x2d = x.reshape(T, D_MODEL)
    tgt = loss_target.reshape(T, D_MODEL)
    me = 4 * lax.axis_index("x") + 2 * lax.axis_index("y") + lax.axis_index("c")

    inv = ROPE_BASE ** (-jnp.arange(0, LIN_KEY_DIM, 2, dtype=F32) / LIN_KEY_DIM)
    ang = jnp.arange(T, dtype=F32)[:, None] * inv[None, :]
    cos = jnp.concatenate([jnp.cos(ang), jnp.cos(ang)], axis=1)
    sin = jnp.concatenate([-jnp.sin(ang), jnp.sin(ang)], axis=1)
    log_gamma = jnp.log1p(-jnp.exp2(-5.0 - jnp.arange(LIN_HEADS, dtype=F32)))
    ret_decay = jnp.repeat(log_gamma, LIN_KEY_DIM).reshape(1, LIN_HEADS * LIN_KEY_DIM)
    consts = (cos, sin, ret_decay)

    layers = [dict(blr=_row2(gla_b_lr[l]), gnorm=_row2(gla_norm_g[l]), ln1_g=_row2(ln1_g[l]),
                   ln1_b=_row2(ln1_b[l]), ln2_g=_row2(ln2_g[l]), ln2_b=_row2(ln2_b[l])) for l in range(DEPTH)]
    shards = dict(branch=w_branch, out=w_out, up=w_up, down=w_down)
    shards["in"] = w_in
    wire = [{n: shards[n][l].astype(WIRE_DTYPE) for n in SHARDED} for l in range(DEPTH)]

    def gather_of(l, names):
        def landed(res):
            for n, g in zip(names, res):
                layers[l][n] = _full_in(g) if n == "in" else _FULL[n](g)
        return lambda ctx: (Side("gather", [wire[l][n] for n in names]), landed)

    first = Side("gather", [wire[0]["in"]] + [gla_w_lr[l] for l in range(DEPTH)])
    bias_tabs = bias_table(rel_bias, side=first)
    layers[0]["in"] = _full_in(first.results[0])
    for l in range(DEPTH):
        layers[l]["wlr"] = _full_wlr(first.results[1 + l])
        layers[l]["bias_tab"] = (bias_tabs, l * ATTN_HEADS)

    fwd_hooks = [{"proj": gather_of(1, ("in",)), "attn_fwd": gather_of(0, ("up", "down")),
                  "ret_fwd": gather_of(0, ("branch", "out"))},
                 {"proj": gather_of(1, ("up", "down")), "attn_fwd": gather_of(1, ("branch", "out"))}]
    h, hb = row_call("ln_in", _twice(f_ln_in), [Row(x2d)], [_row2(ln_in_g), _row2(ln_in_b)], STREAM, tm=512)
    saved = []
    for l in range(DEPTH):
        h, hb, s = _layer_forward(h, hb, layers[l], consts, fwd_hooks[l])
        saved.append(s)
    dh, loss_lanes = row_call("loss", f_loss, [Row(h), Row(tgt)], [], [(D_MODEL, F32)], [(1, 128)], tm=512)

    received = [dict() for _ in range(DEPTH)]

    def scatter_of(l, names, sums=None):
        def make(ctx):
            parts = [sums[n] for n in names] if sums is not None else _pair_sums(
                "pair_grads", [_SLABS[n](ctx[n]) for n in names])

            def landed(res):
                received[l].update(zip(names, res))
            return Side("chips", parts), landed
        return make

    small = [None] * DEPTH
    dh, big, small[1] = _layer_backward(dh, saved[1], layers[1], consts, {})
    slabs1 = [_SLABS[n](big["g_in"] if n == "in" else big[n]) for n in SHARDED]
    sums1 = {}

    def pair_exchange_1(ctx):
        def landed(res):
            sums1.update(zip(SHARDED, _pair_sums("pair_grads", slabs1, from_sibling=res)))
        return Side("pair", slabs1), landed

    rest = SHARDED[1:]
    sums0 = {}

    def pair_exchange_0(ctx):
        slabs0 = [_SLABS[n](ctx[n]) for n in rest]

        def landed(res):
            sums0.update(zip(rest, _pair_sums("pair_grads", slabs0, from_sibling=res)))
        return Side("pair", slabs0), landed

    dh, big, small[0] = _layer_backward(dh, saved[0], layers[0], consts, {
        "mlp_down_bwd": pair_exchange_1, "gla_bwd": scatter_of(1, ("in",), sums1), "ret_bwd": pair_exchange_0,
        "attn_bwd": scatter_of(1, rest, sums1), "proj_wgrad": scatter_of(0, rest, sums0), "proj_bwd": scatter_of(0, ("in",))})
    grad_x, d_in_g, d_in_b = row_call("ln_in_bwd", vjp_fn(f_ln_in, 1, 1), [Row(x2d), Row(dh)],
                                      [_row2(ln_in_g), _row2(ln_in_b)], [(D_MODEL, F32)], [(1, D_MODEL), (1, D_MODEL)], tm=256)

    small_names = ("rel_bias", "wlr", "blr", "gnorm", "ln1_g", "ln1_b", "ln2_g", "ln2_b")
    parts = [d_in_g, d_in_b] + [small[l][n] for l in range(DEPTH) for n in small_names] + [loss_lanes]
    shapes = [(D_MODEL,), (D_MODEL,)] + [s for l in range(DEPTH) for s in (
        (ATTN_HEADS, N_REL), (GLA_RANK, 512), (512,), (LIN_VAL_DIM,), (D_MODEL,), (D_MODEL,), (D_MODEL,), (D_MODEL,))] + [(128,)]
    red = _unpack(all_reduce_small(_pack(parts)), shapes)
    loss = jnp.sum(red[-1])
    g_small = {"ln_in_g": red[0], "ln_in_b": red[1]}
    for l in range(DEPTH):
        for j, n in enumerate(small_names):
            g_small[(l, n)] = red[2 + l * len(small_names) + j]
    shard = LIN_HEADS * LIN_KEY_DIM // N_DEV
    g_wlr = jnp.stack([lax.dynamic_slice_in_dim(g_small[(l, "wlr")], me * shard, shard, axis=1) for l in range(DEPTH)])
    stack = lambda n: jnp.stack([g_small[(l, n)] for l in range(DEPTH)])
    g_rep = [g_small["ln_in_g"], g_small["ln_in_b"], stack("rel_bias"), g_wlr, stack("blr"), stack("gnorm"),
             stack("ln1_g"), stack("ln1_b"), stack("ln2_g"), stack("ln2_b")]
    w_rep = [ln_in_g, ln_in_b, rel_bias, gla_w_lr, gla_b_lr, gla_norm_g, ln1_g, ln1_b, ln2_g, ln2_b]
    m_rep = [m_ln_in_g, m_ln_in_b, m_rel_bias, m_gla_w_lr, m_gla_b_lr, m_gla_norm_g, m_ln1_g, m_ln1_b, m_ln2_g, m_ln2_b]
    v_rep = [v_ln_in_g, v_ln_in_b, v_rel_bias, v_gla_w_lr, v_gla_b_lr, v_gla_norm_g, v_ln1_g, v_ln1_b, v_ln2_g, v_ln2_b]
    rep_shapes = [w.shape for w in w_rep]
    d_pack, m_pack, v_pack = row_call("adam_small", f_adam, [Row(_pack(g_rep)), Row(_pack(w_rep)), Row(_pack(m_rep)), Row(_pack(v_rep))],
                                      [], [(128, F32)] * 3, tm=256)
    d_rep, nm_rep, nv_rep = _unpack(d_pack, rep_shapes), _unpack(m_pack, rep_shapes), _unpack(v_pack, rep_shapes)

    def owned(key, w, m, v, name, tm):
        C = w.shape[-1]
        flat = lambda a: a.reshape(DEPTH, -1, C)
        outs = None
        for l in range(DEPTH):
            outs = _sum_adam(name, received[l][key].reshape(4, -1, C), flat(w), flat(m), flat(v), l, tm, outs)
        return [o.reshape(w.shape) for o in outs]

    o_in = owned("in", w_in, m_w_in, v_w_in, "adam_w_in", 256)
    o_br = owned("branch", w_branch, m_w_branch, v_w_branch, "adam_w_branch", 128)
    o_out = owned("out", w_out, m_w_out, v_w_out, "adam_w_out", 128)
    o_up = owned("up", w_up, m_w_up, v_w_up, "adam_w_up", 256)
    o_down = owned("down", w_down, m_w_down, v_w_down, "adam_w_down", 256)

    def tree(j, rep):
        return [rep[0], rep[1], o_in[j], rep[2], rep[3], rep[4], rep[5], o_br[j], o_out[j], rep[6], rep[7], o_up[j], o_down[j],
                rep[8], rep[9]]

    return (loss, grad_x.reshape(1, T, D_MODEL), *tree(0, g_rep), *tree(1, d_rep), *tree(2, nm_rep), *tree(3, nv_rep))
```

```python
import functools
import math

import jax
import jax.numpy as jnp
from jax import lax
from jax.experimental import pallas as pl
from jax.experimental.pallas import tpu as pltpu

F32 = jnp.float32
MXU_DTYPE = jnp.bfloat16
WIRE_DTYPE = jnp.bfloat16
PROJ_DTYPE = jnp.bfloat16

N_DEV = 8
D_MODEL = 1024
DEPTH = 2
CHUNK = 64
ATTN_HEADS = 8
ATTN_HEAD_DIM = 128
ATTN_LEFT_CHUNKS = 8
BAND = (ATTN_LEFT_CHUNKS + 1) * CHUNK
PAD_ROWS = ATTN_LEFT_CHUNKS * CHUNK
REL_CLIP = 2 * CHUNK
N_REL = 2 * REL_CLIP + 1
REL_MIN = REL_CLIP - (CHUNK - 1)
LIN_HEADS = 4
LIN_KEY_DIM = 128
LIN_VAL_DIM = 256
GLA_RANK = 16
GLA_GATE_NORMALIZER = 16.0
ROPE_BASE = 10000.0
D_FF = 4 * D_MODEL
LN_EPS = 1e-5
NEG_INF = -1e30
ALPHA = (2 * DEPTH) ** 0.25
ADAM_LR, ADAM_B1, ADAM_B2, ADAM_EPS, ADAM_WD, ADAM_STEP = 0.001, 0.9, 0.999, 1e-08, 0.01, 10

PROJ_WIDTH = 12304
GLR_SRC = 9216
GATES_SRC = GLR_SRC + GLA_RANK
MAIN_W = 9216
GATES_W = 3 * D_MODEL
GATES_OFF = MAIN_W
GLR_OFF = MAIN_W + GATES_W
PROJ_PAD = 12800
GLR_BLOCK = 128
COL = dict(aq=0, ak=1024, av=2048, rq=3072, rk=3584, rv=4096, rg=5120, gq=6144, gk=6656, gv=7168, gg=8192)

SEG_CHUNKS = 8
SEG = SEG_CHUNKS * CHUNK
VMEM_BIG = 56 * 1024 * 1024

_NN = (((1,), (0,)), ((), ()))
_NT = (((1,), (1,)), ((), ()))
_TN = (((0,), (0,)), ((), ()))


def _mxu(a, b, dn):
    return lax.dot_general(a.astype(MXU_DTYPE), b.astype(MXU_DTYPE), dn, preferred_element_type=F32)


@jax.custom_vjp
def dot_nn(a, b):
    return _mxu(a, b, _NN)


def _dot_nn_fwd(a, b):
    return _mxu(a, b, _NN), (a, b)


def _dot_nn_bwd(res, g):
    a, b = res
    return _mxu(g, b, _NT).astype(a.dtype), _mxu(a, g, _TN).astype(b.dtype)


dot_nn.defvjp(_dot_nn_fwd, _dot_nn_bwd)


@jax.custom_vjp
def dot_nt(a, b):
    return _mxu(a, b, _NT)


def _dot_nt_fwd(a, b):
    return _mxu(a, b, _NT), (a, b)


def _dot_nt_bwd(res, g):
    a, b = res
    return _mxu(g, b, _NN).astype(a.dtype), _mxu(g, a, _TN).astype(b.dtype)


dot_nt.defvjp(_dot_nt_fwd, _dot_nt_bwd)


@jax.custom_vjp
def dot_tn(a, b):
    return _mxu(a, b, _TN)


def _dot_tn_fwd(a, b):
    return _mxu(a, b, _TN), (a, b)


def _dot_tn_bwd(res, g):
    a, b = res
    return _mxu(b, g, _NT).astype(a.dtype), _mxu(a, g, _NN).astype(b.dtype)


dot_tn.defvjp(_dot_tn_fwd, _dot_tn_bwd)


def _tri(n, lower):
    r = lax.broadcasted_iota(jnp.int32, (n, n), 0)
    c = lax.broadcasted_iota(jnp.int32, (n, n), 1)
    return (r >= c) if lower else (r <= c)


def _tri_dot(lower, x):
    tri = _tri(x.shape[0], lower).astype(jnp.bfloat16)
    hi = x.astype(jnp.bfloat16)
    rest = x - hi.astype(F32)
    mid = rest.astype(jnp.bfloat16)
    lo = (rest - mid.astype(F32)).astype(jnp.bfloat16)
    dot = lambda part: lax.dot_general(tri, part, _NN, preferred_element_type=F32)
    return dot(hi) + dot(mid) + dot(lo)


@jax.custom_vjp
def chunk_cumsum(x):
    return _tri_dot(True, x)


def _chunk_cumsum_fwd(x):
    return chunk_cumsum(x), None


def _chunk_cumsum_bwd(_, g):
    return (_tri_dot(False, g),)


chunk_cumsum.defvjp(_chunk_cumsum_fwd, _chunk_cumsum_bwd)


@jax.custom_vjp
def swap_halves(t):
    return pltpu.roll(t, t.shape[-1] // 2, t.ndim - 1)


def _swap_fwd(t):
    return swap_halves(t), None


def _swap_bwd(_, g):
    return (swap_halves(g),)


swap_halves.defvjp(_swap_fwd, _swap_bwd)


def mm(name, a, b, dims, *, tm, tn, tk, out_dtype=F32, extras=(), row_params=(), epi=None, n_out=1, out_dtypes=None,
       side=None, column_slabs=False):
    if dims == "nn":
        (M, K), N = a.shape, b.shape[1]
    elif dims == "nt":
        (M, K), N = a.shape, b.shape[0]
    else:
        (K, M), N = a.shape, b.shape[1]
    tm, tn, tk = min(tm, M), min(tn, N), min(tk, K)
    assert M % tm == 0 and N % tn == 0 and K % tk == 0, (name, M, N, K, tm, tn, tk)
    nk = K // tk
    dn = {"nn": _NN, "nt": _NT, "tn": _TN}[dims]
    a_spec = pl.BlockSpec((tk, tm), lambda i, j, k: (k, i)) if dims == "tn" else pl.BlockSpec((tm, tk), lambda i, j, k: (i, k))
    b_spec = pl.BlockSpec((tn, tk), lambda i, j, k: (j, k)) if dims == "nt" else pl.BlockSpec((tk, tn), lambda i, j, k: (k, j))
    o_spec = pl.BlockSpec((tm, tn), lambda i, j, k: (i, j))
    row_spec = pl.BlockSpec((1, tn), lambda i, j, k: (0, j))
    n_ex = len(extras) + len(row_params)
    out_dtypes = out_dtypes or (out_dtype,) * n_out

    def finish(r, ex_refs, o_refs):
        if epi is not None:
            r = epi(r, *[e[...] for e in ex_refs])
        r = r if isinstance(r, tuple) else (r,)
        for o, v in zip(o_refs, r):
            o[...] = v.astype(o.dtype)

    def body(*refs):
        a_ref, b_ref = refs[:2]
        ex_refs = refs[2:2 + n_ex]
        o_refs = refs[2 + n_ex:2 + n_ex + n_out]
        part = _mxu(a_ref[...], b_ref[...], dn)
        if nk == 1:
            finish(part, ex_refs, o_refs)
            return
        acc = refs[-1]
        k = pl.program_id(2)

        @pl.when(k == 0)
        def _():
            acc[...] = part

        @pl.when(k > 0)
        def _():
            acc[...] += part

        @pl.when(k == nk - 1)
        def _():
            finish(acc[...], ex_refs, o_refs)

    outs = _side_call(
        side, body,
        name=name,
        grid=(M // tm, N // tn, nk),
        in_specs=[a_spec, b_spec] + [o_spec] * len(extras) + [row_spec] * len(row_params),
        out_specs=[pl.BlockSpec((None, tm, tn), lambda i, j, k: (j, i, 0))] * n_out if column_slabs else [o_spec] * n_out,
        out_shape=[jax.ShapeDtypeStruct((N // tn, M, tn) if column_slabs else (M, N), dt) for dt in out_dtypes],
        scratch_shapes=[pltpu.VMEM((tm, tn), F32)] if nk > 1 else [],
        args=[a, b, *extras, *row_params],
        semantics=("parallel", "parallel", "arbitrary"),
    )
    return outs[0] if n_out == 1 else outs


class Row:
    def __init__(self, arr, width=None, col=0, lead=(), rowmap=None):
        self.arr, self.lead, self.col = arr, tuple(lead), col
        self.width = arr.shape[-1] if width is None else width
        self.rowmap = rowmap

    def spec(self, tm):
        lead, col, rowmap = self.lead, self.col, self.rowmap or (lambda i: i)
        return pl.BlockSpec((None,) * len(lead) + (tm, self.width), lambda i: (*lead, rowmap(i), col))


def row_call(name, fn, rows, params, new_outs, red_shapes=(), *, tm, into=None, into_cols=(), stack=None, stacked_onto=None):
    R = rows[0].arr.shape[len(rows[0].lead)]
    tm = min(tm, R)
    assert R % tm == 0, (name, R, tm)
    n_row, n_par, n_new, n_into, n_red = len(rows), len(params), len(new_outs), len(into_cols), len(red_shapes)
    fresh = isinstance(into, jax.ShapeDtypeStruct)

    def body(*refs):
        row_refs = refs[:n_row]
        par_refs = refs[n_row:n_row + n_par]
        pos = n_row + n_par + (1 if (into is not None and not fresh) else 0) + (n_new if stacked_onto is not None else 0)
        new_refs = refs[pos:pos + n_new]
        red_refs = refs[pos + n_new:pos + n_new + n_red]
        pos2 = pos + n_new + n_red
        into_ref = refs[pos2] if into is not None else None
        stage = refs[pos2 + 1:pos2 + 1 + n_into] if into is not None else ()
        i = pl.program_id(0)
        widen = lambda t: t.astype(F32) if jnp.issubdtype(t.dtype, jnp.floating) else t
        vals = fn(*[widen(r[...]) for r in row_refs], *[p[...] for p in par_refs])
        vals = vals if isinstance(vals, (tuple, list)) else (vals,)
        for o, v in zip(new_refs, vals[:n_new]):
            o[...] = v.astype(o.dtype)

        def window_copy(j, step):
            width, off = into_cols[j]
            rows_at = pl.ds(pl.multiple_of(step * tm, tm), tm)
            return pltpu.make_async_copy(stage[j], into_ref.at[rows_at, pl.ds(off, width)], refs[pos2 + 1 + n_into].at[j])

        for j, v in enumerate(vals[n_new:n_new + n_into]):
            @pl.when(i > 0)
            def _(j=j):
                window_copy(j, i - 1).wait()

            stage[j][...] = v.astype(stage[j].dtype)
            window_copy(j, i).start()

            @pl.when(i == R // tm - 1)
            def _(j=j):
                window_copy(j, i).wait()
        for o, v in zip(red_refs, vals[n_new + n_into:]):
            @pl.when(i == 0)
            def _(o=o, v=v):
                o[...] = v.astype(o.dtype)

            @pl.when(i > 0)
            def _(o=o, v=v):
                o[...] += v.astype(o.dtype)

    in_specs = [r.spec(tm) for r in rows] + [pl.BlockSpec(p.shape, lambda i, nd=p.ndim: (0,) * nd) for p in params]
    args = [r.arr for r in rows] + list(params)
    if stack is None:
        out_specs = [pl.BlockSpec((tm, w), lambda i: (i, 0)) for w, _ in new_outs]
        out_shape = [jax.ShapeDtypeStruct((R, w), dt) for w, dt in new_outs]
    else:
        index, depth = stack
        out_specs = [pl.BlockSpec((None, tm, w), lambda i: (index, i, 0)) for w, _ in new_outs]
        out_shape = [jax.ShapeDtypeStruct((depth, R, w), dt) for w, dt in new_outs]
    out_specs += [pl.BlockSpec(s, lambda i, nd=len(s): (0,) * nd) for s in red_shapes]
    out_shape += [jax.ShapeDtypeStruct(s, F32) for s in red_shapes]
    aliases, scratch = {}, []
    if into is not None:
        out_specs.append(pl.BlockSpec(memory_space=pl.ANY))
        out_shape.append(jax.ShapeDtypeStruct(into.shape, into.dtype))
        if not fresh:
            in_specs.append(pl.BlockSpec(memory_space=pl.ANY))
            args.append(into)
            aliases = {len(args) - 1: len(out_shape) - 1}
        scratch = [pltpu.VMEM((tm, w), into.dtype) for w, _ in into_cols] + [pltpu.SemaphoreType.DMA((n_into,))]
    if stacked_onto is not None:
        for j, prev in enumerate(stacked_onto):
            in_specs.append(pl.BlockSpec(memory_space=pl.ANY))
            args.append(prev)
            aliases[len(args) - 1] = j
    outs = pl.pallas_call(
        body,
        name=name,
        grid=(R // tm,),
        in_specs=in_specs,
        out_specs=out_specs,
        out_shape=out_shape,
        scratch_shapes=scratch,
        input_output_aliases=aliases,
        compiler_params=pltpu.CompilerParams(dimension_semantics=("arbitrary",), vmem_limit_bytes=VMEM_BIG),
    )(*args)
    return list(outs)


def vjp_fn(fn, n_row, n_out, passthrough=0):
    def bwd(*args):
        rows = [a.astype(F32) for a in args[:n_row]]
        cts = args[n_row:n_row + n_out]
        thru = args[n_row + n_out:n_row + n_out + passthrough]
        pars = args[n_row + n_out + passthrough:]
        outs, pull = jax.vjp(lambda *xs: tuple(fn(*xs)), *rows, *pars)
        grads = pull(tuple(c.astype(o.dtype) for c, o in zip(cts, outs)))
        return tuple(grads[:n_row]) + tuple(thru) + tuple(grads[n_row:])

    return bwd


def _ln(x, g, b):
    mu = jnp.mean(x, axis=-1, keepdims=True)
    xc = x - mu
    var = jnp.mean(xc * xc, axis=-1, keepdims=True)
    return xc * lax.rsqrt(var + LN_EPS) * g + b


def f_ln_in(x, g, b):
    return (_ln(x, g, b),)


def f_deepnorm(x, y, g, b):
    return (_ln(ALPHA * x + y, g, b),)


def _twice(fn):
    return lambda *args: fn(*args) * 2


STREAM = [(D_MODEL, F32), (D_MODEL, MXU_DTYPE)]


def _rope(t, cos, sin):
    parts = []
    for h in range(LIN_HEADS):
        th = t[:, h * LIN_KEY_DIM:(h + 1) * LIN_KEY_DIM]
        parts.append(th * cos + swap_halves(th) * sin)
    return jnp.concatenate(parts, axis=1)


def f_ret_pre(rq, rk, cos, sin):
    return _rope(rq, cos, sin), _rope(rk, cos, sin) * LIN_KEY_DIM ** -0.5


def _log_sigmoid(x):
    return jnp.minimum(x, 0.0) - jnp.log(1.0 + jnp.exp(-jnp.abs(x)))


def f_gla_pre(gq, glr, wlr, blr):
    gate_pre = dot_nn(glr, wlr) + blr
    return gq * LIN_KEY_DIM ** -0.5, _log_sigmoid(gate_pre) / GLA_GATE_NORMALIZER


def f_normgate(ret_raw, rg, gla_raw, gg, gnorm):
    rets, glas = [], []
    for h in range(LIN_HEADS):
        sl = slice(h * LIN_VAL_DIM, (h + 1) * LIN_VAL_DIM)
        t = ret_raw[:, sl]
        mu = jnp.mean(t, axis=-1, keepdims=True)
        tc = t - mu
        rets.append(tc * lax.rsqrt(jnp.mean(tc * tc, axis=-1, keepdims=True) + LN_EPS))
        u = gla_raw[:, sl]
        glas.append(u * lax.rsqrt(jnp.mean(u * u, axis=-1, keepdims=True) + LN_EPS) * gnorm)
    ret = jax.nn.silu(rg) * jnp.concatenate(rets, axis=1)
    gla = jax.nn.silu(gg) * jnp.concatenate(glas, axis=1)
    return ret, gla


def f_merge(p0, p1, p2, gl):
    out = jax.nn.sigmoid(gl[:, :D_MODEL]) * p0
    out = out + jax.nn.sigmoid(gl[:, D_MODEL:2 * D_MODEL]) * p1
    return (out + jax.nn.sigmoid(gl[:, 2 * D_MODEL:]) * p2,)


def f_loss(y, tgt):
    diff = y - tgt
    sq = jnp.sum(diff * diff, axis=0, keepdims=True)
    lanes = sq[:, :128]
    for j in range(1, D_MODEL // 128):
        lanes = lanes + sq[:, j * 128:(j + 1) * 128]
    return diff * (1.0 / D_MODEL), lanes * (0.5 / D_MODEL)


def _adam(g, w, m, v):
    m = ADAM_B1 * m + (1.0 - ADAM_B1) * g
    v = ADAM_B2 * v + (1.0 - ADAM_B2) * jnp.square(g)
    m_hat = m / (1.0 - ADAM_B1 ** ADAM_STEP)
    v_hat = v / (1.0 - ADAM_B2 ** ADAM_STEP)
    delta = -ADAM_LR * (m_hat / (jnp.sqrt(v_hat) + ADAM_EPS) + ADAM_WD * w)
    return delta, m, v


def f_sum_adam(*args):
    g = args[0].astype(F32)
    for p in args[1:-3]:
        g = g + p.astype(F32)
    return (g,) + _adam(g, *args[-3:])


def f_adam(g, w, m, v):
    return _adam(g, w, m, v)


GROUP = 4
GROUP_ROWS = GROUP * CHUNK
GROUP_KEYS = (GROUP + ATTN_LEFT_CHUNKS) * CHUNK
DIAG_ROWS = REL_CLIP + CHUNK
GRAD_ROWS = 200


def _blocks_by_delta():
    blocks = {}
    for qi in range(GROUP):
        for J in range(GROUP_KEYS // 128):
            blocks.setdefault(qi - 2 * J, []).append((qi, J))
    return blocks


def bias_table(rel_bias, side=None):
    n_tab = rel_bias.shape[0] * ATTN_HEADS

    def body(rb_ref, o_ref):
        h = pl.program_id(0)
        rm = lax.broadcasted_iota(jnp.int32, (CHUNK, 128), 0)
        cm = lax.broadcasted_iota(jnp.int32, (CHUNK, 128), 1)
        ahead = jnp.where(cm >= CHUNK, 1, 0)
        for delta, where in _blocks_by_delta().items():
            base = 64 * delta + PAD_ROWS
            rel = jnp.minimum(base + rm - cm, REL_CLIP) + REL_CLIP
            in_band = (ahead - delta >= 0) & (ahead - delta <= ATTN_LEFT_CHUNKS)
            lo = max(REL_MIN, min(base - 127, REL_CLIP) + REL_CLIP)
            hi = min(base + CHUNK - 1, REL_CLIP) + REL_CLIP

            def step(r, acc, rel=rel):
                return jnp.where(rel == r, rb_ref[h * N_REL + r], acc)

            blk = jnp.zeros((CHUNK, 128), F32) if hi < lo else lax.fori_loop(lo, hi + 1, step, jnp.zeros((CHUNK, 128), F32))
            blk = jnp.where(in_band, blk, NEG_INF)
            for qi, J in where:
                o_ref[pl.ds(qi * CHUNK, CHUNK), pl.ds(J * 128, 128)] = blk

    return _side_call(
        side, body,
        name="bias_table",
        grid=(n_tab,),
        in_specs=[pl.BlockSpec(memory_space=pltpu.SMEM)],
        out_specs=[pl.BlockSpec((None, GROUP_ROWS, GROUP_KEYS), lambda h: (h, 0, 0))],
        out_shape=[jax.ShapeDtypeStruct((n_tab, GROUP_ROWS, GROUP_KEYS), F32)],
        scratch_shapes=[],
        args=[rel_bias.reshape(n_tab * N_REL)],
    )[0]


def bias_table_grad(dbias):
    blocks = _blocks_by_delta()
    below_clip = range(-(CHUNK - 1), REL_CLIP)
    deltas = [d for d in sorted(blocks) if any(-127 <= pos - 64 * d - PAD_ROWS <= CHUNK - 1 for pos in below_clip)]

    def body(d_ref, o_ref, diag):
        sums = []
        for delta in deltas:
            acc = None
            for qi, J in blocks[delta]:
                blk = d_ref[pl.ds(qi * CHUNK, CHUNK), pl.ds(J * 128, 128)]
                acc = blk if acc is None else acc + blk
            sums.append(acc)
        off = lax.broadcasted_iota(jnp.int32, (CHUNK, 128), 0) - lax.broadcasted_iota(jnp.int32, (CHUNK, 128), 1)
        diag[...] = jnp.zeros_like(diag)

        def step(t, carry):
            pos = t - (CHUNK - 1)
            picked = jnp.zeros((CHUNK, 128), F32)
            for delta, s in zip(deltas, sums):
                picked = picked + jnp.where(off == pos - 64 * delta - PAD_ROWS, s, 0.0)
            diag[pl.ds(t, 1), :] = jnp.sum(picked, axis=0, keepdims=True)
            return carry

        lax.fori_loop(0, DIAG_ROWS, step, 0, unroll=8)
        d = d_ref[...]
        total = jnp.sum(jnp.sum(d, axis=1, keepdims=True), axis=0, keepdims=True)
        o_ref[pl.ds(0, DIAG_ROWS), :] = jnp.broadcast_to(jnp.sum(diag[...], axis=1, keepdims=True), (DIAG_ROWS, 128))
        o_ref[pl.ds(DIAG_ROWS, GRAD_ROWS - DIAG_ROWS), :] = jnp.broadcast_to(total, (GRAD_ROWS - DIAG_ROWS, 128))

    out = pl.pallas_call(
        body,
        name="bias_table_grad",
        grid=(ATTN_HEADS,),
        in_specs=[pl.BlockSpec((None, GROUP_ROWS, GROUP_KEYS), lambda h: (h, 0, 0))],
        out_specs=pl.BlockSpec((None, GRAD_ROWS, 128), lambda h: (h, 0, 0)),
        out_shape=jax.ShapeDtypeStruct((ATTN_HEADS, GRAD_ROWS, 128), F32),
        scratch_shapes=[pltpu.VMEM((DIAG_ROWS, 128), F32)],
        compiler_params=pltpu.CompilerParams(dimension_semantics=("arbitrary",)),
    )(dbias)
    below = out[:, :DIAG_ROWS - 1, 0]
    clipped = out[:, DIAG_ROWS, 0] - jnp.sum(below, axis=1)
    return jnp.concatenate([jnp.zeros((ATTN_HEADS, REL_MIN), F32), below, clipped[:, None]], axis=1)


def _attn_stage(q_ref, k_ref, v_ref, qs, kp, vp, T):
    qs[...] = (q_ref[...] * ATTN_HEAD_DIM ** -0.5).astype(qs.dtype)
    for dst, src in ((kp, k_ref), (vp, v_ref)):
        dst[pl.ds(0, PAD_ROWS), :] = jnp.zeros((PAD_ROWS, ATTN_HEAD_DIM), dst.dtype)
        dst[pl.ds(PAD_ROWS, T), :] = src[...].astype(dst.dtype)


STRIP = 256


def _strip_scores(g, t, first_groups, r0, qc, kp, b_ref):
    keys = pl.ds(r0 + t * STRIP if first_groups else pl.multiple_of(r0 + t * STRIP, STRIP), STRIP)
    s = _mxu(qc, kp[keys, :], _NT) + b_ref[:, t * STRIP:(t + 1) * STRIP]
    if first_groups:
        k_chunk = lax.shift_right_logical(lax.broadcasted_iota(jnp.int32, (GROUP_ROWS, STRIP), 1), 6)
        s = jnp.where(g * GROUP + t * (STRIP // CHUNK) + k_chunk >= ATTN_LEFT_CHUNKS, s, NEG_INF)
    return s, keys


def _for_groups(T, step):
    n_first = min(ATTN_LEFT_CHUNKS // GROUP, T // GROUP_ROWS)
    for g in range(n_first):
        step(g, True)
    n_loop = T // GROUP_ROWS - n_first
    unroll = next(u for u in (14, 7, 4, 2, 1) if n_loop % u == 0)
    lax.fori_loop(n_first, T // GROUP_ROWS, lambda g, c: (step(g, False), c)[1], 0, unroll=unroll)


def _head_spec(T, base):
    return pl.BlockSpec((T, ATTN_HEAD_DIM), lambda h: (0, base + h))


def _attn_scratch(T):
    return [pltpu.VMEM((T, ATTN_HEAD_DIM), MXU_DTYPE)] + [pltpu.VMEM((T + PAD_ROWS, ATTN_HEAD_DIM), MXU_DTYPE)] * 2


def attn_fwd(proj, bias_tab, first_tab, side=None):
    T = proj.shape[0]

    def body(q_ref, k_ref, v_ref, b_ref, o_ref, lse_ref, qs, kp, vp):
        _attn_stage(q_ref, k_ref, v_ref, qs, kp, vp, T)

        def step(g, first):
            r0 = g * GROUP_ROWS if first else pl.multiple_of(g * GROUP_ROWS, GROUP_ROWS)
            rows = pl.ds(r0, GROUP_ROWS)
            qc = qs[rows, :]
            m = jnp.full((GROUP_ROWS, 1), NEG_INF, F32)
            l = jnp.zeros((GROUP_ROWS, 1), F32)
            acc = jnp.zeros((GROUP_ROWS, ATTN_HEAD_DIM), F32)
            for t in range(GROUP_KEYS // STRIP):
                s, keys = _strip_scores(g, t, first, r0, qc, kp, b_ref)
                m_new = jnp.maximum(m, jnp.max(s, axis=-1, keepdims=True))
                keep = jnp.exp(m - m_new)
                e = jnp.exp(s - m_new)
                l = l * keep + jnp.sum(e, axis=-1, keepdims=True)
                acc = acc * keep + _mxu(e, vp[keys, :], _NN)
                m = m_new
            o_ref[rows, :] = (acc * (1.0 / l)).astype(o_ref.dtype)
            lse_ref[rows, :] = jnp.broadcast_to(m + jnp.log(l), (GROUP_ROWS, ATTN_HEAD_DIM))

        _for_groups(T, step)

    out, lse = _side_call(
        side, body,
        name="attn_fwd",
        grid=(ATTN_HEADS,),
        in_specs=[_head_spec(T, COL["aq"] // 128), _head_spec(T, COL["ak"] // 128), _head_spec(T, COL["av"] // 128),
                  pl.BlockSpec((None, GROUP_ROWS, GROUP_KEYS), lambda h: (first_tab + h, 0, 0))],
        out_specs=[pl.BlockSpec((T, ATTN_HEAD_DIM), lambda h: (0, h))] * 2,
        out_shape=[jax.ShapeDtypeStruct((T, ATTN_HEADS * ATTN_HEAD_DIM), MXU_DTYPE),
                   jax.ShapeDtypeStruct((T, ATTN_HEADS * ATTN_HEAD_DIM), F32)],
        scratch_shapes=_attn_scratch(T),
        args=[proj, proj, proj, bias_tab],
    )
    return out, lse


def attn_bwd(proj, bias_tab, first_tab, out, lse, dout, dproj, side=None):
    T = proj.shape[0]

    def body(q_ref, k_ref, v_ref, b_ref, do_ref, o_ref, lse_ref, dproj_in, dbias_ref, dproj_ref, qs, kp, vp, dkp, dvp, dq, stage,
             stage_sems):
        del dproj_in
        h = pl.program_id(0)
        _attn_stage(q_ref, k_ref, v_ref, qs, kp, vp, T)
        dkp[...] = jnp.zeros_like(dkp)
        dvp[...] = jnp.zeros_like(dvp)
        dbias_ref[...] = jnp.zeros_like(dbias_ref)

        def step(g, first):
            r0 = g * GROUP_ROWS if first else pl.multiple_of(g * GROUP_ROWS, GROUP_ROWS)
            rows = pl.ds(r0, GROUP_ROWS)
            qc, do = qs[rows, :], do_ref[rows, :]
            lse = jnp.concatenate([lse_ref[rows, :]] * (STRIP // ATTN_HEAD_DIM), axis=1)
            row_term = jnp.sum(do.astype(F32) * o_ref[rows, :].astype(F32), axis=-1, keepdims=True)
            dq_acc = jnp.zeros((GROUP_ROWS, ATTN_HEAD_DIM), F32)
            for t in range(GROUP_KEYS // STRIP):
                s, keys = _strip_scores(g, t, first, r0, qc, kp, b_ref)
                cols = slice(t * STRIP, (t + 1) * STRIP)
                kb, vb = kp[keys, :], vp[keys, :]
                p = jnp.exp(s - lse)
                dvp[keys, :] += _mxu(p, do, _TN)
                ds = p * (_mxu(do, vb, _NT) - row_term)
                dq_acc = dq_acc + _mxu(ds, kb, _NN)
                dkp[keys, :] += _mxu(ds, qc, _TN)
                dbias_ref[:, cols] += ds
            dq[rows, :] = dq_acc * ATTN_HEAD_DIM ** -0.5

        _for_groups(T, step)
        col = pl.multiple_of(h * ATTN_HEAD_DIM, ATTN_HEAD_DIM)
        for n, (grad, first_row, base) in enumerate(((dq, 0, "aq"), (dkp, PAD_ROWS, "ak"), (dvp, PAD_ROWS, "av"))):
            copy = pltpu.make_async_copy(stage.at[n], dproj_ref.at[:, pl.ds(COL[base] + col, ATTN_HEAD_DIM)], stage_sems.at[n])

            @pl.when(h > 0)
            def _(copy=copy):
                copy.wait()

            stage[n] = grad[pl.ds(first_row, T), :].astype(stage.dtype)
            copy.start()

            @pl.when(h == ATTN_HEADS - 1)
            def _(copy=copy):
                copy.wait()

    dbias, dproj = _side_call(
        side, body,
        name="attn_bwd",
        grid=(ATTN_HEADS,),
        in_specs=[_head_spec(T, COL["aq"] // 128), _head_spec(T, COL["ak"] // 128), _head_spec(T, COL["av"] // 128),
                  pl.BlockSpec((None, GROUP_ROWS, GROUP_KEYS), lambda h: (first_tab + h, 0, 0))]
        + [pl.BlockSpec((T, ATTN_HEAD_DIM), lambda h: (0, h))] * 3 + [pl.BlockSpec(memory_space=pl.ANY)],
        out_specs=[pl.BlockSpec((None, GROUP_ROWS, GROUP_KEYS), lambda h: (h, 0, 0)), pl.BlockSpec(memory_space=pl.ANY)],
        out_shape=[jax.ShapeDtypeStruct((ATTN_HEADS, GROUP_ROWS, GROUP_KEYS), F32),
                   jax.ShapeDtypeStruct(dproj.shape, dproj.dtype)],
        scratch_shapes=_attn_scratch(T) + [pltpu.VMEM((T + PAD_ROWS, ATTN_HEAD_DIM), F32)] * 2
        + [pltpu.VMEM((T, ATTN_HEAD_DIM), F32), pltpu.VMEM((3, T, ATTN_HEAD_DIM), dproj.dtype), pltpu.SemaphoreType.DMA((3,))],
        args=[proj, proj, proj, bias_tab, dout, out, lse, dproj],
        aliases={7: 1},
    )
    return dproj, dbias


def _lin_segment(q, k, v, la, st):
    lower = _tri(CHUNK, True)
    steps = (lax.broadcasted_iota(jnp.int32, (CHUNK, 1), 0) + 1).astype(F32)
    outs = []
    for i in range(SEG_CHUNKS):
        sl = slice(i * CHUNK, (i + 1) * CHUNK)
        qi, ki, vi = q[sl], k[sl], v[sl]
        if la.shape[0] == 1:
            cum, last = la * steps, la * float(CHUNK)
        else:
            cum, last = chunk_cumsum(la[sl]), jnp.sum(la[sl], axis=0, keepdims=True)
        e_pos, e_neg = jnp.exp(cum), jnp.exp(-cum)
        q_fwd = qi * e_pos
        s_fwd = dot_nt(q_fwd, ki * e_neg)
        s_bwd = dot_nt(qi * e_neg, ki * e_pos)
        o = dot_nn(jnp.where(lower, s_fwd, s_bwd), vi) + dot_nt(q_fwd, st)
        st = jnp.exp(last) * st + dot_tn(vi, ki * jnp.exp(last - cum))
        outs.append(o)
    return jnp.concatenate(outs, axis=0), st


class Win:
    def __init__(self, arr, width, base=0):
        self.arr, self.width, self.base = arr, width, base


HEADS_PER_STEP = 2


def _lin_spec(w, seg_of):
    assert w.base % HEADS_PER_STEP == 0
    first, width = w.base // HEADS_PER_STEP, HEADS_PER_STEP * w.width
    if w.arr.shape[0] == 1:
        return pl.BlockSpec((1, width), lambda p, s: (0, first + p))
    return pl.BlockSpec((SEG, width), lambda p, s: (seg_of(s), first + p))


def _head_cols(ref, j, width):
    return ref[:, j * width:(j + 1) * width]


def _decay_rows(la_ref, j):
    return _head_cols(la_ref, j, LIN_KEY_DIM)


_STATE_BLOCK = (HEADS_PER_STEP, None, LIN_VAL_DIM, LIN_KEY_DIM)


def lin_fwd(name, q, k, v, la, side=None):
    T = q.arr.shape[0]
    nseg = T // SEG

    def body(q_ref, k_ref, v_ref, la_ref, o_ref, st_ref, st):
        @pl.when(pl.program_id(1) == 0)
        def _():
            st[...] = jnp.zeros_like(st)

        for j in range(HEADS_PER_STEP):
            st0 = st[j]
            st_ref[j] = st0
            o, st1 = _lin_segment(_head_cols(q_ref, j, LIN_KEY_DIM), _head_cols(k_ref, j, LIN_KEY_DIM),
                                  _head_cols(v_ref, j, LIN_VAL_DIM), _decay_rows(la_ref, j), st0)
            o_ref[:, j * LIN_VAL_DIM:(j + 1) * LIN_VAL_DIM] = o
            st[j] = st1

    fwd = lambda s: s
    out, states = _side_call(
        side, body,
        name=name,
        grid=(LIN_HEADS // HEADS_PER_STEP, nseg),
        in_specs=[_lin_spec(w, fwd) for w in (q, k, v, la)],
        out_specs=[pl.BlockSpec((SEG, HEADS_PER_STEP * LIN_VAL_DIM), lambda p, s: (s, p)),
                   pl.BlockSpec(_STATE_BLOCK, lambda p, s: (p, s, 0, 0))],
        out_shape=[jax.ShapeDtypeStruct((T, LIN_HEADS * LIN_VAL_DIM), F32),
                   jax.ShapeDtypeStruct((LIN_HEADS, nseg, LIN_VAL_DIM, LIN_KEY_DIM), F32)],
        scratch_shapes=[pltpu.VMEM((HEADS_PER_STEP, LIN_VAL_DIM, LIN_KEY_DIM), F32)],
        args=[q.arr, k.arr, v.arr, la.arr],
    )
    return out, states


def lin_bwd(name, q, k, v, la, states, dout, with_decay_grad, side=None, into=None, windows=None):
    T = q.arr.shape[0]
    nseg = T // SEG
    n_out = 4 if with_decay_grad else 3

    widths = [LIN_KEY_DIM, LIN_KEY_DIM, LIN_VAL_DIM] + ([LIN_KEY_DIM] if with_decay_grad else [])
    windows = dict(windows or {})
    fresh = [i for i in range(n_out) if i not in windows]
    placed = sorted(windows)
    n_pairs = LIN_HEADS // HEADS_PER_STEP
    rev = lambda s: nseg - 1 - s

    def body(q_ref, k_ref, v_ref, la_ref, st_ref, do_ref, *rest):
        rest = rest[1:] if placed else rest
        outs = dict(zip(fresh, rest[:len(fresh)]))
        rest = rest[len(fresh):]
        into_ref, rest = (rest[0], rest[1:]) if placed else (None, rest)
        dst, stages = rest[0], dict(zip(placed, rest[1:1 + len(placed)]))
        p, s = pl.program_id(0), pl.program_id(1)

        @pl.when(s == 0)
        def _():
            dst[...] = jnp.zeros_like(dst)

        def window_copy(n, i):
            width = HEADS_PER_STEP * widths[i]
            rows = pl.ds(pl.multiple_of(rev(s) * SEG, SEG), SEG)
            cols = pl.ds(pl.multiple_of(windows[i] + p * width, width), width)
            return pltpu.make_async_copy(stages[i], into_ref.at[rows, cols], rest[1 + len(placed)].at[n])

        for n, i in enumerate(placed):
            @pl.when((p > 0) | (s > 0))
            def _(n=n, i=i):
                window_copy(n, i).wait()

        for j in range(HEADS_PER_STEP):
            _, pull = jax.vjp(_lin_segment, _head_cols(q_ref, j, LIN_KEY_DIM).astype(F32),
                              _head_cols(k_ref, j, LIN_KEY_DIM).astype(F32), _head_cols(v_ref, j, LIN_VAL_DIM).astype(F32),
                              _decay_rows(la_ref, j), st_ref[j])
            grads = pull((_head_cols(do_ref, j, LIN_VAL_DIM), dst[j]))
            for i in range(n_out):
                dest = stages[i] if i in windows else outs[i]
                dest[:, j * widths[i]:(j + 1) * widths[i]] = grads[i].astype(dest.dtype)
            dst[j] = grads[4]

        for n, i in enumerate(placed):
            window_copy(n, i).start()

            @pl.when((p == n_pairs - 1) & (s == nseg - 1))
            def _(n=n, i=i):
                window_copy(n, i).wait()

    key_spec = pl.BlockSpec((SEG, HEADS_PER_STEP * LIN_KEY_DIM), lambda p, s: (rev(s), p))
    val_spec = pl.BlockSpec((SEG, HEADS_PER_STEP * LIN_VAL_DIM), lambda p, s: (rev(s), p))
    all_specs = [key_spec, key_spec, val_spec] + ([key_spec] if with_decay_grad else [])
    any_spec = pl.BlockSpec(memory_space=pl.ANY)
    outs = _side_call(
        side, body,
        name=name,
        grid=(n_pairs, nseg),
        in_specs=[_lin_spec(w, rev) for w in (q, k, v, la)] + [
            pl.BlockSpec(_STATE_BLOCK, lambda p, s: (p, rev(s), 0, 0)), val_spec] + ([any_spec] if placed else []),
        out_specs=[all_specs[i] for i in fresh] + ([any_spec] if placed else []),
        out_shape=[jax.ShapeDtypeStruct((T, LIN_HEADS * widths[i]), F32) for i in fresh]
        + ([jax.ShapeDtypeStruct(into.shape, into.dtype)] if placed else []),
        scratch_shapes=[pltpu.VMEM((HEADS_PER_STEP, LIN_VAL_DIM, LIN_KEY_DIM), F32)]
        + [pltpu.VMEM((SEG, HEADS_PER_STEP * widths[i]), into.dtype) for i in placed]
        + ([pltpu.SemaphoreType.DMA((len(placed),))] if placed else []),
        args=[q.arr, k.arr, v.arr, la.arr, states, dout] + ([into] if placed else []),
        aliases={6: len(fresh)} if placed else None,
    )
    return outs


def _coords():
    return lax.axis_index("x"), lax.axis_index("y"), lax.axis_index("c")


def _peer(k):
    x, y, c = _coords()
    px = 1 - x if k & 4 else x
    py = 1 - y if k & 2 else y
    pc = 1 - c if k & 1 else c
    return (px, py, pc), 4 * px + 2 * py + pc


def _remote(src, dst, send_sem, recv_sem, device):
    return pltpu.make_async_remote_copy(src_ref=src, dst_ref=dst, send_sem=send_sem, recv_sem=recv_sem,
                                        device_id=device, device_id_type=pl.DeviceIdType.MESH)


class Side:
    def __init__(self, kind, arrays):
        self.kind, self.arrays, self.n = kind, list(arrays), len(arrays)
        self.outs_per = 1
        self.slots = {"gather": N_DEV - 1, "pair": 4, "chips": 3}[kind]

    def out_shape(self):
        if self.kind == "gather":
            return [jax.ShapeDtypeStruct((N_DEV,) + a.shape, a.dtype) for a in self.arrays]
        return [jax.ShapeDtypeStruct((4,) + a.shape[1:], a.dtype) for a in self.arrays]

    def sems(self):
        return [pltpu.SemaphoreType.DMA((self.n, self.slots)), pltpu.SemaphoreType.DMA((self.n, self.slots)),
                pltpu.SemaphoreType.DMA((self.n,))]

    def start(self, ins, outs, sems):
        send, recv, own = sems
        x, y, c = _coords()
        me = 4 * x + 2 * y + c
        for a in range(self.n):
            if self.kind == "gather":
                pltpu.make_async_copy(ins[a], outs[a].at[me], own.at[a]).start()
                for k in (1, 2, 4, 6):
                    peer, _ = _peer(k)
                    _remote(ins[a], outs[a].at[me], send.at[a, k - 1], recv.at[a, k - 1], peer).start()
            elif self.kind == "pair":
                for j in range(4):
                    _remote(ins[a].at[2 * j + 1 - c], outs[a].at[j], send.at[a, j], recv.at[a, j], (x, y, 1 - c)).start()
            else:
                chip = 2 * x + y
                pltpu.make_async_copy(ins[a].at[chip], outs[a].at[chip], own.at[a]).start()
                for d in (1, 2, 3):
                    tx, ty = (1 - x if d & 2 else x), (1 - y if d & 1 else y)
                    _remote(ins[a].at[2 * tx + ty], outs[a].at[chip], send.at[a, d - 1], recv.at[a, d - 1], (tx, ty, c)).start()

    def finish(self, ins, outs, sems):
        send, recv, own = sems
        x, y, c = _coords()
        me = 4 * x + 2 * y + c
        sibling = (x, y, 1 - c)
        for a in range(self.n):
            if self.kind == "gather":
                blk = outs[a].at[me]
                for k in (2, 4, 6):
                    _, pidx = _peer(k)
                    _remote(blk, blk, send.at[a, k - 1], recv.at[a, k - 1], sibling).wait_recv()
                    _remote(outs[a].at[pidx], outs[a].at[pidx], send.at[a, k], recv.at[a, k], sibling).start()
                pltpu.make_async_copy(ins[a], blk, own.at[a]).wait()
                for k in range(1, N_DEV):
                    _remote(blk, blk, send.at[a, k - 1], recv.at[a, k - 1], sibling).wait_send()
                for k in (1, 3, 5, 7):
                    _remote(blk, blk, send.at[a, k - 1], recv.at[a, k - 1], sibling).wait_recv()
            elif self.kind == "pair":
                for j in range(4):
                    _remote(ins[a].at[j], outs[a].at[j], send.at[a, j], recv.at[a, j], sibling).wait()
            else:
                slab = outs[a].at[0]
                pltpu.make_async_copy(slab, slab, own.at[a]).wait()
                for d in (1, 2, 3):
                    _remote(slab, slab, send.at[a, d - 1], recv.at[a, d - 1], sibling).wait()


def _side_call(side, body, *, name, grid, in_specs, out_specs, out_shape, scratch_shapes, args, semantics=None,
               aliases=None):
    in_specs, out_specs, out_shape = list(in_specs), list(out_specs), list(out_shape)
    scratch_shapes, args = list(scratch_shapes), list(args)
    n_in, n_out = len(in_specs), len(out_specs)
    kernel_body = body
    if side is not None:
        n_so = side.n * side.outs_per

        def kernel_body(*refs):
            ins, side_ins = refs[:n_in], refs[n_in:n_in + side.n]
            pos = n_in + side.n
            outs, side_outs = refs[pos:pos + n_out], refs[pos + n_out:pos + n_out + n_so]
            rest = refs[pos + n_out + n_so:]
            scratch, sems = rest[:-3], rest[-3:]
            ids = [pl.program_id(ax) for ax in range(len(grid))]
            first = functools.reduce(jnp.logical_and, [i == 0 for i in ids])
            last = functools.reduce(jnp.logical_and, [i == g - 1 for i, g in zip(ids, grid)])

            @pl.when(first)
            def _():
                side.start(side_ins, side_outs, sems)

            body(*ins, *outs, *scratch)

            @pl.when(last)
            def _():
                side.finish(side_ins, side_outs, sems)

        any_spec = pl.BlockSpec(memory_space=pl.ANY)
        in_specs += [any_spec] * side.n
        args += side.arrays
        out_specs += [any_spec] * n_so
        out_shape += side.out_shape()
        scratch_shapes += side.sems()
        semantics = ("arbitrary",) * len(grid)
    outs = pl.pallas_call(
        kernel_body,
        name=name,
        grid=grid,
        in_specs=in_specs,
        out_specs=out_specs,
        out_shape=out_shape,
        scratch_shapes=scratch_shapes,
        input_output_aliases=aliases or {},
        compiler_params=pltpu.CompilerParams(dimension_semantics=semantics or ("arbitrary",) * len(grid),
                                             vmem_limit_bytes=VMEM_BIG),
    )(*args)
    if side is not None:
        side.results = list(outs[n_out:])
    return list(outs[:n_out])


def carrier(name, side):
    _side_call(side, lambda: None, name=name, grid=(1,), in_specs=[], out_specs=[], out_shape=[], scratch_shapes=[], args=[])
    return side.results


def all_reduce_small(vec):
    def body(v_ref, o_ref, land, send_sems, recv_sems):
        x, y, c = _coords()
        me = 4 * x + 2 * y + c
        land[me] = v_ref[...]
        copies = []
        for k in range(1, N_DEV):
            peer, _ = _peer(k)
            cp = pltpu.make_async_remote_copy(
                src_ref=v_ref, dst_ref=land.at[me], send_sem=send_sems.at[k - 1], recv_sem=recv_sems.at[k - 1],
                device_id=peer, device_id_type=pl.DeviceIdType.MESH)
            cp.start()
            copies.append(cp)
        for cp in copies:
            cp.wait()
        total = land[0]
        for s in range(1, N_DEV):
            total = total + land[s]
        o_ref[...] = total

    return pl.pallas_call(
        body,
        name="all_reduce_small",
        in_specs=[pl.BlockSpec(memory_space=pltpu.VMEM)],
        out_specs=pl.BlockSpec(memory_space=pltpu.VMEM),
        out_shape=jax.ShapeDtypeStruct(vec.shape, F32),
        scratch_shapes=[pltpu.VMEM((N_DEV,) + vec.shape, F32), pltpu.SemaphoreType.DMA((N_DEV - 1,)),
                        pltpu.SemaphoreType.DMA((N_DEV - 1,))],
    )(vec)


def _pack(parts):
    flat = jnp.concatenate([p.reshape(-1).astype(F32) for p in parts])
    n = flat.shape[0]
    rows = -(-n // 1024) * 8
    return jnp.pad(flat, (0, rows * 128 - n)).reshape(rows, 128)


def _unpack(packed, shapes):
    flat, out, pos = packed.reshape(-1), [], 0
    for s in shapes:
        n = math.prod(s)
        out.append(flat[pos:pos + n].reshape(s))
        pos += n
    return out


def _row2(v):
    return v.reshape(1, -1)


def _rider(hooks, name, ctx=None):
    if name not in hooks:
        return None, lambda: None
    side, done = hooks[name](ctx)
    return side, lambda: done(side.results)


def _layer_forward(h, hb, W, consts, hooks):
    cos, sin, ret_decay = consts
    side, landed = _rider(hooks, "proj")
    proj = mm("proj", hb, W["in"], "nn", tm=4096, tn=512, tk=1024, side=side, out_dtype=PROJ_DTYPE)
    landed()
    bias_tab, first_tab = W["bias_tab"]
    side, landed = _rider(hooks, "attn_fwd")
    attn, attn_lse = attn_fwd(proj, bias_tab, first_tab, side=side)
    landed()
    rq2, rk2 = row_call("ret_pre", f_ret_pre, [Row(proj, 512, COL["rq"] // 512), Row(proj, 512, COL["rk"] // 512),
                                              Row(cos), Row(sin)], [], [(512, F32), (512, F32)], tm=512)
    side, landed = _rider(hooks, "ret_fwd")
    ret_raw, ret_st = lin_fwd("ret_fwd", Win(rq2, 128), Win(rk2, 128), Win(proj, 256, COL["rv"] // 256), Win(ret_decay, 128),
                              side=side)
    landed()
    gq2, gla_la = row_call("gla_pre", f_gla_pre, [Row(proj, 512, COL["gq"] // 512), Row(proj, GLR_BLOCK, GLR_OFF // GLR_BLOCK)],
                           [W["wlr"], W["blr"]], [(512, F32), (512, F32)], tm=512)
    side, landed = _rider(hooks, "gla_fwd")
    gla_raw, gla_st = lin_fwd("gla_fwd", Win(gq2, 128), Win(proj, 128, COL["gk"] // 128), Win(proj, 256, COL["gv"] // 256),
                              Win(gla_la, 128), side=side)
    landed()
    ret_b, gla_b = row_call("normgate", f_normgate, [Row(ret_raw), Row(proj, 1024, COL["rg"] // 1024), Row(gla_raw),
                                                    Row(proj, 1024, COL["gg"] // 1024)], [W["gnorm"]],
                            [(1024, MXU_DTYPE), (1024, MXU_DTYPE)], tm=512)
    branches = (attn, ret_b, gla_b)
    p = [mm("branch_proj", branches[n], W["branch"][n], "nn", tm=2048, tn=1024, tk=1024) for n in range(3)]
    merged, = row_call("merge", f_merge, [Row(p[0]), Row(p[1]), Row(p[2]), Row(proj, GATES_W, GATES_OFF // GATES_W)], [],
                       [(1024, MXU_DTYPE)], tm=512)
    post_norm = lambda y, x, g, b: f_deepnorm(x, y, g, b) * 2 + (y,)
    x1, x1b, mix = mm("out_proj_ln1", merged, W["out"], "nn", tm=512, tn=D_MODEL, tk=1024, extras=(h,),
                      row_params=(W["ln1_g"], W["ln1_b"]), epi=post_norm, n_out=3, out_dtypes=(F32, MXU_DTYPE, F32))
    side, landed = _rider(hooks, "mlp_up")
    u, act = mm("mlp_up", x1b, W["up"], "nn", tm=4096, tn=512, tk=1024, n_out=2, out_dtypes=(F32, MXU_DTYPE),
                epi=lambda r: (r, jnp.square(jnp.maximum(r, 0.0))), side=side)
    landed()
    x2, x2b, y = mm("mlp_down_ln2", act, W["down"], "nn", tm=1024, tn=D_MODEL, tk=1024, extras=(x1,),
                    row_params=(W["ln2_g"], W["ln2_b"]), epi=post_norm, n_out=3, out_dtypes=(F32, MXU_DTYPE, F32))
    saved = dict(h=h, hb=hb, proj=proj, attn_lse=attn_lse, rq2=rq2, rk2=rk2, ret_st=ret_st, gq2=gq2, gla_la=gla_la,
                 gla_st=gla_st, ret_raw=ret_raw, gla_raw=gla_raw, branches=branches, p=p, merged=merged, mix=mix, x1=x1, x1b=x1b,
                 u=u, act=act, y=y)
    return x2, x2b, saved


def _layer_backward(dx2, S, W, consts, hooks):
    cos, sin, ret_decay = consts
    h, proj, T = S["h"], S["proj"], S["h"].shape[0]
    add = lambda r, e: r + e
    dx1a, dy, dg2, db2 = row_call("ln2_bwd", vjp_fn(f_deepnorm, 2, 1), [Row(S["x1"]), Row(S["y"]), Row(dx2)],
                                  [W["ln2_g"], W["ln2_b"]], [(1024, F32), (1024, MXU_DTYPE)], [(1, 1024), (1, 1024)], tm=512)
    side, landed = _rider(hooks, "mlp_down_bwd")
    du = mm("mlp_down_bwd", dy, W["down"], "nt", tm=2048, tn=512, tk=1024, extras=(S["u"],),
            epi=lambda r, u: r * (2.0 * jnp.maximum(u, 0.0)), out_dtype=MXU_DTYPE, side=side)
    landed()
    g_down = mm("mlp_down_wgrad", S["act"], dy, "tn", tm=1024, tn=1024, tk=2048, out_dtype=WIRE_DTYPE)
    dx1 = mm("mlp_up_bwd", du, W["up"], "nt", tm=1024, tn=1024, tk=2048, extras=(dx1a,), epi=add)
    g_up = mm("mlp_up_wgrad", S["x1b"], du, "tn", tm=1024, tn=D_FF // N_DEV, tk=2048, out_dtype=WIRE_DTYPE,
              column_slabs=True)
    dha, dmix, dg1, db1 = row_call("ln1_bwd", vjp_fn(f_deepnorm, 2, 1), [Row(h), Row(S["mix"]), Row(dx1)],
                                   [W["ln1_g"], W["ln1_b"]], [(1024, F32), (1024, MXU_DTYPE)], [(1, 1024), (1, 1024)], tm=512)
    dmerged = mm("out_proj_bwd", dmix, W["out"], "nt", tm=1024, tn=1024, tk=1024)
    g_out = mm("out_proj_wgrad", S["merged"], dmix, "tn", tm=1024, tn=1024, tk=2048, out_dtype=WIRE_DTYPE)
    dproj = jax.ShapeDtypeStruct((T, PROJ_PAD), MXU_DTYPE)
    p = S["p"]
    dp0, dp1, dp2, dproj = row_call(
        "merge_bwd", vjp_fn(f_merge, 4, 1), [Row(p[0]), Row(p[1]), Row(p[2]), Row(proj, GATES_W, GATES_OFF // GATES_W), Row(dmerged)],
        [], [(1024, MXU_DTYPE)] * 3, tm=512, into=dproj, into_cols=[(GATES_W, GATES_OFF)])
    dps = (dp0, dp1, dp2)
    dbr = [mm("branch_proj_bwd", dps[n], W["branch"][n], "nt", tm=2048, tn=1024, tk=1024,
              out_dtype=MXU_DTYPE if n == 0 else F32) for n in range(3)]
    g_branch = [mm("branch_proj_wgrad", S["branches"][n], dps[n], "tn", tm=1024, tn=1024, tk=2048, out_dtype=WIRE_DTYPE)
                for n in range(3)]
    dret_raw, dgla_raw, dgnorm, dproj = row_call(
        "normgate_bwd", vjp_fn(lambda a, b, c, d, g: _reorder_normgate(a, b, c, d, g), 4, 2),
        [Row(S["ret_raw"]), Row(S["gla_raw"]), Row(proj, 1024, COL["rg"] // 1024), Row(proj, 1024, COL["gg"] // 1024),
         Row(dbr[1]), Row(dbr[2])], [W["gnorm"]], [(1024, F32), (1024, F32)], [(1, LIN_VAL_DIM)], tm=512,
        into=dproj, into_cols=[(1024, COL["rg"]), (1024, COL["gg"])])
    side, landed = _rider(hooks, "gla_bwd")
    dgq2, dgla_la, dproj = lin_bwd("gla_bwd", Win(S["gq2"], 128), Win(proj, 128, COL["gk"] // 128),
                                   Win(proj, 256, COL["gv"] // 256), Win(S["gla_la"], 128), S["gla_st"], dgla_raw, True,
                                   side=side, into=dproj, windows={1: COL["gk"], 2: COL["gv"]})
    landed()
    dwlr, dblr, dproj = row_call(
        "gla_pre_bwd", _gla_pre_bwd, [Row(proj, 512, COL["gq"] // 512), Row(proj, GLR_BLOCK, GLR_OFF // GLR_BLOCK),
                                      Row(dgq2), Row(dgla_la)], [W["wlr"], W["blr"]], [],
        [(GLR_BLOCK, 512), (1, 512)], tm=512, into=dproj,
        into_cols=[(512, COL["gq"]), (PROJ_PAD - GLR_OFF, GLR_OFF)])
    side, landed = _rider(hooks, "ret_bwd", dict(branch=g_branch, out=g_out, up=g_up, down=g_down))
    drq2, drk2, dproj = lin_bwd("ret_bwd", Win(S["rq2"], 128), Win(S["rk2"], 128), Win(proj, 256, COL["rv"] // 256),
                                Win(ret_decay, 128), S["ret_st"], dret_raw, False, side=side, into=dproj,
                                windows={2: COL["rv"]})
    landed()
    dproj, = row_call(
        "ret_pre_bwd", _ret_pre_bwd, [Row(proj, 512, COL["rq"] // 512), Row(proj, 512, COL["rk"] // 512), Row(cos), Row(sin),
                                      Row(drq2), Row(drk2)], [], [], tm=512, into=dproj,
        into_cols=[(512, COL["rq"]), (512, COL["rk"])])
    side, landed = _rider(hooks, "attn_bwd")
    dproj, dbias = attn_bwd(proj, *W["bias_tab"], S["branches"][0], S["attn_lse"], dbr[0], dproj, side=side)
    landed()
    d_rel_bias = bias_table_grad(dbias)
    side, landed = _rider(hooks, "proj_wgrad", dict(branch=g_branch, out=g_out, up=g_up, down=g_down))
    g_in = mm("proj_wgrad", S["hb"], dproj, "tn", tm=1024, tn=1280, tk=2048, out_dtype=WIRE_DTYPE, side=side)
    landed()
    side, landed = _rider(hooks, "proj_bwd", {"in": g_in})
    dh = mm("proj_bwd", dproj, W["in"], "nt", tm=1024, tn=1024, tk=2560, extras=(dha,), epi=add, side=side)
    landed()
    big = dict(g_in=g_in, branch=g_branch, out=g_out, up=g_up, down=g_down)
    small = dict(rel_bias=d_rel_bias, wlr=dwlr[:GLA_RANK], blr=dblr, gnorm=dgnorm, ln1_g=dg1, ln1_b=db1, ln2_g=dg2, ln2_b=db2)
    return dh, big, small


def _reorder_normgate(ret_raw, gla_raw, rg, gg, gnorm):
    return f_normgate(ret_raw, rg, gla_raw, gg, gnorm)


def _gla_pre_bwd(gq, glr, dq2, dla, wlr, blr):
    _, pull = jax.vjp(f_gla_pre, gq, glr, wlr, blr)
    dgq, dglr, dwlr, dblr = pull((dq2, dla))
    dglr = jnp.concatenate([dglr, jnp.zeros((dglr.shape[0], PROJ_PAD - GLR_OFF - GLR_BLOCK), F32)], axis=1)
    return dgq, dglr, dwlr, dblr


def _ret_pre_bwd(rq, rk, cos, sin, drq2, drk2):
    _, pull = jax.vjp(lambda a, b: f_ret_pre(a, b, cos, sin), rq, rk)
    return pull((drq2, drk2))


SHARDED = ("in", "branch", "out", "up", "down")

_FULL = {
    "branch": lambda g: [g[:, n].reshape(D_MODEL, D_MODEL) for n in range(3)],
    "out": lambda g: g.reshape(D_MODEL, D_MODEL),
    "up": lambda g: jnp.transpose(g, (1, 0, 2)).reshape(D_MODEL, D_FF),
    "down": lambda g: g.reshape(D_FF, D_MODEL),
}


def _owner_pieces(s):
    shard = PROJ_WIDTH // N_DEV
    lo, hi = s * shard, (s + 1) * shard
    pieces = []
    for a, b, padded in ((0, GLR_SRC, 0), (GLR_SRC, GATES_SRC, GLR_OFF), (GATES_SRC, PROJ_WIDTH, GATES_OFF)):
        x0, x1 = max(lo, a), min(hi, b)
        if x0 < x1:
            pieces.append((x0 - lo, x1 - x0, padded + x0 - a))
    return pieces


def _full_in(g):
    placed = sorted((p, s, c, w) for s in range(N_DEV) for c, w, p in _owner_pieces(s))
    cols = [g[s][:, c:c + w] for _, s, c, w in placed]
    return jnp.concatenate(cols + [jnp.zeros((D_MODEL, PROJ_PAD - PROJ_WIDTH), g.dtype)], axis=1)


def _full_wlr(g):
    wlr = jnp.transpose(g, (1, 0, 2)).reshape(GLA_RANK, LIN_HEADS * LIN_KEY_DIM)
    return jnp.pad(wlr, ((0, GLR_BLOCK - GLA_RANK), (0, 0)))


def _slab_in(g):
    return jnp.stack([jnp.concatenate([g[:, p:p + w] for _, w, p in _owner_pieces(s)], axis=1) for s in range(N_DEV)])


_SLABS = {
    "in": _slab_in,
    "branch": lambda gs: jnp.stack([b.reshape(N_DEV, D_MODEL // N_DEV, D_MODEL) for b in gs], axis=1),
    "out": lambda g: g.reshape(N_DEV, D_MODEL // N_DEV, D_MODEL),
    "up": lambda g: g,
    "down": lambda g: g.reshape(N_DEV, D_FF // N_DEV, D_MODEL),
}


def _pair_sums(name, slabs, from_sibling=None):
    sums = []
    for mine, sibs in zip(slabs, from_sibling or carrier(name, Side("pair", slabs))):
        width = mine.shape[-1]
        rows = math.prod(mine.shape[1:-1])
        tm = 256 if rows % 256 == 0 else 128
        nb = rows // tm
        my_rows = Row(mine.reshape(-1, width), rowmap=lambda i, nb=nb: (2 * (i // nb) + lax.axis_index("c")) * nb + i % nb)
        out, = row_call("pair_sum", lambda q, p: (p.astype(F32) + q.astype(F32),),
                        [Row(sibs.reshape(-1, width)), my_rows], [], [(width, WIRE_DTYPE)], tm=tm)
        sums.append(out.reshape(sibs.shape))
    return sums


def _sum_adam(name, parts, w, m, v, layer, tm, earlier):
    C = w.shape[-1]
    rows = [Row(parts, lead=(s,)) for s in range(parts.shape[0])] + [Row(a, lead=(layer,)) for a in (w, m, v)]
    return row_call(name, f_sum_adam, rows, [], [(C, F32)] * 4, tm=tm, stack=(layer, DEPTH), stacked_onto=earlier)


def kernel(x, ln_in_g, ln_in_b, w_in, rel_bias, gla_w_lr, gla_b_lr, gla_norm_g, w_branch, w_out, ln1_g, ln1_b, w_up, w_down, ln2_g, ln2_b, loss_target, m_ln_in_g, m_ln_in_b, m_w_in, m_rel_bias, m_gla_w_lr, m_gla_b_lr, m_gla_norm_g, m_w_branch, m_w_out, m_ln1_g, m_ln1_b, m_w_up, m_w_down, m_ln2_g, m_ln2_b, v_ln_in_g, v_ln_in_b, v_w_in, v_rel_bias, v_gla_w_lr, v_gla_b_lr, v_gla_norm_g, v_w_branch, v_w_out, v_ln1_g, v_ln1_b, v_w_up, v_w_down, v_ln2_g, v_ln2_b):
    T = x.shape[1]
    x2d = x.reshape(T, D_MODEL)
    tgt = loss_target.reshape(T, D_MODEL)
    me = 4 * lax.axis_index("x") + 2 * lax.axis_index("y") + lax.axis_index("c")

    inv = ROPE_BASE ** (-jnp.arange(0, LIN_KEY_DIM, 2, dtype=F32) / LIN_KEY_DIM)
    ang = jnp.arange(T, dtype=F32)[:, None] * inv[None, :]
    cos = jnp.concatenate([jnp.cos(ang), jnp.cos(ang)], axis=1)
    sin = jnp.concatenate([-jnp.sin(ang), jnp.sin(ang)], axis=1)
    log_gamma = jnp.log1p(-jnp.exp2(-5.0 - jnp.arange(LIN_HEADS, dtype=F32)))
    ret_decay = jnp.repeat(log_gamma, LIN_KEY_DIM).reshape(1, LIN_HEADS * LIN_KEY_DIM)
    consts = (cos, sin, ret_decay)

    layers = [dict(blr=_row2(gla_b_lr[l]), gnorm=_row2(gla_norm_g[l]), ln1_g=_row2(ln1_g[l]),
                   ln1_b=_row2(ln1_b[l]), ln2_g=_row2(ln2_g[l]), ln2_b=_row2(ln2_b[l])) for l in range(DEPTH)]
    shards = dict(branch=w_branch, out=w_out, up=w_up, down=w_down)
    shards["in"] = w_in
    wire = [{n: shards[n][l].astype(WIRE_DTYPE) for n in SHARDED} for l in range(DEPTH)]

    def gather_of(l, names):
        def landed(res):
            for n, g in zip(names, res):
                layers[l][n] = _full_in(g) if n == "in" else _FULL[n](g)
        return lambda ctx: (Side("gather", [wire[l][n] for n in names]), landed)

    first = Side("gather", [wire[0]["in"]] + [gla_w_lr[l] for l in range(DEPTH)])
    bias_tabs = bias_table(rel_bias, side=first)
    layers[0]["in"] = _full_in(first.results[0])
    for l in range(DEPTH):
        layers[l]["wlr"] = _full_wlr(first.results[1 + l])
        layers[l]["bias_tab"] = (bias_tabs, l * ATTN_HEADS)

    fwd_hooks = [{"proj": gather_of(1, ("in",)), "attn_fwd": gather_of(0, ("up", "down")),
                  "ret_fwd": gather_of(0, ("branch", "out"))},
                 {"proj": gather_of(1, ("up", "down")), "attn_fwd": gather_of(1, ("branch", "out"))}]
    h, hb = row_call("ln_in", _twice(f_ln_in), [Row(x2d)], [_row2(ln_in_g), _row2(ln_in_b)], STREAM, tm=512)
    saved = []
    for l in range(DEPTH):
        h, hb, s = _layer_forward(h, hb, layers[l], consts, fwd_hooks[l])
        saved.append(s)
    dh, loss_lanes = row_call("loss", f_loss, [Row(h), Row(tgt)], [], [(D_MODEL, F32)], [(1, 128)], tm=512)

    received = [dict() for _ in range(DEPTH)]

    def scatter_of(l, names, sums=None):
        def make(ctx):
            parts = [sums[n] for n in names] if sums is not None else _pair_sums(
                "pair_grads", [_SLABS[n](ctx[n]) for n in names])

            def landed(res):
                received[l].update(zip(names, res))
            return Side("chips", parts), landed
        return make

    small = [None] * DEPTH
    dh, big, small[1] = _layer_backward(dh, saved[1], layers[1], consts, {})
    slabs1 = [_SLABS[n](big["g_in"] if n == "in" else big[n]) for n in SHARDED]
    sums1 = {}

    def pair_exchange_1(ctx):
        def landed(res):
            sums1.update(zip(SHARDED, _pair_sums("pair_grads", slabs1, from_sibling=res)))
        return Side("pair", slabs1), landed

    rest = SHARDED[1:]
    sums0 = {}

    def pair_exchange_0(ctx):
        slabs0 = [_SLABS[n](ctx[n]) for n in rest]

        def landed(res):
            sums0.update(zip(rest, _pair_sums("pair_grads", slabs0, from_sibling=res)))
        return Side("pair", slabs0), landed

    dh, big, small[0] = _layer_backward(dh, saved[0], layers[0], consts, {
        "mlp_down_bwd": pair_exchange_1, "gla_bwd": scatter_of(1, ("in",), sums1), "ret_bwd": pair_exchange_0,
        "attn_bwd": scatter_of(1, rest, sums1), "proj_wgrad": scatter_of(0, rest, sums0), "proj_bwd": scatter_of(0, ("in",))})
    grad_x, d_in_g, d_in_b = row_call("ln_in_bwd", vjp_fn(f_ln_in, 1, 1), [Row(x2d), Row(dh)],
                                      [_row2(ln_in_g), _row2(ln_in_b)], [(D_MODEL, F32)], [(1, D_MODEL), (1, D_MODEL)], tm=256)

    small_names = ("rel_bias", "wlr", "blr", "gnorm", "ln1_g", "ln1_b", "ln2_g", "ln2_b")
    parts = [d_in_g, d_in_b] + [small[l][n] for l in range(DEPTH) for n in small_names] + [loss_lanes]
    shapes = [(D_MODEL,), (D_MODEL,)] + [s for l in range(DEPTH) for s in (
        (ATTN_HEADS, N_REL), (GLA_RANK, 512), (512,), (LIN_VAL_DIM,), (D_MODEL,), (D_MODEL,), (D_MODEL,), (D_MODEL,))] + [(128,)]
    red = _unpack(all_reduce_small(_pack(parts)), shapes)
    loss = jnp.sum(red[-1])
    g_small = {"ln_in_g": red[0], "ln_in_b": red[1]}
    for l in range(DEPTH):
        for j, n in enumerate(small_names):
            g_small[(l, n)] = red[2 + l * len(small_names) + j]
    shard = LIN_HEADS * LIN_KEY_DIM // N_DEV
    g_wlr = jnp.stack([lax.dynamic_slice_in_dim(g_small[(l, "wlr")], me * shard, shard, axis=1) for l in range(DEPTH)])
    stack = lambda n: jnp.stack([g_small[(l, n)] for l in range(DEPTH)])
    g_rep = [g_small["ln_in_g"], g_small["ln_in_b"], stack("rel_bias"), g_wlr, stack("blr"), stack("gnorm"),
             stack("ln1_g"), stack("ln1_b"), stack("ln2_g"), stack("ln2_b")]
    w_rep = [ln_in_g, ln_in_b, rel_bias, gla_w_lr, gla_b_lr, gla_norm_g, ln1_g, ln1_b, ln2_g, ln2_b]
    m_rep = [m_ln_in_g, m_ln_in_b, m_rel_bias, m_gla_w_lr, m_gla_b_lr, m_gla_norm_g, m_ln1_g, m_ln1_b, m_ln2_g, m_ln2_b]
    v_rep = [v_ln_in_g, v_ln_in_b, v_rel_bias, v_gla_w_lr, v_gla_b_lr, v_gla_norm_g, v_ln1_g, v_ln1_b, v_ln2_g, v_ln2_b]
    rep_shapes = [w.shape for w in w_rep]
    d_pack, m_pack, v_pack = row_call("adam_small", f_adam, [Row(_pack(g_rep)), Row(_pack(w_rep)), Row(_pack(m_rep)), Row(_pack(v_rep))],
                                      [], [(128, F32)] * 3, tm=256)
    d_rep, nm_rep, nv_rep = _unpack(d_pack, rep_shapes), _unpack(m_pack, rep_shapes), _unpack(v_pack, rep_shapes)

    def owned(key, w, m, v, name, tm):
        C = w.shape[-1]
        flat = lambda a: a.reshape(DEPTH, -1, C)
        outs = None
        for l in range(DEPTH):
            outs = _sum_adam(name, received[l][key].reshape(4, -1, C), flat(w), flat(m), flat(v), l, tm, outs)
        return [o.reshape(w.shape) for o in outs]

    o_in = owned("in", w_in, m_w_in, v_w_in, "adam_w_in", 256)
    o_br = owned("branch", w_branch, m_w_branch, v_w_branch, "adam_w_branch", 128)
    o_out = owned("out", w_out, m_w_out, v_w_out, "adam_w_out", 128)
    o_up = owned("up", w_up, m_w_up, v_w_up, "adam_w_up", 256)
    o_down = owned("down", w_down, m_w_down, v_w_down, "adam_w_down", 256)

    def tree(j, rep):
        return [rep[0], rep[1], o_in[j], rep[2], rep[3], rep[4], rep[5], o_br[j], o_out[j], rep[6], rep[7], o_up[j], o_down[j],
                rep[8], rep[9]]

    return (loss, grad_x.reshape(1, T, D_MODEL), *tree(0, g_rep), *tree(1, d_rep), *tree(2, nm_rep), *tree(3, nv_rep))
```

```python
import functools
import math

import jax
import jax.numpy as jnp
from jax import lax
from jax.experimental import pallas as pl
from jax.experimental.pallas import tpu as pltpu

F32 = jnp.float32
MXU_DTYPE = jnp.bfloat16
WIRE_DTYPE = jnp.bfloat16
PROJ_DTYPE = jnp.bfloat16

N_DEV = 8
D_MODEL = 1024
DEPTH = 2
CHUNK = 64
ATTN_HEADS = 8
ATTN_HEAD_DIM = 128
ATTN_LEFT_CHUNKS = 8
BAND = (ATTN_LEFT_CHUNKS + 1) * CHUNK
PAD_ROWS = ATTN_LEFT_CHUNKS * CHUNK
REL_CLIP = 2 * CHUNK
N_REL = 2 * REL_CLIP + 1
REL_MIN = REL_CLIP - (CHUNK - 1)
LIN_HEADS = 4
LIN_KEY_DIM = 128
LIN_VAL_DIM = 256
GLA_RANK = 16
GLA_GATE_NORMALIZER = 16.0
ROPE_BASE = 10000.0
D_FF = 4 * D_MODEL
LN_EPS = 1e-5
NEG_INF = -1e30
ALPHA = (2 * DEPTH) ** 0.25
ADAM_LR, ADAM_B1, ADAM_B2, ADAM_EPS, ADAM_WD, ADAM_STEP = 0.001, 0.9, 0.999, 1e-08, 0.01, 10

PROJ_WIDTH = 12304
GLR_SRC = 9216
GATES_SRC = GLR_SRC + GLA_RANK
MAIN_W = 9216
GATES_W = 3 * D_MODEL
GATES_OFF = MAIN_W
GLR_OFF = MAIN_W + GATES_W
PROJ_PAD = 12800
GLR_BLOCK = 128
COL = dict(aq=0, ak=1024, av=2048, rq=3072, rk=3584, rv=4096, rg=5120, gq=6144, gk=6656, gv=7168, gg=8192)

SEG_CHUNKS = 8
SEG = SEG_CHUNKS * CHUNK
VMEM_BIG = 56 * 1024 * 1024

_NN = (((1,), (0,)), ((), ()))
_NT = (((1,), (1,)), ((), ()))
_TN = (((0,), (0,)), ((), ()))


def _mxu(a, b, dn):
    return lax.dot_general(a.astype(MXU_DTYPE), b.astype(MXU_DTYPE), dn, preferred_element_type=F32)


@jax.custom_vjp
def dot_nn(a, b):
    return _mxu(a, b, _NN)


def _dot_nn_fwd(a, b):
    return _mxu(a, b, _NN), (a, b)


def _dot_nn_bwd(res, g):
    a, b = res
    return _mxu(g, b, _NT).astype(a.dtype), _mxu(a, g, _TN).astype(b.dtype)


dot_nn.defvjp(_dot_nn_fwd, _dot_nn_bwd)


@jax.custom_vjp
def dot_nt(a, b):
    return _mxu(a, b, _NT)


def _dot_nt_fwd(a, b):
    return _mxu(a, b, _NT), (a, b)


def _dot_nt_bwd(res, g):
    a, b = res
    return _mxu(g, b, _NN).astype(a.dtype), _mxu(g, a, _TN).astype(b.dtype)


dot_nt.defvjp(_dot_nt_fwd, _dot_nt_bwd)


@jax.custom_vjp
def dot_tn(a, b):
    return _mxu(a, b, _TN)


def _dot_tn_fwd(a, b):
    return _mxu(a, b, _TN), (a, b)


def _dot_tn_bwd(res, g):
    a, b = res
    return _mxu(b, g, _NT).astype(a.dtype), _mxu(a, g, _NN).astype(b.dtype)


dot_tn.defvjp(_dot_tn_fwd, _dot_tn_bwd)


def _tri(n, lower):
    r = lax.broadcasted_iota(jnp.int32, (n, n), 0)
    c = lax.broadcasted_iota(jnp.int32, (n, n), 1)
    return (r >= c) if lower else (r <= c)


def _tri_dot(lower, x):
    tri = _tri(x.shape[0], lower).astype(jnp.bfloat16)
    hi = x.astype(jnp.bfloat16)
    rest = x - hi.astype(F32)
    mid = rest.astype(jnp.bfloat16)
    lo = (rest - mid.astype(F32)).astype(jnp.bfloat16)
    dot = lambda part: lax.dot_general(tri, part, _NN, preferred_element_type=F32)
    return dot(hi) + dot(mid) + dot(lo)


@jax.custom_vjp
def chunk_cumsum(x):
    return _tri_dot(True, x)


def _chunk_cumsum_fwd(x):
    return chunk_cumsum(x), None


def _chunk_cumsum_bwd(_, g):
    return (_tri_dot(False, g),)


chunk_cumsum.defvjp(_chunk_cumsum_fwd, _chunk_cumsum_bwd)


@jax.custom_vjp
def swap_halves(t):
    return pltpu.roll(t, t.shape[-1] // 2, t.ndim - 1)


def _swap_fwd(t):
    return swap_halves(t), None


def _swap_bwd(_, g):
    return (swap_halves(g),)


swap_halves.defvjp(_swap_fwd, _swap_bwd)


def mm(name, a, b, dims, *, tm, tn, tk, out_dtype=F32, extras=(), row_params=(), epi=None, n_out=1, out_dtypes=None,
       side=None, column_slabs=False):
    if dims == "nn":
        (M, K), N = a.shape, b.shape[1]
    elif dims == "nt":
        (M, K), N = a.shape, b.shape[0]
    else:
        (K, M), N = a.shape, b.shape[1]
    tm, tn, tk = min(tm, M), min(tn, N), min(tk, K)
    assert M % tm == 0 and N % tn == 0 and K % tk == 0, (name, M, N, K, tm, tn, tk)
    nk = K // tk
    dn = {"nn": _NN, "nt": _NT, "tn": _TN}[dims]
    a_spec = pl.BlockSpec((tk, tm), lambda i, j, k: (k, i)) if dims == "tn" else pl.BlockSpec((tm, tk), lambda i, j, k: (i, k))
    b_spec = pl.BlockSpec((tn, tk), lambda i, j, k: (j, k)) if dims == "nt" else pl.BlockSpec((tk, tn), lambda i, j, k: (k, j))
    o_spec = pl.BlockSpec((tm, tn), lambda i, j, k: (i, j))
    row_spec = pl.BlockSpec((1, tn), lambda i, j, k: (0, j))
    n_ex = len(extras) + len(row_params)
    out_dtypes = out_dtypes or (out_dtype,) * n_out

    def finish(r, ex_refs, o_refs):
        if epi is not None:
            r = epi(r, *[e[...] for e in ex_refs])
        r = r if isinstance(r, tuple) else (r,)
        for o, v in zip(o_refs, r):
            o[...] = v.astype(o.dtype)

    def body(*refs):
        a_ref, b_ref = refs[:2]
        ex_refs = refs[2:2 + n_ex]
        o_refs = refs[2 + n_ex:2 + n_ex + n_out]
        part = _mxu(a_ref[...], b_ref[...], dn)
        if nk == 1:
            finish(part, ex_refs, o_refs)
            return
        acc = refs[-1]
        k = pl.program_id(2)

        @pl.when(k == 0)
        def _():
            acc[...] = part

        @pl.when(k > 0)
        def _():
            acc[...] += part

        @pl.when(k == nk - 1)
        def _():
            finish(acc[...], ex_refs, o_refs)

    outs = _side_call(
        side, body,
        name=name,
        grid=(M // tm, N // tn, nk),
        in_specs=[a_spec, b_spec] + [o_spec] * len(extras) + [row_spec] * len(row_params),
        out_specs=[pl.BlockSpec((None, tm, tn), lambda i, j, k: (j, i, 0))] * n_out if column_slabs else [o_spec] * n_out,
        out_shape=[jax.ShapeDtypeStruct((N // tn, M, tn) if column_slabs else (M, N), dt) for dt in out_dtypes],
        scratch_shapes=[pltpu.VMEM((tm, tn), F32)] if nk > 1 else [],
        args=[a, b, *extras, *row_params],
        semantics=("parallel", "parallel", "arbitrary"),
    )
    return outs[0] if n_out == 1 else outs


class Row:
    def __init__(self, arr, width=None, col=0, lead=(), rowmap=None):
        self.arr, self.lead, self.col = arr, tuple(lead), col
        self.width = arr.shape[-1] if width is None else width
        self.rowmap = rowmap

    def spec(self, tm):
        lead, col, rowmap = self.lead, self.col, self.rowmap or (lambda i: i)
        return pl.BlockSpec((None,) * len(lead) + (tm, self.width), lambda i: (*lead, rowmap(i), col))


def row_call(name, fn, rows, params, new_outs, red_shapes=(), *, tm, into=None, into_cols=(), stack=None, stacked_onto=None):
    R = rows[0].arr.shape[len(rows[0].lead)]
    tm = min(tm, R)
    assert R % tm == 0, (name, R, tm)
    n_row, n_par, n_new, n_into, n_red = len(rows), len(params), len(new_outs), len(into_cols), len(red_shapes)
    fresh = isinstance(into, jax.ShapeDtypeStruct)

    def body(*refs):
        row_refs = refs[:n_row]
        par_refs = refs[n_row:n_row + n_par]
        pos = n_row + n_par + (1 if (into is not None and not fresh) else 0) + (n_new if stacked_onto is not None else 0)
        new_refs = refs[pos:pos + n_new]
        red_refs = refs[pos + n_new:pos + n_new + n_red]
        pos2 = pos + n_new + n_red
        into_ref = refs[pos2] if into is not None else None
        stage = refs[pos2 + 1:pos2 + 1 + n_into] if into is not None else ()
        i = pl.program_id(0)
        widen = lambda t: t.astype(F32) if jnp.issubdtype(t.dtype, jnp.floating) else t
        vals = fn(*[widen(r[...]) for r in row_refs], *[p[...] for p in par_refs])
        vals = vals if isinstance(vals, (tuple, list)) else (vals,)
        for o, v in zip(new_refs, vals[:n_new]):
            o[...] = v.astype(o.dtype)

        def window_copy(j, step):
            width, off = into_cols[j]
            rows_at = pl.ds(pl.multiple_of(step * tm, tm), tm)
            return pltpu.make_async_copy(stage[j], into_ref.at[rows_at, pl.ds(off, width)], refs[pos2 + 1 + n_into].at[j])

        for j, v in enumerate(vals[n_new:n_new + n_into]):
            @pl.when(i > 0)
            def _(j=j):
                window_copy(j, i - 1).wait()

            stage[j][...] = v.astype(stage[j].dtype)
            window_copy(j, i).start()

            @pl.when(i == R // tm - 1)
            def _(j=j):
                window_copy(j, i).wait()
        for o, v in zip(red_refs, vals[n_new + n_into:]):
            @pl.when(i == 0)
            def _(o=o, v=v):
                o[...] = v.astype(o.dtype)

            @pl.when(i > 0)
            def _(o=o, v=v):
                o[...] += v.astype(o.dtype)

    in_specs = [r.spec(tm) for r in rows] + [pl.BlockSpec(p.shape, lambda i, nd=p.ndim: (0,) * nd) for p in params]
    args = [r.arr for r in rows] + list(params)
    if stack is None:
        out_specs = [pl.BlockSpec((tm, w), lambda i: (i, 0)) for w, _ in new_outs]
        out_shape = [jax.ShapeDtypeStruct((R, w), dt) for w, dt in new_outs]
    else:
        index, depth = stack
        out_specs = [pl.BlockSpec((None, tm, w), lambda i: (index, i, 0)) for w, _ in new_outs]
        out_shape = [jax.ShapeDtypeStruct((depth, R, w), dt) for w, dt in new_outs]
    out_specs += [pl.BlockSpec(s, lambda i, nd=len(s): (0,) * nd) for s in red_shapes]
    out_shape += [jax.ShapeDtypeStruct(s, F32) for s in red_shapes]
    aliases, scratch = {}, []
    if into is not None:
        out_specs.append(pl.BlockSpec(memory_space=pl.ANY))
        out_shape.append(jax.ShapeDtypeStruct(into.shape, into.dtype))
        if not fresh:
            in_specs.append(pl.BlockSpec(memory_space=pl.ANY))
            args.append(into)
            aliases = {len(args) - 1: len(out_shape) - 1}
        scratch = [pltpu.VMEM((tm, w), into.dtype) for w, _ in into_cols] + [pltpu.SemaphoreType.DMA((n_into,))]
    if stacked_onto is not None:
        for j, prev in enumerate(stacked_onto):
            in_specs.append(pl.BlockSpec(memory_space=pl.ANY))
            args.append(prev)
            aliases[len(args) - 1] = j
    outs = pl.pallas_call(
        body,
        name=name,
        grid=(R // tm,),
        in_specs=in_specs,
        out_specs=out_specs,
        out_shape=out_shape,
        scratch_shapes=scratch,
        input_output_aliases=aliases,
        compiler_params=pltpu.CompilerParams(dimension_semantics=("arbitrary",), vmem_limit_bytes=VMEM_BIG),
    )(*args)
    return list(outs)


def vjp_fn(fn, n_row, n_out, passthrough=0):
    def bwd(*args):
        rows = [a.astype(F32) for a in args[:n_row]]
        cts = args[n_row:n_row + n_out]
        thru = args[n_row + n_out:n_row + n_out + passthrough]
        pars = args[n_row + n_out + passthrough:]
        outs, pull = jax.vjp(lambda *xs: tuple(fn(*xs)), *rows, *pars)
        grads = pull(tuple(c.astype(o.dtype) for c, o in zip(cts, outs)))
        return tuple(grads[:n_row]) + tuple(thru) + tuple(grads[n_row:])

    return bwd


def _ln(x, g, b):
    mu = jnp.mean(x, axis=-1, keepdims=True)
    xc = x - mu
    var = jnp.mean(xc * xc, axis=-1, keepdims=True)
    return xc * lax.rsqrt(var + LN_EPS) * g + b


def f_ln_in(x, g, b):
    return (_ln(x, g, b),)


def f_deepnorm(x, y, g, b):
    return (_ln(ALPHA * x + y, g, b),)


def _twice(fn):
    return lambda *args: fn(*args) * 2


STREAM = [(D_MODEL, F32), (D_MODEL, MXU_DTYPE)]


def _rope(t, cos, sin):
    parts = []
    for h in range(LIN_HEADS):
        th = t[:, h * LIN_KEY_DIM:(h + 1) * LIN_KEY_DIM]
        parts.append(th * cos + swap_halves(th) * sin)
    return jnp.concatenate(parts, axis=1)


def f_ret_pre(rq, rk, cos, sin):
    return _rope(rq, cos, sin), _rope(rk, cos, sin) * LIN_KEY_DIM ** -0.5


def _log_sigmoid(x):
    return jnp.minimum(x, 0.0) - jnp.log(1.0 + jnp.exp(-jnp.abs(x)))


def f_gla_pre(gq, glr, wlr, blr):
    gate_pre = dot_nn(glr, wlr) + blr
    return gq * LIN_KEY_DIM ** -0.5, _log_sigmoid(gate_pre) / GLA_GATE_NORMALIZER


def f_normgate(ret_raw, rg, gla_raw, gg, gnorm):
    rets, glas = [], []
    for h in range(LIN_HEADS):
        sl = slice(h * LIN_VAL_DIM, (h + 1) * LIN_VAL_DIM)
        t = ret_raw[:, sl]
        mu = jnp.mean(t, axis=-1, keepdims=True)
        tc = t - mu
        rets.append(tc * lax.rsqrt(jnp.mean(tc * tc, axis=-1, keepdims=True) + LN_EPS))
        u = gla_raw[:, sl]
        glas.append(u * lax.rsqrt(jnp.mean(u * u, axis=-1, keepdims=True) + LN_EPS) * gnorm)
    ret = jax.nn.silu(rg) * jnp.concatenate(rets, axis=1)
    gla = jax.nn.silu(gg) * jnp.concatenate(glas, axis=1)
    return ret, gla


def f_merge(p0, p1, p2, gl):
    out = jax.nn.sigmoid(gl[:, :D_MODEL]) * p0
    out = out + jax.nn.sigmoid(gl[:, D_MODEL:2 * D_MODEL]) * p1
    return (out + jax.nn.sigmoid(gl[:, 2 * D_MODEL:]) * p2,)


def f_loss(y, tgt):
    diff = y - tgt
    sq = jnp.sum(diff * diff, axis=0, keepdims=True)
    lanes = sq[:, :128]
    for j in range(1, D_MODEL // 128):
        lanes = lanes + sq[:, j * 128:(j + 1) * 128]
    return diff * (1.0 / D_MODEL), lanes * (0.5 / D_MODEL)


def _adam(g, w, m, v):
    m = ADAM_B1 * m + (1.0 - ADAM_B1) * g
    v = ADAM_B2 * v + (1.0 - ADAM_B2) * jnp.square(g)
    m_hat = m / (1.0 - ADAM_B1 ** ADAM_STEP)
    v_hat = v / (1.0 - ADAM_B2 ** ADAM_STEP)
    delta = -ADAM_LR * (m_hat / (jnp.sqrt(v_hat) + ADAM_EPS) + ADAM_WD * w)
    return delta, m, v


def f_sum_adam(*args):
    g = args[0].astype(F32)
    for p in args[1:-3]:
        g = g + p.astype(F32)
    return (g,) + _adam(g, *args[-3:])


def f_adam(g, w, m, v):
    return _adam(g, w, m, v)


GROUP = 4
GROUP_ROWS = GROUP * CHUNK
GROUP_KEYS = (GROUP + ATTN_LEFT_CHUNKS) * CHUNK
DIAG_ROWS = REL_CLIP + CHUNK
GRAD_ROWS = 200


def _blocks_by_delta():
    blocks = {}
    for qi in range(GROUP):
        for J in range(GROUP_KEYS // 128):
            blocks.setdefault(qi - 2 * J, []).append((qi, J))
    return blocks


def bias_table(rel_bias, side=None):
    n_tab = rel_bias.shape[0] * ATTN_HEADS

    def body(rb_ref, o_ref):
        h = pl.program_id(0)
        rm = lax.broadcasted_iota(jnp.int32, (CHUNK, 128), 0)
        cm = lax.broadcasted_iota(jnp.int32, (CHUNK, 128), 1)
        ahead = jnp.where(cm >= CHUNK, 1, 0)
        for delta, where in _blocks_by_delta().items():
            base = 64 * delta + PAD_ROWS
            rel = jnp.minimum(base + rm - cm, REL_CLIP) + REL_CLIP
            in_band = (ahead - delta >= 0) & (ahead - delta <= ATTN_LEFT_CHUNKS)
            lo = max(REL_MIN, min(base - 127, REL_CLIP) + REL_CLIP)
            hi = min(base + CHUNK - 1, REL_CLIP) + REL_CLIP

            def step(r, acc, rel=rel):
                return jnp.where(rel == r, rb_ref[h * N_REL + r], acc)

            blk = jnp.zeros((CHUNK, 128), F32) if hi < lo else lax.fori_loop(lo, hi + 1, step, jnp.zeros((CHUNK, 128), F32))
            blk = jnp.where(in_band, blk, NEG_INF)
            for qi, J in where:
                o_ref[pl.ds(qi * CHUNK, CHUNK), pl.ds(J * 128, 128)] = blk

    return _side_call(
        side, body,
        name="bias_table",
        grid=(n_tab,),
        in_specs=[pl.BlockSpec(memory_space=pltpu.SMEM)],
        out_specs=[pl.BlockSpec((None, GROUP_ROWS, GROUP_KEYS), lambda h: (h, 0, 0))],
        out_shape=[jax.ShapeDtypeStruct((n_tab, GROUP_ROWS, GROUP_KEYS), F32)],
        scratch_shapes=[],
        args=[rel_bias.reshape(n_tab * N_REL)],
    )[0]


def bias_table_grad(dbias):
    blocks = _blocks_by_delta()
    below_clip = range(-(CHUNK - 1), REL_CLIP)
    deltas = [d for d in sorted(blocks) if any(-127 <= pos - 64 * d - PAD_ROWS <= CHUNK - 1 for pos in below_clip)]

    def body(d_ref, o_ref, diag):
        sums = []
        for delta in deltas:
            acc = None
            for qi, J in blocks[delta]:
                blk = d_ref[pl.ds(qi * CHUNK, CHUNK), pl.ds(J * 128, 128)]
                acc = blk if acc is None else acc + blk
            sums.append(acc)
        off = lax.broadcasted_iota(jnp.int32, (CHUNK, 128), 0) - lax.broadcasted_iota(jnp.int32, (CHUNK, 128), 1)
        diag[...] = jnp.zeros_like(diag)

        def step(t, carry):
            pos = t - (CHUNK - 1)
            picked = jnp.zeros((CHUNK, 128), F32)
            for delta, s in zip(deltas, sums):
                picked = picked + jnp.where(off == pos - 64 * delta - PAD_ROWS, s, 0.0)
            diag[pl.ds(t, 1), :] = jnp.sum(picked, axis=0, keepdims=True)
            return carry

        lax.fori_loop(0, DIAG_ROWS, step, 0, unroll=8)
        d = d_ref[...]
        total = jnp.sum(jnp.sum(d, axis=1, keepdims=True), axis=0, keepdims=True)
        o_ref[pl.ds(0, DIAG_ROWS), :] = jnp.broadcast_to(jnp.sum(diag[...], axis=1, keepdims=True), (DIAG_ROWS, 128))
        o_ref[pl.ds(DIAG_ROWS, GRAD_ROWS - DIAG_ROWS), :] = jnp.broadcast_to(total, (GRAD_ROWS - DIAG_ROWS, 128))

    out = pl.pallas_call(
        body,
        name="bias_table_grad",
        grid=(ATTN_HEADS,),
        in_specs=[pl.BlockSpec((None, GROUP_ROWS, GROUP_KEYS), lambda h: (h, 0, 0))],
        out_specs=pl.BlockSpec((None, GRAD_ROWS, 128), lambda h: (h, 0, 0)),
        out_shape=jax.ShapeDtypeStruct((ATTN_HEADS, GRAD_ROWS, 128), F32),
        scratch_shapes=[pltpu.VMEM((DIAG_ROWS, 128), F32)],
        compiler_params=pltpu.CompilerParams(dimension_semantics=("arbitrary",)),
    )(dbias)
    below = out[:, :DIAG_ROWS - 1, 0]
    clipped = out[:, DIAG_ROWS, 0] - jnp.sum(below, axis=1)
    return jnp.concatenate([jnp.zeros((ATTN_HEADS, REL_MIN), F32), below, clipped[:, None]], axis=1)


def _attn_stage(q_ref, k_ref, v_ref, qs, kp, vp, T):
    qs[...] = (q_ref[...] * ATTN_HEAD_DIM ** -0.5).astype(qs.dtype)
    for dst, src in ((kp, k_ref), (vp, v_ref)):
        dst[pl.ds(0, PAD_ROWS), :] = jnp.zeros((PAD_ROWS, ATTN_HEAD_DIM), dst.dtype)
        dst[pl.ds(PAD_ROWS, T), :] = src[...].astype(dst.dtype)


STRIP = 256


def _strip_scores(g, t, first_groups, r0, qc, kp, b_ref):
    keys = pl.ds(r0 + t * STRIP if first_groups else pl.multiple_of(r0 + t * STRIP, STRIP), STRIP)
    s = _mxu(qc, kp[keys, :], _NT) + b_ref[:, t * STRIP:(t + 1) * STRIP]
    if first_groups:
        k_chunk = lax.shift_right_logical(lax.broadcasted_iota(jnp.int32, (GROUP_ROWS, STRIP), 1), 6)
        s = jnp.where(g * GROUP + t * (STRIP // CHUNK) + k_chunk >= ATTN_LEFT_CHUNKS, s, NEG_INF)
    return s, keys


def _for_groups(T, step):
    n_first = min(ATTN_LEFT_CHUNKS // GROUP, T // GROUP_ROWS)
    for g in range(n_first):
        step(g, True)
    n_loop = T // GROUP_ROWS - n_first
    unroll = next(u for u in (14, 7, 4, 2, 1) if n_loop % u == 0)
    lax.fori_loop(n_first, T // GROUP_ROWS, lambda g, c: (step(g, False), c)[1], 0, unroll=unroll)


def _head_spec(T, base):
    return pl.BlockSpec((T, ATTN_HEAD_DIM), lambda h: (0, base + h))


def _attn_scratch(T):
    return [pltpu.VMEM((T, ATTN_HEAD_DIM), MXU_DTYPE)] + [pltpu.VMEM((T + PAD_ROWS, ATTN_HEAD_DIM), MXU_DTYPE)] * 2


def attn_fwd(proj, bias_tab, first_tab, side=None):
    T = proj.shape[0]

    def body(q_ref, k_ref, v_ref, b_ref, o_ref, lse_ref, qs, kp, vp):
        _attn_stage(q_ref, k_ref, v_ref, qs, kp, vp, T)

        def step(g, first):
            r0 = g * GROUP_ROWS if first else pl.multiple_of(g * GROUP_ROWS, GROUP_ROWS)
            rows = pl.ds(r0, GROUP_ROWS)
            qc = qs[rows, :]
            m = jnp.full((GROUP_ROWS, 1), NEG_INF, F32)
            l = jnp.zeros((GROUP_ROWS, 1), F32)
            acc = jnp.zeros((GROUP_ROWS, ATTN_HEAD_DIM), F32)
            for t in range(GROUP_KEYS // STRIP):
                s, keys = _strip_scores(g, t, first, r0, qc, kp, b_ref)
                m_new = jnp.maximum(m, jnp.max(s, axis=-1, keepdims=True))
                keep = jnp.exp(m - m_new)
                e = jnp.exp(s - m_new)
                l = l * keep + jnp.sum(e, axis=-1, keepdims=True)
                acc = acc * keep + _mxu(e, vp[keys, :], _NN)
                m = m_new
            o_ref[rows, :] = (acc * (1.0 / l)).astype(o_ref.dtype)
            lse_ref[rows, :] = jnp.broadcast_to(m + jnp.log(l), (GROUP_ROWS, ATTN_HEAD_DIM))

        _for_groups(T, step)

    out, lse = _side_call(
        side, body,
        name="attn_fwd",
        grid=(ATTN_HEADS,),
        in_specs=[_head_spec(T, COL["aq"] // 128), _head_spec(T, COL["ak"] // 128), _head_spec(T, COL["av"] // 128),
                  pl.BlockSpec((None, GROUP_ROWS, GROUP_KEYS), lambda h: (first_tab + h, 0, 0))],
        out_specs=[pl.BlockSpec((T, ATTN_HEAD_DIM), lambda h: (0, h))] * 2,
        out_shape=[jax.ShapeDtypeStruct((T, ATTN_HEADS * ATTN_HEAD_DIM), MXU_DTYPE),
                   jax.ShapeDtypeStruct((T, ATTN_HEADS * ATTN_HEAD_DIM), F32)],
        scratch_shapes=_attn_scratch(T),
        args=[proj, proj, proj, bias_tab],
    )
    return out, lse


def attn_bwd(proj, bias_tab, first_tab, out, lse, dout, dproj, side=None):
    T = proj.shape[0]

    def body(q_ref, k_ref, v_ref, b_ref, do_ref, o_ref, lse_ref, dproj_in, dbias_ref, dproj_ref, qs, kp, vp, dkp, dvp, dq, stage,
             stage_sems):
        del dproj_in
        h = pl.program_id(0)
        _attn_stage(q_ref, k_ref, v_ref, qs, kp, vp, T)
        dkp[...] = jnp.zeros_like(dkp)
        dvp[...] = jnp.zeros_like(dvp)
        dbias_ref[...] = jnp.zeros_like(dbias_ref)

        def step(g, first):
            r0 = g * GROUP_ROWS if first else pl.multiple_of(g * GROUP_ROWS, GROUP_ROWS)
            rows = pl.ds(r0, GROUP_ROWS)
            qc, do = qs[rows, :], do_ref[rows, :]
            lse = jnp.concatenate([lse_ref[rows, :]] * (STRIP // ATTN_HEAD_DIM), axis=1)
            row_term = jnp.sum(do.astype(F32) * o_ref[rows, :].astype(F32), axis=-1, keepdims=True)
            dq_acc = jnp.zeros((GROUP_ROWS, ATTN_HEAD_DIM), F32)
            for t in range(GROUP_KEYS // STRIP):
                s, keys = _strip_scores(g, t, first, r0, qc, kp, b_ref)
                cols = slice(t * STRIP, (t + 1) * STRIP)
                kb, vb = kp[keys, :], vp[keys, :]
                p = jnp.exp(s - lse)
                dvp[keys, :] += _mxu(p, do, _TN)
                ds = p * (_mxu(do, vb, _NT) - row_term)
                dq_acc = dq_acc + _mxu(ds, kb, _NN)
                dkp[keys, :] += _mxu(ds, qc, _TN)
                dbias_ref[:, cols] += ds
            dq[rows, :] = dq_acc * ATTN_HEAD_DIM ** -0.5

        _for_groups(T, step)
        col = pl.multiple_of(h * ATTN_HEAD_DIM, ATTN_HEAD_DIM)
        for n, (grad, first_row, base) in enumerate(((dq, 0, "aq"), (dkp, PAD_ROWS, "ak"), (dvp, PAD_ROWS, "av"))):
            copy = pltpu.make_async_copy(stage.at[n], dproj_ref.at[:, pl.ds(COL[base] + col, ATTN_HEAD_DIM)], stage_sems.at[n])

            @pl.when(h > 0)
            def _(copy=copy):
                copy.wait()

            stage[n] = grad[pl.ds(first_row, T), :].astype(stage.dtype)
            copy.start()

            @pl.when(h == ATTN_HEADS - 1)
            def _(copy=copy):
                copy.wait()

    dbias, dproj = _side_call(
        side, body,
        name="attn_bwd",
        grid=(ATTN_HEADS,),
        in_specs=[_head_spec(T, COL["aq"] // 128), _head_spec(T, COL["ak"] // 128), _head_spec(T, COL["av"] // 128),
                  pl.BlockSpec((None, GROUP_ROWS, GROUP_KEYS), lambda h: (first_tab + h, 0, 0))]
        + [pl.BlockSpec((T, ATTN_HEAD_DIM), lambda h: (0, h))] * 3 + [pl.BlockSpec(memory_space=pl.ANY)],
        out_specs=[pl.BlockSpec((None, GROUP_ROWS, GROUP_KEYS), lambda h: (h, 0, 0)), pl.BlockSpec(memory_space=pl.ANY)],
        out_shape=[jax.ShapeDtypeStruct((ATTN_HEADS, GROUP_ROWS, GROUP_KEYS), F32),
                   jax.ShapeDtypeStruct(dproj.shape, dproj.dtype)],
        scratch_shapes=_attn_scratch(T) + [pltpu.VMEM((T + PAD_ROWS, ATTN_HEAD_DIM), F32)] * 2
        + [pltpu.VMEM((T, ATTN_HEAD_DIM), F32), pltpu.VMEM((3, T, ATTN_HEAD_DIM), dproj.dtype), pltpu.SemaphoreType.DMA((3,))],
        args=[proj, proj, proj, bias_tab, dout, out, lse, dproj],
        aliases={7: 1},
    )
    return dproj, dbias


def _lin_segment(q, k, v, la, st):
    lower = _tri(CHUNK, True)
    steps = (lax.broadcasted_iota(jnp.int32, (CHUNK, 1), 0) + 1).astype(F32)
    outs = []
    for i in range(SEG_CHUNKS):
        sl = slice(i * CHUNK, (i + 1) * CHUNK)
        qi, ki, vi = q[sl], k[sl], v[sl]
        if la.shape[0] == 1:
            cum, last = la * steps, la * float(CHUNK)
        else:
            cum, last = chunk_cumsum(la[sl]), jnp.sum(la[sl], axis=0, keepdims=True)
        e_pos, e_neg = jnp.exp(cum), jnp.exp(-cum)
        q_fwd = qi * e_pos
        s_fwd = dot_nt(q_fwd, ki * e_neg)
        s_bwd = dot_nt(qi * e_neg, ki * e_pos)
        o = dot_nn(jnp.where(lower, s_fwd, s_bwd), vi) + dot_nt(q_fwd, st)
        st = jnp.exp(last) * st + dot_tn(vi, ki * jnp.exp(last - cum))
        outs.append(o)
    return jnp.concatenate(outs, axis=0), st


class Win:
    def __init__(self, arr, width, base=0):
        self.arr, self.width, self.base = arr, width, base


HEADS_PER_STEP = 4


def _lin_spec(w, seg_of):
    assert w.base % HEADS_PER_STEP == 0
    first, width = w.base // HEADS_PER_STEP, HEADS_PER_STEP * w.width
    if w.arr.shape[0] == 1:
        return pl.BlockSpec((1, width), lambda p, s: (0, first + p))
    return pl.BlockSpec((SEG, width), lambda p, s: (seg_of(s), first + p))


def _head_cols(ref, j, width):
    return ref[:, j * width:(j + 1) * width]


def _decay_rows(la_ref, j):
    return _head_cols(la_ref, j, LIN_KEY_DIM)


_STATE_BLOCK = (HEADS_PER_STEP, None, LIN_VAL_DIM, LIN_KEY_DIM)


def lin_fwd(name, q, k, v, la, side=None):
    T = q.arr.shape[0]
    nseg = T // SEG

    def body(q_ref, k_ref, v_ref, la_ref, o_ref, st_ref, st):
        @pl.when(pl.program_id(1) == 0)
        def _():
            st[...] = jnp.zeros_like(st)

        for j in range(HEADS_PER_STEP):
            st0 = st[j]
            st_ref[j] = st0
            o, st1 = _lin_segment(_head_cols(q_ref, j, LIN_KEY_DIM), _head_cols(k_ref, j, LIN_KEY_DIM),
                                  _head_cols(v_ref, j, LIN_VAL_DIM), _decay_rows(la_ref, j), st0)
            o_ref[:, j * LIN_VAL_DIM:(j + 1) * LIN_VAL_DIM] = o
            st[j] = st1

    fwd = lambda s: s
    out, states = _side_call(
        side, body,
        name=name,
        grid=(LIN_HEADS // HEADS_PER_STEP, nseg),
        in_specs=[_lin_spec(w, fwd) for w in (q, k, v, la)],
        out_specs=[pl.BlockSpec((SEG, HEADS_PER_STEP * LIN_VAL_DIM), lambda p, s: (s, p)),
                   pl.BlockSpec(_STATE_BLOCK, lambda p, s: (p, s, 0, 0))],
        out_shape=[jax.ShapeDtypeStruct((T, LIN_HEADS * LIN_VAL_DIM), F32),
                   jax.ShapeDtypeStruct((LIN_HEADS, nseg, LIN_VAL_DIM, LIN_KEY_DIM), F32)],
        scratch_shapes=[pltpu.VMEM((HEADS_PER_STEP, LIN_VAL_DIM, LIN_KEY_DIM), F32)],
        args=[q.arr, k.arr, v.arr, la.arr],
    )
    return out, states


def lin_bwd(name, q, k, v, la, states, dout, with_decay_grad, side=None, into=None, windows=None):
    T = q.arr.shape[0]
    nseg = T // SEG
    n_out = 4 if with_decay_grad else 3

    widths = [LIN_KEY_DIM, LIN_KEY_DIM, LIN_VAL_DIM] + ([LIN_KEY_DIM] if with_decay_grad else [])
    windows = dict(windows or {})
    fresh = [i for i in range(n_out) if i not in windows]
    placed = sorted(windows)
    n_pairs = LIN_HEADS // HEADS_PER_STEP
    rev = lambda s: nseg - 1 - s

    def body(q_ref, k_ref, v_ref, la_ref, st_ref, do_ref, *rest):
        rest = rest[1:] if placed else rest
        outs = dict(zip(fresh, rest[:len(fresh)]))
        rest = rest[len(fresh):]
        into_ref, rest = (rest[0], rest[1:]) if placed else (None, rest)
        dst, stages = rest[0], dict(zip(placed, rest[1:1 + len(placed)]))
        p, s = pl.program_id(0), pl.program_id(1)

        @pl.when(s == 0)
        def _():
            dst[...] = jnp.zeros_like(dst)

        def window_copy(n, i):
            width = HEADS_PER_STEP * widths[i]
            rows = pl.ds(pl.multiple_of(rev(s) * SEG, SEG), SEG)
            cols = pl.ds(pl.multiple_of(windows[i] + p * width, width), width)
            return pltpu.make_async_copy(stages[i], into_ref.at[rows, cols], rest[1 + len(placed)].at[n])

        for n, i in enumerate(placed):
            @pl.when((p > 0) | (s > 0))
            def _(n=n, i=i):
                window_copy(n, i).wait()

        for j in range(HEADS_PER_STEP):
            _, pull = jax.vjp(_lin_segment, _head_cols(q_ref, j, LIN_KEY_DIM).astype(F32),
                              _head_cols(k_ref, j, LIN_KEY_DIM).astype(F32), _head_cols(v_ref, j, LIN_VAL_DIM).astype(F32),
                              _decay_rows(la_ref, j), st_ref[j])
            grads = pull((_head_cols(do_ref, j, LIN_VAL_DIM), dst[j]))
            for i in range(n_out):
                dest = stages[i] if i in windows else outs[i]
                dest[:, j * widths[i]:(j + 1) * widths[i]] = grads[i].astype(dest.dtype)
            dst[j] = grads[4]

        for n, i in enumerate(placed):
            window_copy(n, i).start()

            @pl.when((p == n_pairs - 1) & (s == nseg - 1))
            def _(n=n, i=i):
                window_copy(n, i).wait()

    key_spec = pl.BlockSpec((SEG, HEADS_PER_STEP * LIN_KEY_DIM), lambda p, s: (rev(s), p))
    val_spec = pl.BlockSpec((SEG, HEADS_PER_STEP * LIN_VAL_DIM), lambda p, s: (rev(s), p))
    all_specs = [key_spec, key_spec, val_spec] + ([key_spec] if with_decay_grad else [])
    any_spec = pl.BlockSpec(memory_space=pl.ANY)
    outs = _side_call(
        side, body,
        name=name,
        grid=(n_pairs, nseg),
        in_specs=[_lin_spec(w, rev) for w in (q, k, v, la)] + [
            pl.BlockSpec(_STATE_BLOCK, lambda p, s: (p, rev(s), 0, 0)), val_spec] + ([any_spec] if placed else []),
        out_specs=[all_specs[i] for i in fresh] + ([any_spec] if placed else []),
        out_shape=[jax.ShapeDtypeStruct((T, LIN_HEADS * widths[i]), F32) for i in fresh]
        + ([jax.ShapeDtypeStruct(into.shape, into.dtype)] if placed else []),
        scratch_shapes=[pltpu.VMEM((HEADS_PER_STEP, LIN_VAL_DIM, LIN_KEY_DIM), F32)]
        + [pltpu.VMEM((SEG, HEADS_PER_STEP * widths[i]), into.dtype) for i in placed]
        + ([pltpu.SemaphoreType.DMA((len(placed),))] if placed else []),
        args=[q.arr, k.arr, v.arr, la.arr, states, dout] + ([into] if placed else []),
        aliases={6: len(fresh)} if placed else None,
    )
    return outs


def _coords():
    return lax.axis_index("x"), lax.axis_index("y"), lax.axis_index("c")


def _peer(k):
    x, y, c = _coords()
    px = 1 - x if k & 4 else x
    py = 1 - y if k & 2 else y
    pc = 1 - c if k & 1 else c
    return (px, py, pc), 4 * px + 2 * py + pc


def _remote(src, dst, send_sem, recv_sem, device):
    return pltpu.make_async_remote_copy(src_ref=src, dst_ref=dst, send_sem=send_sem, recv_sem=recv_sem,
                                        device_id=device, device_id_type=pl.DeviceIdType.MESH)


class Side:
    def __init__(self, kind, arrays):
        self.kind, self.arrays, self.n = kind, list(arrays), len(arrays)
        self.outs_per = 1
        self.slots = {"gather": N_DEV - 1, "pair": 4, "chips": 3}[kind]

    def out_shape(self):
        if self.kind == "gather":
            return [jax.ShapeDtypeStruct((N_DEV,) + a.shape, a.dtype) for a in self.arrays]
        return [jax.ShapeDtypeStruct((4,) + a.shape[1:], a.dtype) for a in self.arrays]

    def sems(self):
        return [pltpu.SemaphoreType.DMA((self.n, self.slots)), pltpu.SemaphoreType.DMA((self.n, self.slots)),
                pltpu.SemaphoreType.DMA((self.n,))]

    def start(self, ins, outs, sems):
        send, recv, own = sems
        x, y, c = _coords()
        me = 4 * x + 2 * y + c
        for a in range(self.n):
            if self.kind == "gather":
                pltpu.make_async_copy(ins[a], outs[a].at[me], own.at[a]).start()
                for k in (1, 2, 4, 6):
                    peer, _ = _peer(k)
                    _remote(ins[a], outs[a].at[me], send.at[a, k - 1], recv.at[a, k - 1], peer).start()
            elif self.kind == "pair":
                for j in range(4):
                    _remote(ins[a].at[2 * j + 1 - c], outs[a].at[j], send.at[a, j], recv.at[a, j], (x, y, 1 - c)).start()
            else:
                chip = 2 * x + y
                pltpu.make_async_copy(ins[a].at[chip], outs[a].at[chip], own.at[a]).start()
                for d in (1, 2, 3):
                    tx, ty = (1 - x if d & 2 else x), (1 - y if d & 1 else y)
                    _remote(ins[a].at[2 * tx + ty], outs[a].at[chip], send.at[a, d - 1], recv.at[a, d - 1], (tx, ty, c)).start()

    def finish(self, ins, outs, sems):
        send, recv, own = sems
        x, y, c = _coords()
        me = 4 * x + 2 * y + c
        sibling = (x, y, 1 - c)
        for a in range(self.n):
            if self.kind == "gather":
                blk = outs[a].at[me]
                for k in (2, 4, 6):
                    _, pidx = _peer(k)
                    _remote(blk, blk, send.at[a, k - 1], recv.at[a, k - 1], sibling).wait_recv()
                    _remote(outs[a].at[pidx], outs[a].at[pidx], send.at[a, k], recv.at[a, k], sibling).start()
                pltpu.make_async_copy(ins[a], blk, own.at[a]).wait()
                for k in range(1, N_DEV):
                    _remote(blk, blk, send.at[a, k - 1], recv.at[a, k - 1], sibling).wait_send()
                for k in (1, 3, 5, 7):
                    _remote(blk, blk, send.at[a, k - 1], recv.at[a, k - 1], sibling).wait_recv()
            elif self.kind == "pair":
                for j in range(4):
                    _remote(ins[a].at[j], outs[a].at[j], send.at[a, j], recv.at[a, j], sibling).wait()
            else:
                slab = outs[a].at[0]
                pltpu.make_async_copy(slab, slab, own.at[a]).wait()
                for d in (1, 2, 3):
                    _remote(slab, slab, send.at[a, d - 1], recv.at[a, d - 1], sibling).wait()


def _side_call(side, body, *, name, grid, in_specs, out_specs, out_shape, scratch_shapes, args, semantics=None,
               aliases=None):
    in_specs, out_specs, out_shape = list(in_specs), list(out_specs), list(out_shape)
    scratch_shapes, args = list(scratch_shapes), list(args)
    n_in, n_out = len(in_specs), len(out_specs)
    kernel_body = body
    if side is not None:
        n_so = side.n * side.outs_per

        def kernel_body(*refs):
            ins, side_ins = refs[:n_in], refs[n_in:n_in + side.n]
            pos = n_in + side.n
            outs, side_outs = refs[pos:pos + n_out], refs[pos + n_out:pos + n_out + n_so]
            rest = refs[pos + n_out + n_so:]
            scratch, sems = rest[:-3], rest[-3:]
            ids = [pl.program_id(ax) for ax in range(len(grid))]
            first = functools.reduce(jnp.logical_and, [i == 0 for i in ids])
            last = functools.reduce(jnp.logical_and, [i == g - 1 for i, g in zip(ids, grid)])

            @pl.when(first)
            def _():
                side.start(side_ins, side_outs, sems)

            body(*ins, *outs, *scratch)

            @pl.when(last)
            def _():
                side.finish(side_ins, side_outs, sems)

        any_spec = pl.BlockSpec(memory_space=pl.ANY)
        in_specs += [any_spec] * side.n
        args += side.arrays
        out_specs += [any_spec] * n_so
        out_shape += side.out_shape()
        scratch_shapes += side.sems()
        semantics = ("arbitrary",) * len(grid)
    outs = pl.pallas_call(
        kernel_body,
        name=name,
        grid=grid,
        in_specs=in_specs,
        out_specs=out_specs,
        out_shape=out_shape,
        scratch_shapes=scratch_shapes,
        input_output_aliases=aliases or {},
        compiler_params=pltpu.CompilerParams(dimension_semantics=semantics or ("arbitrary",) * len(grid),
                                             vmem_limit_bytes=VMEM_BIG),
    )(*args)
    if side is not None:
        side.results = list(outs[n_out:])
    return list(outs[:n_out])


def carrier(name, side):
    _side_call(side, lambda: None, name=name, grid=(1,), in_specs=[], out_specs=[], out_shape=[], scratch_shapes=[], args=[])
    return side.results


def all_reduce_small(vec):
    def body(v_ref, o_ref, land, send_sems, recv_sems):
        x, y, c = _coords()
        me = 4 * x + 2 * y + c
        land[me] = v_ref[...]
        copies = []
        for k in range(1, N_DEV):
            peer, _ = _peer(k)
            cp = pltpu.make_async_remote_copy(
                src_ref=v_ref, dst_ref=land.at[me], send_sem=send_sems.at[k - 1], recv_sem=recv_sems.at[k - 1],
                device_id=peer, device_id_type=pl.DeviceIdType.MESH)
            cp.start()
            copies.append(cp)
        for cp in copies:
            cp.wait()
        total = land[0]
        for s in range(1, N_DEV):
            total = total + land[s]
        o_ref[...] = total

    return pl.pallas_call(
        body,
        name="all_reduce_small",
        in_specs=[pl.BlockSpec(memory_space=pltpu.VMEM)],
        out_specs=pl.BlockSpec(memory_space=pltpu.VMEM),
        out_shape=jax.ShapeDtypeStruct(vec.shape, F32),
        scratch_shapes=[pltpu.VMEM((N_DEV,) + vec.shape, F32), pltpu.SemaphoreType.DMA((N_DEV - 1,)),
                        pltpu.SemaphoreType.DMA((N_DEV - 1,))],
    )(vec)


def _pack(parts):
    flat = jnp.concatenate([p.reshape(-1).astype(F32) for p in parts])
    n = flat.shape[0]
    rows = -(-n // 1024) * 8
    return jnp.pad(flat, (0, rows * 128 - n)).reshape(rows, 128)


def _unpack(packed, shapes):
    flat, out, pos = packed.reshape(-1), [], 0
    for s in shapes:
        n = math.prod(s)
        out.append(flat[pos:pos + n].reshape(s))
        pos += n
    return out


def _row2(v):
    return v.reshape(1, -1)


def _rider(hooks, name, ctx=None):
    if name not in hooks:
        return None, lambda: None
    side, done = hooks[name](ctx)
    return side, lambda: done(side.results)


def _layer_forward(h, hb, W, consts, hooks):
    cos, sin, ret_decay = consts
    side, landed = _rider(hooks, "proj")
    proj = mm("proj", hb, W["in"], "nn", tm=4096, tn=512, tk=1024, side=side, out_dtype=PROJ_DTYPE)
    landed()
    bias_tab, first_tab = W["bias_tab"]
    side, landed = _rider(hooks, "attn_fwd")
    attn, attn_lse = attn_fwd(proj, bias_tab, first_tab, side=side)
    landed()
    rq2, rk2 = row_call("ret_pre", f_ret_pre, [Row(proj, 512, COL["rq"] // 512), Row(proj, 512, COL["rk"] // 512),
                                              Row(cos), Row(sin)], [], [(512, F32), (512, F32)], tm=512)
    side, landed = _rider(hooks, "ret_fwd")
    ret_raw, ret_st = lin_fwd("ret_fwd", Win(rq2, 128), Win(rk2, 128), Win(proj, 256, COL["rv"] // 256), Win(ret_decay, 128),
                              side=side)
    landed()
    gq2, gla_la = row_call("gla_pre", f_gla_pre, [Row(proj, 512, COL["gq"] // 512), Row(proj, GLR_BLOCK, GLR_OFF // GLR_BLOCK)],
                           [W["wlr"], W["blr"]], [(512, F32), (512, F32)], tm=512)
    side, landed = _rider(hooks, "gla_fwd")
    gla_raw, gla_st = lin_fwd("gla_fwd", Win(gq2, 128), Win(proj, 128, COL["gk"] // 128), Win(proj, 256, COL["gv"] // 256),
                              Win(gla_la, 128), side=side)
    landed()
    ret_b, gla_b = row_call("normgate", f_normgate, [Row(ret_raw), Row(proj, 1024, COL["rg"] // 1024), Row(gla_raw),
                                                    Row(proj, 1024, COL["gg"] // 1024)], [W["gnorm"]],
                            [(1024, MXU_DTYPE), (1024, MXU_DTYPE)], tm=512)
    branches = (attn, ret_b, gla_b)
    p = [mm("branch_proj", branches[n], W["branch"][n], "nn", tm=2048, tn=1024, tk=1024) for n in range(3)]
    merged, = row_call("merge", f_merge, [Row(p[0]), Row(p[1]), Row(p[2]), Row(proj, GATES_W, GATES_OFF // GATES_W)], [],
                       [(1024, MXU_DTYPE)], tm=512)
    post_norm = lambda y, x, g, b: f_deepnorm(x, y, g, b) * 2 + (y,)
    x1, x1b, mix = mm("out_proj_ln1", merged, W["out"], "nn", tm=512, tn=D_MODEL, tk=1024, extras=(h,),
                      row_params=(W["ln1_g"], W["ln1_b"]), epi=post_norm, n_out=3, out_dtypes=(F32, MXU_DTYPE, F32))
    side, landed = _rider(hooks, "mlp_up")
    u, act = mm("mlp_up", x1b, W["up"], "nn", tm=4096, tn=512, tk=1024, n_out=2, out_dtypes=(F32, MXU_DTYPE),
                epi=lambda r: (r, jnp.square(jnp.maximum(r, 0.0))), side=side)
    landed()
    x2, x2b, y = mm("mlp_down_ln2", act, W["down"], "nn", tm=1024, tn=D_MODEL, tk=1024, extras=(x1,),
                    row_params=(W["ln2_g"], W["ln2_b"]), epi=post_norm, n_out=3, out_dtypes=(F32, MXU_DTYPE, F32))
    saved = dict(h=h, hb=hb, proj=proj, attn_lse=attn_lse, rq2=rq2, rk2=rk2, ret_st=ret_st, gq2=gq2, gla_la=gla_la,
                 gla_st=gla_st, ret_raw=ret_raw, gla_raw=gla_raw, branches=branches, p=p, merged=merged, mix=mix, x1=x1, x1b=x1b,
                 u=u, act=act, y=y)
    return x2, x2b, saved


def _layer_backward(dx2, S, W, consts, hooks):
    cos, sin, ret_decay = consts
    h, proj, T = S["h"], S["proj"], S["h"].shape[0]
    add = lambda r, e: r + e
    dx1a, dy, dg2, db2 = row_call("ln2_bwd", vjp_fn(f_deepnorm, 2, 1), [Row(S["x1"]), Row(S["y"]), Row(dx2)],
                                  [W["ln2_g"], W["ln2_b"]], [(1024, F32), (1024, MXU_DTYPE)], [(1, 1024), (1, 1024)], tm=512)
    side, landed = _rider(hooks, "mlp_down_bwd")
    du = mm("mlp_down_bwd", dy, W["down"], "nt", tm=2048, tn=512, tk=1024, extras=(S["u"],),
            epi=lambda r, u: r * (2.0 * jnp.maximum(u, 0.0)), out_dtype=MXU_DTYPE, side=side)
    landed()
    g_down = mm("mlp_down_wgrad", S["act"], dy, "tn", tm=1024, tn=1024, tk=2048, out_dtype=WIRE_DTYPE)
    dx1 = mm("mlp_up_bwd", du, W["up"], "nt", tm=1024, tn=1024, tk=2048, extras=(dx1a,), epi=add)
    g_up = mm("mlp_up_wgrad", S["x1b"], du, "tn", tm=1024, tn=D_FF // N_DEV, tk=2048, out_dtype=WIRE_DTYPE,
              column_slabs=True)
    dha, dmix, dg1, db1 = row_call("ln1_bwd", vjp_fn(f_deepnorm, 2, 1), [Row(h), Row(S["mix"]), Row(dx1)],
                                   [W["ln1_g"], W["ln1_b"]], [(1024, F32), (1024, MXU_DTYPE)], [(1, 1024), (1, 1024)], tm=512)
    dmerged = mm("out_proj_bwd", dmix, W["out"], "nt", tm=1024, tn=1024, tk=1024)
    g_out = mm("out_proj_wgrad", S["merged"], dmix, "tn", tm=1024, tn=1024, tk=2048, out_dtype=WIRE_DTYPE)
    dproj = jax.ShapeDtypeStruct((T, PROJ_PAD), MXU_DTYPE)
    p = S["p"]
    dp0, dp1, dp2, dproj = row_call(
        "merge_bwd", vjp_fn(f_merge, 4, 1), [Row(p[0]), Row(p[1]), Row(p[2]), Row(proj, GATES_W, GATES_OFF // GATES_W), Row(dmerged)],
        [], [(1024, MXU_DTYPE)] * 3, tm=512, into=dproj, into_cols=[(GATES_W, GATES_OFF)])
    dps = (dp0, dp1, dp2)
    dbr = [mm("branch_proj_bwd", dps[n], W["branch"][n], "nt", tm=2048, tn=1024, tk=1024,
              out_dtype=MXU_DTYPE if n == 0 else F32) for n in range(3)]
    g_branch = [mm("branch_proj_wgrad", S["branches"][n], dps[n], "tn", tm=1024, tn=1024, tk=2048, out_dtype=WIRE_DTYPE)
                for n in range(3)]
    dret_raw, dgla_raw, dgnorm, dproj = row_call(
        "normgate_bwd", vjp_fn(lambda a, b, c, d, g: _reorder_normgate(a, b, c, d, g), 4, 2),
        [Row(S["ret_raw"]), Row(S["gla_raw"]), Row(proj, 1024, COL["rg"] // 1024), Row(proj, 1024, COL["gg"] // 1024),
         Row(dbr[1]), Row(dbr[2])], [W["gnorm"]], [(1024, F32), (1024, F32)], [(1, LIN_VAL_DIM)], tm=512,
        into=dproj, into_cols=[(1024, COL["rg"]), (1024, COL["gg"])])
    side, landed = _rider(hooks, "gla_bwd")
    dgq2, dgla_la, dproj = lin_bwd("gla_bwd", Win(S["gq2"], 128), Win(proj, 128, COL["gk"] // 128),
                                   Win(proj, 256, COL["gv"] // 256), Win(S["gla_la"], 128), S["gla_st"], dgla_raw, True,
                                   side=side, into=dproj, windows={1: COL["gk"], 2: COL["gv"]})
    landed()
    dwlr, dblr, dproj = row_call(
        "gla_pre_bwd", _gla_pre_bwd, [Row(proj, 512, COL["gq"] // 512), Row(proj, GLR_BLOCK, GLR_OFF // GLR_BLOCK),
                                      Row(dgq2), Row(dgla_la)], [W["wlr"], W["blr"]], [],
        [(GLR_BLOCK, 512), (1, 512)], tm=512, into=dproj,
        into_cols=[(512, COL["gq"]), (PROJ_PAD - GLR_OFF, GLR_OFF)])
    side, landed = _rider(hooks, "ret_bwd", dict(branch=g_branch, out=g_out, up=g_up, down=g_down))
    drq2, drk2, dproj = lin_bwd("ret_bwd", Win(S["rq2"], 128), Win(S["rk2"], 128), Win(proj, 256, COL["rv"] // 256),
                                Win(ret_decay, 128), S["ret_st"], dret_raw, False, side=side, into=dproj,
                                windows={2: COL["rv"]})
    landed()
    dproj, = row_call(
        "ret_pre_bwd", _ret_pre_bwd, [Row(proj, 512, COL["rq"] // 512), Row(proj, 512, COL["rk"] // 512), Row(cos), Row(sin),
                                      Row(drq2), Row(drk2)], [], [], tm=512, into=dproj,
        into_cols=[(512, COL["rq"]), (512, COL["rk"])])
    side, landed = _rider(hooks, "attn_bwd")
    dproj, dbias = attn_bwd(proj, *W["bias_tab"], S["branches"][0], S["attn_lse"], dbr[0], dproj, side=side)
    landed()
    d_rel_bias = bias_table_grad(dbias)
    side, landed = _rider(hooks, "proj_wgrad", dict(branch=g_branch, out=g_out, up=g_up, down=g_down))
    g_in = mm("proj_wgrad", S["hb"], dproj, "tn", tm=1024, tn=1280, tk=2048, out_dtype=WIRE_DTYPE, side=side)
    landed()
    side, landed = _rider(hooks, "proj_bwd", {"in": g_in})
    dh = mm("proj_bwd", dproj, W["in"], "nt", tm=1024, tn=1024, tk=2560, extras=(dha,), epi=add, side=side)
    landed()
    big = dict(g_in=g_in, branch=g_branch, out=g_out, up=g_up, down=g_down)
    small = dict(rel_bias=d_rel_bias, wlr=dwlr[:GLA_RANK], blr=dblr, gnorm=dgnorm, ln1_g=dg1, ln1_b=db1, ln2_g=dg2, ln2_b=db2)
    return dh, big, small


def _reorder_normgate(ret_raw, gla_raw, rg, gg, gnorm):
    return f_normgate(ret_raw, rg, gla_raw, gg, gnorm)


def _gla_pre_bwd(gq, glr, dq2, dla, wlr, blr):
    _, pull = jax.vjp(f_gla_pre, gq, glr, wlr, blr)
    dgq, dglr, dwlr, dblr = pull((dq2, dla))
    dglr = jnp.concatenate([dglr, jnp.zeros((dglr.shape[0], PROJ_PAD - GLR_OFF - GLR_BLOCK), F32)], axis=1)
    return dgq, dglr, dwlr, dblr


def _ret_pre_bwd(rq, rk, cos, sin, drq2, drk2):
    _, pull = jax.vjp(lambda a, b: f_ret_pre(a, b, cos, sin), rq, rk)
    return pull((drq2, drk2))


SHARDED = ("in", "branch", "out", "up", "down")

_FULL = {
    "branch": lambda g: [g[:, n].reshape(D_MODEL, D_MODEL) for n in range(3)],
    "out": lambda g: g.reshape(D_MODEL, D_MODEL),
    "up": lambda g: jnp.transpose(g, (1, 0, 2)).reshape(D_MODEL, D_FF),
    "down": lambda g: g.reshape(D_FF, D_MODEL),
}


def _owner_pieces(s):
    shard = PROJ_WIDTH // N_DEV
    lo, hi = s * shard, (s + 1) * shard
    pieces = []
    for a, b, padded in ((0, GLR_SRC, 0), (GLR_SRC, GATES_SRC, GLR_OFF), (GATES_SRC, PROJ_WIDTH, GATES_OFF)):
        x0, x1 = max(lo, a), min(hi, b)
        if x0 < x1:
            pieces.append((x0 - lo, x1 - x0, padded + x0 - a))
    return pieces


def _full_in(g):
    placed = sorted((p, s, c, w) for s in range(N_DEV) for c, w, p in _owner_pieces(s))
    cols = [g[s][:, c:c + w] for _, s, c, w in placed]
    return jnp.concatenate(cols + [jnp.zeros((D_MODEL, PROJ_PAD - PROJ_WIDTH), g.dtype)], axis=1)


def _full_wlr(g):
    wlr = jnp.transpose(g, (1, 0, 2)).reshape(GLA_RANK, LIN_HEADS * LIN_KEY_DIM)
    return jnp.pad(wlr, ((0, GLR_BLOCK - GLA_RANK), (0, 0)))


def _slab_in(g):
    return jnp.stack([jnp.concatenate([g[:, p:p + w] for _, w, p in _owner_pieces(s)], axis=1) for s in range(N_DEV)])


_SLABS = {
    "in": _slab_in,
    "branch": lambda gs: jnp.stack([b.reshape(N_DEV, D_MODEL // N_DEV, D_MODEL) for b in gs], axis=1),
    "out": lambda g: g.reshape(N_DEV, D_MODEL // N_DEV, D_MODEL),
    "up": lambda g: g,
    "down": lambda g: g.reshape(N_DEV, D_FF // N_DEV, D_MODEL),
}


def _pair_sums(name, slabs, from_sibling=None):
    sums = []
    for mine, sibs in zip(slabs, from_sibling or carrier(name, Side("pair", slabs))):
        width = mine.shape[-1]
        rows = math.prod(mine.shape[1:-1])
        tm = 256 if rows % 256 == 0 else 128
        nb = rows // tm
        my_rows = Row(mine.reshape(-1, width), rowmap=lambda i, nb=nb: (2 * (i // nb) + lax.axis_index("c")) * nb + i % nb)
        out, = row_call("pair_sum", lambda q, p: (p.astype(F32) + q.astype(F32),),
                        [Row(sibs.reshape(-1, width)), my_rows], [], [(width, WIRE_DTYPE)], tm=tm)
        sums.append(out.reshape(sibs.shape))
    return sums


def _sum_adam(name, parts, w, m, v, layer, tm, earlier):
    C = w.shape[-1]
    rows = [Row(parts, lead=(s,)) for s in range(parts.shape[0])] + [Row(a, lead=(layer,)) for a in (w, m, v)]
    return row_call(name, f_sum_adam, rows, [], [(C, F32)] * 4, tm=tm, stack=(layer, DEPTH), stacked_onto=earlier)


def kernel(x, ln_in_g, ln_in_b, w_in, rel_bias, gla_w_lr, gla_b_lr, gla_norm_g, w_branch, w_out, ln1_g, ln1_b, w_up, w_down, ln2_g, ln2_b, loss_target, m_ln_in_g, m_ln_in_b, m_w_in, m_rel_bias, m_gla_w_lr, m_gla_b_lr, m_gla_norm_g, m_w_branch, m_w_out, m_ln1_g, m_ln1_b, m_w_up, m_w_down, m_ln2_g, m_ln2_b, v_ln_in_g, v_ln_in_b, v_w_in, v_rel_bias, v_gla_w_lr, v_gla_b_lr, v_gla_norm_g, v_w_branch, v_w_out, v_ln1_g, v_ln1_b, v_w_up, v_w_down, v_ln2_g, v_ln2_b):
    T = x.shape[1]
    x2d = x.reshape(T, D_MODEL)
    tgt = loss_target.reshape(T, D_MODEL)
    me = 4 * lax.axis_index("x") + 2 * lax.axis_index("y") + lax.axis_index("c")

    inv = ROPE_BASE ** (-jnp.arange(0, LIN_KEY_DIM, 2, dtype=F32) / LIN_KEY_DIM)
    ang = jnp.arange(T, dtype=F32)[:, None] * inv[None, :]
    cos = jnp.concatenate([jnp.cos(ang), jnp.cos(ang)], axis=1)
    sin = jnp.concatenate([-jnp.sin(ang), jnp.sin(ang)], axis=1)
    log_gamma = jnp.log1p(-jnp.exp2(-5.0 - jnp.arange(LIN_HEADS, dtype=F32)))
    ret_decay = jnp.repeat(log_gamma, LIN_KEY_DIM).reshape(1, LIN_HEADS * LIN_KEY_DIM)
    consts = (cos, sin, ret_decay)

    layers = [dict(blr=_row2(gla_b_lr[l]), gnorm=_row2(gla_norm_g[l]), ln1_g=_row2(ln1_g[l]),
                   ln1_b=_row2(ln1_b[l]), ln2_g=_row2(ln2_g[l]), ln2_b=_row2(ln2_b[l])) for l in range(DEPTH)]
    shards = dict(branch=w_branch, out=w_out, up=w_up, down=w_down)
    shards["in"] = w_in
    wire = [{n: shards[n][l].astype(WIRE_DTYPE) for n in SHARDED} for l in range(DEPTH)]

    def gather_of(l, names):
        def landed(res):
            for n, g in zip(names, res):
                layers[l][n] = _full_in(g) if n == "in" else _FULL[n](g)
        return lambda ctx: (Side("gather", [wire[l][n] for n in names]), landed)

    first = Side("gather", [wire[0]["in"]] + [gla_w_lr[l] for l in range(DEPTH)])
    bias_tabs = bias_table(rel_bias, side=first)
    layers[0]["in"] = _full_in(first.results[0])
    for l in range(DEPTH):
        layers[l]["wlr"] = _full_wlr(first.results[1 + l])
        layers[l]["bias_tab"] = (bias_tabs, l * ATTN_HEADS)

    fwd_hooks = [{"proj": gather_of(1, ("in",)), "attn_fwd": gather_of(0, ("up", "down")),
                  "ret_fwd": gather_of(0, ("branch", "out"))},
                 {"proj": gather_of(1, ("up", "down")), "attn_fwd": gather_of(1, ("branch", "out"))}]
    h, hb = row_call("ln_in", _twice(f_ln_in), [Row(x2d)], [_row2(ln_in_g), _row2(ln_in_b)], STREAM, tm=512)
    saved = []
    for l in range(DEPTH):
        h, hb, s = _layer_forward(h, hb, layers[l], consts, fwd_hooks[l])
        saved.append(s)
    dh, loss_lanes = row_call("loss", f_loss, [Row(h), Row(tgt)], [], [(D_MODEL, F32)], [(1, 128)], tm=512)

    received = [dict() for _ in range(DEPTH)]

    def scatter_of(l, names, sums=None):
        def make(ctx):
            parts = [sums[n] for n in names] if sums is not None else _pair_sums(
                "pair_grads", [_SLABS[n](ctx[n]) for n in names])

            def landed(res):
                received[l].update(zip(names, res))
            return Side("chips", parts), landed
        return make

    small = [None] * DEPTH
    dh, big, small[1] = _layer_backward(dh, saved[1], layers[1], consts, {})
    slabs1 = [_SLABS[n](big["g_in"] if n == "in" else big[n]) for n in SHARDED]
    sums1 = {}

    def pair_exchange_1(ctx):
        def landed(res):
            sums1.update(zip(SHARDED, _pair_sums("pair_grads", slabs1, from_sibling=res)))
        return Side("pair", slabs1), landed

    rest = SHARDED[1:]
    sums0 = {}

    def pair_exchange_0(ctx):
        slabs0 = [_SLABS[n](ctx[n]) for n in rest]

        def landed(res):
            sums0.update(zip(rest, _pair_sums("pair_grads", slabs0, from_sibling=res)))
        return Side("pair", slabs0), landed

    dh, big, small[0] = _layer_backward(dh, saved[0], layers[0], consts, {
        "mlp_down_bwd": pair_exchange_1, "gla_bwd": scatter_of(1, ("in",), sums1), "ret_bwd": pair_exchange_0,
        "attn_bwd": scatter_of(1, rest, sums1), "proj_wgrad": scatter_of(0, rest, sums0), "proj_bwd": scatter_of(0, ("in",))})
    grad_x, d_in_g, d_in_b = row_call("ln_in_bwd", vjp_fn(f_ln_in, 1, 1), [Row(x2d), Row(dh)],
                                      [_row2(ln_in_g), _row2(ln_in_b)], [(D_MODEL, F32)], [(1, D_MODEL), (1, D_MODEL)], tm=256)

    small_names = ("rel_bias", "wlr", "blr", "gnorm", "ln1_g", "ln1_b", "ln2_g", "ln2_b")
    parts = [d_in_g, d_in_b] + [small[l][n] for l in range(DEPTH) for n in small_names] + [loss_lanes]
    shapes = [(D_MODEL,), (D_MODEL,)] + [s for l in range(DEPTH) for s in (
        (ATTN_HEADS, N_REL), (GLA_RANK, 512), (512,), (LIN_VAL_DIM,), (D_MODEL,), (D_MODEL,), (D_MODEL,), (D_MODEL,))] + [(128,)]
    red = _unpack(all_reduce_small(_pack(parts)), shapes)
    loss = jnp.sum(red[-1])
    g_small = {"ln_in_g": red[0], "ln_in_b": red[1]}
    for l in range(DEPTH):
        for j, n in enumerate(small_names):
            g_small[(l, n)] = red[2 + l * len(small_names) + j]
    shard = LIN_HEADS * LIN_KEY_DIM // N_DEV
    g_wlr = jnp.stack([lax.dynamic_slice_in_dim(g_small[(l, "wlr")], me * shard, shard, axis=1) for l in range(DEPTH)])
    stack = lambda n: jnp.stack([g_small[(l, n)] for l in range(DEPTH)])
    g_rep = [g_small["ln_in_g"], g_small["ln_in_b"], stack("rel_bias"), g_wlr, stack("blr"), stack("gnorm"),
             stack("ln1_g"), stack("ln1_b"), stack("ln2_g"), stack("ln2_b")]
    w_rep = [ln_in_g, ln_in_b, rel_bias, gla_w_lr, gla_b_lr, gla_norm_g, ln1_g, ln1_b, ln2_g, ln2_b]
    m_rep = [m_ln_in_g, m_ln_in_b, m_rel_bias, m_gla_w_lr, m_gla_b_lr, m_gla_norm_g, m_ln1_g, m_ln1_b, m_ln2_g, m_ln2_b]
    v_rep = [v_ln_in_g, v_ln_in_b, v_rel_bias, v_gla_w_lr, v_gla_b_lr, v_gla_norm_g, v_ln1_g, v_ln1_b, v_ln2_g, v_ln2_b]
    rep_shapes = [w.shape for w in w_rep]
    d_pack, m_pack, v_pack = row_call("adam_small", f_adam, [Row(_pack(g_rep)), Row(_pack(w_rep)), Row(_pack(m_rep)), Row(_pack(v_rep))],
                                      [], [(128, F32)] * 3, tm=256)
    d_rep, nm_rep, nv_rep = _unpack(d_pack, rep_shapes), _unpack(m_pack, rep_shapes), _unpack(v_pack, rep_shapes)

    def owned(key, w, m, v, name, tm):
        C = w.shape[-1]
        flat = lambda a: a.reshape(DEPTH, -1, C)
        outs = None
        for l in range(DEPTH):
            outs = _sum_adam(name, received[l][key].reshape(4, -1, C), flat(w), flat(m), flat(v), l, tm, outs)
        return [o.reshape(w.shape) for o in outs]

    o_in = owned("in", w_in, m_w_in, v_w_in, "adam_w_in", 256)
    o_br = owned("branch", w_branch, m_w_branch, v_w_branch, "adam_w_branch", 128)
    o_out = owned("out", w_out, m_w_out, v_w_out, "adam_w_out", 128)
    o_up = owned("up", w_up, m_w_up, v_w_up, "adam_w_up", 256)
    o_down = owned("down", w_down, m_w_down, v_w_down, "adam_w_down", 256)

    def tree(j, rep):
        return [rep[0], rep[1], o_in[j], rep[2], rep[3], rep[4], rep[5], o_br[j], o_out[j], rep[6], rep[7], o_up[j], o_down[j],
                rep[8], rep[9]]

    return (loss, grad_x.reshape(1, T, D_MODEL), *tree(0, g_rep), *tree(1, d_rep), *tree(2, nm_rep), *tree(3, nv_rep))
```
